```python
import math
import jax, jax.numpy as jnp
from jax import lax
import numpy as np

D_MODEL = 2048
BATCH = 4
SEQ = 2048
DEPTH = 1

HEAD_DIM = 64
ATT_WIDTH = D_MODEL // 2
ATT_HEADS = ATT_WIDTH // HEAD_DIM
ATT_KV_HEADS = ATT_HEADS // 4
KV_WIDTH = ATT_KV_HEADS * HEAD_DIM
WINDOW = 128
ATT_BLOCK = 128
ROPE_DIM = HEAD_DIM // 4
ROPE_THETA = 500000.0
SG_WIDTH = D_MODEL // 2
SG_GROUP_DIM = 128
SG_GROUPS = SG_WIDTH // SG_GROUP_DIM
SG_CHUNK = 128
N_BRANCHES = 2
OFF_Q = 0
OFF_K = OFF_Q + ATT_WIDTH
OFF_V = OFF_K + KV_WIDTH
OFF_U = OFF_V + KV_WIDTH
OFF_S = OFF_U + SG_WIDTH
OFF_G = OFF_S + SG_WIDTH
IN_COLS = OFF_G + N_BRANCHES * D_MODEL
N_GROUPS = 8
EXPERTS_PER_GROUP = 8
N_EXPERTS = N_GROUPS * EXPERTS_PER_GROUP
TOP_K = 2
EXPERT_FF = D_MODEL // 4
MOE_BLOCK = 128
EPS = 1e-6
NEG_INF = -1e30

kernel_name = "hybrid_gated_window_gqa_spatial_gating_hmoe"


def rms_norm(x, g):
    xf = x.astype(jnp.float32)
    r = lax.rsqrt(jnp.mean(xf * xf, axis=-1, keepdims=True) + EPS)
    return (xf * r).astype(x.dtype) * g


def layer_norm(x, g, b):
    xf = x.astype(jnp.float32)
    mu = jnp.mean(xf, axis=-1, keepdims=True)
    xc = xf - mu
    var = jnp.mean(xc * xc, axis=-1, keepdims=True)
    return (xc * lax.rsqrt(var + EPS)).astype(x.dtype) * g + b


def rope_tables(positions, dtype):
    inv = ROPE_THETA ** (-jnp.arange(0, ROPE_DIM, 2, dtype=jnp.float32) / ROPE_DIM)
    ang = positions.astype(jnp.float32)[..., None] * inv
    return jnp.cos(ang)[:, :, None, :].astype(dtype), jnp.sin(ang)[:, :, None, :].astype(dtype)


def partial_rope(t, cos, sin):
    half = ROPE_DIM // 2
    r1, r2, rest = t[..., :half], t[..., half:ROPE_DIM], t[..., ROPE_DIM:]
    return jnp.concatenate([r1 * cos - r2 * sin, r2 * cos + r1 * sin, rest], axis=-1)


def window_attention(q, k, v, sink):
    B, S = q.shape[0], q.shape[1]
    nb = S // ATT_BLOCK
    G = ATT_HEADS // ATT_KV_HEADS
    qb = q.reshape(B, nb, ATT_BLOCK, ATT_KV_HEADS, G, HEAD_DIM)

    def neighbours(t):
        tp = jnp.pad(t, ((0, 0), (ATT_BLOCK, ATT_BLOCK), (0, 0), (0, 0)))
        tp = tp.reshape(B, nb + 2, ATT_BLOCK, ATT_KV_HEADS, HEAD_DIM)
        return jnp.concatenate([tp[:, :-2], tp[:, 1:-1], tp[:, 2:]], axis=2)

    kb, vb = neighbours(k), neighbours(v)
    scores = jnp.einsum('bnqhgd,bnkhd->bnhgqk', qb, kb).astype(jnp.float32) * (HEAD_DIM ** -0.5)
    qi = jnp.arange(ATT_BLOCK)[:, None]
    kj = jnp.arange(3 * ATT_BLOCK)[None, :]
    band = jnp.abs(kj - ATT_BLOCK - qi) <= WINDOW
    key_pos = jnp.arange(nb)[:, None] * ATT_BLOCK + kj - ATT_BLOCK
    in_seq = (key_pos >= 0) & (key_pos < S)
    mask = band[None, :, :] & in_seq[:, None, :]
    scores = jnp.where(mask[None, :, None, None], scores, NEG_INF)
    sink_col = jnp.broadcast_to(
        sink.astype(jnp.float32).reshape(1, 1, ATT_KV_HEADS, G, 1, 1),
        scores.shape[:-1] + (1,))
    probs = jax.nn.softmax(jnp.concatenate([scores, sink_col], axis=-1), axis=-1)[..., :-1]
    out = jnp.einsum('bnhgqk,bnkhd->bnqhgd', probs.astype(v.dtype), vb)
    return out.reshape(B, S, ATT_WIDTH)


def spatial_gating(u, s, ln_g, ln_b, w_s, b_s):
    B, S = u.shape[0], u.shape[1]
    sg = layer_norm(s.reshape(B, S, SG_GROUPS, SG_GROUP_DIM), ln_g, ln_b)
    sc = sg.reshape(B, S // SG_CHUNK, SG_CHUNK, SG_GROUPS, SG_GROUP_DIM)
    mixed = jnp.einsum('gts,bnsgc->bntgc', w_s, sc) + b_s.T[None, None, :, :, None]
    return u * mixed.reshape(B, S, SG_WIDTH)


def hierarchical_moe(x, w_group, b_group, w_erouter, b_erouter, w_gate, w_up, w_down):
    B, S, D = x.shape
    T = B * S
    xt = x.reshape(T, D)
    g_logits = (xt @ w_group).astype(jnp.float32) + b_group
    g_prob = jax.nn.softmax(g_logits, axis=-1)
    grp = jnp.argmax(g_logits, axis=-1)
    g_w = jnp.take_along_axis(g_prob, grp[:, None], axis=1)[:, 0]
    e_logits = ((xt @ w_erouter).astype(jnp.float32) + b_erouter).reshape(T, N_GROUPS, EXPERTS_PER_GROUP)
    e_logits = jnp.take_along_axis(e_logits, grp[:, None, None], axis=1)[:, 0]
    top_v, top_i = lax.top_k(e_logits, TOP_K)
    top_w = jax.nn.softmax(top_v, axis=-1) * g_w[:, None]
    expert = grp[:, None] * EXPERTS_PER_GROUP + top_i
    TK = T * TOP_K
    flat_e = expert.reshape(TK).astype(jnp.int32)
    flat_t = jnp.repeat(jnp.arange(T, dtype=jnp.int32), TOP_K)
    flat_w = top_w.reshape(TK)
    order = jnp.argsort(flat_e)
    se, st, sw = flat_e[order], flat_t[order], flat_w[order]
    counts = jnp.bincount(flat_e, length=N_EXPERTS)
    starts = jnp.cumsum(counts) - counts
    padded = (counts + MOE_BLOCK - 1) // MOE_BLOCK * MOE_BLOCK
    pad_ends = jnp.cumsum(padded)
    pad_starts = pad_ends - padded
    dest = pad_starts[se] + jnp.arange(TK) - starts[se]
    n_rows = TK + N_EXPERTS * MOE_BLOCK
    n_blocks = n_rows // MOE_BLOCK
    row_tok = jnp.full((n_rows,), T, jnp.int32).at[dest].set(st)
    row_w = jnp.zeros((n_rows,), jnp.float32).at[dest].set(sw)
    block_e = jnp.minimum(
        jnp.searchsorted(pad_ends, jnp.arange(n_blocks) * MOE_BLOCK, side='right'),
        N_EXPERTS - 1)
    xpad = jnp.concatenate([xt, jnp.zeros((1, D), xt.dtype)], axis=0)
    xrows = xpad[row_tok].reshape(n_blocks, MOE_BLOCK, D)

    def expert_block(args):
        xb, e = args
        hdn = jax.nn.silu(xb @ w_gate[e]) * (xb @ w_up[e])
        return hdn @ w_down[e]

    yrows = lax.map(expert_block, (xrows, block_e)).reshape(n_rows, D)
    yrows = yrows * row_w[:, None].astype(yrows.dtype)
    y = jax.ops.segment_sum(yrows, row_tok, num_segments=T + 1)[:T]
    return y.reshape(B, S, D)


def setup_inputs(seed: int = 0) -> dict:
    key = jax.random.key(seed)
    ks = jax.random.split(key, 24)
    f32 = jnp.float32

    def nrm(k, shape, scale):
        return jax.random.normal(k, shape, f32) * scale

    L = DEPTH
    x = jax.random.normal(ks[0], (BATCH, SEQ, D_MODEL), f32)
    offsets = jax.random.randint(ks[1], (BATCH, 1), 0, 4096, dtype=jnp.int32)
    positions = offsets + jnp.arange(SEQ, dtype=jnp.int32)[None, :]
    return {
        "x": x,
        "positions": positions,
        "norm1_g": 1.0 + nrm(ks[2], (L, D_MODEL), 0.02),
        "w_in": nrm(ks[3], (L, D_MODEL, IN_COLS), D_MODEL ** -0.5),
        "q_norm_g": 1.0 + nrm(ks[4], (L, HEAD_DIM), 0.02),
        "k_norm_g": 1.0 + nrm(ks[5], (L, HEAD_DIM), 0.02),
        "sink_logits": nrm(ks[6], (L, ATT_HEADS), 0.5),
        "sg_ln_g": 1.0 + nrm(ks[7], (L, SG_GROUPS, SG_GROUP_DIM), 0.02),
        "sg_ln_b": nrm(ks[8], (L, SG_GROUPS, SG_GROUP_DIM), 0.02),
        "sg_w": nrm(ks[9], (L, SG_GROUPS, SG_CHUNK, SG_CHUNK), SG_CHUNK ** -0.5),
        "sg_b": 1.0 + nrm(ks[10], (L, SG_GROUPS, SG_CHUNK), 0.02),
        "w_branch_att": nrm(ks[11], (L, ATT_WIDTH, D_MODEL), ATT_WIDTH ** -0.5),
        "w_branch_sg": nrm(ks[12], (L, SG_WIDTH, D_MODEL), SG_WIDTH ** -0.5),
        "w_out": nrm(ks[13], (L, D_MODEL, D_MODEL), D_MODEL ** -0.5),
        "norm2_g": 1.0 + nrm(ks[14], (L, D_MODEL), 0.02),
        "w_group_router": nrm(ks[15], (L, D_MODEL, N_GROUPS), D_MODEL ** -0.5),
        "b_group_router": nrm(ks[16], (L, N_GROUPS), 0.01),
        "w_expert_router": nrm(ks[17], (L, D_MODEL, N_EXPERTS), D_MODEL ** -0.5),
        "b_expert_router": nrm(ks[18], (L, N_EXPERTS), 0.01),
        "w_gate": nrm(ks[19], (L, N_EXPERTS, D_MODEL, EXPERT_FF), D_MODEL ** -0.5),
        "w_up": nrm(ks[20], (L, N_EXPERTS, D_MODEL, EXPERT_FF), D_MODEL ** -0.5),
        "w_down": nrm(ks[21], (L, N_EXPERTS, EXPERT_FF, D_MODEL), EXPERT_FF ** -0.5),
    }


def reference(x, positions, norm1_g, w_in, q_norm_g, k_norm_g, sink_logits, sg_ln_g, sg_ln_b,
              sg_w, sg_b, w_branch_att, w_branch_sg, w_out, norm2_g, w_group_router,
              b_group_router, w_expert_router, b_expert_router, w_gate, w_up, w_down):
    B, S, D = x.shape
    cos, sin = rope_tables(positions, x.dtype)
    h = x
    for l in range(DEPTH):
        xn = rms_norm(h, norm1_g[l])
        proj = xn @ w_in[l]
        q = proj[..., OFF_Q:OFF_K].reshape(B, S, ATT_HEADS, HEAD_DIM)
        k = proj[..., OFF_K:OFF_V].reshape(B, S, ATT_KV_HEADS, HEAD_DIM)
        v = proj[..., OFF_V:OFF_U].reshape(B, S, ATT_KV_HEADS, HEAD_DIM)
        u = jax.nn.gelu(proj[..., OFF_U:OFF_S])
        s = jax.nn.gelu(proj[..., OFF_S:OFF_G])
        gates = jax.nn.sigmoid(proj[..., OFF_G:]).reshape(B, S, N_BRANCHES, D)
        q = partial_rope(rms_norm(q, q_norm_g[l]), cos, sin)
        k = partial_rope(rms_norm(k, k_norm_g[l]), cos, sin)
        att = window_attention(q, k, v, sink_logits[l])
        sgo = spatial_gating(u, s, sg_ln_g[l], sg_ln_b[l], sg_w[l], sg_b[l])
        merged = (gates[:, :, 0, :] * (att @ w_branch_att[l])
                  + gates[:, :, 1, :] * (sgo @ w_branch_sg[l]))
        h = h + merged @ w_out[l]
        hn = rms_norm(h, norm2_g[l])
        h = h + hierarchical_moe(hn, w_group_router[l], b_group_router[l], w_expert_router[l],
                                 b_expert_router[l], w_gate[l], w_up[l], w_down[l])
    return h
```

```python
import functools

import jax
import jax.numpy as jnp
from jax import lax
from jax.experimental import pallas as pl
from jax.experimental.pallas import tpu as pltpu

F32 = jnp.float32
BF16 = jnp.bfloat16

D_MODEL = 2048
HEAD_DIM = 64
ATT_WIDTH = D_MODEL // 2
ATT_HEADS = ATT_WIDTH // HEAD_DIM
ATT_KV_HEADS = ATT_HEADS // 4
Q_PER_KV = ATT_HEADS // ATT_KV_HEADS
KV_WIDTH = ATT_KV_HEADS * HEAD_DIM
WINDOW = 128
ATT_BLOCK = 128
ROPE_DIM = HEAD_DIM // 4
ROPE_HALF = ROPE_DIM // 2
ROPE_THETA = 500000.0
SG_WIDTH = D_MODEL // 2
SG_GROUP_DIM = 128
SG_GROUPS = SG_WIDTH // SG_GROUP_DIM
SG_CHUNK = 128
OFF_Q = 0
OFF_K = OFF_Q + ATT_WIDTH
OFF_V = OFF_K + KV_WIDTH
OFF_U = OFF_V + KV_WIDTH
OFF_S = OFF_U + SG_WIDTH
OFF_G = OFF_S + SG_WIDTH
IN_COLS = OFF_G + 2 * D_MODEL
N_GROUPS = 8
EXPERTS_PER_GROUP = 8
N_EXPERTS = N_GROUPS * EXPERTS_PER_GROUP
TOP_K = 2
EXPERT_FF = D_MODEL // 4
MOE_BLOCK = 128
EPS = 1e-6
NEG_INF = -1e30

LANES = 128
VMEM_LIMIT = 56 * 1024 * 1024


def _params(sem, vmem=VMEM_LIMIT):
    return pltpu.CompilerParams(dimension_semantics=sem, vmem_limit_bytes=vmem)


def _rmsnorm_kernel(x_ref, g_ref, o_ref):
    x = x_ref[...]
    r = lax.rsqrt(jnp.mean(x * x, axis=-1, keepdims=True) + EPS)
    o_ref[...] = ((x * r) * g_ref[...]).astype(o_ref.dtype)


def _rmsnorm(x, g, tm=512):
    T, D = x.shape
    return pl.pallas_call(
        _rmsnorm_kernel,
        grid=(T // tm,),
        in_specs=[pl.BlockSpec((tm, D), lambda i: (i, 0)),
                  pl.BlockSpec((1, D), lambda i: (0, 0))],
        out_specs=pl.BlockSpec((tm, D), lambda i: (i, 0)),
        out_shape=jax.ShapeDtypeStruct((T, D), BF16),
        compiler_params=_params(("parallel",)),
        name="norm1",
    )(x, g.reshape(1, D))


def _proj_kernel(x_ref, w_ref, o_ref, wbf_ref):
    @pl.when(pl.program_id(1) == 0)
    def _():
        wbf_ref[...] = w_ref[...].astype(BF16)

    o_ref[...] = jnp.dot(x_ref[...], wbf_ref[...], preferred_element_type=F32)


def _in_proj(xn, w, tm=512, tn=1280):
    T, D = xn.shape
    N = w.shape[1]
    return pl.pallas_call(
        _proj_kernel,
        grid=(N // tn, T // tm),
        in_specs=[pl.BlockSpec((tm, D), lambda j, i: (i, 0)),
                  pl.BlockSpec((D, tn), lambda j, i: (0, j))],
        out_specs=pl.BlockSpec((tm, tn), lambda j, i: (i, j)),
        out_shape=jax.ShapeDtypeStruct((T, N), F32),
        scratch_shapes=[pltpu.VMEM((D, tn), BF16)],
        compiler_params=_params(("arbitrary", "arbitrary")),
        name="in_proj",
    )(xn, w)


def _qkv_prep_kernel(p_ref, pos_ref, invf_ref, sign_ref, gq_ref, gk_ref, q_ref, k_ref, v_ref):
    pos = pos_ref[...].astype(F32)
    ang = pos * invf_ref[...]
    cos = jnp.cos(ang)
    sin = jnp.sin(ang) * sign_ref[...]
    lane = lax.broadcasted_iota(jnp.int32, ang.shape, 1)
    low_head = lane < HEAD_DIM
    first_half = (lane % HEAD_DIM) < ROPE_HALF

    def norm_rope(x, g):
        x2 = x * x
        s_lo = jnp.sum(jnp.where(low_head, x2, 0.0), axis=-1, keepdims=True)
        s_hi = jnp.sum(jnp.where(low_head, 0.0, x2), axis=-1, keepdims=True)
        ssq = jnp.where(low_head, s_lo, s_hi)
        xn = (x * lax.rsqrt(ssq * (1.0 / HEAD_DIM) + EPS)) * g
        partner = jnp.where(first_half,
                            pltpu.roll(xn, LANES - ROPE_HALF, 1),
                            pltpu.roll(xn, ROPE_HALF, 1))
        return xn * cos + partner * sin

    for c in range(ATT_WIDTH // LANES):
        x = p_ref[:, OFF_Q + c * LANES:OFF_Q + (c + 1) * LANES]
        q_ref[:, c * LANES:(c + 1) * LANES] = (
            norm_rope(x, gq_ref[...]) * (HEAD_DIM ** -0.5)).astype(q_ref.dtype)
    for c in range(KV_WIDTH // LANES):
        x = p_ref[:, OFF_K + c * LANES:OFF_K + (c + 1) * LANES]
        k_ref[:, c * LANES:(c + 1) * LANES] = norm_rope(x, gk_ref[...]).astype(k_ref.dtype)
    v_ref[...] = p_ref[:, OFF_V:OFF_V + KV_WIDTH].astype(v_ref.dtype)


def _qkv_prep(proj, positions, q_g, k_g, tq=256):
    T = proj.shape[0]
    width = OFF_U
    lane = jnp.arange(LANES) % HEAD_DIM
    inv = ROPE_THETA ** (-jnp.arange(0, ROPE_DIM, 2, dtype=F32) / ROPE_DIM)
    invf = jnp.where(lane < ROPE_DIM, inv[lane % ROPE_HALF], 0.0).reshape(1, LANES)
    sign = jnp.where(lane < ROPE_HALF, -1.0, jnp.where(lane < ROPE_DIM, 1.0, 0.0))
    sign = sign.astype(F32).reshape(1, LANES)
    gq = jnp.tile(q_g, LANES // HEAD_DIM).reshape(1, LANES)
    gk = jnp.tile(k_g, LANES // HEAD_DIM).reshape(1, LANES)
    row = lambda i: (i, 0)
    const = lambda i: (0, 0)
    return pl.pallas_call(
        _qkv_prep_kernel,
        grid=(T // tq,),
        in_specs=[pl.BlockSpec((tq, width), row),
                  pl.BlockSpec((tq, 1), row),
                  pl.BlockSpec((1, LANES), const),
                  pl.BlockSpec((1, LANES), const),
                  pl.BlockSpec((1, LANES), const),
                  pl.BlockSpec((1, LANES), const)],
        out_specs=[pl.BlockSpec((tq, ATT_WIDTH), row),
                   pl.BlockSpec((tq, KV_WIDTH), row),
                   pl.BlockSpec((tq, KV_WIDTH), row)],
        out_shape=[jax.ShapeDtypeStruct((T, ATT_WIDTH), BF16),
                   jax.ShapeDtypeStruct((T, KV_WIDTH), BF16),
                   jax.ShapeDtypeStruct((T, KV_WIDTH), BF16)],
        compiler_params=_params(("parallel",)),
        name="qkv_prep",
    )(proj, positions.reshape(T, 1), invf, sign, gq, gk)


def _attn_kernel(sink_ref, q_ref, kp_ref, kc_ref, kn_ref, vp_ref, vc_ref, vn_ref, o_ref, *, nb):
    n = pl.program_id(1)
    rows = Q_PER_KV * ATT_BLOCK
    keys = 3 * ATT_BLOCK
    qi = lax.broadcasted_iota(jnp.int32, (rows, keys), 0) % ATT_BLOCK
    kj = lax.broadcasted_iota(jnp.int32, (rows, keys), 1)
    key_pos = kj + (n - 1) * ATT_BLOCK
    in_seq = jnp.logical_and(key_pos >= 0, key_pos < nb * ATT_BLOCK)
    mask = jnp.logical_and(jnp.abs(kj - ATT_BLOCK - qi) <= WINDOW, in_seq)
    row_head = lax.broadcasted_iota(jnp.int32, (rows, 1), 0) // ATT_BLOCK

    for kvh in range(ATT_KV_HEADS):
        cols = slice(kvh * HEAD_DIM, (kvh + 1) * HEAD_DIM)
        k = jnp.concatenate([kp_ref[:, cols], kc_ref[:, cols], kn_ref[:, cols]], axis=0)
        v = jnp.concatenate([vp_ref[:, cols], vc_ref[:, cols], vn_ref[:, cols]], axis=0)
        q = jnp.concatenate(
            [q_ref[:, (kvh * Q_PER_KV + g) * HEAD_DIM:(kvh * Q_PER_KV + g + 1) * HEAD_DIM]
             for g in range(Q_PER_KV)], axis=0)
        sink = jnp.zeros((rows, 1), F32)
        for g in range(Q_PER_KV):
            sink = jnp.where(row_head == g, sink_ref[kvh * Q_PER_KV + g], sink)
        s = lax.dot_general(q, k, (((1,), (1,)), ((), ())), preferred_element_type=F32)
        s = jnp.where(mask, s, NEG_INF)
        m = jnp.maximum(jnp.max(s, axis=-1, keepdims=True), sink)
        e = jnp.exp(s - m)
        denom = jnp.sum(e, axis=-1, keepdims=True) + jnp.exp(sink - m)
        p = (e / denom).astype(BF16)
        o = jnp.dot(p, v, preferred_element_type=F32)
        for g in range(Q_PER_KV):
            h = kvh * Q_PER_KV + g
            o_ref[:, h * HEAD_DIM:(h + 1) * HEAD_DIM] = (
                o[g * ATT_BLOCK:(g + 1) * ATT_BLOCK].astype(o_ref.dtype))


def _attention(q, k, v, sink, batch):
    T = q.shape[0]
    nb = T // batch // ATT_BLOCK
    cur = lambda b, n: (b * nb + n, 0)
    prev = lambda b, n: (b * nb + jnp.maximum(n - 1, 0), 0)
    nxt = lambda b, n: (b * nb + jnp.minimum(n + 1, nb - 1), 0)
    kv = lambda im: pl.BlockSpec((ATT_BLOCK, KV_WIDTH), im)
    return pl.pallas_call(
        functools.partial(_attn_kernel, nb=nb),
        grid=(batch, nb),
        in_specs=[pl.BlockSpec(memory_space=pltpu.SMEM),
                  pl.BlockSpec((ATT_BLOCK, ATT_WIDTH), cur),
                  kv(prev), kv(cur), kv(nxt), kv(prev), kv(cur), kv(nxt)],
        out_specs=pl.BlockSpec((ATT_BLOCK, ATT_WIDTH), cur),
        out_shape=jax.ShapeDtypeStruct((T, ATT_WIDTH), BF16),
        compiler_params=_params(("parallel", "parallel")),
        name="window_attn",
    )(sink, q, k, k, k, v, v, v)


def _sg_kernel(u_ref, s_ref, lng_ref, lnb_ref, w_ref, b_ref, o_ref):
    groups = w_ref.shape[0]
    for gi in range(groups):
        cols = slice(gi * SG_GROUP_DIM, (gi + 1) * SG_GROUP_DIM)
        s = jax.nn.gelu(s_ref[:, cols])
        mu = jnp.mean(s, axis=-1, keepdims=True)
        sc = s - mu
        var = jnp.mean(sc * sc, axis=-1, keepdims=True)
        sn = (sc * lax.rsqrt(var + EPS)) * lng_ref[0, gi:gi + 1, :] + lnb_ref[0, gi:gi + 1, :]
        mixed = jnp.dot(w_ref[gi], sn.astype(BF16), preferred_element_type=F32)
        mixed = mixed + b_ref[0, :, gi:gi + 1]
        o_ref[:, cols] = (jax.nn.gelu(u_ref[:, cols]) * mixed).astype(o_ref.dtype)


def _spatial_gating(proj, ln_g, ln_b, w_s, b_s, halves=2):
    T = proj.shape[0]
    half_w = SG_WIDTH // halves
    gph = SG_GROUPS // halves
    u0, s0 = OFF_U // half_w, OFF_S // half_w
    return pl.pallas_call(
        _sg_kernel,
        grid=(T // SG_CHUNK, halves),
        in_specs=[pl.BlockSpec((SG_CHUNK, half_w), lambda i, j: (i, u0 + j)),
                  pl.BlockSpec((SG_CHUNK, half_w), lambda i, j: (i, s0 + j)),
                  pl.BlockSpec((1, gph, SG_GROUP_DIM), lambda i, j: (j, 0, 0)),
                  pl.BlockSpec((1, gph, SG_GROUP_DIM), lambda i, j: (j, 0, 0)),
                  pl.BlockSpec((gph, SG_CHUNK, SG_CHUNK), lambda i, j: (j, 0, 0)),
                  pl.BlockSpec((1, SG_CHUNK, gph), lambda i, j: (j, 0, 0))],
        out_specs=pl.BlockSpec((SG_CHUNK, half_w), lambda i, j: (i, j)),
        out_shape=jax.ShapeDtypeStruct((T, SG_WIDTH), BF16),
        compiler_params=_params(("parallel", "parallel")),
        name="spatial_gating",
    )(proj, proj,
      ln_g.reshape(halves, gph, SG_GROUP_DIM), ln_b.reshape(halves, gph, SG_GROUP_DIM),
      w_s.astype(BF16),
      b_s.reshape(halves, gph, SG_CHUNK).transpose(0, 2, 1))


def _merge_kernel(att_ref, sgo_ref, ga_ref, gb_ref, wa_ref, wb_ref, o_ref):
    a = jnp.dot(att_ref[...], wa_ref[...], preferred_element_type=F32)
    b = jnp.dot(sgo_ref[...], wb_ref[...], preferred_element_type=F32)
    m = jax.nn.sigmoid(ga_ref[...]) * a + jax.nn.sigmoid(gb_ref[...]) * b
    o_ref[...] = m.astype(o_ref.dtype)


def _merge(att, sgo, proj, w_a, w_b, tm=512, tn=512):
    T = att.shape[0]
    ga0 = OFF_G // tn
    gb0 = (OFF_G + D_MODEL) // tn
    return pl.pallas_call(
        _merge_kernel,
        grid=(T // tm, D_MODEL // tn),
        in_specs=[pl.BlockSpec((tm, ATT_WIDTH), lambda i, j: (i, 0)),
                  pl.BlockSpec((tm, SG_WIDTH), lambda i, j: (i, 0)),
                  pl.BlockSpec((tm, tn), lambda i, j: (i, ga0 + j)),
                  pl.BlockSpec((tm, tn), lambda i, j: (i, gb0 + j)),
                  pl.BlockSpec((ATT_WIDTH, tn), lambda i, j: (0, j)),
                  pl.BlockSpec((SG_WIDTH, tn), lambda i, j: (0, j))],
        out_specs=pl.BlockSpec((tm, tn), lambda i, j: (i, j)),
        out_shape=jax.ShapeDtypeStruct((T, D_MODEL), BF16),
        compiler_params=_params(("parallel", "parallel")),
        name="merge",
    )(att, sgo, proj, proj, w_a.astype(BF16), w_b.astype(BF16))


def _route(logits):
    lane = lax.broadcasted_iota(jnp.int32, logits.shape, 1)
    lane_f = lane.astype(F32)
    is_g = lane < N_GROUPS
    gl = jnp.where(is_g, logits, -jnp.inf)
    gmax = jnp.max(gl, axis=-1, keepdims=True)
    grp = jnp.min(jnp.where(gl == gmax, lane_f, float(LANES)), axis=-1, keepdims=True)
    gsum = jnp.sum(jnp.where(is_g, jnp.exp(logits - gmax), 0.0), axis=-1, keepdims=True)
    g_w = 1.0 / gsum
    e_lane = lane - N_GROUPS
    in_grp = jnp.logical_and(
        jnp.logical_and(e_lane >= 0, e_lane < N_EXPERTS),
        (e_lane // EXPERTS_PER_GROUP).astype(F32) == grp)
    el = jnp.where(in_grp, logits, -jnp.inf)
    v1 = jnp.max(el, axis=-1, keepdims=True)
    i1 = jnp.min(jnp.where(jnp.logical_and(in_grp, el == v1), lane_f, float(LANES)),
                 axis=-1, keepdims=True)
    rest = jnp.logical_and(in_grp, lane_f != i1)
    el2 = jnp.where(rest, logits, -jnp.inf)
    v2 = jnp.max(el2, axis=-1, keepdims=True)
    i2 = jnp.min(jnp.where(jnp.logical_and(rest, el2 == v2), lane_f, float(LANES)),
                 axis=-1, keepdims=True)
    e21 = jnp.exp(v2 - v1)
    w1 = g_w / (1.0 + e21)
    w2 = g_w * e21 / (1.0 + e21)
    idx = jnp.where(lane == 0, i1, i2) - float(N_GROUPS)
    wts = jnp.where(lane == 0, w1, jnp.where(lane == 1, w2, 0.0))
    return idx.astype(jnp.int32), wts


def _out_kernel(m_ref, w_ref, x_ref, g_ref, wr_ref, br_ref, h_ref, hn_ref, idx_ref, wt_ref):
    h = x_ref[...] + jnp.dot(m_ref[...], w_ref[...], preferred_element_type=F32)
    h_ref[...] = h
    r = lax.rsqrt(jnp.mean(h * h, axis=-1, keepdims=True) + EPS)
    hn = (h * r) * g_ref[...]
    hn_ref[...] = hn
    logits = jnp.dot(hn.astype(BF16), wr_ref[...], preferred_element_type=F32) + br_ref[...]
    idx, wts = _route(logits)
    idx_ref[...] = idx
    wt_ref[...] = wts


def _out_proj(merged, w_out, x, g2, w_router, b_router, tm=256):
    T, D = x.shape
    row = lambda i: (i, 0)
    const = lambda i: (0, 0)
    return pl.pallas_call(
        _out_kernel,
        grid=(T // tm,),
        in_specs=[pl.BlockSpec((tm, D), row),
                  pl.BlockSpec((D, D), const),
                  pl.BlockSpec((tm, D), row),
                  pl.BlockSpec((1, D), const),
                  pl.BlockSpec((D, LANES), const),
                  pl.BlockSpec((1, LANES), const)],
        out_specs=[pl.BlockSpec((tm, D), row),
                   pl.BlockSpec((tm, D), row),
                   pl.BlockSpec((tm, LANES), row),
                   pl.BlockSpec((tm, LANES), row)],
        out_shape=[jax.ShapeDtypeStruct((T, D), F32),
                   jax.ShapeDtypeStruct((T, D), F32),
                   jax.ShapeDtypeStruct((T, LANES), jnp.int32),
                   jax.ShapeDtypeStruct((T, LANES), F32)],
        compiler_params=_params(("parallel",)),
        name="out_proj_router",
    )(merged, w_out.astype(BF16), x, g2.reshape(1, D), w_router, b_router)


def _expert_kernel(be_ref, rt_ref, nu_ref, hn_hbm, wg_ref, wu_ref, wd_ref, y_ref,
                   xbuf, wgb, wub, wdb, sem):
    b = pl.program_id(0)
    used = b < nu_ref[0]

    def row_copy(tok, r):
        return pltpu.make_async_copy(hn_hbm.at[pl.ds(tok, 1)], xbuf.at[pl.ds(r, 1)], sem)

    @pl.when(used)
    def _():
        base = b * MOE_BLOCK

        def issue(r, c):
            row_copy(rt_ref[base + r], r).start()
            return c

        lax.fori_loop(0, MOE_BLOCK, issue, 0)

        first = jnp.logical_or(b == 0, be_ref[b] != be_ref[jnp.maximum(b - 1, 0)])

        @pl.when(first)
        def _():
            wgb[...] = wg_ref[...].astype(BF16)
            wub[...] = wu_ref[...].astype(BF16)
            wdb[...] = wd_ref[...].astype(BF16)

        def wait(r, c):
            row_copy(0, r).wait()
            return c

        lax.fori_loop(0, MOE_BLOCK, wait, 0)

        x = xbuf[...].astype(BF16)
        hg = jnp.dot(x, wgb[...], preferred_element_type=F32)
        hu = jnp.dot(x, wub[...], preferred_element_type=F32)
        hdn = (jax.nn.silu(hg) * hu).astype(BF16)
        y_ref[...] = jnp.dot(hdn, wdb[...], preferred_element_type=F32)

    @pl.when(jnp.logical_not(used))
    def _():
        y_ref[...] = jnp.zeros(y_ref.shape, y_ref.dtype)


def _experts(hn, w_gate, w_up, w_down, block_e, row_tok, n_used):
    T, D = hn.shape
    n_rows = row_tok.shape[0]
    n_blocks = n_rows // MOE_BLOCK
    wsel = lambda b, be, rt, nu: (be[b], 0, 0)
    grid_spec = pltpu.PrefetchScalarGridSpec(
        num_scalar_prefetch=3,
        grid=(n_blocks,),
        in_specs=[pl.BlockSpec(memory_space=pl.ANY),
                  pl.BlockSpec((None, D, EXPERT_FF), wsel),
                  pl.BlockSpec((None, D, EXPERT_FF), wsel),
                  pl.BlockSpec((None, EXPERT_FF, D), wsel)],
        out_specs=pl.BlockSpec((MOE_BLOCK, D), lambda b, be, rt, nu: (b, 0)),
        scratch_shapes=[pltpu.VMEM((MOE_BLOCK, D), F32),
                        pltpu.VMEM((D, EXPERT_FF), BF16),
                        pltpu.VMEM((D, EXPERT_FF), BF16),
                        pltpu.VMEM((EXPERT_FF, D), BF16),
                        pltpu.SemaphoreType.DMA(())],
    )
    return pl.pallas_call(
        _expert_kernel,
        grid_spec=grid_spec,
        out_shape=jax.ShapeDtypeStruct((n_rows, D), F32),
        compiler_params=_params(("arbitrary",)),
        name="experts",
    )(block_e, row_tok, n_used, hn, w_gate, w_up, w_down)


def _combine_kernel(dest_ref, y_hbm, h_ref, wt_ref, o_ref, ybuf, sem, *, tc):
    i = pl.program_id(0)

    def row_copy(src, k, r):
        return pltpu.make_async_copy(y_hbm.at[pl.ds(src, 1)], ybuf.at[k, pl.ds(r, 1)], sem)

    def issue(r, c):
        for k in range(TOP_K):
            row_copy(dest_ref[(i * tc + r) * TOP_K + k], k, r).start()
        return c

    lax.fori_loop(0, tc, issue, 0)

    def wait(r, c):
        for k in range(TOP_K):
            row_copy(0, k, r).wait()
        return c

    lax.fori_loop(0, tc, wait, 0)
    wt = wt_ref[...]
    o_ref[...] = h_ref[...] + (wt[:, 0:1] * ybuf[0] + wt[:, 1:2] * ybuf[1])


def _combine(yrows, h, wts, dest, tc=128):
    T, D = h.shape
    row = lambda i, d: (i, 0)
    grid_spec = pltpu.PrefetchScalarGridSpec(
        num_scalar_prefetch=1,
        grid=(T // tc,),
        in_specs=[pl.BlockSpec(memory_space=pl.ANY),
                  pl.BlockSpec((tc, D), row),
                  pl.BlockSpec((tc, LANES), row)],
        out_specs=pl.BlockSpec((tc, D), row),
        scratch_shapes=[pltpu.VMEM((TOP_K, tc, D), F32),
                        pltpu.SemaphoreType.DMA(())],
    )
    return pl.pallas_call(
        functools.partial(_combine_kernel, tc=tc),
        grid_spec=grid_spec,
        out_shape=jax.ShapeDtypeStruct((T, D), F32),
        compiler_params=_params(("arbitrary",)),
        name="combine",
    )(dest, yrows, h, wts)


def _dispatch(expert):
    T = expert.shape[0]
    TK = T * TOP_K
    flat_e = expert.reshape(TK)
    flat_t = jnp.repeat(jnp.arange(T, dtype=jnp.int32), TOP_K)
    order = jnp.argsort(flat_e)
    se, st = flat_e[order], flat_t[order]
    counts = jnp.bincount(flat_e, length=N_EXPERTS)
    starts = jnp.cumsum(counts) - counts
    padded = (counts + MOE_BLOCK - 1) // MOE_BLOCK * MOE_BLOCK
    pad_ends = jnp.cumsum(padded)
    pad_starts = pad_ends - padded
    dest_sorted = (pad_starts[se] + jnp.arange(TK) - starts[se]).astype(jnp.int32)
    n_rows = TK + N_EXPERTS * MOE_BLOCK
    n_blocks = n_rows // MOE_BLOCK
    row_tok = jnp.zeros((n_rows,), jnp.int32).at[dest_sorted].set(st)
    dest = jnp.zeros((TK,), jnp.int32).at[order].set(dest_sorted)
    block_e = jnp.minimum(
        jnp.searchsorted(pad_ends, jnp.arange(n_blocks) * MOE_BLOCK, side='right'),
        N_EXPERTS - 1).astype(jnp.int32)
    n_used = (pad_ends[-1] // MOE_BLOCK).astype(jnp.int32).reshape(1)
    return block_e, row_tok, n_used, dest


def kernel(x, positions, norm1_g, w_in, q_norm_g, k_norm_g, sink_logits, sg_ln_g, sg_ln_b, sg_w, sg_b, w_branch_att, w_branch_sg, w_out, norm2_g, w_group_router, b_group_router, w_expert_router, b_expert_router, w_gate, w_up, w_down):
    B, S, D = x.shape
    T = B * S
    h = x.reshape(T, D)
    pos = positions.reshape(T)
    for l in range(norm1_g.shape[0]):
        xn = _rmsnorm(h, norm1_g[l])
        proj = _in_proj(xn, w_in[l])
        q, k, v = _qkv_prep(proj, pos, q_norm_g[l], k_norm_g[l])
        att = _attention(q, k, v, sink_logits[l], B)
        sgo = _spatial_gating(proj, sg_ln_g[l], sg_ln_b[l], sg_w[l], sg_b[l])
        merged = _merge(att, sgo, proj, w_branch_att[l], w_branch_sg[l])
        pad = LANES - N_GROUPS - N_EXPERTS
        w_router = jnp.concatenate(
            [w_group_router[l], w_expert_router[l], jnp.zeros((D, pad), F32)], axis=1).astype(BF16)
        b_router = jnp.concatenate(
            [b_group_router[l], b_expert_router[l], jnp.zeros((pad,), F32)]).reshape(1, LANES)
        h, hn, idx, wts = _out_proj(merged, w_out[l], h, norm2_g[l], w_router, b_router)
        block_e, row_tok, n_used, dest = _dispatch(idx[:, :TOP_K])
        yrows = _experts(hn, w_gate[l], w_up[l], w_down[l], block_e, row_tok, n_used)
        h = _combine(yrows, h, wts, dest)
    return h.reshape(B, S, D)
```

```python
import functools

import jax
import jax.numpy as jnp
from jax import lax
from jax.experimental import pallas as pl
from jax.experimental.pallas import tpu as pltpu

F32 = jnp.float32
BF16 = jnp.bfloat16

D_MODEL = 2048
HEAD_DIM = 64
ATT_WIDTH = D_MODEL // 2
ATT_HEADS = ATT_WIDTH // HEAD_DIM
ATT_KV_HEADS = ATT_HEADS // 4
Q_PER_KV = ATT_HEADS // ATT_KV_HEADS
KV_WIDTH = ATT_KV_HEADS * HEAD_DIM
WINDOW = 128
ATT_BLOCK = 128
ROPE_DIM = HEAD_DIM // 4
ROPE_HALF = ROPE_DIM // 2
ROPE_THETA = 500000.0
SG_WIDTH = D_MODEL // 2
SG_GROUP_DIM = 128
SG_GROUPS = SG_WIDTH // SG_GROUP_DIM
SG_CHUNK = 128
OFF_Q = 0
OFF_K = OFF_Q + ATT_WIDTH
OFF_V = OFF_K + KV_WIDTH
OFF_U = OFF_V + KV_WIDTH
OFF_S = OFF_U + SG_WIDTH
OFF_G = OFF_S + SG_WIDTH
IN_COLS = OFF_G + 2 * D_MODEL
N_GROUPS = 8
EXPERTS_PER_GROUP = 8
N_EXPERTS = N_GROUPS * EXPERTS_PER_GROUP
TOP_K = 2
EXPERT_FF = D_MODEL // 4
MOE_BLOCK = 128
EPS = 1e-6
NEG_INF = -1e30

LANES = 128
VMEM_LIMIT = 56 * 1024 * 1024


def _params(sem, vmem=VMEM_LIMIT):
    return pltpu.CompilerParams(dimension_semantics=sem, vmem_limit_bytes=vmem)


def _rmsnorm_kernel(x_ref, g_ref, o_ref):
    x = x_ref[...]
    r = lax.rsqrt(jnp.mean(x * x, axis=-1, keepdims=True) + EPS)
    o_ref[...] = ((x * r) * g_ref[...]).astype(o_ref.dtype)


def _rmsnorm(x, g, tm=512):
    T, D = x.shape
    return pl.pallas_call(
        _rmsnorm_kernel,
        grid=(T // tm,),
        in_specs=[pl.BlockSpec((tm, D), lambda i: (i, 0)),
                  pl.BlockSpec((1, D), lambda i: (0, 0))],
        out_specs=pl.BlockSpec((tm, D), lambda i: (i, 0)),
        out_shape=jax.ShapeDtypeStruct((T, D), BF16),
        compiler_params=_params(("parallel",)),
        name="norm1",
    )(x, g.reshape(1, D))


def _proj_kernel(x_ref, w_ref, o_ref, wbf_ref):
    @pl.when(pl.program_id(1) == 0)
    def _():
        wbf_ref[...] = w_ref[...].astype(BF16)

    o_ref[...] = jnp.dot(x_ref[...], wbf_ref[...], preferred_element_type=F32)


def _in_proj(xn, w, tm=512, tn=1280):
    T, D = xn.shape
    N = w.shape[1]
    return pl.pallas_call(
        _proj_kernel,
        grid=(N // tn, T // tm),
        in_specs=[pl.BlockSpec((tm, D), lambda j, i: (i, 0)),
                  pl.BlockSpec((D, tn), lambda j, i: (0, j))],
        out_specs=pl.BlockSpec((tm, tn), lambda j, i: (i, j)),
        out_shape=jax.ShapeDtypeStruct((T, N), F32),
        scratch_shapes=[pltpu.VMEM((D, tn), BF16)],
        compiler_params=_params(("arbitrary", "arbitrary")),
        name="in_proj",
    )(xn, w)


def _qkv_prep_kernel(p_ref, pos_ref, invf_ref, sign_ref, gq_ref, gk_ref, q_ref, k_ref, v_ref):
    pos = pos_ref[...].astype(F32)
    ang = pos * invf_ref[...]
    cos = jnp.cos(ang)
    sin = jnp.sin(ang) * sign_ref[...]
    lane = lax.broadcasted_iota(jnp.int32, ang.shape, 1)
    low_head = lane < HEAD_DIM
    first_half = (lane % HEAD_DIM) < ROPE_HALF

    def norm_rope(x, g):
        x2 = x * x
        s_lo = jnp.sum(jnp.where(low_head, x2, 0.0), axis=-1, keepdims=True)
        s_hi = jnp.sum(jnp.where(low_head, 0.0, x2), axis=-1, keepdims=True)
        ssq = jnp.where(low_head, s_lo, s_hi)
        xn = (x * lax.rsqrt(ssq * (1.0 / HEAD_DIM) + EPS)) * g
        partner = jnp.where(first_half,
                            pltpu.roll(xn, LANES - ROPE_HALF, 1),
                            pltpu.roll(xn, ROPE_HALF, 1))
        return xn * cos + partner * sin

    for c in range(ATT_WIDTH // LANES):
        x = p_ref[:, OFF_Q + c * LANES:OFF_Q + (c + 1) * LANES]
        q_ref[:, c * LANES:(c + 1) * LANES] = (
            norm_rope(x, gq_ref[...]) * (HEAD_DIM ** -0.5)).astype(q_ref.dtype)
    for c in range(KV_WIDTH // LANES):
        x = p_ref[:, OFF_K + c * LANES:OFF_K + (c + 1) * LANES]
        k_ref[:, c * LANES:(c + 1) * LANES] = norm_rope(x, gk_ref[...]).astype(k_ref.dtype)
    v_ref[...] = p_ref[:, OFF_V:OFF_V + KV_WIDTH].astype(v_ref.dtype)


def _qkv_prep(proj, positions, q_g, k_g, tq=256):
    T = proj.shape[0]
    width = OFF_U
    lane = jnp.arange(LANES) % HEAD_DIM
    inv = ROPE_THETA ** (-jnp.arange(0, ROPE_DIM, 2, dtype=F32) / ROPE_DIM)
    invf = jnp.where(lane < ROPE_DIM, inv[lane % ROPE_HALF], 0.0).reshape(1, LANES)
    sign = jnp.where(lane < ROPE_HALF, -1.0, jnp.where(lane < ROPE_DIM, 1.0, 0.0))
    sign = sign.astype(F32).reshape(1, LANES)
    gq = jnp.tile(q_g, LANES // HEAD_DIM).reshape(1, LANES)
    gk = jnp.tile(k_g, LANES // HEAD_DIM).reshape(1, LANES)
    row = lambda i: (i, 0)
    const = lambda i: (0, 0)
    return pl.pallas_call(
        _qkv_prep_kernel,
        grid=(T // tq,),
        in_specs=[pl.BlockSpec((tq, width), row),
                  pl.BlockSpec((tq, 1), row),
                  pl.BlockSpec((1, LANES), const),
                  pl.BlockSpec((1, LANES), const),
                  pl.BlockSpec((1, LANES), const),
                  pl.BlockSpec((1, LANES), const)],
        out_specs=[pl.BlockSpec((tq, ATT_WIDTH), row),
                   pl.BlockSpec((tq, KV_WIDTH), row),
                   pl.BlockSpec((tq, KV_WIDTH), row)],
        out_shape=[jax.ShapeDtypeStruct((T, ATT_WIDTH), BF16),
                   jax.ShapeDtypeStruct((T, KV_WIDTH), BF16),
                   jax.ShapeDtypeStruct((T, KV_WIDTH), BF16)],
        compiler_params=_params(("parallel",)),
        name="qkv_prep",
    )(proj, positions.reshape(T, 1), invf, sign, gq, gk)


def _attn_kernel(sink_ref, q_ref, kp_ref, kc_ref, kn_ref, vp_ref, vc_ref, vn_ref, o_ref, *, nb):
    n = pl.program_id(1)
    rows = Q_PER_KV * ATT_BLOCK
    keys = 3 * ATT_BLOCK
    qi = lax.broadcasted_iota(jnp.int32, (rows, keys), 0) % ATT_BLOCK
    kj = lax.broadcasted_iota(jnp.int32, (rows, keys), 1)
    key_pos = kj + (n - 1) * ATT_BLOCK
    in_seq = jnp.logical_and(key_pos >= 0, key_pos < nb * ATT_BLOCK)
    mask = jnp.logical_and(jnp.abs(kj - ATT_BLOCK - qi) <= WINDOW, in_seq)
    row_head = lax.broadcasted_iota(jnp.int32, (rows, 1), 0) // ATT_BLOCK

    for kvh in range(ATT_KV_HEADS):
        cols = slice(kvh * HEAD_DIM, (kvh + 1) * HEAD_DIM)
        k = jnp.concatenate([kp_ref[:, cols], kc_ref[:, cols], kn_ref[:, cols]], axis=0)
        v = jnp.concatenate([vp_ref[:, cols], vc_ref[:, cols], vn_ref[:, cols]], axis=0)
        q = jnp.concatenate(
            [q_ref[:, (kvh * Q_PER_KV + g) * HEAD_DIM:(kvh * Q_PER_KV + g + 1) * HEAD_DIM]
             for g in range(Q_PER_KV)], axis=0)
        sink = jnp.zeros((rows, 1), F32)
        for g in range(Q_PER_KV):
            sink = jnp.where(row_head == g, sink_ref[kvh * Q_PER_KV + g], sink)
        s = lax.dot_general(q, k, (((1,), (1,)), ((), ())), preferred_element_type=F32)
        s = jnp.where(mask, s, NEG_INF)
        m = jnp.maximum(jnp.max(s, axis=-1, keepdims=True), sink)
        e = jnp.exp(s - m)
        denom = jnp.sum(e, axis=-1, keepdims=True) + jnp.exp(sink - m)
        p = (e / denom).astype(BF16)
        o = jnp.dot(p, v, preferred_element_type=F32)
        for g in range(Q_PER_KV):
            h = kvh * Q_PER_KV + g
            o_ref[:, h * HEAD_DIM:(h + 1) * HEAD_DIM] = (
                o[g * ATT_BLOCK:(g + 1) * ATT_BLOCK].astype(o_ref.dtype))


def _attention(q, k, v, sink, batch):
    T = q.shape[0]
    nb = T // batch // ATT_BLOCK
    cur = lambda b, n: (b * nb + n, 0)
    prev = lambda b, n: (b * nb + jnp.maximum(n - 1, 0), 0)
    nxt = lambda b, n: (b * nb + jnp.minimum(n + 1, nb - 1), 0)
    kv = lambda im: pl.BlockSpec((ATT_BLOCK, KV_WIDTH), im)
    return pl.pallas_call(
        functools.partial(_attn_kernel, nb=nb),
        grid=(batch, nb),
        in_specs=[pl.BlockSpec(memory_space=pltpu.SMEM),
                  pl.BlockSpec((ATT_BLOCK, ATT_WIDTH), cur),
                  kv(prev), kv(cur), kv(nxt), kv(prev), kv(cur), kv(nxt)],
        out_specs=pl.BlockSpec((ATT_BLOCK, ATT_WIDTH), cur),
        out_shape=jax.ShapeDtypeStruct((T, ATT_WIDTH), BF16),
        compiler_params=_params(("parallel", "parallel")),
        name="window_attn",
    )(sink, q, k, k, k, v, v, v)


def _sg_kernel(u_ref, s_ref, lng_ref, lnb_ref, w_ref, b_ref, o_ref):
    groups = w_ref.shape[0]
    for gi in range(groups):
        cols = slice(gi * SG_GROUP_DIM, (gi + 1) * SG_GROUP_DIM)
        s = jax.nn.gelu(s_ref[:, cols])
        mu = jnp.mean(s, axis=-1, keepdims=True)
        sc = s - mu
        var = jnp.mean(sc * sc, axis=-1, keepdims=True)
        sn = (sc * lax.rsqrt(var + EPS)) * lng_ref[0, gi:gi + 1, :] + lnb_ref[0, gi:gi + 1, :]
        mixed = jnp.dot(w_ref[gi], sn.astype(BF16), preferred_element_type=F32)
        mixed = mixed + b_ref[0, :, gi:gi + 1]
        o_ref[:, cols] = (jax.nn.gelu(u_ref[:, cols]) * mixed).astype(o_ref.dtype)


def _spatial_gating(proj, ln_g, ln_b, w_s, b_s, halves=2):
    T = proj.shape[0]
    half_w = SG_WIDTH // halves
    gph = SG_GROUPS // halves
    u0, s0 = OFF_U // half_w, OFF_S // half_w
    return pl.pallas_call(
        _sg_kernel,
        grid=(T // SG_CHUNK, halves),
        in_specs=[pl.BlockSpec((SG_CHUNK, half_w), lambda i, j: (i, u0 + j)),
                  pl.BlockSpec((SG_CHUNK, half_w), lambda i, j: (i, s0 + j)),
                  pl.BlockSpec((1, gph, SG_GROUP_DIM), lambda i, j: (j, 0, 0)),
                  pl.BlockSpec((1, gph, SG_GROUP_DIM), lambda i, j: (j, 0, 0)),
                  pl.BlockSpec((gph, SG_CHUNK, SG_CHUNK), lambda i, j: (j, 0, 0)),
                  pl.BlockSpec((1, SG_CHUNK, gph), lambda i, j: (j, 0, 0))],
        out_specs=pl.BlockSpec((SG_CHUNK, half_w), lambda i, j: (i, j)),
        out_shape=jax.ShapeDtypeStruct((T, SG_WIDTH), BF16),
        compiler_params=_params(("parallel", "parallel")),
        name="spatial_gating",
    )(proj, proj,
      ln_g.reshape(halves, gph, SG_GROUP_DIM), ln_b.reshape(halves, gph, SG_GROUP_DIM),
      w_s.astype(BF16),
      b_s.reshape(halves, gph, SG_CHUNK).transpose(0, 2, 1))


def _merge_kernel(att_ref, sgo_ref, ga_ref, gb_ref, wa_ref, wb_ref, o_ref):
    a = jnp.dot(att_ref[...], wa_ref[...], preferred_element_type=F32)
    b = jnp.dot(sgo_ref[...], wb_ref[...], preferred_element_type=F32)
    m = jax.nn.sigmoid(ga_ref[...]) * a + jax.nn.sigmoid(gb_ref[...]) * b
    o_ref[...] = m.astype(o_ref.dtype)


def _merge(att, sgo, proj, w_a, w_b, tm=512, tn=512):
    T = att.shape[0]
    ga0 = OFF_G // tn
    gb0 = (OFF_G + D_MODEL) // tn
    return pl.pallas_call(
        _merge_kernel,
        grid=(T // tm, D_MODEL // tn),
        in_specs=[pl.BlockSpec((tm, ATT_WIDTH), lambda i, j: (i, 0)),
                  pl.BlockSpec((tm, SG_WIDTH), lambda i, j: (i, 0)),
                  pl.BlockSpec((tm, tn), lambda i, j: (i, ga0 + j)),
                  pl.BlockSpec((tm, tn), lambda i, j: (i, gb0 + j)),
                  pl.BlockSpec((ATT_WIDTH, tn), lambda i, j: (0, j)),
                  pl.BlockSpec((SG_WIDTH, tn), lambda i, j: (0, j))],
        out_specs=pl.BlockSpec((tm, tn), lambda i, j: (i, j)),
        out_shape=jax.ShapeDtypeStruct((T, D_MODEL), BF16),
        compiler_params=_params(("parallel", "parallel")),
        name="merge",
    )(att, sgo, proj, proj, w_a.astype(BF16), w_b.astype(BF16))


def _route(logits):
    lane = lax.broadcasted_iota(jnp.int32, logits.shape, 1)
    lane_f = lane.astype(F32)
    is_g = lane < N_GROUPS
    gl = jnp.where(is_g, logits, -jnp.inf)
    gmax = jnp.max(gl, axis=-1, keepdims=True)
    grp = jnp.min(jnp.where(gl == gmax, lane_f, float(LANES)), axis=-1, keepdims=True)
    gsum = jnp.sum(jnp.where(is_g, jnp.exp(logits - gmax), 0.0), axis=-1, keepdims=True)
    g_w = 1.0 / gsum
    e_lane = lane - N_GROUPS
    in_grp = jnp.logical_and(
        jnp.logical_and(e_lane >= 0, e_lane < N_EXPERTS),
        (e_lane // EXPERTS_PER_GROUP).astype(F32) == grp)
    el = jnp.where(in_grp, logits, -jnp.inf)
    v1 = jnp.max(el, axis=-1, keepdims=True)
    i1 = jnp.min(jnp.where(jnp.logical_and(in_grp, el == v1), lane_f, float(LANES)),
                 axis=-1, keepdims=True)
    rest = jnp.logical_and(in_grp, lane_f != i1)
    el2 = jnp.where(rest, logits, -jnp.inf)
    v2 = jnp.max(el2, axis=-1, keepdims=True)
    i2 = jnp.min(jnp.where(jnp.logical_and(rest, el2 == v2), lane_f, float(LANES)),
                 axis=-1, keepdims=True)
    e21 = jnp.exp(v2 - v1)
    w1 = g_w / (1.0 + e21)
    w2 = g_w * e21 / (1.0 + e21)
    idx = jnp.where(lane == 0, i1, i2) - float(N_GROUPS)
    wts = jnp.where(lane == 0, w1, jnp.where(lane == 1, w2, 0.0))
    return idx.astype(jnp.int32), wts


def _out_kernel(m_ref, w_ref, x_ref, g_ref, wr_ref, br_ref, h_ref, hn_ref, idx_ref, wt_ref):
    h = x_ref[...] + jnp.dot(m_ref[...], w_ref[...], preferred_element_type=F32)
    h_ref[...] = h
    r = lax.rsqrt(jnp.mean(h * h, axis=-1, keepdims=True) + EPS)
    hn = (h * r) * g_ref[...]
    hn_ref[...] = hn
    logits = jnp.dot(hn.astype(BF16), wr_ref[...], preferred_element_type=F32) + br_ref[...]
    idx, wts = _route(logits)
    idx_ref[...] = idx
    wt_ref[...] = wts


def _out_proj(merged, w_out, x, g2, w_router, b_router, tm=256):
    T, D = x.shape
    row = lambda i: (i, 0)
    const = lambda i: (0, 0)
    return pl.pallas_call(
        _out_kernel,
        grid=(T // tm,),
        in_specs=[pl.BlockSpec((tm, D), row),
                  pl.BlockSpec((D, D), const),
                  pl.BlockSpec((tm, D), row),
                  pl.BlockSpec((1, D), const),
                  pl.BlockSpec((D, LANES), const),
                  pl.BlockSpec((1, LANES), const)],
        out_specs=[pl.BlockSpec((tm, D), row),
                   pl.BlockSpec((tm, D), row),
                   pl.BlockSpec((tm, LANES), row),
                   pl.BlockSpec((tm, LANES), row)],
        out_shape=[jax.ShapeDtypeStruct((T, D), F32),
                   jax.ShapeDtypeStruct((T, D), F32),
                   jax.ShapeDtypeStruct((T, LANES), jnp.int32),
                   jax.ShapeDtypeStruct((T, LANES), F32)],
        compiler_params=_params(("parallel",)),
        name="out_proj_router",
    )(merged, w_out.astype(BF16), x, g2.reshape(1, D), w_router, b_router)


def _expert_kernel(be_ref, rt_ref, nu_ref, hn_hbm, wg_ref, wu_ref, wd_ref, y_ref,
                   xbuf, wgb, wub, wdb, sem):
    b = pl.program_id(0)
    n_used = nu_ref[0]
    used = b < n_used
    slot = b % 2

    def gather_rows(blk, slot_):
        base = blk * MOE_BLOCK

        def issue(r, c):
            pltpu.make_async_copy(hn_hbm.at[pl.ds(rt_ref[base + r], 1)],
                                  xbuf.at[slot_, pl.ds(r, 1)], sem.at[slot_]).start()
            return c

        lax.fori_loop(0, MOE_BLOCK, issue, 0, unroll=8)

    @pl.when(b == 0)
    def _():
        gather_rows(0, 0)

    @pl.when(b + 1 < n_used)
    def _():
        gather_rows(b + 1, 1 - slot)

    @pl.when(used)
    def _():
        first = jnp.logical_or(b == 0, be_ref[b] != be_ref[jnp.maximum(b - 1, 0)])

        @pl.when(first)
        def _():
            wgb[...] = wg_ref[...].astype(BF16)
            wub[...] = wu_ref[...].astype(BF16)
            wdb[...] = wd_ref[...].astype(BF16)

        pltpu.make_async_copy(hn_hbm.at[pl.ds(0, MOE_BLOCK)], xbuf.at[slot], sem.at[slot]).wait()

        x = xbuf[slot].astype(BF16)
        hg = jnp.dot(x, wgb[...], preferred_element_type=F32)
        hu = jnp.dot(x, wub[...], preferred_element_type=F32)
        hdn = (jax.nn.silu(hg) * hu).astype(BF16)
        y_ref[...] = jnp.dot(hdn, wdb[...], preferred_element_type=F32)

    @pl.when(jnp.logical_not(used))
    def _():
        y_ref[...] = jnp.zeros(y_ref.shape, y_ref.dtype)


def _experts(hn, w_gate, w_up, w_down, block_e, row_tok, n_used):
    T, D = hn.shape
    n_rows = row_tok.shape[0]
    n_blocks = n_rows // MOE_BLOCK
    wsel = lambda b, be, rt, nu: (be[b], 0, 0)
    grid_spec = pltpu.PrefetchScalarGridSpec(
        num_scalar_prefetch=3,
        grid=(n_blocks,),
        in_specs=[pl.BlockSpec(memory_space=pl.ANY),
                  pl.BlockSpec((None, D, EXPERT_FF), wsel),
                  pl.BlockSpec((None, D, EXPERT_FF), wsel),
                  pl.BlockSpec((None, EXPERT_FF, D), wsel)],
        out_specs=pl.BlockSpec((MOE_BLOCK, D), lambda b, be, rt, nu: (b, 0)),
        scratch_shapes=[pltpu.VMEM((2, MOE_BLOCK, D), F32),
                        pltpu.VMEM((D, EXPERT_FF), BF16),
                        pltpu.VMEM((D, EXPERT_FF), BF16),
                        pltpu.VMEM((EXPERT_FF, D), BF16),
                        pltpu.SemaphoreType.DMA((2,))],
    )
    return pl.pallas_call(
        _expert_kernel,
        grid_spec=grid_spec,
        out_shape=jax.ShapeDtypeStruct((n_rows, D), F32),
        compiler_params=_params(("arbitrary",)),
        name="experts",
    )(block_e, row_tok, n_used, hn, w_gate, w_up, w_down)


def _combine_kernel(dest_ref, y_hbm, h_ref, wt_ref, o_ref, ybuf, sem, *, tc):
    i = pl.program_id(0)
    slot = i % 2

    def gather_rows(blk, slot_):
        def issue(r, c):
            for k in range(TOP_K):
                src = dest_ref[(blk * tc + r) * TOP_K + k]
                pltpu.make_async_copy(y_hbm.at[pl.ds(src, 1)],
                                      ybuf.at[slot_, k, pl.ds(r, 1)], sem.at[slot_]).start()
            return c

        lax.fori_loop(0, tc, issue, 0, unroll=4)

    @pl.when(i == 0)
    def _():
        gather_rows(0, 0)

    @pl.when(i + 1 < pl.num_programs(0))
    def _():
        gather_rows(i + 1, 1 - slot)

    for k in range(TOP_K):
        pltpu.make_async_copy(y_hbm.at[pl.ds(0, tc)], ybuf.at[slot, k], sem.at[slot]).wait()
    wt = wt_ref[...]
    o_ref[...] = h_ref[...] + (wt[:, 0:1] * ybuf[slot, 0] + wt[:, 1:2] * ybuf[slot, 1])


def _combine(yrows, h, wts, dest, tc=128):
    T, D = h.shape
    row = lambda i, d: (i, 0)
    grid_spec = pltpu.PrefetchScalarGridSpec(
        num_scalar_prefetch=1,
        grid=(T // tc,),
        in_specs=[pl.BlockSpec(memory_space=pl.ANY),
                  pl.BlockSpec((tc, D), row),
                  pl.BlockSpec((tc, LANES), row)],
        out_specs=pl.BlockSpec((tc, D), row),
        scratch_shapes=[pltpu.VMEM((2, TOP_K, tc, D), F32),
                        pltpu.SemaphoreType.DMA((2,))],
    )
    return pl.pallas_call(
        functools.partial(_combine_kernel, tc=tc),
        grid_spec=grid_spec,
        out_shape=jax.ShapeDtypeStruct((T, D), F32),
        compiler_params=_params(("arbitrary",)),
        name="combine",
    )(dest, yrows, h, wts)


def _dispatch(expert):
    T = expert.shape[0]
    TK = T * TOP_K
    flat_e = expert.reshape(TK)
    flat_t = jnp.repeat(jnp.arange(T, dtype=jnp.int32), TOP_K)
    order = jnp.argsort(flat_e)
    se, st = flat_e[order], flat_t[order]
    counts = jnp.bincount(flat_e, length=N_EXPERTS)
    starts = jnp.cumsum(counts) - counts
    padded = (counts + MOE_BLOCK - 1) // MOE_BLOCK * MOE_BLOCK
    pad_ends = jnp.cumsum(padded)
    pad_starts = pad_ends - padded
    dest_sorted = (pad_starts[se] + jnp.arange(TK) - starts[se]).astype(jnp.int32)
    n_rows = TK + N_EXPERTS * MOE_BLOCK
    n_blocks = n_rows // MOE_BLOCK
    row_tok = jnp.zeros((n_rows,), jnp.int32).at[dest_sorted].set(st)
    dest = jnp.zeros((TK,), jnp.int32).at[order].set(dest_sorted)
    block_e = jnp.minimum(
        jnp.searchsorted(pad_ends, jnp.arange(n_blocks) * MOE_BLOCK, side='right'),
        N_EXPERTS - 1).astype(jnp.int32)
    n_used = (pad_ends[-1] // MOE_BLOCK).astype(jnp.int32).reshape(1)
    return block_e, row_tok, n_used, dest


def kernel(x, positions, norm1_g, w_in, q_norm_g, k_norm_g, sink_logits, sg_ln_g, sg_ln_b, sg_w, sg_b, w_branch_att, w_branch_sg, w_out, norm2_g, w_group_router, b_group_router, w_expert_router, b_expert_router, w_gate, w_up, w_down):
    B, S, D = x.shape
    T = B * S
    h = x.reshape(T, D)
    pos = positions.reshape(T)
    for l in range(norm1_g.shape[0]):
        xn = _rmsnorm(h, norm1_g[l])
        proj = _in_proj(xn, w_in[l])
        q, k, v = _qkv_prep(proj, pos, q_norm_g[l], k_norm_g[l])
        att = _attention(q, k, v, sink_logits[l], B)
        sgo = _spatial_gating(proj, sg_ln_g[l], sg_ln_b[l], sg_w[l], sg_b[l])
        merged = _merge(att, sgo, proj, w_branch_att[l], w_branch_sg[l])
        pad = LANES - N_GROUPS - N_EXPERTS
        w_router = jnp.concatenate(
            [w_group_router[l], w_expert_router[l], jnp.zeros((D, pad), F32)], axis=1).astype(BF16)
        b_router = jnp.concatenate(
            [b_group_router[l], b_expert_router[l], jnp.zeros((pad,), F32)]).reshape(1, LANES)
        h, hn, idx, wts = _out_proj(merged, w_out[l], h, norm2_g[l], w_router, b_router)
        block_e, row_tok, n_used, dest = _dispatch(idx[:, :TOP_K])
        yrows = _experts(hn, w_gate[l], w_up[l], w_down[l], block_e, row_tok, n_used)
        h = _combine(yrows, h, wts, dest)
    return h.reshape(B, S, D)
```

```python
import functools

import jax
import jax.numpy as jnp
from jax import lax
from jax.experimental import pallas as pl
from jax.experimental.pallas import tpu as pltpu

F32 = jnp.float32
BF16 = jnp.bfloat16

D_MODEL = 2048
HEAD_DIM = 64
ATT_WIDTH = D_MODEL // 2
ATT_HEADS = ATT_WIDTH // HEAD_DIM
ATT_KV_HEADS = ATT_HEADS // 4
Q_PER_KV = ATT_HEADS // ATT_KV_HEADS
KV_WIDTH = ATT_KV_HEADS * HEAD_DIM
WINDOW = 128
ATT_BLOCK = 128
ROPE_DIM = HEAD_DIM // 4
ROPE_HALF = ROPE_DIM // 2
ROPE_THETA = 500000.0
SG_WIDTH = D_MODEL // 2
SG_GROUP_DIM = 128
SG_GROUPS = SG_WIDTH // SG_GROUP_DIM
SG_CHUNK = 128
OFF_Q = 0
OFF_K = OFF_Q + ATT_WIDTH
OFF_V = OFF_K + KV_WIDTH
OFF_U = OFF_V + KV_WIDTH
OFF_S = OFF_U + SG_WIDTH
OFF_G = OFF_S + SG_WIDTH
IN_COLS = OFF_G + 2 * D_MODEL
N_GROUPS = 8
EXPERTS_PER_GROUP = 8
N_EXPERTS = N_GROUPS * EXPERTS_PER_GROUP
TOP_K = 2
EXPERT_FF = D_MODEL // 4
MOE_BLOCK = 128
EPS = 1e-6
NEG_INF = -1e30

LANES = 128
VMEM_LIMIT = 56 * 1024 * 1024


def _params(sem, vmem=VMEM_LIMIT):
    return pltpu.CompilerParams(dimension_semantics=sem, vmem_limit_bytes=vmem)


def _rmsnorm_kernel(x_ref, g_ref, o_ref):
    x = x_ref[...]
    r = lax.rsqrt(jnp.mean(x * x, axis=-1, keepdims=True) + EPS)
    o_ref[...] = ((x * r) * g_ref[...]).astype(o_ref.dtype)


def _rmsnorm(x, g, tm=512):
    T, D = x.shape
    return pl.pallas_call(
        _rmsnorm_kernel,
        grid=(T // tm,),
        in_specs=[pl.BlockSpec((tm, D), lambda i: (i, 0)),
                  pl.BlockSpec((1, D), lambda i: (0, 0))],
        out_specs=pl.BlockSpec((tm, D), lambda i: (i, 0)),
        out_shape=jax.ShapeDtypeStruct((T, D), BF16),
        compiler_params=_params(("parallel",)),
        name="norm1",
    )(x, g.reshape(1, D))


def _proj_kernel(x_ref, w_ref, o_ref, wbf_ref):
    @pl.when(pl.program_id(1) == 0)
    def _():
        wbf_ref[...] = w_ref[...].astype(BF16)

    o_ref[...] = jnp.dot(x_ref[...], wbf_ref[...], preferred_element_type=F32)


def _in_proj(xn, w, tm=512, tn=1280):
    T, D = xn.shape
    N = w.shape[1]
    return pl.pallas_call(
        _proj_kernel,
        grid=(N // tn, T // tm),
        in_specs=[pl.BlockSpec((tm, D), lambda j, i: (i, 0)),
                  pl.BlockSpec((D, tn), lambda j, i: (0, j))],
        out_specs=pl.BlockSpec((tm, tn), lambda j, i: (i, j)),
        out_shape=jax.ShapeDtypeStruct((T, N), F32),
        scratch_shapes=[pltpu.VMEM((D, tn), BF16)],
        compiler_params=_params(("arbitrary", "arbitrary")),
        name="in_proj",
    )(xn, w)


def _qkv_prep_kernel(p_ref, pos_ref, invf_ref, sign_ref, gq_ref, gk_ref, q_ref, k_ref, v_ref):
    pos = pos_ref[...].astype(F32)
    ang = pos * invf_ref[...]
    cos = jnp.cos(ang)
    sin = jnp.sin(ang) * sign_ref[...]
    lane = lax.broadcasted_iota(jnp.int32, ang.shape, 1)
    low_head = lane < HEAD_DIM
    first_half = (lane % HEAD_DIM) < ROPE_HALF

    def norm_rope(x, g):
        x2 = x * x
        s_lo = jnp.sum(jnp.where(low_head, x2, 0.0), axis=-1, keepdims=True)
        s_hi = jnp.sum(jnp.where(low_head, 0.0, x2), axis=-1, keepdims=True)
        ssq = jnp.where(low_head, s_lo, s_hi)
        xn = (x * lax.rsqrt(ssq * (1.0 / HEAD_DIM) + EPS)) * g
        partner = jnp.where(first_half,
                            pltpu.roll(xn, LANES - ROPE_HALF, 1),
                            pltpu.roll(xn, ROPE_HALF, 1))
        return xn * cos + partner * sin

    for c in range(ATT_WIDTH // LANES):
        x = p_ref[:, OFF_Q + c * LANES:OFF_Q + (c + 1) * LANES]
        q_ref[:, c * LANES:(c + 1) * LANES] = (
            norm_rope(x, gq_ref[...]) * (HEAD_DIM ** -0.5)).astype(q_ref.dtype)
    for c in range(KV_WIDTH // LANES):
        x = p_ref[:, OFF_K + c * LANES:OFF_K + (c + 1) * LANES]
        k_ref[:, c * LANES:(c + 1) * LANES] = norm_rope(x, gk_ref[...]).astype(k_ref.dtype)
    v_ref[...] = p_ref[:, OFF_V:OFF_V + KV_WIDTH].astype(v_ref.dtype)


def _qkv_prep(proj, positions, q_g, k_g, tq=256):
    T = proj.shape[0]
    width = OFF_U
    lane = jnp.arange(LANES) % HEAD_DIM
    inv = ROPE_THETA ** (-jnp.arange(0, ROPE_DIM, 2, dtype=F32) / ROPE_DIM)
    invf = jnp.where(lane < ROPE_DIM, inv[lane % ROPE_HALF], 0.0).reshape(1, LANES)
    sign = jnp.where(lane < ROPE_HALF, -1.0, jnp.where(lane < ROPE_DIM, 1.0, 0.0))
    sign = sign.astype(F32).reshape(1, LANES)
    gq = jnp.tile(q_g, LANES // HEAD_DIM).reshape(1, LANES)
    gk = jnp.tile(k_g, LANES // HEAD_DIM).reshape(1, LANES)
    row = lambda i: (i, 0)
    const = lambda i: (0, 0)
    return pl.pallas_call(
        _qkv_prep_kernel,
        grid=(T // tq,),
        in_specs=[pl.BlockSpec((tq, width), row),
                  pl.BlockSpec((tq, 1), row),
                  pl.BlockSpec((1, LANES), const),
                  pl.BlockSpec((1, LANES), const),
                  pl.BlockSpec((1, LANES), const),
                  pl.BlockSpec((1, LANES), const)],
        out_specs=[pl.BlockSpec((tq, ATT_WIDTH), row),
                   pl.BlockSpec((tq, KV_WIDTH), row),
                   pl.BlockSpec((tq, KV_WIDTH), row)],
        out_shape=[jax.ShapeDtypeStruct((T, ATT_WIDTH), BF16),
                   jax.ShapeDtypeStruct((T, KV_WIDTH), BF16),
                   jax.ShapeDtypeStruct((T, KV_WIDTH), BF16)],
        compiler_params=_params(("parallel",)),
        name="qkv_prep",
    )(proj, positions.reshape(T, 1), invf, sign, gq, gk)


def _attn_kernel(sink_ref, q_ref, kp_ref, kc_ref, kn_ref, vp_ref, vc_ref, vn_ref, o_ref, *, nb):
    n = pl.program_id(1)
    rows = Q_PER_KV * ATT_BLOCK
    keys = 3 * ATT_BLOCK
    qi = lax.broadcasted_iota(jnp.int32, (rows, keys), 0) % ATT_BLOCK
    kj = lax.broadcasted_iota(jnp.int32, (rows, keys), 1)
    key_pos = kj + (n - 1) * ATT_BLOCK
    in_seq = jnp.logical_and(key_pos >= 0, key_pos < nb * ATT_BLOCK)
    mask = jnp.logical_and(jnp.abs(kj - ATT_BLOCK - qi) <= WINDOW, in_seq)
    row_head = lax.broadcasted_iota(jnp.int32, (rows, 1), 0) // ATT_BLOCK

    for kvh in range(ATT_KV_HEADS):
        cols = slice(kvh * HEAD_DIM, (kvh + 1) * HEAD_DIM)
        k = jnp.concatenate([kp_ref[:, cols], kc_ref[:, cols], kn_ref[:, cols]], axis=0)
        v = jnp.concatenate([vp_ref[:, cols], vc_ref[:, cols], vn_ref[:, cols]], axis=0)
        q = jnp.concatenate(
            [q_ref[:, (kvh * Q_PER_KV + g) * HEAD_DIM:(kvh * Q_PER_KV + g + 1) * HEAD_DIM]
             for g in range(Q_PER_KV)], axis=0)
        sink = jnp.zeros((rows, 1), F32)
        for g in range(Q_PER_KV):
            sink = jnp.where(row_head == g, sink_ref[kvh * Q_PER_KV + g], sink)
        s = lax.dot_general(q, k, (((1,), (1,)), ((), ())), preferred_element_type=F32)
        s = jnp.where(mask, s, NEG_INF)
        m = jnp.maximum(jnp.max(s, axis=-1, keepdims=True), sink)
        e = jnp.exp(s - m)
        denom = jnp.sum(e, axis=-1, keepdims=True) + jnp.exp(sink - m)
        p = (e / denom).astype(BF16)
        o = jnp.dot(p, v, preferred_element_type=F32)
        for g in range(Q_PER_KV):
            h = kvh * Q_PER_KV + g
            o_ref[:, h * HEAD_DIM:(h + 1) * HEAD_DIM] = (
                o[g * ATT_BLOCK:(g + 1) * ATT_BLOCK].astype(o_ref.dtype))


def _attention(q, k, v, sink, batch):
    T = q.shape[0]
    nb = T // batch // ATT_BLOCK
    cur = lambda b, n: (b * nb + n, 0)
    prev = lambda b, n: (b * nb + jnp.maximum(n - 1, 0), 0)
    nxt = lambda b, n: (b * nb + jnp.minimum(n + 1, nb - 1), 0)
    kv = lambda im: pl.BlockSpec((ATT_BLOCK, KV_WIDTH), im)
    return pl.pallas_call(
        functools.partial(_attn_kernel, nb=nb),
        grid=(batch, nb),
        in_specs=[pl.BlockSpec(memory_space=pltpu.SMEM),
                  pl.BlockSpec((ATT_BLOCK, ATT_WIDTH), cur),
                  kv(prev), kv(cur), kv(nxt), kv(prev), kv(cur), kv(nxt)],
        out_specs=pl.BlockSpec((ATT_BLOCK, ATT_WIDTH), cur),
        out_shape=jax.ShapeDtypeStruct((T, ATT_WIDTH), BF16),
        compiler_params=_params(("parallel", "parallel")),
        name="window_attn",
    )(sink, q, k, k, k, v, v, v)


def _sg_kernel(u_ref, s_ref, lng_ref, lnb_ref, w_ref, b_ref, o_ref):
    groups = w_ref.shape[0]
    for gi in range(groups):
        cols = slice(gi * SG_GROUP_DIM, (gi + 1) * SG_GROUP_DIM)
        s = jax.nn.gelu(s_ref[:, cols])
        mu = jnp.mean(s, axis=-1, keepdims=True)
        sc = s - mu
        var = jnp.mean(sc * sc, axis=-1, keepdims=True)
        sn = (sc * lax.rsqrt(var + EPS)) * lng_ref[0, gi:gi + 1, :] + lnb_ref[0, gi:gi + 1, :]
        mixed = jnp.dot(w_ref[gi], sn.astype(BF16), preferred_element_type=F32)
        mixed = mixed + b_ref[0, :, gi:gi + 1]
        o_ref[:, cols] = (jax.nn.gelu(u_ref[:, cols]) * mixed).astype(o_ref.dtype)


def _spatial_gating(proj, ln_g, ln_b, w_s, b_s, halves=2):
    T = proj.shape[0]
    half_w = SG_WIDTH // halves
    gph = SG_GROUPS // halves
    u0, s0 = OFF_U // half_w, OFF_S // half_w
    return pl.pallas_call(
        _sg_kernel,
        grid=(T // SG_CHUNK, halves),
        in_specs=[pl.BlockSpec((SG_CHUNK, half_w), lambda i, j: (i, u0 + j)),
                  pl.BlockSpec((SG_CHUNK, half_w), lambda i, j: (i, s0 + j)),
                  pl.BlockSpec((1, gph, SG_GROUP_DIM), lambda i, j: (j, 0, 0)),
                  pl.BlockSpec((1, gph, SG_GROUP_DIM), lambda i, j: (j, 0, 0)),
                  pl.BlockSpec((gph, SG_CHUNK, SG_CHUNK), lambda i, j: (j, 0, 0)),
                  pl.BlockSpec((1, SG_CHUNK, gph), lambda i, j: (j, 0, 0))],
        out_specs=pl.BlockSpec((SG_CHUNK, half_w), lambda i, j: (i, j)),
        out_shape=jax.ShapeDtypeStruct((T, SG_WIDTH), BF16),
        compiler_params=_params(("parallel", "parallel")),
        name="spatial_gating",
    )(proj, proj,
      ln_g.reshape(halves, gph, SG_GROUP_DIM), ln_b.reshape(halves, gph, SG_GROUP_DIM),
      w_s.astype(BF16),
      b_s.reshape(halves, gph, SG_CHUNK).transpose(0, 2, 1))


def _merge_kernel(att_ref, sgo_ref, ga_ref, gb_ref, wa_ref, wb_ref, o_ref):
    a = jnp.dot(att_ref[...], wa_ref[...], preferred_element_type=F32)
    b = jnp.dot(sgo_ref[...], wb_ref[...], preferred_element_type=F32)
    m = jax.nn.sigmoid(ga_ref[...]) * a + jax.nn.sigmoid(gb_ref[...]) * b
    o_ref[...] = m.astype(o_ref.dtype)


def _merge(att, sgo, proj, w_a, w_b, tm=512, tn=512):
    T = att.shape[0]
    ga0 = OFF_G // tn
    gb0 = (OFF_G + D_MODEL) // tn
    return pl.pallas_call(
        _merge_kernel,
        grid=(T // tm, D_MODEL // tn),
        in_specs=[pl.BlockSpec((tm, ATT_WIDTH), lambda i, j: (i, 0)),
                  pl.BlockSpec((tm, SG_WIDTH), lambda i, j: (i, 0)),
                  pl.BlockSpec((tm, tn), lambda i, j: (i, ga0 + j)),
                  pl.BlockSpec((tm, tn), lambda i, j: (i, gb0 + j)),
                  pl.BlockSpec((ATT_WIDTH, tn), lambda i, j: (0, j)),
                  pl.BlockSpec((SG_WIDTH, tn), lambda i, j: (0, j))],
        out_specs=pl.BlockSpec((tm, tn), lambda i, j: (i, j)),
        out_shape=jax.ShapeDtypeStruct((T, D_MODEL), BF16),
        compiler_params=_params(("parallel", "parallel")),
        name="merge",
    )(att, sgo, proj, proj, w_a.astype(BF16), w_b.astype(BF16))


def _route(logits):
    lane = lax.broadcasted_iota(jnp.int32, logits.shape, 1)
    lane_f = lane.astype(F32)
    is_g = lane < N_GROUPS
    gl = jnp.where(is_g, logits, -jnp.inf)
    gmax = jnp.max(gl, axis=-1, keepdims=True)
    grp = jnp.min(jnp.where(gl == gmax, lane_f, float(LANES)), axis=-1, keepdims=True)
    gsum = jnp.sum(jnp.where(is_g, jnp.exp(logits - gmax), 0.0), axis=-1, keepdims=True)
    g_w = 1.0 / gsum
    e_lane = lane - N_GROUPS
    in_grp = jnp.logical_and(
        jnp.logical_and(e_lane >= 0, e_lane < N_EXPERTS),
        (e_lane // EXPERTS_PER_GROUP).astype(F32) == grp)
    el = jnp.where(in_grp, logits, -jnp.inf)
    v1 = jnp.max(el, axis=-1, keepdims=True)
    i1 = jnp.min(jnp.where(jnp.logical_and(in_grp, el == v1), lane_f, float(LANES)),
                 axis=-1, keepdims=True)
    rest = jnp.logical_and(in_grp, lane_f != i1)
    el2 = jnp.where(rest, logits, -jnp.inf)
    v2 = jnp.max(el2, axis=-1, keepdims=True)
    i2 = jnp.min(jnp.where(jnp.logical_and(rest, el2 == v2), lane_f, float(LANES)),
                 axis=-1, keepdims=True)
    e21 = jnp.exp(v2 - v1)
    w1 = g_w / (1.0 + e21)
    w2 = g_w * e21 / (1.0 + e21)
    idx = jnp.where(lane == 0, i1, i2) - float(N_GROUPS)
    wts = jnp.where(lane == 0, w1, jnp.where(lane == 1, w2, 0.0))
    return idx.astype(jnp.int32), wts


def _out_kernel(m_ref, w_ref, x_ref, g_ref, wr_ref, br_ref, h_ref, hn_ref, idx_ref, wt_ref):
    h = x_ref[...] + jnp.dot(m_ref[...], w_ref[...], preferred_element_type=F32)
    h_ref[...] = h
    r = lax.rsqrt(jnp.mean(h * h, axis=-1, keepdims=True) + EPS)
    hn = (h * r) * g_ref[...]
    hn_ref[...] = hn
    logits = jnp.dot(hn.astype(BF16), wr_ref[...], preferred_element_type=F32) + br_ref[...]
    idx, wts = _route(logits)
    idx_ref[...] = idx
    wt_ref[...] = wts


def _out_proj(merged, w_out, x, g2, w_router, b_router, tm=256):
    T, D = x.shape
    row = lambda i: (i, 0)
    const = lambda i: (0, 0)
    return pl.pallas_call(
        _out_kernel,
        grid=(T // tm,),
        in_specs=[pl.BlockSpec((tm, D), row),
                  pl.BlockSpec((D, D), const),
                  pl.BlockSpec((tm, D), row),
                  pl.BlockSpec((1, D), const),
                  pl.BlockSpec((D, LANES), const),
                  pl.BlockSpec((1, LANES), const)],
        out_specs=[pl.BlockSpec((tm, D), row),
                   pl.BlockSpec((tm, D), row),
                   pl.BlockSpec((tm, LANES), row),
                   pl.BlockSpec((tm, LANES), row)],
        out_shape=[jax.ShapeDtypeStruct((T, D), F32),
                   jax.ShapeDtypeStruct((T, D), F32),
                   jax.ShapeDtypeStruct((T, LANES), jnp.int32),
                   jax.ShapeDtypeStruct((T, LANES), F32)],
        compiler_params=_params(("parallel",)),
        name="out_proj_router",
    )(merged, w_out.astype(BF16), x, g2.reshape(1, D), w_router, b_router)


def _expert_kernel(be_ref, eo_ref, ue_ref, rt_ref, nu_ref, hn_hbm, wg_hbm, wu_hbm, wd_hbm, y_ref,
                   xbuf, wgf, wuf, wdf, wgb, wub, wdb, sem, wsem):
    b = pl.program_id(0)
    n_used = nu_ref[0]
    n_exp = nu_ref[1]
    used = b < n_used
    slot = b % 2

    def weight_copies(ordinal, slot_):
        e = ue_ref[ordinal]
        return (pltpu.make_async_copy(wg_hbm.at[e], wgf.at[slot_], wsem.at[slot_]),
                pltpu.make_async_copy(wu_hbm.at[e], wuf.at[slot_], wsem.at[slot_]),
                pltpu.make_async_copy(wd_hbm.at[e], wdf.at[slot_], wsem.at[slot_]))

    def start_weights(ordinal):
        for c in weight_copies(ordinal, ordinal % 2):
            c.start()

    def gather_rows(blk, slot_):
        base = blk * MOE_BLOCK

        def issue(r, c):
            pltpu.make_async_copy(hn_hbm.at[pl.ds(rt_ref[base + r], 1)],
                                  xbuf.at[slot_, pl.ds(r, 1)], sem.at[slot_]).start()
            return c

        lax.fori_loop(0, MOE_BLOCK, issue, 0, unroll=8)

    @pl.when(b == 0)
    def _():
        start_weights(0)

        @pl.when(n_exp > 1)
        def _():
            start_weights(1)

        gather_rows(0, 0)

    @pl.when(b + 1 < n_used)
    def _():
        gather_rows(b + 1, 1 - slot)

    @pl.when(used)
    def _():
        first = jnp.logical_or(b == 0, be_ref[b] != be_ref[jnp.maximum(b - 1, 0)])

        @pl.when(first)
        def _():
            ordinal = eo_ref[b]
            ws = ordinal % 2
            for c in weight_copies(ordinal, ws):
                c.wait()
            wgb[...] = wgf[ws].astype(BF16)
            wub[...] = wuf[ws].astype(BF16)
            wdb[...] = wdf[ws].astype(BF16)

            @pl.when(ordinal + 2 < n_exp)
            def _():
                start_weights(ordinal + 2)

        pltpu.make_async_copy(hn_hbm.at[pl.ds(0, MOE_BLOCK)], xbuf.at[slot], sem.at[slot]).wait()

        x = xbuf[slot].astype(BF16)
        hg = jnp.dot(x, wgb[...], preferred_element_type=F32)
        hu = jnp.dot(x, wub[...], preferred_element_type=F32)
        hdn = (jax.nn.silu(hg) * hu).astype(BF16)
        y_ref[...] = jnp.dot(hdn, wdb[...], preferred_element_type=F32)

    @pl.when(jnp.logical_not(used))
    def _():
        y_ref[...] = jnp.zeros(y_ref.shape, y_ref.dtype)


def _experts(hn, w_gate, w_up, w_down, block_e, block_ord, used_experts, row_tok, n_used):
    T, D = hn.shape
    n_rows = row_tok.shape[0]
    n_blocks = n_rows // MOE_BLOCK
    hbm = pl.BlockSpec(memory_space=pl.ANY)
    grid_spec = pltpu.PrefetchScalarGridSpec(
        num_scalar_prefetch=5,
        grid=(n_blocks,),
        in_specs=[hbm, hbm, hbm, hbm],
        out_specs=pl.BlockSpec((MOE_BLOCK, D), lambda b, *_: (b, 0)),
        scratch_shapes=[pltpu.VMEM((2, MOE_BLOCK, D), F32),
                        pltpu.VMEM((2, D, EXPERT_FF), F32),
                        pltpu.VMEM((2, D, EXPERT_FF), F32),
                        pltpu.VMEM((2, EXPERT_FF, D), F32),
                        pltpu.VMEM((D, EXPERT_FF), BF16),
                        pltpu.VMEM((D, EXPERT_FF), BF16),
                        pltpu.VMEM((EXPERT_FF, D), BF16),
                        pltpu.SemaphoreType.DMA((2,)),
                        pltpu.SemaphoreType.DMA((2,))],
    )
    return pl.pallas_call(
        _expert_kernel,
        grid_spec=grid_spec,
        out_shape=jax.ShapeDtypeStruct((n_rows, D), F32),
        compiler_params=_params(("arbitrary",)),
        name="experts",
    )(block_e, block_ord, used_experts, row_tok, n_used, hn, w_gate, w_up, w_down)


def _combine_kernel(dest_ref, y_hbm, h_ref, wt_ref, o_ref, ybuf, sem, *, tc):
    i = pl.program_id(0)
    slot = i % 2

    def gather_rows(blk, slot_):
        def issue(r, c):
            for k in range(TOP_K):
                src = dest_ref[(blk * tc + r) * TOP_K + k]
                pltpu.make_async_copy(y_hbm.at[pl.ds(src, 1)],
                                      ybuf.at[slot_, k, pl.ds(r, 1)], sem.at[slot_]).start()
            return c

        lax.fori_loop(0, tc, issue, 0, unroll=4)

    @pl.when(i == 0)
    def _():
        gather_rows(0, 0)

    @pl.when(i + 1 < pl.num_programs(0))
    def _():
        gather_rows(i + 1, 1 - slot)

    for k in range(TOP_K):
        pltpu.make_async_copy(y_hbm.at[pl.ds(0, tc)], ybuf.at[slot, k], sem.at[slot]).wait()
    wt = wt_ref[...]
    o_ref[...] = h_ref[...] + (wt[:, 0:1] * ybuf[slot, 0] + wt[:, 1:2] * ybuf[slot, 1])


def _combine(yrows, h, wts, dest, tc=128):
    T, D = h.shape
    row = lambda i, d: (i, 0)
    grid_spec = pltpu.PrefetchScalarGridSpec(
        num_scalar_prefetch=1,
        grid=(T // tc,),
        in_specs=[pl.BlockSpec(memory_space=pl.ANY),
                  pl.BlockSpec((tc, D), row),
                  pl.BlockSpec((tc, LANES), row)],
        out_specs=pl.BlockSpec((tc, D), row),
        scratch_shapes=[pltpu.VMEM((2, TOP_K, tc, D), F32),
                        pltpu.SemaphoreType.DMA((2,))],
    )
    return pl.pallas_call(
        functools.partial(_combine_kernel, tc=tc),
        grid_spec=grid_spec,
        out_shape=jax.ShapeDtypeStruct((T, D), F32),
        compiler_params=_params(("arbitrary",)),
        name="combine",
    )(dest, yrows, h, wts)


def _dispatch(expert):
    T = expert.shape[0]
    TK = T * TOP_K
    flat_e = expert.reshape(TK)
    flat_t = jnp.repeat(jnp.arange(T, dtype=jnp.int32), TOP_K)
    order = jnp.argsort(flat_e)
    se, st = flat_e[order], flat_t[order]
    counts = jnp.bincount(flat_e, length=N_EXPERTS)
    starts = jnp.cumsum(counts) - counts
    padded = (counts + MOE_BLOCK - 1) // MOE_BLOCK * MOE_BLOCK
    pad_ends = jnp.cumsum(padded)
    pad_starts = pad_ends - padded
    dest_sorted = (pad_starts[se] + jnp.arange(TK) - starts[se]).astype(jnp.int32)
    n_rows = TK + N_EXPERTS * MOE_BLOCK
    n_blocks = n_rows // MOE_BLOCK
    row_tok = jnp.zeros((n_rows,), jnp.int32).at[dest_sorted].set(st)
    dest = jnp.zeros((TK,), jnp.int32).at[order].set(dest_sorted)
    block_e = jnp.minimum(
        jnp.searchsorted(pad_ends, jnp.arange(n_blocks) * MOE_BLOCK, side='right'),
        N_EXPERTS - 1).astype(jnp.int32)
    has_rows = counts > 0
    used_experts = jnp.nonzero(has_rows, size=N_EXPERTS, fill_value=0)[0].astype(jnp.int32)
    block_ord = (jnp.cumsum(has_rows) - 1)[block_e].astype(jnp.int32)
    n_used = jnp.stack([pad_ends[-1] // MOE_BLOCK, jnp.sum(has_rows)]).astype(jnp.int32)
    return block_e, block_ord, used_experts, row_tok, n_used, dest


def kernel(x, positions, norm1_g, w_in, q_norm_g, k_norm_g, sink_logits, sg_ln_g, sg_ln_b, sg_w, sg_b, w_branch_att, w_branch_sg, w_out, norm2_g, w_group_router, b_group_router, w_expert_router, b_expert_router, w_gate, w_up, w_down):
    B, S, D = x.shape
    T = B * S
    h = x.reshape(T, D)
    pos = positions.reshape(T)
    for l in range(norm1_g.shape[0]):
        xn = _rmsnorm(h, norm1_g[l])
        proj = _in_proj(xn, w_in[l])
        q, k, v = _qkv_prep(proj, pos, q_norm_g[l], k_norm_g[l])
        att = _attention(q, k, v, sink_logits[l], B)
        sgo = _spatial_gating(proj, sg_ln_g[l], sg_ln_b[l], sg_w[l], sg_b[l])
        merged = _merge(att, sgo, proj, w_branch_att[l], w_branch_sg[l])
        pad = LANES - N_GROUPS - N_EXPERTS
        w_router = jnp.concatenate(
            [w_group_router[l], w_expert_router[l], jnp.zeros((D, pad), F32)], axis=1).astype(BF16)
        b_router = jnp.concatenate(
            [b_group_router[l], b_expert_router[l], jnp.zeros((pad,), F32)]).reshape(1, LANES)
        h, hn, idx, wts = _out_proj(merged, w_out[l], h, norm2_g[l], w_router, b_router)
        block_e, block_ord, used_experts, row_tok, n_used, dest = _dispatch(idx[:, :TOP_K])
        yrows = _experts(hn, w_gate[l], w_up[l], w_down[l],
                         block_e, block_ord, used_experts, row_tok, n_used)
        h = _combine(yrows, h, wts, dest)
    return h.reshape(B, S, D)
```

```python
import functools

import jax
import jax.numpy as jnp
from jax import lax
from jax.experimental import pallas as pl
from jax.experimental.pallas import tpu as pltpu

F32 = jnp.float32
BF16 = jnp.bfloat16

D_MODEL = 2048
HEAD_DIM = 64
ATT_WIDTH = D_MODEL // 2
ATT_HEADS = ATT_WIDTH // HEAD_DIM
ATT_KV_HEADS = ATT_HEADS // 4
Q_PER_KV = ATT_HEADS // ATT_KV_HEADS
KV_WIDTH = ATT_KV_HEADS * HEAD_DIM
WINDOW = 128
ATT_BLOCK = 128
ROPE_DIM = HEAD_DIM // 4
ROPE_HALF = ROPE_DIM // 2
ROPE_THETA = 500000.0
SG_WIDTH = D_MODEL // 2
SG_GROUP_DIM = 128
SG_GROUPS = SG_WIDTH // SG_GROUP_DIM
SG_CHUNK = 128
OFF_Q = 0
OFF_K = OFF_Q + ATT_WIDTH
OFF_V = OFF_K + KV_WIDTH
OFF_U = OFF_V + KV_WIDTH
OFF_S = OFF_U + SG_WIDTH
OFF_G = OFF_S + SG_WIDTH
IN_COLS = OFF_G + 2 * D_MODEL
N_GROUPS = 8
EXPERTS_PER_GROUP = 8
N_EXPERTS = N_GROUPS * EXPERTS_PER_GROUP
TOP_K = 2
EXPERT_FF = D_MODEL // 4
MOE_BLOCK = 128
EPS = 1e-6
NEG_INF = -1e30

LANES = 128
VMEM_LIMIT = 56 * 1024 * 1024


def _params(sem, vmem=VMEM_LIMIT):
    return pltpu.CompilerParams(dimension_semantics=sem, vmem_limit_bytes=vmem)


def _rmsnorm_kernel(x_ref, g_ref, o_ref):
    x = x_ref[...]
    r = lax.rsqrt(jnp.mean(x * x, axis=-1, keepdims=True) + EPS)
    o_ref[...] = ((x * r) * g_ref[...]).astype(o_ref.dtype)


def _rmsnorm(x, g, tm=512):
    T, D = x.shape
    return pl.pallas_call(
        _rmsnorm_kernel,
        grid=(T // tm,),
        in_specs=[pl.BlockSpec((tm, D), lambda i: (i, 0)),
                  pl.BlockSpec((1, D), lambda i: (0, 0))],
        out_specs=pl.BlockSpec((tm, D), lambda i: (i, 0)),
        out_shape=jax.ShapeDtypeStruct((T, D), BF16),
        compiler_params=_params(("parallel",)),
        name="norm1",
    )(x, g.reshape(1, D))


def _proj_kernel(x_ref, w_ref, o_ref, wbf_ref):
    @pl.when(pl.program_id(1) == 0)
    def _():
        wbf_ref[...] = w_ref[...].astype(BF16)

    o_ref[...] = jnp.dot(x_ref[...], wbf_ref[...], preferred_element_type=F32)


def _in_proj(xn, w, tm=512, tn=1280):
    T, D = xn.shape
    N = w.shape[1]
    return pl.pallas_call(
        _proj_kernel,
        grid=(N // tn, T // tm),
        in_specs=[pl.BlockSpec((tm, D), lambda j, i: (i, 0)),
                  pl.BlockSpec((D, tn), lambda j, i: (0, j))],
        out_specs=pl.BlockSpec((tm, tn), lambda j, i: (i, j)),
        out_shape=jax.ShapeDtypeStruct((T, N), F32),
        scratch_shapes=[pltpu.VMEM((D, tn), BF16)],
        compiler_params=_params(("arbitrary", "arbitrary")),
        name="in_proj",
    )(xn, w)


def _qkv_prep_kernel(p_ref, pos_ref, invf_ref, sign_ref, gq_ref, gk_ref, q_ref, k_ref, v_ref):
    pos = pos_ref[...].astype(F32)
    ang = pos * invf_ref[...]
    cos = jnp.cos(ang)
    sin = jnp.sin(ang) * sign_ref[...]
    lane = lax.broadcasted_iota(jnp.int32, ang.shape, 1)
    low_head = lane < HEAD_DIM
    first_half = (lane % HEAD_DIM) < ROPE_HALF

    def norm_rope(x, g):
        x2 = x * x
        s_lo = jnp.sum(jnp.where(low_head, x2, 0.0), axis=-1, keepdims=True)
        s_hi = jnp.sum(jnp.where(low_head, 0.0, x2), axis=-1, keepdims=True)
        ssq = jnp.where(low_head, s_lo, s_hi)
        xn = (x * lax.rsqrt(ssq * (1.0 / HEAD_DIM) + EPS)) * g
        partner = jnp.where(first_half,
                            pltpu.roll(xn, LANES - ROPE_HALF, 1),
                            pltpu.roll(xn, ROPE_HALF, 1))
        return xn * cos + partner * sin

    for c in range(ATT_WIDTH // LANES):
        x = p_ref[:, OFF_Q + c * LANES:OFF_Q + (c + 1) * LANES]
        q_ref[:, c * LANES:(c + 1) * LANES] = (
            norm_rope(x, gq_ref[...]) * (HEAD_DIM ** -0.5)).astype(q_ref.dtype)
    for c in range(KV_WIDTH // LANES):
        x = p_ref[:, OFF_K + c * LANES:OFF_K + (c + 1) * LANES]
        k_ref[:, c * LANES:(c + 1) * LANES] = norm_rope(x, gk_ref[...]).astype(k_ref.dtype)
    v_ref[...] = p_ref[:, OFF_V:OFF_V + KV_WIDTH].astype(v_ref.dtype)


def _qkv_prep(proj, positions, q_g, k_g, tq=256):
    T = proj.shape[0]
    width = OFF_U
    lane = jnp.arange(LANES) % HEAD_DIM
    inv = ROPE_THETA ** (-jnp.arange(0, ROPE_DIM, 2, dtype=F32) / ROPE_DIM)
    invf = jnp.where(lane < ROPE_DIM, inv[lane % ROPE_HALF], 0.0).reshape(1, LANES)
    sign = jnp.where(lane < ROPE_HALF, -1.0, jnp.where(lane < ROPE_DIM, 1.0, 0.0))
    sign = sign.astype(F32).reshape(1, LANES)
    gq = jnp.tile(q_g, LANES // HEAD_DIM).reshape(1, LANES)
    gk = jnp.tile(k_g, LANES // HEAD_DIM).reshape(1, LANES)
    row = lambda i: (i, 0)
    const = lambda i: (0, 0)
    return pl.pallas_call(
        _qkv_prep_kernel,
        grid=(T // tq,),
        in_specs=[pl.BlockSpec((tq, width), row),
                  pl.BlockSpec((tq, 1), row),
                  pl.BlockSpec((1, LANES), const),
                  pl.BlockSpec((1, LANES), const),
                  pl.BlockSpec((1, LANES), const),
                  pl.BlockSpec((1, LANES), const)],
        out_specs=[pl.BlockSpec((tq, ATT_WIDTH), row),
                   pl.BlockSpec((tq, KV_WIDTH), row),
                   pl.BlockSpec((tq, KV_WIDTH), row)],
        out_shape=[jax.ShapeDtypeStruct((T, ATT_WIDTH), BF16),
                   jax.ShapeDtypeStruct((T, KV_WIDTH), BF16),
                   jax.ShapeDtypeStruct((T, KV_WIDTH), BF16)],
        compiler_params=_params(("parallel",)),
        name="qkv_prep",
    )(proj, positions.reshape(T, 1), invf, sign, gq, gk)


def _attn_kernel(sink_ref, q_ref, kp_ref, kc_ref, kn_ref, vp_ref, vc_ref, vn_ref, o_ref, *, nb):
    n = pl.program_id(1)
    rows = Q_PER_KV * ATT_BLOCK
    keys = 3 * ATT_BLOCK
    qi = lax.broadcasted_iota(jnp.int32, (rows, keys), 0) % ATT_BLOCK
    kj = lax.broadcasted_iota(jnp.int32, (rows, keys), 1)
    key_pos = kj + (n - 1) * ATT_BLOCK
    in_seq = jnp.logical_and(key_pos >= 0, key_pos < nb * ATT_BLOCK)
    mask = jnp.logical_and(jnp.abs(kj - ATT_BLOCK - qi) <= WINDOW, in_seq)
    row_head = lax.broadcasted_iota(jnp.int32, (rows, 1), 0) // ATT_BLOCK

    for kvh in range(ATT_KV_HEADS):
        cols = slice(kvh * HEAD_DIM, (kvh + 1) * HEAD_DIM)
        k = jnp.concatenate([kp_ref[:, cols], kc_ref[:, cols], kn_ref[:, cols]], axis=0)
        v = jnp.concatenate([vp_ref[:, cols], vc_ref[:, cols], vn_ref[:, cols]], axis=0)
        q = jnp.concatenate(
            [q_ref[:, (kvh * Q_PER_KV + g) * HEAD_DIM:(kvh * Q_PER_KV + g + 1) * HEAD_DIM]
             for g in range(Q_PER_KV)], axis=0)
        sink = jnp.zeros((rows, 1), F32)
        for g in range(Q_PER_KV):
            sink = jnp.where(row_head == g, sink_ref[kvh * Q_PER_KV + g], sink)
        s = lax.dot_general(q, k, (((1,), (1,)), ((), ())), preferred_element_type=F32)
        s = jnp.where(mask, s, NEG_INF)
        m = jnp.maximum(jnp.max(s, axis=-1, keepdims=True), sink)
        e = jnp.exp(s - m)
        denom = jnp.sum(e, axis=-1, keepdims=True) + jnp.exp(sink - m)
        p = (e / denom).astype(BF16)
        o = jnp.dot(p, v, preferred_element_type=F32)
        for g in range(Q_PER_KV):
            h = kvh * Q_PER_KV + g
            o_ref[:, h * HEAD_DIM:(h + 1) * HEAD_DIM] = (
                o[g * ATT_BLOCK:(g + 1) * ATT_BLOCK].astype(o_ref.dtype))


def _attention(q, k, v, sink, batch):
    T = q.shape[0]
    nb = T // batch // ATT_BLOCK
    cur = lambda b, n: (b * nb + n, 0)
    prev = lambda b, n: (b * nb + jnp.maximum(n - 1, 0), 0)
    nxt = lambda b, n: (b * nb + jnp.minimum(n + 1, nb - 1), 0)
    kv = lambda im: pl.BlockSpec((ATT_BLOCK, KV_WIDTH), im)
    return pl.pallas_call(
        functools.partial(_attn_kernel, nb=nb),
        grid=(batch, nb),
        in_specs=[pl.BlockSpec(memory_space=pltpu.SMEM),
                  pl.BlockSpec((ATT_BLOCK, ATT_WIDTH), cur),
                  kv(prev), kv(cur), kv(nxt), kv(prev), kv(cur), kv(nxt)],
        out_specs=pl.BlockSpec((ATT_BLOCK, ATT_WIDTH), cur),
        out_shape=jax.ShapeDtypeStruct((T, ATT_WIDTH), BF16),
        compiler_params=_params(("parallel", "parallel")),
        name="window_attn",
    )(sink, q, k, k, k, v, v, v)


def _sg_kernel(u_ref, s_ref, lng_ref, lnb_ref, w_ref, b_ref, o_ref):
    groups = w_ref.shape[0]
    for gi in range(groups):
        cols = slice(gi * SG_GROUP_DIM, (gi + 1) * SG_GROUP_DIM)
        s = jax.nn.gelu(s_ref[:, cols])
        mu = jnp.mean(s, axis=-1, keepdims=True)
        sc = s - mu
        var = jnp.mean(sc * sc, axis=-1, keepdims=True)
        sn = (sc * lax.rsqrt(var + EPS)) * lng_ref[0, gi:gi + 1, :] + lnb_ref[0, gi:gi + 1, :]
        mixed = jnp.dot(w_ref[gi], sn.astype(BF16), preferred_element_type=F32)
        mixed = mixed + b_ref[0, :, gi:gi + 1]
        o_ref[:, cols] = (jax.nn.gelu(u_ref[:, cols]) * mixed).astype(o_ref.dtype)


def _spatial_gating(proj, ln_g, ln_b, w_s, b_s, halves=2):
    T = proj.shape[0]
    half_w = SG_WIDTH // halves
    gph = SG_GROUPS // halves
    u0, s0 = OFF_U // half_w, OFF_S // half_w
    return pl.pallas_call(
        _sg_kernel,
        grid=(T // SG_CHUNK, halves),
        in_specs=[pl.BlockSpec((SG_CHUNK, half_w), lambda i, j: (i, u0 + j)),
                  pl.BlockSpec((SG_CHUNK, half_w), lambda i, j: (i, s0 + j)),
                  pl.BlockSpec((1, gph, SG_GROUP_DIM), lambda i, j: (j, 0, 0)),
                  pl.BlockSpec((1, gph, SG_GROUP_DIM), lambda i, j: (j, 0, 0)),
                  pl.BlockSpec((gph, SG_CHUNK, SG_CHUNK), lambda i, j: (j, 0, 0)),
                  pl.BlockSpec((1, SG_CHUNK, gph), lambda i, j: (j, 0, 0))],
        out_specs=pl.BlockSpec((SG_CHUNK, half_w), lambda i, j: (i, j)),
        out_shape=jax.ShapeDtypeStruct((T, SG_WIDTH), BF16),
        compiler_params=_params(("parallel", "parallel")),
        name="spatial_gating",
    )(proj, proj,
      ln_g.reshape(halves, gph, SG_GROUP_DIM), ln_b.reshape(halves, gph, SG_GROUP_DIM),
      w_s.astype(BF16),
      b_s.reshape(halves, gph, SG_CHUNK).transpose(0, 2, 1))


def _merge_kernel(att_ref, sgo_ref, ga_ref, gb_ref, wa_ref, wb_ref, o_ref):
    a = jnp.dot(att_ref[...], wa_ref[...], preferred_element_type=F32)
    b = jnp.dot(sgo_ref[...], wb_ref[...], preferred_element_type=F32)
    m = jax.nn.sigmoid(ga_ref[...]) * a + jax.nn.sigmoid(gb_ref[...]) * b
    o_ref[...] = m.astype(o_ref.dtype)


def _merge(att, sgo, proj, w_a, w_b, tm=512, tn=512):
    T = att.shape[0]
    ga0 = OFF_G // tn
    gb0 = (OFF_G + D_MODEL) // tn
    return pl.pallas_call(
        _merge_kernel,
        grid=(T // tm, D_MODEL // tn),
        in_specs=[pl.BlockSpec((tm, ATT_WIDTH), lambda i, j: (i, 0)),
                  pl.BlockSpec((tm, SG_WIDTH), lambda i, j: (i, 0)),
                  pl.BlockSpec((tm, tn), lambda i, j: (i, ga0 + j)),
                  pl.BlockSpec((tm, tn), lambda i, j: (i, gb0 + j)),
                  pl.BlockSpec((ATT_WIDTH, tn), lambda i, j: (0, j)),
                  pl.BlockSpec((SG_WIDTH, tn), lambda i, j: (0, j))],
        out_specs=pl.BlockSpec((tm, tn), lambda i, j: (i, j)),
        out_shape=jax.ShapeDtypeStruct((T, D_MODEL), BF16),
        compiler_params=_params(("parallel", "parallel")),
        name="merge",
    )(att, sgo, proj, proj, w_a.astype(BF16), w_b.astype(BF16))


def _route(logits):
    lane = lax.broadcasted_iota(jnp.int32, logits.shape, 1)
    lane_f = lane.astype(F32)
    is_g = lane < N_GROUPS
    gl = jnp.where(is_g, logits, -jnp.inf)
    gmax = jnp.max(gl, axis=-1, keepdims=True)
    grp = jnp.min(jnp.where(gl == gmax, lane_f, float(LANES)), axis=-1, keepdims=True)
    gsum = jnp.sum(jnp.where(is_g, jnp.exp(logits - gmax), 0.0), axis=-1, keepdims=True)
    g_w = 1.0 / gsum
    e_lane = lane - N_GROUPS
    in_grp = jnp.logical_and(
        jnp.logical_and(e_lane >= 0, e_lane < N_EXPERTS),
        (e_lane // EXPERTS_PER_GROUP).astype(F32) == grp)
    el = jnp.where(in_grp, logits, -jnp.inf)
    v1 = jnp.max(el, axis=-1, keepdims=True)
    i1 = jnp.min(jnp.where(jnp.logical_and(in_grp, el == v1), lane_f, float(LANES)),
                 axis=-1, keepdims=True)
    rest = jnp.logical_and(in_grp, lane_f != i1)
    el2 = jnp.where(rest, logits, -jnp.inf)
    v2 = jnp.max(el2, axis=-1, keepdims=True)
    i2 = jnp.min(jnp.where(jnp.logical_and(rest, el2 == v2), lane_f, float(LANES)),
                 axis=-1, keepdims=True)
    e21 = jnp.exp(v2 - v1)
    w1 = g_w / (1.0 + e21)
    w2 = g_w * e21 / (1.0 + e21)
    idx = jnp.where(lane == 0, i1, i2) - float(N_GROUPS)
    wts = jnp.where(lane == 0, w1, jnp.where(lane == 1, w2, 0.0))
    return idx.astype(jnp.int32), wts


def _out_kernel(m_ref, w_ref, x_ref, g_ref, wr_ref, br_ref, h_ref, hn_ref, idx_ref, wt_ref):
    h = x_ref[...] + jnp.dot(m_ref[...], w_ref[...], preferred_element_type=F32)
    h_ref[...] = h
    r = lax.rsqrt(jnp.mean(h * h, axis=-1, keepdims=True) + EPS)
    hn = (h * r) * g_ref[...]
    hn_ref[...] = hn
    logits = jnp.dot(hn.astype(BF16), wr_ref[...], preferred_element_type=F32) + br_ref[...]
    idx, wts = _route(logits)
    idx_ref[...] = idx
    wt_ref[...] = wts


def _out_proj(merged, w_out, x, g2, w_router, b_router, tm=256):
    T, D = x.shape
    row = lambda i: (i, 0)
    const = lambda i: (0, 0)
    return pl.pallas_call(
        _out_kernel,
        grid=(T // tm,),
        in_specs=[pl.BlockSpec((tm, D), row),
                  pl.BlockSpec((D, D), const),
                  pl.BlockSpec((tm, D), row),
                  pl.BlockSpec((1, D), const),
                  pl.BlockSpec((D, LANES), const),
                  pl.BlockSpec((1, LANES), const)],
        out_specs=[pl.BlockSpec((tm, D), row),
                   pl.BlockSpec((tm, D), row),
                   pl.BlockSpec((tm, LANES), row),
                   pl.BlockSpec((tm, LANES), row)],
        out_shape=[jax.ShapeDtypeStruct((T, D), F32),
                   jax.ShapeDtypeStruct((T, D), F32),
                   jax.ShapeDtypeStruct((T, LANES), jnp.int32),
                   jax.ShapeDtypeStruct((T, LANES), F32)],
        compiler_params=_params(("parallel",)),
        name="out_proj_router",
    )(merged, w_out.astype(BF16), x, g2.reshape(1, D), w_router, b_router)


def _expert_kernel(be_ref, eo_ref, ue_ref, rt_ref, nu_ref, hn_hbm, wg_hbm, wu_hbm, wd_hbm, y_ref,
                   xbuf, wgf, wuf, wdf, wgb, wub, wdb, sem, wsem):
    b = pl.program_id(0)
    n_used = nu_ref[0]
    n_exp = nu_ref[1]
    used = b < n_used
    slot = b % 2

    def weight_copies(ordinal, slot_):
        e = ue_ref[ordinal]
        return (pltpu.make_async_copy(wg_hbm.at[e], wgf.at[slot_], wsem.at[slot_]),
                pltpu.make_async_copy(wu_hbm.at[e], wuf.at[slot_], wsem.at[slot_]),
                pltpu.make_async_copy(wd_hbm.at[e], wdf.at[slot_], wsem.at[slot_]))

    def start_weights(ordinal):
        for c in weight_copies(ordinal, ordinal % 2):
            c.start(priority=1)

    def gather_rows(blk, slot_):
        base = blk * MOE_BLOCK

        def issue(r, c):
            pltpu.make_async_copy(hn_hbm.at[pl.ds(rt_ref[base + r], 1)],
                                  xbuf.at[slot_, pl.ds(r, 1)], sem.at[slot_]).start()
            return c

        lax.fori_loop(0, MOE_BLOCK, issue, 0, unroll=8)

    @pl.when(b == 0)
    def _():
        start_weights(0)

        @pl.when(n_exp > 1)
        def _():
            start_weights(1)

        gather_rows(0, 0)

    @pl.when(b + 1 < n_used)
    def _():
        gather_rows(b + 1, 1 - slot)

    @pl.when(used)
    def _():
        first = jnp.logical_or(b == 0, be_ref[b] != be_ref[jnp.maximum(b - 1, 0)])

        @pl.when(first)
        def _():
            ordinal = eo_ref[b]
            ws = ordinal % 2
            for c in weight_copies(ordinal, ws):
                c.wait()
            wgb[...] = wgf[ws].astype(BF16)
            wub[...] = wuf[ws].astype(BF16)
            wdb[...] = wdf[ws].astype(BF16)

            @pl.when(ordinal + 2 < n_exp)
            def _():
                start_weights(ordinal + 2)

        pltpu.make_async_copy(hn_hbm.at[pl.ds(0, MOE_BLOCK)], xbuf.at[slot], sem.at[slot]).wait()

        x = xbuf[slot].astype(BF16)
        hg = jnp.dot(x, wgb[...], preferred_element_type=F32)
        hu = jnp.dot(x, wub[...], preferred_element_type=F32)
        hdn = (jax.nn.silu(hg) * hu).astype(BF16)
        y_ref[...] = jnp.dot(hdn, wdb[...], preferred_element_type=F32)

    @pl.when(jnp.logical_not(used))
    def _():
        y_ref[...] = jnp.zeros(y_ref.shape, y_ref.dtype)


def _experts(hn, w_gate, w_up, w_down, block_e, block_ord, used_experts, row_tok, n_used):
    T, D = hn.shape
    n_rows = row_tok.shape[0]
    n_blocks = n_rows // MOE_BLOCK
    hbm = pl.BlockSpec(memory_space=pl.ANY)
    grid_spec = pltpu.PrefetchScalarGridSpec(
        num_scalar_prefetch=5,
        grid=(n_blocks,),
        in_specs=[hbm, hbm, hbm, hbm],
        out_specs=pl.BlockSpec((MOE_BLOCK, D), lambda b, *_: (b, 0)),
        scratch_shapes=[pltpu.VMEM((2, MOE_BLOCK, D), F32),
                        pltpu.VMEM((2, D, EXPERT_FF), F32),
                        pltpu.VMEM((2, D, EXPERT_FF), F32),
                        pltpu.VMEM((2, EXPERT_FF, D), F32),
                        pltpu.VMEM((D, EXPERT_FF), BF16),
                        pltpu.VMEM((D, EXPERT_FF), BF16),
                        pltpu.VMEM((EXPERT_FF, D), BF16),
                        pltpu.SemaphoreType.DMA((2,)),
                        pltpu.SemaphoreType.DMA((2,))],
    )
    return pl.pallas_call(
        _expert_kernel,
        grid_spec=grid_spec,
        out_shape=jax.ShapeDtypeStruct((n_rows, D), F32),
        compiler_params=_params(("arbitrary",)),
        name="experts",
    )(block_e, block_ord, used_experts, row_tok, n_used, hn, w_gate, w_up, w_down)


def _combine_kernel(dest_ref, y_hbm, h_ref, wt_ref, o_ref, ybuf, sem, *, tc):
    i = pl.program_id(0)
    slot = i % 2

    def gather_rows(blk, slot_):
        def issue(r, c):
            for k in range(TOP_K):
                src = dest_ref[(blk * tc + r) * TOP_K + k]
                pltpu.make_async_copy(y_hbm.at[pl.ds(src, 1)],
                                      ybuf.at[slot_, k, pl.ds(r, 1)],
                                      sem.at[slot_]).start(priority=k % 2)
            return c

        lax.fori_loop(0, tc, issue, 0, unroll=4)

    @pl.when(i == 0)
    def _():
        gather_rows(0, 0)

    @pl.when(i + 1 < pl.num_programs(0))
    def _():
        gather_rows(i + 1, 1 - slot)

    for k in range(TOP_K):
        pltpu.make_async_copy(y_hbm.at[pl.ds(0, tc)], ybuf.at[slot, k], sem.at[slot]).wait()
    wt = wt_ref[...]
    o_ref[...] = h_ref[...] + (wt[:, 0:1] * ybuf[slot, 0] + wt[:, 1:2] * ybuf[slot, 1])


def _combine(yrows, h, wts, dest, tc=128):
    T, D = h.shape
    row = lambda i, d: (i, 0)
    grid_spec = pltpu.PrefetchScalarGridSpec(
        num_scalar_prefetch=1,
        grid=(T // tc,),
        in_specs=[pl.BlockSpec(memory_space=pl.ANY),
                  pl.BlockSpec((tc, D), row),
                  pl.BlockSpec((tc, LANES), row)],
        out_specs=pl.BlockSpec((tc, D), row),
        scratch_shapes=[pltpu.VMEM((2, TOP_K, tc, D), F32),
                        pltpu.SemaphoreType.DMA((2,))],
    )
    return pl.pallas_call(
        functools.partial(_combine_kernel, tc=tc),
        grid_spec=grid_spec,
        out_shape=jax.ShapeDtypeStruct((T, D), F32),
        compiler_params=_params(("arbitrary",)),
        name="combine",
    )(dest, yrows, h, wts)


def _dispatch(expert):
    T = expert.shape[0]
    TK = T * TOP_K
    flat_e = expert.reshape(TK)
    flat_t = jnp.repeat(jnp.arange(T, dtype=jnp.int32), TOP_K)
    order = jnp.argsort(flat_e)
    se, st = flat_e[order], flat_t[order]
    counts = jnp.bincount(flat_e, length=N_EXPERTS)
    starts = jnp.cumsum(counts) - counts
    padded = (counts + MOE_BLOCK - 1) // MOE_BLOCK * MOE_BLOCK
    pad_ends = jnp.cumsum(padded)
    pad_starts = pad_ends - padded
    dest_sorted = (pad_starts[se] + jnp.arange(TK) - starts[se]).astype(jnp.int32)
    n_rows = TK + N_EXPERTS * MOE_BLOCK
    n_blocks = n_rows // MOE_BLOCK
    row_tok = jnp.zeros((n_rows,), jnp.int32).at[dest_sorted].set(st)
    dest = jnp.zeros((TK,), jnp.int32).at[order].set(dest_sorted)
    block_e = jnp.minimum(
        jnp.searchsorted(pad_ends, jnp.arange(n_blocks) * MOE_BLOCK, side='right'),
        N_EXPERTS - 1).astype(jnp.int32)
    has_rows = counts > 0
    used_experts = jnp.nonzero(has_rows, size=N_EXPERTS, fill_value=0)[0].astype(jnp.int32)
    block_ord = (jnp.cumsum(has_rows) - 1)[block_e].astype(jnp.int32)
    n_used = jnp.stack([pad_ends[-1] // MOE_BLOCK, jnp.sum(has_rows)]).astype(jnp.int32)
    return block_e, block_ord, used_experts, row_tok, n_used, dest


def kernel(x, positions, norm1_g, w_in, q_norm_g, k_norm_g, sink_logits, sg_ln_g, sg_ln_b, sg_w, sg_b, w_branch_att, w_branch_sg, w_out, norm2_g, w_group_router, b_group_router, w_expert_router, b_expert_router, w_gate, w_up, w_down):
    B, S, D = x.shape
    T = B * S
    h = x.reshape(T, D)
    pos = positions.reshape(T)
    for l in range(norm1_g.shape[0]):
        xn = _rmsnorm(h, norm1_g[l])
        proj = _in_proj(xn, w_in[l])
        q, k, v = _qkv_prep(proj, pos, q_norm_g[l], k_norm_g[l])
        att = _attention(q, k, v, sink_logits[l], B)
        sgo = _spatial_gating(proj, sg_ln_g[l], sg_ln_b[l], sg_w[l], sg_b[l])
        merged = _merge(att, sgo, proj, w_branch_att[l], w_branch_sg[l])
        pad = LANES - N_GROUPS - N_EXPERTS
        w_router = jnp.concatenate(
            [w_group_router[l], w_expert_router[l], jnp.zeros((D, pad), F32)], axis=1).astype(BF16)
        b_router = jnp.concatenate(
            [b_group_router[l], b_expert_router[l], jnp.zeros((pad,), F32)]).reshape(1, LANES)
        h, hn, idx, wts = _out_proj(merged, w_out[l], h, norm2_g[l], w_router, b_router)
        block_e, block_ord, used_experts, row_tok, n_used, dest = _dispatch(idx[:, :TOP_K])
        yrows = _experts(hn, w_gate[l], w_up[l], w_down[l],
                         block_e, block_ord, used_experts, row_tok, n_used)
        h = _combine(yrows, h, wts, dest)
    return h.reshape(B, S, D)
```

```python
import functools

import jax
import jax.numpy as jnp
from jax import lax
from jax.experimental import pallas as pl
from jax.experimental.pallas import tpu as pltpu

F32 = jnp.float32
BF16 = jnp.bfloat16

D_MODEL = 2048
HEAD_DIM = 64
ATT_WIDTH = D_MODEL // 2
ATT_HEADS = ATT_WIDTH // HEAD_DIM
ATT_KV_HEADS = ATT_HEADS // 4
Q_PER_KV = ATT_HEADS // ATT_KV_HEADS
KV_WIDTH = ATT_KV_HEADS * HEAD_DIM
WINDOW = 128
ATT_BLOCK = 128
ROPE_DIM = HEAD_DIM // 4
ROPE_HALF = ROPE_DIM // 2
ROPE_THETA = 500000.0
SG_WIDTH = D_MODEL // 2
SG_GROUP_DIM = 128
SG_GROUPS = SG_WIDTH // SG_GROUP_DIM
SG_CHUNK = 128
OFF_Q = 0
OFF_K = OFF_Q + ATT_WIDTH
OFF_V = OFF_K + KV_WIDTH
OFF_U = OFF_V + KV_WIDTH
OFF_S = OFF_U + SG_WIDTH
OFF_G = OFF_S + SG_WIDTH
IN_COLS = OFF_G + 2 * D_MODEL
N_GROUPS = 8
EXPERTS_PER_GROUP = 8
N_EXPERTS = N_GROUPS * EXPERTS_PER_GROUP
TOP_K = 2
EXPERT_FF = D_MODEL // 4
MOE_BLOCK = 128
EPS = 1e-6
NEG_INF = -1e30

LANES = 128
VMEM_LIMIT = 56 * 1024 * 1024


def _params(sem, vmem=VMEM_LIMIT):
    return pltpu.CompilerParams(dimension_semantics=sem, vmem_limit_bytes=vmem)


def _rmsnorm_kernel(x_ref, g_ref, o_ref):
    x = x_ref[...]
    r = lax.rsqrt(jnp.mean(x * x, axis=-1, keepdims=True) + EPS)
    o_ref[...] = ((x * r) * g_ref[...]).astype(o_ref.dtype)


def _rmsnorm(x, g, tm=512):
    T, D = x.shape
    return pl.pallas_call(
        _rmsnorm_kernel,
        grid=(T // tm,),
        in_specs=[pl.BlockSpec((tm, D), lambda i: (i, 0)),
                  pl.BlockSpec((1, D), lambda i: (0, 0))],
        out_specs=pl.BlockSpec((tm, D), lambda i: (i, 0)),
        out_shape=jax.ShapeDtypeStruct((T, D), BF16),
        compiler_params=_params(("parallel",)),
        name="norm1",
    )(x, g.reshape(1, D))


def _proj_kernel(x_ref, w_ref, o_ref, wbf_ref):
    @pl.when(pl.program_id(1) == 0)
    def _():
        wbf_ref[...] = w_ref[...].astype(BF16)

    o_ref[...] = jnp.dot(x_ref[...], wbf_ref[...], preferred_element_type=F32)


def _in_proj(xn, w, tm=512, tn=1280):
    T, D = xn.shape
    N = w.shape[1]
    return pl.pallas_call(
        _proj_kernel,
        grid=(N // tn, T // tm),
        in_specs=[pl.BlockSpec((tm, D), lambda j, i: (i, 0)),
                  pl.BlockSpec((D, tn), lambda j, i: (0, j))],
        out_specs=pl.BlockSpec((tm, tn), lambda j, i: (i, j)),
        out_shape=jax.ShapeDtypeStruct((T, N), F32),
        scratch_shapes=[pltpu.VMEM((D, tn), BF16)],
        compiler_params=_params(("arbitrary", "arbitrary")),
        name="in_proj",
    )(xn, w)


def _qkv_prep_kernel(p_ref, pos_ref, invf_ref, sign_ref, gq_ref, gk_ref, q_ref, k_ref, v_ref):
    pos = pos_ref[...].astype(F32)
    ang = pos * invf_ref[...]
    cos = jnp.cos(ang)
    sin = jnp.sin(ang) * sign_ref[...]
    lane = lax.broadcasted_iota(jnp.int32, ang.shape, 1)
    low_head = lane < HEAD_DIM
    first_half = (lane % HEAD_DIM) < ROPE_HALF

    def norm_rope(x, g):
        x2 = x * x
        s_lo = jnp.sum(jnp.where(low_head, x2, 0.0), axis=-1, keepdims=True)
        s_hi = jnp.sum(jnp.where(low_head, 0.0, x2), axis=-1, keepdims=True)
        ssq = jnp.where(low_head, s_lo, s_hi)
        xn = (x * lax.rsqrt(ssq * (1.0 / HEAD_DIM) + EPS)) * g
        partner = jnp.where(first_half,
                            pltpu.roll(xn, LANES - ROPE_HALF, 1),
                            pltpu.roll(xn, ROPE_HALF, 1))
        return xn * cos + partner * sin

    for c in range(ATT_WIDTH // LANES):
        x = p_ref[:, OFF_Q + c * LANES:OFF_Q + (c + 1) * LANES]
        q_ref[:, c * LANES:(c + 1) * LANES] = (
            norm_rope(x, gq_ref[...]) * (HEAD_DIM ** -0.5)).astype(q_ref.dtype)
    for c in range(KV_WIDTH // LANES):
        x = p_ref[:, OFF_K + c * LANES:OFF_K + (c + 1) * LANES]
        k_ref[:, c * LANES:(c + 1) * LANES] = norm_rope(x, gk_ref[...]).astype(k_ref.dtype)
    v_ref[...] = p_ref[:, OFF_V:OFF_V + KV_WIDTH].astype(v_ref.dtype)


def _qkv_prep(proj, positions, q_g, k_g, tq=256):
    T = proj.shape[0]
    width = OFF_U
    lane = jnp.arange(LANES) % HEAD_DIM
    inv = ROPE_THETA ** (-jnp.arange(0, ROPE_DIM, 2, dtype=F32) / ROPE_DIM)
    invf = jnp.where(lane < ROPE_DIM, inv[lane % ROPE_HALF], 0.0).reshape(1, LANES)
    sign = jnp.where(lane < ROPE_HALF, -1.0, jnp.where(lane < ROPE_DIM, 1.0, 0.0))
    sign = sign.astype(F32).reshape(1, LANES)
    gq = jnp.tile(q_g, LANES // HEAD_DIM).reshape(1, LANES)
    gk = jnp.tile(k_g, LANES // HEAD_DIM).reshape(1, LANES)
    row = lambda i: (i, 0)
    const = lambda i: (0, 0)
    return pl.pallas_call(
        _qkv_prep_kernel,
        grid=(T // tq,),
        in_specs=[pl.BlockSpec((tq, width), row),
                  pl.BlockSpec((tq, 1), row),
                  pl.BlockSpec((1, LANES), const),
                  pl.BlockSpec((1, LANES), const),
                  pl.BlockSpec((1, LANES), const),
                  pl.BlockSpec((1, LANES), const)],
        out_specs=[pl.BlockSpec((tq, ATT_WIDTH), row),
                   pl.BlockSpec((tq, KV_WIDTH), row),
                   pl.BlockSpec((tq, KV_WIDTH), row)],
        out_shape=[jax.ShapeDtypeStruct((T, ATT_WIDTH), BF16),
                   jax.ShapeDtypeStruct((T, KV_WIDTH), BF16),
                   jax.ShapeDtypeStruct((T, KV_WIDTH), BF16)],
        compiler_params=_params(("parallel",)),
        name="qkv_prep",
    )(proj, positions.reshape(T, 1), invf, sign, gq, gk)


def _attn_kernel(sink_ref, q_ref, kp_ref, kc_ref, kn_ref, vp_ref, vc_ref, vn_ref, o_ref, *, nb):
    n = pl.program_id(1)
    rows = Q_PER_KV * ATT_BLOCK
    keys = 3 * ATT_BLOCK
    qi = lax.broadcasted_iota(jnp.int32, (rows, keys), 0) % ATT_BLOCK
    kj = lax.broadcasted_iota(jnp.int32, (rows, keys), 1)
    key_pos = kj + (n - 1) * ATT_BLOCK
    in_seq = jnp.logical_and(key_pos >= 0, key_pos < nb * ATT_BLOCK)
    mask = jnp.logical_and(jnp.abs(kj - ATT_BLOCK - qi) <= WINDOW, in_seq)
    row_head = lax.broadcasted_iota(jnp.int32, (rows, 1), 0) // ATT_BLOCK

    for kvh in range(ATT_KV_HEADS):
        cols = slice(kvh * HEAD_DIM, (kvh + 1) * HEAD_DIM)
        k = jnp.concatenate([kp_ref[:, cols], kc_ref[:, cols], kn_ref[:, cols]], axis=0)
        v = jnp.concatenate([vp_ref[:, cols], vc_ref[:, cols], vn_ref[:, cols]], axis=0)
        q = jnp.concatenate(
            [q_ref[:, (kvh * Q_PER_KV + g) * HEAD_DIM:(kvh * Q_PER_KV + g + 1) * HEAD_DIM]
             for g in range(Q_PER_KV)], axis=0)
        sink = jnp.zeros((rows, 1), F32)
        for g in range(Q_PER_KV):
            sink = jnp.where(row_head == g, sink_ref[kvh * Q_PER_KV + g], sink)
        s = lax.dot_general(q, k, (((1,), (1,)), ((), ())), preferred_element_type=F32)
        s = jnp.where(mask, s, NEG_INF)
        m = jnp.maximum(jnp.max(s, axis=-1, keepdims=True), sink)
        e = jnp.exp(s - m)
        denom = jnp.sum(e, axis=-1, keepdims=True) + jnp.exp(sink - m)
        p = (e / denom).astype(BF16)
        o = jnp.dot(p, v, preferred_element_type=F32)
        for g in range(Q_PER_KV):
            h = kvh * Q_PER_KV + g
            o_ref[:, h * HEAD_DIM:(h + 1) * HEAD_DIM] = (
                o[g * ATT_BLOCK:(g + 1) * ATT_BLOCK].astype(o_ref.dtype))


def _attention(q, k, v, sink, batch):
    T = q.shape[0]
    nb = T // batch // ATT_BLOCK
    cur = lambda b, n: (b * nb + n, 0)
    prev = lambda b, n: (b * nb + jnp.maximum(n - 1, 0), 0)
    nxt = lambda b, n: (b * nb + jnp.minimum(n + 1, nb - 1), 0)
    kv = lambda im: pl.BlockSpec((ATT_BLOCK, KV_WIDTH), im)
    return pl.pallas_call(
        functools.partial(_attn_kernel, nb=nb),
        grid=(batch, nb),
        in_specs=[pl.BlockSpec(memory_space=pltpu.SMEM),
                  pl.BlockSpec((ATT_BLOCK, ATT_WIDTH), cur),
                  kv(prev), kv(cur), kv(nxt), kv(prev), kv(cur), kv(nxt)],
        out_specs=pl.BlockSpec((ATT_BLOCK, ATT_WIDTH), cur),
        out_shape=jax.ShapeDtypeStruct((T, ATT_WIDTH), BF16),
        compiler_params=_params(("parallel", "parallel")),
        name="window_attn",
    )(sink, q, k, k, k, v, v, v)


def _sg_kernel(u_ref, s_ref, lng_ref, lnb_ref, w_ref, b_ref, o_ref):
    groups = w_ref.shape[0]
    for gi in range(groups):
        cols = slice(gi * SG_GROUP_DIM, (gi + 1) * SG_GROUP_DIM)
        s = jax.nn.gelu(s_ref[:, cols])
        mu = jnp.mean(s, axis=-1, keepdims=True)
        sc = s - mu
        var = jnp.mean(sc * sc, axis=-1, keepdims=True)
        sn = (sc * lax.rsqrt(var + EPS)) * lng_ref[0, gi:gi + 1, :] + lnb_ref[0, gi:gi + 1, :]
        mixed = jnp.dot(w_ref[gi], sn.astype(BF16), preferred_element_type=F32)
        mixed = mixed + b_ref[0, :, gi:gi + 1]
        o_ref[:, cols] = (jax.nn.gelu(u_ref[:, cols]) * mixed).astype(o_ref.dtype)


def _spatial_gating(proj, ln_g, ln_b, w_s, b_s, halves=2):
    T = proj.shape[0]
    half_w = SG_WIDTH // halves
    gph = SG_GROUPS // halves
    u0, s0 = OFF_U // half_w, OFF_S // half_w
    return pl.pallas_call(
        _sg_kernel,
        grid=(T // SG_CHUNK, halves),
        in_specs=[pl.BlockSpec((SG_CHUNK, half_w), lambda i, j: (i, u0 + j)),
                  pl.BlockSpec((SG_CHUNK, half_w), lambda i, j: (i, s0 + j)),
                  pl.BlockSpec((1, gph, SG_GROUP_DIM), lambda i, j: (j, 0, 0)),
                  pl.BlockSpec((1, gph, SG_GROUP_DIM), lambda i, j: (j, 0, 0)),
                  pl.BlockSpec((gph, SG_CHUNK, SG_CHUNK), lambda i, j: (j, 0, 0)),
                  pl.BlockSpec((1, SG_CHUNK, gph), lambda i, j: (j, 0, 0))],
        out_specs=pl.BlockSpec((SG_CHUNK, half_w), lambda i, j: (i, j)),
        out_shape=jax.ShapeDtypeStruct((T, SG_WIDTH), BF16),
        compiler_params=_params(("parallel", "parallel")),
        name="spatial_gating",
    )(proj, proj,
      ln_g.reshape(halves, gph, SG_GROUP_DIM), ln_b.reshape(halves, gph, SG_GROUP_DIM),
      w_s.astype(BF16),
      b_s.reshape(halves, gph, SG_CHUNK).transpose(0, 2, 1))


def _merge_kernel(att_ref, sgo_ref, ga_ref, gb_ref, wa_ref, wb_ref, o_ref):
    a = jnp.dot(att_ref[...], wa_ref[...], preferred_element_type=F32)
    b = jnp.dot(sgo_ref[...], wb_ref[...], preferred_element_type=F32)
    m = jax.nn.sigmoid(ga_ref[...]) * a + jax.nn.sigmoid(gb_ref[...]) * b
    o_ref[...] = m.astype(o_ref.dtype)


def _merge(att, sgo, proj, w_a, w_b, tm=512, tn=512):
    T = att.shape[0]
    ga0 = OFF_G // tn
    gb0 = (OFF_G + D_MODEL) // tn
    return pl.pallas_call(
        _merge_kernel,
        grid=(T // tm, D_MODEL // tn),
        in_specs=[pl.BlockSpec((tm, ATT_WIDTH), lambda i, j: (i, 0)),
                  pl.BlockSpec((tm, SG_WIDTH), lambda i, j: (i, 0)),
                  pl.BlockSpec((tm, tn), lambda i, j: (i, ga0 + j)),
                  pl.BlockSpec((tm, tn), lambda i, j: (i, gb0 + j)),
                  pl.BlockSpec((ATT_WIDTH, tn), lambda i, j: (0, j)),
                  pl.BlockSpec((SG_WIDTH, tn), lambda i, j: (0, j))],
        out_specs=pl.BlockSpec((tm, tn), lambda i, j: (i, j)),
        out_shape=jax.ShapeDtypeStruct((T, D_MODEL), BF16),
        compiler_params=_params(("parallel", "parallel")),
        name="merge",
    )(att, sgo, proj, proj, w_a.astype(BF16), w_b.astype(BF16))


def _route(logits):
    lane = lax.broadcasted_iota(jnp.int32, logits.shape, 1)
    lane_f = lane.astype(F32)
    is_g = lane < N_GROUPS
    gl = jnp.where(is_g, logits, -jnp.inf)
    gmax = jnp.max(gl, axis=-1, keepdims=True)
    grp = jnp.min(jnp.where(gl == gmax, lane_f, float(LANES)), axis=-1, keepdims=True)
    gsum = jnp.sum(jnp.where(is_g, jnp.exp(logits - gmax), 0.0), axis=-1, keepdims=True)
    g_w = 1.0 / gsum
    e_lane = lane - N_GROUPS
    in_grp = jnp.logical_and(
        jnp.logical_and(e_lane >= 0, e_lane < N_EXPERTS),
        (e_lane // EXPERTS_PER_GROUP).astype(F32) == grp)
    el = jnp.where(in_grp, logits, -jnp.inf)
    v1 = jnp.max(el, axis=-1, keepdims=True)
    i1 = jnp.min(jnp.where(jnp.logical_and(in_grp, el == v1), lane_f, float(LANES)),
                 axis=-1, keepdims=True)
    rest = jnp.logical_and(in_grp, lane_f != i1)
    el2 = jnp.where(rest, logits, -jnp.inf)
    v2 = jnp.max(el2, axis=-1, keepdims=True)
    i2 = jnp.min(jnp.where(jnp.logical_and(rest, el2 == v2), lane_f, float(LANES)),
                 axis=-1, keepdims=True)
    e21 = jnp.exp(v2 - v1)
    w1 = g_w / (1.0 + e21)
    w2 = g_w * e21 / (1.0 + e21)
    idx = jnp.where(lane == 0, i1, i2) - float(N_GROUPS)
    wts = jnp.where(lane == 0, w1, jnp.where(lane == 1, w2, 0.0))
    return idx.astype(jnp.int32), wts


def _out_kernel(m_ref, w_ref, x_ref, g_ref, wr_ref, br_ref, h_ref, hn_ref, idx_ref, wt_ref):
    h = x_ref[...] + jnp.dot(m_ref[...], w_ref[...], preferred_element_type=F32)
    h_ref[...] = h
    r = lax.rsqrt(jnp.mean(h * h, axis=-1, keepdims=True) + EPS)
    hn = (h * r) * g_ref[...]
    hn_ref[...] = hn
    logits = jnp.dot(hn.astype(BF16), wr_ref[...], preferred_element_type=F32) + br_ref[...]
    idx, wts = _route(logits)
    idx_ref[...] = idx
    wt_ref[...] = wts


def _out_proj(merged, w_out, x, g2, w_router, b_router, tm=256):
    T, D = x.shape
    row = lambda i: (i, 0)
    const = lambda i: (0, 0)
    return pl.pallas_call(
        _out_kernel,
        grid=(T // tm,),
        in_specs=[pl.BlockSpec((tm, D), row),
                  pl.BlockSpec((D, D), const),
                  pl.BlockSpec((tm, D), row),
                  pl.BlockSpec((1, D), const),
                  pl.BlockSpec((D, LANES), const),
                  pl.BlockSpec((1, LANES), const)],
        out_specs=[pl.BlockSpec((tm, D), row),
                   pl.BlockSpec((tm, D), row),
                   pl.BlockSpec((tm, LANES), row),
                   pl.BlockSpec((tm, LANES), row)],
        out_shape=[jax.ShapeDtypeStruct((T, D), F32),
                   jax.ShapeDtypeStruct((T, D), F32),
                   jax.ShapeDtypeStruct((T, LANES), jnp.int32),
                   jax.ShapeDtypeStruct((T, LANES), F32)],
        compiler_params=_params(("parallel",)),
        name="out_proj_router",
    )(merged, w_out.astype(BF16), x, g2.reshape(1, D), w_router, b_router)


def _expert_kernel(be_ref, eo_ref, ue_ref, rt_ref, nu_ref, hn_hbm, wg_hbm, wu_hbm, wd_hbm, y_ref,
                   xbuf, wgf, wuf, wdf, wgb, wub, wdb, sem, wsem):
    b = pl.program_id(0)
    n_used = nu_ref[0]
    n_exp = nu_ref[1]
    used = b < n_used
    slot = b % 2

    def weight_copies(ordinal, slot_):
        e = ue_ref[ordinal]
        return (pltpu.make_async_copy(wg_hbm.at[e], wgf.at[slot_], wsem.at[slot_]),
                pltpu.make_async_copy(wu_hbm.at[e], wuf.at[slot_], wsem.at[slot_]),
                pltpu.make_async_copy(wd_hbm.at[e], wdf.at[slot_], wsem.at[slot_]))

    def start_weights(ordinal):
        for c in weight_copies(ordinal, ordinal % 2):
            c.start(priority=1)

    def gather_rows(blk, slot_):
        base = blk * MOE_BLOCK

        def issue(r, c):
            pltpu.make_async_copy(hn_hbm.at[pl.ds(rt_ref[base + r], 1)],
                                  xbuf.at[slot_, pl.ds(r, 1)], sem.at[slot_]).start()
            return c

        lax.fori_loop(0, MOE_BLOCK, issue, 0, unroll=8)

    @pl.when(b == 0)
    def _():
        start_weights(0)

        @pl.when(n_exp > 1)
        def _():
            start_weights(1)

        gather_rows(0, 0)

    @pl.when(b + 1 < n_used)
    def _():
        gather_rows(b + 1, 1 - slot)

    @pl.when(used)
    def _():
        first = jnp.logical_or(b == 0, be_ref[b] != be_ref[jnp.maximum(b - 1, 0)])

        @pl.when(first)
        def _():
            ordinal = eo_ref[b]
            ws = ordinal % 2
            for c in weight_copies(ordinal, ws):
                c.wait()
            wgb[...] = wgf[ws].astype(BF16)
            wub[...] = wuf[ws].astype(BF16)
            wdb[...] = wdf[ws].astype(BF16)

            @pl.when(ordinal + 2 < n_exp)
            def _():
                start_weights(ordinal + 2)

        pltpu.make_async_copy(hn_hbm.at[pl.ds(0, MOE_BLOCK)], xbuf.at[slot], sem.at[slot]).wait()

        x = xbuf[slot].astype(BF16)
        hg = jnp.dot(x, wgb[...], preferred_element_type=F32)
        hu = jnp.dot(x, wub[...], preferred_element_type=F32)
        hdn = (jax.nn.silu(hg) * hu).astype(BF16)
        y_ref[...] = jnp.dot(hdn, wdb[...], preferred_element_type=F32)

    @pl.when(jnp.logical_not(used))
    def _():
        y_ref[...] = jnp.zeros(y_ref.shape, y_ref.dtype)


def _experts(hn, w_gate, w_up, w_down, block_e, block_ord, used_experts, row_tok, n_used):
    T, D = hn.shape
    n_rows = row_tok.shape[0]
    n_blocks = n_rows // MOE_BLOCK
    hbm = pl.BlockSpec(memory_space=pl.ANY)
    grid_spec = pltpu.PrefetchScalarGridSpec(
        num_scalar_prefetch=5,
        grid=(n_blocks,),
        in_specs=[hbm, hbm, hbm, hbm],
        out_specs=pl.BlockSpec((MOE_BLOCK, D), lambda b, *_: (b, 0)),
        scratch_shapes=[pltpu.VMEM((2, MOE_BLOCK, D), F32),
                        pltpu.VMEM((2, D, EXPERT_FF), F32),
                        pltpu.VMEM((2, D, EXPERT_FF), F32),
                        pltpu.VMEM((2, EXPERT_FF, D), F32),
                        pltpu.VMEM((D, EXPERT_FF), BF16),
                        pltpu.VMEM((D, EXPERT_FF), BF16),
                        pltpu.VMEM((EXPERT_FF, D), BF16),
                        pltpu.SemaphoreType.DMA((2,)),
                        pltpu.SemaphoreType.DMA((2,))],
    )
    return pl.pallas_call(
        _expert_kernel,
        grid_spec=grid_spec,
        out_shape=jax.ShapeDtypeStruct((n_rows, D), F32),
        compiler_params=_params(("arbitrary",)),
        name="experts",
    )(block_e, block_ord, used_experts, row_tok, n_used, hn, w_gate, w_up, w_down)


def _combine_kernel(dest_ref, y_hbm, h_ref, wt_ref, o_ref, ybuf, sem, *, tc):
    i = pl.program_id(0)
    slot = i % 2

    def gather_rows(blk, slot_):
        def issue(r, c):
            for k in range(TOP_K):
                src = dest_ref[(blk * tc + r) * TOP_K + k]
                pltpu.make_async_copy(y_hbm.at[pl.ds(src, 1)],
                                      ybuf.at[slot_, k, pl.ds(r, 1)],
                                      sem.at[slot_]).start(priority=k % 2)
            return c

        lax.fori_loop(0, tc, issue, 0, unroll=4)

    @pl.when(i == 0)
    def _():
        gather_rows(0, 0)

    @pl.when(i + 1 < pl.num_programs(0))
    def _():
        gather_rows(i + 1, 1 - slot)

    for k in range(TOP_K):
        pltpu.make_async_copy(y_hbm.at[pl.ds(0, tc)], ybuf.at[slot, k], sem.at[slot]).wait()
    wt = wt_ref[...]
    o_ref[...] = h_ref[...] + (wt[:, 0:1] * ybuf[slot, 0] + wt[:, 1:2] * ybuf[slot, 1])


def _combine(yrows, h, wts, dest, tc=128):
    T, D = h.shape
    row = lambda i, d: (i, 0)
    grid_spec = pltpu.PrefetchScalarGridSpec(
        num_scalar_prefetch=1,
        grid=(T // tc,),
        in_specs=[pl.BlockSpec(memory_space=pl.ANY),
                  pl.BlockSpec((tc, D), row),
                  pl.BlockSpec((tc, LANES), row)],
        out_specs=pl.BlockSpec((tc, D), row),
        scratch_shapes=[pltpu.VMEM((2, TOP_K, tc, D), F32),
                        pltpu.SemaphoreType.DMA((2,))],
    )
    return pl.pallas_call(
        functools.partial(_combine_kernel, tc=tc),
        grid_spec=grid_spec,
        out_shape=jax.ShapeDtypeStruct((T, D), F32),
        compiler_params=_params(("arbitrary",)),
        name="combine",
    )(dest, yrows, h, wts)


SUBLANES = 8
META_ROWS = 256


def _lane_cumsum(x):
    lane = lax.broadcasted_iota(jnp.int32, x.shape, 1)
    s = 1
    while s < LANES:
        x = x + jnp.where(lane >= s, pltpu.roll(x, s, 1), 0)
        s *= 2
    return x


def _dispatch_kernel(idx_ref, dest_ref, meta_ref, run_ref, prefix_ref, start_ref, *, tb):
    p = pl.program_id(0)
    i = pl.program_id(1)
    idx = idx_ref[...]
    lane = lax.broadcasted_iota(jnp.int32, idx.shape, 1)
    e1 = idx[:, 0:1]
    e2 = idx[:, 1:2]
    onehot = jnp.where(jnp.logical_or(lane == e1, lane == e2), 1.0, 0.0)

    @pl.when(jnp.logical_and(p == 0, i == 0))
    def _():
        run_ref[...] = jnp.zeros(run_ref.shape, F32)

    @pl.when(p == 0)
    def _():
        prefix_ref[i] = run_ref[...]
        run_ref[...] = run_ref[...] + jnp.sum(onehot, axis=0, keepdims=True)

    @pl.when(jnp.logical_and(p == 1, i == 0))
    def _():
        counts = run_ref[...].astype(jnp.int32)
        nblk = (counts + (MOE_BLOCK - 1)) // MOE_BLOCK
        end_blk = _lane_cumsum(nblk)
        start_ref[...] = ((end_blk - nblk) * MOE_BLOCK).astype(F32)
        has = jnp.where(counts > 0, 1, 0)
        ordinal = _lane_cumsum(has) - 1
        end_blk, has, ordinal = end_blk[0:1], has[0:1], ordinal[0:1]
        rows = lax.broadcasted_iota(jnp.int32, (META_ROWS, LANES), 0)
        lanes = lax.broadcasted_iota(jnp.int32, (META_ROWS, LANES), 1)
        is_e = lanes < N_EXPERTS
        rsum = lambda v: jnp.sum(v, axis=-1, keepdims=True)
        be = rsum(jnp.where(jnp.logical_and(is_e, end_blk <= rows), 1, 0))
        be = jnp.minimum(be, N_EXPERTS - 1)
        eo = rsum(jnp.where(lanes == be, ordinal, 0))
        ue = rsum(jnp.where(jnp.logical_and(has > 0, ordinal == rows), lanes, 0))
        n_blk = rsum(jnp.where(lanes == N_EXPERTS - 1, end_blk, 0))
        n_exp = rsum(jnp.where(is_e, has, 0))
        meta_ref[...] = jnp.where(
            lanes == 0, be, jnp.where(lanes == 1, eo, jnp.where(
                lanes == 2, ue, jnp.where(lanes == 3, n_blk, n_exp))))

    @pl.when(p == 1)
    def _():
        r = lax.broadcasted_iota(jnp.int32, (tb, tb), 0)
        c = lax.broadcasted_iota(jnp.int32, (tb, tb), 1)
        earlier = jnp.where(c < r, 1.0, 0.0).astype(BF16)
        rank = jnp.dot(earlier, onehot.astype(BF16), preferred_element_type=F32)
        rank = rank + prefix_ref[i][0:1] + start_ref[0:1]
        d1 = jnp.sum(jnp.where(lane == e1, rank, 0.0), axis=-1, keepdims=True)
        d2 = jnp.sum(jnp.where(lane == e2, rank, 0.0), axis=-1, keepdims=True)
        dest_ref[...] = jnp.where(lane == 0, d1, d2).astype(jnp.int32)


def _dispatch(idx, tb=256):
    T = idx.shape[0]
    n_rows = T * TOP_K + N_EXPERTS * MOE_BLOCK
    n_blocks = n_rows // MOE_BLOCK
    assert n_blocks <= META_ROWS
    dest2, meta = pl.pallas_call(
        functools.partial(_dispatch_kernel, tb=tb),
        grid=(2, T // tb),
        in_specs=[pl.BlockSpec((tb, LANES), lambda p, i: (i, 0))],
        out_specs=[pl.BlockSpec((tb, LANES), lambda p, i: (i * p, 0)),
                   pl.BlockSpec((META_ROWS, LANES), lambda p, i: (0, 0))],
        out_shape=[jax.ShapeDtypeStruct((T, LANES), jnp.int32),
                   jax.ShapeDtypeStruct((META_ROWS, LANES), jnp.int32)],
        scratch_shapes=[pltpu.VMEM((SUBLANES, LANES), F32),
                        pltpu.VMEM((T // tb, SUBLANES, LANES), F32),
                        pltpu.VMEM((SUBLANES, LANES), F32)],
        compiler_params=_params(("arbitrary", "arbitrary")),
        name="dispatch",
    )(idx)
    dest = dest2[:, :TOP_K].reshape(T * TOP_K)
    tok = jnp.repeat(jnp.arange(T, dtype=jnp.int32), TOP_K)
    row_tok = jnp.zeros((n_rows,), jnp.int32).at[dest].set(tok)
    block_e = meta[:n_blocks, 0]
    block_ord = meta[:n_blocks, 1]
    used_experts = meta[:N_EXPERTS, 2]
    n_used = meta[0, 3:5]
    return block_e, block_ord, used_experts, row_tok, n_used, dest


def kernel(x, positions, norm1_g, w_in, q_norm_g, k_norm_g, sink_logits, sg_ln_g, sg_ln_b, sg_w, sg_b, w_branch_att, w_branch_sg, w_out, norm2_g, w_group_router, b_group_router, w_expert_router, b_expert_router, w_gate, w_up, w_down):
    B, S, D = x.shape
    T = B * S
    h = x.reshape(T, D)
    pos = positions.reshape(T)
    for l in range(norm1_g.shape[0]):
        xn = _rmsnorm(h, norm1_g[l])
        proj = _in_proj(xn, w_in[l])
        q, k, v = _qkv_prep(proj, pos, q_norm_g[l], k_norm_g[l])
        att = _attention(q, k, v, sink_logits[l], B)
        sgo = _spatial_gating(proj, sg_ln_g[l], sg_ln_b[l], sg_w[l], sg_b[l])
        merged = _merge(att, sgo, proj, w_branch_att[l], w_branch_sg[l])
        pad = LANES - N_GROUPS - N_EXPERTS
        w_router = jnp.concatenate(
            [w_group_router[l], w_expert_router[l], jnp.zeros((D, pad), F32)], axis=1).astype(BF16)
        b_router = jnp.concatenate(
            [b_group_router[l], b_expert_router[l], jnp.zeros((pad,), F32)]).reshape(1, LANES)
        h, hn, idx, wts = _out_proj(merged, w_out[l], h, norm2_g[l], w_router, b_router)
        block_e, block_ord, used_experts, row_tok, n_used, dest = _dispatch(idx)
        yrows = _experts(hn, w_gate[l], w_up[l], w_down[l],
                         block_e, block_ord, used_experts, row_tok, n_used)
        h = _combine(yrows, h, wts, dest)
    return h.reshape(B, S, D)
```

```python
import functools

import jax
import jax.numpy as jnp
from jax import lax
from jax.experimental import pallas as pl
from jax.experimental.pallas import tpu as pltpu

F32 = jnp.float32
BF16 = jnp.bfloat16

D_MODEL = 2048
HEAD_DIM = 64
ATT_WIDTH = D_MODEL // 2
ATT_HEADS = ATT_WIDTH // HEAD_DIM
ATT_KV_HEADS = ATT_HEADS // 4
Q_PER_KV = ATT_HEADS // ATT_KV_HEADS
KV_WIDTH = ATT_KV_HEADS * HEAD_DIM
WINDOW = 128
ATT_BLOCK = 128
ROPE_DIM = HEAD_DIM // 4
ROPE_HALF = ROPE_DIM // 2
ROPE_THETA = 500000.0
SG_WIDTH = D_MODEL // 2
SG_GROUP_DIM = 128
SG_GROUPS = SG_WIDTH // SG_GROUP_DIM
SG_CHUNK = 128
OFF_Q = 0
OFF_K = OFF_Q + ATT_WIDTH
OFF_V = OFF_K + KV_WIDTH
OFF_U = OFF_V + KV_WIDTH
OFF_S = OFF_U + SG_WIDTH
OFF_G = OFF_S + SG_WIDTH
IN_COLS = OFF_G + 2 * D_MODEL
N_GROUPS = 8
EXPERTS_PER_GROUP = 8
N_EXPERTS = N_GROUPS * EXPERTS_PER_GROUP
TOP_K = 2
EXPERT_FF = D_MODEL // 4
MOE_BLOCK = 128
EPS = 1e-6
NEG_INF = -1e30

LANES = 128
HN_WORD_CHUNKS = D_MODEL // LANES // 2
VMEM_LIMIT = 56 * 1024 * 1024


def _params(sem, vmem=VMEM_LIMIT):
    return pltpu.CompilerParams(dimension_semantics=sem, vmem_limit_bytes=vmem)


def _rmsnorm_kernel(x_ref, g_ref, o_ref):
    x = x_ref[...]
    r = lax.rsqrt(jnp.mean(x * x, axis=-1, keepdims=True) + EPS)
    o_ref[...] = ((x * r) * g_ref[...]).astype(o_ref.dtype)


def _rmsnorm(x, g, tm=512):
    T, D = x.shape
    return pl.pallas_call(
        _rmsnorm_kernel,
        grid=(T // tm,),
        in_specs=[pl.BlockSpec((tm, D), lambda i: (i, 0)),
                  pl.BlockSpec((1, D), lambda i: (0, 0))],
        out_specs=pl.BlockSpec((tm, D), lambda i: (i, 0)),
        out_shape=jax.ShapeDtypeStruct((T, D), BF16),
        compiler_params=_params(("parallel",)),
        name="norm1",
    )(x, g.reshape(1, D))


def _proj_kernel(x_ref, w_ref, o_ref, wbf_ref):
    @pl.when(pl.program_id(1) == 0)
    def _():
        wbf_ref[...] = w_ref[...].astype(BF16)

    o_ref[...] = jnp.dot(x_ref[...], wbf_ref[...], preferred_element_type=F32)


def _in_proj(xn, w, tm=512, tn=1280):
    T, D = xn.shape
    N = w.shape[1]
    return pl.pallas_call(
        _proj_kernel,
        grid=(N // tn, T // tm),
        in_specs=[pl.BlockSpec((tm, D), lambda j, i: (i, 0)),
                  pl.BlockSpec((D, tn), lambda j, i: (0, j))],
        out_specs=pl.BlockSpec((tm, tn), lambda j, i: (i, j)),
        out_shape=jax.ShapeDtypeStruct((T, N), F32),
        scratch_shapes=[pltpu.VMEM((D, tn), BF16)],
        compiler_params=_params(("arbitrary", "arbitrary")),
        name="in_proj",
    )(xn, w)


def _qkv_prep_kernel(p_ref, pos_ref, invf_ref, sign_ref, gq_ref, gk_ref, q_ref, k_ref, v_ref):
    pos = pos_ref[...].astype(F32)
    ang = pos * invf_ref[...]
    cos = jnp.cos(ang)
    sin = jnp.sin(ang) * sign_ref[...]
    lane = lax.broadcasted_iota(jnp.int32, ang.shape, 1)
    low_head = lane < HEAD_DIM
    first_half = (lane % HEAD_DIM) < ROPE_HALF

    def norm_rope(x, g):
        x2 = x * x
        s_lo = jnp.sum(jnp.where(low_head, x2, 0.0), axis=-1, keepdims=True)
        s_hi = jnp.sum(jnp.where(low_head, 0.0, x2), axis=-1, keepdims=True)
        ssq = jnp.where(low_head, s_lo, s_hi)
        xn = (x * lax.rsqrt(ssq * (1.0 / HEAD_DIM) + EPS)) * g
        partner = jnp.where(first_half,
                            pltpu.roll(xn, LANES - ROPE_HALF, 1),
                            pltpu.roll(xn, ROPE_HALF, 1))
        return xn * cos + partner * sin

    for c in range(ATT_WIDTH // LANES):
        x = p_ref[:, OFF_Q + c * LANES:OFF_Q + (c + 1) * LANES]
        q_ref[:, c * LANES:(c + 1) * LANES] = (
            norm_rope(x, gq_ref[...]) * (HEAD_DIM ** -0.5)).astype(q_ref.dtype)
    for c in range(KV_WIDTH // LANES):
        x = p_ref[:, OFF_K + c * LANES:OFF_K + (c + 1) * LANES]
        k_ref[:, c * LANES:(c + 1) * LANES] = norm_rope(x, gk_ref[...]).astype(k_ref.dtype)
    v_ref[...] = p_ref[:, OFF_V:OFF_V + KV_WIDTH].astype(v_ref.dtype)


def _qkv_prep(proj, positions, q_g, k_g, tq=256):
    T = proj.shape[0]
    width = OFF_U
    lane = jnp.arange(LANES) % HEAD_DIM
    inv = ROPE_THETA ** (-jnp.arange(0, ROPE_DIM, 2, dtype=F32) / ROPE_DIM)
    invf = jnp.where(lane < ROPE_DIM, inv[lane % ROPE_HALF], 0.0).reshape(1, LANES)
    sign = jnp.where(lane < ROPE_HALF, -1.0, jnp.where(lane < ROPE_DIM, 1.0, 0.0))
    sign = sign.astype(F32).reshape(1, LANES)
    gq = jnp.tile(q_g, LANES // HEAD_DIM).reshape(1, LANES)
    gk = jnp.tile(k_g, LANES // HEAD_DIM).reshape(1, LANES)
    row = lambda i: (i, 0)
    const = lambda i: (0, 0)
    return pl.pallas_call(
        _qkv_prep_kernel,
        grid=(T // tq,),
        in_specs=[pl.BlockSpec((tq, width), row),
                  pl.BlockSpec((tq, 1), row),
                  pl.BlockSpec((1, LANES), const),
                  pl.BlockSpec((1, LANES), const),
                  pl.BlockSpec((1, LANES), const),
                  pl.BlockSpec((1, LANES), const)],
        out_specs=[pl.BlockSpec((tq, ATT_WIDTH), row),
                   pl.BlockSpec((tq, KV_WIDTH), row),
                   pl.BlockSpec((tq, KV_WIDTH), row)],
        out_shape=[jax.ShapeDtypeStruct((T, ATT_WIDTH), BF16),
                   jax.ShapeDtypeStruct((T, KV_WIDTH), BF16),
                   jax.ShapeDtypeStruct((T, KV_WIDTH), BF16)],
        compiler_params=_params(("parallel",)),
        name="qkv_prep",
    )(proj, positions.reshape(T, 1), invf, sign, gq, gk)


def _attn_kernel(sink_ref, q_ref, kp_ref, kc_ref, kn_ref, vp_ref, vc_ref, vn_ref, o_ref, *, nb):
    n = pl.program_id(1)
    rows = Q_PER_KV * ATT_BLOCK
    keys = 3 * ATT_BLOCK
    qi = lax.broadcasted_iota(jnp.int32, (rows, keys), 0) % ATT_BLOCK
    kj = lax.broadcasted_iota(jnp.int32, (rows, keys), 1)
    key_pos = kj + (n - 1) * ATT_BLOCK
    in_seq = jnp.logical_and(key_pos >= 0, key_pos < nb * ATT_BLOCK)
    mask = jnp.logical_and(jnp.abs(kj - ATT_BLOCK - qi) <= WINDOW, in_seq)
    row_head = lax.broadcasted_iota(jnp.int32, (rows, 1), 0) // ATT_BLOCK

    for kvh in range(ATT_KV_HEADS):
        cols = slice(kvh * HEAD_DIM, (kvh + 1) * HEAD_DIM)
        k = jnp.concatenate([kp_ref[:, cols], kc_ref[:, cols], kn_ref[:, cols]], axis=0)
        v = jnp.concatenate([vp_ref[:, cols], vc_ref[:, cols], vn_ref[:, cols]], axis=0)
        q = jnp.concatenate(
            [q_ref[:, (kvh * Q_PER_KV + g) * HEAD_DIM:(kvh * Q_PER_KV + g + 1) * HEAD_DIM]
             for g in range(Q_PER_KV)], axis=0)
        sink = jnp.zeros((rows, 1), F32)
        for g in range(Q_PER_KV):
            sink = jnp.where(row_head == g, sink_ref[kvh * Q_PER_KV + g], sink)
        s = lax.dot_general(q, k, (((1,), (1,)), ((), ())), preferred_element_type=F32)
        s = jnp.where(mask, s, NEG_INF)
        m = jnp.maximum(jnp.max(s, axis=-1, keepdims=True), sink)
        e = jnp.exp(s - m)
        denom = jnp.sum(e, axis=-1, keepdims=True) + jnp.exp(sink - m)
        p = (e / denom).astype(BF16)
        o = jnp.dot(p, v, preferred_element_type=F32)
        for g in range(Q_PER_KV):
            h = kvh * Q_PER_KV + g
            o_ref[:, h * HEAD_DIM:(h + 1) * HEAD_DIM] = (
                o[g * ATT_BLOCK:(g + 1) * ATT_BLOCK].astype(o_ref.dtype))


def _attention(q, k, v, sink, batch):
    T = q.shape[0]
    nb = T // batch // ATT_BLOCK
    cur = lambda b, n: (b * nb + n, 0)
    prev = lambda b, n: (b * nb + jnp.maximum(n - 1, 0), 0)
    nxt = lambda b, n: (b * nb + jnp.minimum(n + 1, nb - 1), 0)
    kv = lambda im: pl.BlockSpec((ATT_BLOCK, KV_WIDTH), im)
    return pl.pallas_call(
        functools.partial(_attn_kernel, nb=nb),
        grid=(batch, nb),
        in_specs=[pl.BlockSpec(memory_space=pltpu.SMEM),
                  pl.BlockSpec((ATT_BLOCK, ATT_WIDTH), cur),
                  kv(prev), kv(cur), kv(nxt), kv(prev), kv(cur), kv(nxt)],
        out_specs=pl.BlockSpec((ATT_BLOCK, ATT_WIDTH), cur),
        out_shape=jax.ShapeDtypeStruct((T, ATT_WIDTH), BF16),
        compiler_params=_params(("parallel", "parallel")),
        name="window_attn",
    )(sink, q, k, k, k, v, v, v)


def _sg_kernel(u_ref, s_ref, lng_ref, lnb_ref, w_ref, b_ref, o_ref):
    groups = w_ref.shape[0]
    for gi in range(groups):
        cols = slice(gi * SG_GROUP_DIM, (gi + 1) * SG_GROUP_DIM)
        s = jax.nn.gelu(s_ref[:, cols])
        mu = jnp.mean(s, axis=-1, keepdims=True)
        sc = s - mu
        var = jnp.mean(sc * sc, axis=-1, keepdims=True)
        sn = (sc * lax.rsqrt(var + EPS)) * lng_ref[0, gi:gi + 1, :] + lnb_ref[0, gi:gi + 1, :]
        mixed = jnp.dot(w_ref[gi], sn.astype(BF16), preferred_element_type=F32)
        mixed = mixed + b_ref[0, :, gi:gi + 1]
        o_ref[:, cols] = (jax.nn.gelu(u_ref[:, cols]) * mixed).astype(o_ref.dtype)


def _spatial_gating(proj, ln_g, ln_b, w_s, b_s, halves=2):
    T = proj.shape[0]
    half_w = SG_WIDTH // halves
    gph = SG_GROUPS // halves
    u0, s0 = OFF_U // half_w, OFF_S // half_w
    return pl.pallas_call(
        _sg_kernel,
        grid=(T // SG_CHUNK, halves),
        in_specs=[pl.BlockSpec((SG_CHUNK, half_w), lambda i, j: (i, u0 + j)),
                  pl.BlockSpec((SG_CHUNK, half_w), lambda i, j: (i, s0 + j)),
                  pl.BlockSpec((1, gph, SG_GROUP_DIM), lambda i, j: (j, 0, 0)),
                  pl.BlockSpec((1, gph, SG_GROUP_DIM), lambda i, j: (j, 0, 0)),
                  pl.BlockSpec((gph, SG_CHUNK, SG_CHUNK), lambda i, j: (j, 0, 0)),
                  pl.BlockSpec((1, SG_CHUNK, gph), lambda i, j: (j, 0, 0))],
        out_specs=pl.BlockSpec((SG_CHUNK, half_w), lambda i, j: (i, j)),
        out_shape=jax.ShapeDtypeStruct((T, SG_WIDTH), BF16),
        compiler_params=_params(("parallel", "parallel")),
        name="spatial_gating",
    )(proj, proj,
      ln_g.reshape(halves, gph, SG_GROUP_DIM), ln_b.reshape(halves, gph, SG_GROUP_DIM),
      w_s.astype(BF16),
      b_s.reshape(halves, gph, SG_CHUNK).transpose(0, 2, 1))


def _merge_kernel(att_ref, sgo_ref, ga_ref, gb_ref, wa_ref, wb_ref, o_ref):
    a = jnp.dot(att_ref[...], wa_ref[...], preferred_element_type=F32)
    b = jnp.dot(sgo_ref[...], wb_ref[...], preferred_element_type=F32)
    m = jax.nn.sigmoid(ga_ref[...]) * a + jax.nn.sigmoid(gb_ref[...]) * b
    o_ref[...] = m.astype(o_ref.dtype)


def _merge(att, sgo, proj, w_a, w_b, tm=512, tn=512):
    T = att.shape[0]
    ga0 = OFF_G // tn
    gb0 = (OFF_G + D_MODEL) // tn
    return pl.pallas_call(
        _merge_kernel,
        grid=(T // tm, D_MODEL // tn),
        in_specs=[pl.BlockSpec((tm, ATT_WIDTH), lambda i, j: (i, 0)),
                  pl.BlockSpec((tm, SG_WIDTH), lambda i, j: (i, 0)),
                  pl.BlockSpec((tm, tn), lambda i, j: (i, ga0 + j)),
                  pl.BlockSpec((tm, tn), lambda i, j: (i, gb0 + j)),
                  pl.BlockSpec((ATT_WIDTH, tn), lambda i, j: (0, j)),
                  pl.BlockSpec((SG_WIDTH, tn), lambda i, j: (0, j))],
        out_specs=pl.BlockSpec((tm, tn), lambda i, j: (i, j)),
        out_shape=jax.ShapeDtypeStruct((T, D_MODEL), BF16),
        compiler_params=_params(("parallel", "parallel")),
        name="merge",
    )(att, sgo, proj, proj, w_a.astype(BF16), w_b.astype(BF16))


def _route(logits):
    lane = lax.broadcasted_iota(jnp.int32, logits.shape, 1)
    lane_f = lane.astype(F32)
    is_g = lane < N_GROUPS
    gl = jnp.where(is_g, logits, -jnp.inf)
    gmax = jnp.max(gl, axis=-1, keepdims=True)
    grp = jnp.min(jnp.where(gl == gmax, lane_f, float(LANES)), axis=-1, keepdims=True)
    gsum = jnp.sum(jnp.where(is_g, jnp.exp(logits - gmax), 0.0), axis=-1, keepdims=True)
    g_w = 1.0 / gsum
    e_lane = lane - N_GROUPS
    in_grp = jnp.logical_and(
        jnp.logical_and(e_lane >= 0, e_lane < N_EXPERTS),
        (e_lane // EXPERTS_PER_GROUP).astype(F32) == grp)
    el = jnp.where(in_grp, logits, -jnp.inf)
    v1 = jnp.max(el, axis=-1, keepdims=True)
    i1 = jnp.min(jnp.where(jnp.logical_and(in_grp, el == v1), lane_f, float(LANES)),
                 axis=-1, keepdims=True)
    rest = jnp.logical_and(in_grp, lane_f != i1)
    el2 = jnp.where(rest, logits, -jnp.inf)
    v2 = jnp.max(el2, axis=-1, keepdims=True)
    i2 = jnp.min(jnp.where(jnp.logical_and(rest, el2 == v2), lane_f, float(LANES)),
                 axis=-1, keepdims=True)
    e21 = jnp.exp(v2 - v1)
    w1 = g_w / (1.0 + e21)
    w2 = g_w * e21 / (1.0 + e21)
    idx = jnp.where(lane == 0, i1, i2) - float(N_GROUPS)
    wts = jnp.where(lane == 0, w1, jnp.where(lane == 1, w2, 0.0))
    return idx.astype(jnp.int32), wts


def _out_kernel(m_ref, w_ref, x_ref, g_ref, wr_ref, br_ref, h_ref, hn_ref, idx_ref, wt_ref):
    h = x_ref[...] + jnp.dot(m_ref[...], w_ref[...], preferred_element_type=F32)
    h_ref[...] = h
    r = lax.rsqrt(jnp.mean(h * h, axis=-1, keepdims=True) + EPS)
    hn = ((h * r) * g_ref[...]).astype(BF16)
    bits = lax.bitcast_convert_type(hn.astype(F32), jnp.uint32)
    for c in range(HN_WORD_CHUNKS):
        hi = bits[:, c * LANES:(c + 1) * LANES]
        lo = bits[:, (c + HN_WORD_CHUNKS) * LANES:(c + HN_WORD_CHUNKS + 1) * LANES]
        hn_ref[pl.ds(c, hn.shape[0], stride=HN_WORD_CHUNKS), :] = hi | (lo >> 16)
    logits = jnp.dot(hn, wr_ref[...], preferred_element_type=F32) + br_ref[...]
    idx, wts = _route(logits)
    idx_ref[...] = idx
    wt_ref[...] = wts


def _out_proj(merged, w_out, x, g2, w_router, b_router, tm=256):
    T, D = x.shape
    row = lambda i: (i, 0)
    const = lambda i: (0, 0)
    return pl.pallas_call(
        _out_kernel,
        grid=(T // tm,),
        in_specs=[pl.BlockSpec((tm, D), row),
                  pl.BlockSpec((D, D), const),
                  pl.BlockSpec((tm, D), row),
                  pl.BlockSpec((1, D), const),
                  pl.BlockSpec((D, LANES), const),
                  pl.BlockSpec((1, LANES), const)],
        out_specs=[pl.BlockSpec((tm, D), row),
                   pl.BlockSpec((tm * HN_WORD_CHUNKS, LANES), row),
                   pl.BlockSpec((tm, LANES), row),
                   pl.BlockSpec((tm, LANES), row)],
        out_shape=[jax.ShapeDtypeStruct((T, D), F32),
                   jax.ShapeDtypeStruct((T * HN_WORD_CHUNKS, LANES), jnp.uint32),
                   jax.ShapeDtypeStruct((T, LANES), jnp.int32),
                   jax.ShapeDtypeStruct((T, LANES), F32)],
        compiler_params=_params(("parallel",)),
        name="out_proj_router",
    )(merged, w_out.astype(BF16), x, g2.reshape(1, D), w_router, b_router)


def _expert_kernel(be_ref, eo_ref, ue_ref, rt_ref, nu_ref, hn_hbm, wg_hbm, wu_hbm, wd_hbm, y_ref,
                   xbuf, wgf, wuf, wdf, wgb, wub, wdb, sem, wsem):
    b = pl.program_id(0)
    n_used = nu_ref[0]
    n_exp = nu_ref[1]
    used = b < n_used
    slot = b % 2

    def weight_copies(ordinal, slot_):
        e = ue_ref[ordinal]
        return (pltpu.make_async_copy(wg_hbm.at[e], wgf.at[slot_], wsem.at[slot_]),
                pltpu.make_async_copy(wu_hbm.at[e], wuf.at[slot_], wsem.at[slot_]),
                pltpu.make_async_copy(wd_hbm.at[e], wdf.at[slot_], wsem.at[slot_]))

    def start_weights(ordinal):
        for c in weight_copies(ordinal, ordinal % 2):
            c.start(priority=1)

    def gather_rows(blk, slot_):
        base = blk * MOE_BLOCK

        def issue(r, c):
            src = pl.multiple_of(rt_ref[base + r] * HN_WORD_CHUNKS, HN_WORD_CHUNKS)
            dst = pl.multiple_of(r * HN_WORD_CHUNKS, HN_WORD_CHUNKS)
            pltpu.make_async_copy(hn_hbm.at[pl.ds(src, HN_WORD_CHUNKS)],
                                  xbuf.at[slot_, pl.ds(dst, HN_WORD_CHUNKS)], sem.at[slot_]).start()
            return c

        lax.fori_loop(0, MOE_BLOCK, issue, 0, unroll=8)

    @pl.when(b == 0)
    def _():
        start_weights(0)

        @pl.when(n_exp > 1)
        def _():
            start_weights(1)

        gather_rows(0, 0)

    @pl.when(b + 1 < n_used)
    def _():
        gather_rows(b + 1, 1 - slot)

    @pl.when(used)
    def _():
        first = jnp.logical_or(b == 0, be_ref[b] != be_ref[jnp.maximum(b - 1, 0)])

        @pl.when(first)
        def _():
            ordinal = eo_ref[b]
            ws = ordinal % 2
            for c in weight_copies(ordinal, ws):
                c.wait()
            wgb[...] = wgf[ws].astype(BF16)
            wub[...] = wuf[ws].astype(BF16)
            wdb[...] = wdf[ws].astype(BF16)

            @pl.when(ordinal + 2 < n_exp)
            def _():
                start_weights(ordinal + 2)

        pltpu.make_async_copy(hn_hbm.at[pl.ds(0, MOE_BLOCK * HN_WORD_CHUNKS)], xbuf.at[slot],
                              sem.at[slot]).wait()

        halves = ([], [])
        for c in range(HN_WORD_CHUNKS):
            w = xbuf[slot, pl.ds(c, MOE_BLOCK, stride=HN_WORD_CHUNKS), :]
            hi = lax.bitcast_convert_type(w & jnp.uint32(0xFFFF0000), F32)
            lo = lax.bitcast_convert_type(w << 16, F32)
            halves[0].append(hi.astype(BF16))
            halves[1].append(lo.astype(BF16))
        x = jnp.concatenate(halves[0] + halves[1], axis=1)
        hg = jnp.dot(x, wgb[...], preferred_element_type=F32)
        hu = jnp.dot(x, wub[...], preferred_element_type=F32)
        hdn = (jax.nn.silu(hg) * hu).astype(BF16)
        y_ref[...] = jnp.dot(hdn, wdb[...], preferred_element_type=F32)

    @pl.when(jnp.logical_not(used))
    def _():
        y_ref[...] = jnp.zeros(y_ref.shape, y_ref.dtype)


def _experts(hn, w_gate, w_up, w_down, block_e, block_ord, used_experts, row_tok, n_used):
    D = w_gate.shape[1]
    n_rows = row_tok.shape[0]
    n_blocks = n_rows // MOE_BLOCK
    hbm = pl.BlockSpec(memory_space=pl.ANY)
    grid_spec = pltpu.PrefetchScalarGridSpec(
        num_scalar_prefetch=5,
        grid=(n_blocks,),
        in_specs=[hbm, hbm, hbm, hbm],
        out_specs=pl.BlockSpec((MOE_BLOCK, D), lambda b, *_: (b, 0)),
        scratch_shapes=[pltpu.VMEM((2, MOE_BLOCK * HN_WORD_CHUNKS, LANES), jnp.uint32),
                        pltpu.VMEM((2, D, EXPERT_FF), F32),
                        pltpu.VMEM((2, D, EXPERT_FF), F32),
                        pltpu.VMEM((2, EXPERT_FF, D), F32),
                        pltpu.VMEM((D, EXPERT_FF), BF16),
                        pltpu.VMEM((D, EXPERT_FF), BF16),
                        pltpu.VMEM((EXPERT_FF, D), BF16),
                        pltpu.SemaphoreType.DMA((2,)),
                        pltpu.SemaphoreType.DMA((2,))],
    )
    return pl.pallas_call(
        _expert_kernel,
        grid_spec=grid_spec,
        out_shape=jax.ShapeDtypeStruct((n_rows, D), F32),
        compiler_params=_params(("arbitrary",)),
        name="experts",
    )(block_e, block_ord, used_experts, row_tok, n_used, hn, w_gate, w_up, w_down)


def _combine_kernel(dest_ref, y_hbm, h_ref, wt_ref, o_ref, ybuf, sem, *, tc):
    i = pl.program_id(0)
    slot = i % 2

    def gather_rows(blk, slot_):
        def issue(r, c):
            for k in range(TOP_K):
                src = dest_ref[(blk * tc + r) * TOP_K + k]
                pltpu.make_async_copy(y_hbm.at[pl.ds(src, 1)],
                                      ybuf.at[slot_, k, pl.ds(r, 1)],
                                      sem.at[slot_]).start(priority=k % 2)
            return c

        lax.fori_loop(0, tc, issue, 0, unroll=4)

    @pl.when(i == 0)
    def _():
        gather_rows(0, 0)

    @pl.when(i + 1 < pl.num_programs(0))
    def _():
        gather_rows(i + 1, 1 - slot)

    for k in range(TOP_K):
        pltpu.make_async_copy(y_hbm.at[pl.ds(0, tc)], ybuf.at[slot, k], sem.at[slot]).wait()
    wt = wt_ref[...]
    o_ref[...] = h_ref[...] + (wt[:, 0:1] * ybuf[slot, 0] + wt[:, 1:2] * ybuf[slot, 1])


def _combine(yrows, h, wts, dest, tc=128):
    T, D = h.shape
    row = lambda i, d: (i, 0)
    grid_spec = pltpu.PrefetchScalarGridSpec(
        num_scalar_prefetch=1,
        grid=(T // tc,),
        in_specs=[pl.BlockSpec(memory_space=pl.ANY),
                  pl.BlockSpec((tc, D), row),
                  pl.BlockSpec((tc, LANES), row)],
        out_specs=pl.BlockSpec((tc, D), row),
        scratch_shapes=[pltpu.VMEM((2, TOP_K, tc, D), F32),
                        pltpu.SemaphoreType.DMA((2,))],
    )
    return pl.pallas_call(
        functools.partial(_combine_kernel, tc=tc),
        grid_spec=grid_spec,
        out_shape=jax.ShapeDtypeStruct((T, D), F32),
        compiler_params=_params(("arbitrary",)),
        name="combine",
    )(dest, yrows, h, wts)


SUBLANES = 8
META_ROWS = 256


def _lane_cumsum(x):
    lane = lax.broadcasted_iota(jnp.int32, x.shape, 1)
    s = 1
    while s < LANES:
        x = x + jnp.where(lane >= s, pltpu.roll(x, s, 1), 0)
        s *= 2
    return x


def _dispatch_kernel(idx_ref, dest_ref, meta_ref, run_ref, prefix_ref, start_ref, *, tb):
    p = pl.program_id(0)
    i = pl.program_id(1)
    idx = idx_ref[...]
    lane = lax.broadcasted_iota(jnp.int32, idx.shape, 1)
    e1 = idx[:, 0:1]
    e2 = idx[:, 1:2]
    onehot = jnp.where(jnp.logical_or(lane == e1, lane == e2), 1.0, 0.0)

    @pl.when(jnp.logical_and(p == 0, i == 0))
    def _():
        run_ref[...] = jnp.zeros(run_ref.shape, F32)

    @pl.when(p == 0)
    def _():
        prefix_ref[i] = run_ref[...]
        run_ref[...] = run_ref[...] + jnp.sum(onehot, axis=0, keepdims=True)

    @pl.when(jnp.logical_and(p == 1, i == 0))
    def _():
        counts = run_ref[...].astype(jnp.int32)
        nblk = (counts + (MOE_BLOCK - 1)) // MOE_BLOCK
        end_blk = _lane_cumsum(nblk)
        start_ref[...] = ((end_blk - nblk) * MOE_BLOCK).astype(F32)
        has = jnp.where(counts > 0, 1, 0)
        ordinal = _lane_cumsum(has) - 1
        end_blk, has, ordinal = end_blk[0:1], has[0:1], ordinal[0:1]
        rows = lax.broadcasted_iota(jnp.int32, (META_ROWS, LANES), 0)
        lanes = lax.broadcasted_iota(jnp.int32, (META_ROWS, LANES), 1)
        is_e = lanes < N_EXPERTS
        rsum = lambda v: jnp.sum(v, axis=-1, keepdims=True)
        be = rsum(jnp.where(jnp.logical_and(is_e, end_blk <= rows), 1, 0))
        be = jnp.minimum(be, N_EXPERTS - 1)
        eo = rsum(jnp.where(lanes == be, ordinal, 0))
        ue = rsum(jnp.where(jnp.logical_and(has > 0, ordinal == rows), lanes, 0))
        n_blk = rsum(jnp.where(lanes == N_EXPERTS - 1, end_blk, 0))
        n_exp = rsum(jnp.where(is_e, has, 0))
        meta_ref[...] = jnp.where(
            lanes == 0, be, jnp.where(lanes == 1, eo, jnp.where(
                lanes == 2, ue, jnp.where(lanes == 3, n_blk, n_exp))))

    @pl.when(p == 1)
    def _():
        r = lax.broadcasted_iota(jnp.int32, (tb, tb), 0)
        c = lax.broadcasted_iota(jnp.int32, (tb, tb), 1)
        earlier = jnp.where(c < r, 1.0, 0.0).astype(BF16)
        rank = jnp.dot(earlier, onehot.astype(BF16), preferred_element_type=F32)
        rank = rank + prefix_ref[i][0:1] + start_ref[0:1]
        d1 = jnp.sum(jnp.where(lane == e1, rank, 0.0), axis=-1, keepdims=True)
        d2 = jnp.sum(jnp.where(lane == e2, rank, 0.0), axis=-1, keepdims=True)
        dest_ref[...] = jnp.where(lane == 0, d1, d2).astype(jnp.int32)


def _dispatch(idx, tb=256):
    T = idx.shape[0]
    n_rows = T * TOP_K + N_EXPERTS * MOE_BLOCK
    n_blocks = n_rows // MOE_BLOCK
    assert n_blocks <= META_ROWS
    dest2, meta = pl.pallas_call(
        functools.partial(_dispatch_kernel, tb=tb),
        grid=(2, T // tb),
        in_specs=[pl.BlockSpec((tb, LANES), lambda p, i: (i, 0))],
        out_specs=[pl.BlockSpec((tb, LANES), lambda p, i: (i * p, 0)),
                   pl.BlockSpec((META_ROWS, LANES), lambda p, i: (0, 0))],
        out_shape=[jax.ShapeDtypeStruct((T, LANES), jnp.int32),
                   jax.ShapeDtypeStruct((META_ROWS, LANES), jnp.int32)],
        scratch_shapes=[pltpu.VMEM((SUBLANES, LANES), F32),
                        pltpu.VMEM((T // tb, SUBLANES, LANES), F32),
                        pltpu.VMEM((SUBLANES, LANES), F32)],
        compiler_params=_params(("arbitrary", "arbitrary")),
        name="dispatch",
    )(idx)
    dest = dest2[:, :TOP_K].reshape(T * TOP_K)
    tok = jnp.repeat(jnp.arange(T, dtype=jnp.int32), TOP_K)
    row_tok = jnp.zeros((n_rows,), jnp.int32).at[dest].set(tok)
    block_e = meta[:n_blocks, 0]
    block_ord = meta[:n_blocks, 1]
    used_experts = meta[:N_EXPERTS, 2]
    n_used = meta[0, 3:5]
    return block_e, block_ord, used_experts, row_tok, n_used, dest


def kernel(x, positions, norm1_g, w_in, q_norm_g, k_norm_g, sink_logits, sg_ln_g, sg_ln_b, sg_w, sg_b, w_branch_att, w_branch_sg, w_out, norm2_g, w_group_router, b_group_router, w_expert_router, b_expert_router, w_gate, w_up, w_down):
    B, S, D = x.shape
    T = B * S
    h = x.reshape(T, D)
    pos = positions.reshape(T)
    for l in range(norm1_g.shape[0]):
        xn = _rmsnorm(h, norm1_g[l])
        proj = _in_proj(xn, w_in[l])
        q, k, v = _qkv_prep(proj, pos, q_norm_g[l], k_norm_g[l])
        att = _attention(q, k, v, sink_logits[l], B)
        sgo = _spatial_gating(proj, sg_ln_g[l], sg_ln_b[l], sg_w[l], sg_b[l])
        merged = _merge(att, sgo, proj, w_branch_att[l], w_branch_sg[l])
        pad = LANES - N_GROUPS - N_EXPERTS
        w_router = jnp.concatenate(
            [w_group_router[l], w_expert_router[l], jnp.zeros((D, pad), F32)], axis=1).astype(BF16)
        b_router = jnp.concatenate(
            [b_group_router[l], b_expert_router[l], jnp.zeros((pad,), F32)]).reshape(1, LANES)
        h, hn, idx, wts = _out_proj(merged, w_out[l], h, norm2_g[l], w_router, b_router)
        block_e, block_ord, used_experts, row_tok, n_used, dest = _dispatch(idx)
        yrows = _experts(hn, w_gate[l], w_up[l], w_down[l],
                         block_e, block_ord, used_experts, row_tok, n_used)
        h = _combine(yrows, h, wts, dest)
    return h.reshape(B, S, D)
```

```python
import functools

import jax
import jax.numpy as jnp
from jax import lax
from jax.experimental import pallas as pl
from jax.experimental.pallas import tpu as pltpu

F32 = jnp.float32
BF16 = jnp.bfloat16

D_MODEL = 2048
HEAD_DIM = 64
ATT_WIDTH = D_MODEL // 2
ATT_HEADS = ATT_WIDTH // HEAD_DIM
ATT_KV_HEADS = ATT_HEADS // 4
Q_PER_KV = ATT_HEADS // ATT_KV_HEADS
KV_WIDTH = ATT_KV_HEADS * HEAD_DIM
WINDOW = 128
ATT_BLOCK = 128
ROPE_DIM = HEAD_DIM // 4
ROPE_HALF = ROPE_DIM // 2
ROPE_THETA = 500000.0
SG_WIDTH = D_MODEL // 2
SG_GROUP_DIM = 128
SG_GROUPS = SG_WIDTH // SG_GROUP_DIM
SG_CHUNK = 128
OFF_Q = 0
OFF_K = OFF_Q + ATT_WIDTH
OFF_V = OFF_K + KV_WIDTH
OFF_U = OFF_V + KV_WIDTH
OFF_S = OFF_U + SG_WIDTH
OFF_G = OFF_S + SG_WIDTH
IN_COLS = OFF_G + 2 * D_MODEL
N_GROUPS = 8
EXPERTS_PER_GROUP = 8
N_EXPERTS = N_GROUPS * EXPERTS_PER_GROUP
TOP_K = 2
EXPERT_FF = D_MODEL // 4
MOE_BLOCK = 128
EPS = 1e-6
NEG_INF = -1e30

LANES = 128
HN_WORD_CHUNKS = D_MODEL // LANES // 2
ROWS_PER_ISSUE = 8
VMEM_LIMIT = 56 * 1024 * 1024


def _params(sem, vmem=VMEM_LIMIT):
    return pltpu.CompilerParams(dimension_semantics=sem, vmem_limit_bytes=vmem)


def _rmsnorm_kernel(x_ref, g_ref, o_ref):
    x = x_ref[...]
    r = lax.rsqrt(jnp.mean(x * x, axis=-1, keepdims=True) + EPS)
    o_ref[...] = ((x * r) * g_ref[...]).astype(o_ref.dtype)


def _rmsnorm(x, g, tm=512):
    T, D = x.shape
    return pl.pallas_call(
        _rmsnorm_kernel,
        grid=(T // tm,),
        in_specs=[pl.BlockSpec((tm, D), lambda i: (i, 0)),
                  pl.BlockSpec((1, D), lambda i: (0, 0))],
        out_specs=pl.BlockSpec((tm, D), lambda i: (i, 0)),
        out_shape=jax.ShapeDtypeStruct((T, D), BF16),
        compiler_params=_params(("parallel",)),
        name="norm1",
    )(x, g.reshape(1, D))


def _proj_kernel(x_ref, w_ref, o_ref, wbf_ref):
    @pl.when(pl.program_id(1) == 0)
    def _():
        wbf_ref[...] = w_ref[...].astype(BF16)

    o_ref[...] = jnp.dot(x_ref[...], wbf_ref[...], preferred_element_type=F32)


def _in_proj(xn, w, tm=512, tn=1280):
    T, D = xn.shape
    N = w.shape[1]
    return pl.pallas_call(
        _proj_kernel,
        grid=(N // tn, T // tm),
        in_specs=[pl.BlockSpec((tm, D), lambda j, i: (i, 0)),
                  pl.BlockSpec((D, tn), lambda j, i: (0, j))],
        out_specs=pl.BlockSpec((tm, tn), lambda j, i: (i, j)),
        out_shape=jax.ShapeDtypeStruct((T, N), F32),
        scratch_shapes=[pltpu.VMEM((D, tn), BF16)],
        compiler_params=_params(("arbitrary", "arbitrary")),
        name="in_proj",
    )(xn, w)


def _qkv_prep_kernel(p_ref, pos_ref, invf_ref, sign_ref, gq_ref, gk_ref, q_ref, k_ref, v_ref):
    pos = pos_ref[...].astype(F32)
    ang = pos * invf_ref[...]
    cos = jnp.cos(ang)
    sin = jnp.sin(ang) * sign_ref[...]
    lane = lax.broadcasted_iota(jnp.int32, ang.shape, 1)
    low_head = lane < HEAD_DIM
    first_half = (lane % HEAD_DIM) < ROPE_HALF

    def norm_rope(x, g):
        x2 = x * x
        s_lo = jnp.sum(jnp.where(low_head, x2, 0.0), axis=-1, keepdims=True)
        s_hi = jnp.sum(jnp.where(low_head, 0.0, x2), axis=-1, keepdims=True)
        ssq = jnp.where(low_head, s_lo, s_hi)
        xn = (x * lax.rsqrt(ssq * (1.0 / HEAD_DIM) + EPS)) * g
        partner = jnp.where(first_half,
                            pltpu.roll(xn, LANES - ROPE_HALF, 1),
                            pltpu.roll(xn, ROPE_HALF, 1))
        return xn * cos + partner * sin

    for c in range(ATT_WIDTH // LANES):
        x = p_ref[:, OFF_Q + c * LANES:OFF_Q + (c + 1) * LANES]
        q_ref[:, c * LANES:(c + 1) * LANES] = (
            norm_rope(x, gq_ref[...]) * (HEAD_DIM ** -0.5)).astype(q_ref.dtype)
    for c in range(KV_WIDTH // LANES):
        x = p_ref[:, OFF_K + c * LANES:OFF_K + (c + 1) * LANES]
        k_ref[:, c * LANES:(c + 1) * LANES] = norm_rope(x, gk_ref[...]).astype(k_ref.dtype)
    v_ref[...] = p_ref[:, OFF_V:OFF_V + KV_WIDTH].astype(v_ref.dtype)


def _qkv_prep(proj, positions, q_g, k_g, tq=256):
    T = proj.shape[0]
    width = OFF_U
    lane = jnp.arange(LANES) % HEAD_DIM
    inv = ROPE_THETA ** (-jnp.arange(0, ROPE_DIM, 2, dtype=F32) / ROPE_DIM)
    invf = jnp.where(lane < ROPE_DIM, inv[lane % ROPE_HALF], 0.0).reshape(1, LANES)
    sign = jnp.where(lane < ROPE_HALF, -1.0, jnp.where(lane < ROPE_DIM, 1.0, 0.0))
    sign = sign.astype(F32).reshape(1, LANES)
    gq = jnp.tile(q_g, LANES // HEAD_DIM).reshape(1, LANES)
    gk = jnp.tile(k_g, LANES // HEAD_DIM).reshape(1, LANES)
    row = lambda i: (i, 0)
    const = lambda i: (0, 0)
    return pl.pallas_call(
        _qkv_prep_kernel,
        grid=(T // tq,),
        in_specs=[pl.BlockSpec((tq, width), row),
                  pl.BlockSpec((tq, 1), row),
                  pl.BlockSpec((1, LANES), const),
                  pl.BlockSpec((1, LANES), const),
                  pl.BlockSpec((1, LANES), const),
                  pl.BlockSpec((1, LANES), const)],
        out_specs=[pl.BlockSpec((tq, ATT_WIDTH), row),
                   pl.BlockSpec((tq, KV_WIDTH), row),
                   pl.BlockSpec((tq, KV_WIDTH), row)],
        out_shape=[jax.ShapeDtypeStruct((T, ATT_WIDTH), BF16),
                   jax.ShapeDtypeStruct((T, KV_WIDTH), BF16),
                   jax.ShapeDtypeStruct((T, KV_WIDTH), BF16)],
        compiler_params=_params(("parallel",)),
        name="qkv_prep",
    )(proj, positions.reshape(T, 1), invf, sign, gq, gk)


def _attn_kernel(sink_ref, q_ref, kp_ref, kc_ref, kn_ref, vp_ref, vc_ref, vn_ref, o_ref, *, nb):
    n = pl.program_id(1)
    rows = Q_PER_KV * ATT_BLOCK
    keys = 3 * ATT_BLOCK
    qi = lax.broadcasted_iota(jnp.int32, (rows, keys), 0) % ATT_BLOCK
    kj = lax.broadcasted_iota(jnp.int32, (rows, keys), 1)
    key_pos = kj + (n - 1) * ATT_BLOCK
    in_seq = jnp.logical_and(key_pos >= 0, key_pos < nb * ATT_BLOCK)
    mask = jnp.logical_and(jnp.abs(kj - ATT_BLOCK - qi) <= WINDOW, in_seq)
    row_head = lax.broadcasted_iota(jnp.int32, (rows, 1), 0) // ATT_BLOCK

    for kvh in range(ATT_KV_HEADS):
        cols = slice(kvh * HEAD_DIM, (kvh + 1) * HEAD_DIM)
        k = jnp.concatenate([kp_ref[:, cols], kc_ref[:, cols], kn_ref[:, cols]], axis=0)
        v = jnp.concatenate([vp_ref[:, cols], vc_ref[:, cols], vn_ref[:, cols]], axis=0)
        q = jnp.concatenate(
            [q_ref[:, (kvh * Q_PER_KV + g) * HEAD_DIM:(kvh * Q_PER_KV + g + 1) * HEAD_DIM]
             for g in range(Q_PER_KV)], axis=0)
        sink = jnp.zeros((rows, 1), F32)
        for g in range(Q_PER_KV):
            sink = jnp.where(row_head == g, sink_ref[kvh * Q_PER_KV + g], sink)
        s = lax.dot_general(q, k, (((1,), (1,)), ((), ())), preferred_element_type=F32)
        s = jnp.where(mask, s, NEG_INF)
        m = jnp.maximum(jnp.max(s, axis=-1, keepdims=True), sink)
        e = jnp.exp(s - m)
        denom = jnp.sum(e, axis=-1, keepdims=True) + jnp.exp(sink - m)
        p = (e / denom).astype(BF16)
        o = jnp.dot(p, v, preferred_element_type=F32)
        for g in range(Q_PER_KV):
            h = kvh * Q_PER_KV + g
            o_ref[:, h * HEAD_DIM:(h + 1) * HEAD_DIM] = (
                o[g * ATT_BLOCK:(g + 1) * ATT_BLOCK].astype(o_ref.dtype))


def _attention(q, k, v, sink, batch):
    T = q.shape[0]
    nb = T // batch // ATT_BLOCK
    cur = lambda b, n: (b * nb + n, 0)
    prev = lambda b, n: (b * nb + jnp.maximum(n - 1, 0), 0)
    nxt = lambda b, n: (b * nb + jnp.minimum(n + 1, nb - 1), 0)
    kv = lambda im: pl.BlockSpec((ATT_BLOCK, KV_WIDTH), im)
    return pl.pallas_call(
        functools.partial(_attn_kernel, nb=nb),
        grid=(batch, nb),
        in_specs=[pl.BlockSpec(memory_space=pltpu.SMEM),
                  pl.BlockSpec((ATT_BLOCK, ATT_WIDTH), cur),
                  kv(prev), kv(cur), kv(nxt), kv(prev), kv(cur), kv(nxt)],
        out_specs=pl.BlockSpec((ATT_BLOCK, ATT_WIDTH), cur),
        out_shape=jax.ShapeDtypeStruct((T, ATT_WIDTH), BF16),
        compiler_params=_params(("parallel", "parallel")),
        name="window_attn",
    )(sink, q, k, k, k, v, v, v)


def _sg_kernel(u_ref, s_ref, lng_ref, lnb_ref, w_ref, b_ref, o_ref):
    groups = w_ref.shape[0]
    for gi in range(groups):
        cols = slice(gi * SG_GROUP_DIM, (gi + 1) * SG_GROUP_DIM)
        s = jax.nn.gelu(s_ref[:, cols])
        mu = jnp.mean(s, axis=-1, keepdims=True)
        sc = s - mu
        var = jnp.mean(sc * sc, axis=-1, keepdims=True)
        sn = (sc * lax.rsqrt(var + EPS)) * lng_ref[0, gi:gi + 1, :] + lnb_ref[0, gi:gi + 1, :]
        mixed = jnp.dot(w_ref[gi], sn.astype(BF16), preferred_element_type=F32)
        mixed = mixed + b_ref[0, :, gi:gi + 1]
        o_ref[:, cols] = (jax.nn.gelu(u_ref[:, cols]) * mixed).astype(o_ref.dtype)


def _spatial_gating(proj, ln_g, ln_b, w_s, b_s, halves=2):
    T = proj.shape[0]
    half_w = SG_WIDTH // halves
    gph = SG_GROUPS // halves
    u0, s0 = OFF_U // half_w, OFF_S // half_w
    return pl.pallas_call(
        _sg_kernel,
        grid=(T // SG_CHUNK, halves),
        in_specs=[pl.BlockSpec((SG_CHUNK, half_w), lambda i, j: (i, u0 + j)),
                  pl.BlockSpec((SG_CHUNK, half_w), lambda i, j: (i, s0 + j)),
                  pl.BlockSpec((1, gph, SG_GROUP_DIM), lambda i, j: (j, 0, 0)),
                  pl.BlockSpec((1, gph, SG_GROUP_DIM), lambda i, j: (j, 0, 0)),
                  pl.BlockSpec((gph, SG_CHUNK, SG_CHUNK), lambda i, j: (j, 0, 0)),
                  pl.BlockSpec((1, SG_CHUNK, gph), lambda i, j: (j, 0, 0))],
        out_specs=pl.BlockSpec((SG_CHUNK, half_w), lambda i, j: (i, j)),
        out_shape=jax.ShapeDtypeStruct((T, SG_WIDTH), BF16),
        compiler_params=_params(("parallel", "parallel")),
        name="spatial_gating",
    )(proj, proj,
      ln_g.reshape(halves, gph, SG_GROUP_DIM), ln_b.reshape(halves, gph, SG_GROUP_DIM),
      w_s.astype(BF16),
      b_s.reshape(halves, gph, SG_CHUNK).transpose(0, 2, 1))


def _merge_kernel(att_ref, sgo_ref, ga_ref, gb_ref, wa_ref, wb_ref, o_ref):
    a = jnp.dot(att_ref[...], wa_ref[...], preferred_element_type=F32)
    b = jnp.dot(sgo_ref[...], wb_ref[...], preferred_element_type=F32)
    m = jax.nn.sigmoid(ga_ref[...]) * a + jax.nn.sigmoid(gb_ref[...]) * b
    o_ref[...] = m.astype(o_ref.dtype)


def _merge(att, sgo, proj, w_a, w_b, tm=512, tn=512):
    T = att.shape[0]
    ga0 = OFF_G // tn
    gb0 = (OFF_G + D_MODEL) // tn
    return pl.pallas_call(
        _merge_kernel,
        grid=(T // tm, D_MODEL // tn),
        in_specs=[pl.BlockSpec((tm, ATT_WIDTH), lambda i, j: (i, 0)),
                  pl.BlockSpec((tm, SG_WIDTH), lambda i, j: (i, 0)),
                  pl.BlockSpec((tm, tn), lambda i, j: (i, ga0 + j)),
                  pl.BlockSpec((tm, tn), lambda i, j: (i, gb0 + j)),
                  pl.BlockSpec((ATT_WIDTH, tn), lambda i, j: (0, j)),
                  pl.BlockSpec((SG_WIDTH, tn), lambda i, j: (0, j))],
        out_specs=pl.BlockSpec((tm, tn), lambda i, j: (i, j)),
        out_shape=jax.ShapeDtypeStruct((T, D_MODEL), BF16),
        compiler_params=_params(("parallel", "parallel")),
        name="merge",
    )(att, sgo, proj, proj, w_a.astype(BF16), w_b.astype(BF16))


def _route(logits):
    lane = lax.broadcasted_iota(jnp.int32, logits.shape, 1)
    lane_f = lane.astype(F32)
    is_g = lane < N_GROUPS
    gl = jnp.where(is_g, logits, -jnp.inf)
    gmax = jnp.max(gl, axis=-1, keepdims=True)
    grp = jnp.min(jnp.where(gl == gmax, lane_f, float(LANES)), axis=-1, keepdims=True)
    gsum = jnp.sum(jnp.where(is_g, jnp.exp(logits - gmax), 0.0), axis=-1, keepdims=True)
    g_w = 1.0 / gsum
    e_lane = lane - N_GROUPS
    in_grp = jnp.logical_and(
        jnp.logical_and(e_lane >= 0, e_lane < N_EXPERTS),
        (e_lane // EXPERTS_PER_GROUP).astype(F32) == grp)
    el = jnp.where(in_grp, logits, -jnp.inf)
    v1 = jnp.max(el, axis=-1, keepdims=True)
    i1 = jnp.min(jnp.where(jnp.logical_and(in_grp, el == v1), lane_f, float(LANES)),
                 axis=-1, keepdims=True)
    rest = jnp.logical_and(in_grp, lane_f != i1)
    el2 = jnp.where(rest, logits, -jnp.inf)
    v2 = jnp.max(el2, axis=-1, keepdims=True)
    i2 = jnp.min(jnp.where(jnp.logical_and(rest, el2 == v2), lane_f, float(LANES)),
                 axis=-1, keepdims=True)
    e21 = jnp.exp(v2 - v1)
    w1 = g_w / (1.0 + e21)
    w2 = g_w * e21 / (1.0 + e21)
    idx = jnp.where(lane == 0, i1, i2) - float(N_GROUPS)
    wts = jnp.where(lane == 0, w1, jnp.where(lane == 1, w2, 0.0))
    return idx.astype(jnp.int32), wts


def _out_kernel(m_ref, w_ref, x_ref, g_ref, wr_ref, br_ref, h_ref, hn_ref, idx_ref, wt_ref):
    h = x_ref[...] + jnp.dot(m_ref[...], w_ref[...], preferred_element_type=F32)
    h_ref[...] = h
    r = lax.rsqrt(jnp.mean(h * h, axis=-1, keepdims=True) + EPS)
    hn = ((h * r) * g_ref[...]).astype(BF16)
    bits = lax.bitcast_convert_type(hn.astype(F32), jnp.uint32)
    for c in range(HN_WORD_CHUNKS):
        hi = bits[:, c * LANES:(c + 1) * LANES]
        lo = bits[:, (c + HN_WORD_CHUNKS) * LANES:(c + HN_WORD_CHUNKS + 1) * LANES]
        hn_ref[pl.ds(c, hn.shape[0], stride=HN_WORD_CHUNKS), :] = hi | (lo >> 16)
    logits = jnp.dot(hn, wr_ref[...], preferred_element_type=F32) + br_ref[...]
    idx, wts = _route(logits)
    idx_ref[...] = idx
    wt_ref[...] = wts


def _out_proj(merged, w_out, x, g2, w_router, b_router, tm=256):
    T, D = x.shape
    row = lambda i: (i, 0)
    const = lambda i: (0, 0)
    return pl.pallas_call(
        _out_kernel,
        grid=(T // tm,),
        in_specs=[pl.BlockSpec((tm, D), row),
                  pl.BlockSpec((D, D), const),
                  pl.BlockSpec((tm, D), row),
                  pl.BlockSpec((1, D), const),
                  pl.BlockSpec((D, LANES), const),
                  pl.BlockSpec((1, LANES), const)],
        out_specs=[pl.BlockSpec((tm, D), row),
                   pl.BlockSpec((tm * HN_WORD_CHUNKS, LANES), row),
                   pl.BlockSpec((tm, LANES), row),
                   pl.BlockSpec((tm, LANES), row)],
        out_shape=[jax.ShapeDtypeStruct((T, D), F32),
                   jax.ShapeDtypeStruct((T * HN_WORD_CHUNKS, LANES), jnp.uint32),
                   jax.ShapeDtypeStruct((T, LANES), jnp.int32),
                   jax.ShapeDtypeStruct((T, LANES), F32)],
        compiler_params=_params(("parallel",)),
        name="out_proj_router",
    )(merged, w_out.astype(BF16), x, g2.reshape(1, D), w_router, b_router)


def _expert_kernel(be_ref, eo_ref, ue_ref, nu_ref, rtc_ref, rtn_ref, hn_hbm, wg_hbm, wu_hbm, wd_hbm,
                   y_ref, xbuf, wgf, wuf, wdf, wgb, wub, wdb, sem, wsem):
    b = pl.program_id(0)
    n_used = nu_ref[0]
    n_exp = nu_ref[1]
    used = b < n_used
    slot = b % 2

    def weight_copies(ordinal, slot_):
        e = ue_ref[ordinal]
        return (pltpu.make_async_copy(wg_hbm.at[e], wgf.at[slot_], wsem.at[slot_]),
                pltpu.make_async_copy(wu_hbm.at[e], wuf.at[slot_], wsem.at[slot_]),
                pltpu.make_async_copy(wd_hbm.at[e], wdf.at[slot_], wsem.at[slot_]))

    def start_weights(ordinal):
        for c in weight_copies(ordinal, ordinal % 2):
            c.start(priority=1)

    def gather_rows(tok_ref, slot_):
        def issue(g, c):
            for j in range(ROWS_PER_ISSUE):
                r = g * ROWS_PER_ISSUE + j
                src = pl.multiple_of(tok_ref[0, 0, r] * HN_WORD_CHUNKS, HN_WORD_CHUNKS)
                dst = pl.multiple_of(r * HN_WORD_CHUNKS, HN_WORD_CHUNKS)
                pltpu.make_async_copy(hn_hbm.at[pl.ds(src, HN_WORD_CHUNKS)],
                                      xbuf.at[slot_, pl.ds(dst, HN_WORD_CHUNKS)],
                                      sem.at[slot_]).start()
            return c

        lax.fori_loop(0, MOE_BLOCK // ROWS_PER_ISSUE, issue, 0)

    @pl.when(b == 0)
    def _():
        start_weights(0)

        @pl.when(n_exp > 1)
        def _():
            start_weights(1)

        gather_rows(rtc_ref, 0)

    @pl.when(b + 1 < n_used)
    def _():
        gather_rows(rtn_ref, 1 - slot)

    @pl.when(used)
    def _():
        first = jnp.logical_or(b == 0, be_ref[b] != be_ref[jnp.maximum(b - 1, 0)])

        @pl.when(first)
        def _():
            ordinal = eo_ref[b]
            ws = ordinal % 2
            for c in weight_copies(ordinal, ws):
                c.wait()
            wgb[...] = wgf[ws].astype(BF16)
            wub[...] = wuf[ws].astype(BF16)
            wdb[...] = wdf[ws].astype(BF16)

            @pl.when(ordinal + 2 < n_exp)
            def _():
                start_weights(ordinal + 2)

        pltpu.make_async_copy(hn_hbm.at[pl.ds(0, MOE_BLOCK * HN_WORD_CHUNKS)], xbuf.at[slot],
                              sem.at[slot]).wait()

        halves = ([], [])
        for c in range(HN_WORD_CHUNKS):
            w = xbuf[slot, pl.ds(c, MOE_BLOCK, stride=HN_WORD_CHUNKS), :]
            hi = lax.bitcast_convert_type(w & jnp.uint32(0xFFFF0000), F32)
            lo = lax.bitcast_convert_type(w << 16, F32)
            halves[0].append(hi.astype(BF16))
            halves[1].append(lo.astype(BF16))
        x = jnp.concatenate(halves[0] + halves[1], axis=1)
        hg = jnp.dot(x, wgb[...], preferred_element_type=F32)
        hu = jnp.dot(x, wub[...], preferred_element_type=F32)
        hdn = (jax.nn.silu(hg) * hu).astype(BF16)
        y_ref[...] = jnp.dot(hdn, wdb[...], preferred_element_type=F32)

    @pl.when(jnp.logical_not(used))
    def _():
        y_ref[...] = jnp.zeros(y_ref.shape, y_ref.dtype)


def _experts(hn, w_gate, w_up, w_down, block_e, block_ord, used_experts, row_tok, n_used):
    D = w_gate.shape[1]
    n_rows = row_tok.shape[0]
    n_blocks = n_rows // MOE_BLOCK
    tok3 = row_tok.reshape(n_blocks, 1, MOE_BLOCK)
    hbm = pl.BlockSpec(memory_space=pl.ANY)
    tok_block = lambda im: pl.BlockSpec((1, 1, MOE_BLOCK), im, memory_space=pltpu.SMEM)
    grid_spec = pltpu.PrefetchScalarGridSpec(
        num_scalar_prefetch=4,
        grid=(n_blocks,),
        in_specs=[tok_block(lambda b, *_: (b, 0, 0)),
                  tok_block(lambda b, *_: (jnp.minimum(b + 1, n_blocks - 1), 0, 0)),
                  hbm, hbm, hbm, hbm],
        out_specs=pl.BlockSpec((MOE_BLOCK, D), lambda b, *_: (b, 0)),
        scratch_shapes=[pltpu.VMEM((2, MOE_BLOCK * HN_WORD_CHUNKS, LANES), jnp.uint32),
                        pltpu.VMEM((2, D, EXPERT_FF), F32),
                        pltpu.VMEM((2, D, EXPERT_FF), F32),
                        pltpu.VMEM((2, EXPERT_FF, D), F32),
                        pltpu.VMEM((D, EXPERT_FF), BF16),
                        pltpu.VMEM((D, EXPERT_FF), BF16),
                        pltpu.VMEM((EXPERT_FF, D), BF16),
                        pltpu.SemaphoreType.DMA((2,)),
                        pltpu.SemaphoreType.DMA((2,))],
    )
    return pl.pallas_call(
        _expert_kernel,
        grid_spec=grid_spec,
        out_shape=jax.ShapeDtypeStruct((n_rows, D), F32),
        compiler_params=_params(("arbitrary",)),
        name="experts",
    )(block_e, block_ord, used_experts, n_used, tok3, tok3, hn, w_gate, w_up, w_down)


def _combine_kernel(dc_ref, dn_ref, y_hbm, h_ref, wt_ref, o_ref, ybuf, sem, *, tc):
    i = pl.program_id(0)
    slot = i % 2

    def gather_rows(dest_ref, slot_):
        def issue(r, c):
            for k in range(TOP_K):
                src = dest_ref[0, 0, r * TOP_K + k]
                pltpu.make_async_copy(y_hbm.at[pl.ds(src, 1)],
                                      ybuf.at[slot_, k, pl.ds(r, 1)],
                                      sem.at[slot_]).start(priority=k % 2)
            return c

        lax.fori_loop(0, tc, issue, 0, unroll=4)

    @pl.when(i == 0)
    def _():
        gather_rows(dc_ref, 0)

    @pl.when(i + 1 < pl.num_programs(0))
    def _():
        gather_rows(dn_ref, 1 - slot)

    for k in range(TOP_K):
        pltpu.make_async_copy(y_hbm.at[pl.ds(0, tc)], ybuf.at[slot, k], sem.at[slot]).wait()
    wt = wt_ref[...]
    o_ref[...] = h_ref[...] + (wt[:, 0:1] * ybuf[slot, 0] + wt[:, 1:2] * ybuf[slot, 1])


def _combine(yrows, h, wts, dest, tc=128):
    T, D = h.shape
    steps = T // tc
    row = lambda i: (i, 0)
    dest3 = dest.reshape(steps, 1, tc * TOP_K)
    dest_block = lambda im: pl.BlockSpec((1, 1, tc * TOP_K), im, memory_space=pltpu.SMEM)
    return pl.pallas_call(
        functools.partial(_combine_kernel, tc=tc),
        grid=(steps,),
        in_specs=[dest_block(lambda i: (i, 0, 0)),
                  dest_block(lambda i: (jnp.minimum(i + 1, steps - 1), 0, 0)),
                  pl.BlockSpec(memory_space=pl.ANY),
                  pl.BlockSpec((tc, D), row),
                  pl.BlockSpec((tc, LANES), row)],
        out_specs=pl.BlockSpec((tc, D), row),
        out_shape=jax.ShapeDtypeStruct((T, D), F32),
        scratch_shapes=[pltpu.VMEM((2, TOP_K, tc, D), F32),
                        pltpu.SemaphoreType.DMA((2,))],
        compiler_params=_params(("arbitrary",)),
        name="combine",
    )(dest3, dest3, yrows, h, wts)


SUBLANES = 8
META_ROWS = 256


def _lane_cumsum(x):
    lane = lax.broadcasted_iota(jnp.int32, x.shape, 1)
    s = 1
    while s < LANES:
        x = x + jnp.where(lane >= s, pltpu.roll(x, s, 1), 0)
        s *= 2
    return x


def _dispatch_kernel(idx_ref, dest_ref, meta_ref, run_ref, prefix_ref, start_ref, *, tb):
    p = pl.program_id(0)
    i = pl.program_id(1)
    idx = idx_ref[...]
    lane = lax.broadcasted_iota(jnp.int32, idx.shape, 1)
    e1 = idx[:, 0:1]
    e2 = idx[:, 1:2]
    onehot = jnp.where(jnp.logical_or(lane == e1, lane == e2), 1.0, 0.0)

    @pl.when(jnp.logical_and(p == 0, i == 0))
    def _():
        run_ref[...] = jnp.zeros(run_ref.shape, F32)

    @pl.when(p == 0)
    def _():
        prefix_ref[i] = run_ref[...]
        run_ref[...] = run_ref[...] + jnp.sum(onehot, axis=0, keepdims=True)

    @pl.when(jnp.logical_and(p == 1, i == 0))
    def _():
        counts = run_ref[...].astype(jnp.int32)
        nblk = (counts + (MOE_BLOCK - 1)) // MOE_BLOCK
        end_blk = _lane_cumsum(nblk)
        start_ref[...] = ((end_blk - nblk) * MOE_BLOCK).astype(F32)
        has = jnp.where(counts > 0, 1, 0)
        ordinal = _lane_cumsum(has) - 1
        end_blk, has, ordinal = end_blk[0:1], has[0:1], ordinal[0:1]
        rows = lax.broadcasted_iota(jnp.int32, (META_ROWS, LANES), 0)
        lanes = lax.broadcasted_iota(jnp.int32, (META_ROWS, LANES), 1)
        is_e = lanes < N_EXPERTS
        rsum = lambda v: jnp.sum(v, axis=-1, keepdims=True)
        be = rsum(jnp.where(jnp.logical_and(is_e, end_blk <= rows), 1, 0))
        be = jnp.minimum(be, N_EXPERTS - 1)
        eo = rsum(jnp.where(lanes == be, ordinal, 0))
        ue = rsum(jnp.where(jnp.logical_and(has > 0, ordinal == rows), lanes, 0))
        n_blk = rsum(jnp.where(lanes == N_EXPERTS - 1, end_blk, 0))
        n_exp = rsum(jnp.where(is_e, has, 0))
        meta_ref[...] = jnp.where(
            lanes == 0, be, jnp.where(lanes == 1, eo, jnp.where(
                lanes == 2, ue, jnp.where(lanes == 3, n_blk, n_exp))))

    @pl.when(p == 1)
    def _():
        r = lax.broadcasted_iota(jnp.int32, (tb, tb), 0)
        c = lax.broadcasted_iota(jnp.int32, (tb, tb), 1)
        earlier = jnp.where(c < r, 1.0, 0.0).astype(BF16)
        rank = jnp.dot(earlier, onehot.astype(BF16), preferred_element_type=F32)
        rank = rank + prefix_ref[i][0:1] + start_ref[0:1]
        d1 = jnp.sum(jnp.where(lane == e1, rank, 0.0), axis=-1, keepdims=True)
        d2 = jnp.sum(jnp.where(lane == e2, rank, 0.0), axis=-1, keepdims=True)
        dest_ref[...] = jnp.where(lane == 0, d1, d2).astype(jnp.int32)


def _dispatch(idx, tb=256):
    T = idx.shape[0]
    n_rows = T * TOP_K + N_EXPERTS * MOE_BLOCK
    n_blocks = n_rows // MOE_BLOCK
    assert n_blocks <= META_ROWS
    dest2, meta = pl.pallas_call(
        functools.partial(_dispatch_kernel, tb=tb),
        grid=(2, T // tb),
        in_specs=[pl.BlockSpec((tb, LANES), lambda p, i: (i, 0))],
        out_specs=[pl.BlockSpec((tb, LANES), lambda p, i: (i * p, 0)),
                   pl.BlockSpec((META_ROWS, LANES), lambda p, i: (0, 0))],
        out_shape=[jax.ShapeDtypeStruct((T, LANES), jnp.int32),
                   jax.ShapeDtypeStruct((META_ROWS, LANES), jnp.int32)],
        scratch_shapes=[pltpu.VMEM((SUBLANES, LANES), F32),
                        pltpu.VMEM((T // tb, SUBLANES, LANES), F32),
                        pltpu.VMEM((SUBLANES, LANES), F32)],
        compiler_params=_params(("arbitrary", "arbitrary")),
        name="dispatch",
    )(idx)
    dest = dest2[:, :TOP_K].reshape(T * TOP_K)
    tok = jnp.repeat(jnp.arange(T, dtype=jnp.int32), TOP_K)
    row_tok = jnp.zeros((n_rows,), jnp.int32).at[dest].set(tok)
    block_e = meta[:n_blocks, 0]
    block_ord = meta[:n_blocks, 1]
    used_experts = meta[:N_EXPERTS, 2]
    n_used = meta[0, 3:5]
    return block_e, block_ord, used_experts, row_tok, n_used, dest


def kernel(x, positions, norm1_g, w_in, q_norm_g, k_norm_g, sink_logits, sg_ln_g, sg_ln_b, sg_w, sg_b, w_branch_att, w_branch_sg, w_out, norm2_g, w_group_router, b_group_router, w_expert_router, b_expert_router, w_gate, w_up, w_down):
    B, S, D = x.shape
    T = B * S
    h = x.reshape(T, D)
    pos = positions.reshape(T)
    for l in range(norm1_g.shape[0]):
        xn = _rmsnorm(h, norm1_g[l])
        proj = _in_proj(xn, w_in[l])
        q, k, v = _qkv_prep(proj, pos, q_norm_g[l], k_norm_g[l])
        att = _attention(q, k, v, sink_logits[l], B)
        sgo = _spatial_gating(proj, sg_ln_g[l], sg_ln_b[l], sg_w[l], sg_b[l])
        merged = _merge(att, sgo, proj, w_branch_att[l], w_branch_sg[l])
        pad = LANES - N_GROUPS - N_EXPERTS
        w_router = jnp.concatenate(
            [w_group_router[l], w_expert_router[l], jnp.zeros((D, pad), F32)], axis=1).astype(BF16)
        b_router = jnp.concatenate(
            [b_group_router[l], b_expert_router[l], jnp.zeros((pad,), F32)]).reshape(1, LANES)
        h, hn, idx, wts = _out_proj(merged, w_out[l], h, norm2_g[l], w_router, b_router)
        block_e, block_ord, used_experts, row_tok, n_used, dest = _dispatch(idx)
        yrows = _experts(hn, w_gate[l], w_up[l], w_down[l],
                         block_e, block_ord, used_experts, row_tok, n_used)
        h = _combine(yrows, h, wts, dest)
    return h.reshape(B, S, D)
```

```python
import functools

import jax
import jax.numpy as jnp
from jax import lax
from jax.experimental import pallas as pl
from jax.experimental.pallas import tpu as pltpu

F32 = jnp.float32
BF16 = jnp.bfloat16

D_MODEL = 2048
HEAD_DIM = 64
ATT_WIDTH = D_MODEL // 2
ATT_HEADS = ATT_WIDTH // HEAD_DIM
ATT_KV_HEADS = ATT_HEADS // 4
Q_PER_KV = ATT_HEADS // ATT_KV_HEADS
KV_WIDTH = ATT_KV_HEADS * HEAD_DIM
WINDOW = 128
ATT_BLOCK = 128
ROPE_DIM = HEAD_DIM // 4
ROPE_HALF = ROPE_DIM // 2
ROPE_THETA = 500000.0
SG_WIDTH = D_MODEL // 2
SG_GROUP_DIM = 128
SG_GROUPS = SG_WIDTH // SG_GROUP_DIM
SG_CHUNK = 128
OFF_Q = 0
OFF_K = OFF_Q + ATT_WIDTH
OFF_V = OFF_K + KV_WIDTH
OFF_U = OFF_V + KV_WIDTH
OFF_S = OFF_U + SG_WIDTH
OFF_G = OFF_S + SG_WIDTH
IN_COLS = OFF_G + 2 * D_MODEL
N_GROUPS = 8
EXPERTS_PER_GROUP = 8
N_EXPERTS = N_GROUPS * EXPERTS_PER_GROUP
TOP_K = 2
EXPERT_FF = D_MODEL // 4
MOE_BLOCK = 128
EPS = 1e-6
NEG_INF = -1e30

LANES = 128
HN_WORD_CHUNKS = D_MODEL // LANES // 2
ROWS_PER_ISSUE = 8
VMEM_LIMIT = 56 * 1024 * 1024


def _params(sem, vmem=VMEM_LIMIT):
    return pltpu.CompilerParams(dimension_semantics=sem, vmem_limit_bytes=vmem)


def _rmsnorm_kernel(x_ref, g_ref, o_ref):
    x = x_ref[...]
    r = lax.rsqrt(jnp.mean(x * x, axis=-1, keepdims=True) + EPS)
    o_ref[...] = ((x * r) * g_ref[...]).astype(o_ref.dtype)


def _rmsnorm(x, g, tm=512):
    T, D = x.shape
    return pl.pallas_call(
        _rmsnorm_kernel,
        grid=(T // tm,),
        in_specs=[pl.BlockSpec((tm, D), lambda i: (i, 0)),
                  pl.BlockSpec((1, D), lambda i: (0, 0))],
        out_specs=pl.BlockSpec((tm, D), lambda i: (i, 0)),
        out_shape=jax.ShapeDtypeStruct((T, D), BF16),
        compiler_params=_params(("parallel",)),
        name="norm1",
    )(x, g.reshape(1, D))


def _proj_kernel(x_ref, w_ref, o_ref, wbf_ref):
    @pl.when(pl.program_id(1) == 0)
    def _():
        wbf_ref[...] = w_ref[...].astype(BF16)

    o_ref[...] = jnp.dot(x_ref[...], wbf_ref[...], preferred_element_type=F32)


def _in_proj(xn, w, tm=512, tn=1280):
    T, D = xn.shape
    N = w.shape[1]
    return pl.pallas_call(
        _proj_kernel,
        grid=(N // tn, T // tm),
        in_specs=[pl.BlockSpec((tm, D), lambda j, i: (i, 0)),
                  pl.BlockSpec((D, tn), lambda j, i: (0, j))],
        out_specs=pl.BlockSpec((tm, tn), lambda j, i: (i, j)),
        out_shape=jax.ShapeDtypeStruct((T, N), F32),
        scratch_shapes=[pltpu.VMEM((D, tn), BF16)],
        compiler_params=_params(("arbitrary", "arbitrary")),
        name="in_proj",
    )(xn, w)


def _rope_table_kernel(pos_ref, invf_ref, cos_ref, sin_ref):
    ang = pos_ref[...].astype(F32) * invf_ref[...]
    cos_ref[...] = jnp.cos(ang)
    sin_ref[...] = jnp.sin(ang)


def _rope_tables(positions):
    T = positions.shape[0]
    rows = T * ROPE_HALF // LANES
    inv = ROPE_THETA ** (-jnp.arange(0, ROPE_DIM, 2, dtype=F32) / ROPE_DIM)
    invf = jnp.tile(inv, LANES // ROPE_HALF).reshape(1, LANES)
    pos = jnp.repeat(positions, ROPE_HALF).reshape(rows, LANES)
    whole = lambda: (0, 0)
    cos, sin = pl.pallas_call(
        _rope_table_kernel,
        in_specs=[pl.BlockSpec((rows, LANES), whole), pl.BlockSpec((1, LANES), whole)],
        out_specs=[pl.BlockSpec((rows, LANES), whole), pl.BlockSpec((rows, LANES), whole)],
        out_shape=[jax.ShapeDtypeStruct((rows, LANES), F32)] * 2,
        name="rope_tables",
    )(pos, invf)
    return cos.reshape(T, ROPE_HALF), sin.reshape(T, ROPE_HALF)


def _qkv_prep_kernel(p_ref, cos_ref, sin_ref, gq_ref, gk_ref, q_ref, k_ref, v_ref):
    cos = cos_ref[...]
    sin = sin_ref[...]
    lane = lax.broadcasted_iota(jnp.int32, cos.shape, 1)
    low_head = lane < HEAD_DIM
    first_half = (lane % HEAD_DIM) < ROPE_HALF

    def norm_rope(x, g):
        x2 = x * x
        s_lo = jnp.sum(jnp.where(low_head, x2, 0.0), axis=-1, keepdims=True)
        s_hi = jnp.sum(jnp.where(low_head, 0.0, x2), axis=-1, keepdims=True)
        ssq = jnp.where(low_head, s_lo, s_hi)
        xn = (x * lax.rsqrt(ssq * (1.0 / HEAD_DIM) + EPS)) * g
        partner = jnp.where(first_half,
                            pltpu.roll(xn, LANES - ROPE_HALF, 1),
                            pltpu.roll(xn, ROPE_HALF, 1))
        return xn * cos + partner * sin

    for c in range(ATT_WIDTH // LANES):
        x = p_ref[:, OFF_Q + c * LANES:OFF_Q + (c + 1) * LANES]
        q_ref[:, c * LANES:(c + 1) * LANES] = (
            norm_rope(x, gq_ref[...]) * (HEAD_DIM ** -0.5)).astype(q_ref.dtype)
    for c in range(KV_WIDTH // LANES):
        x = p_ref[:, OFF_K + c * LANES:OFF_K + (c + 1) * LANES]
        k_ref[:, c * LANES:(c + 1) * LANES] = norm_rope(x, gk_ref[...]).astype(k_ref.dtype)
    v_ref[...] = p_ref[:, OFF_V:OFF_V + KV_WIDTH].astype(v_ref.dtype)


def _qkv_prep(proj, positions, q_g, k_g, tq=256):
    T = proj.shape[0]
    width = OFF_U
    cos8, sin8 = _rope_tables(positions)
    rest = HEAD_DIM - ROPE_DIM
    cos_t = jnp.tile(jnp.concatenate([cos8, cos8, jnp.ones((T, rest), F32)], axis=1),
                     (1, LANES // HEAD_DIM))
    sin_t = jnp.tile(jnp.concatenate([-sin8, sin8, jnp.zeros((T, rest), F32)], axis=1),
                     (1, LANES // HEAD_DIM))
    gq = jnp.tile(q_g, LANES // HEAD_DIM).reshape(1, LANES)
    gk = jnp.tile(k_g, LANES // HEAD_DIM).reshape(1, LANES)
    row = lambda i: (i, 0)
    const = lambda i: (0, 0)
    return pl.pallas_call(
        _qkv_prep_kernel,
        grid=(T // tq,),
        in_specs=[pl.BlockSpec((tq, width), row),
                  pl.BlockSpec((tq, LANES), row),
                  pl.BlockSpec((tq, LANES), row),
                  pl.BlockSpec((1, LANES), const),
                  pl.BlockSpec((1, LANES), const)],
        out_specs=[pl.BlockSpec((tq, ATT_WIDTH), row),
                   pl.BlockSpec((tq, KV_WIDTH), row),
                   pl.BlockSpec((tq, KV_WIDTH), row)],
        out_shape=[jax.ShapeDtypeStruct((T, ATT_WIDTH), BF16),
                   jax.ShapeDtypeStruct((T, KV_WIDTH), BF16),
                   jax.ShapeDtypeStruct((T, KV_WIDTH), BF16)],
        compiler_params=_params(("parallel",)),
        name="qkv_prep",
    )(proj, cos_t, sin_t, gq, gk)


def _attn_kernel(sink_ref, q_ref, kp_ref, kc_ref, kn_ref, vp_ref, vc_ref, vn_ref, o_ref, *, nb):
    n = pl.program_id(1)
    rows = Q_PER_KV * ATT_BLOCK
    keys = 3 * ATT_BLOCK
    qi = lax.broadcasted_iota(jnp.int32, (rows, ATT_BLOCK), 0) % ATT_BLOCK
    kj = lax.broadcasted_iota(jnp.int32, (rows, ATT_BLOCK), 1)
    lo_prev = jnp.where(n > 0, 0, ATT_BLOCK)
    hi_next = jnp.where(n < nb - 1, 0, -ATT_BLOCK)
    cap_prev = jnp.where(kj - qi >= lo_prev, jnp.inf, NEG_INF)
    cap_next = jnp.where(kj - qi <= hi_next, jnp.inf, NEG_INF)
    cap = jnp.concatenate([cap_prev, jnp.full((rows, ATT_BLOCK), jnp.inf, F32), cap_next], axis=1)
    row_head = lax.broadcasted_iota(jnp.int32, (rows, 1), 0) // ATT_BLOCK

    for kvh in range(ATT_KV_HEADS):
        cols = slice(kvh * HEAD_DIM, (kvh + 1) * HEAD_DIM)
        k = jnp.concatenate([kp_ref[:, cols], kc_ref[:, cols], kn_ref[:, cols]], axis=0)
        v = jnp.concatenate([vp_ref[:, cols], vc_ref[:, cols], vn_ref[:, cols]], axis=0)
        q = jnp.concatenate(
            [q_ref[:, (kvh * Q_PER_KV + g) * HEAD_DIM:(kvh * Q_PER_KV + g + 1) * HEAD_DIM]
             for g in range(Q_PER_KV)], axis=0)
        sink = jnp.zeros((rows, 1), F32)
        for g in range(Q_PER_KV):
            sink = jnp.where(row_head == g, sink_ref[kvh * Q_PER_KV + g], sink)
        s = lax.dot_general(q, k, (((1,), (1,)), ((), ())), preferred_element_type=F32)
        s = jnp.minimum(s, cap)
        m = jnp.maximum(jnp.max(s, axis=-1, keepdims=True), sink)
        e = jnp.exp(s - m)
        denom = jnp.sum(e, axis=-1, keepdims=True) + jnp.exp(sink - m)
        p = (e / denom).astype(BF16)
        o = jnp.dot(p, v, preferred_element_type=F32)
        for g in range(Q_PER_KV):
            h = kvh * Q_PER_KV + g
            o_ref[:, h * HEAD_DIM:(h + 1) * HEAD_DIM] = (
                o[g * ATT_BLOCK:(g + 1) * ATT_BLOCK].astype(o_ref.dtype))


def _attention(q, k, v, sink, batch):
    T = q.shape[0]
    nb = T // batch // ATT_BLOCK
    cur = lambda b, n: (b * nb + n, 0)
    prev = lambda b, n: (b * nb + jnp.maximum(n - 1, 0), 0)
    nxt = lambda b, n: (b * nb + jnp.minimum(n + 1, nb - 1), 0)
    kv = lambda im: pl.BlockSpec((ATT_BLOCK, KV_WIDTH), im)
    return pl.pallas_call(
        functools.partial(_attn_kernel, nb=nb),
        grid=(batch, nb),
        in_specs=[pl.BlockSpec(memory_space=pltpu.SMEM),
                  pl.BlockSpec((ATT_BLOCK, ATT_WIDTH), cur),
                  kv(prev), kv(cur), kv(nxt), kv(prev), kv(cur), kv(nxt)],
        out_specs=pl.BlockSpec((ATT_BLOCK, ATT_WIDTH), cur),
        out_shape=jax.ShapeDtypeStruct((T, ATT_WIDTH), BF16),
        compiler_params=_params(("parallel", "parallel")),
        name="window_attn",
    )(sink, q, k, k, k, v, v, v)


def _sg_kernel(u_ref, s_ref, lng_ref, lnb_ref, w_ref, b_ref, o_ref):
    groups = w_ref.shape[0]

    def one_chunk(ci, carry):
        rows = pl.ds(pl.multiple_of(ci * SG_CHUNK, SG_CHUNK), SG_CHUNK)
        for gi in range(groups):
            cols = slice(gi * SG_GROUP_DIM, (gi + 1) * SG_GROUP_DIM)
            s = jax.nn.gelu(s_ref[rows, cols])
            mu = jnp.mean(s, axis=-1, keepdims=True)
            sc = s - mu
            var = jnp.mean(sc * sc, axis=-1, keepdims=True)
            sn = (sc * lax.rsqrt(var + EPS)) * lng_ref[0, gi:gi + 1, :] + lnb_ref[0, gi:gi + 1, :]
            mixed = jnp.dot(w_ref[gi], sn.astype(BF16), preferred_element_type=F32)
            mixed = mixed + b_ref[0, :, gi:gi + 1]
            o_ref[rows, cols] = (jax.nn.gelu(u_ref[rows, cols]) * mixed).astype(o_ref.dtype)
        return carry

    lax.fori_loop(0, u_ref.shape[0] // SG_CHUNK, one_chunk, 0)


def _spatial_gating(proj, ln_g, ln_b, w_s, b_s, halves=2, chunks=4):
    T = proj.shape[0]
    half_w = SG_WIDTH // halves
    gph = SG_GROUPS // halves
    u0, s0 = OFF_U // half_w, OFF_S // half_w
    tq = chunks * SG_CHUNK
    return pl.pallas_call(
        _sg_kernel,
        grid=(T // tq, halves),
        in_specs=[pl.BlockSpec((tq, half_w), lambda i, j: (i, u0 + j)),
                  pl.BlockSpec((tq, half_w), lambda i, j: (i, s0 + j)),
                  pl.BlockSpec((1, gph, SG_GROUP_DIM), lambda i, j: (j, 0, 0)),
                  pl.BlockSpec((1, gph, SG_GROUP_DIM), lambda i, j: (j, 0, 0)),
                  pl.BlockSpec((gph, SG_CHUNK, SG_CHUNK), lambda i, j: (j, 0, 0)),
                  pl.BlockSpec((1, SG_CHUNK, gph), lambda i, j: (j, 0, 0))],
        out_specs=pl.BlockSpec((tq, half_w), lambda i, j: (i, j)),
        out_shape=jax.ShapeDtypeStruct((T, SG_WIDTH), BF16),
        compiler_params=_params(("parallel", "parallel")),
        name="spatial_gating",
    )(proj, proj,
      ln_g.reshape(halves, gph, SG_GROUP_DIM), ln_b.reshape(halves, gph, SG_GROUP_DIM),
      w_s.astype(BF16),
      b_s.reshape(halves, gph, SG_CHUNK).transpose(0, 2, 1))


def _merge_kernel(att_ref, sgo_ref, ga_ref, gb_ref, wa_ref, wb_ref, o_ref):
    a = jnp.dot(att_ref[...], wa_ref[...], preferred_element_type=F32)
    b = jnp.dot(sgo_ref[...], wb_ref[...], preferred_element_type=F32)
    m = jax.nn.sigmoid(ga_ref[...]) * a + jax.nn.sigmoid(gb_ref[...]) * b
    o_ref[...] = m.astype(o_ref.dtype)


def _merge(att, sgo, proj, w_a, w_b, tm=1024, tn=512):
    T = att.shape[0]
    ga0 = OFF_G // tn
    gb0 = (OFF_G + D_MODEL) // tn
    return pl.pallas_call(
        _merge_kernel,
        grid=(T // tm, D_MODEL // tn),
        in_specs=[pl.BlockSpec((tm, ATT_WIDTH), lambda i, j: (i, 0)),
                  pl.BlockSpec((tm, SG_WIDTH), lambda i, j: (i, 0)),
                  pl.BlockSpec((tm, tn), lambda i, j: (i, ga0 + j)),
                  pl.BlockSpec((tm, tn), lambda i, j: (i, gb0 + j)),
                  pl.BlockSpec((ATT_WIDTH, tn), lambda i, j: (0, j)),
                  pl.BlockSpec((SG_WIDTH, tn), lambda i, j: (0, j))],
        out_specs=pl.BlockSpec((tm, tn), lambda i, j: (i, j)),
        out_shape=jax.ShapeDtypeStruct((T, D_MODEL), BF16),
        compiler_params=_params(("parallel", "parallel")),
        name="merge",
    )(att, sgo, proj, proj, w_a.astype(BF16), w_b.astype(BF16))


def _route(logits):
    lane = lax.broadcasted_iota(jnp.int32, logits.shape, 1)
    lane_f = lane.astype(F32)
    is_g = lane < N_GROUPS
    gl = jnp.where(is_g, logits, -jnp.inf)
    gmax = jnp.max(gl, axis=-1, keepdims=True)
    grp = jnp.min(jnp.where(gl == gmax, lane_f, float(LANES)), axis=-1, keepdims=True)
    gsum = jnp.sum(jnp.where(is_g, jnp.exp(logits - gmax), 0.0), axis=-1, keepdims=True)
    g_w = 1.0 / gsum
    e_lane = lane - N_GROUPS
    in_grp = jnp.logical_and(
        jnp.logical_and(e_lane >= 0, e_lane < N_EXPERTS),
        (e_lane // EXPERTS_PER_GROUP).astype(F32) == grp)
    el = jnp.where(in_grp, logits, -jnp.inf)
    v1 = jnp.max(el, axis=-1, keepdims=True)
    i1 = jnp.min(jnp.where(jnp.logical_and(in_grp, el == v1), lane_f, float(LANES)),
                 axis=-1, keepdims=True)
    rest = jnp.logical_and(in_grp, lane_f != i1)
    el2 = jnp.where(rest, logits, -jnp.inf)
    v2 = jnp.max(el2, axis=-1, keepdims=True)
    i2 = jnp.min(jnp.where(jnp.logical_and(rest, el2 == v2), lane_f, float(LANES)),
                 axis=-1, keepdims=True)
    e21 = jnp.exp(v2 - v1)
    w1 = g_w / (1.0 + e21)
    w2 = g_w * e21 / (1.0 + e21)
    idx = jnp.where(lane == 0, i1, i2) - float(N_GROUPS)
    wts = jnp.where(lane == 0, w1, jnp.where(lane == 1, w2, 0.0))
    return idx.astype(jnp.int32), wts


def _out_kernel(m_ref, w_ref, x_ref, g_ref, wr_ref, br_ref, h_ref, hn_ref, idx_ref, wt_ref):
    h = x_ref[...] + jnp.dot(m_ref[...], w_ref[...], preferred_element_type=F32)
    h_ref[...] = h
    r = lax.rsqrt(jnp.mean(h * h, axis=-1, keepdims=True) + EPS)
    hn = ((h * r) * g_ref[...]).astype(BF16)
    bits = lax.bitcast_convert_type(hn.astype(F32), jnp.uint32)
    for c in range(HN_WORD_CHUNKS):
        hi = bits[:, c * LANES:(c + 1) * LANES]
        lo = bits[:, (c + HN_WORD_CHUNKS) * LANES:(c + HN_WORD_CHUNKS + 1) * LANES]
        hn_ref[pl.ds(c, hn.shape[0], stride=HN_WORD_CHUNKS), :] = hi | (lo >> 16)
    logits = jnp.dot(hn, wr_ref[...], preferred_element_type=F32) + br_ref[...]
    idx, wts = _route(logits)
    idx_ref[...] = idx
    wt_ref[...] = wts


def _out_proj(merged, w_out, x, g2, w_router, b_router, tm=256):
    T, D = x.shape
    row = lambda i: (i, 0)
    const = lambda i: (0, 0)
    return pl.pallas_call(
        _out_kernel,
        grid=(T // tm,),
        in_specs=[pl.BlockSpec((tm, D), row),
                  pl.BlockSpec((D, D), const),
                  pl.BlockSpec((tm, D), row),
                  pl.BlockSpec((1, D), const),
                  pl.BlockSpec((D, LANES), const),
                  pl.BlockSpec((1, LANES), const)],
        out_specs=[pl.BlockSpec((tm, D), row),
                   pl.BlockSpec((tm * HN_WORD_CHUNKS, LANES), row),
                   pl.BlockSpec((tm, LANES), row),
                   pl.BlockSpec((tm, LANES), row)],
        out_shape=[jax.ShapeDtypeStruct((T, D), F32),
                   jax.ShapeDtypeStruct((T * HN_WORD_CHUNKS, LANES), jnp.uint32),
                   jax.ShapeDtypeStruct((T, LANES), jnp.int32),
                   jax.ShapeDtypeStruct((T, LANES), F32)],
        compiler_params=_params(("parallel",)),
        name="out_proj_router",
    )(merged, w_out.astype(BF16), x, g2.reshape(1, D), w_router, b_router)


def _expert_kernel(be_ref, eo_ref, ue_ref, nu_ref, rtc_ref, rtn_ref, hn_hbm, wg_hbm, wu_hbm, wd_hbm,
                   y_ref, xbuf, wgf, wuf, wdf, wgb, wub, wdb, sem, wsem):
    b = pl.program_id(0)
    n_used = nu_ref[0]
    n_exp = nu_ref[1]
    used = b < n_used
    slot = b % 2

    def weight_copies(ordinal, slot_):
        e = ue_ref[ordinal]
        return (pltpu.make_async_copy(wg_hbm.at[e], wgf.at[slot_], wsem.at[slot_]),
                pltpu.make_async_copy(wu_hbm.at[e], wuf.at[slot_], wsem.at[slot_]),
                pltpu.make_async_copy(wd_hbm.at[e], wdf.at[slot_], wsem.at[slot_]))

    def start_weights(ordinal):
        for c in weight_copies(ordinal, ordinal % 2):
            c.start(priority=1)

    def gather_rows(tok_ref, slot_):
        def issue(g, c):
            for j in range(ROWS_PER_ISSUE):
                r = g * ROWS_PER_ISSUE + j
                src = pl.multiple_of(tok_ref[0, 0, r] * HN_WORD_CHUNKS, HN_WORD_CHUNKS)
                dst = pl.multiple_of(r * HN_WORD_CHUNKS, HN_WORD_CHUNKS)
                pltpu.make_async_copy(hn_hbm.at[pl.ds(src, HN_WORD_CHUNKS)],
                                      xbuf.at[slot_, pl.ds(dst, HN_WORD_CHUNKS)],
                                      sem.at[slot_]).start()
            return c

        lax.fori_loop(0, MOE_BLOCK // ROWS_PER_ISSUE, issue, 0)

    @pl.when(b == 0)
    def _():
        start_weights(0)

        @pl.when(n_exp > 1)
        def _():
            start_weights(1)

        gather_rows(rtc_ref, 0)

    @pl.when(b + 1 < n_used)
    def _():
        gather_rows(rtn_ref, 1 - slot)

    @pl.when(used)
    def _():
        first = jnp.logical_or(b == 0, be_ref[b] != be_ref[jnp.maximum(b - 1, 0)])

        @pl.when(first)
        def _():
            ordinal = eo_ref[b]
            ws = ordinal % 2
            for c in weight_copies(ordinal, ws):
                c.wait()
            wgb[...] = wgf[ws].astype(BF16)
            wub[...] = wuf[ws].astype(BF16)
            wdb[...] = wdf[ws].astype(BF16)

            @pl.when(ordinal + 2 < n_exp)
            def _():
                start_weights(ordinal + 2)

        pltpu.make_async_copy(hn_hbm.at[pl.ds(0, MOE_BLOCK * HN_WORD_CHUNKS)], xbuf.at[slot],
                              sem.at[slot]).wait()

        halves = ([], [])
        for c in range(HN_WORD_CHUNKS):
            w = xbuf[slot, pl.ds(c, MOE_BLOCK, stride=HN_WORD_CHUNKS), :]
            hi = lax.bitcast_convert_type(w & jnp.uint32(0xFFFF0000), F32)
            lo = lax.bitcast_convert_type(w << 16, F32)
            halves[0].append(hi.astype(BF16))
            halves[1].append(lo.astype(BF16))
        x = jnp.concatenate(halves[0] + halves[1], axis=1)
        hg = jnp.dot(x, wgb[...], preferred_element_type=F32)
        hu = jnp.dot(x, wub[...], preferred_element_type=F32)
        hdn = (jax.nn.silu(hg) * hu).astype(BF16)
        y_ref[...] = jnp.dot(hdn, wdb[...], preferred_element_type=F32)

    @pl.when(jnp.logical_not(used))
    def _():
        y_ref[...] = jnp.zeros(y_ref.shape, y_ref.dtype)


def _experts(hn, w_gate, w_up, w_down, block_e, block_ord, used_experts, row_tok, n_used):
    D = w_gate.shape[1]
    n_rows = row_tok.shape[0]
    n_blocks = n_rows // MOE_BLOCK
    tok3 = row_tok.reshape(n_blocks, 1, MOE_BLOCK)
    hbm = pl.BlockSpec(memory_space=pl.ANY)
    tok_block = lambda im: pl.BlockSpec((1, 1, MOE_BLOCK), im, memory_space=pltpu.SMEM)
    grid_spec = pltpu.PrefetchScalarGridSpec(
        num_scalar_prefetch=4,
        grid=(n_blocks,),
        in_specs=[tok_block(lambda b, *_: (b, 0, 0)),
                  tok_block(lambda b, *_: (jnp.minimum(b + 1, n_blocks - 1), 0, 0)),
                  hbm, hbm, hbm, hbm],
        out_specs=pl.BlockSpec((MOE_BLOCK, D), lambda b, *_: (b, 0)),
        scratch_shapes=[pltpu.VMEM((2, MOE_BLOCK * HN_WORD_CHUNKS, LANES), jnp.uint32),
                        pltpu.VMEM((2, D, EXPERT_FF), F32),
                        pltpu.VMEM((2, D, EXPERT_FF), F32),
                        pltpu.VMEM((2, EXPERT_FF, D), F32),
                        pltpu.VMEM((D, EXPERT_FF), BF16),
                        pltpu.VMEM((D, EXPERT_FF), BF16),
                        pltpu.VMEM((EXPERT_FF, D), BF16),
                        pltpu.SemaphoreType.DMA((2,)),
                        pltpu.SemaphoreType.DMA((2,))],
    )
    return pl.pallas_call(
        _expert_kernel,
        grid_spec=grid_spec,
        out_shape=jax.ShapeDtypeStruct((n_rows, D), F32),
        compiler_params=_params(("arbitrary",)),
        name="experts",
    )(block_e, block_ord, used_experts, n_used, tok3, tok3, hn, w_gate, w_up, w_down)


def _combine_kernel(dc_ref, dn_ref, y_hbm, h_ref, wt_ref, o_ref, ybuf, sem, *, tc):
    i = pl.program_id(0)
    slot = i % 2

    def gather_rows(dest_ref, slot_):
        def issue(r, c):
            for k in range(TOP_K):
                src = dest_ref[0, 0, r * TOP_K + k]
                pltpu.make_async_copy(y_hbm.at[pl.ds(src, 1)],
                                      ybuf.at[slot_, k, pl.ds(r, 1)],
                                      sem.at[slot_]).start(priority=k % 2)
            return c

        lax.fori_loop(0, tc, issue, 0, unroll=4)

    @pl.when(i == 0)
    def _():
        gather_rows(dc_ref, 0)

    @pl.when(i + 1 < pl.num_programs(0))
    def _():
        gather_rows(dn_ref, 1 - slot)

    for k in range(TOP_K):
        pltpu.make_async_copy(y_hbm.at[pl.ds(0, tc)], ybuf.at[slot, k], sem.at[slot]).wait()
    wt = wt_ref[...]
    o_ref[...] = h_ref[...] + (wt[:, 0:1] * ybuf[slot, 0] + wt[:, 1:2] * ybuf[slot, 1])


def _combine(yrows, h, wts, dest, tc=128):
    T, D = h.shape
    steps = T // tc
    row = lambda i: (i, 0)
    dest3 = dest.reshape(steps, 1, tc * TOP_K)
    dest_block = lambda im: pl.BlockSpec((1, 1, tc * TOP_K), im, memory_space=pltpu.SMEM)
    return pl.pallas_call(
        functools.partial(_combine_kernel, tc=tc),
        grid=(steps,),
        in_specs=[dest_block(lambda i: (i, 0, 0)),
                  dest_block(lambda i: (jnp.minimum(i + 1, steps - 1), 0, 0)),
                  pl.BlockSpec(memory_space=pl.ANY),
                  pl.BlockSpec((tc, D), row),
                  pl.BlockSpec((tc, LANES), row)],
        out_specs=pl.BlockSpec((tc, D), row),
        out_shape=jax.ShapeDtypeStruct((T, D), F32),
        scratch_shapes=[pltpu.VMEM((2, TOP_K, tc, D), F32),
                        pltpu.SemaphoreType.DMA((2,))],
        compiler_params=_params(("arbitrary",)),
        name="combine",
    )(dest3, dest3, yrows, h, wts)


SUBLANES = 8
META_ROWS = 256


def _lane_cumsum(x):
    lane = lax.broadcasted_iota(jnp.int32, x.shape, 1)
    s = 1
    while s < LANES:
        x = x + jnp.where(lane >= s, pltpu.roll(x, s, 1), 0)
        s *= 2
    return x


def _dispatch_kernel(idx_ref, dest_ref, meta_ref, run_ref, prefix_ref, start_ref, *, tb):
    p = pl.program_id(0)
    i = pl.program_id(1)
    idx = idx_ref[...]
    lane = lax.broadcasted_iota(jnp.int32, idx.shape, 1)
    e1 = idx[:, 0:1]
    e2 = idx[:, 1:2]
    onehot = jnp.where(jnp.logical_or(lane == e1, lane == e2), 1.0, 0.0)

    @pl.when(jnp.logical_and(p == 0, i == 0))
    def _():
        run_ref[...] = jnp.zeros(run_ref.shape, F32)

    @pl.when(p == 0)
    def _():
        prefix_ref[i] = run_ref[...]
        run_ref[...] = run_ref[...] + jnp.sum(onehot, axis=0, keepdims=True)

    @pl.when(jnp.logical_and(p == 1, i == 0))
    def _():
        counts = run_ref[...].astype(jnp.int32)
        nblk = (counts + (MOE_BLOCK - 1)) // MOE_BLOCK
        end_blk = _lane_cumsum(nblk)
        start_ref[...] = ((end_blk - nblk) * MOE_BLOCK).astype(F32)
        has = jnp.where(counts > 0, 1, 0)
        ordinal = _lane_cumsum(has) - 1
        end_blk, has, ordinal = end_blk[0:1], has[0:1], ordinal[0:1]
        rows = lax.broadcasted_iota(jnp.int32, (META_ROWS, LANES), 0)
        lanes = lax.broadcasted_iota(jnp.int32, (META_ROWS, LANES), 1)
        is_e = lanes < N_EXPERTS
        rsum = lambda v: jnp.sum(v, axis=-1, keepdims=True)
        be = rsum(jnp.where(jnp.logical_and(is_e, end_blk <= rows), 1, 0))
        be = jnp.minimum(be, N_EXPERTS - 1)
        eo = rsum(jnp.where(lanes == be, ordinal, 0))
        ue = rsum(jnp.where(jnp.logical_and(has > 0, ordinal == rows), lanes, 0))
        n_blk = rsum(jnp.where(lanes == N_EXPERTS - 1, end_blk, 0))
        n_exp = rsum(jnp.where(is_e, has, 0))
        meta_ref[...] = jnp.where(
            lanes == 0, be, jnp.where(lanes == 1, eo, jnp.where(
                lanes == 2, ue, jnp.where(lanes == 3, n_blk, n_exp))))

    @pl.when(p == 1)
    def _():
        r = lax.broadcasted_iota(jnp.int32, (tb, tb), 0)
        c = lax.broadcasted_iota(jnp.int32, (tb, tb), 1)
        earlier = jnp.where(c < r, 1.0, 0.0).astype(BF16)
        rank = jnp.dot(earlier, onehot.astype(BF16), preferred_element_type=F32)
        rank = rank + prefix_ref[i][0:1] + start_ref[0:1]
        d1 = jnp.sum(jnp.where(lane == e1, rank, 0.0), axis=-1, keepdims=True)
        d2 = jnp.sum(jnp.where(lane == e2, rank, 0.0), axis=-1, keepdims=True)
        dest_ref[...] = jnp.where(lane == 0, d1, d2).astype(jnp.int32)


def _dispatch(idx, tb=512):
    T = idx.shape[0]
    n_rows = T * TOP_K + N_EXPERTS * MOE_BLOCK
    n_blocks = n_rows // MOE_BLOCK
    assert n_blocks <= META_ROWS
    dest2, meta = pl.pallas_call(
        functools.partial(_dispatch_kernel, tb=tb),
        grid=(2, T // tb),
        in_specs=[pl.BlockSpec((tb, LANES), lambda p, i: (i, 0))],
        out_specs=[pl.BlockSpec((tb, LANES), lambda p, i: (i * p, 0)),
                   pl.BlockSpec((META_ROWS, LANES), lambda p, i: (0, 0))],
        out_shape=[jax.ShapeDtypeStruct((T, LANES), jnp.int32),
                   jax.ShapeDtypeStruct((META_ROWS, LANES), jnp.int32)],
        scratch_shapes=[pltpu.VMEM((SUBLANES, LANES), F32),
                        pltpu.VMEM((T // tb, SUBLANES, LANES), F32),
                        pltpu.VMEM((SUBLANES, LANES), F32)],
        compiler_params=_params(("arbitrary", "arbitrary")),
        name="dispatch",
    )(idx)
    dest = dest2[:, :TOP_K].reshape(T * TOP_K)
    tok = jnp.repeat(jnp.arange(T, dtype=jnp.int32), TOP_K)
    row_tok = jnp.zeros((n_rows,), jnp.int32).at[dest].set(tok)
    block_e = meta[:n_blocks, 0]
    block_ord = meta[:n_blocks, 1]
    used_experts = meta[:N_EXPERTS, 2]
    n_used = meta[0, 3:5]
    return block_e, block_ord, used_experts, row_tok, n_used, dest


def kernel(x, positions, norm1_g, w_in, q_norm_g, k_norm_g, sink_logits, sg_ln_g, sg_ln_b, sg_w, sg_b, w_branch_att, w_branch_sg, w_out, norm2_g, w_group_router, b_group_router, w_expert_router, b_expert_router, w_gate, w_up, w_down):
    B, S, D = x.shape
    T = B * S
    h = x.reshape(T, D)
    pos = positions.reshape(T)
    for l in range(norm1_g.shape[0]):
        xn = _rmsnorm(h, norm1_g[l])
        proj = _in_proj(xn, w_in[l])
        q, k, v = _qkv_prep(proj, pos, q_norm_g[l], k_norm_g[l])
        att = _attention(q, k, v, sink_logits[l], B)
        sgo = _spatial_gating(proj, sg_ln_g[l], sg_ln_b[l], sg_w[l], sg_b[l])
        merged = _merge(att, sgo, proj, w_branch_att[l], w_branch_sg[l])
        pad = LANES - N_GROUPS - N_EXPERTS
        w_router = jnp.concatenate(
            [w_group_router[l], w_expert_router[l], jnp.zeros((D, pad), F32)], axis=1).astype(BF16)
        b_router = jnp.concatenate(
            [b_group_router[l], b_expert_router[l], jnp.zeros((pad,), F32)]).reshape(1, LANES)
        h, hn, idx, wts = _out_proj(merged, w_out[l], h, norm2_g[l], w_router, b_router)
        block_e, block_ord, used_experts, row_tok, n_used, dest = _dispatch(idx)
        yrows = _experts(hn, w_gate[l], w_up[l], w_down[l],
                         block_e, block_ord, used_experts, row_tok, n_used)
        h = _combine(yrows, h, wts, dest)
    return h.reshape(B, S, D)
```

```python
import functools

import jax
import jax.numpy as jnp
from jax import lax
from jax.experimental import pallas as pl
from jax.experimental.pallas import tpu as pltpu

F32 = jnp.float32
BF16 = jnp.bfloat16

D_MODEL = 2048
HEAD_DIM = 64
ATT_WIDTH = D_MODEL // 2
ATT_HEADS = ATT_WIDTH // HEAD_DIM
ATT_KV_HEADS = ATT_HEADS // 4
Q_PER_KV = ATT_HEADS // ATT_KV_HEADS
KV_WIDTH = ATT_KV_HEADS * HEAD_DIM
WINDOW = 128
ATT_BLOCK = 128
ROPE_DIM = HEAD_DIM // 4
ROPE_HALF = ROPE_DIM // 2
ROPE_THETA = 500000.0
SG_WIDTH = D_MODEL // 2
SG_GROUP_DIM = 128
SG_GROUPS = SG_WIDTH // SG_GROUP_DIM
SG_CHUNK = 128
OFF_Q = 0
OFF_K = OFF_Q + ATT_WIDTH
OFF_V = OFF_K + KV_WIDTH
OFF_U = OFF_V + KV_WIDTH
OFF_S = OFF_U + SG_WIDTH
OFF_G = OFF_S + SG_WIDTH
IN_COLS = OFF_G + 2 * D_MODEL
N_GROUPS = 8
EXPERTS_PER_GROUP = 8
N_EXPERTS = N_GROUPS * EXPERTS_PER_GROUP
TOP_K = 2
EXPERT_FF = D_MODEL // 4
MOE_BLOCK = 256
EPS = 1e-6
NEG_INF = -1e30

LANES = 128
HN_WORD_CHUNKS = D_MODEL // LANES // 2
ROWS_PER_ISSUE = 8
VMEM_LIMIT = 56 * 1024 * 1024


def _params(sem, vmem=VMEM_LIMIT):
    return pltpu.CompilerParams(dimension_semantics=sem, vmem_limit_bytes=vmem)


def _rmsnorm_kernel(x_ref, g_ref, o_ref):
    x = x_ref[...]
    r = lax.rsqrt(jnp.mean(x * x, axis=-1, keepdims=True) + EPS)
    o_ref[...] = ((x * r) * g_ref[...]).astype(o_ref.dtype)


def _rmsnorm(x, g, tm=512):
    T, D = x.shape
    return pl.pallas_call(
        _rmsnorm_kernel,
        grid=(T // tm,),
        in_specs=[pl.BlockSpec((tm, D), lambda i: (i, 0)),
                  pl.BlockSpec((1, D), lambda i: (0, 0))],
        out_specs=pl.BlockSpec((tm, D), lambda i: (i, 0)),
        out_shape=jax.ShapeDtypeStruct((T, D), BF16),
        compiler_params=_params(("parallel",)),
        name="norm1",
    )(x, g.reshape(1, D))


def _proj_kernel(x_ref, w_ref, o_ref, wbf_ref):
    @pl.when(pl.program_id(1) == 0)
    def _():
        wbf_ref[...] = w_ref[...].astype(BF16)

    o_ref[...] = jnp.dot(x_ref[...], wbf_ref[...], preferred_element_type=F32)


def _in_proj(xn, w, tm=512, tn=1280):
    T, D = xn.shape
    N = w.shape[1]
    return pl.pallas_call(
        _proj_kernel,
        grid=(N // tn, T // tm),
        in_specs=[pl.BlockSpec((tm, D), lambda j, i: (i, 0)),
                  pl.BlockSpec((D, tn), lambda j, i: (0, j))],
        out_specs=pl.BlockSpec((tm, tn), lambda j, i: (i, j)),
        out_shape=jax.ShapeDtypeStruct((T, N), F32),
        scratch_shapes=[pltpu.VMEM((D, tn), BF16)],
        compiler_params=_params(("arbitrary", "arbitrary")),
        name="in_proj",
    )(xn, w)


def _rope_table_kernel(pos_ref, invf_ref, cos_ref, sin_ref):
    ang = pos_ref[...].astype(F32) * invf_ref[...]
    cos_ref[...] = jnp.cos(ang)
    sin_ref[...] = jnp.sin(ang)


def _rope_tables(positions):
    T = positions.shape[0]
    rows = T * ROPE_HALF // LANES
    inv = ROPE_THETA ** (-jnp.arange(0, ROPE_DIM, 2, dtype=F32) / ROPE_DIM)
    invf = jnp.tile(inv, LANES // ROPE_HALF).reshape(1, LANES)
    pos = jnp.repeat(positions, ROPE_HALF).reshape(rows, LANES)
    whole = lambda: (0, 0)
    cos, sin = pl.pallas_call(
        _rope_table_kernel,
        in_specs=[pl.BlockSpec((rows, LANES), whole), pl.BlockSpec((1, LANES), whole)],
        out_specs=[pl.BlockSpec((rows, LANES), whole), pl.BlockSpec((rows, LANES), whole)],
        out_shape=[jax.ShapeDtypeStruct((rows, LANES), F32)] * 2,
        name="rope_tables",
    )(pos, invf)
    return cos.reshape(T, ROPE_HALF), sin.reshape(T, ROPE_HALF)


def _qkv_prep_kernel(p_ref, cos_ref, sin_ref, gq_ref, gk_ref, q_ref, k_ref, v_ref):
    cos = cos_ref[...]
    sin = sin_ref[...]
    lane = lax.broadcasted_iota(jnp.int32, cos.shape, 1)
    low_head = lane < HEAD_DIM
    first_half = (lane % HEAD_DIM) < ROPE_HALF

    def norm_rope(x, g):
        x2 = x * x
        s_lo = jnp.sum(jnp.where(low_head, x2, 0.0), axis=-1, keepdims=True)
        s_hi = jnp.sum(jnp.where(low_head, 0.0, x2), axis=-1, keepdims=True)
        ssq = jnp.where(low_head, s_lo, s_hi)
        xn = (x * lax.rsqrt(ssq * (1.0 / HEAD_DIM) + EPS)) * g
        partner = jnp.where(first_half,
                            pltpu.roll(xn, LANES - ROPE_HALF, 1),
                            pltpu.roll(xn, ROPE_HALF, 1))
        return xn * cos + partner * sin

    for c in range(ATT_WIDTH // LANES):
        x = p_ref[:, OFF_Q + c * LANES:OFF_Q + (c + 1) * LANES]
        q_ref[:, c * LANES:(c + 1) * LANES] = (
            norm_rope(x, gq_ref[...]) * (HEAD_DIM ** -0.5)).astype(q_ref.dtype)
    for c in range(KV_WIDTH // LANES):
        x = p_ref[:, OFF_K + c * LANES:OFF_K + (c + 1) * LANES]
        k_ref[:, c * LANES:(c + 1) * LANES] = norm_rope(x, gk_ref[...]).astype(k_ref.dtype)
    v_ref[...] = p_ref[:, OFF_V:OFF_V + KV_WIDTH].astype(v_ref.dtype)


def _qkv_prep(proj, positions, q_g, k_g, tq=256):
    T = proj.shape[0]
    width = OFF_U
    cos8, sin8 = _rope_tables(positions)
    rest = HEAD_DIM - ROPE_DIM
    cos_t = jnp.tile(jnp.concatenate([cos8, cos8, jnp.ones((T, rest), F32)], axis=1),
                     (1, LANES // HEAD_DIM))
    sin_t = jnp.tile(jnp.concatenate([-sin8, sin8, jnp.zeros((T, rest), F32)], axis=1),
                     (1, LANES // HEAD_DIM))
    gq = jnp.tile(q_g, LANES // HEAD_DIM).reshape(1, LANES)
    gk = jnp.tile(k_g, LANES // HEAD_DIM).reshape(1, LANES)
    row = lambda i: (i, 0)
    const = lambda i: (0, 0)
    return pl.pallas_call(
        _qkv_prep_kernel,
        grid=(T // tq,),
        in_specs=[pl.BlockSpec((tq, width), row),
                  pl.BlockSpec((tq, LANES), row),
                  pl.BlockSpec((tq, LANES), row),
                  pl.BlockSpec((1, LANES), const),
                  pl.BlockSpec((1, LANES), const)],
        out_specs=[pl.BlockSpec((tq, ATT_WIDTH), row),
                   pl.BlockSpec((tq, KV_WIDTH), row),
                   pl.BlockSpec((tq, KV_WIDTH), row)],
        out_shape=[jax.ShapeDtypeStruct((T, ATT_WIDTH), BF16),
                   jax.ShapeDtypeStruct((T, KV_WIDTH), BF16),
                   jax.ShapeDtypeStruct((T, KV_WIDTH), BF16)],
        compiler_params=_params(("parallel",)),
        name="qkv_prep",
    )(proj, cos_t, sin_t, gq, gk)


def _attn_kernel(sink_ref, q_ref, kp_ref, kc_ref, kn_ref, vp_ref, vc_ref, vn_ref, o_ref, *, nb):
    n = pl.program_id(1)
    rows = Q_PER_KV * ATT_BLOCK
    keys = 3 * ATT_BLOCK
    qi = lax.broadcasted_iota(jnp.int32, (rows, ATT_BLOCK), 0) % ATT_BLOCK
    kj = lax.broadcasted_iota(jnp.int32, (rows, ATT_BLOCK), 1)
    lo_prev = jnp.where(n > 0, 0, ATT_BLOCK)
    hi_next = jnp.where(n < nb - 1, 0, -ATT_BLOCK)
    cap_prev = jnp.where(kj - qi >= lo_prev, jnp.inf, NEG_INF)
    cap_next = jnp.where(kj - qi <= hi_next, jnp.inf, NEG_INF)
    cap = jnp.concatenate([cap_prev, jnp.full((rows, ATT_BLOCK), jnp.inf, F32), cap_next], axis=1)
    row_head = lax.broadcasted_iota(jnp.int32, (rows, 1), 0) // ATT_BLOCK

    for kvh in range(ATT_KV_HEADS):
        cols = slice(kvh * HEAD_DIM, (kvh + 1) * HEAD_DIM)
        k = jnp.concatenate([kp_ref[:, cols], kc_ref[:, cols], kn_ref[:, cols]], axis=0)
        v = jnp.concatenate([vp_ref[:, cols], vc_ref[:, cols], vn_ref[:, cols]], axis=0)
        q = jnp.concatenate(
            [q_ref[:, (kvh * Q_PER_KV + g) * HEAD_DIM:(kvh * Q_PER_KV + g + 1) * HEAD_DIM]
             for g in range(Q_PER_KV)], axis=0)
        sink = jnp.zeros((rows, 1), F32)
        for g in range(Q_PER_KV):
            sink = jnp.where(row_head == g, sink_ref[kvh * Q_PER_KV + g], sink)
        s = lax.dot_general(q, k, (((1,), (1,)), ((), ())), preferred_element_type=F32)
        s = jnp.minimum(s, cap)
        m = jnp.maximum(jnp.max(s, axis=-1, keepdims=True), sink)
        e = jnp.exp(s - m)
        denom = jnp.sum(e, axis=-1, keepdims=True) + jnp.exp(sink - m)
        p = (e / denom).astype(BF16)
        o = jnp.dot(p, v, preferred_element_type=F32)
        for g in range(Q_PER_KV):
            h = kvh * Q_PER_KV + g
            o_ref[:, h * HEAD_DIM:(h + 1) * HEAD_DIM] = (
                o[g * ATT_BLOCK:(g + 1) * ATT_BLOCK].astype(o_ref.dtype))


def _attention(q, k, v, sink, batch):
    T = q.shape[0]
    nb = T // batch // ATT_BLOCK
    cur = lambda b, n: (b * nb + n, 0)
    prev = lambda b, n: (b * nb + jnp.maximum(n - 1, 0), 0)
    nxt = lambda b, n: (b * nb + jnp.minimum(n + 1, nb - 1), 0)
    kv = lambda im: pl.BlockSpec((ATT_BLOCK, KV_WIDTH), im)
    return pl.pallas_call(
        functools.partial(_attn_kernel, nb=nb),
        grid=(batch, nb),
        in_specs=[pl.BlockSpec(memory_space=pltpu.SMEM),
                  pl.BlockSpec((ATT_BLOCK, ATT_WIDTH), cur),
                  kv(prev), kv(cur), kv(nxt), kv(prev), kv(cur), kv(nxt)],
        out_specs=pl.BlockSpec((ATT_BLOCK, ATT_WIDTH), cur),
        out_shape=jax.ShapeDtypeStruct((T, ATT_WIDTH), BF16),
        compiler_params=_params(("parallel", "parallel")),
        name="window_attn",
    )(sink, q, k, k, k, v, v, v)


def _sg_kernel(u_ref, s_ref, lng_ref, lnb_ref, w_ref, b_ref, o_ref):
    groups = w_ref.shape[0]

    def one_chunk(ci, carry):
        rows = pl.ds(pl.multiple_of(ci * SG_CHUNK, SG_CHUNK), SG_CHUNK)
        for gi in range(groups):
            cols = slice(gi * SG_GROUP_DIM, (gi + 1) * SG_GROUP_DIM)
            s = jax.nn.gelu(s_ref[rows, cols])
            mu = jnp.mean(s, axis=-1, keepdims=True)
            sc = s - mu
            var = jnp.mean(sc * sc, axis=-1, keepdims=True)
            sn = (sc * lax.rsqrt(var + EPS)) * lng_ref[0, gi:gi + 1, :] + lnb_ref[0, gi:gi + 1, :]
            mixed = jnp.dot(w_ref[gi], sn.astype(BF16), preferred_element_type=F32)
            mixed = mixed + b_ref[0, :, gi:gi + 1]
            o_ref[rows, cols] = (jax.nn.gelu(u_ref[rows, cols]) * mixed).astype(o_ref.dtype)
        return carry

    lax.fori_loop(0, u_ref.shape[0] // SG_CHUNK, one_chunk, 0)


def _spatial_gating(proj, ln_g, ln_b, w_s, b_s, halves=2, chunks=4):
    T = proj.shape[0]
    half_w = SG_WIDTH // halves
    gph = SG_GROUPS // halves
    u0, s0 = OFF_U // half_w, OFF_S // half_w
    tq = chunks * SG_CHUNK
    return pl.pallas_call(
        _sg_kernel,
        grid=(T // tq, halves),
        in_specs=[pl.BlockSpec((tq, half_w), lambda i, j: (i, u0 + j)),
                  pl.BlockSpec((tq, half_w), lambda i, j: (i, s0 + j)),
                  pl.BlockSpec((1, gph, SG_GROUP_DIM), lambda i, j: (j, 0, 0)),
                  pl.BlockSpec((1, gph, SG_GROUP_DIM), lambda i, j: (j, 0, 0)),
                  pl.BlockSpec((gph, SG_CHUNK, SG_CHUNK), lambda i, j: (j, 0, 0)),
                  pl.BlockSpec((1, SG_CHUNK, gph), lambda i, j: (j, 0, 0))],
        out_specs=pl.BlockSpec((tq, half_w), lambda i, j: (i, j)),
        out_shape=jax.ShapeDtypeStruct((T, SG_WIDTH), BF16),
        compiler_params=_params(("parallel", "parallel")),
        name="spatial_gating",
    )(proj, proj,
      ln_g.reshape(halves, gph, SG_GROUP_DIM), ln_b.reshape(halves, gph, SG_GROUP_DIM),
      w_s.astype(BF16),
      b_s.reshape(halves, gph, SG_CHUNK).transpose(0, 2, 1))


def _merge_kernel(att_ref, sgo_ref, ga_ref, gb_ref, wa_ref, wb_ref, o_ref):
    a = jnp.dot(att_ref[...], wa_ref[...], preferred_element_type=F32)
    b = jnp.dot(sgo_ref[...], wb_ref[...], preferred_element_type=F32)
    m = jax.nn.sigmoid(ga_ref[...]) * a + jax.nn.sigmoid(gb_ref[...]) * b
    o_ref[...] = m.astype(o_ref.dtype)


def _merge(att, sgo, proj, w_a, w_b, tm=1024, tn=512):
    T = att.shape[0]
    ga0 = OFF_G // tn
    gb0 = (OFF_G + D_MODEL) // tn
    return pl.pallas_call(
        _merge_kernel,
        grid=(T // tm, D_MODEL // tn),
        in_specs=[pl.BlockSpec((tm, ATT_WIDTH), lambda i, j: (i, 0)),
                  pl.BlockSpec((tm, SG_WIDTH), lambda i, j: (i, 0)),
                  pl.BlockSpec((tm, tn), lambda i, j: (i, ga0 + j)),
                  pl.BlockSpec((tm, tn), lambda i, j: (i, gb0 + j)),
                  pl.BlockSpec((ATT_WIDTH, tn), lambda i, j: (0, j)),
                  pl.BlockSpec((SG_WIDTH, tn), lambda i, j: (0, j))],
        out_specs=pl.BlockSpec((tm, tn), lambda i, j: (i, j)),
        out_shape=jax.ShapeDtypeStruct((T, D_MODEL), BF16),
        compiler_params=_params(("parallel", "parallel")),
        name="merge",
    )(att, sgo, proj, proj, w_a.astype(BF16), w_b.astype(BF16))


def _route(logits):
    lane = lax.broadcasted_iota(jnp.int32, logits.shape, 1)
    lane_f = lane.astype(F32)
    is_g = lane < N_GROUPS
    gl = jnp.where(is_g, logits, -jnp.inf)
    gmax = jnp.max(gl, axis=-1, keepdims=True)
    grp = jnp.min(jnp.where(gl == gmax, lane_f, float(LANES)), axis=-1, keepdims=True)
    gsum = jnp.sum(jnp.where(is_g, jnp.exp(logits - gmax), 0.0), axis=-1, keepdims=True)
    g_w = 1.0 / gsum
    e_lane = lane - N_GROUPS
    in_grp = jnp.logical_and(
        jnp.logical_and(e_lane >= 0, e_lane < N_EXPERTS),
        (e_lane // EXPERTS_PER_GROUP).astype(F32) == grp)
    el = jnp.where(in_grp, logits, -jnp.inf)
    v1 = jnp.max(el, axis=-1, keepdims=True)
    i1 = jnp.min(jnp.where(jnp.logical_and(in_grp, el == v1), lane_f, float(LANES)),
                 axis=-1, keepdims=True)
    rest = jnp.logical_and(in_grp, lane_f != i1)
    el2 = jnp.where(rest, logits, -jnp.inf)
    v2 = jnp.max(el2, axis=-1, keepdims=True)
    i2 = jnp.min(jnp.where(jnp.logical_and(rest, el2 == v2), lane_f, float(LANES)),
                 axis=-1, keepdims=True)
    e21 = jnp.exp(v2 - v1)
    w1 = g_w / (1.0 + e21)
    w2 = g_w * e21 / (1.0 + e21)
    idx = jnp.where(lane == 0, i1, i2) - float(N_GROUPS)
    wts = jnp.where(lane == 0, w1, jnp.where(lane == 1, w2, 0.0))
    return idx.astype(jnp.int32), wts


def _out_kernel(m_ref, w_ref, x_ref, g_ref, wr_ref, br_ref, h_ref, hn_ref, idx_ref, wt_ref):
    h = x_ref[...] + jnp.dot(m_ref[...], w_ref[...], preferred_element_type=F32)
    h_ref[...] = h
    r = lax.rsqrt(jnp.mean(h * h, axis=-1, keepdims=True) + EPS)
    hn = ((h * r) * g_ref[...]).astype(BF16)
    bits = lax.bitcast_convert_type(hn.astype(F32), jnp.uint32)
    for c in range(HN_WORD_CHUNKS):
        hi = bits[:, c * LANES:(c + 1) * LANES]
        lo = bits[:, (c + HN_WORD_CHUNKS) * LANES:(c + HN_WORD_CHUNKS + 1) * LANES]
        hn_ref[pl.ds(c, hn.shape[0], stride=HN_WORD_CHUNKS), :] = hi | (lo >> 16)
    logits = jnp.dot(hn, wr_ref[...], preferred_element_type=F32) + br_ref[...]
    idx, wts = _route(logits)
    idx_ref[...] = idx
    wt_ref[...] = wts


def _out_proj(merged, w_out, x, g2, w_router, b_router, tm=256):
    T, D = x.shape
    row = lambda i: (i, 0)
    const = lambda i: (0, 0)
    return pl.pallas_call(
        _out_kernel,
        grid=(T // tm,),
        in_specs=[pl.BlockSpec((tm, D), row),
                  pl.BlockSpec((D, D), const),
                  pl.BlockSpec((tm, D), row),
                  pl.BlockSpec((1, D), const),
                  pl.BlockSpec((D, LANES), const),
                  pl.BlockSpec((1, LANES), const)],
        out_specs=[pl.BlockSpec((tm, D), row),
                   pl.BlockSpec((tm * HN_WORD_CHUNKS, LANES), row),
                   pl.BlockSpec((tm, LANES), row),
                   pl.BlockSpec((tm, LANES), row)],
        out_shape=[jax.ShapeDtypeStruct((T, D), F32),
                   jax.ShapeDtypeStruct((T * HN_WORD_CHUNKS, LANES), jnp.uint32),
                   jax.ShapeDtypeStruct((T, LANES), jnp.int32),
                   jax.ShapeDtypeStruct((T, LANES), F32)],
        compiler_params=_params(("parallel",)),
        name="out_proj_router",
    )(merged, w_out.astype(BF16), x, g2.reshape(1, D), w_router, b_router)


def _expert_kernel(be_ref, eo_ref, ue_ref, nu_ref, rtc_ref, rtn_ref, hn_hbm, wg_hbm, wu_hbm, wd_hbm,
                   y_ref, xbuf, wgf, wuf, wdf, wgb, wub, wdb, sem, wsem):
    b = pl.program_id(0)
    n_used = nu_ref[0]
    n_exp = nu_ref[1]
    used = b < n_used
    slot = b % 2

    def weight_copies(ordinal, slot_):
        e = ue_ref[ordinal]
        return (pltpu.make_async_copy(wg_hbm.at[e], wgf.at[slot_], wsem.at[slot_]),
                pltpu.make_async_copy(wu_hbm.at[e], wuf.at[slot_], wsem.at[slot_]),
                pltpu.make_async_copy(wd_hbm.at[e], wdf.at[slot_], wsem.at[slot_]))

    def start_weights(ordinal):
        for c in weight_copies(ordinal, ordinal % 2):
            c.start(priority=1)

    def gather_rows(tok_ref, slot_):
        def issue(g, c):
            for j in range(ROWS_PER_ISSUE):
                r = g * ROWS_PER_ISSUE + j
                src = pl.multiple_of(tok_ref[0, 0, r] * HN_WORD_CHUNKS, HN_WORD_CHUNKS)
                dst = pl.multiple_of(r * HN_WORD_CHUNKS, HN_WORD_CHUNKS)
                pltpu.make_async_copy(hn_hbm.at[pl.ds(src, HN_WORD_CHUNKS)],
                                      xbuf.at[slot_, pl.ds(dst, HN_WORD_CHUNKS)],
                                      sem.at[slot_]).start()
            return c

        lax.fori_loop(0, MOE_BLOCK // ROWS_PER_ISSUE, issue, 0)

    @pl.when(b == 0)
    def _():
        start_weights(0)

        @pl.when(n_exp > 1)
        def _():
            start_weights(1)

        gather_rows(rtc_ref, 0)

    @pl.when(b + 1 < n_used)
    def _():
        gather_rows(rtn_ref, 1 - slot)

    @pl.when(used)
    def _():
        first = jnp.logical_or(b == 0, be_ref[b] != be_ref[jnp.maximum(b - 1, 0)])

        @pl.when(first)
        def _():
            ordinal = eo_ref[b]
            ws = ordinal % 2
            for c in weight_copies(ordinal, ws):
                c.wait()
            wgb[...] = wgf[ws].astype(BF16)
            wub[...] = wuf[ws].astype(BF16)
            wdb[...] = wdf[ws].astype(BF16)

            @pl.when(ordinal + 2 < n_exp)
            def _():
                start_weights(ordinal + 2)

        pltpu.make_async_copy(hn_hbm.at[pl.ds(0, MOE_BLOCK * HN_WORD_CHUNKS)], xbuf.at[slot],
                              sem.at[slot]).wait()

        halves = ([], [])
        for c in range(HN_WORD_CHUNKS):
            w = xbuf[slot, pl.ds(c, MOE_BLOCK, stride=HN_WORD_CHUNKS), :]
            hi = lax.bitcast_convert_type(w & jnp.uint32(0xFFFF0000), F32)
            lo = lax.bitcast_convert_type(w << 16, F32)
            halves[0].append(hi.astype(BF16))
            halves[1].append(lo.astype(BF16))
        x = jnp.concatenate(halves[0] + halves[1], axis=1)
        hg = jnp.dot(x, wgb[...], preferred_element_type=F32)
        hu = jnp.dot(x, wub[...], preferred_element_type=F32)
        hdn = (jax.nn.silu(hg) * hu).astype(BF16)
        y_ref[...] = jnp.dot(hdn, wdb[...], preferred_element_type=F32)

    @pl.when(jnp.logical_not(used))
    def _():
        y_ref[...] = jnp.zeros(y_ref.shape, y_ref.dtype)


def _experts(hn, w_gate, w_up, w_down, block_e, block_ord, used_experts, row_tok, n_used):
    D = w_gate.shape[1]
    n_rows = row_tok.shape[0]
    n_blocks = n_rows // MOE_BLOCK
    tok3 = row_tok.reshape(n_blocks, 1, MOE_BLOCK)
    hbm = pl.BlockSpec(memory_space=pl.ANY)
    tok_block = lambda im: pl.BlockSpec((1, 1, MOE_BLOCK), im, memory_space=pltpu.SMEM)
    grid_spec = pltpu.PrefetchScalarGridSpec(
        num_scalar_prefetch=4,
        grid=(n_blocks,),
        in_specs=[tok_block(lambda b, *_: (b, 0, 0)),
                  tok_block(lambda b, *_: (jnp.minimum(b + 1, n_blocks - 1), 0, 0)),
                  hbm, hbm, hbm, hbm],
        out_specs=pl.BlockSpec((MOE_BLOCK, D), lambda b, *_: (b, 0)),
        scratch_shapes=[pltpu.VMEM((2, MOE_BLOCK * HN_WORD_CHUNKS, LANES), jnp.uint32),
                        pltpu.VMEM((2, D, EXPERT_FF), F32),
                        pltpu.VMEM((2, D, EXPERT_FF), F32),
                        pltpu.VMEM((2, EXPERT_FF, D), F32),
                        pltpu.VMEM((D, EXPERT_FF), BF16),
                        pltpu.VMEM((D, EXPERT_FF), BF16),
                        pltpu.VMEM((EXPERT_FF, D), BF16),
                        pltpu.SemaphoreType.DMA((2,)),
                        pltpu.SemaphoreType.DMA((2,))],
    )
    return pl.pallas_call(
        _expert_kernel,
        grid_spec=grid_spec,
        out_shape=jax.ShapeDtypeStruct((n_rows, D), F32),
        compiler_params=_params(("arbitrary",)),
        name="experts",
    )(block_e, block_ord, used_experts, n_used, tok3, tok3, hn, w_gate, w_up, w_down)


def _combine_kernel(dc_ref, dn_ref, y_hbm, h_ref, wt_ref, o_ref, ybuf, sem, *, tc):
    i = pl.program_id(0)
    slot = i % 2

    def gather_rows(dest_ref, slot_):
        def issue(r, c):
            for k in range(TOP_K):
                src = dest_ref[0, 0, r * TOP_K + k]
                pltpu.make_async_copy(y_hbm.at[pl.ds(src, 1)],
                                      ybuf.at[slot_, k, pl.ds(r, 1)],
                                      sem.at[slot_]).start(priority=k % 2)
            return c

        lax.fori_loop(0, tc, issue, 0, unroll=4)

    @pl.when(i == 0)
    def _():
        gather_rows(dc_ref, 0)

    @pl.when(i + 1 < pl.num_programs(0))
    def _():
        gather_rows(dn_ref, 1 - slot)

    for k in range(TOP_K):
        pltpu.make_async_copy(y_hbm.at[pl.ds(0, tc)], ybuf.at[slot, k], sem.at[slot]).wait()
    wt = wt_ref[...]
    o_ref[...] = h_ref[...] + (wt[:, 0:1] * ybuf[slot, 0] + wt[:, 1:2] * ybuf[slot, 1])


def _combine(yrows, h, wts, dest, tc=128):
    T, D = h.shape
    steps = T // tc
    row = lambda i: (i, 0)
    dest3 = dest.reshape(steps, 1, tc * TOP_K)
    dest_block = lambda im: pl.BlockSpec((1, 1, tc * TOP_K), im, memory_space=pltpu.SMEM)
    return pl.pallas_call(
        functools.partial(_combine_kernel, tc=tc),
        grid=(steps,),
        in_specs=[dest_block(lambda i: (i, 0, 0)),
                  dest_block(lambda i: (jnp.minimum(i + 1, steps - 1), 0, 0)),
                  pl.BlockSpec(memory_space=pl.ANY),
                  pl.BlockSpec((tc, D), row),
                  pl.BlockSpec((tc, LANES), row)],
        out_specs=pl.BlockSpec((tc, D), row),
        out_shape=jax.ShapeDtypeStruct((T, D), F32),
        scratch_shapes=[pltpu.VMEM((2, TOP_K, tc, D), F32),
                        pltpu.SemaphoreType.DMA((2,))],
        compiler_params=_params(("arbitrary",)),
        name="combine",
    )(dest3, dest3, yrows, h, wts)


SUBLANES = 8
META_ROWS = 256


def _lane_cumsum(x):
    lane = lax.broadcasted_iota(jnp.int32, x.shape, 1)
    s = 1
    while s < LANES:
        x = x + jnp.where(lane >= s, pltpu.roll(x, s, 1), 0)
        s *= 2
    return x


def _dispatch_kernel(idx_ref, dest_ref, meta_ref, run_ref, prefix_ref, start_ref, *, tb):
    p = pl.program_id(0)
    i = pl.program_id(1)
    idx = idx_ref[...]
    lane = lax.broadcasted_iota(jnp.int32, idx.shape, 1)
    e1 = idx[:, 0:1]
    e2 = idx[:, 1:2]
    onehot = jnp.where(jnp.logical_or(lane == e1, lane == e2), 1.0, 0.0)

    @pl.when(jnp.logical_and(p == 0, i == 0))
    def _():
        run_ref[...] = jnp.zeros(run_ref.shape, F32)

    @pl.when(p == 0)
    def _():
        prefix_ref[i] = run_ref[...]
        run_ref[...] = run_ref[...] + jnp.sum(onehot, axis=0, keepdims=True)

    @pl.when(jnp.logical_and(p == 1, i == 0))
    def _():
        counts = run_ref[...].astype(jnp.int32)
        nblk = (counts + (MOE_BLOCK - 1)) // MOE_BLOCK
        end_blk = _lane_cumsum(nblk)
        start_ref[...] = ((end_blk - nblk) * MOE_BLOCK).astype(F32)
        has = jnp.where(counts > 0, 1, 0)
        ordinal = _lane_cumsum(has) - 1
        end_blk, has, ordinal = end_blk[0:1], has[0:1], ordinal[0:1]
        rows = lax.broadcasted_iota(jnp.int32, (META_ROWS, LANES), 0)
        lanes = lax.broadcasted_iota(jnp.int32, (META_ROWS, LANES), 1)
        is_e = lanes < N_EXPERTS
        rsum = lambda v: jnp.sum(v, axis=-1, keepdims=True)
        be = rsum(jnp.where(jnp.logical_and(is_e, end_blk <= rows), 1, 0))
        be = jnp.minimum(be, N_EXPERTS - 1)
        eo = rsum(jnp.where(lanes == be, ordinal, 0))
        ue = rsum(jnp.where(jnp.logical_and(has > 0, ordinal == rows), lanes, 0))
        n_blk = rsum(jnp.where(lanes == N_EXPERTS - 1, end_blk, 0))
        n_exp = rsum(jnp.where(is_e, has, 0))
        meta_ref[...] = jnp.where(
            lanes == 0, be, jnp.where(lanes == 1, eo, jnp.where(
                lanes == 2, ue, jnp.where(lanes == 3, n_blk, n_exp))))

    @pl.when(p == 1)
    def _():
        r = lax.broadcasted_iota(jnp.int32, (tb, tb), 0)
        c = lax.broadcasted_iota(jnp.int32, (tb, tb), 1)
        earlier = jnp.where(c < r, 1.0, 0.0).astype(BF16)
        rank = jnp.dot(earlier, onehot.astype(BF16), preferred_element_type=F32)
        rank = rank + prefix_ref[i][0:1] + start_ref[0:1]
        d1 = jnp.sum(jnp.where(lane == e1, rank, 0.0), axis=-1, keepdims=True)
        d2 = jnp.sum(jnp.where(lane == e2, rank, 0.0), axis=-1, keepdims=True)
        dest_ref[...] = jnp.where(lane == 0, d1, d2).astype(jnp.int32)


def _dispatch(idx, tb=512):
    T = idx.shape[0]
    n_rows = T * TOP_K + N_EXPERTS * MOE_BLOCK
    n_blocks = n_rows // MOE_BLOCK
    assert n_blocks <= META_ROWS
    dest2, meta = pl.pallas_call(
        functools.partial(_dispatch_kernel, tb=tb),
        grid=(2, T // tb),
        in_specs=[pl.BlockSpec((tb, LANES), lambda p, i: (i, 0))],
        out_specs=[pl.BlockSpec((tb, LANES), lambda p, i: (i * p, 0)),
                   pl.BlockSpec((META_ROWS, LANES), lambda p, i: (0, 0))],
        out_shape=[jax.ShapeDtypeStruct((T, LANES), jnp.int32),
                   jax.ShapeDtypeStruct((META_ROWS, LANES), jnp.int32)],
        scratch_shapes=[pltpu.VMEM((SUBLANES, LANES), F32),
                        pltpu.VMEM((T // tb, SUBLANES, LANES), F32),
                        pltpu.VMEM((SUBLANES, LANES), F32)],
        compiler_params=_params(("arbitrary", "arbitrary")),
        name="dispatch",
    )(idx)
    dest = dest2[:, :TOP_K].reshape(T * TOP_K)
    tok = jnp.repeat(jnp.arange(T, dtype=jnp.int32), TOP_K)
    row_tok = jnp.zeros((n_rows,), jnp.int32).at[dest].set(tok)
    block_e = meta[:n_blocks, 0]
    block_ord = meta[:n_blocks, 1]
    used_experts = meta[:N_EXPERTS, 2]
    n_used = meta[0, 3:5]
    return block_e, block_ord, used_experts, row_tok, n_used, dest


def kernel(x, positions, norm1_g, w_in, q_norm_g, k_norm_g, sink_logits, sg_ln_g, sg_ln_b, sg_w, sg_b, w_branch_att, w_branch_sg, w_out, norm2_g, w_group_router, b_group_router, w_expert_router, b_expert_router, w_gate, w_up, w_down):
    B, S, D = x.shape
    T = B * S
    h = x.reshape(T, D)
    pos = positions.reshape(T)
    for l in range(norm1_g.shape[0]):
        xn = _rmsnorm(h, norm1_g[l])
        proj = _in_proj(xn, w_in[l])
        q, k, v = _qkv_prep(proj, pos, q_norm_g[l], k_norm_g[l])
        att = _attention(q, k, v, sink_logits[l], B)
        sgo = _spatial_gating(proj, sg_ln_g[l], sg_ln_b[l], sg_w[l], sg_b[l])
        merged = _merge(att, sgo, proj, w_branch_att[l], w_branch_sg[l])
        pad = LANES - N_GROUPS - N_EXPERTS
        w_router = jnp.concatenate(
            [w_group_router[l], w_expert_router[l], jnp.zeros((D, pad), F32)], axis=1).astype(BF16)
        b_router = jnp.concatenate(
            [b_group_router[l], b_expert_router[l], jnp.zeros((pad,), F32)]).reshape(1, LANES)
        h, hn, idx, wts = _out_proj(merged, w_out[l], h, norm2_g[l], w_router, b_router)
        block_e, block_ord, used_experts, row_tok, n_used, dest = _dispatch(idx)
        yrows = _experts(hn, w_gate[l], w_up[l], w_down[l],
                         block_e, block_ord, used_experts, row_tok, n_used)
        h = _combine(yrows, h, wts, dest)
    return h.reshape(B, S, D)
```

```python
import functools

import jax
import jax.numpy as jnp
from jax import lax
from jax.experimental import pallas as pl
from jax.experimental.pallas import tpu as pltpu

F32 = jnp.float32
BF16 = jnp.bfloat16

D_MODEL = 2048
HEAD_DIM = 64
ATT_WIDTH = D_MODEL // 2
ATT_HEADS = ATT_WIDTH // HEAD_DIM
ATT_KV_HEADS = ATT_HEADS // 4
Q_PER_KV = ATT_HEADS // ATT_KV_HEADS
KV_WIDTH = ATT_KV_HEADS * HEAD_DIM
WINDOW = 128
ATT_BLOCK = 128
ROPE_DIM = HEAD_DIM // 4
ROPE_HALF = ROPE_DIM // 2
ROPE_THETA = 500000.0
SG_WIDTH = D_MODEL // 2
SG_GROUP_DIM = 128
SG_GROUPS = SG_WIDTH // SG_GROUP_DIM
SG_CHUNK = 128
OFF_Q = 0
OFF_K = OFF_Q + ATT_WIDTH
OFF_V = OFF_K + KV_WIDTH
OFF_U = OFF_V + KV_WIDTH
OFF_S = OFF_U + SG_WIDTH
OFF_G = OFF_S + SG_WIDTH
IN_COLS = OFF_G + 2 * D_MODEL
N_GROUPS = 8
EXPERTS_PER_GROUP = 8
N_EXPERTS = N_GROUPS * EXPERTS_PER_GROUP
TOP_K = 2
EXPERT_FF = D_MODEL // 4
MOE_BLOCK = 128
EPS = 1e-6
NEG_INF = -1e30

LANES = 128
HN_WORD_CHUNKS = D_MODEL // LANES // 2
ROWS_PER_ISSUE = 8
ROW_AHEAD = 4
ROW_SLOTS = ROW_AHEAD + 1
COMBINE_AHEAD = 2
COMBINE_SLOTS = COMBINE_AHEAD + 1
VMEM_LIMIT = 56 * 1024 * 1024


def _params(sem, vmem=VMEM_LIMIT):
    return pltpu.CompilerParams(dimension_semantics=sem, vmem_limit_bytes=vmem)


def _rmsnorm_kernel(x_ref, g_ref, o_ref):
    x = x_ref[...]
    r = lax.rsqrt(jnp.mean(x * x, axis=-1, keepdims=True) + EPS)
    o_ref[...] = ((x * r) * g_ref[...]).astype(o_ref.dtype)


def _rmsnorm(x, g, tm=512):
    T, D = x.shape
    return pl.pallas_call(
        _rmsnorm_kernel,
        grid=(T // tm,),
        in_specs=[pl.BlockSpec((tm, D), lambda i: (i, 0)),
                  pl.BlockSpec((1, D), lambda i: (0, 0))],
        out_specs=pl.BlockSpec((tm, D), lambda i: (i, 0)),
        out_shape=jax.ShapeDtypeStruct((T, D), BF16),
        compiler_params=_params(("parallel",)),
        name="norm1",
    )(x, g.reshape(1, D))


def _proj_kernel(x_ref, w_ref, o_ref, wbf_ref):
    @pl.when(pl.program_id(1) == 0)
    def _():
        wbf_ref[...] = w_ref[...].astype(BF16)

    o_ref[...] = jnp.dot(x_ref[...], wbf_ref[...], preferred_element_type=F32)


def _in_proj(xn, w, tm=512, tn=1280):
    T, D = xn.shape
    N = w.shape[1]
    return pl.pallas_call(
        _proj_kernel,
        grid=(N // tn, T // tm),
        in_specs=[pl.BlockSpec((tm, D), lambda j, i: (i, 0)),
                  pl.BlockSpec((D, tn), lambda j, i: (0, j))],
        out_specs=pl.BlockSpec((tm, tn), lambda j, i: (i, j)),
        out_shape=jax.ShapeDtypeStruct((T, N), F32),
        scratch_shapes=[pltpu.VMEM((D, tn), BF16)],
        compiler_params=_params(("arbitrary", "arbitrary")),
        name="in_proj",
    )(xn, w)


def _rope_table_kernel(pos_ref, invf_ref, cos_ref, sin_ref):
    ang = pos_ref[...].astype(F32) * invf_ref[...]
    cos_ref[...] = jnp.cos(ang)
    sin_ref[...] = jnp.sin(ang)


def _rope_tables(positions):
    T = positions.shape[0]
    rows = T * ROPE_HALF // LANES
    inv = ROPE_THETA ** (-jnp.arange(0, ROPE_DIM, 2, dtype=F32) / ROPE_DIM)
    invf = jnp.tile(inv, LANES // ROPE_HALF).reshape(1, LANES)
    pos = jnp.repeat(positions, ROPE_HALF).reshape(rows, LANES)
    whole = lambda: (0, 0)
    cos, sin = pl.pallas_call(
        _rope_table_kernel,
        in_specs=[pl.BlockSpec((rows, LANES), whole), pl.BlockSpec((1, LANES), whole)],
        out_specs=[pl.BlockSpec((rows, LANES), whole), pl.BlockSpec((rows, LANES), whole)],
        out_shape=[jax.ShapeDtypeStruct((rows, LANES), F32)] * 2,
        name="rope_tables",
    )(pos, invf)
    return cos.reshape(T, ROPE_HALF), sin.reshape(T, ROPE_HALF)


def _qkv_prep_kernel(p_ref, cos_ref, sin_ref, gq_ref, gk_ref, q_ref, k_ref, v_ref):
    cos = cos_ref[...]
    sin = sin_ref[...]
    lane = lax.broadcasted_iota(jnp.int32, cos.shape, 1)
    low_head = lane < HEAD_DIM
    first_half = (lane % HEAD_DIM) < ROPE_HALF

    def norm_rope(x, g):
        x2 = x * x
        s_lo = jnp.sum(jnp.where(low_head, x2, 0.0), axis=-1, keepdims=True)
        s_hi = jnp.sum(jnp.where(low_head, 0.0, x2), axis=-1, keepdims=True)
        ssq = jnp.where(low_head, s_lo, s_hi)
        xn = (x * lax.rsqrt(ssq * (1.0 / HEAD_DIM) + EPS)) * g
        partner = jnp.where(first_half,
                            pltpu.roll(xn, LANES - ROPE_HALF, 1),
                            pltpu.roll(xn, ROPE_HALF, 1))
        return xn * cos + partner * sin

    for c in range(ATT_WIDTH // LANES):
        x = p_ref[:, OFF_Q + c * LANES:OFF_Q + (c + 1) * LANES]
        q_ref[:, c * LANES:(c + 1) * LANES] = (
            norm_rope(x, gq_ref[...]) * (HEAD_DIM ** -0.5)).astype(q_ref.dtype)
    for c in range(KV_WIDTH // LANES):
        x = p_ref[:, OFF_K + c * LANES:OFF_K + (c + 1) * LANES]
        k_ref[:, c * LANES:(c + 1) * LANES] = norm_rope(x, gk_ref[...]).astype(k_ref.dtype)
    v_ref[...] = p_ref[:, OFF_V:OFF_V + KV_WIDTH].astype(v_ref.dtype)


def _qkv_prep(proj, positions, q_g, k_g, tq=256):
    T = proj.shape[0]
    width = OFF_U
    cos8, sin8 = _rope_tables(positions)
    rest = HEAD_DIM - ROPE_DIM
    cos_t = jnp.tile(jnp.concatenate([cos8, cos8, jnp.ones((T, rest), F32)], axis=1),
                     (1, LANES // HEAD_DIM))
    sin_t = jnp.tile(jnp.concatenate([-sin8, sin8, jnp.zeros((T, rest), F32)], axis=1),
                     (1, LANES // HEAD_DIM))
    gq = jnp.tile(q_g, LANES // HEAD_DIM).reshape(1, LANES)
    gk = jnp.tile(k_g, LANES // HEAD_DIM).reshape(1, LANES)
    row = lambda i: (i, 0)
    const = lambda i: (0, 0)
    return pl.pallas_call(
        _qkv_prep_kernel,
        grid=(T // tq,),
        in_specs=[pl.BlockSpec((tq, width), row),
                  pl.BlockSpec((tq, LANES), row),
                  pl.BlockSpec((tq, LANES), row),
                  pl.BlockSpec((1, LANES), const),
                  pl.BlockSpec((1, LANES), const)],
        out_specs=[pl.BlockSpec((tq, ATT_WIDTH), row),
                   pl.BlockSpec((tq, KV_WIDTH), row),
                   pl.BlockSpec((tq, KV_WIDTH), row)],
        out_shape=[jax.ShapeDtypeStruct((T, ATT_WIDTH), BF16),
                   jax.ShapeDtypeStruct((T, KV_WIDTH), BF16),
                   jax.ShapeDtypeStruct((T, KV_WIDTH), BF16)],
        compiler_params=_params(("parallel",)),
        name="qkv_prep",
    )(proj, cos_t, sin_t, gq, gk)


def _attn_kernel(sink_ref, q_ref, kp_ref, kc_ref, kn_ref, vp_ref, vc_ref, vn_ref, o_ref, *, nb):
    n = pl.program_id(1)
    rows = Q_PER_KV * ATT_BLOCK
    keys = 3 * ATT_BLOCK
    qi = lax.broadcasted_iota(jnp.int32, (rows, ATT_BLOCK), 0) % ATT_BLOCK
    kj = lax.broadcasted_iota(jnp.int32, (rows, ATT_BLOCK), 1)
    lo_prev = jnp.where(n > 0, 0, ATT_BLOCK)
    hi_next = jnp.where(n < nb - 1, 0, -ATT_BLOCK)
    cap_prev = jnp.where(kj - qi >= lo_prev, jnp.inf, NEG_INF)
    cap_next = jnp.where(kj - qi <= hi_next, jnp.inf, NEG_INF)
    cap = jnp.concatenate([cap_prev, jnp.full((rows, ATT_BLOCK), jnp.inf, F32), cap_next], axis=1)
    row_head = lax.broadcasted_iota(jnp.int32, (rows, 1), 0) // ATT_BLOCK

    for kvh in range(ATT_KV_HEADS):
        cols = slice(kvh * HEAD_DIM, (kvh + 1) * HEAD_DIM)
        k = jnp.concatenate([kp_ref[:, cols], kc_ref[:, cols], kn_ref[:, cols]], axis=0)
        v = jnp.concatenate([vp_ref[:, cols], vc_ref[:, cols], vn_ref[:, cols]], axis=0)
        q = jnp.concatenate(
            [q_ref[:, (kvh * Q_PER_KV + g) * HEAD_DIM:(kvh * Q_PER_KV + g + 1) * HEAD_DIM]
             for g in range(Q_PER_KV)], axis=0)
        sink = jnp.zeros((rows, 1), F32)
        for g in range(Q_PER_KV):
            sink = jnp.where(row_head == g, sink_ref[kvh * Q_PER_KV + g], sink)
        s = lax.dot_general(q, k, (((1,), (1,)), ((), ())), preferred_element_type=F32)
        s = jnp.minimum(s, cap)
        m = jnp.maximum(jnp.max(s, axis=-1, keepdims=True), sink)
        e = jnp.exp(s - m)
        denom = jnp.sum(e, axis=-1, keepdims=True) + jnp.exp(sink - m)
        p = (e / denom).astype(BF16)
        o = jnp.dot(p, v, preferred_element_type=F32)
        for g in range(Q_PER_KV):
            h = kvh * Q_PER_KV + g
            o_ref[:, h * HEAD_DIM:(h + 1) * HEAD_DIM] = (
                o[g * ATT_BLOCK:(g + 1) * ATT_BLOCK].astype(o_ref.dtype))


def _attention(q, k, v, sink, batch):
    T = q.shape[0]
    nb = T // batch // ATT_BLOCK
    cur = lambda b, n: (b * nb + n, 0)
    prev = lambda b, n: (b * nb + jnp.maximum(n - 1, 0), 0)
    nxt = lambda b, n: (b * nb + jnp.minimum(n + 1, nb - 1), 0)
    kv = lambda im: pl.BlockSpec((ATT_BLOCK, KV_WIDTH), im)
    return pl.pallas_call(
        functools.partial(_attn_kernel, nb=nb),
        grid=(batch, nb),
        in_specs=[pl.BlockSpec(memory_space=pltpu.SMEM),
                  pl.BlockSpec((ATT_BLOCK, ATT_WIDTH), cur),
                  kv(prev), kv(cur), kv(nxt), kv(prev), kv(cur), kv(nxt)],
        out_specs=pl.BlockSpec((ATT_BLOCK, ATT_WIDTH), cur),
        out_shape=jax.ShapeDtypeStruct((T, ATT_WIDTH), BF16),
        compiler_params=_params(("parallel", "parallel")),
        name="window_attn",
    )(sink, q, k, k, k, v, v, v)


def _sg_kernel(u_ref, s_ref, lng_ref, lnb_ref, w_ref, b_ref, o_ref):
    groups = w_ref.shape[0]

    def one_chunk(ci, carry):
        rows = pl.ds(pl.multiple_of(ci * SG_CHUNK, SG_CHUNK), SG_CHUNK)
        for gi in range(groups):
            cols = slice(gi * SG_GROUP_DIM, (gi + 1) * SG_GROUP_DIM)
            s = jax.nn.gelu(s_ref[rows, cols])
            mu = jnp.mean(s, axis=-1, keepdims=True)
            sc = s - mu
            var = jnp.mean(sc * sc, axis=-1, keepdims=True)
            sn = (sc * lax.rsqrt(var + EPS)) * lng_ref[0, gi:gi + 1, :] + lnb_ref[0, gi:gi + 1, :]
            mixed = jnp.dot(w_ref[gi], sn.astype(BF16), preferred_element_type=F32)
            mixed = mixed + b_ref[0, :, gi:gi + 1]
            o_ref[rows, cols] = (jax.nn.gelu(u_ref[rows, cols]) * mixed).astype(o_ref.dtype)
        return carry

    lax.fori_loop(0, u_ref.shape[0] // SG_CHUNK, one_chunk, 0)


def _spatial_gating(proj, ln_g, ln_b, w_s, b_s, halves=2, chunks=4):
    T = proj.shape[0]
    half_w = SG_WIDTH // halves
    gph = SG_GROUPS // halves
    u0, s0 = OFF_U // half_w, OFF_S // half_w
    tq = chunks * SG_CHUNK
    return pl.pallas_call(
        _sg_kernel,
        grid=(T // tq, halves),
        in_specs=[pl.BlockSpec((tq, half_w), lambda i, j: (i, u0 + j)),
                  pl.BlockSpec((tq, half_w), lambda i, j: (i, s0 + j)),
                  pl.BlockSpec((1, gph, SG_GROUP_DIM), lambda i, j: (j, 0, 0)),
                  pl.BlockSpec((1, gph, SG_GROUP_DIM), lambda i, j: (j, 0, 0)),
                  pl.BlockSpec((gph, SG_CHUNK, SG_CHUNK), lambda i, j: (j, 0, 0)),
                  pl.BlockSpec((1, SG_CHUNK, gph), lambda i, j: (j, 0, 0))],
        out_specs=pl.BlockSpec((tq, half_w), lambda i, j: (i, j)),
        out_shape=jax.ShapeDtypeStruct((T, SG_WIDTH), BF16),
        compiler_params=_params(("parallel", "parallel")),
        name="spatial_gating",
    )(proj, proj,
      ln_g.reshape(halves, gph, SG_GROUP_DIM), ln_b.reshape(halves, gph, SG_GROUP_DIM),
      w_s.astype(BF16),
      b_s.reshape(halves, gph, SG_CHUNK).transpose(0, 2, 1))


def _merge_kernel(att_ref, sgo_ref, ga_ref, gb_ref, wa_ref, wb_ref, o_ref):
    a = jnp.dot(att_ref[...], wa_ref[...], preferred_element_type=F32)
    b = jnp.dot(sgo_ref[...], wb_ref[...], preferred_element_type=F32)
    m = jax.nn.sigmoid(ga_ref[...]) * a + jax.nn.sigmoid(gb_ref[...]) * b
    o_ref[...] = m.astype(o_ref.dtype)


def _merge(att, sgo, proj, w_a, w_b, tm=1024, tn=512):
    T = att.shape[0]
    ga0 = OFF_G // tn
    gb0 = (OFF_G + D_MODEL) // tn
    return pl.pallas_call(
        _merge_kernel,
        grid=(T // tm, D_MODEL // tn),
        in_specs=[pl.BlockSpec((tm, ATT_WIDTH), lambda i, j: (i, 0)),
                  pl.BlockSpec((tm, SG_WIDTH), lambda i, j: (i, 0)),
                  pl.BlockSpec((tm, tn), lambda i, j: (i, ga0 + j)),
                  pl.BlockSpec((tm, tn), lambda i, j: (i, gb0 + j)),
                  pl.BlockSpec((ATT_WIDTH, tn), lambda i, j: (0, j)),
                  pl.BlockSpec((SG_WIDTH, tn), lambda i, j: (0, j))],
        out_specs=pl.BlockSpec((tm, tn), lambda i, j: (i, j)),
        out_shape=jax.ShapeDtypeStruct((T, D_MODEL), BF16),
        compiler_params=_params(("parallel", "parallel")),
        name="merge",
    )(att, sgo, proj, proj, w_a.astype(BF16), w_b.astype(BF16))


def _route(logits):
    lane = lax.broadcasted_iota(jnp.int32, logits.shape, 1)
    lane_f = lane.astype(F32)
    is_g = lane < N_GROUPS
    gl = jnp.where(is_g, logits, -jnp.inf)
    gmax = jnp.max(gl, axis=-1, keepdims=True)
    grp = jnp.min(jnp.where(gl == gmax, lane_f, float(LANES)), axis=-1, keepdims=True)
    gsum = jnp.sum(jnp.where(is_g, jnp.exp(logits - gmax), 0.0), axis=-1, keepdims=True)
    g_w = 1.0 / gsum
    e_lane = lane - N_GROUPS
    in_grp = jnp.logical_and(
        jnp.logical_and(e_lane >= 0, e_lane < N_EXPERTS),
        (e_lane // EXPERTS_PER_GROUP).astype(F32) == grp)
    el = jnp.where(in_grp, logits, -jnp.inf)
    v1 = jnp.max(el, axis=-1, keepdims=True)
    i1 = jnp.min(jnp.where(jnp.logical_and(in_grp, el == v1), lane_f, float(LANES)),
                 axis=-1, keepdims=True)
    rest = jnp.logical_and(in_grp, lane_f != i1)
    el2 = jnp.where(rest, logits, -jnp.inf)
    v2 = jnp.max(el2, axis=-1, keepdims=True)
    i2 = jnp.min(jnp.where(jnp.logical_and(rest, el2 == v2), lane_f, float(LANES)),
                 axis=-1, keepdims=True)
    e21 = jnp.exp(v2 - v1)
    w1 = g_w / (1.0 + e21)
    w2 = g_w * e21 / (1.0 + e21)
    idx = jnp.where(lane == 0, i1, i2) - float(N_GROUPS)
    wts = jnp.where(lane == 0, w1, jnp.where(lane == 1, w2, 0.0))
    return idx.astype(jnp.int32), wts


def _out_kernel(m_ref, w_ref, x_ref, g_ref, wr_ref, br_ref, h_ref, hn_ref, idx_ref, wt_ref):
    h = x_ref[...] + jnp.dot(m_ref[...], w_ref[...], preferred_element_type=F32)
    h_ref[...] = h
    r = lax.rsqrt(jnp.mean(h * h, axis=-1, keepdims=True) + EPS)
    hn = ((h * r) * g_ref[...]).astype(BF16)
    bits = lax.bitcast_convert_type(hn.astype(F32), jnp.uint32)
    for c in range(HN_WORD_CHUNKS):
        hi = bits[:, c * LANES:(c + 1) * LANES]
        lo = bits[:, (c + HN_WORD_CHUNKS) * LANES:(c + HN_WORD_CHUNKS + 1) * LANES]
        hn_ref[pl.ds(c, hn.shape[0], stride=HN_WORD_CHUNKS), :] = hi | (lo >> 16)
    logits = jnp.dot(hn, wr_ref[...], preferred_element_type=F32) + br_ref[...]
    idx, wts = _route(logits)
    idx_ref[...] = idx
    wt_ref[...] = wts


def _out_proj(merged, w_out, x, g2, w_router, b_router, tm=256):
    T, D = x.shape
    row = lambda i: (i, 0)
    const = lambda i: (0, 0)
    return pl.pallas_call(
        _out_kernel,
        grid=(T // tm,),
        in_specs=[pl.BlockSpec((tm, D), row),
                  pl.BlockSpec((D, D), const),
                  pl.BlockSpec((tm, D), row),
                  pl.BlockSpec((1, D), const),
                  pl.BlockSpec((D, LANES), const),
                  pl.BlockSpec((1, LANES), const)],
        out_specs=[pl.BlockSpec((tm, D), row),
                   pl.BlockSpec((tm * HN_WORD_CHUNKS, LANES), row),
                   pl.BlockSpec((tm, LANES), row),
                   pl.BlockSpec((tm, LANES), row)],
        out_shape=[jax.ShapeDtypeStruct((T, D), F32),
                   jax.ShapeDtypeStruct((T * HN_WORD_CHUNKS, LANES), jnp.uint32),
                   jax.ShapeDtypeStruct((T, LANES), jnp.int32),
                   jax.ShapeDtypeStruct((T, LANES), F32)],
        compiler_params=_params(("parallel",)),
        name="out_proj_router",
    )(merged, w_out.astype(BF16), x, g2.reshape(1, D), w_router, b_router)


def _expert_kernel(be_ref, eo_ref, ue_ref, nu_ref, *refs):
    rt_refs = refs[:ROW_SLOTS]
    (hn_hbm, wg_hbm, wu_hbm, wd_hbm, y_ref,
     xbuf, wgf, wuf, wdf, wgb, wub, wdb, sem, wsem) = refs[ROW_SLOTS:]
    b = pl.program_id(0)
    n_used = nu_ref[0]
    n_exp = nu_ref[1]
    used = b < n_used
    slot = b % ROW_SLOTS

    def weight_copies(ordinal, slot_):
        e = ue_ref[ordinal]
        return (pltpu.make_async_copy(wg_hbm.at[e], wgf.at[slot_], wsem.at[slot_]),
                pltpu.make_async_copy(wu_hbm.at[e], wuf.at[slot_], wsem.at[slot_]),
                pltpu.make_async_copy(wd_hbm.at[e], wdf.at[slot_], wsem.at[slot_]))

    def start_weights(ordinal):
        for c in weight_copies(ordinal, ordinal % 2):
            c.start(priority=1)

    def gather_rows(tok_ref, slot_):
        def issue(g, c):
            for j in range(ROWS_PER_ISSUE):
                r = g * ROWS_PER_ISSUE + j
                src = pl.multiple_of(tok_ref[0, 0, r] * HN_WORD_CHUNKS, HN_WORD_CHUNKS)
                dst = pl.multiple_of(r * HN_WORD_CHUNKS, HN_WORD_CHUNKS)
                pltpu.make_async_copy(hn_hbm.at[pl.ds(src, HN_WORD_CHUNKS)],
                                      xbuf.at[slot_, pl.ds(dst, HN_WORD_CHUNKS)],
                                      sem.at[slot_]).start()
            return c

        lax.fori_loop(0, MOE_BLOCK // ROWS_PER_ISSUE, issue, 0)

    @pl.when(b == 0)
    def _():
        start_weights(0)

        @pl.when(n_exp > 1)
        def _():
            start_weights(1)

        gather_rows(rt_refs[0], 0)
        for a in range(1, ROW_AHEAD):
            @pl.when(n_used > a)
            def _(a=a):
                gather_rows(rt_refs[a], a)

    @pl.when(b + ROW_AHEAD < n_used)
    def _():
        gather_rows(rt_refs[ROW_AHEAD], (b + ROW_AHEAD) % ROW_SLOTS)

    @pl.when(used)
    def _():
        first = jnp.logical_or(b == 0, be_ref[b] != be_ref[jnp.maximum(b - 1, 0)])

        @pl.when(first)
        def _():
            ordinal = eo_ref[b]
            ws = ordinal % 2
            for c in weight_copies(ordinal, ws):
                c.wait()
            wgb[...] = wgf[ws].astype(BF16)
            wub[...] = wuf[ws].astype(BF16)
            wdb[...] = wdf[ws].astype(BF16)

            @pl.when(ordinal + 2 < n_exp)
            def _():
                start_weights(ordinal + 2)

        pltpu.make_async_copy(hn_hbm.at[pl.ds(0, MOE_BLOCK * HN_WORD_CHUNKS)], xbuf.at[slot],
                              sem.at[slot]).wait()

        halves = ([], [])
        for c in range(HN_WORD_CHUNKS):
            w = xbuf[slot, pl.ds(c, MOE_BLOCK, stride=HN_WORD_CHUNKS), :]
            hi = lax.bitcast_convert_type(w & jnp.uint32(0xFFFF0000), F32)
            lo = lax.bitcast_convert_type(w << 16, F32)
            halves[0].append(hi.astype(BF16))
            halves[1].append(lo.astype(BF16))
        x = jnp.concatenate(halves[0] + halves[1], axis=1)
        hg = jnp.dot(x, wgb[...], preferred_element_type=F32)
        hu = jnp.dot(x, wub[...], preferred_element_type=F32)
        hdn = (jax.nn.silu(hg) * hu).astype(BF16)
        y_ref[...] = jnp.dot(hdn, wdb[...], preferred_element_type=F32)

    @pl.when(jnp.logical_not(used))
    def _():
        y_ref[...] = jnp.zeros(y_ref.shape, y_ref.dtype)


def _experts(hn, w_gate, w_up, w_down, block_e, block_ord, used_experts, row_tok, n_used):
    D = w_gate.shape[1]
    n_rows = row_tok.shape[0]
    n_blocks = n_rows // MOE_BLOCK
    tok3 = row_tok.reshape(n_blocks, 1, MOE_BLOCK)
    hbm = pl.BlockSpec(memory_space=pl.ANY)
    tok_block = lambda im: pl.BlockSpec((1, 1, MOE_BLOCK), im, memory_space=pltpu.SMEM)
    grid_spec = pltpu.PrefetchScalarGridSpec(
        num_scalar_prefetch=4,
        grid=(n_blocks,),
        in_specs=[tok_block(lambda b, *_, a=a: (jnp.minimum(b + a, n_blocks - 1), 0, 0))
                  for a in range(ROW_SLOTS)] + [hbm, hbm, hbm, hbm],
        out_specs=pl.BlockSpec((MOE_BLOCK, D), lambda b, *_: (b, 0)),
        scratch_shapes=[pltpu.VMEM((ROW_SLOTS, MOE_BLOCK * HN_WORD_CHUNKS, LANES), jnp.uint32),
                        pltpu.VMEM((2, D, EXPERT_FF), F32),
                        pltpu.VMEM((2, D, EXPERT_FF), F32),
                        pltpu.VMEM((2, EXPERT_FF, D), F32),
                        pltpu.VMEM((D, EXPERT_FF), BF16),
                        pltpu.VMEM((D, EXPERT_FF), BF16),
                        pltpu.VMEM((EXPERT_FF, D), BF16),
                        pltpu.SemaphoreType.DMA((ROW_SLOTS,)),
                        pltpu.SemaphoreType.DMA((2,))],
    )
    return pl.pallas_call(
        _expert_kernel,
        grid_spec=grid_spec,
        out_shape=jax.ShapeDtypeStruct((n_rows, D), F32),
        compiler_params=_params(("arbitrary",)),
        name="experts",
    )(block_e, block_ord, used_experts, n_used, *([tok3] * ROW_SLOTS), hn, w_gate, w_up, w_down)


def _combine_kernel(*refs, tc):
    dest_refs = refs[:COMBINE_SLOTS]
    y_hbm, h_ref, wt_ref, o_ref, ybuf, sem = refs[COMBINE_SLOTS:]
    i = pl.program_id(0)
    slot = i % COMBINE_SLOTS

    def gather_rows(dest_ref, slot_):
        def issue(r, c):
            for k in range(TOP_K):
                src = dest_ref[0, 0, r * TOP_K + k]
                pltpu.make_async_copy(y_hbm.at[pl.ds(src, 1)],
                                      ybuf.at[slot_, k, pl.ds(r, 1)],
                                      sem.at[slot_]).start(priority=k % 2)
            return c

        lax.fori_loop(0, tc, issue, 0, unroll=4)

    @pl.when(i == 0)
    def _():
        for a in range(COMBINE_AHEAD):
            gather_rows(dest_refs[a], a)

    @pl.when(i + COMBINE_AHEAD < pl.num_programs(0))
    def _():
        gather_rows(dest_refs[COMBINE_AHEAD], (i + COMBINE_AHEAD) % COMBINE_SLOTS)

    for k in range(TOP_K):
        pltpu.make_async_copy(y_hbm.at[pl.ds(0, tc)], ybuf.at[slot, k], sem.at[slot]).wait()
    wt = wt_ref[...]
    o_ref[...] = h_ref[...] + (wt[:, 0:1] * ybuf[slot, 0] + wt[:, 1:2] * ybuf[slot, 1])


def _combine(yrows, h, wts, dest, tc=128):
    T, D = h.shape
    steps = T // tc
    row = lambda i: (i, 0)
    dest3 = dest.reshape(steps, 1, tc * TOP_K)
    dest_block = lambda im: pl.BlockSpec((1, 1, tc * TOP_K), im, memory_space=pltpu.SMEM)
    return pl.pallas_call(
        functools.partial(_combine_kernel, tc=tc),
        grid=(steps,),
        in_specs=[dest_block(lambda i, a=a: (jnp.minimum(i + a, steps - 1), 0, 0))
                  for a in range(COMBINE_SLOTS)]
                 + [pl.BlockSpec(memory_space=pl.ANY),
                    pl.BlockSpec((tc, D), row),
                    pl.BlockSpec((tc, LANES), row)],
        out_specs=pl.BlockSpec((tc, D), row),
        out_shape=jax.ShapeDtypeStruct((T, D), F32),
        scratch_shapes=[pltpu.VMEM((COMBINE_SLOTS, TOP_K, tc, D), F32),
                        pltpu.SemaphoreType.DMA((COMBINE_SLOTS,))],
        compiler_params=_params(("arbitrary",)),
        name="combine",
    )(*([dest3] * COMBINE_SLOTS), yrows, h, wts)


SUBLANES = 8
META_ROWS = 256


def _lane_cumsum(x):
    lane = lax.broadcasted_iota(jnp.int32, x.shape, 1)
    s = 1
    while s < LANES:
        x = x + jnp.where(lane >= s, pltpu.roll(x, s, 1), 0)
        s *= 2
    return x


def _dispatch_kernel(idx_ref, dest_ref, meta_ref, run_ref, prefix_ref, start_ref, *, tb):
    p = pl.program_id(0)
    i = pl.program_id(1)
    idx = idx_ref[...]
    lane = lax.broadcasted_iota(jnp.int32, idx.shape, 1)
    e1 = idx[:, 0:1]
    e2 = idx[:, 1:2]
    onehot = jnp.where(jnp.logical_or(lane == e1, lane == e2), 1.0, 0.0)

    @pl.when(jnp.logical_and(p == 0, i == 0))
    def _():
        run_ref[...] = jnp.zeros(run_ref.shape, F32)

    @pl.when(p == 0)
    def _():
        prefix_ref[i] = run_ref[...]
        run_ref[...] = run_ref[...] + jnp.sum(onehot, axis=0, keepdims=True)

    @pl.when(jnp.logical_and(p == 1, i == 0))
    def _():
        counts = run_ref[...].astype(jnp.int32)
        nblk = (counts + (MOE_BLOCK - 1)) // MOE_BLOCK
        end_blk = _lane_cumsum(nblk)
        start_ref[...] = ((end_blk - nblk) * MOE_BLOCK).astype(F32)
        has = jnp.where(counts > 0, 1, 0)
        ordinal = _lane_cumsum(has) - 1
        end_blk, has, ordinal = end_blk[0:1], has[0:1], ordinal[0:1]
        rows = lax.broadcasted_iota(jnp.int32, (META_ROWS, LANES), 0)
        lanes = lax.broadcasted_iota(jnp.int32, (META_ROWS, LANES), 1)
        is_e = lanes < N_EXPERTS
        rsum = lambda v: jnp.sum(v, axis=-1, keepdims=True)
        be = rsum(jnp.where(jnp.logical_and(is_e, end_blk <= rows), 1, 0))
        be = jnp.minimum(be, N_EXPERTS - 1)
        eo = rsum(jnp.where(lanes == be, ordinal, 0))
        ue = rsum(jnp.where(jnp.logical_and(has > 0, ordinal == rows), lanes, 0))
        n_blk = rsum(jnp.where(lanes == N_EXPERTS - 1, end_blk, 0))
        n_exp = rsum(jnp.where(is_e, has, 0))
        meta_ref[...] = jnp.where(
            lanes == 0, be, jnp.where(lanes == 1, eo, jnp.where(
                lanes == 2, ue, jnp.where(lanes == 3, n_blk, n_exp))))

    @pl.when(p == 1)
    def _():
        r = lax.broadcasted_iota(jnp.int32, (tb, tb), 0)
        c = lax.broadcasted_iota(jnp.int32, (tb, tb), 1)
        earlier = jnp.where(c < r, 1.0, 0.0).astype(BF16)
        rank = jnp.dot(earlier, onehot.astype(BF16), preferred_element_type=F32)
        rank = rank + prefix_ref[i][0:1] + start_ref[0:1]
        d1 = jnp.sum(jnp.where(lane == e1, rank, 0.0), axis=-1, keepdims=True)
        d2 = jnp.sum(jnp.where(lane == e2, rank, 0.0), axis=-1, keepdims=True)
        dest_ref[...] = jnp.where(lane == 0, d1, d2).astype(jnp.int32)


def _dispatch(idx, tb=512):
    T = idx.shape[0]
    n_rows = T * TOP_K + N_EXPERTS * MOE_BLOCK
    n_blocks = n_rows // MOE_BLOCK
    assert n_blocks <= META_ROWS
    dest2, meta = pl.pallas_call(
        functools.partial(_dispatch_kernel, tb=tb),
        grid=(2, T // tb),
        in_specs=[pl.BlockSpec((tb, LANES), lambda p, i: (i, 0))],
        out_specs=[pl.BlockSpec((tb, LANES), lambda p, i: (i * p, 0)),
                   pl.BlockSpec((META_ROWS, LANES), lambda p, i: (0, 0))],
        out_shape=[jax.ShapeDtypeStruct((T, LANES), jnp.int32),
                   jax.ShapeDtypeStruct((META_ROWS, LANES), jnp.int32)],
        scratch_shapes=[pltpu.VMEM((SUBLANES, LANES), F32),
                        pltpu.VMEM((T // tb, SUBLANES, LANES), F32),
                        pltpu.VMEM((SUBLANES, LANES), F32)],
        compiler_params=_params(("arbitrary", "arbitrary")),
        name="dispatch",
    )(idx)
    dest = dest2[:, :TOP_K].reshape(T * TOP_K)
    tok = jnp.repeat(jnp.arange(T, dtype=jnp.int32), TOP_K)
    row_tok = jnp.zeros((n_rows,), jnp.int32).at[dest].set(tok)
    block_e = meta[:n_blocks, 0]
    block_ord = meta[:n_blocks, 1]
    used_experts = meta[:N_EXPERTS, 2]
    n_used = meta[0, 3:5]
    return block_e, block_ord, used_experts, row_tok, n_used, dest


def kernel(x, positions, norm1_g, w_in, q_norm_g, k_norm_g, sink_logits, sg_ln_g, sg_ln_b, sg_w, sg_b, w_branch_att, w_branch_sg, w_out, norm2_g, w_group_router, b_group_router, w_expert_router, b_expert_router, w_gate, w_up, w_down):
    B, S, D = x.shape
    T = B * S
    h = x.reshape(T, D)
    pos = positions.reshape(T)
    for l in range(norm1_g.shape[0]):
        xn = _rmsnorm(h, norm1_g[l])
        proj = _in_proj(xn, w_in[l])
        q, k, v = _qkv_prep(proj, pos, q_norm_g[l], k_norm_g[l])
        att = _attention(q, k, v, sink_logits[l], B)
        sgo = _spatial_gating(proj, sg_ln_g[l], sg_ln_b[l], sg_w[l], sg_b[l])
        merged = _merge(att, sgo, proj, w_branch_att[l], w_branch_sg[l])
        pad = LANES - N_GROUPS - N_EXPERTS
        w_router = jnp.concatenate(
            [w_group_router[l], w_expert_router[l], jnp.zeros((D, pad), F32)], axis=1).astype(BF16)
        b_router = jnp.concatenate(
            [b_group_router[l], b_expert_router[l], jnp.zeros((pad,), F32)]).reshape(1, LANES)
        h, hn, idx, wts = _out_proj(merged, w_out[l], h, norm2_g[l], w_router, b_router)
        block_e, block_ord, used_experts, row_tok, n_used, dest = _dispatch(idx)
        yrows = _experts(hn, w_gate[l], w_up[l], w_down[l],
                         block_e, block_ord, used_experts, row_tok, n_used)
        h = _combine(yrows, h, wts, dest)
    return h.reshape(B, S, D)
```

```python
import functools

import jax
import jax.numpy as jnp
from jax import lax
from jax.experimental import pallas as pl
from jax.experimental.pallas import tpu as pltpu

F32 = jnp.float32
BF16 = jnp.bfloat16

D_MODEL = 2048
HEAD_DIM = 64
ATT_WIDTH = D_MODEL // 2
ATT_HEADS = ATT_WIDTH // HEAD_DIM
ATT_KV_HEADS = ATT_HEADS // 4
Q_PER_KV = ATT_HEADS // ATT_KV_HEADS
KV_WIDTH = ATT_KV_HEADS * HEAD_DIM
WINDOW = 128
ATT_BLOCK = 128
ROPE_DIM = HEAD_DIM // 4
ROPE_HALF = ROPE_DIM // 2
ROPE_THETA = 500000.0
SG_WIDTH = D_MODEL // 2
SG_GROUP_DIM = 128
SG_GROUPS = SG_WIDTH // SG_GROUP_DIM
SG_CHUNK = 128
OFF_Q = 0
OFF_K = OFF_Q + ATT_WIDTH
OFF_V = OFF_K + KV_WIDTH
OFF_U = OFF_V + KV_WIDTH
OFF_S = OFF_U + SG_WIDTH
OFF_G = OFF_S + SG_WIDTH
IN_COLS = OFF_G + 2 * D_MODEL
N_GROUPS = 8
EXPERTS_PER_GROUP = 8
N_EXPERTS = N_GROUPS * EXPERTS_PER_GROUP
TOP_K = 2
EXPERT_FF = D_MODEL // 4
MOE_BLOCK = 128
EPS = 1e-6
NEG_INF = -1e30

LANES = 128
HN_WORD_CHUNKS = D_MODEL // LANES // 2
ROWS_PER_ISSUE = 8
ROW_AHEAD = 4
ROW_SLOTS = ROW_AHEAD + 1
COMBINE_AHEAD = 1
COMBINE_SLOTS = COMBINE_AHEAD + 1
VMEM_LIMIT = 56 * 1024 * 1024


def _params(sem, vmem=VMEM_LIMIT):
    return pltpu.CompilerParams(dimension_semantics=sem, vmem_limit_bytes=vmem)


def _rmsnorm_kernel(x_ref, g_ref, o_ref):
    x = x_ref[...]
    r = lax.rsqrt(jnp.mean(x * x, axis=-1, keepdims=True) + EPS)
    o_ref[...] = ((x * r) * g_ref[...]).astype(o_ref.dtype)


def _rmsnorm(x, g, tm=512):
    T, D = x.shape
    return pl.pallas_call(
        _rmsnorm_kernel,
        grid=(T // tm,),
        in_specs=[pl.BlockSpec((tm, D), lambda i: (i, 0)),
                  pl.BlockSpec((1, D), lambda i: (0, 0))],
        out_specs=pl.BlockSpec((tm, D), lambda i: (i, 0)),
        out_shape=jax.ShapeDtypeStruct((T, D), BF16),
        compiler_params=_params(("parallel",)),
        name="norm1",
    )(x, g.reshape(1, D))


def _proj_kernel(x_ref, w_ref, o_ref, wbf_ref):
    @pl.when(pl.program_id(1) == 0)
    def _():
        wbf_ref[...] = w_ref[...].astype(BF16)

    o_ref[...] = jnp.dot(x_ref[...], wbf_ref[...], preferred_element_type=F32)


def _in_proj(xn, w, tm=1024, tn=1280):
    T, D = xn.shape
    N = w.shape[1]
    return pl.pallas_call(
        _proj_kernel,
        grid=(N // tn, T // tm),
        in_specs=[pl.BlockSpec((tm, D), lambda j, i: (i, 0)),
                  pl.BlockSpec((D, tn), lambda j, i: (0, j))],
        out_specs=pl.BlockSpec((tm, tn), lambda j, i: (i, j)),
        out_shape=jax.ShapeDtypeStruct((T, N), F32),
        scratch_shapes=[pltpu.VMEM((D, tn), BF16)],
        compiler_params=_params(("arbitrary", "arbitrary")),
        name="in_proj",
    )(xn, w)


def _rope_table_kernel(pos_ref, invf_ref, cos_ref, sin_ref):
    ang = pos_ref[...].astype(F32) * invf_ref[...]
    cos_ref[...] = jnp.cos(ang)
    sin_ref[...] = jnp.sin(ang)


def _rope_tables(positions):
    T = positions.shape[0]
    rows = T * ROPE_HALF // LANES
    inv = ROPE_THETA ** (-jnp.arange(0, ROPE_DIM, 2, dtype=F32) / ROPE_DIM)
    invf = jnp.tile(inv, LANES // ROPE_HALF).reshape(1, LANES)
    pos = jnp.repeat(positions, ROPE_HALF).reshape(rows, LANES)
    whole = lambda: (0, 0)
    cos, sin = pl.pallas_call(
        _rope_table_kernel,
        in_specs=[pl.BlockSpec((rows, LANES), whole), pl.BlockSpec((1, LANES), whole)],
        out_specs=[pl.BlockSpec((rows, LANES), whole), pl.BlockSpec((rows, LANES), whole)],
        out_shape=[jax.ShapeDtypeStruct((rows, LANES), F32)] * 2,
        name="rope_tables",
    )(pos, invf)
    return cos.reshape(T, ROPE_HALF), sin.reshape(T, ROPE_HALF)


def _qkv_prep_kernel(p_ref, cos_ref, sin_ref, gq_ref, gk_ref, seg_ref, q_ref, k_ref, v_ref):
    cos = cos_ref[...]
    sin = sin_ref[...]
    lane = lax.broadcasted_iota(jnp.int32, cos.shape, 1)
    first_half = (lane % HEAD_DIM) < ROPE_HALF

    def norm_rope(x, g):
        x2 = x * x
        x2_hi = x2.astype(BF16)
        x2_lo = (x2 - x2_hi.astype(F32)).astype(BF16)
        both = jnp.dot(jnp.concatenate([x2_hi, x2_lo], axis=0), seg_ref[...],
                       preferred_element_type=F32)
        ssq = both[:x.shape[0]] + both[x.shape[0]:]
        xn = (x * lax.rsqrt(ssq * (1.0 / HEAD_DIM) + EPS)) * g
        partner = jnp.where(first_half,
                            pltpu.roll(xn, LANES - ROPE_HALF, 1),
                            pltpu.roll(xn, ROPE_HALF, 1))
        return xn * cos + partner * sin

    for c in range(ATT_WIDTH // LANES):
        x = p_ref[:, OFF_Q + c * LANES:OFF_Q + (c + 1) * LANES]
        q_ref[:, c * LANES:(c + 1) * LANES] = (
            norm_rope(x, gq_ref[...]) * (HEAD_DIM ** -0.5)).astype(q_ref.dtype)
    for c in range(KV_WIDTH // LANES):
        x = p_ref[:, OFF_K + c * LANES:OFF_K + (c + 1) * LANES]
        k_ref[:, c * LANES:(c + 1) * LANES] = norm_rope(x, gk_ref[...]).astype(k_ref.dtype)
    v_ref[...] = p_ref[:, OFF_V:OFF_V + KV_WIDTH].astype(v_ref.dtype)


def _qkv_prep(proj, positions, q_g, k_g, tq=256):
    T = proj.shape[0]
    width = OFF_U
    cos8, sin8 = _rope_tables(positions)
    rest = HEAD_DIM - ROPE_DIM
    cos_t = jnp.tile(jnp.concatenate([cos8, cos8, jnp.ones((T, rest), F32)], axis=1),
                     (1, LANES // HEAD_DIM))
    sin_t = jnp.tile(jnp.concatenate([-sin8, sin8, jnp.zeros((T, rest), F32)], axis=1),
                     (1, LANES // HEAD_DIM))
    head_of_lane = jnp.arange(LANES) // HEAD_DIM
    same_head = (head_of_lane[:, None] == head_of_lane[None, :]).astype(BF16)
    gq = jnp.tile(q_g, LANES // HEAD_DIM).reshape(1, LANES)
    gk = jnp.tile(k_g, LANES // HEAD_DIM).reshape(1, LANES)
    row = lambda i: (i, 0)
    const = lambda i: (0, 0)
    return pl.pallas_call(
        _qkv_prep_kernel,
        grid=(T // tq,),
        in_specs=[pl.BlockSpec((tq, width), row),
                  pl.BlockSpec((tq, LANES), row),
                  pl.BlockSpec((tq, LANES), row),
                  pl.BlockSpec((1, LANES), const),
                  pl.BlockSpec((1, LANES), const),
                  pl.BlockSpec((LANES, LANES), const)],
        out_specs=[pl.BlockSpec((tq, ATT_WIDTH), row),
                   pl.BlockSpec((tq, KV_WIDTH), row),
                   pl.BlockSpec((tq, KV_WIDTH), row)],
        out_shape=[jax.ShapeDtypeStruct((T, ATT_WIDTH), BF16),
                   jax.ShapeDtypeStruct((T, KV_WIDTH), BF16),
                   jax.ShapeDtypeStruct((T, KV_WIDTH), BF16)],
        compiler_params=_params(("parallel",)),
        name="qkv_prep",
    )(proj, cos_t, sin_t, gq, gk, same_head)


def _attn_kernel(sink_ref, q_ref, kp_ref, kc_ref, kn_ref, vp_ref, vc_ref, vn_ref, o_ref, *, nb):
    n = pl.program_id(1)
    rows = Q_PER_KV * ATT_BLOCK
    qi = lax.broadcasted_iota(jnp.int32, (rows, ATT_BLOCK), 0) % ATT_BLOCK
    kj = lax.broadcasted_iota(jnp.int32, (rows, ATT_BLOCK), 1)
    lo_prev = jnp.where(n > 0, 0, ATT_BLOCK)
    hi_next = jnp.where(n < nb - 1, 0, -ATT_BLOCK)
    cap_prev = jnp.where(kj - qi >= lo_prev, jnp.inf, NEG_INF)
    cap_next = jnp.where(kj - qi <= hi_next, jnp.inf, NEG_INF)
    cap = jnp.concatenate([cap_prev, jnp.full((rows, ATT_BLOCK), jnp.inf, F32), cap_next], axis=1)
    row_head = lax.broadcasted_iota(jnp.int32, (rows, 1), 0) // ATT_BLOCK

    for kvh in range(ATT_KV_HEADS):
        cols = slice(kvh * HEAD_DIM, (kvh + 1) * HEAD_DIM)
        k = jnp.concatenate([kp_ref[:, cols], kc_ref[:, cols], kn_ref[:, cols]], axis=0)
        v = jnp.concatenate([vp_ref[:, cols], vc_ref[:, cols], vn_ref[:, cols]], axis=0)
        q = jnp.concatenate(
            [q_ref[:, (kvh * Q_PER_KV + g) * HEAD_DIM:(kvh * Q_PER_KV + g + 1) * HEAD_DIM]
             for g in range(Q_PER_KV)], axis=0)
        sink = jnp.zeros((rows, 1), F32)
        for g in range(Q_PER_KV):
            sink = jnp.where(row_head == g, sink_ref[kvh * Q_PER_KV + g], sink)
        s = lax.dot_general(q, k, (((1,), (1,)), ((), ())), preferred_element_type=F32)
        s = jnp.minimum(s, cap)
        m = jnp.maximum(jnp.max(s, axis=-1, keepdims=True), sink)
        e = jnp.exp(s - m)
        denom = jnp.sum(e, axis=-1, keepdims=True) + jnp.exp(sink - m)
        p = (e / denom).astype(BF16)
        o = jnp.dot(p, v, preferred_element_type=F32)
        for g in range(Q_PER_KV):
            h = kvh * Q_PER_KV + g
            o_ref[:, h * HEAD_DIM:(h + 1) * HEAD_DIM] = (
                o[g * ATT_BLOCK:(g + 1) * ATT_BLOCK].astype(o_ref.dtype))


def _attention(q, k, v, sink, batch):
    T = q.shape[0]
    nb = T // batch // ATT_BLOCK
    cur = lambda b, n: (b * nb + n, 0)
    prev = lambda b, n: (b * nb + jnp.maximum(n - 1, 0), 0)
    nxt = lambda b, n: (b * nb + jnp.minimum(n + 1, nb - 1), 0)
    kv = lambda im: pl.BlockSpec((ATT_BLOCK, KV_WIDTH), im)
    return pl.pallas_call(
        functools.partial(_attn_kernel, nb=nb),
        grid=(batch, nb),
        in_specs=[pl.BlockSpec(memory_space=pltpu.SMEM),
                  pl.BlockSpec((ATT_BLOCK, ATT_WIDTH), cur),
                  kv(prev), kv(cur), kv(nxt), kv(prev), kv(cur), kv(nxt)],
        out_specs=pl.BlockSpec((ATT_BLOCK, ATT_WIDTH), cur),
        out_shape=jax.ShapeDtypeStruct((T, ATT_WIDTH), BF16),
        compiler_params=_params(("parallel", "parallel")),
        name="window_attn",
    )(sink, q, k, k, k, v, v, v)


def _sg_kernel(u_ref, s_ref, lng_ref, lnb_ref, w_ref, b_ref, o_ref):
    groups = w_ref.shape[0]

    def one_chunk(ci, carry):
        rows = pl.ds(pl.multiple_of(ci * SG_CHUNK, SG_CHUNK), SG_CHUNK)
        for gi in range(groups):
            cols = slice(gi * SG_GROUP_DIM, (gi + 1) * SG_GROUP_DIM)
            s = jax.nn.gelu(s_ref[rows, cols])
            mu = jnp.mean(s, axis=-1, keepdims=True)
            sc = s - mu
            var = jnp.mean(sc * sc, axis=-1, keepdims=True)
            sn = (sc * lax.rsqrt(var + EPS)) * lng_ref[0, gi:gi + 1, :] + lnb_ref[0, gi:gi + 1, :]
            mixed = jnp.dot(w_ref[gi], sn.astype(BF16), preferred_element_type=F32)
            mixed = mixed + b_ref[0, :, gi:gi + 1]
            o_ref[rows, cols] = (jax.nn.gelu(u_ref[rows, cols]) * mixed).astype(o_ref.dtype)
        return carry

    lax.fori_loop(0, u_ref.shape[0] // SG_CHUNK, one_chunk, 0)


def _spatial_gating(proj, ln_g, ln_b, w_s, b_s, halves=2, chunks=4):
    T = proj.shape[0]
    half_w = SG_WIDTH // halves
    gph = SG_GROUPS // halves
    u0, s0 = OFF_U // half_w, OFF_S // half_w
    tq = chunks * SG_CHUNK
    return pl.pallas_call(
        _sg_kernel,
        grid=(T // tq, halves),
        in_specs=[pl.BlockSpec((tq, half_w), lambda i, j: (i, u0 + j)),
                  pl.BlockSpec((tq, half_w), lambda i, j: (i, s0 + j)),
                  pl.BlockSpec((1, gph, SG_GROUP_DIM), lambda i, j: (j, 0, 0)),
                  pl.BlockSpec((1, gph, SG_GROUP_DIM), lambda i, j: (j, 0, 0)),
                  pl.BlockSpec((gph, SG_CHUNK, SG_CHUNK), lambda i, j: (j, 0, 0)),
                  pl.BlockSpec((1, SG_CHUNK, gph), lambda i, j: (j, 0, 0))],
        out_specs=pl.BlockSpec((tq, half_w), lambda i, j: (i, j)),
        out_shape=jax.ShapeDtypeStruct((T, SG_WIDTH), BF16),
        compiler_params=_params(("parallel", "parallel")),
        name="spatial_gating",
    )(proj, proj,
      ln_g.reshape(halves, gph, SG_GROUP_DIM), ln_b.reshape(halves, gph, SG_GROUP_DIM),
      w_s.astype(BF16),
      b_s.reshape(halves, gph, SG_CHUNK).transpose(0, 2, 1))


def _merge_kernel(att_ref, sgo_ref, ga_ref, gb_ref, wa_ref, wb_ref, o_ref):
    a = jnp.dot(att_ref[...], wa_ref[...], preferred_element_type=F32)
    b = jnp.dot(sgo_ref[...], wb_ref[...], preferred_element_type=F32)
    m = jax.nn.sigmoid(ga_ref[...]) * a + jax.nn.sigmoid(gb_ref[...]) * b
    o_ref[...] = m.astype(o_ref.dtype)


def _merge(att, sgo, proj, w_a, w_b, tm=1024, tn=512):
    T = att.shape[0]
    ga0 = OFF_G // tn
    gb0 = (OFF_G + D_MODEL) // tn
    return pl.pallas_call(
        _merge_kernel,
        grid=(T // tm, D_MODEL // tn),
        in_specs=[pl.BlockSpec((tm, ATT_WIDTH), lambda i, j: (i, 0)),
                  pl.BlockSpec((tm, SG_WIDTH), lambda i, j: (i, 0)),
                  pl.BlockSpec((tm, tn), lambda i, j: (i, ga0 + j)),
                  pl.BlockSpec((tm, tn), lambda i, j: (i, gb0 + j)),
                  pl.BlockSpec((ATT_WIDTH, tn), lambda i, j: (0, j)),
                  pl.BlockSpec((SG_WIDTH, tn), lambda i, j: (0, j))],
        out_specs=pl.BlockSpec((tm, tn), lambda i, j: (i, j)),
        out_shape=jax.ShapeDtypeStruct((T, D_MODEL), BF16),
        compiler_params=_params(("parallel", "parallel")),
        name="merge",
    )(att, sgo, proj, proj, w_a.astype(BF16), w_b.astype(BF16))


def _route(logits):
    lane = lax.broadcasted_iota(jnp.int32, logits.shape, 1)
    lane_f = lane.astype(F32)
    is_g = lane < N_GROUPS
    gl = jnp.where(is_g, logits, -jnp.inf)
    gmax = jnp.max(gl, axis=-1, keepdims=True)
    grp = jnp.min(jnp.where(gl == gmax, lane_f, float(LANES)), axis=-1, keepdims=True)
    gsum = jnp.sum(jnp.where(is_g, jnp.exp(logits - gmax), 0.0), axis=-1, keepdims=True)
    g_w = 1.0 / gsum
    e_lane = lane - N_GROUPS
    in_grp = jnp.logical_and(
        jnp.logical_and(e_lane >= 0, e_lane < N_EXPERTS),
        (e_lane // EXPERTS_PER_GROUP).astype(F32) == grp)
    el = jnp.where(in_grp, logits, -jnp.inf)
    v1 = jnp.max(el, axis=-1, keepdims=True)
    i1 = jnp.min(jnp.where(jnp.logical_and(in_grp, el == v1), lane_f, float(LANES)),
                 axis=-1, keepdims=True)
    rest = jnp.logical_and(in_grp, lane_f != i1)
    el2 = jnp.where(rest, logits, -jnp.inf)
    v2 = jnp.max(el2, axis=-1, keepdims=True)
    i2 = jnp.min(jnp.where(jnp.logical_and(rest, el2 == v2), lane_f, float(LANES)),
                 axis=-1, keepdims=True)
    e21 = jnp.exp(v2 - v1)
    w1 = g_w / (1.0 + e21)
    w2 = g_w * e21 / (1.0 + e21)
    idx = jnp.where(lane == 0, i1, i2) - float(N_GROUPS)
    wts = jnp.where(lane == 0, w1, jnp.where(lane == 1, w2, 0.0))
    return idx.astype(jnp.int32), wts


def _out_kernel(m_ref, w_ref, x_ref, g_ref, wr_ref, br_ref, h_ref, hn_ref, idx_ref, wt_ref):
    tm = m_ref.shape[0]
    h = x_ref[...] + jnp.dot(m_ref[...], w_ref[...], preferred_element_type=F32)
    h_ref[...] = h
    r = lax.rsqrt(jnp.mean(h * h, axis=-1, keepdims=True) + EPS)
    hn = ((h * r) * g_ref[...]).astype(BF16)
    bits = lax.bitcast_convert_type(hn.astype(F32), jnp.uint32)
    for c in range(HN_WORD_CHUNKS):
        hi = bits[:, c * LANES:(c + 1) * LANES]
        lo = bits[:, (c + HN_WORD_CHUNKS) * LANES:(c + HN_WORD_CHUNKS + 1) * LANES]
        hn_ref[pl.ds(c, tm, stride=HN_WORD_CHUNKS), :] = hi | (lo >> 16)
    logits = jnp.dot(hn, wr_ref[...], preferred_element_type=F32) + br_ref[...]
    idx, wts = _route(logits)
    idx_ref[...] = idx
    wt_ref[...] = wts


def _out_proj(merged, w_out, x, g2, w_router, b_router, tm=256):
    T, D = x.shape
    row = lambda i: (i, 0)
    const = lambda i: (0, 0)
    return pl.pallas_call(
        _out_kernel,
        grid=(T // tm,),
        in_specs=[pl.BlockSpec((tm, D), row),
                  pl.BlockSpec((D, D), const),
                  pl.BlockSpec((tm, D), row),
                  pl.BlockSpec((1, D), const),
                  pl.BlockSpec((D, LANES), const),
                  pl.BlockSpec((1, LANES), const)],
        out_specs=[pl.BlockSpec((tm, D), row),
                   pl.BlockSpec((tm * HN_WORD_CHUNKS, LANES), row),
                   pl.BlockSpec((tm, LANES), row),
                   pl.BlockSpec((tm, LANES), row)],
        out_shape=[jax.ShapeDtypeStruct((T, D), F32),
                   jax.ShapeDtypeStruct((T * HN_WORD_CHUNKS, LANES), jnp.uint32),
                   jax.ShapeDtypeStruct((T, LANES), jnp.int32),
                   jax.ShapeDtypeStruct((T, LANES), F32)],
        compiler_params=_params(("parallel",)),
        name="out_proj_router",
    )(merged, w_out.astype(BF16), x, g2.reshape(1, D), w_router, b_router)


def _expert_kernel(be_ref, eo_ref, ue_ref, nu_ref, *refs):
    rt_refs = refs[:ROW_SLOTS]
    (hn_hbm, wg_hbm, wu_hbm, wd_hbm, y_ref,
     xbuf, wgf, wuf, wdf, wgb, wub, wdb, sem, wsem) = refs[ROW_SLOTS:]
    b = pl.program_id(0)
    n_used = nu_ref[0]
    n_exp = nu_ref[1]
    used = b < n_used
    slot = b % ROW_SLOTS

    def weight_copies(ordinal, slot_):
        e = ue_ref[ordinal]
        return (pltpu.make_async_copy(wg_hbm.at[e], wgf.at[slot_], wsem.at[slot_]),
                pltpu.make_async_copy(wu_hbm.at[e], wuf.at[slot_], wsem.at[slot_]),
                pltpu.make_async_copy(wd_hbm.at[e], wdf.at[slot_], wsem.at[slot_]))

    def start_weights(ordinal):
        for c in weight_copies(ordinal, ordinal % 2):
            c.start(priority=1)

    def gather_rows(tok_ref, slot_):
        def issue(g, c):
            for j in range(ROWS_PER_ISSUE):
                r = g * ROWS_PER_ISSUE + j
                src = pl.multiple_of(tok_ref[0, 0, r] * HN_WORD_CHUNKS, HN_WORD_CHUNKS)
                dst = pl.multiple_of(r * HN_WORD_CHUNKS, HN_WORD_CHUNKS)
                pltpu.make_async_copy(hn_hbm.at[pl.ds(src, HN_WORD_CHUNKS)],
                                      xbuf.at[slot_, pl.ds(dst, HN_WORD_CHUNKS)],
                                      sem.at[slot_]).start()
            return c

        lax.fori_loop(0, MOE_BLOCK // ROWS_PER_ISSUE, issue, 0)

    @pl.when(b == 0)
    def _():
        start_weights(0)

        @pl.when(n_exp > 1)
        def _():
            start_weights(1)

        gather_rows(rt_refs[0], 0)
        for a in range(1, ROW_AHEAD):
            @pl.when(n_used > a)
            def _(a=a):
                gather_rows(rt_refs[a], a)

    @pl.when(b + ROW_AHEAD < n_used)
    def _():
        gather_rows(rt_refs[ROW_AHEAD], (b + ROW_AHEAD) % ROW_SLOTS)

    @pl.when(used)
    def _():
        first = jnp.logical_or(b == 0, be_ref[b] != be_ref[jnp.maximum(b - 1, 0)])

        @pl.when(first)
        def _():
            ordinal = eo_ref[b]
            ws = ordinal % 2
            for c in weight_copies(ordinal, ws):
                c.wait()
            wgb[...] = wgf[ws].astype(BF16)
            wub[...] = wuf[ws].astype(BF16)
            wdb[...] = wdf[ws].astype(BF16)

            @pl.when(ordinal + 2 < n_exp)
            def _():
                start_weights(ordinal + 2)

        pltpu.make_async_copy(hn_hbm.at[pl.ds(0, MOE_BLOCK * HN_WORD_CHUNKS)], xbuf.at[slot],
                              sem.at[slot]).wait()

        halves = ([], [])
        for c in range(HN_WORD_CHUNKS):
            w = xbuf[slot, pl.ds(c, MOE_BLOCK, stride=HN_WORD_CHUNKS), :]
            hi = lax.bitcast_convert_type(w & jnp.uint32(0xFFFF0000), F32)
            lo = lax.bitcast_convert_type(w << 16, F32)
            halves[0].append(hi.astype(BF16))
            halves[1].append(lo.astype(BF16))
        x = jnp.concatenate(halves[0] + halves[1], axis=1)
        hg = jnp.dot(x, wgb[...], preferred_element_type=F32)
        hu = jnp.dot(x, wub[...], preferred_element_type=F32)
        hdn = (jax.nn.silu(hg) * hu).astype(BF16)
        y_ref[...] = jnp.dot(hdn, wdb[...], preferred_element_type=F32)


def _experts(hn, w_gate, w_up, w_down, block_e, block_ord, used_experts, row_tok, n_used):
    D = w_gate.shape[1]
    n_rows = row_tok.shape[0]
    n_blocks = n_rows // MOE_BLOCK
    tok3 = row_tok.reshape(n_blocks, 1, MOE_BLOCK)
    hbm = pl.BlockSpec(memory_space=pl.ANY)
    tok_block = lambda im: pl.BlockSpec((1, 1, MOE_BLOCK), im, memory_space=pltpu.SMEM)
    grid_spec = pltpu.PrefetchScalarGridSpec(
        num_scalar_prefetch=4,
        grid=(n_blocks,),
        in_specs=[tok_block(lambda b, *_, a=a: (jnp.minimum(b + a, n_blocks - 1), 0, 0))
                  for a in range(ROW_SLOTS)] + [hbm, hbm, hbm, hbm],
        out_specs=pl.BlockSpec((MOE_BLOCK, D),
                               lambda b, be, eo, ue, nu: (jnp.minimum(b, nu[0] - 1), 0)),
        scratch_shapes=[pltpu.VMEM((ROW_SLOTS, MOE_BLOCK * HN_WORD_CHUNKS, LANES), jnp.uint32),
                        pltpu.VMEM((2, D, EXPERT_FF), F32),
                        pltpu.VMEM((2, D, EXPERT_FF), F32),
                        pltpu.VMEM((2, EXPERT_FF, D), F32),
                        pltpu.VMEM((D, EXPERT_FF), BF16),
                        pltpu.VMEM((D, EXPERT_FF), BF16),
                        pltpu.VMEM((EXPERT_FF, D), BF16),
                        pltpu.SemaphoreType.DMA((ROW_SLOTS,)),
                        pltpu.SemaphoreType.DMA((2,))],
    )
    return pl.pallas_call(
        _expert_kernel,
        grid_spec=grid_spec,
        out_shape=jax.ShapeDtypeStruct((n_rows, D), F32),
        compiler_params=_params(("arbitrary",)),
        name="experts",
    )(block_e, block_ord, used_experts, n_used, *([tok3] * ROW_SLOTS), hn, w_gate, w_up, w_down)


def _combine_kernel(*refs, tc):
    dest_refs = refs[:COMBINE_SLOTS]
    y_hbm, h_ref, wt_ref, o_ref, ybuf, sem = refs[COMBINE_SLOTS:]
    i = pl.program_id(0)
    slot = i % COMBINE_SLOTS

    def gather_rows(dest_ref, slot_):
        def issue(r, c):
            for k in range(TOP_K):
                src = dest_ref[0, 0, r * TOP_K + k]
                pltpu.make_async_copy(y_hbm.at[pl.ds(src, 1)],
                                      ybuf.at[slot_, k, pl.ds(r, 1)],
                                      sem.at[slot_]).start(priority=k % 2)
            return c

        lax.fori_loop(0, tc, issue, 0, unroll=4)

    @pl.when(i == 0)
    def _():
        for a in range(COMBINE_AHEAD):
            gather_rows(dest_refs[a], a)

    @pl.when(i + COMBINE_AHEAD < pl.num_programs(0))
    def _():
        gather_rows(dest_refs[COMBINE_AHEAD], (i + COMBINE_AHEAD) % COMBINE_SLOTS)

    for k in range(TOP_K):
        pltpu.make_async_copy(y_hbm.at[pl.ds(0, tc)], ybuf.at[slot, k], sem.at[slot]).wait()
    wt = wt_ref[...]
    o_ref[...] = h_ref[...] + (wt[:, 0:1] * ybuf[slot, 0] + wt[:, 1:2] * ybuf[slot, 1])


def _combine(yrows, h, wts, dest, tc=128):
    T, D = h.shape
    steps = T // tc
    row = lambda i: (i, 0)
    dest3 = dest.reshape(steps, 1, tc * TOP_K)
    dest_block = lambda im: pl.BlockSpec((1, 1, tc * TOP_K), im, memory_space=pltpu.SMEM)
    return pl.pallas_call(
        functools.partial(_combine_kernel, tc=tc),
        grid=(steps,),
        in_specs=[dest_block(lambda i, a=a: (jnp.minimum(i + a, steps - 1), 0, 0))
                  for a in range(COMBINE_SLOTS)]
                 + [pl.BlockSpec(memory_space=pl.ANY),
                    pl.BlockSpec((tc, D), row),
                    pl.BlockSpec((tc, LANES), row)],
        out_specs=pl.BlockSpec((tc, D), row),
        out_shape=jax.ShapeDtypeStruct((T, D), F32),
        scratch_shapes=[pltpu.VMEM((COMBINE_SLOTS, TOP_K, tc, D), F32),
                        pltpu.SemaphoreType.DMA((COMBINE_SLOTS,))],
        compiler_params=_params(("arbitrary",)),
        name="combine",
    )(*([dest3] * COMBINE_SLOTS), yrows, h, wts)


SUBLANES = 8
META_ROWS = 256


def _lane_cumsum(x):
    lane = lax.broadcasted_iota(jnp.int32, x.shape, 1)
    s = 1
    while s < LANES:
        x = x + jnp.where(lane >= s, pltpu.roll(x, s, 1), 0)
        s *= 2
    return x


def _dispatch_kernel(idx_ref, dest_ref, meta_ref, run_ref, prefix_ref, start_ref, *, tb):
    p = pl.program_id(0)
    i = pl.program_id(1)
    idx = idx_ref[...]
    lane = lax.broadcasted_iota(jnp.int32, idx.shape, 1)
    e1 = idx[:, 0:1]
    e2 = idx[:, 1:2]
    onehot = jnp.where(jnp.logical_or(lane == e1, lane == e2), 1.0, 0.0)

    @pl.when(jnp.logical_and(p == 0, i == 0))
    def _():
        run_ref[...] = jnp.zeros(run_ref.shape, F32)

    @pl.when(p == 0)
    def _():
        prefix_ref[i] = run_ref[...]
        run_ref[...] = run_ref[...] + jnp.sum(onehot, axis=0, keepdims=True)

    @pl.when(jnp.logical_and(p == 1, i == 0))
    def _():
        counts = run_ref[...].astype(jnp.int32)
        nblk = (counts + (MOE_BLOCK - 1)) // MOE_BLOCK
        end_blk = _lane_cumsum(nblk)
        start_ref[...] = ((end_blk - nblk) * MOE_BLOCK).astype(F32)
        has = jnp.where(counts > 0, 1, 0)
        ordinal = _lane_cumsum(has) - 1
        end_blk, has, ordinal = end_blk[0:1], has[0:1], ordinal[0:1]
        rows = lax.broadcasted_iota(jnp.int32, (META_ROWS, LANES), 0)
        lanes = lax.broadcasted_iota(jnp.int32, (META_ROWS, LANES), 1)
        is_e = lanes < N_EXPERTS
        rsum = lambda v: jnp.sum(v, axis=-1, keepdims=True)
        be = rsum(jnp.where(jnp.logical_and(is_e, end_blk <= rows), 1, 0))
        be = jnp.minimum(be, N_EXPERTS - 1)
        eo = rsum(jnp.where(lanes == be, ordinal, 0))
        ue = rsum(jnp.where(jnp.logical_and(has > 0, ordinal == rows), lanes, 0))
        n_blk = rsum(jnp.where(lanes == N_EXPERTS - 1, end_blk, 0))
        n_exp = rsum(jnp.where(is_e, has, 0))
        meta_ref[...] = jnp.where(
            lanes == 0, be, jnp.where(lanes == 1, eo, jnp.where(
                lanes == 2, ue, jnp.where(lanes == 3, n_blk, n_exp))))

    @pl.when(p == 1)
    def _():
        r = lax.broadcasted_iota(jnp.int32, (tb, tb), 0)
        c = lax.broadcasted_iota(jnp.int32, (tb, tb), 1)
        earlier = jnp.where(c < r, 1.0, 0.0).astype(BF16)
        rank = jnp.dot(earlier, onehot.astype(BF16), preferred_element_type=F32)
        rank = rank + prefix_ref[i][0:1] + start_ref[0:1]
        d1 = jnp.sum(jnp.where(lane == e1, rank, 0.0), axis=-1, keepdims=True)
        d2 = jnp.sum(jnp.where(lane == e2, rank, 0.0), axis=-1, keepdims=True)
        dest_ref[...] = jnp.where(lane == 0, d1, d2).astype(jnp.int32)


def _dispatch(idx, tb=512):
    T = idx.shape[0]
    n_rows = T * TOP_K + N_EXPERTS * MOE_BLOCK
    n_blocks = n_rows // MOE_BLOCK
    assert n_blocks <= META_ROWS
    dest2, meta = pl.pallas_call(
        functools.partial(_dispatch_kernel, tb=tb),
        grid=(2, T // tb),
        in_specs=[pl.BlockSpec((tb, LANES), lambda p, i: (i, 0))],
        out_specs=[pl.BlockSpec((tb, LANES), lambda p, i: (i * p, 0)),
                   pl.BlockSpec((META_ROWS, LANES), lambda p, i: (0, 0))],
        out_shape=[jax.ShapeDtypeStruct((T, LANES), jnp.int32),
                   jax.ShapeDtypeStruct((META_ROWS, LANES), jnp.int32)],
        scratch_shapes=[pltpu.VMEM((SUBLANES, LANES), F32),
                        pltpu.VMEM((T // tb, SUBLANES, LANES), F32),
                        pltpu.VMEM((SUBLANES, LANES), F32)],
        compiler_params=_params(("arbitrary", "arbitrary")),
        name="dispatch",
    )(idx)
    dest = dest2[:, :TOP_K].reshape(T * TOP_K)
    tok = jnp.repeat(jnp.arange(T, dtype=jnp.int32), TOP_K)
    row_tok = jnp.zeros((n_rows,), jnp.int32).at[dest].set(tok)
    block_e = meta[:n_blocks, 0]
    block_ord = meta[:n_blocks, 1]
    used_experts = meta[:N_EXPERTS, 2]
    n_used = meta[0, 3:5]
    return block_e, block_ord, used_experts, row_tok, n_used, dest


def kernel(x, positions, norm1_g, w_in, q_norm_g, k_norm_g, sink_logits, sg_ln_g, sg_ln_b, sg_w, sg_b, w_branch_att, w_branch_sg, w_out, norm2_g, w_group_router, b_group_router, w_expert_router, b_expert_router, w_gate, w_up, w_down):
    B, S, D = x.shape
    T = B * S
    h = x.reshape(T, D)
    pos = positions.reshape(T)
    for l in range(norm1_g.shape[0]):
        xn = _rmsnorm(h, norm1_g[l])
        proj = _in_proj(xn, w_in[l])
        q, k, v = _qkv_prep(proj, pos, q_norm_g[l], k_norm_g[l])
        att = _attention(q, k, v, sink_logits[l], B)
        sgo = _spatial_gating(proj, sg_ln_g[l], sg_ln_b[l], sg_w[l], sg_b[l])
        merged = _merge(att, sgo, proj, w_branch_att[l], w_branch_sg[l])
        pad = LANES - N_GROUPS - N_EXPERTS
        w_router = jnp.concatenate(
            [w_group_router[l], w_expert_router[l], jnp.zeros((D, pad), F32)], axis=1).astype(BF16)
        b_router = jnp.concatenate(
            [b_group_router[l], b_expert_router[l], jnp.zeros((pad,), F32)]).reshape(1, LANES)
        h, hn, idx, wts = _out_proj(merged, w_out[l], h, norm2_g[l], w_router, b_router)
        block_e, block_ord, used_experts, row_tok, n_used, dest = _dispatch(idx)
        yrows = _experts(hn, w_gate[l], w_up[l], w_down[l],
                         block_e, block_ord, used_experts, row_tok, n_used)
        h = _combine(yrows, h, wts, dest)
    return h.reshape(B, S, D)
```

```python
import functools

import jax
import jax.numpy as jnp
from jax import lax
from jax.experimental import pallas as pl
from jax.experimental.pallas import tpu as pltpu

F32 = jnp.float32
BF16 = jnp.bfloat16

D_MODEL = 2048
HEAD_DIM = 64
ATT_WIDTH = D_MODEL // 2
ATT_HEADS = ATT_WIDTH // HEAD_DIM
ATT_KV_HEADS = ATT_HEADS // 4
Q_PER_KV = ATT_HEADS // ATT_KV_HEADS
KV_WIDTH = ATT_KV_HEADS * HEAD_DIM
WINDOW = 128
ATT_BLOCK = 128
ROPE_DIM = HEAD_DIM // 4
ROPE_HALF = ROPE_DIM // 2
ROPE_THETA = 500000.0
SG_WIDTH = D_MODEL // 2
SG_GROUP_DIM = 128
SG_GROUPS = SG_WIDTH // SG_GROUP_DIM
SG_CHUNK = 128
OFF_Q = 0
OFF_K = OFF_Q + ATT_WIDTH
OFF_V = OFF_K + KV_WIDTH
OFF_U = OFF_V + KV_WIDTH
OFF_S = OFF_U + SG_WIDTH
OFF_G = OFF_S + SG_WIDTH
IN_COLS = OFF_G + 2 * D_MODEL
N_GROUPS = 8
EXPERTS_PER_GROUP = 8
N_EXPERTS = N_GROUPS * EXPERTS_PER_GROUP
TOP_K = 2
EXPERT_FF = D_MODEL // 4
MOE_BLOCK = 128
EPS = 1e-6
NEG_INF = -1e30

LANES = 128
HN_WORD_CHUNKS = D_MODEL // LANES // 2
ROWS_PER_ISSUE = 8
ROW_AHEAD = 4
ROW_SLOTS = ROW_AHEAD + 1
COMBINE_AHEAD = 1
COMBINE_SLOTS = COMBINE_AHEAD + 1
VMEM_LIMIT = 56 * 1024 * 1024


def _params(sem, vmem=VMEM_LIMIT):
    return pltpu.CompilerParams(dimension_semantics=sem, vmem_limit_bytes=vmem)


def _rmsnorm_kernel(x_ref, g_ref, o_ref):
    x = x_ref[...]
    r = lax.rsqrt(jnp.mean(x * x, axis=-1, keepdims=True) + EPS)
    o_ref[...] = ((x * r) * g_ref[...]).astype(o_ref.dtype)


def _rmsnorm(x, g, tm=512):
    T, D = x.shape
    return pl.pallas_call(
        _rmsnorm_kernel,
        grid=(T // tm,),
        in_specs=[pl.BlockSpec((tm, D), lambda i: (i, 0)),
                  pl.BlockSpec((1, D), lambda i: (0, 0))],
        out_specs=pl.BlockSpec((tm, D), lambda i: (i, 0)),
        out_shape=jax.ShapeDtypeStruct((T, D), BF16),
        compiler_params=_params(("parallel",)),
        name="norm1",
    )(x, g.reshape(1, D))


def _proj_kernel(x_ref, w_ref, o_ref, wbf_ref):
    @pl.when(pl.program_id(1) == 0)
    def _():
        wbf_ref[...] = w_ref[...].astype(BF16)

    o_ref[...] = jnp.dot(x_ref[...], wbf_ref[...], preferred_element_type=F32)


def _in_proj(xn, w, tm=1024, tn=1280):
    T, D = xn.shape
    N = w.shape[1]
    return pl.pallas_call(
        _proj_kernel,
        grid=(N // tn, T // tm),
        in_specs=[pl.BlockSpec((tm, D), lambda j, i: (i, 0)),
                  pl.BlockSpec((D, tn), lambda j, i: (0, j))],
        out_specs=pl.BlockSpec((tm, tn), lambda j, i: (i, j)),
        out_shape=jax.ShapeDtypeStruct((T, N), F32),
        scratch_shapes=[pltpu.VMEM((D, tn), BF16)],
        compiler_params=_params(("arbitrary", "arbitrary")),
        name="in_proj",
    )(xn, w)


def _rope_table_kernel(pos_ref, invf_ref, cos_ref, sin_ref):
    ang = pos_ref[...].astype(F32) * invf_ref[...]
    cos_ref[...] = jnp.cos(ang)
    sin_ref[...] = jnp.sin(ang)


def _rope_tables(positions):
    T = positions.shape[0]
    rows = T * ROPE_HALF // LANES
    inv = ROPE_THETA ** (-jnp.arange(0, ROPE_DIM, 2, dtype=F32) / ROPE_DIM)
    invf = jnp.tile(inv, LANES // ROPE_HALF).reshape(1, LANES)
    pos = jnp.repeat(positions, ROPE_HALF).reshape(rows, LANES)
    whole = lambda: (0, 0)
    cos, sin = pl.pallas_call(
        _rope_table_kernel,
        in_specs=[pl.BlockSpec((rows, LANES), whole), pl.BlockSpec((1, LANES), whole)],
        out_specs=[pl.BlockSpec((rows, LANES), whole), pl.BlockSpec((rows, LANES), whole)],
        out_shape=[jax.ShapeDtypeStruct((rows, LANES), F32)] * 2,
        name="rope_tables",
    )(pos, invf)
    return cos.reshape(T, ROPE_HALF), sin.reshape(T, ROPE_HALF)


def _qkv_prep_kernel(p_ref, cos_ref, sin_ref, gq_ref, gk_ref, seg_ref, q_ref, k_ref, v_ref):
    cos = cos_ref[...]
    sin = sin_ref[...]
    lane = lax.broadcasted_iota(jnp.int32, cos.shape, 1)
    first_half = (lane % HEAD_DIM) < ROPE_HALF

    def norm_rope(x, g):
        x2 = x * x
        x2_hi = x2.astype(BF16)
        x2_lo = (x2 - x2_hi.astype(F32)).astype(BF16)
        both = jnp.dot(jnp.concatenate([x2_hi, x2_lo], axis=0), seg_ref[...],
                       preferred_element_type=F32)
        ssq = both[:x.shape[0]] + both[x.shape[0]:]
        xn = (x * lax.rsqrt(ssq * (1.0 / HEAD_DIM) + EPS)) * g
        partner = jnp.where(first_half,
                            pltpu.roll(xn, LANES - ROPE_HALF, 1),
                            pltpu.roll(xn, ROPE_HALF, 1))
        return xn * cos + partner * sin

    for c in range(ATT_WIDTH // LANES):
        x = p_ref[:, OFF_Q + c * LANES:OFF_Q + (c + 1) * LANES]
        q_ref[:, c * LANES:(c + 1) * LANES] = (
            norm_rope(x, gq_ref[...]) * (HEAD_DIM ** -0.5)).astype(q_ref.dtype)
    for c in range(KV_WIDTH // LANES):
        x = p_ref[:, OFF_K + c * LANES:OFF_K + (c + 1) * LANES]
        k_ref[:, c * LANES:(c + 1) * LANES] = norm_rope(x, gk_ref[...]).astype(k_ref.dtype)
    v_ref[...] = p_ref[:, OFF_V:OFF_V + KV_WIDTH].astype(v_ref.dtype)


def _qkv_prep(proj, positions, q_g, k_g, tq=256):
    T = proj.shape[0]
    width = OFF_U
    cos8, sin8 = _rope_tables(positions)
    rest = HEAD_DIM - ROPE_DIM
    cos_t = jnp.tile(jnp.concatenate([cos8, cos8, jnp.ones((T, rest), F32)], axis=1),
                     (1, LANES // HEAD_DIM))
    sin_t = jnp.tile(jnp.concatenate([-sin8, sin8, jnp.zeros((T, rest), F32)], axis=1),
                     (1, LANES // HEAD_DIM))
    head_of_lane = jnp.arange(LANES) // HEAD_DIM
    same_head = (head_of_lane[:, None] == head_of_lane[None, :]).astype(BF16)
    gq = jnp.tile(q_g, LANES // HEAD_DIM).reshape(1, LANES)
    gk = jnp.tile(k_g, LANES // HEAD_DIM).reshape(1, LANES)
    row = lambda i: (i, 0)
    const = lambda i: (0, 0)
    return pl.pallas_call(
        _qkv_prep_kernel,
        grid=(T // tq,),
        in_specs=[pl.BlockSpec((tq, width), row),
                  pl.BlockSpec((tq, LANES), row),
                  pl.BlockSpec((tq, LANES), row),
                  pl.BlockSpec((1, LANES), const),
                  pl.BlockSpec((1, LANES), const),
                  pl.BlockSpec((LANES, LANES), const)],
        out_specs=[pl.BlockSpec((tq, ATT_WIDTH), row),
                   pl.BlockSpec((tq, KV_WIDTH), row),
                   pl.BlockSpec((tq, KV_WIDTH), row)],
        out_shape=[jax.ShapeDtypeStruct((T, ATT_WIDTH), BF16),
                   jax.ShapeDtypeStruct((T, KV_WIDTH), BF16),
                   jax.ShapeDtypeStruct((T, KV_WIDTH), BF16)],
        compiler_params=_params(("parallel",)),
        name="qkv_prep",
    )(proj, cos_t, sin_t, gq, gk, same_head)


def _attn_kernel(sink_ref, q_ref, kp_ref, kc_ref, kn_ref, vp_ref, vc_ref, vn_ref, o_ref, *, nb):
    n = pl.program_id(1)
    rows = ATT_BLOCK
    qi = lax.broadcasted_iota(jnp.int32, (rows, ATT_BLOCK), 0) % ATT_BLOCK
    kj = lax.broadcasted_iota(jnp.int32, (rows, ATT_BLOCK), 1)
    lo_prev = jnp.where(n > 0, 0, ATT_BLOCK)
    hi_next = jnp.where(n < nb - 1, 0, -ATT_BLOCK)
    cap_prev = jnp.where(kj - qi >= lo_prev, jnp.inf, NEG_INF)
    cap_next = jnp.where(kj - qi <= hi_next, jnp.inf, NEG_INF)
    cap = jnp.concatenate([cap_prev, jnp.full((rows, ATT_BLOCK), jnp.inf, F32), cap_next], axis=1)

    for kvh in range(ATT_KV_HEADS):
        cols = slice(kvh * HEAD_DIM, (kvh + 1) * HEAD_DIM)
        k = jnp.concatenate([kp_ref[:, cols], kc_ref[:, cols], kn_ref[:, cols]], axis=0)
        v = jnp.concatenate([vp_ref[:, cols], vc_ref[:, cols], vn_ref[:, cols]], axis=0)
        q = jnp.concatenate(
            [q_ref[:, (kvh * Q_PER_KV + g) * HEAD_DIM:(kvh * Q_PER_KV + g + 1) * HEAD_DIM]
             for g in range(Q_PER_KV)], axis=0)
        s = lax.dot_general(q, k, (((1,), (1,)), ((), ())), preferred_element_type=F32)
        probs = []
        for g in range(Q_PER_KV):
            sink = sink_ref[kvh * Q_PER_KV + g]
            sg = jnp.minimum(s[g * ATT_BLOCK:(g + 1) * ATT_BLOCK], cap)
            m = jnp.maximum(jnp.max(sg, axis=-1, keepdims=True), sink)
            e = jnp.exp(sg - m)
            denom = jnp.sum(e, axis=-1, keepdims=True) + jnp.exp(sink - m)
            probs.append((e / denom).astype(BF16))
        p = jnp.concatenate(probs, axis=0)
        o = jnp.dot(p, v, preferred_element_type=F32)
        for g in range(Q_PER_KV):
            h = kvh * Q_PER_KV + g
            o_ref[:, h * HEAD_DIM:(h + 1) * HEAD_DIM] = (
                o[g * ATT_BLOCK:(g + 1) * ATT_BLOCK].astype(o_ref.dtype))


def _attention(q, k, v, sink, batch):
    T = q.shape[0]
    nb = T // batch // ATT_BLOCK
    cur = lambda b, n: (b * nb + n, 0)
    prev = lambda b, n: (b * nb + jnp.maximum(n - 1, 0), 0)
    nxt = lambda b, n: (b * nb + jnp.minimum(n + 1, nb - 1), 0)
    kv = lambda im: pl.BlockSpec((ATT_BLOCK, KV_WIDTH), im)
    return pl.pallas_call(
        functools.partial(_attn_kernel, nb=nb),
        grid=(batch, nb),
        in_specs=[pl.BlockSpec(memory_space=pltpu.SMEM),
                  pl.BlockSpec((ATT_BLOCK, ATT_WIDTH), cur),
                  kv(prev), kv(cur), kv(nxt), kv(prev), kv(cur), kv(nxt)],
        out_specs=pl.BlockSpec((ATT_BLOCK, ATT_WIDTH), cur),
        out_shape=jax.ShapeDtypeStruct((T, ATT_WIDTH), BF16),
        compiler_params=_params(("parallel", "parallel")),
        name="window_attn",
    )(sink, q, k, k, k, v, v, v)


def _sg_kernel(u_ref, s_ref, lng_ref, lnb_ref, w_ref, b_ref, o_ref):
    groups = w_ref.shape[0]

    def one_chunk(ci, carry):
        rows = pl.ds(pl.multiple_of(ci * SG_CHUNK, SG_CHUNK), SG_CHUNK)
        for gi in range(groups):
            cols = slice(gi * SG_GROUP_DIM, (gi + 1) * SG_GROUP_DIM)
            s = jax.nn.gelu(s_ref[rows, cols])
            mu = jnp.mean(s, axis=-1, keepdims=True)
            sc = s - mu
            var = jnp.mean(sc * sc, axis=-1, keepdims=True)
            sn = (sc * lax.rsqrt(var + EPS)) * lng_ref[0, gi:gi + 1, :] + lnb_ref[0, gi:gi + 1, :]
            mixed = jnp.dot(w_ref[gi], sn.astype(BF16), preferred_element_type=F32)
            mixed = mixed + b_ref[0, :, gi:gi + 1]
            o_ref[rows, cols] = (jax.nn.gelu(u_ref[rows, cols]) * mixed).astype(o_ref.dtype)
        return carry

    lax.fori_loop(0, u_ref.shape[0] // SG_CHUNK, one_chunk, 0)


def _spatial_gating(proj, ln_g, ln_b, w_s, b_s, halves=2, chunks=4):
    T = proj.shape[0]
    half_w = SG_WIDTH // halves
    gph = SG_GROUPS // halves
    u0, s0 = OFF_U // half_w, OFF_S // half_w
    tq = chunks * SG_CHUNK
    return pl.pallas_call(
        _sg_kernel,
        grid=(T // tq, halves),
        in_specs=[pl.BlockSpec((tq, half_w), lambda i, j: (i, u0 + j)),
                  pl.BlockSpec((tq, half_w), lambda i, j: (i, s0 + j)),
                  pl.BlockSpec((1, gph, SG_GROUP_DIM), lambda i, j: (j, 0, 0)),
                  pl.BlockSpec((1, gph, SG_GROUP_DIM), lambda i, j: (j, 0, 0)),
                  pl.BlockSpec((gph, SG_CHUNK, SG_CHUNK), lambda i, j: (j, 0, 0)),
                  pl.BlockSpec((1, SG_CHUNK, gph), lambda i, j: (j, 0, 0))],
        out_specs=pl.BlockSpec((tq, half_w), lambda i, j: (i, j)),
        out_shape=jax.ShapeDtypeStruct((T, SG_WIDTH), BF16),
        compiler_params=_params(("parallel", "parallel")),
        name="spatial_gating",
    )(proj, proj,
      ln_g.reshape(halves, gph, SG_GROUP_DIM), ln_b.reshape(halves, gph, SG_GROUP_DIM),
      w_s.astype(BF16),
      b_s.reshape(halves, gph, SG_CHUNK).transpose(0, 2, 1))


def _merge_kernel(att_ref, sgo_ref, ga_ref, gb_ref, wa_ref, wb_ref, o_ref, wa_bf, wb_bf):
    @pl.when(pl.program_id(1) == 0)
    def _():
        wa_bf[...] = wa_ref[...].astype(BF16)
        wb_bf[...] = wb_ref[...].astype(BF16)

    a = jnp.dot(att_ref[...], wa_bf[...], preferred_element_type=F32)
    b = jnp.dot(sgo_ref[...], wb_bf[...], preferred_element_type=F32)
    m = jax.nn.sigmoid(ga_ref[...]) * a + jax.nn.sigmoid(gb_ref[...]) * b
    o_ref[...] = m.astype(o_ref.dtype)


def _merge(att, sgo, proj, w_a, w_b, tm=1024, tn=512):
    T = att.shape[0]
    ga0 = OFF_G // tn
    gb0 = (OFF_G + D_MODEL) // tn
    return pl.pallas_call(
        _merge_kernel,
        grid=(D_MODEL // tn, T // tm),
        in_specs=[pl.BlockSpec((tm, ATT_WIDTH), lambda j, i: (i, 0)),
                  pl.BlockSpec((tm, SG_WIDTH), lambda j, i: (i, 0)),
                  pl.BlockSpec((tm, tn), lambda j, i: (i, ga0 + j)),
                  pl.BlockSpec((tm, tn), lambda j, i: (i, gb0 + j)),
                  pl.BlockSpec((ATT_WIDTH, tn), lambda j, i: (0, j)),
                  pl.BlockSpec((SG_WIDTH, tn), lambda j, i: (0, j))],
        out_specs=pl.BlockSpec((tm, tn), lambda j, i: (i, j)),
        out_shape=jax.ShapeDtypeStruct((T, D_MODEL), BF16),
        scratch_shapes=[pltpu.VMEM((ATT_WIDTH, tn), BF16),
                        pltpu.VMEM((SG_WIDTH, tn), BF16)],
        compiler_params=_params(("arbitrary", "arbitrary")),
        name="merge",
    )(att, sgo, proj, proj, w_a, w_b)


def _route(logits):
    lane = lax.broadcasted_iota(jnp.int32, logits.shape, 1)
    lane_f = lane.astype(F32)
    is_g = lane < N_GROUPS
    gl = jnp.where(is_g, logits, -jnp.inf)
    gmax = jnp.max(gl, axis=-1, keepdims=True)
    grp = jnp.min(jnp.where(gl == gmax, lane_f, float(LANES)), axis=-1, keepdims=True)
    gsum = jnp.sum(jnp.where(is_g, jnp.exp(logits - gmax), 0.0), axis=-1, keepdims=True)
    g_w = 1.0 / gsum
    e_lane = lane - N_GROUPS
    in_grp = jnp.logical_and(
        jnp.logical_and(e_lane >= 0, e_lane < N_EXPERTS),
        (e_lane // EXPERTS_PER_GROUP).astype(F32) == grp)
    el = jnp.where(in_grp, logits, -jnp.inf)
    v1 = jnp.max(el, axis=-1, keepdims=True)
    i1 = jnp.min(jnp.where(jnp.logical_and(in_grp, el == v1), lane_f, float(LANES)),
                 axis=-1, keepdims=True)
    rest = jnp.logical_and(in_grp, lane_f != i1)
    el2 = jnp.where(rest, logits, -jnp.inf)
    v2 = jnp.max(el2, axis=-1, keepdims=True)
    i2 = jnp.min(jnp.where(jnp.logical_and(rest, el2 == v2), lane_f, float(LANES)),
                 axis=-1, keepdims=True)
    e21 = jnp.exp(v2 - v1)
    w1 = g_w / (1.0 + e21)
    w2 = g_w * e21 / (1.0 + e21)
    idx = jnp.where(lane == 0, i1, i2) - float(N_GROUPS)
    wts = jnp.where(lane == 0, w1, jnp.where(lane == 1, w2, 0.0))
    return idx.astype(jnp.int32), wts


def _out_kernel(m_ref, w_ref, x_ref, g_ref, wr_ref, br_ref, h_ref, hn_ref, idx_ref, wt_ref):
    tm = m_ref.shape[0]
    h = x_ref[...] + jnp.dot(m_ref[...], w_ref[...], preferred_element_type=F32)
    h_ref[...] = h
    r = lax.rsqrt(jnp.mean(h * h, axis=-1, keepdims=True) + EPS)
    hn = ((h * r) * g_ref[...]).astype(BF16)
    bits = lax.bitcast_convert_type(hn.astype(F32), jnp.uint32)
    for c in range(HN_WORD_CHUNKS):
        hi = bits[:, c * LANES:(c + 1) * LANES]
        lo = bits[:, (c + HN_WORD_CHUNKS) * LANES:(c + HN_WORD_CHUNKS + 1) * LANES]
        hn_ref[pl.ds(c, tm, stride=HN_WORD_CHUNKS), :] = hi | (lo >> 16)
    logits = jnp.dot(hn, wr_ref[...], preferred_element_type=F32) + br_ref[...]
    idx, wts = _route(logits)
    idx_ref[...] = idx
    wt_ref[...] = wts


def _out_proj(merged, w_out, x, g2, w_router, b_router, tm=256):
    T, D = x.shape
    row = lambda i: (i, 0)
    const = lambda i: (0, 0)
    return pl.pallas_call(
        _out_kernel,
        grid=(T // tm,),
        in_specs=[pl.BlockSpec((tm, D), row),
                  pl.BlockSpec((D, D), const),
                  pl.BlockSpec((tm, D), row),
                  pl.BlockSpec((1, D), const),
                  pl.BlockSpec((D, LANES), const),
                  pl.BlockSpec((1, LANES), const)],
        out_specs=[pl.BlockSpec((tm, D), row),
                   pl.BlockSpec((tm * HN_WORD_CHUNKS, LANES), row),
                   pl.BlockSpec((tm, LANES), row),
                   pl.BlockSpec((tm, LANES), row)],
        out_shape=[jax.ShapeDtypeStruct((T, D), F32),
                   jax.ShapeDtypeStruct((T * HN_WORD_CHUNKS, LANES), jnp.uint32),
                   jax.ShapeDtypeStruct((T, LANES), jnp.int32),
                   jax.ShapeDtypeStruct((T, LANES), F32)],
        compiler_params=_params(("parallel",)),
        name="out_proj_router",
    )(merged, w_out.astype(BF16), x, g2.reshape(1, D), w_router, b_router)


def _expert_kernel(be_ref, eo_ref, ue_ref, nu_ref, *refs):
    rt_refs = refs[:ROW_SLOTS]
    (hn_hbm, wg_hbm, wu_hbm, wd_hbm, y_ref,
     xbuf, wgf, wuf, wdf, wgb, wub, wdb, sem, wsem) = refs[ROW_SLOTS:]
    b = pl.program_id(0)
    n_used = nu_ref[0]
    n_exp = nu_ref[1]
    used = b < n_used
    slot = b % ROW_SLOTS

    def weight_copies(ordinal, slot_):
        e = ue_ref[ordinal]
        return (pltpu.make_async_copy(wg_hbm.at[e], wgf.at[slot_], wsem.at[slot_]),
                pltpu.make_async_copy(wu_hbm.at[e], wuf.at[slot_], wsem.at[slot_]),
                pltpu.make_async_copy(wd_hbm.at[e], wdf.at[slot_], wsem.at[slot_]))

    def start_weights(ordinal):
        for c in weight_copies(ordinal, ordinal % 2):
            c.start(priority=1)

    def gather_rows(tok_ref, slot_):
        def issue(g, c):
            for j in range(ROWS_PER_ISSUE):
                r = g * ROWS_PER_ISSUE + j
                src = pl.multiple_of(tok_ref[0, 0, r] * HN_WORD_CHUNKS, HN_WORD_CHUNKS)
                dst = pl.multiple_of(r * HN_WORD_CHUNKS, HN_WORD_CHUNKS)
                pltpu.make_async_copy(hn_hbm.at[pl.ds(src, HN_WORD_CHUNKS)],
                                      xbuf.at[slot_, pl.ds(dst, HN_WORD_CHUNKS)],
                                      sem.at[slot_]).start()
            return c

        lax.fori_loop(0, MOE_BLOCK // ROWS_PER_ISSUE, issue, 0)

    @pl.when(b == 0)
    def _():
        start_weights(0)

        @pl.when(n_exp > 1)
        def _():
            start_weights(1)

        gather_rows(rt_refs[0], 0)
        for a in range(1, ROW_AHEAD):
            @pl.when(n_used > a)
            def _(a=a):
                gather_rows(rt_refs[a], a)

    @pl.when(b + ROW_AHEAD < n_used)
    def _():
        gather_rows(rt_refs[ROW_AHEAD], (b + ROW_AHEAD) % ROW_SLOTS)

    @pl.when(used)
    def _():
        first = jnp.logical_or(b == 0, be_ref[b] != be_ref[jnp.maximum(b - 1, 0)])

        @pl.when(first)
        def _():
            ordinal = eo_ref[b]
            ws = ordinal % 2
            for c in weight_copies(ordinal, ws):
                c.wait()
            wgb[...] = wgf[ws].astype(BF16)
            wub[...] = wuf[ws].astype(BF16)
            wdb[...] = wdf[ws].astype(BF16)

            @pl.when(ordinal + 2 < n_exp)
            def _():
                start_weights(ordinal + 2)

        pltpu.make_async_copy(hn_hbm.at[pl.ds(0, MOE_BLOCK * HN_WORD_CHUNKS)], xbuf.at[slot],
                              sem.at[slot]).wait()

        halves = ([], [])
        for c in range(HN_WORD_CHUNKS):
            w = xbuf[slot, pl.ds(c, MOE_BLOCK, stride=HN_WORD_CHUNKS), :]
            hi = lax.bitcast_convert_type(w & jnp.uint32(0xFFFF0000), F32)
            lo = lax.bitcast_convert_type(w << 16, F32)
            halves[0].append(hi.astype(BF16))
            halves[1].append(lo.astype(BF16))
        x = jnp.concatenate(halves[0] + halves[1], axis=1)
        hg = jnp.dot(x, wgb[...], preferred_element_type=F32)
        hu = jnp.dot(x, wub[...], preferred_element_type=F32)
        hdn = (jax.nn.silu(hg) * hu).astype(BF16)
        y_ref[...] = jnp.dot(hdn, wdb[...], preferred_element_type=F32)

    @pl.when(jnp.logical_not(used))
    def _():
        y_ref[...] = jnp.zeros(y_ref.shape, y_ref.dtype)


def _experts(hn, w_gate, w_up, w_down, block_e, block_ord, used_experts, row_tok, n_used):
    D = w_gate.shape[1]
    n_rows = row_tok.shape[0]
    n_blocks = n_rows // MOE_BLOCK
    tok3 = row_tok.reshape(n_blocks, 1, MOE_BLOCK)
    hbm = pl.BlockSpec(memory_space=pl.ANY)
    tok_block = lambda im: pl.BlockSpec((1, 1, MOE_BLOCK), im, memory_space=pltpu.SMEM)
    grid_spec = pltpu.PrefetchScalarGridSpec(
        num_scalar_prefetch=4,
        grid=(n_blocks,),
        in_specs=[tok_block(lambda b, *_, a=a: (jnp.minimum(b + a, n_blocks - 1), 0, 0))
                  for a in range(ROW_SLOTS)] + [hbm, hbm, hbm, hbm],
        out_specs=pl.BlockSpec((MOE_BLOCK, D), lambda b, *_: (b, 0)),
        scratch_shapes=[pltpu.VMEM((ROW_SLOTS, MOE_BLOCK * HN_WORD_CHUNKS, LANES), jnp.uint32),
                        pltpu.VMEM((2, D, EXPERT_FF), F32),
                        pltpu.VMEM((2, D, EXPERT_FF), F32),
                        pltpu.VMEM((2, EXPERT_FF, D), F32),
                        pltpu.VMEM((D, EXPERT_FF), BF16),
                        pltpu.VMEM((D, EXPERT_FF), BF16),
                        pltpu.VMEM((EXPERT_FF, D), BF16),
                        pltpu.SemaphoreType.DMA((ROW_SLOTS,)),
                        pltpu.SemaphoreType.DMA((2,))],
    )
    return pl.pallas_call(
        _expert_kernel,
        grid_spec=grid_spec,
        out_shape=jax.ShapeDtypeStruct((n_rows, D), F32),
        compiler_params=_params(("arbitrary",)),
        name="experts",
    )(block_e, block_ord, used_experts, n_used, *([tok3] * ROW_SLOTS), hn, w_gate, w_up, w_down)


def _combine_kernel(*refs, tc):
    dest_refs = refs[:COMBINE_SLOTS]
    y_hbm, h_ref, wt_ref, o_ref, ybuf, sem = refs[COMBINE_SLOTS:]
    i = pl.program_id(0)
    slot = i % COMBINE_SLOTS

    def gather_rows(dest_ref, slot_):
        def issue(r, c):
            for k in range(TOP_K):
                src = dest_ref[0, 0, r * TOP_K + k]
                pltpu.make_async_copy(y_hbm.at[pl.ds(src, 1)],
                                      ybuf.at[slot_, k, pl.ds(r, 1)],
                                      sem.at[slot_]).start(priority=k % 2)
            return c

        lax.fori_loop(0, tc, issue, 0, unroll=4)

    @pl.when(i == 0)
    def _():
        for a in range(COMBINE_AHEAD):
            gather_rows(dest_refs[a], a)

    @pl.when(i + COMBINE_AHEAD < pl.num_programs(0))
    def _():
        gather_rows(dest_refs[COMBINE_AHEAD], (i + COMBINE_AHEAD) % COMBINE_SLOTS)

    for k in range(TOP_K):
        pltpu.make_async_copy(y_hbm.at[pl.ds(0, tc)], ybuf.at[slot, k], sem.at[slot]).wait()
    wt = wt_ref[...]
    o_ref[...] = h_ref[...] + (wt[:, 0:1] * ybuf[slot, 0] + wt[:, 1:2] * ybuf[slot, 1])


def _combine(yrows, h, wts, dest, tc=128):
    T, D = h.shape
    steps = T // tc
    row = lambda i: (i, 0)
    dest3 = dest.reshape(steps, 1, tc * TOP_K)
    dest_block = lambda im: pl.BlockSpec((1, 1, tc * TOP_K), im, memory_space=pltpu.SMEM)
    return pl.pallas_call(
        functools.partial(_combine_kernel, tc=tc),
        grid=(steps,),
        in_specs=[dest_block(lambda i, a=a: (jnp.minimum(i + a, steps - 1), 0, 0))
                  for a in range(COMBINE_SLOTS)]
                 + [pl.BlockSpec(memory_space=pl.ANY),
                    pl.BlockSpec((tc, D), row),
                    pl.BlockSpec((tc, LANES), row)],
        out_specs=pl.BlockSpec((tc, D), row),
        out_shape=jax.ShapeDtypeStruct((T, D), F32),
        scratch_shapes=[pltpu.VMEM((COMBINE_SLOTS, TOP_K, tc, D), F32),
                        pltpu.SemaphoreType.DMA((COMBINE_SLOTS,))],
        compiler_params=_params(("arbitrary",)),
        name="combine",
    )(*([dest3] * COMBINE_SLOTS), yrows, h, wts)


SUBLANES = 8
META_ROWS = 256

def _lane_cumsum(x):
    lane = lax.broadcasted_iota(jnp.int32, x.shape, 1)
    s = 1
    while s < LANES:
        x = x + jnp.where(lane >= s, pltpu.roll(x, s, 1), 0)
        s *= 2
    return x


def _dispatch_kernel(idx_ref, dest_ref, meta_ref, run_ref, prefix_ref, start_ref, *, tb):
    p = pl.program_id(0)
    i = pl.program_id(1)
    idx = idx_ref[...]
    lane = lax.broadcasted_iota(jnp.int32, idx.shape, 1)
    e1 = idx[:, 0:1]
    e2 = idx[:, 1:2]
    onehot = jnp.where(jnp.logical_or(lane == e1, lane == e2), 1.0, 0.0)

    @pl.when(jnp.logical_and(p == 0, i == 0))
    def _():
        run_ref[...] = jnp.zeros(run_ref.shape, F32)

    @pl.when(p == 0)
    def _():
        prefix_ref[i] = run_ref[...]
        run_ref[...] = run_ref[...] + jnp.sum(onehot, axis=0, keepdims=True)

    @pl.when(jnp.logical_and(p == 1, i == 0))
    def _():
        counts = run_ref[...].astype(jnp.int32)
        nblk = (counts + (MOE_BLOCK - 1)) // MOE_BLOCK
        end_blk = _lane_cumsum(nblk)
        start_ref[...] = ((end_blk - nblk) * MOE_BLOCK).astype(F32)
        has = jnp.where(counts > 0, 1, 0)
        ordinal = _lane_cumsum(has) - 1
        end_blk, has, ordinal = end_blk[0:1], has[0:1], ordinal[0:1]
        rows = lax.broadcasted_iota(jnp.int32, (META_ROWS, LANES), 0)
        lanes = lax.broadcasted_iota(jnp.int32, (META_ROWS, LANES), 1)
        is_e = lanes < N_EXPERTS
        rsum = lambda v: jnp.sum(v, axis=-1, keepdims=True)
        be = rsum(jnp.where(jnp.logical_and(is_e, end_blk <= rows), 1, 0))
        be = jnp.minimum(be, N_EXPERTS - 1)
        eo = rsum(jnp.where(lanes == be, ordinal, 0))
        ue = rsum(jnp.where(jnp.logical_and(has > 0, ordinal == rows), lanes, 0))
        n_blk = rsum(jnp.where(lanes == N_EXPERTS - 1, end_blk, 0))
        n_exp = rsum(jnp.where(is_e, has, 0))
        meta_ref[...] = jnp.where(
            lanes == 0, be, jnp.where(lanes == 1, eo, jnp.where(
                lanes == 2, ue, jnp.where(lanes == 3, n_blk, n_exp))))

    @pl.when(p == 1)
    def _():
        r = lax.broadcasted_iota(jnp.int32, (tb, tb), 0)
        c = lax.broadcasted_iota(jnp.int32, (tb, tb), 1)
        earlier = jnp.where(c < r, 1.0, 0.0).astype(BF16)
        rank = jnp.dot(earlier, onehot.astype(BF16), preferred_element_type=F32)
        rank = rank + prefix_ref[i][0:1] + start_ref[0:1]
        d1 = jnp.sum(jnp.where(lane == e1, rank, 0.0), axis=-1, keepdims=True)
        d2 = jnp.sum(jnp.where(lane == e2, rank, 0.0), axis=-1, keepdims=True)
        dest_ref[...] = jnp.where(lane == 0, d1, d2).astype(jnp.int32)


def _dispatch(idx, tb=512):
    T = idx.shape[0]
    n_rows = T * TOP_K + N_EXPERTS * MOE_BLOCK
    n_blocks = n_rows // MOE_BLOCK
    assert n_blocks <= META_ROWS
    dest2, meta = pl.pallas_call(
        functools.partial(_dispatch_kernel, tb=tb),
        grid=(2, T // tb),
        in_specs=[pl.BlockSpec((tb, LANES), lambda p, i: (i, 0))],
        out_specs=[pl.BlockSpec((tb, LANES), lambda p, i: (i * p, 0)),
                   pl.BlockSpec((META_ROWS, LANES), lambda p, i: (0, 0))],
        out_shape=[jax.ShapeDtypeStruct((T, LANES), jnp.int32),
                   jax.ShapeDtypeStruct((META_ROWS, LANES), jnp.int32)],
        scratch_shapes=[pltpu.VMEM((SUBLANES, LANES), F32),
                        pltpu.VMEM((T // tb, SUBLANES, LANES), F32),
                        pltpu.VMEM((SUBLANES, LANES), F32)],
        compiler_params=_params(("arbitrary", "arbitrary")),
        name="dispatch",
    )(idx)
    dest = dest2[:, :TOP_K].reshape(T * TOP_K)
    tok = jnp.repeat(jnp.arange(T, dtype=jnp.int32), TOP_K)
    row_tok = jnp.zeros((n_rows,), jnp.int32).at[dest].set(tok)
    block_e = meta[:n_blocks, 0]
    block_ord = meta[:n_blocks, 1]
    used_experts = meta[:N_EXPERTS, 2]
    n_used = meta[0, 3:5]
    return block_e, block_ord, used_experts, row_tok, n_used, dest


def kernel(x, positions, norm1_g, w_in, q_norm_g, k_norm_g, sink_logits, sg_ln_g, sg_ln_b, sg_w, sg_b, w_branch_att, w_branch_sg, w_out, norm2_g, w_group_router, b_group_router, w_expert_router, b_expert_router, w_gate, w_up, w_down):
    B, S, D = x.shape
    T = B * S
    h = x.reshape(T, D)
    pos = positions.reshape(T)
    for l in range(norm1_g.shape[0]):
        xn = _rmsnorm(h, norm1_g[l])
        proj = _in_proj(xn, w_in[l])
        q, k, v = _qkv_prep(proj, pos, q_norm_g[l], k_norm_g[l])
        att = _attention(q, k, v, sink_logits[l], B)
        sgo = _spatial_gating(proj, sg_ln_g[l], sg_ln_b[l], sg_w[l], sg_b[l])
        merged = _merge(att, sgo, proj, w_branch_att[l], w_branch_sg[l])
        pad = LANES - N_GROUPS - N_EXPERTS
        w_router = jnp.concatenate(
            [w_group_router[l], w_expert_router[l], jnp.zeros((D, pad), F32)], axis=1).astype(BF16)
        b_router = jnp.concatenate(
            [b_group_router[l], b_expert_router[l], jnp.zeros((pad,), F32)]).reshape(1, LANES)
        h, hn, idx, wts = _out_proj(merged, w_out[l], h, norm2_g[l], w_router, b_router)
        block_e, block_ord, used_experts, row_tok, n_used, dest = _dispatch(idx)
        yrows = _experts(hn, w_gate[l], w_up[l], w_down[l],
                         block_e, block_ord, used_experts, row_tok, n_used)
        h = _combine(yrows, h, wts, dest)
    return h.reshape(B, S, D)
```

```python
import functools

import jax
import jax.numpy as jnp
from jax import lax
from jax.experimental import pallas as pl
from jax.experimental.pallas import tpu as pltpu

F32 = jnp.float32
BF16 = jnp.bfloat16

D_MODEL = 2048
HEAD_DIM = 64
ATT_WIDTH = D_MODEL // 2
ATT_HEADS = ATT_WIDTH // HEAD_DIM
ATT_KV_HEADS = ATT_HEADS // 4
Q_PER_KV = ATT_HEADS // ATT_KV_HEADS
KV_WIDTH = ATT_KV_HEADS * HEAD_DIM
WINDOW = 128
ATT_BLOCK = 128
ROPE_DIM = HEAD_DIM // 4
ROPE_HALF = ROPE_DIM // 2
ROPE_THETA = 500000.0
SG_WIDTH = D_MODEL // 2
SG_GROUP_DIM = 128
SG_GROUPS = SG_WIDTH // SG_GROUP_DIM
SG_CHUNK = 128
OFF_Q = 0
OFF_K = OFF_Q + ATT_WIDTH
OFF_V = OFF_K + KV_WIDTH
OFF_U = OFF_V + KV_WIDTH
OFF_S = OFF_U + SG_WIDTH
OFF_G = OFF_S + SG_WIDTH
IN_COLS = OFF_G + 2 * D_MODEL
N_GROUPS = 8
EXPERTS_PER_GROUP = 8
N_EXPERTS = N_GROUPS * EXPERTS_PER_GROUP
TOP_K = 2
EXPERT_FF = D_MODEL // 4
MOE_BLOCK = 128
EPS = 1e-6
NEG_INF = -1e30

LANES = 128
HN_WORD_CHUNKS = D_MODEL // LANES // 2
ROWS_PER_ISSUE = 8
ROW_AHEAD = 4
ROW_SLOTS = ROW_AHEAD + 1
MERGE_ROW_PARTS = 4
OUT_ROW_PARTS = 4
WEIGHT_SLOTS = 3
COMBINE_AHEAD = 1
COMBINE_SLOTS = COMBINE_AHEAD + 1
VMEM_LIMIT = 56 * 1024 * 1024


def _params(sem, vmem=VMEM_LIMIT):
    return pltpu.CompilerParams(dimension_semantics=sem, vmem_limit_bytes=vmem)


def _rmsnorm_kernel(x_ref, g_ref, o_ref):
    x = x_ref[...]
    r = lax.rsqrt(jnp.mean(x * x, axis=-1, keepdims=True) + EPS)
    o_ref[...] = ((x * r) * g_ref[...]).astype(o_ref.dtype)


def _rmsnorm(x, g, tm=512):
    T, D = x.shape
    return pl.pallas_call(
        _rmsnorm_kernel,
        grid=(T // tm,),
        in_specs=[pl.BlockSpec((tm, D), lambda i: (i, 0)),
                  pl.BlockSpec((1, D), lambda i: (0, 0))],
        out_specs=pl.BlockSpec((tm, D), lambda i: (i, 0)),
        out_shape=jax.ShapeDtypeStruct((T, D), BF16),
        compiler_params=_params(("parallel",)),
        name="norm1",
    )(x, g.reshape(1, D))


def _proj_kernel(x_ref, w_ref, o_ref, wbf_ref):
    @pl.when(pl.program_id(1) == 0)
    def _():
        wbf_ref[...] = w_ref[...].astype(BF16)

    o_ref[...] = jnp.dot(x_ref[...], wbf_ref[...], preferred_element_type=F32)


def _in_proj(xn, w, tm=1024, tn=1280):
    T, D = xn.shape
    N = w.shape[1]
    return pl.pallas_call(
        _proj_kernel,
        grid=(N // tn, T // tm),
        in_specs=[pl.BlockSpec((tm, D), lambda j, i: (i, 0)),
                  pl.BlockSpec((D, tn), lambda j, i: (0, j))],
        out_specs=pl.BlockSpec((tm, tn), lambda j, i: (i, j)),
        out_shape=jax.ShapeDtypeStruct((T, N), F32),
        scratch_shapes=[pltpu.VMEM((D, tn), BF16)],
        compiler_params=_params(("arbitrary", "arbitrary")),
        name="in_proj",
    )(xn, w)


def _rope_table_kernel(pos_ref, invf_ref, cos_ref, sin_ref):
    ang = pos_ref[...].astype(F32) * invf_ref[...]
    cos_ref[...] = jnp.cos(ang)
    sin_ref[...] = jnp.sin(ang)


def _rope_tables(positions):
    T = positions.shape[0]
    rows = T * ROPE_HALF // LANES
    inv = ROPE_THETA ** (-jnp.arange(0, ROPE_DIM, 2, dtype=F32) / ROPE_DIM)
    invf = jnp.tile(inv, LANES // ROPE_HALF).reshape(1, LANES)
    pos = jnp.repeat(positions, ROPE_HALF).reshape(rows, LANES)
    whole = lambda: (0, 0)
    cos, sin = pl.pallas_call(
        _rope_table_kernel,
        in_specs=[pl.BlockSpec((rows, LANES), whole), pl.BlockSpec((1, LANES), whole)],
        out_specs=[pl.BlockSpec((rows, LANES), whole), pl.BlockSpec((rows, LANES), whole)],
        out_shape=[jax.ShapeDtypeStruct((rows, LANES), F32)] * 2,
        name="rope_tables",
    )(pos, invf)
    return cos.reshape(T, ROPE_HALF), sin.reshape(T, ROPE_HALF)


def _qkv_prep_kernel(p_ref, cos_ref, sin_ref, gq_ref, gk_ref, seg_ref, q_ref, k_ref, v_ref):
    cos = cos_ref[...]
    sin = sin_ref[...]
    lane = lax.broadcasted_iota(jnp.int32, cos.shape, 1)
    first_half = (lane % HEAD_DIM) < ROPE_HALF

    def norm_rope(x, g):
        x2 = x * x
        x2_hi = x2.astype(BF16)
        x2_lo = (x2 - x2_hi.astype(F32)).astype(BF16)
        both = jnp.dot(jnp.concatenate([x2_hi, x2_lo], axis=0), seg_ref[...],
                       preferred_element_type=F32)
        ssq = both[:x.shape[0]] + both[x.shape[0]:]
        xn = (x * lax.rsqrt(ssq * (1.0 / HEAD_DIM) + EPS)) * g
        partner = jnp.where(first_half,
                            pltpu.roll(xn, LANES - ROPE_HALF, 1),
                            pltpu.roll(xn, ROPE_HALF, 1))
        return xn * cos + partner * sin

    for c in range(ATT_WIDTH // LANES):
        x = p_ref[:, OFF_Q + c * LANES:OFF_Q + (c + 1) * LANES]
        q_ref[:, c * LANES:(c + 1) * LANES] = (
            norm_rope(x, gq_ref[...]) * (HEAD_DIM ** -0.5)).astype(q_ref.dtype)
    for c in range(KV_WIDTH // LANES):
        x = p_ref[:, OFF_K + c * LANES:OFF_K + (c + 1) * LANES]
        k_ref[:, c * LANES:(c + 1) * LANES] = norm_rope(x, gk_ref[...]).astype(k_ref.dtype)
    v_ref[...] = p_ref[:, OFF_V:OFF_V + KV_WIDTH].astype(v_ref.dtype)


def _qkv_prep(proj, positions, q_g, k_g, tq=256):
    T = proj.shape[0]
    width = OFF_U
    cos8, sin8 = _rope_tables(positions)
    rest = HEAD_DIM - ROPE_DIM
    cos_t = jnp.tile(jnp.concatenate([cos8, cos8, jnp.ones((T, rest), F32)], axis=1),
                     (1, LANES // HEAD_DIM))
    sin_t = jnp.tile(jnp.concatenate([-sin8, sin8, jnp.zeros((T, rest), F32)], axis=1),
                     (1, LANES // HEAD_DIM))
    head_of_lane = jnp.arange(LANES) // HEAD_DIM
    same_head = (head_of_lane[:, None] == head_of_lane[None, :]).astype(BF16)
    gq = jnp.tile(q_g, LANES // HEAD_DIM).reshape(1, LANES)
    gk = jnp.tile(k_g, LANES // HEAD_DIM).reshape(1, LANES)
    row = lambda i: (i, 0)
    const = lambda i: (0, 0)
    return pl.pallas_call(
        _qkv_prep_kernel,
        grid=(T // tq,),
        in_specs=[pl.BlockSpec((tq, width), row),
                  pl.BlockSpec((tq, LANES), row),
                  pl.BlockSpec((tq, LANES), row),
                  pl.BlockSpec((1, LANES), const),
                  pl.BlockSpec((1, LANES), const),
                  pl.BlockSpec((LANES, LANES), const)],
        out_specs=[pl.BlockSpec((tq, ATT_WIDTH), row),
                   pl.BlockSpec((tq, KV_WIDTH), row),
                   pl.BlockSpec((tq, KV_WIDTH), row)],
        out_shape=[jax.ShapeDtypeStruct((T, ATT_WIDTH), BF16),
                   jax.ShapeDtypeStruct((T, KV_WIDTH), BF16),
                   jax.ShapeDtypeStruct((T, KV_WIDTH), BF16)],
        compiler_params=_params(("parallel",)),
        name="qkv_prep",
    )(proj, cos_t, sin_t, gq, gk, same_head)


def _attn_kernel(sink_ref, q_ref, kp_ref, kc_ref, kn_ref, vp_ref, vc_ref, vn_ref, o_ref, *, nb):
    n = pl.program_id(1)
    rows = ATT_BLOCK
    qi = lax.broadcasted_iota(jnp.int32, (rows, ATT_BLOCK), 0) % ATT_BLOCK
    kj = lax.broadcasted_iota(jnp.int32, (rows, ATT_BLOCK), 1)
    lo_prev = jnp.where(n > 0, 0, ATT_BLOCK)
    hi_next = jnp.where(n < nb - 1, 0, -ATT_BLOCK)
    cap_prev = jnp.where(kj - qi >= lo_prev, jnp.inf, NEG_INF)
    cap_next = jnp.where(kj - qi <= hi_next, jnp.inf, NEG_INF)
    cap = jnp.concatenate([cap_prev, jnp.full((rows, ATT_BLOCK), jnp.inf, F32), cap_next], axis=1)

    kvhs = range(ATT_KV_HEADS)
    scores = []
    for kvh in kvhs:
        cols = slice(kvh * HEAD_DIM, (kvh + 1) * HEAD_DIM)
        k = jnp.concatenate([kp_ref[:, cols], kc_ref[:, cols], kn_ref[:, cols]], axis=0)
        q = jnp.concatenate(
            [q_ref[:, (kvh * Q_PER_KV + g) * HEAD_DIM:(kvh * Q_PER_KV + g + 1) * HEAD_DIM]
             for g in range(Q_PER_KV)], axis=0)
        scores.append(lax.dot_general(q, k, (((1,), (1,)), ((), ())),
                                      preferred_element_type=F32))
    probs = []
    for kvh in kvhs:
        strips = []
        for g in range(Q_PER_KV):
            sink = sink_ref[kvh * Q_PER_KV + g]
            sg = jnp.minimum(scores[kvh][g * ATT_BLOCK:(g + 1) * ATT_BLOCK], cap)
            m = jnp.maximum(jnp.max(sg, axis=-1, keepdims=True), sink)
            e = jnp.exp(sg - m)
            denom = jnp.sum(e, axis=-1, keepdims=True) + jnp.exp(sink - m)
            strips.append((e / denom).astype(BF16))
        probs.append(jnp.concatenate(strips, axis=0))
    for kvh in kvhs:
        cols = slice(kvh * HEAD_DIM, (kvh + 1) * HEAD_DIM)
        v = jnp.concatenate([vp_ref[:, cols], vc_ref[:, cols], vn_ref[:, cols]], axis=0)
        o = jnp.dot(probs[kvh], v, preferred_element_type=F32)
        for g in range(Q_PER_KV):
            h = kvh * Q_PER_KV + g
            o_ref[:, h * HEAD_DIM:(h + 1) * HEAD_DIM] = (
                o[g * ATT_BLOCK:(g + 1) * ATT_BLOCK].astype(o_ref.dtype))


def _attention(q, k, v, sink, batch):
    T = q.shape[0]
    nb = T // batch // ATT_BLOCK
    cur = lambda b, n: (b * nb + n, 0)
    prev = lambda b, n: (b * nb + jnp.maximum(n - 1, 0), 0)
    nxt = lambda b, n: (b * nb + jnp.minimum(n + 1, nb - 1), 0)
    kv = lambda im: pl.BlockSpec((ATT_BLOCK, KV_WIDTH), im)
    return pl.pallas_call(
        functools.partial(_attn_kernel, nb=nb),
        grid=(batch, nb),
        in_specs=[pl.BlockSpec(memory_space=pltpu.SMEM),
                  pl.BlockSpec((ATT_BLOCK, ATT_WIDTH), cur),
                  kv(prev), kv(cur), kv(nxt), kv(prev), kv(cur), kv(nxt)],
        out_specs=pl.BlockSpec((ATT_BLOCK, ATT_WIDTH), cur),
        out_shape=jax.ShapeDtypeStruct((T, ATT_WIDTH), BF16),
        compiler_params=_params(("parallel", "parallel")),
        name="window_attn",
    )(sink, q, k, k, k, v, v, v)


def _sg_kernel(u_ref, s_ref, lng_ref, lnb_ref, w_ref, b_ref, o_ref):
    groups = w_ref.shape[0]

    def one_chunk(ci, carry):
        rows = pl.ds(pl.multiple_of(ci * SG_CHUNK, SG_CHUNK), SG_CHUNK)
        gs = range(groups)
        cols = [slice(gi * SG_GROUP_DIM, (gi + 1) * SG_GROUP_DIM) for gi in gs]
        s = [jax.nn.gelu(s_ref[rows, cols[gi]]) for gi in gs]
        sc = [s[gi] - jnp.mean(s[gi], axis=-1, keepdims=True) for gi in gs]
        var = [jnp.mean(sc[gi] * sc[gi], axis=-1, keepdims=True) for gi in gs]
        sn = [(sc[gi] * lax.rsqrt(var[gi] + EPS)) * lng_ref[0, gi:gi + 1, :]
              + lnb_ref[0, gi:gi + 1, :] for gi in gs]
        mixed = [jnp.dot(w_ref[gi], sn[gi].astype(BF16), preferred_element_type=F32)
                 + b_ref[0, :, gi:gi + 1] for gi in gs]
        for gi in gs:
            o_ref[rows, cols[gi]] = (jax.nn.gelu(u_ref[rows, cols[gi]]) * mixed[gi]).astype(o_ref.dtype)
        return carry

    lax.fori_loop(0, u_ref.shape[0] // SG_CHUNK, one_chunk, 0)


def _spatial_gating(proj, ln_g, ln_b, w_s, b_s, halves=2, chunks=4):
    T = proj.shape[0]
    half_w = SG_WIDTH // halves
    gph = SG_GROUPS // halves
    u0, s0 = OFF_U // half_w, OFF_S // half_w
    tq = chunks * SG_CHUNK
    return pl.pallas_call(
        _sg_kernel,
        grid=(T // tq, halves),
        in_specs=[pl.BlockSpec((tq, half_w), lambda i, j: (i, u0 + j)),
                  pl.BlockSpec((tq, half_w), lambda i, j: (i, s0 + j)),
                  pl.BlockSpec((1, gph, SG_GROUP_DIM), lambda i, j: (j, 0, 0)),
                  pl.BlockSpec((1, gph, SG_GROUP_DIM), lambda i, j: (j, 0, 0)),
                  pl.BlockSpec((gph, SG_CHUNK, SG_CHUNK), lambda i, j: (j, 0, 0)),
                  pl.BlockSpec((1, SG_CHUNK, gph), lambda i, j: (j, 0, 0))],
        out_specs=pl.BlockSpec((tq, half_w), lambda i, j: (i, j)),
        out_shape=jax.ShapeDtypeStruct((T, SG_WIDTH), BF16),
        compiler_params=_params(("parallel", "parallel")),
        name="spatial_gating",
    )(proj, proj,
      ln_g.reshape(halves, gph, SG_GROUP_DIM), ln_b.reshape(halves, gph, SG_GROUP_DIM),
      w_s.astype(BF16),
      b_s.reshape(halves, gph, SG_CHUNK).transpose(0, 2, 1))


def _merge_kernel(att_ref, sgo_ref, ga_ref, gb_ref, wa_ref, wb_ref, o_ref, wa_bf, wb_bf):
    @pl.when(pl.program_id(1) == 0)
    def _():
        wa_bf[...] = wa_ref[...].astype(BF16)
        wb_bf[...] = wb_ref[...].astype(BF16)

    part_rows = att_ref.shape[0] // MERGE_ROW_PARTS
    parts = [slice(p * part_rows, (p + 1) * part_rows) for p in range(MERGE_ROW_PARTS)]
    a = [jnp.dot(att_ref[r, :], wa_bf[...], preferred_element_type=F32) for r in parts]
    b = [jnp.dot(sgo_ref[r, :], wb_bf[...], preferred_element_type=F32) for r in parts]
    for p, r in enumerate(parts):
        m = jax.nn.sigmoid(ga_ref[r, :]) * a[p] + jax.nn.sigmoid(gb_ref[r, :]) * b[p]
        o_ref[r, :] = m.astype(o_ref.dtype)


def _merge(att, sgo, proj, w_a, w_b, tm=1024, tn=512):
    T = att.shape[0]
    ga0 = OFF_G // tn
    gb0 = (OFF_G + D_MODEL) // tn
    return pl.pallas_call(
        _merge_kernel,
        grid=(D_MODEL // tn, T // tm),
        in_specs=[pl.BlockSpec((tm, ATT_WIDTH), lambda j, i: (i, 0)),
                  pl.BlockSpec((tm, SG_WIDTH), lambda j, i: (i, 0)),
                  pl.BlockSpec((tm, tn), lambda j, i: (i, ga0 + j)),
                  pl.BlockSpec((tm, tn), lambda j, i: (i, gb0 + j)),
                  pl.BlockSpec((ATT_WIDTH, tn), lambda j, i: (0, j)),
                  pl.BlockSpec((SG_WIDTH, tn), lambda j, i: (0, j))],
        out_specs=pl.BlockSpec((tm, tn), lambda j, i: (i, j)),
        out_shape=jax.ShapeDtypeStruct((T, D_MODEL), BF16),
        scratch_shapes=[pltpu.VMEM((ATT_WIDTH, tn), BF16),
                        pltpu.VMEM((SG_WIDTH, tn), BF16)],
        compiler_params=_params(("arbitrary", "arbitrary")),
        name="merge",
    )(att, sgo, proj, proj, w_a, w_b)


def _route(logits):
    lane = lax.broadcasted_iota(jnp.int32, logits.shape, 1)
    lane_f = lane.astype(F32)
    is_g = lane < N_GROUPS
    gl = jnp.where(is_g, logits, -jnp.inf)
    gmax = jnp.max(gl, axis=-1, keepdims=True)
    grp = jnp.min(jnp.where(gl == gmax, lane_f, float(LANES)), axis=-1, keepdims=True)
    gsum = jnp.sum(jnp.where(is_g, jnp.exp(logits - gmax), 0.0), axis=-1, keepdims=True)
    g_w = 1.0 / gsum
    e_lane = lane - N_GROUPS
    in_grp = jnp.logical_and(
        jnp.logical_and(e_lane >= 0, e_lane < N_EXPERTS),
        (e_lane // EXPERTS_PER_GROUP).astype(F32) == grp)
    el = jnp.where(in_grp, logits, -jnp.inf)
    v1 = jnp.max(el, axis=-1, keepdims=True)
    i1 = jnp.min(jnp.where(jnp.logical_and(in_grp, el == v1), lane_f, float(LANES)),
                 axis=-1, keepdims=True)
    rest = jnp.logical_and(in_grp, lane_f != i1)
    el2 = jnp.where(rest, logits, -jnp.inf)
    v2 = jnp.max(el2, axis=-1, keepdims=True)
    i2 = jnp.min(jnp.where(jnp.logical_and(rest, el2 == v2), lane_f, float(LANES)),
                 axis=-1, keepdims=True)
    e21 = jnp.exp(v2 - v1)
    w1 = g_w / (1.0 + e21)
    w2 = g_w * e21 / (1.0 + e21)
    idx = jnp.where(lane == 0, i1, i2) - float(N_GROUPS)
    wts = jnp.where(lane == 0, w1, jnp.where(lane == 1, w2, 0.0))
    return idx.astype(jnp.int32), wts


def _out_kernel(m_ref, w_ref, x_ref, g_ref, wr_ref, br_ref, h_ref, hn_ref, idx_ref, wt_ref):
    part_rows = m_ref.shape[0] // OUT_ROW_PARTS
    parts = [slice(p * part_rows, (p + 1) * part_rows) for p in range(OUT_ROW_PARTS)]
    hs = [x_ref[r, :] + jnp.dot(m_ref[r, :], w_ref[...], preferred_element_type=F32)
          for r in parts]
    for p, r in enumerate(parts):
        h = hs[p]
        h_ref[r, :] = h
        rs = lax.rsqrt(jnp.mean(h * h, axis=-1, keepdims=True) + EPS)
        hn = ((h * rs) * g_ref[...]).astype(BF16)
        bits = lax.bitcast_convert_type(hn.astype(F32), jnp.uint32)
        for c in range(HN_WORD_CHUNKS):
            hi = bits[:, c * LANES:(c + 1) * LANES]
            lo = bits[:, (c + HN_WORD_CHUNKS) * LANES:(c + HN_WORD_CHUNKS + 1) * LANES]
            hn_ref[pl.ds(p * part_rows * HN_WORD_CHUNKS + c, part_rows,
                         stride=HN_WORD_CHUNKS), :] = hi | (lo >> 16)
        logits = jnp.dot(hn, wr_ref[...], preferred_element_type=F32) + br_ref[...]
        idx, wts = _route(logits)
        idx_ref[r, :] = idx
        wt_ref[r, :] = wts


def _out_proj(merged, w_out, x, g2, w_router, b_router, tm=512):
    T, D = x.shape
    row = lambda i: (i, 0)
    const = lambda i: (0, 0)
    return pl.pallas_call(
        _out_kernel,
        grid=(T // tm,),
        in_specs=[pl.BlockSpec((tm, D), row),
                  pl.BlockSpec((D, D), const),
                  pl.BlockSpec((tm, D), row),
                  pl.BlockSpec((1, D), const),
                  pl.BlockSpec((D, LANES), const),
                  pl.BlockSpec((1, LANES), const)],
        out_specs=[pl.BlockSpec((tm, D), row),
                   pl.BlockSpec((tm * HN_WORD_CHUNKS, LANES), row),
                   pl.BlockSpec((tm, LANES), row),
                   pl.BlockSpec((tm, LANES), row)],
        out_shape=[jax.ShapeDtypeStruct((T, D), F32),
                   jax.ShapeDtypeStruct((T * HN_WORD_CHUNKS, LANES), jnp.uint32),
                   jax.ShapeDtypeStruct((T, LANES), jnp.int32),
                   jax.ShapeDtypeStruct((T, LANES), F32)],
        compiler_params=_params(("parallel",)),
        name="out_proj_router",
    )(merged, w_out.astype(BF16), x, g2.reshape(1, D), w_router, b_router)


def _expert_kernel(be_ref, eo_ref, ue_ref, nu_ref, *refs):
    rt_refs = refs[:ROW_SLOTS]
    (hn_hbm, wg_hbm, wu_hbm, wd_hbm, y_ref,
     xbuf, wgf, wuf, wdf, sem, wsem) = refs[ROW_SLOTS:]
    b = pl.program_id(0)
    n_used = nu_ref[0]
    n_exp = nu_ref[1]
    used = b < n_used
    slot = b % ROW_SLOTS

    def weight_copies(ordinal, slot_):
        e = ue_ref[ordinal]
        return (pltpu.make_async_copy(wg_hbm.at[e], wgf.at[slot_], wsem.at[slot_]),
                pltpu.make_async_copy(wu_hbm.at[e], wuf.at[slot_], wsem.at[slot_]),
                pltpu.make_async_copy(wd_hbm.at[e], wdf.at[slot_], wsem.at[slot_]))

    def start_weights(ordinal):
        for c in weight_copies(ordinal, ordinal % WEIGHT_SLOTS):
            c.start(priority=1)

    def gather_rows(tok_ref, slot_):
        def issue(g, c):
            for j in range(ROWS_PER_ISSUE):
                r = g * ROWS_PER_ISSUE + j
                src = pl.multiple_of(tok_ref[0, 0, r] * HN_WORD_CHUNKS, HN_WORD_CHUNKS)
                dst = pl.multiple_of(r * HN_WORD_CHUNKS, HN_WORD_CHUNKS)
                pltpu.make_async_copy(hn_hbm.at[pl.ds(src, HN_WORD_CHUNKS)],
                                      xbuf.at[slot_, pl.ds(dst, HN_WORD_CHUNKS)],
                                      sem.at[slot_]).start()
            return c

        lax.fori_loop(0, MOE_BLOCK // ROWS_PER_ISSUE, issue, 0)

    @pl.when(b == 0)
    def _():
        start_weights(0)

        @pl.when(n_exp > 1)
        def _():
            start_weights(1)

        gather_rows(rt_refs[0], 0)
        for a in range(1, ROW_AHEAD):
            @pl.when(n_used > a)
            def _(a=a):
                gather_rows(rt_refs[a], a)

    @pl.when(b + ROW_AHEAD < n_used)
    def _():
        gather_rows(rt_refs[ROW_AHEAD], (b + ROW_AHEAD) % ROW_SLOTS)

    @pl.when(used)
    def _():
        first = jnp.logical_or(b == 0, be_ref[b] != be_ref[jnp.maximum(b - 1, 0)])

        ordinal = eo_ref[b]
        ws = ordinal % WEIGHT_SLOTS

        @pl.when(first)
        def _():
            for c in weight_copies(ordinal, ws):
                c.wait()

            @pl.when(ordinal + 2 < n_exp)
            def _():
                start_weights(ordinal + 2)

        pltpu.make_async_copy(hn_hbm.at[pl.ds(0, MOE_BLOCK * HN_WORD_CHUNKS)], xbuf.at[slot],
                              sem.at[slot]).wait()

        halves = ([], [])
        for c in range(HN_WORD_CHUNKS):
            w = xbuf[slot, pl.ds(c, MOE_BLOCK, stride=HN_WORD_CHUNKS), :]
            hi = lax.bitcast_convert_type(w & jnp.uint32(0xFFFF0000), F32)
            lo = lax.bitcast_convert_type(w << 16, F32)
            halves[0].append(hi.astype(BF16))
            halves[1].append(lo.astype(BF16))
        x = jnp.concatenate(halves[0] + halves[1], axis=1)
        hg = jnp.dot(x, wgf[ws].astype(BF16), preferred_element_type=F32)
        hu = jnp.dot(x, wuf[ws].astype(BF16), preferred_element_type=F32)
        hdn = (jax.nn.silu(hg) * hu).astype(BF16)
        y_ref[...] = jnp.dot(hdn, wdf[ws].astype(BF16), preferred_element_type=F32)

    @pl.when(jnp.logical_not(used))
    def _():
        y_ref[...] = jnp.zeros(y_ref.shape, y_ref.dtype)


def _experts(hn, w_gate, w_up, w_down, block_e, block_ord, used_experts, row_tok, n_used):
    D = w_gate.shape[1]
    n_rows = row_tok.shape[0]
    n_blocks = n_rows // MOE_BLOCK
    tok3 = row_tok.reshape(n_blocks, 1, MOE_BLOCK)
    hbm = pl.BlockSpec(memory_space=pl.ANY)
    tok_block = lambda im: pl.BlockSpec((1, 1, MOE_BLOCK), im, memory_space=pltpu.SMEM)
    grid_spec = pltpu.PrefetchScalarGridSpec(
        num_scalar_prefetch=4,
        grid=(n_blocks,),
        in_specs=[tok_block(lambda b, *_, a=a: (jnp.minimum(b + a, n_blocks - 1), 0, 0))
                  for a in range(ROW_SLOTS)] + [hbm, hbm, hbm, hbm],
        out_specs=pl.BlockSpec((MOE_BLOCK, D), lambda b, *_: (b, 0)),
        scratch_shapes=[pltpu.VMEM((ROW_SLOTS, MOE_BLOCK * HN_WORD_CHUNKS, LANES), jnp.uint32),
                        pltpu.VMEM((WEIGHT_SLOTS, D, EXPERT_FF), F32),
                        pltpu.VMEM((WEIGHT_SLOTS, D, EXPERT_FF), F32),
                        pltpu.VMEM((WEIGHT_SLOTS, EXPERT_FF, D), F32),
                        pltpu.SemaphoreType.DMA((ROW_SLOTS,)),
                        pltpu.SemaphoreType.DMA((WEIGHT_SLOTS,))],
    )
    return pl.pallas_call(
        _expert_kernel,
        grid_spec=grid_spec,
        out_shape=jax.ShapeDtypeStruct((n_rows, D), F32),
        compiler_params=_params(("arbitrary",)),
        name="experts",
    )(block_e, block_ord, used_experts, n_used, *([tok3] * ROW_SLOTS), hn, w_gate, w_up, w_down)


def _combine_kernel(*refs, tc):
    dest_refs = refs[:COMBINE_SLOTS]
    y_hbm, h_ref, wt_ref, o_ref, ybuf, sem = refs[COMBINE_SLOTS:]
    i = pl.program_id(0)
    slot = i % COMBINE_SLOTS

    def gather_rows(dest_ref, slot_):
        def issue(r, c):
            for k in range(TOP_K):
                src = dest_ref[0, 0, r * TOP_K + k]
                pltpu.make_async_copy(y_hbm.at[pl.ds(src, 1)],
                                      ybuf.at[slot_, k, pl.ds(r, 1)],
                                      sem.at[slot_]).start(priority=k % 2)
            return c

        lax.fori_loop(0, tc, issue, 0, unroll=4)

    @pl.when(i == 0)
    def _():
        for a in range(COMBINE_AHEAD):
            gather_rows(dest_refs[a], a)

    @pl.when(i + COMBINE_AHEAD < pl.num_programs(0))
    def _():
        gather_rows(dest_refs[COMBINE_AHEAD], (i + COMBINE_AHEAD) % COMBINE_SLOTS)

    for k in range(TOP_K):
        pltpu.make_async_copy(y_hbm.at[pl.ds(0, tc)], ybuf.at[slot, k], sem.at[slot]).wait()
    wt = wt_ref[...]
    o_ref[...] = h_ref[...] + (wt[:, 0:1] * ybuf[slot, 0] + wt[:, 1:2] * ybuf[slot, 1])


def _combine(yrows, h, wts, dest, tc=128):
    T, D = h.shape
    steps = T // tc
    row = lambda i: (i, 0)
    dest3 = dest.reshape(steps, 1, tc * TOP_K)
    dest_block = lambda im: pl.BlockSpec((1, 1, tc * TOP_K), im, memory_space=pltpu.SMEM)
    return pl.pallas_call(
        functools.partial(_combine_kernel, tc=tc),
        grid=(steps,),
        in_specs=[dest_block(lambda i, a=a: (jnp.minimum(i + a, steps - 1), 0, 0))
                  for a in range(COMBINE_SLOTS)]
                 + [pl.BlockSpec(memory_space=pl.ANY),
                    pl.BlockSpec((tc, D), row),
                    pl.BlockSpec((tc, LANES), row)],
        out_specs=pl.BlockSpec((tc, D), row),
        out_shape=jax.ShapeDtypeStruct((T, D), F32),
        scratch_shapes=[pltpu.VMEM((COMBINE_SLOTS, TOP_K, tc, D), F32),
                        pltpu.SemaphoreType.DMA((COMBINE_SLOTS,))],
        compiler_params=_params(("arbitrary",)),
        name="combine",
    )(*([dest3] * COMBINE_SLOTS), yrows, h, wts)


SUBLANES = 8
META_ROWS = 256

def _lane_cumsum(x):
    lane = lax.broadcasted_iota(jnp.int32, x.shape, 1)
    s = 1
    while s < LANES:
        x = x + jnp.where(lane >= s, pltpu.roll(x, s, 1), 0)
        s *= 2
    return x


def _dispatch_kernel(idx_ref, dest_ref, meta_ref, run_ref, prefix_ref, start_ref, *, tb):
    p = pl.program_id(0)
    i = pl.program_id(1)
    idx = idx_ref[...]
    lane = lax.broadcasted_iota(jnp.int32, idx.shape, 1)
    e1 = idx[:, 0:1]
    e2 = idx[:, 1:2]
    onehot = jnp.where(jnp.logical_or(lane == e1, lane == e2), 1.0, 0.0)

    @pl.when(jnp.logical_and(p == 0, i == 0))
    def _():
        run_ref[...] = jnp.zeros(run_ref.shape, F32)

    @pl.when(p == 0)
    def _():
        prefix_ref[i] = run_ref[...]
        run_ref[...] = run_ref[...] + jnp.sum(onehot, axis=0, keepdims=True)

    @pl.when(jnp.logical_and(p == 1, i == 0))
    def _():
        counts = run_ref[...].astype(jnp.int32)
        nblk = (counts + (MOE_BLOCK - 1)) // MOE_BLOCK
        end_blk = _lane_cumsum(nblk)
        start_ref[...] = ((end_blk - nblk) * MOE_BLOCK).astype(F32)
        has = jnp.where(counts > 0, 1, 0)
        ordinal = _lane_cumsum(has) - 1
        end_blk, has, ordinal = end_blk[0:1], has[0:1], ordinal[0:1]
        rows = lax.broadcasted_iota(jnp.int32, (META_ROWS, LANES), 0)
        lanes = lax.broadcasted_iota(jnp.int32, (META_ROWS, LANES), 1)
        is_e = lanes < N_EXPERTS
        rsum = lambda v: jnp.sum(v, axis=-1, keepdims=True)
        be = rsum(jnp.where(jnp.logical_and(is_e, end_blk <= rows), 1, 0))
        be = jnp.minimum(be, N_EXPERTS - 1)
        eo = rsum(jnp.where(lanes == be, ordinal, 0))
        ue = rsum(jnp.where(jnp.logical_and(has > 0, ordinal == rows), lanes, 0))
        n_blk = rsum(jnp.where(lanes == N_EXPERTS - 1, end_blk, 0))
        n_exp = rsum(jnp.where(is_e, has, 0))
        meta_ref[...] = jnp.where(
            lanes == 0, be, jnp.where(lanes == 1, eo, jnp.where(
                lanes == 2, ue, jnp.where(lanes == 3, n_blk, n_exp))))

    @pl.when(p == 1)
    def _():
        r = lax.broadcasted_iota(jnp.int32, (tb, tb), 0)
        c = lax.broadcasted_iota(jnp.int32, (tb, tb), 1)
        earlier = jnp.where(c < r, 1.0, 0.0).astype(BF16)
        rank = jnp.dot(earlier, onehot.astype(BF16), preferred_element_type=F32)
        rank = rank + prefix_ref[i][0:1] + start_ref[0:1]
        d1 = jnp.sum(jnp.where(lane == e1, rank, 0.0), axis=-1, keepdims=True)
        d2 = jnp.sum(jnp.where(lane == e2, rank, 0.0), axis=-1, keepdims=True)
        dest_ref[...] = jnp.where(lane == 0, d1, d2).astype(jnp.int32)


def _dispatch(idx, tb=512):
    T = idx.shape[0]
    n_rows = T * TOP_K + N_EXPERTS * MOE_BLOCK
    n_blocks = n_rows // MOE_BLOCK
    assert n_blocks <= META_ROWS
    dest2, meta = pl.pallas_call(
        functools.partial(_dispatch_kernel, tb=tb),
        grid=(2, T // tb),
        in_specs=[pl.BlockSpec((tb, LANES), lambda p, i: (i, 0))],
        out_specs=[pl.BlockSpec((tb, LANES), lambda p, i: (i * p, 0)),
                   pl.BlockSpec((META_ROWS, LANES), lambda p, i: (0, 0))],
        out_shape=[jax.ShapeDtypeStruct((T, LANES), jnp.int32),
                   jax.ShapeDtypeStruct((META_ROWS, LANES), jnp.int32)],
        scratch_shapes=[pltpu.VMEM((SUBLANES, LANES), F32),
                        pltpu.VMEM((T // tb, SUBLANES, LANES), F32),
                        pltpu.VMEM((SUBLANES, LANES), F32)],
        compiler_params=_params(("arbitrary", "arbitrary")),
        name="dispatch",
    )(idx)
    dest = dest2[:, :TOP_K].reshape(T * TOP_K)
    tok = jnp.repeat(jnp.arange(T, dtype=jnp.int32), TOP_K)
    row_tok = jnp.zeros((n_rows,), jnp.int32).at[dest].set(tok)
    block_e = meta[:n_blocks, 0]
    block_ord = meta[:n_blocks, 1]
    used_experts = meta[:N_EXPERTS, 2]
    n_used = meta[0, 3:5]
    return block_e, block_ord, used_experts, row_tok, n_used, dest


def kernel(x, positions, norm1_g, w_in, q_norm_g, k_norm_g, sink_logits, sg_ln_g, sg_ln_b, sg_w, sg_b, w_branch_att, w_branch_sg, w_out, norm2_g, w_group_router, b_group_router, w_expert_router, b_expert_router, w_gate, w_up, w_down):
    B, S, D = x.shape
    T = B * S
    h = x.reshape(T, D)
    pos = positions.reshape(T)
    for l in range(norm1_g.shape[0]):
        xn = _rmsnorm(h, norm1_g[l])
        proj = _in_proj(xn, w_in[l])
        q, k, v = _qkv_prep(proj, pos, q_norm_g[l], k_norm_g[l])
        att = _attention(q, k, v, sink_logits[l], B)
        sgo = _spatial_gating(proj, sg_ln_g[l], sg_ln_b[l], sg_w[l], sg_b[l])
        merged = _merge(att, sgo, proj, w_branch_att[l], w_branch_sg[l])
        pad = LANES - N_GROUPS - N_EXPERTS
        w_router = jnp.concatenate(
            [w_group_router[l], w_expert_router[l], jnp.zeros((D, pad), F32)], axis=1).astype(BF16)
        b_router = jnp.concatenate(
            [b_group_router[l], b_expert_router[l], jnp.zeros((pad,), F32)]).reshape(1, LANES)
        h, hn, idx, wts = _out_proj(merged, w_out[l], h, norm2_g[l], w_router, b_router)
        block_e, block_ord, used_experts, row_tok, n_used, dest = _dispatch(idx)
        yrows = _experts(hn, w_gate[l], w_up[l], w_down[l],
                         block_e, block_ord, used_experts, row_tok, n_used)
        h = _combine(yrows, h, wts, dest)
    return h.reshape(B, S, D)
```

```python
import functools

import jax
import jax.numpy as jnp
from jax import lax
from jax.experimental import pallas as pl
from jax.experimental.pallas import tpu as pltpu

F32 = jnp.float32
BF16 = jnp.bfloat16

D_MODEL = 2048
HEAD_DIM = 64
ATT_WIDTH = D_MODEL // 2
ATT_HEADS = ATT_WIDTH // HEAD_DIM
ATT_KV_HEADS = ATT_HEADS // 4
Q_PER_KV = ATT_HEADS // ATT_KV_HEADS
KV_WIDTH = ATT_KV_HEADS * HEAD_DIM
WINDOW = 128
ATT_BLOCK = 128
ROPE_DIM = HEAD_DIM // 4
ROPE_HALF = ROPE_DIM // 2
ROPE_THETA = 500000.0
SG_WIDTH = D_MODEL // 2
SG_GROUP_DIM = 128
SG_GROUPS = SG_WIDTH // SG_GROUP_DIM
SG_CHUNK = 128
OFF_Q = 0
OFF_K = OFF_Q + ATT_WIDTH
OFF_V = OFF_K + KV_WIDTH
OFF_U = OFF_V + KV_WIDTH
OFF_S = OFF_U + SG_WIDTH
OFF_G = OFF_S + SG_WIDTH
IN_COLS = OFF_G + 2 * D_MODEL
N_GROUPS = 8
EXPERTS_PER_GROUP = 8
N_EXPERTS = N_GROUPS * EXPERTS_PER_GROUP
TOP_K = 2
EXPERT_FF = D_MODEL // 4
MOE_BLOCK = 128
EPS = 1e-6
NEG_INF = -1e30

LANES = 128
HN_WORD_CHUNKS = D_MODEL // LANES // 2
ROWS_PER_ISSUE = 8
ROW_AHEAD = 4
ROW_SLOTS = ROW_AHEAD + 1
MERGE_ROW_PARTS = 4
OUT_ROW_PARTS = 4
WEIGHT_SLOTS = 3
COMBINE_AHEAD = 1
COMBINE_SLOTS = COMBINE_AHEAD + 1
VMEM_LIMIT = 56 * 1024 * 1024


def _params(sem, vmem=VMEM_LIMIT):
    return pltpu.CompilerParams(dimension_semantics=sem, vmem_limit_bytes=vmem)


def _rmsnorm_kernel(x_ref, g_ref, o_ref):
    x = x_ref[...]
    r = lax.rsqrt(jnp.mean(x * x, axis=-1, keepdims=True) + EPS)
    o_ref[...] = ((x * r) * g_ref[...]).astype(o_ref.dtype)


def _rmsnorm(x, g, tm=512):
    T, D = x.shape
    return pl.pallas_call(
        _rmsnorm_kernel,
        grid=(T // tm,),
        in_specs=[pl.BlockSpec((tm, D), lambda i: (i, 0)),
                  pl.BlockSpec((1, D), lambda i: (0, 0))],
        out_specs=pl.BlockSpec((tm, D), lambda i: (i, 0)),
        out_shape=jax.ShapeDtypeStruct((T, D), BF16),
        compiler_params=_params(("parallel",)),
        name="norm1",
    )(x, g.reshape(1, D))


def _proj_kernel(x_ref, w_ref, o_ref, wbf_ref):
    @pl.when(pl.program_id(1) == 0)
    def _():
        wbf_ref[...] = w_ref[...].astype(BF16)

    o_ref[...] = jnp.dot(x_ref[...], wbf_ref[...], preferred_element_type=F32)


def _in_proj(xn, w, tm=1024, tn=1280):
    T, D = xn.shape
    N = w.shape[1]
    return pl.pallas_call(
        _proj_kernel,
        grid=(N // tn, T // tm),
        in_specs=[pl.BlockSpec((tm, D), lambda j, i: (i, 0)),
                  pl.BlockSpec((D, tn), lambda j, i: (0, j))],
        out_specs=pl.BlockSpec((tm, tn), lambda j, i: (i, j)),
        out_shape=jax.ShapeDtypeStruct((T, N), F32),
        scratch_shapes=[pltpu.VMEM((D, tn), BF16)],
        compiler_params=_params(("arbitrary", "arbitrary")),
        name="in_proj",
    )(xn, w)


def _rope_table_kernel(pos_ref, invf_ref, cos_ref, sin_ref):
    ang = pos_ref[...].astype(F32) * invf_ref[...]
    cos_ref[...] = jnp.cos(ang)
    sin_ref[...] = jnp.sin(ang)


def _rope_tables(positions):
    T = positions.shape[0]
    rows = T * ROPE_HALF // LANES
    inv = ROPE_THETA ** (-jnp.arange(0, ROPE_DIM, 2, dtype=F32) / ROPE_DIM)
    invf = jnp.tile(inv, LANES // ROPE_HALF).reshape(1, LANES)
    pos = jnp.repeat(positions, ROPE_HALF).reshape(rows, LANES)
    whole = lambda: (0, 0)
    cos, sin = pl.pallas_call(
        _rope_table_kernel,
        in_specs=[pl.BlockSpec((rows, LANES), whole), pl.BlockSpec((1, LANES), whole)],
        out_specs=[pl.BlockSpec((rows, LANES), whole), pl.BlockSpec((rows, LANES), whole)],
        out_shape=[jax.ShapeDtypeStruct((rows, LANES), F32)] * 2,
        name="rope_tables",
    )(pos, invf)
    return cos.reshape(T, ROPE_HALF), sin.reshape(T, ROPE_HALF)


def _qkv_prep_kernel(p_ref, cos_ref, sin_ref, gq_ref, gk_ref, seg_ref, q_ref, k_ref, v_ref):
    cos = cos_ref[...]
    sin = sin_ref[...]
    lane = lax.broadcasted_iota(jnp.int32, cos.shape, 1)
    first_half = (lane % HEAD_DIM) < ROPE_HALF

    def norm_rope(x, g):
        x2 = x * x
        x2_hi = x2.astype(BF16)
        x2_lo = (x2 - x2_hi.astype(F32)).astype(BF16)
        both = jnp.dot(jnp.concatenate([x2_hi, x2_lo], axis=0), seg_ref[...],
                       preferred_element_type=F32)
        ssq = both[:x.shape[0]] + both[x.shape[0]:]
        xn = (x * lax.rsqrt(ssq * (1.0 / HEAD_DIM) + EPS)) * g
        partner = jnp.where(first_half,
                            pltpu.roll(xn, LANES - ROPE_HALF, 1),
                            pltpu.roll(xn, ROPE_HALF, 1))
        return xn * cos + partner * sin

    for c in range(ATT_WIDTH // LANES):
        x = p_ref[:, OFF_Q + c * LANES:OFF_Q + (c + 1) * LANES]
        q_ref[:, c * LANES:(c + 1) * LANES] = (
            norm_rope(x, gq_ref[...]) * (HEAD_DIM ** -0.5)).astype(q_ref.dtype)
    for c in range(KV_WIDTH // LANES):
        x = p_ref[:, OFF_K + c * LANES:OFF_K + (c + 1) * LANES]
        k_ref[:, c * LANES:(c + 1) * LANES] = norm_rope(x, gk_ref[...]).astype(k_ref.dtype)
    v_ref[...] = p_ref[:, OFF_V:OFF_V + KV_WIDTH].astype(v_ref.dtype)


def _qkv_prep(proj, positions, q_g, k_g, tq=256):
    T = proj.shape[0]
    width = OFF_U
    cos8, sin8 = _rope_tables(positions)
    rest = HEAD_DIM - ROPE_DIM
    cos_t = jnp.tile(jnp.concatenate([cos8, cos8, jnp.ones((T, rest), F32)], axis=1),
                     (1, LANES // HEAD_DIM))
    sin_t = jnp.tile(jnp.concatenate([-sin8, sin8, jnp.zeros((T, rest), F32)], axis=1),
                     (1, LANES // HEAD_DIM))
    head_of_lane = jnp.arange(LANES) // HEAD_DIM
    same_head = (head_of_lane[:, None] == head_of_lane[None, :]).astype(BF16)
    gq = jnp.tile(q_g, LANES // HEAD_DIM).reshape(1, LANES)
    gk = jnp.tile(k_g, LANES // HEAD_DIM).reshape(1, LANES)
    row = lambda i: (i, 0)
    const = lambda i: (0, 0)
    return pl.pallas_call(
        _qkv_prep_kernel,
        grid=(T // tq,),
        in_specs=[pl.BlockSpec((tq, width), row),
                  pl.BlockSpec((tq, LANES), row),
                  pl.BlockSpec((tq, LANES), row),
                  pl.BlockSpec((1, LANES), const),
                  pl.BlockSpec((1, LANES), const),
                  pl.BlockSpec((LANES, LANES), const)],
        out_specs=[pl.BlockSpec((tq, ATT_WIDTH), row),
                   pl.BlockSpec((tq, KV_WIDTH), row),
                   pl.BlockSpec((tq, KV_WIDTH), row)],
        out_shape=[jax.ShapeDtypeStruct((T, ATT_WIDTH), BF16),
                   jax.ShapeDtypeStruct((T, KV_WIDTH), BF16),
                   jax.ShapeDtypeStruct((T, KV_WIDTH), BF16)],
        compiler_params=_params(("parallel",)),
        name="qkv_prep",
    )(proj, cos_t, sin_t, gq, gk, same_head)


def _attn_kernel(sink_ref, q_ref, kp_ref, kc_ref, kn_ref, vp_ref, vc_ref, vn_ref, o_ref, *, nb):
    n = pl.program_id(1)
    rows = ATT_BLOCK
    qi = lax.broadcasted_iota(jnp.int32, (rows, ATT_BLOCK), 0) % ATT_BLOCK
    kj = lax.broadcasted_iota(jnp.int32, (rows, ATT_BLOCK), 1)
    lo_prev = jnp.where(n > 0, 0, ATT_BLOCK)
    hi_next = jnp.where(n < nb - 1, 0, -ATT_BLOCK)
    cap_prev = jnp.where(kj - qi >= lo_prev, jnp.inf, NEG_INF)
    cap_next = jnp.where(kj - qi <= hi_next, jnp.inf, NEG_INF)
    cap = jnp.concatenate([cap_prev, jnp.full((rows, ATT_BLOCK), jnp.inf, F32), cap_next], axis=1)

    kvhs = range(ATT_KV_HEADS)
    scores = []
    for kvh in kvhs:
        cols = slice(kvh * HEAD_DIM, (kvh + 1) * HEAD_DIM)
        k = jnp.concatenate([kp_ref[:, cols], kc_ref[:, cols], kn_ref[:, cols]], axis=0)
        q = jnp.concatenate(
            [q_ref[:, (kvh * Q_PER_KV + g) * HEAD_DIM:(kvh * Q_PER_KV + g + 1) * HEAD_DIM]
             for g in range(Q_PER_KV)], axis=0)
        scores.append(lax.dot_general(q, k, (((1,), (1,)), ((), ())),
                                      preferred_element_type=F32))
    probs = []
    for kvh in kvhs:
        strips = []
        for g in range(Q_PER_KV):
            sink = sink_ref[kvh * Q_PER_KV + g]
            sg = jnp.minimum(scores[kvh][g * ATT_BLOCK:(g + 1) * ATT_BLOCK], cap)
            m = jnp.maximum(jnp.max(sg, axis=-1, keepdims=True), sink)
            e = jnp.exp(sg - m)
            denom = jnp.sum(e, axis=-1, keepdims=True) + jnp.exp(sink - m)
            strips.append((e / denom).astype(BF16))
        probs.append(jnp.concatenate(strips, axis=0))
    for kvh in kvhs:
        cols = slice(kvh * HEAD_DIM, (kvh + 1) * HEAD_DIM)
        v = jnp.concatenate([vp_ref[:, cols], vc_ref[:, cols], vn_ref[:, cols]], axis=0)
        o = jnp.dot(probs[kvh], v, preferred_element_type=F32)
        for g in range(Q_PER_KV):
            h = kvh * Q_PER_KV + g
            o_ref[:, h * HEAD_DIM:(h + 1) * HEAD_DIM] = (
                o[g * ATT_BLOCK:(g + 1) * ATT_BLOCK].astype(o_ref.dtype))


def _attention(q, k, v, sink, batch):
    T = q.shape[0]
    nb = T // batch // ATT_BLOCK
    cur = lambda b, n: (b * nb + n, 0)
    prev = lambda b, n: (b * nb + jnp.maximum(n - 1, 0), 0)
    nxt = lambda b, n: (b * nb + jnp.minimum(n + 1, nb - 1), 0)
    kv = lambda im: pl.BlockSpec((ATT_BLOCK, KV_WIDTH), im)
    return pl.pallas_call(
        functools.partial(_attn_kernel, nb=nb),
        grid=(batch, nb),
        in_specs=[pl.BlockSpec(memory_space=pltpu.SMEM),
                  pl.BlockSpec((ATT_BLOCK, ATT_WIDTH), cur),
                  kv(prev), kv(cur), kv(nxt), kv(prev), kv(cur), kv(nxt)],
        out_specs=pl.BlockSpec((ATT_BLOCK, ATT_WIDTH), cur),
        out_shape=jax.ShapeDtypeStruct((T, ATT_WIDTH), BF16),
        compiler_params=_params(("parallel", "parallel")),
        name="window_attn",
    )(sink, q, k, k, k, v, v, v)


def _sg_kernel(u_ref, s_ref, lng_ref, lnb_ref, w_ref, b_ref, o_ref):
    groups = w_ref.shape[0]

    def one_chunk(ci, carry):
        rows = pl.ds(pl.multiple_of(ci * SG_CHUNK, SG_CHUNK), SG_CHUNK)
        gs = range(groups)
        cols = [slice(gi * SG_GROUP_DIM, (gi + 1) * SG_GROUP_DIM) for gi in gs]
        s = [jax.nn.gelu(s_ref[rows, cols[gi]]) for gi in gs]
        sc = [s[gi] - jnp.mean(s[gi], axis=-1, keepdims=True) for gi in gs]
        var = [jnp.mean(sc[gi] * sc[gi], axis=-1, keepdims=True) for gi in gs]
        sn = [(sc[gi] * lax.rsqrt(var[gi] + EPS)) * lng_ref[0, gi:gi + 1, :]
              + lnb_ref[0, gi:gi + 1, :] for gi in gs]
        mixed = [jnp.dot(w_ref[gi], sn[gi].astype(BF16), preferred_element_type=F32)
                 + b_ref[0, :, gi:gi + 1] for gi in gs]
        for gi in gs:
            o_ref[rows, cols[gi]] = (jax.nn.gelu(u_ref[rows, cols[gi]]) * mixed[gi]).astype(o_ref.dtype)
        return carry

    lax.fori_loop(0, u_ref.shape[0] // SG_CHUNK, one_chunk, 0)


def _spatial_gating(proj, ln_g, ln_b, w_s, b_s, halves=2, chunks=4):
    T = proj.shape[0]
    half_w = SG_WIDTH // halves
    gph = SG_GROUPS // halves
    u0, s0 = OFF_U // half_w, OFF_S // half_w
    tq = chunks * SG_CHUNK
    return pl.pallas_call(
        _sg_kernel,
        grid=(T // tq, halves),
        in_specs=[pl.BlockSpec((tq, half_w), lambda i, j: (i, u0 + j)),
                  pl.BlockSpec((tq, half_w), lambda i, j: (i, s0 + j)),
                  pl.BlockSpec((1, gph, SG_GROUP_DIM), lambda i, j: (j, 0, 0)),
                  pl.BlockSpec((1, gph, SG_GROUP_DIM), lambda i, j: (j, 0, 0)),
                  pl.BlockSpec((gph, SG_CHUNK, SG_CHUNK), lambda i, j: (j, 0, 0)),
                  pl.BlockSpec((1, SG_CHUNK, gph), lambda i, j: (j, 0, 0))],
        out_specs=pl.BlockSpec((tq, half_w), lambda i, j: (i, j)),
        out_shape=jax.ShapeDtypeStruct((T, SG_WIDTH), BF16),
        compiler_params=_params(("parallel", "parallel")),
        name="spatial_gating",
    )(proj, proj,
      ln_g.reshape(halves, gph, SG_GROUP_DIM), ln_b.reshape(halves, gph, SG_GROUP_DIM),
      w_s.astype(BF16),
      b_s.reshape(halves, gph, SG_CHUNK).transpose(0, 2, 1))


def _merge_kernel(att_ref, sgo_ref, ga_ref, gb_ref, wa_ref, wb_ref, o_ref, wa_bf, wb_bf):
    @pl.when(pl.program_id(1) == 0)
    def _():
        wa_bf[...] = wa_ref[...].astype(BF16)
        wb_bf[...] = wb_ref[...].astype(BF16)

    part_rows = att_ref.shape[0] // MERGE_ROW_PARTS
    parts = [slice(p * part_rows, (p + 1) * part_rows) for p in range(MERGE_ROW_PARTS)]
    a = [jnp.dot(att_ref[r, :], wa_bf[...], preferred_element_type=F32) for r in parts]
    b = [jnp.dot(sgo_ref[r, :], wb_bf[...], preferred_element_type=F32) for r in parts]
    for p, r in enumerate(parts):
        m = jax.nn.sigmoid(ga_ref[r, :]) * a[p] + jax.nn.sigmoid(gb_ref[r, :]) * b[p]
        o_ref[r, :] = m.astype(o_ref.dtype)


def _merge(att, sgo, proj, w_a, w_b, tm=1024, tn=512):
    T = att.shape[0]
    ga0 = OFF_G // tn
    gb0 = (OFF_G + D_MODEL) // tn
    return pl.pallas_call(
        _merge_kernel,
        grid=(D_MODEL // tn, T // tm),
        in_specs=[pl.BlockSpec((tm, ATT_WIDTH), lambda j, i: (i, 0)),
                  pl.BlockSpec((tm, SG_WIDTH), lambda j, i: (i, 0)),
                  pl.BlockSpec((tm, tn), lambda j, i: (i, ga0 + j)),
                  pl.BlockSpec((tm, tn), lambda j, i: (i, gb0 + j)),
                  pl.BlockSpec((ATT_WIDTH, tn), lambda j, i: (0, j)),
                  pl.BlockSpec((SG_WIDTH, tn), lambda j, i: (0, j))],
        out_specs=pl.BlockSpec((tm, tn), lambda j, i: (i, j)),
        out_shape=jax.ShapeDtypeStruct((T, D_MODEL), BF16),
        scratch_shapes=[pltpu.VMEM((ATT_WIDTH, tn), BF16),
                        pltpu.VMEM((SG_WIDTH, tn), BF16)],
        compiler_params=_params(("arbitrary", "arbitrary")),
        name="merge",
    )(att, sgo, proj, proj, w_a, w_b)


def _route(logits):
    lane = lax.broadcasted_iota(jnp.int32, logits.shape, 1)
    lane_f = lane.astype(F32)
    is_g = lane < N_GROUPS
    gl = jnp.where(is_g, logits, -jnp.inf)
    gmax = jnp.max(gl, axis=-1, keepdims=True)
    grp = jnp.min(jnp.where(gl == gmax, lane_f, float(LANES)), axis=-1, keepdims=True)
    gsum = jnp.sum(jnp.where(is_g, jnp.exp(logits - gmax), 0.0), axis=-1, keepdims=True)
    g_w = 1.0 / gsum
    e_lane = lane - N_GROUPS
    in_grp = jnp.logical_and(
        jnp.logical_and(e_lane >= 0, e_lane < N_EXPERTS),
        (e_lane // EXPERTS_PER_GROUP).astype(F32) == grp)
    el = jnp.where(in_grp, logits, -jnp.inf)
    v1 = jnp.max(el, axis=-1, keepdims=True)
    i1 = jnp.min(jnp.where(jnp.logical_and(in_grp, el == v1), lane_f, float(LANES)),
                 axis=-1, keepdims=True)
    rest = jnp.logical_and(in_grp, lane_f != i1)
    el2 = jnp.where(rest, logits, -jnp.inf)
    v2 = jnp.max(el2, axis=-1, keepdims=True)
    i2 = jnp.min(jnp.where(jnp.logical_and(rest, el2 == v2), lane_f, float(LANES)),
                 axis=-1, keepdims=True)
    e21 = jnp.exp(v2 - v1)
    w1 = g_w / (1.0 + e21)
    w2 = g_w * e21 / (1.0 + e21)
    idx = jnp.where(lane == 0, i1, i2) - float(N_GROUPS)
    wts = jnp.where(lane == 0, w1, jnp.where(lane == 1, w2, 0.0))
    return idx.astype(jnp.int32), wts


def _out_kernel(m_ref, w_ref, x_ref, g_ref, wr_ref, br_ref, h_ref, hn_ref, idx_ref, wt_ref):
    part_rows = m_ref.shape[0] // OUT_ROW_PARTS
    parts = [slice(p * part_rows, (p + 1) * part_rows) for p in range(OUT_ROW_PARTS)]
    hs = [x_ref[r, :] + jnp.dot(m_ref[r, :], w_ref[...], preferred_element_type=F32)
          for r in parts]
    for p, r in enumerate(parts):
        h = hs[p]
        h_ref[r, :] = h
        rs = lax.rsqrt(jnp.mean(h * h, axis=-1, keepdims=True) + EPS)
        hn = ((h * rs) * g_ref[...]).astype(BF16)
        bits = lax.bitcast_convert_type(hn.astype(F32), jnp.uint32)
        for c in range(HN_WORD_CHUNKS):
            hi = bits[:, c * LANES:(c + 1) * LANES]
            lo = bits[:, (c + HN_WORD_CHUNKS) * LANES:(c + HN_WORD_CHUNKS + 1) * LANES]
            hn_ref[pl.ds(p * part_rows * HN_WORD_CHUNKS + c, part_rows,
                         stride=HN_WORD_CHUNKS), :] = hi | (lo >> 16)
        logits = jnp.dot(hn, wr_ref[...], preferred_element_type=F32) + br_ref[...]
        idx, wts = _route(logits)
        idx_ref[r, :] = idx
        wt_ref[r, :] = wts


def _out_proj(merged, w_out, x, g2, w_router, b_router, tm=512):
    T, D = x.shape
    row = lambda i: (i, 0)
    const = lambda i: (0, 0)
    return pl.pallas_call(
        _out_kernel,
        grid=(T // tm,),
        in_specs=[pl.BlockSpec((tm, D), row),
                  pl.BlockSpec((D, D), const),
                  pl.BlockSpec((tm, D), row),
                  pl.BlockSpec((1, D), const),
                  pl.BlockSpec((D, LANES), const),
                  pl.BlockSpec((1, LANES), const)],
        out_specs=[pl.BlockSpec((tm, D), row),
                   pl.BlockSpec((tm * HN_WORD_CHUNKS, LANES), row),
                   pl.BlockSpec((tm, LANES), row),
                   pl.BlockSpec((tm, LANES), row)],
        out_shape=[jax.ShapeDtypeStruct((T, D), F32),
                   jax.ShapeDtypeStruct((T * HN_WORD_CHUNKS, LANES), jnp.uint32),
                   jax.ShapeDtypeStruct((T, LANES), jnp.int32),
                   jax.ShapeDtypeStruct((T, LANES), F32)],
        compiler_params=_params(("parallel",)),
        name="out_proj_router",
    )(merged, w_out.astype(BF16), x, g2.reshape(1, D), w_router, b_router)


def _expert_kernel(be_ref, eo_ref, ue_ref, nr_ref, nu_ref, *refs):
    rt_refs = refs[:ROW_SLOTS]
    (hn_hbm, wg_hbm, wu_hbm, wd_hbm, y_ref,
     xbuf, wgf, wuf, wdf, sem, wsem) = refs[ROW_SLOTS:]
    b = pl.program_id(0)
    n_used = nu_ref[0]
    n_exp = nu_ref[1]
    used = b < n_used
    slot = b % ROW_SLOTS

    def weight_copies(ordinal, slot_):
        e = ue_ref[ordinal]
        return (pltpu.make_async_copy(wg_hbm.at[e], wgf.at[slot_], wsem.at[slot_]),
                pltpu.make_async_copy(wu_hbm.at[e], wuf.at[slot_], wsem.at[slot_]),
                pltpu.make_async_copy(wd_hbm.at[e], wdf.at[slot_], wsem.at[slot_]))

    def start_weights(ordinal):
        for c in weight_copies(ordinal, ordinal % WEIGHT_SLOTS):
            c.start(priority=1)

    def issue_trips(blk):
        return (nr_ref[blk] + (ROWS_PER_ISSUE - 1)) // ROWS_PER_ISSUE

    def gather_rows(tok_ref, blk, slot_):
        def issue(g, c):
            for j in range(ROWS_PER_ISSUE):
                r = g * ROWS_PER_ISSUE + j
                src = pl.multiple_of(tok_ref[0, 0, r] * HN_WORD_CHUNKS, HN_WORD_CHUNKS)
                dst = pl.multiple_of(r * HN_WORD_CHUNKS, HN_WORD_CHUNKS)
                pltpu.make_async_copy(hn_hbm.at[pl.ds(src, HN_WORD_CHUNKS)],
                                      xbuf.at[slot_, pl.ds(dst, HN_WORD_CHUNKS)],
                                      sem.at[slot_]).start()
            return c

        lax.fori_loop(0, issue_trips(blk), issue, 0)

    @pl.when(b == 0)
    def _():
        start_weights(0)

        @pl.when(n_exp > 1)
        def _():
            start_weights(1)

        xbuf[...] = jnp.zeros(xbuf.shape, xbuf.dtype)
        gather_rows(rt_refs[0], 0, 0)
        for a in range(1, ROW_AHEAD):
            @pl.when(n_used > a)
            def _(a=a):
                gather_rows(rt_refs[a], a, a)

    @pl.when(b + ROW_AHEAD < n_used)
    def _():
        gather_rows(rt_refs[ROW_AHEAD], b + ROW_AHEAD, (b + ROW_AHEAD) % ROW_SLOTS)

    @pl.when(used)
    def _():
        first = jnp.logical_or(b == 0, be_ref[b] != be_ref[jnp.maximum(b - 1, 0)])

        ordinal = eo_ref[b]
        ws = ordinal % WEIGHT_SLOTS

        @pl.when(first)
        def _():
            for c in weight_copies(ordinal, ws):
                c.wait()

            @pl.when(ordinal + 2 < n_exp)
            def _():
                start_weights(ordinal + 2)

        words = issue_trips(b) * (ROWS_PER_ISSUE * HN_WORD_CHUNKS)
        pltpu.make_async_copy(hn_hbm.at[pl.ds(0, words)], xbuf.at[slot, pl.ds(0, words)],
                              sem.at[slot]).wait()

        halves = ([], [])
        for c in range(HN_WORD_CHUNKS):
            w = xbuf[slot, pl.ds(c, MOE_BLOCK, stride=HN_WORD_CHUNKS), :]
            hi = lax.bitcast_convert_type(w & jnp.uint32(0xFFFF0000), F32)
            lo = lax.bitcast_convert_type(w << 16, F32)
            halves[0].append(hi.astype(BF16))
            halves[1].append(lo.astype(BF16))
        x = jnp.concatenate(halves[0] + halves[1], axis=1)
        hg = jnp.dot(x, wgf[ws].astype(BF16), preferred_element_type=F32)
        hu = jnp.dot(x, wuf[ws].astype(BF16), preferred_element_type=F32)
        hdn = (jax.nn.silu(hg) * hu).astype(BF16)
        y_ref[...] = jnp.dot(hdn, wdf[ws].astype(BF16), preferred_element_type=F32)

    @pl.when(jnp.logical_not(used))
    def _():
        y_ref[...] = jnp.zeros(y_ref.shape, y_ref.dtype)


def _experts(hn, w_gate, w_up, w_down, block_e, block_ord, used_experts, block_rows, row_tok,
             n_used):
    D = w_gate.shape[1]
    n_rows = row_tok.shape[0]
    n_blocks = n_rows // MOE_BLOCK
    tok3 = row_tok.reshape(n_blocks, 1, MOE_BLOCK)
    hbm = pl.BlockSpec(memory_space=pl.ANY)
    tok_block = lambda im: pl.BlockSpec((1, 1, MOE_BLOCK), im, memory_space=pltpu.SMEM)
    grid_spec = pltpu.PrefetchScalarGridSpec(
        num_scalar_prefetch=5,
        grid=(n_blocks,),
        in_specs=[tok_block(lambda b, *_, a=a: (jnp.minimum(b + a, n_blocks - 1), 0, 0))
                  for a in range(ROW_SLOTS)] + [hbm, hbm, hbm, hbm],
        out_specs=pl.BlockSpec((MOE_BLOCK, D), lambda b, *_: (b, 0)),
        scratch_shapes=[pltpu.VMEM((ROW_SLOTS, MOE_BLOCK * HN_WORD_CHUNKS, LANES), jnp.uint32),
                        pltpu.VMEM((WEIGHT_SLOTS, D, EXPERT_FF), F32),
                        pltpu.VMEM((WEIGHT_SLOTS, D, EXPERT_FF), F32),
                        pltpu.VMEM((WEIGHT_SLOTS, EXPERT_FF, D), F32),
                        pltpu.SemaphoreType.DMA((ROW_SLOTS,)),
                        pltpu.SemaphoreType.DMA((WEIGHT_SLOTS,))],
    )
    return pl.pallas_call(
        _expert_kernel,
        grid_spec=grid_spec,
        out_shape=jax.ShapeDtypeStruct((n_rows, D), F32),
        compiler_params=_params(("arbitrary",)),
        name="experts",
    )(block_e, block_ord, used_experts, block_rows, n_used, *([tok3] * ROW_SLOTS),
      hn, w_gate, w_up, w_down)


def _combine_kernel(*refs, tc):
    dest_refs = refs[:COMBINE_SLOTS]
    y_hbm, h_ref, wt_ref, o_ref, ybuf, sem = refs[COMBINE_SLOTS:]
    i = pl.program_id(0)
    slot = i % COMBINE_SLOTS

    def gather_rows(dest_ref, slot_):
        def issue(r, c):
            for k in range(TOP_K):
                src = dest_ref[0, 0, r * TOP_K + k]
                pltpu.make_async_copy(y_hbm.at[pl.ds(src, 1)],
                                      ybuf.at[slot_, k, pl.ds(r, 1)],
                                      sem.at[slot_]).start(priority=k % 2)
            return c

        lax.fori_loop(0, tc, issue, 0, unroll=4)

    @pl.when(i == 0)
    def _():
        for a in range(COMBINE_AHEAD):
            gather_rows(dest_refs[a], a)

    @pl.when(i + COMBINE_AHEAD < pl.num_programs(0))
    def _():
        gather_rows(dest_refs[COMBINE_AHEAD], (i + COMBINE_AHEAD) % COMBINE_SLOTS)

    for k in range(TOP_K):
        pltpu.make_async_copy(y_hbm.at[pl.ds(0, tc)], ybuf.at[slot, k], sem.at[slot]).wait()
    wt = wt_ref[...]
    o_ref[...] = h_ref[...] + (wt[:, 0:1] * ybuf[slot, 0] + wt[:, 1:2] * ybuf[slot, 1])


def _combine(yrows, h, wts, dest, tc=128):
    T, D = h.shape
    steps = T // tc
    row = lambda i: (i, 0)
    dest3 = dest.reshape(steps, 1, tc * TOP_K)
    dest_block = lambda im: pl.BlockSpec((1, 1, tc * TOP_K), im, memory_space=pltpu.SMEM)
    return pl.pallas_call(
        functools.partial(_combine_kernel, tc=tc),
        grid=(steps,),
        in_specs=[dest_block(lambda i, a=a: (jnp.minimum(i + a, steps - 1), 0, 0))
                  for a in range(COMBINE_SLOTS)]
                 + [pl.BlockSpec(memory_space=pl.ANY),
                    pl.BlockSpec((tc, D), row),
                    pl.BlockSpec((tc, LANES), row)],
        out_specs=pl.BlockSpec((tc, D), row),
        out_shape=jax.ShapeDtypeStruct((T, D), F32),
        scratch_shapes=[pltpu.VMEM((COMBINE_SLOTS, TOP_K, tc, D), F32),
                        pltpu.SemaphoreType.DMA((COMBINE_SLOTS,))],
        compiler_params=_params(("arbitrary",)),
        name="combine",
    )(*([dest3] * COMBINE_SLOTS), yrows, h, wts)


SUBLANES = 8
META_ROWS = 256

def _lane_cumsum(x):
    lane = lax.broadcasted_iota(jnp.int32, x.shape, 1)
    s = 1
    while s < LANES:
        x = x + jnp.where(lane >= s, pltpu.roll(x, s, 1), 0)
        s *= 2
    return x


def _dispatch_kernel(idx_ref, dest_ref, meta_ref, run_ref, prefix_ref, start_ref, *, tb):
    p = pl.program_id(0)
    i = pl.program_id(1)
    idx = idx_ref[...]
    lane = lax.broadcasted_iota(jnp.int32, idx.shape, 1)
    e1 = idx[:, 0:1]
    e2 = idx[:, 1:2]
    onehot = jnp.where(jnp.logical_or(lane == e1, lane == e2), 1.0, 0.0)

    @pl.when(jnp.logical_and(p == 0, i == 0))
    def _():
        run_ref[...] = jnp.zeros(run_ref.shape, F32)

    @pl.when(p == 0)
    def _():
        prefix_ref[i] = run_ref[...]
        run_ref[...] = run_ref[...] + jnp.sum(onehot, axis=0, keepdims=True)

    @pl.when(jnp.logical_and(p == 1, i == 0))
    def _():
        counts = run_ref[...].astype(jnp.int32)
        nblk = (counts + (MOE_BLOCK - 1)) // MOE_BLOCK
        end_blk = _lane_cumsum(nblk)
        start_ref[...] = ((end_blk - nblk) * MOE_BLOCK).astype(F32)
        has = jnp.where(counts > 0, 1, 0)
        ordinal = _lane_cumsum(has) - 1
        start_blk, counts = (end_blk - nblk)[0:1], counts[0:1]
        end_blk, has, ordinal = end_blk[0:1], has[0:1], ordinal[0:1]
        rows = lax.broadcasted_iota(jnp.int32, (META_ROWS, LANES), 0)
        lanes = lax.broadcasted_iota(jnp.int32, (META_ROWS, LANES), 1)
        is_e = lanes < N_EXPERTS
        rsum = lambda v: jnp.sum(v, axis=-1, keepdims=True)
        be = rsum(jnp.where(jnp.logical_and(is_e, end_blk <= rows), 1, 0))
        be = jnp.minimum(be, N_EXPERTS - 1)
        eo = rsum(jnp.where(lanes == be, ordinal, 0))
        ue = rsum(jnp.where(jnp.logical_and(has > 0, ordinal == rows), lanes, 0))
        n_blk = rsum(jnp.where(lanes == N_EXPERTS - 1, end_blk, 0))
        n_exp = rsum(jnp.where(is_e, has, 0))
        own = lanes == be
        left = rsum(jnp.where(own, counts, 0)) - MOE_BLOCK * (rows - rsum(jnp.where(own, start_blk, 0)))
        n_valid = jnp.clip(left, 0, MOE_BLOCK)
        meta_ref[...] = jnp.where(
            lanes == 0, be, jnp.where(lanes == 1, eo, jnp.where(
                lanes == 2, ue, jnp.where(lanes == 3, n_blk, jnp.where(lanes == 4, n_exp, n_valid)))))

    @pl.when(p == 1)
    def _():
        r = lax.broadcasted_iota(jnp.int32, (tb, tb), 0)
        c = lax.broadcasted_iota(jnp.int32, (tb, tb), 1)
        earlier = jnp.where(c < r, 1.0, 0.0).astype(BF16)
        rank = jnp.dot(earlier, onehot.astype(BF16), preferred_element_type=F32)
        rank = rank + prefix_ref[i][0:1] + start_ref[0:1]
        d1 = jnp.sum(jnp.where(lane == e1, rank, 0.0), axis=-1, keepdims=True)
        d2 = jnp.sum(jnp.where(lane == e2, rank, 0.0), axis=-1, keepdims=True)
        dest_ref[...] = jnp.where(lane == 0, d1, d2).astype(jnp.int32)


def _dispatch(idx, tb=512):
    T = idx.shape[0]
    n_rows = T * TOP_K + N_EXPERTS * MOE_BLOCK
    n_blocks = n_rows // MOE_BLOCK
    assert n_blocks <= META_ROWS
    dest2, meta = pl.pallas_call(
        functools.partial(_dispatch_kernel, tb=tb),
        grid=(2, T // tb),
        in_specs=[pl.BlockSpec((tb, LANES), lambda p, i: (i, 0))],
        out_specs=[pl.BlockSpec((tb, LANES), lambda p, i: (i * p, 0)),
                   pl.BlockSpec((META_ROWS, LANES), lambda p, i: (0, 0))],
        out_shape=[jax.ShapeDtypeStruct((T, LANES), jnp.int32),
                   jax.ShapeDtypeStruct((META_ROWS, LANES), jnp.int32)],
        scratch_shapes=[pltpu.VMEM((SUBLANES, LANES), F32),
                        pltpu.VMEM((T // tb, SUBLANES, LANES), F32),
                        pltpu.VMEM((SUBLANES, LANES), F32)],
        compiler_params=_params(("arbitrary", "arbitrary")),
        name="dispatch",
    )(idx)
    dest = dest2[:, :TOP_K].reshape(T * TOP_K)
    tok = jnp.repeat(jnp.arange(T, dtype=jnp.int32), TOP_K)
    row_tok = jnp.zeros((n_rows,), jnp.int32).at[dest].set(tok)
    block_e = meta[:n_blocks, 0]
    block_ord = meta[:n_blocks, 1]
    used_experts = meta[:N_EXPERTS, 2]
    n_used = meta[0, 3:5]
    block_rows = meta[:n_blocks, 5]
    return block_e, block_ord, used_experts, block_rows, row_tok, n_used, dest


def kernel(x, positions, norm1_g, w_in, q_norm_g, k_norm_g, sink_logits, sg_ln_g, sg_ln_b, sg_w, sg_b, w_branch_att, w_branch_sg, w_out, norm2_g, w_group_router, b_group_router, w_expert_router, b_expert_router, w_gate, w_up, w_down):
    B, S, D = x.shape
    T = B * S
    h = x.reshape(T, D)
    pos = positions.reshape(T)
    for l in range(norm1_g.shape[0]):
        xn = _rmsnorm(h, norm1_g[l])
        proj = _in_proj(xn, w_in[l])
        q, k, v = _qkv_prep(proj, pos, q_norm_g[l], k_norm_g[l])
        att = _attention(q, k, v, sink_logits[l], B)
        sgo = _spatial_gating(proj, sg_ln_g[l], sg_ln_b[l], sg_w[l], sg_b[l])
        merged = _merge(att, sgo, proj, w_branch_att[l], w_branch_sg[l])
        pad = LANES - N_GROUPS - N_EXPERTS
        w_router = jnp.concatenate(
            [w_group_router[l], w_expert_router[l], jnp.zeros((D, pad), F32)], axis=1).astype(BF16)
        b_router = jnp.concatenate(
            [b_group_router[l], b_expert_router[l], jnp.zeros((pad,), F32)]).reshape(1, LANES)
        h, hn, idx, wts = _out_proj(merged, w_out[l], h, norm2_g[l], w_router, b_router)
        block_e, block_ord, used_experts, block_rows, row_tok, n_used, dest = _dispatch(idx)
        yrows = _experts(hn, w_gate[l], w_up[l], w_down[l],
                         block_e, block_ord, used_experts, block_rows, row_tok, n_used)
        h = _combine(yrows, h, wts, dest)
    return h.reshape(B, S, D)
```

```python
import functools

import jax
import jax.numpy as jnp
from jax import lax
from jax.experimental import pallas as pl
from jax.experimental.pallas import tpu as pltpu

F32 = jnp.float32
BF16 = jnp.bfloat16

D_MODEL = 2048
HEAD_DIM = 64
ATT_WIDTH = D_MODEL // 2
ATT_HEADS = ATT_WIDTH // HEAD_DIM
ATT_KV_HEADS = ATT_HEADS // 4
Q_PER_KV = ATT_HEADS // ATT_KV_HEADS
KV_WIDTH = ATT_KV_HEADS * HEAD_DIM
WINDOW = 128
ATT_BLOCK = 128
ROPE_DIM = HEAD_DIM // 4
ROPE_HALF = ROPE_DIM // 2
ROPE_THETA = 500000.0
SG_WIDTH = D_MODEL // 2
SG_GROUP_DIM = 128
SG_GROUPS = SG_WIDTH // SG_GROUP_DIM
SG_CHUNK = 128
OFF_Q = 0
OFF_K = OFF_Q + ATT_WIDTH
OFF_V = OFF_K + KV_WIDTH
OFF_U = OFF_V + KV_WIDTH
OFF_S = OFF_U + SG_WIDTH
OFF_G = OFF_S + SG_WIDTH
IN_COLS = OFF_G + 2 * D_MODEL
N_GROUPS = 8
EXPERTS_PER_GROUP = 8
N_EXPERTS = N_GROUPS * EXPERTS_PER_GROUP
TOP_K = 2
EXPERT_FF = D_MODEL // 4
MOE_BLOCK = 128
EPS = 1e-6
NEG_INF = -1e30

LANES = 128
HN_WORD_CHUNKS = D_MODEL // LANES // 2
ROWS_PER_ISSUE = 8
ROW_AHEAD = 4
ROW_SLOTS = ROW_AHEAD + 1
Y_HALVES = 2
Y_HALF_CHUNKS = D_MODEL // LANES // Y_HALVES
MERGE_ROW_PARTS = 4
OUT_ROW_PARTS = 4
WEIGHT_SLOTS = 3
COMBINE_AHEAD = 1
COMBINE_SLOTS = COMBINE_AHEAD + 1
VMEM_LIMIT = 56 * 1024 * 1024


def _params(sem, vmem=VMEM_LIMIT):
    return pltpu.CompilerParams(dimension_semantics=sem, vmem_limit_bytes=vmem)


def _rmsnorm_kernel(x_ref, g_ref, o_ref):
    x = x_ref[...]
    r = lax.rsqrt(jnp.mean(x * x, axis=-1, keepdims=True) + EPS)
    o_ref[...] = ((x * r) * g_ref[...]).astype(o_ref.dtype)


def _rmsnorm(x, g, tm=512):
    T, D = x.shape
    return pl.pallas_call(
        _rmsnorm_kernel,
        grid=(T // tm,),
        in_specs=[pl.BlockSpec((tm, D), lambda i: (i, 0)),
                  pl.BlockSpec((1, D), lambda i: (0, 0))],
        out_specs=pl.BlockSpec((tm, D), lambda i: (i, 0)),
        out_shape=jax.ShapeDtypeStruct((T, D), BF16),
        compiler_params=_params(("parallel",)),
        name="norm1",
    )(x, g.reshape(1, D))


def _proj_kernel(x_ref, w_ref, o_ref, wbf_ref):
    @pl.when(pl.program_id(1) == 0)
    def _():
        wbf_ref[...] = w_ref[...].astype(BF16)

    o_ref[...] = jnp.dot(x_ref[...], wbf_ref[...], preferred_element_type=F32)


def _in_proj(xn, w, tm=1024, tn=1280):
    T, D = xn.shape
    N = w.shape[1]
    return pl.pallas_call(
        _proj_kernel,
        grid=(N // tn, T // tm),
        in_specs=[pl.BlockSpec((tm, D), lambda j, i: (i, 0)),
                  pl.BlockSpec((D, tn), lambda j, i: (0, j))],
        out_specs=pl.BlockSpec((tm, tn), lambda j, i: (i, j)),
        out_shape=jax.ShapeDtypeStruct((T, N), F32),
        scratch_shapes=[pltpu.VMEM((D, tn), BF16)],
        compiler_params=_params(("arbitrary", "arbitrary")),
        name="in_proj",
    )(xn, w)


def _rope_table_kernel(pos_ref, invf_ref, cos_ref, sin_ref):
    ang = pos_ref[...].astype(F32) * invf_ref[...]
    cos_ref[...] = jnp.cos(ang)
    sin_ref[...] = jnp.sin(ang)


def _rope_tables(positions):
    T = positions.shape[0]
    rows = T * ROPE_HALF // LANES
    inv = ROPE_THETA ** (-jnp.arange(0, ROPE_DIM, 2, dtype=F32) / ROPE_DIM)
    invf = jnp.tile(inv, LANES // ROPE_HALF).reshape(1, LANES)
    pos = jnp.repeat(positions, ROPE_HALF).reshape(rows, LANES)
    whole = lambda: (0, 0)
    cos, sin = pl.pallas_call(
        _rope_table_kernel,
        in_specs=[pl.BlockSpec((rows, LANES), whole), pl.BlockSpec((1, LANES), whole)],
        out_specs=[pl.BlockSpec((rows, LANES), whole), pl.BlockSpec((rows, LANES), whole)],
        out_shape=[jax.ShapeDtypeStruct((rows, LANES), F32)] * 2,
        name="rope_tables",
    )(pos, invf)
    return cos.reshape(T, ROPE_HALF), sin.reshape(T, ROPE_HALF)


def _qkv_prep_kernel(p_ref, cos_ref, sin_ref, gq_ref, gk_ref, seg_ref, q_ref, k_ref, v_ref):
    cos = cos_ref[...]
    sin = sin_ref[...]
    lane = lax.broadcasted_iota(jnp.int32, cos.shape, 1)
    first_half = (lane % HEAD_DIM) < ROPE_HALF

    def norm_rope(x, g):
        x2 = x * x
        x2_hi = x2.astype(BF16)
        x2_lo = (x2 - x2_hi.astype(F32)).astype(BF16)
        both = jnp.dot(jnp.concatenate([x2_hi, x2_lo], axis=0), seg_ref[...],
                       preferred_element_type=F32)
        ssq = both[:x.shape[0]] + both[x.shape[0]:]
        xn = (x * lax.rsqrt(ssq * (1.0 / HEAD_DIM) + EPS)) * g
        partner = jnp.where(first_half,
                            pltpu.roll(xn, LANES - ROPE_HALF, 1),
                            pltpu.roll(xn, ROPE_HALF, 1))
        return xn * cos + partner * sin

    for c in range(ATT_WIDTH // LANES):
        x = p_ref[:, OFF_Q + c * LANES:OFF_Q + (c + 1) * LANES]
        q_ref[:, c * LANES:(c + 1) * LANES] = (
            norm_rope(x, gq_ref[...]) * (HEAD_DIM ** -0.5)).astype(q_ref.dtype)
    for c in range(KV_WIDTH // LANES):
        x = p_ref[:, OFF_K + c * LANES:OFF_K + (c + 1) * LANES]
        k_ref[:, c * LANES:(c + 1) * LANES] = norm_rope(x, gk_ref[...]).astype(k_ref.dtype)
    v_ref[...] = p_ref[:, OFF_V:OFF_V + KV_WIDTH].astype(v_ref.dtype)


def _qkv_prep(proj, positions, q_g, k_g, tq=256):
    T = proj.shape[0]
    width = OFF_U
    cos8, sin8 = _rope_tables(positions)
    rest = HEAD_DIM - ROPE_DIM
    cos_t = jnp.tile(jnp.concatenate([cos8, cos8, jnp.ones((T, rest), F32)], axis=1),
                     (1, LANES // HEAD_DIM))
    sin_t = jnp.tile(jnp.concatenate([-sin8, sin8, jnp.zeros((T, rest), F32)], axis=1),
                     (1, LANES // HEAD_DIM))
    head_of_lane = jnp.arange(LANES) // HEAD_DIM
    same_head = (head_of_lane[:, None] == head_of_lane[None, :]).astype(BF16)
    gq = jnp.tile(q_g, LANES // HEAD_DIM).reshape(1, LANES)
    gk = jnp.tile(k_g, LANES // HEAD_DIM).reshape(1, LANES)
    row = lambda i: (i, 0)
    const = lambda i: (0, 0)
    return pl.pallas_call(
        _qkv_prep_kernel,
        grid=(T // tq,),
        in_specs=[pl.BlockSpec((tq, width), row),
                  pl.BlockSpec((tq, LANES), row),
                  pl.BlockSpec((tq, LANES), row),
                  pl.BlockSpec((1, LANES), const),
                  pl.BlockSpec((1, LANES), const),
                  pl.BlockSpec((LANES, LANES), const)],
        out_specs=[pl.BlockSpec((tq, ATT_WIDTH), row),
                   pl.BlockSpec((tq, KV_WIDTH), row),
                   pl.BlockSpec((tq, KV_WIDTH), row)],
        out_shape=[jax.ShapeDtypeStruct((T, ATT_WIDTH), BF16),
                   jax.ShapeDtypeStruct((T, KV_WIDTH), BF16),
                   jax.ShapeDtypeStruct((T, KV_WIDTH), BF16)],
        compiler_params=_params(("parallel",)),
        name="qkv_prep",
    )(proj, cos_t, sin_t, gq, gk, same_head)


def _attn_kernel(sink_ref, q_ref, kp_ref, kc_ref, kn_ref, vp_ref, vc_ref, vn_ref, o_ref, *, nb):
    n = pl.program_id(1)
    rows = ATT_BLOCK
    qi = lax.broadcasted_iota(jnp.int32, (rows, ATT_BLOCK), 0) % ATT_BLOCK
    kj = lax.broadcasted_iota(jnp.int32, (rows, ATT_BLOCK), 1)
    lo_prev = jnp.where(n > 0, 0, ATT_BLOCK)
    hi_next = jnp.where(n < nb - 1, 0, -ATT_BLOCK)
    cap_prev = jnp.where(kj - qi >= lo_prev, jnp.inf, NEG_INF)
    cap_next = jnp.where(kj - qi <= hi_next, jnp.inf, NEG_INF)
    cap = jnp.concatenate([cap_prev, jnp.full((rows, ATT_BLOCK), jnp.inf, F32), cap_next], axis=1)

    kvhs = range(ATT_KV_HEADS)
    scores = []
    for kvh in kvhs:
        cols = slice(kvh * HEAD_DIM, (kvh + 1) * HEAD_DIM)
        k = jnp.concatenate([kp_ref[:, cols], kc_ref[:, cols], kn_ref[:, cols]], axis=0)
        q = jnp.concatenate(
            [q_ref[:, (kvh * Q_PER_KV + g) * HEAD_DIM:(kvh * Q_PER_KV + g + 1) * HEAD_DIM]
             for g in range(Q_PER_KV)], axis=0)
        scores.append(lax.dot_general(q, k, (((1,), (1,)), ((), ())),
                                      preferred_element_type=F32))
    probs = []
    for kvh in kvhs:
        strips = []
        for g in range(Q_PER_KV):
            sink = sink_ref[kvh * Q_PER_KV + g]
            sg = jnp.minimum(scores[kvh][g * ATT_BLOCK:(g + 1) * ATT_BLOCK], cap)
            m = jnp.maximum(jnp.max(sg, axis=-1, keepdims=True), sink)
            e = jnp.exp(sg - m)
            denom = jnp.sum(e, axis=-1, keepdims=True) + jnp.exp(sink - m)
            strips.append((e / denom).astype(BF16))
        probs.append(jnp.concatenate(strips, axis=0))
    for kvh in kvhs:
        cols = slice(kvh * HEAD_DIM, (kvh + 1) * HEAD_DIM)
        v = jnp.concatenate([vp_ref[:, cols], vc_ref[:, cols], vn_ref[:, cols]], axis=0)
        o = jnp.dot(probs[kvh], v, preferred_element_type=F32)
        for g in range(Q_PER_KV):
            h = kvh * Q_PER_KV + g
            o_ref[:, h * HEAD_DIM:(h + 1) * HEAD_DIM] = (
                o[g * ATT_BLOCK:(g + 1) * ATT_BLOCK].astype(o_ref.dtype))


def _attention(q, k, v, sink, batch):
    T = q.shape[0]
    nb = T // batch // ATT_BLOCK
    cur = lambda b, n: (b * nb + n, 0)
    prev = lambda b, n: (b * nb + jnp.maximum(n - 1, 0), 0)
    nxt = lambda b, n: (b * nb + jnp.minimum(n + 1, nb - 1), 0)
    kv = lambda im: pl.BlockSpec((ATT_BLOCK, KV_WIDTH), im)
    return pl.pallas_call(
        functools.partial(_attn_kernel, nb=nb),
        grid=(batch, nb),
        in_specs=[pl.BlockSpec(memory_space=pltpu.SMEM),
                  pl.BlockSpec((ATT_BLOCK, ATT_WIDTH), cur),
                  kv(prev), kv(cur), kv(nxt), kv(prev), kv(cur), kv(nxt)],
        out_specs=pl.BlockSpec((ATT_BLOCK, ATT_WIDTH), cur),
        out_shape=jax.ShapeDtypeStruct((T, ATT_WIDTH), BF16),
        compiler_params=_params(("parallel", "parallel")),
        name="window_attn",
    )(sink, q, k, k, k, v, v, v)


def _sg_kernel(u_ref, s_ref, lng_ref, lnb_ref, w_ref, b_ref, o_ref):
    groups = w_ref.shape[0]

    def one_chunk(ci, carry):
        rows = pl.ds(pl.multiple_of(ci * SG_CHUNK, SG_CHUNK), SG_CHUNK)
        gs = range(groups)
        cols = [slice(gi * SG_GROUP_DIM, (gi + 1) * SG_GROUP_DIM) for gi in gs]
        s = [jax.nn.gelu(s_ref[rows, cols[gi]]) for gi in gs]
        sc = [s[gi] - jnp.mean(s[gi], axis=-1, keepdims=True) for gi in gs]
        var = [jnp.mean(sc[gi] * sc[gi], axis=-1, keepdims=True) for gi in gs]
        sn = [(sc[gi] * lax.rsqrt(var[gi] + EPS)) * lng_ref[0, gi:gi + 1, :]
              + lnb_ref[0, gi:gi + 1, :] for gi in gs]
        mixed = [jnp.dot(w_ref[gi], sn[gi].astype(BF16), preferred_element_type=F32)
                 + b_ref[0, :, gi:gi + 1] for gi in gs]
        for gi in gs:
            o_ref[rows, cols[gi]] = (jax.nn.gelu(u_ref[rows, cols[gi]]) * mixed[gi]).astype(o_ref.dtype)
        return carry

    lax.fori_loop(0, u_ref.shape[0] // SG_CHUNK, one_chunk, 0)


def _spatial_gating(proj, ln_g, ln_b, w_s, b_s, halves=2, chunks=4):
    T = proj.shape[0]
    half_w = SG_WIDTH // halves
    gph = SG_GROUPS // halves
    u0, s0 = OFF_U // half_w, OFF_S // half_w
    tq = chunks * SG_CHUNK
    return pl.pallas_call(
        _sg_kernel,
        grid=(T // tq, halves),
        in_specs=[pl.BlockSpec((tq, half_w), lambda i, j: (i, u0 + j)),
                  pl.BlockSpec((tq, half_w), lambda i, j: (i, s0 + j)),
                  pl.BlockSpec((1, gph, SG_GROUP_DIM), lambda i, j: (j, 0, 0)),
                  pl.BlockSpec((1, gph, SG_GROUP_DIM), lambda i, j: (j, 0, 0)),
                  pl.BlockSpec((gph, SG_CHUNK, SG_CHUNK), lambda i, j: (j, 0, 0)),
                  pl.BlockSpec((1, SG_CHUNK, gph), lambda i, j: (j, 0, 0))],
        out_specs=pl.BlockSpec((tq, half_w), lambda i, j: (i, j)),
        out_shape=jax.ShapeDtypeStruct((T, SG_WIDTH), BF16),
        compiler_params=_params(("parallel", "parallel")),
        name="spatial_gating",
    )(proj, proj,
      ln_g.reshape(halves, gph, SG_GROUP_DIM), ln_b.reshape(halves, gph, SG_GROUP_DIM),
      w_s.astype(BF16),
      b_s.reshape(halves, gph, SG_CHUNK).transpose(0, 2, 1))


def _merge_kernel(att_ref, sgo_ref, ga_ref, gb_ref, wa_ref, wb_ref, o_ref, wa_bf, wb_bf):
    @pl.when(pl.program_id(1) == 0)
    def _():
        wa_bf[...] = wa_ref[...].astype(BF16)
        wb_bf[...] = wb_ref[...].astype(BF16)

    part_rows = att_ref.shape[0] // MERGE_ROW_PARTS
    parts = [slice(p * part_rows, (p + 1) * part_rows) for p in range(MERGE_ROW_PARTS)]
    a = [jnp.dot(att_ref[r, :], wa_bf[...], preferred_element_type=F32) for r in parts]
    b = [jnp.dot(sgo_ref[r, :], wb_bf[...], preferred_element_type=F32) for r in parts]
    for p, r in enumerate(parts):
        m = jax.nn.sigmoid(ga_ref[r, :]) * a[p] + jax.nn.sigmoid(gb_ref[r, :]) * b[p]
        o_ref[r, :] = m.astype(o_ref.dtype)


def _merge(att, sgo, proj, w_a, w_b, tm=1024, tn=512):
    T = att.shape[0]
    ga0 = OFF_G // tn
    gb0 = (OFF_G + D_MODEL) // tn
    return pl.pallas_call(
        _merge_kernel,
        grid=(D_MODEL // tn, T // tm),
        in_specs=[pl.BlockSpec((tm, ATT_WIDTH), lambda j, i: (i, 0)),
                  pl.BlockSpec((tm, SG_WIDTH), lambda j, i: (i, 0)),
                  pl.BlockSpec((tm, tn), lambda j, i: (i, ga0 + j)),
                  pl.BlockSpec((tm, tn), lambda j, i: (i, gb0 + j)),
                  pl.BlockSpec((ATT_WIDTH, tn), lambda j, i: (0, j)),
                  pl.BlockSpec((SG_WIDTH, tn), lambda j, i: (0, j))],
        out_specs=pl.BlockSpec((tm, tn), lambda j, i: (i, j)),
        out_shape=jax.ShapeDtypeStruct((T, D_MODEL), BF16),
        scratch_shapes=[pltpu.VMEM((ATT_WIDTH, tn), BF16),
                        pltpu.VMEM((SG_WIDTH, tn), BF16)],
        compiler_params=_params(("arbitrary", "arbitrary")),
        name="merge",
    )(att, sgo, proj, proj, w_a, w_b)


def _route(logits):
    lane = lax.broadcasted_iota(jnp.int32, logits.shape, 1)
    lane_f = lane.astype(F32)
    is_g = lane < N_GROUPS
    gl = jnp.where(is_g, logits, -jnp.inf)
    gmax = jnp.max(gl, axis=-1, keepdims=True)
    grp = jnp.min(jnp.where(gl == gmax, lane_f, float(LANES)), axis=-1, keepdims=True)
    gsum = jnp.sum(jnp.where(is_g, jnp.exp(logits - gmax), 0.0), axis=-1, keepdims=True)
    g_w = 1.0 / gsum
    e_lane = lane - N_GROUPS
    in_grp = jnp.logical_and(
        jnp.logical_and(e_lane >= 0, e_lane < N_EXPERTS),
        (e_lane // EXPERTS_PER_GROUP).astype(F32) == grp)
    el = jnp.where(in_grp, logits, -jnp.inf)
    v1 = jnp.max(el, axis=-1, keepdims=True)
    i1 = jnp.min(jnp.where(jnp.logical_and(in_grp, el == v1), lane_f, float(LANES)),
                 axis=-1, keepdims=True)
    rest = jnp.logical_and(in_grp, lane_f != i1)
    el2 = jnp.where(rest, logits, -jnp.inf)
    v2 = jnp.max(el2, axis=-1, keepdims=True)
    i2 = jnp.min(jnp.where(jnp.logical_and(rest, el2 == v2), lane_f, float(LANES)),
                 axis=-1, keepdims=True)
    e21 = jnp.exp(v2 - v1)
    w1 = g_w / (1.0 + e21)
    w2 = g_w * e21 / (1.0 + e21)
    idx = jnp.where(lane == 0, i1, i2) - float(N_GROUPS)
    wts = jnp.where(lane == 0, w1, jnp.where(lane == 1, w2, 0.0))
    return idx.astype(jnp.int32), wts


def _out_kernel(m_ref, w_ref, x_ref, g_ref, wr_ref, br_ref, h_ref, hn_ref, idx_ref, wt_ref):
    part_rows = m_ref.shape[0] // OUT_ROW_PARTS
    parts = [slice(p * part_rows, (p + 1) * part_rows) for p in range(OUT_ROW_PARTS)]
    hs = [x_ref[r, :] + jnp.dot(m_ref[r, :], w_ref[...], preferred_element_type=F32)
          for r in parts]
    for p, r in enumerate(parts):
        h = hs[p]
        h_ref[r, :] = h
        rs = lax.rsqrt(jnp.mean(h * h, axis=-1, keepdims=True) + EPS)
        hn = ((h * rs) * g_ref[...]).astype(BF16)
        bits = lax.bitcast_convert_type(hn.astype(F32), jnp.uint32)
        for c in range(HN_WORD_CHUNKS):
            hi = bits[:, c * LANES:(c + 1) * LANES]
            lo = bits[:, (c + HN_WORD_CHUNKS) * LANES:(c + HN_WORD_CHUNKS + 1) * LANES]
            hn_ref[pl.ds(p * part_rows * HN_WORD_CHUNKS + c, part_rows,
                         stride=HN_WORD_CHUNKS), :] = hi | (lo >> 16)
        logits = jnp.dot(hn, wr_ref[...], preferred_element_type=F32) + br_ref[...]
        idx, wts = _route(logits)
        idx_ref[r, :] = idx
        wt_ref[r, :] = wts


def _out_proj(merged, w_out, x, g2, w_router, b_router, tm=512):
    T, D = x.shape
    row = lambda i: (i, 0)
    const = lambda i: (0, 0)
    return pl.pallas_call(
        _out_kernel,
        grid=(T // tm,),
        in_specs=[pl.BlockSpec((tm, D), row),
                  pl.BlockSpec((D, D), const),
                  pl.BlockSpec((tm, D), row),
                  pl.BlockSpec((1, D), const),
                  pl.BlockSpec((D, LANES), const),
                  pl.BlockSpec((1, LANES), const)],
        out_specs=[pl.BlockSpec((tm, D), row),
                   pl.BlockSpec((tm * HN_WORD_CHUNKS, LANES), row),
                   pl.BlockSpec((tm, LANES), row),
                   pl.BlockSpec((tm, LANES), row)],
        out_shape=[jax.ShapeDtypeStruct((T, D), F32),
                   jax.ShapeDtypeStruct((T * HN_WORD_CHUNKS, LANES), jnp.uint32),
                   jax.ShapeDtypeStruct((T, LANES), jnp.int32),
                   jax.ShapeDtypeStruct((T, LANES), F32)],
        compiler_params=_params(("parallel",)),
        name="out_proj_router",
    )(merged, w_out.astype(BF16), x, g2.reshape(1, D), w_router, b_router)


def _expert_kernel(be_ref, eo_ref, ue_ref, nr_ref, nu_ref, *refs):
    rt_refs = refs[:ROW_SLOTS]
    (hn_hbm, wg_hbm, wu_hbm, wd_hbm, y_ref,
     xbuf, wgf, wuf, wdf, sem, wsem) = refs[ROW_SLOTS:]
    b = pl.program_id(0)
    n_used = nu_ref[0]
    n_exp = nu_ref[1]
    used = b < n_used
    slot = b % ROW_SLOTS

    def weight_copies(ordinal, slot_):
        e = ue_ref[ordinal]
        return (pltpu.make_async_copy(wg_hbm.at[e], wgf.at[slot_], wsem.at[slot_]),
                pltpu.make_async_copy(wu_hbm.at[e], wuf.at[slot_], wsem.at[slot_]),
                pltpu.make_async_copy(wd_hbm.at[e], wdf.at[slot_], wsem.at[slot_]))

    def start_weights(ordinal):
        for c in weight_copies(ordinal, ordinal % WEIGHT_SLOTS):
            c.start(priority=1)

    def issue_trips(blk):
        return (nr_ref[blk] + (ROWS_PER_ISSUE - 1)) // ROWS_PER_ISSUE

    def gather_rows(tok_ref, blk, slot_):
        def issue(g, c):
            for j in range(ROWS_PER_ISSUE):
                r = g * ROWS_PER_ISSUE + j
                src = pl.multiple_of(tok_ref[0, 0, r] * HN_WORD_CHUNKS, HN_WORD_CHUNKS)
                dst = pl.multiple_of(r * HN_WORD_CHUNKS, HN_WORD_CHUNKS)
                pltpu.make_async_copy(hn_hbm.at[pl.ds(src, HN_WORD_CHUNKS)],
                                      xbuf.at[slot_, pl.ds(dst, HN_WORD_CHUNKS)],
                                      sem.at[slot_]).start()
            return c

        lax.fori_loop(0, issue_trips(blk), issue, 0)

    @pl.when(b == 0)
    def _():
        start_weights(0)

        @pl.when(n_exp > 1)
        def _():
            start_weights(1)

        xbuf[...] = jnp.zeros(xbuf.shape, xbuf.dtype)
        gather_rows(rt_refs[0], 0, 0)
        for a in range(1, ROW_AHEAD):
            @pl.when(n_used > a)
            def _(a=a):
                gather_rows(rt_refs[a], a, a)

    @pl.when(b + ROW_AHEAD < n_used)
    def _():
        gather_rows(rt_refs[ROW_AHEAD], b + ROW_AHEAD, (b + ROW_AHEAD) % ROW_SLOTS)

    @pl.when(used)
    def _():
        first = jnp.logical_or(b == 0, be_ref[b] != be_ref[jnp.maximum(b - 1, 0)])

        ordinal = eo_ref[b]
        ws = ordinal % WEIGHT_SLOTS

        @pl.when(first)
        def _():
            for c in weight_copies(ordinal, ws):
                c.wait()

            @pl.when(ordinal + 2 < n_exp)
            def _():
                start_weights(ordinal + 2)

        words = issue_trips(b) * (ROWS_PER_ISSUE * HN_WORD_CHUNKS)
        pltpu.make_async_copy(hn_hbm.at[pl.ds(0, words)], xbuf.at[slot, pl.ds(0, words)],
                              sem.at[slot]).wait()

        halves = ([], [])
        for c in range(HN_WORD_CHUNKS):
            w = xbuf[slot, pl.ds(c, MOE_BLOCK, stride=HN_WORD_CHUNKS), :]
            hi = lax.bitcast_convert_type(w & jnp.uint32(0xFFFF0000), F32)
            lo = lax.bitcast_convert_type(w << 16, F32)
            halves[0].append(hi.astype(BF16))
            halves[1].append(lo.astype(BF16))
        x = jnp.concatenate(halves[0] + halves[1], axis=1)
        hg = jnp.dot(x, wgf[ws].astype(BF16), preferred_element_type=F32)
        hu = jnp.dot(x, wuf[ws].astype(BF16), preferred_element_type=F32)
        hdn = (jax.nn.silu(hg) * hu).astype(BF16)
        y = jnp.dot(hdn, wdf[ws].astype(BF16), preferred_element_type=F32)
        for j in range(Y_HALVES):
            for c in range(Y_HALF_CHUNKS):
                col = (j * Y_HALF_CHUNKS + c) * LANES
                y_ref[j, pl.ds(c, MOE_BLOCK, stride=Y_HALF_CHUNKS), :] = y[:, col:col + LANES]

    @pl.when(jnp.logical_not(used))
    def _():
        y_ref[...] = jnp.zeros(y_ref.shape, y_ref.dtype)


def _experts(hn, w_gate, w_up, w_down, block_e, block_ord, used_experts, block_rows, row_tok,
             n_used):
    D = w_gate.shape[1]
    n_rows = row_tok.shape[0]
    n_blocks = n_rows // MOE_BLOCK
    tok3 = row_tok.reshape(n_blocks, 1, MOE_BLOCK)
    hbm = pl.BlockSpec(memory_space=pl.ANY)
    tok_block = lambda im: pl.BlockSpec((1, 1, MOE_BLOCK), im, memory_space=pltpu.SMEM)
    grid_spec = pltpu.PrefetchScalarGridSpec(
        num_scalar_prefetch=5,
        grid=(n_blocks,),
        in_specs=[tok_block(lambda b, *_, a=a: (jnp.minimum(b + a, n_blocks - 1), 0, 0))
                  for a in range(ROW_SLOTS)] + [hbm, hbm, hbm, hbm],
        out_specs=pl.BlockSpec((Y_HALVES, MOE_BLOCK * Y_HALF_CHUNKS, LANES),
                               lambda b, *_: (0, b, 0)),
        scratch_shapes=[pltpu.VMEM((ROW_SLOTS, MOE_BLOCK * HN_WORD_CHUNKS, LANES), jnp.uint32),
                        pltpu.VMEM((WEIGHT_SLOTS, D, EXPERT_FF), F32),
                        pltpu.VMEM((WEIGHT_SLOTS, D, EXPERT_FF), F32),
                        pltpu.VMEM((WEIGHT_SLOTS, EXPERT_FF, D), F32),
                        pltpu.SemaphoreType.DMA((ROW_SLOTS,)),
                        pltpu.SemaphoreType.DMA((WEIGHT_SLOTS,))],
    )
    return pl.pallas_call(
        _expert_kernel,
        grid_spec=grid_spec,
        out_shape=jax.ShapeDtypeStruct((Y_HALVES, n_rows * Y_HALF_CHUNKS, LANES), F32),
        compiler_params=_params(("arbitrary",)),
        name="experts",
    )(block_e, block_ord, used_experts, block_rows, n_used, *([tok3] * ROW_SLOTS),
      hn, w_gate, w_up, w_down)


def _combine_kernel(*refs, tc):
    dest_refs = refs[:COMBINE_SLOTS]
    y_hbm, h_ref, wt_ref, o_ref, ybuf, sem = refs[COMBINE_SLOTS:]
    i = pl.program_id(0)
    slot = i % COMBINE_SLOTS

    def gather_rows(dest_ref, slot_):
        def issue(r, c):
            dst = pl.multiple_of(r * Y_HALF_CHUNKS, Y_HALF_CHUNKS)
            for k in range(TOP_K):
                src = pl.multiple_of(dest_ref[0, 0, r * TOP_K + k] * Y_HALF_CHUNKS, Y_HALF_CHUNKS)
                for j in range(Y_HALVES):
                    pltpu.make_async_copy(y_hbm.at[j, pl.ds(src, Y_HALF_CHUNKS)],
                                          ybuf.at[slot_, k, j, pl.ds(dst, Y_HALF_CHUNKS)],
                                          sem.at[slot_]).start(priority=j % 2)
            return c

        lax.fori_loop(0, tc, issue, 0, unroll=4)

    @pl.when(i == 0)
    def _():
        for a in range(COMBINE_AHEAD):
            gather_rows(dest_refs[a], a)

    @pl.when(i + COMBINE_AHEAD < pl.num_programs(0))
    def _():
        gather_rows(dest_refs[COMBINE_AHEAD], (i + COMBINE_AHEAD) % COMBINE_SLOTS)

    for k in range(TOP_K):
        for j in range(Y_HALVES):
            pltpu.make_async_copy(y_hbm.at[j, pl.ds(0, tc * Y_HALF_CHUNKS)], ybuf.at[slot, k, j],
                                  sem.at[slot]).wait()
    wt = wt_ref[...]
    w1, w2 = wt[:, 0:1], wt[:, 1:2]
    for j in range(Y_HALVES):
        for c in range(Y_HALF_CHUNKS):
            col = (j * Y_HALF_CHUNKS + c) * LANES
            rows = pl.ds(c, tc, stride=Y_HALF_CHUNKS)
            o_ref[:, col:col + LANES] = h_ref[:, col:col + LANES] + (
                w1 * ybuf[slot, 0, j, rows, :] + w2 * ybuf[slot, 1, j, rows, :])


def _combine(yrows, h, wts, dest, tc=128):
    T, D = h.shape
    steps = T // tc
    row = lambda i: (i, 0)
    dest3 = dest.reshape(steps, 1, tc * TOP_K)
    dest_block = lambda im: pl.BlockSpec((1, 1, tc * TOP_K), im, memory_space=pltpu.SMEM)
    return pl.pallas_call(
        functools.partial(_combine_kernel, tc=tc),
        grid=(steps,),
        in_specs=[dest_block(lambda i, a=a: (jnp.minimum(i + a, steps - 1), 0, 0))
                  for a in range(COMBINE_SLOTS)]
                 + [pl.BlockSpec(memory_space=pl.ANY),
                    pl.BlockSpec((tc, D), row),
                    pl.BlockSpec((tc, LANES), row)],
        out_specs=pl.BlockSpec((tc, D), row),
        out_shape=jax.ShapeDtypeStruct((T, D), F32),
        scratch_shapes=[pltpu.VMEM((COMBINE_SLOTS, TOP_K, Y_HALVES, tc * Y_HALF_CHUNKS, LANES), F32),
                        pltpu.SemaphoreType.DMA((COMBINE_SLOTS,))],
        compiler_params=_params(("arbitrary",)),
        name="combine",
    )(*([dest3] * COMBINE_SLOTS), yrows, h, wts)


SUBLANES = 8
META_ROWS = 256

def _lane_cumsum(x):
    lane = lax.broadcasted_iota(jnp.int32, x.shape, 1)
    s = 1
    while s < LANES:
        x = x + jnp.where(lane >= s, pltpu.roll(x, s, 1), 0)
        s *= 2
    return x


def _dispatch_kernel(idx_ref, dest_ref, meta_ref, run_ref, prefix_ref, start_ref, *, tb):
    p = pl.program_id(0)
    i = pl.program_id(1)
    idx = idx_ref[...]
    lane = lax.broadcasted_iota(jnp.int32, idx.shape, 1)
    e1 = idx[:, 0:1]
    e2 = idx[:, 1:2]
    onehot = jnp.where(jnp.logical_or(lane == e1, lane == e2), 1.0, 0.0)

    @pl.when(jnp.logical_and(p == 0, i == 0))
    def _():
        run_ref[...] = jnp.zeros(run_ref.shape, F32)

    @pl.when(p == 0)
    def _():
        prefix_ref[i] = run_ref[...]
        run_ref[...] = run_ref[...] + jnp.sum(onehot, axis=0, keepdims=True)

    @pl.when(jnp.logical_and(p == 1, i == 0))
    def _():
        counts = run_ref[...].astype(jnp.int32)
        nblk = (counts + (MOE_BLOCK - 1)) // MOE_BLOCK
        end_blk = _lane_cumsum(nblk)
        start_ref[...] = ((end_blk - nblk) * MOE_BLOCK).astype(F32)
        has = jnp.where(counts > 0, 1, 0)
        ordinal = _lane_cumsum(has) - 1
        start_blk, counts = (end_blk - nblk)[0:1], counts[0:1]
        end_blk, has, ordinal = end_blk[0:1], has[0:1], ordinal[0:1]
        rows = lax.broadcasted_iota(jnp.int32, (META_ROWS, LANES), 0)
        lanes = lax.broadcasted_iota(jnp.int32, (META_ROWS, LANES), 1)
        is_e = lanes < N_EXPERTS
        rsum = lambda v: jnp.sum(v, axis=-1, keepdims=True)
        be = rsum(jnp.where(jnp.logical_and(is_e, end_blk <= rows), 1, 0))
        be = jnp.minimum(be, N_EXPERTS - 1)
        eo = rsum(jnp.where(lanes == be, ordinal, 0))
        ue = rsum(jnp.where(jnp.logical_and(has > 0, ordinal == rows), lanes, 0))
        n_blk = rsum(jnp.where(lanes == N_EXPERTS - 1, end_blk, 0))
        n_exp = rsum(jnp.where(is_e, has, 0))
        own = lanes == be
        left = rsum(jnp.where(own, counts, 0)) - MOE_BLOCK * (rows - rsum(jnp.where(own, start_blk, 0)))
        n_valid = jnp.clip(left, 0, MOE_BLOCK)
        meta_ref[...] = jnp.where(
            lanes == 0, be, jnp.where(lanes == 1, eo, jnp.where(
                lanes == 2, ue, jnp.where(lanes == 3, n_blk, jnp.where(lanes == 4, n_exp, n_valid)))))

    @pl.when(p == 1)
    def _():
        r = lax.broadcasted_iota(jnp.int32, (tb, tb), 0)
        c = lax.broadcasted_iota(jnp.int32, (tb, tb), 1)
        earlier = jnp.where(c < r, 1.0, 0.0).astype(BF16)
        rank = jnp.dot(earlier, onehot.astype(BF16), preferred_element_type=F32)
        rank = rank + prefix_ref[i][0:1] + start_ref[0:1]
        d1 = jnp.sum(jnp.where(lane == e1, rank, 0.0), axis=-1, keepdims=True)
        d2 = jnp.sum(jnp.where(lane == e2, rank, 0.0), axis=-1, keepdims=True)
        dest_ref[...] = jnp.where(lane == 0, d1, d2).astype(jnp.int32)


def _dispatch(idx, tb=512):
    T = idx.shape[0]
    n_rows = T * TOP_K + N_EXPERTS * MOE_BLOCK
    n_blocks = n_rows // MOE_BLOCK
    assert n_blocks <= META_ROWS
    dest2, meta = pl.pallas_call(
        functools.partial(_dispatch_kernel, tb=tb),
        grid=(2, T // tb),
        in_specs=[pl.BlockSpec((tb, LANES), lambda p, i: (i, 0))],
        out_specs=[pl.BlockSpec((tb, LANES), lambda p, i: (i * p, 0)),
                   pl.BlockSpec((META_ROWS, LANES), lambda p, i: (0, 0))],
        out_shape=[jax.ShapeDtypeStruct((T, LANES), jnp.int32),
                   jax.ShapeDtypeStruct((META_ROWS, LANES), jnp.int32)],
        scratch_shapes=[pltpu.VMEM((SUBLANES, LANES), F32),
                        pltpu.VMEM((T // tb, SUBLANES, LANES), F32),
                        pltpu.VMEM((SUBLANES, LANES), F32)],
        compiler_params=_params(("arbitrary", "arbitrary")),
        name="dispatch",
    )(idx)
    dest = dest2[:, :TOP_K].reshape(T * TOP_K)
    tok = jnp.repeat(jnp.arange(T, dtype=jnp.int32), TOP_K)
    row_tok = jnp.zeros((n_rows,), jnp.int32).at[dest].set(tok)
    block_e = meta[:n_blocks, 0]
    block_ord = meta[:n_blocks, 1]
    used_experts = meta[:N_EXPERTS, 2]
    n_used = meta[0, 3:5]
    block_rows = meta[:n_blocks, 5]
    return block_e, block_ord, used_experts, block_rows, row_tok, n_used, dest


def kernel(x, positions, norm1_g, w_in, q_norm_g, k_norm_g, sink_logits, sg_ln_g, sg_ln_b, sg_w, sg_b, w_branch_att, w_branch_sg, w_out, norm2_g, w_group_router, b_group_router, w_expert_router, b_expert_router, w_gate, w_up, w_down):
    B, S, D = x.shape
    T = B * S
    h = x.reshape(T, D)
    pos = positions.reshape(T)
    for l in range(norm1_g.shape[0]):
        xn = _rmsnorm(h, norm1_g[l])
        proj = _in_proj(xn, w_in[l])
        q, k, v = _qkv_prep(proj, pos, q_norm_g[l], k_norm_g[l])
        att = _attention(q, k, v, sink_logits[l], B)
        sgo = _spatial_gating(proj, sg_ln_g[l], sg_ln_b[l], sg_w[l], sg_b[l])
        merged = _merge(att, sgo, proj, w_branch_att[l], w_branch_sg[l])
        pad = LANES - N_GROUPS - N_EXPERTS
        w_router = jnp.concatenate(
            [w_group_router[l], w_expert_router[l], jnp.zeros((D, pad), F32)], axis=1).astype(BF16)
        b_router = jnp.concatenate(
            [b_group_router[l], b_expert_router[l], jnp.zeros((pad,), F32)]).reshape(1, LANES)
        h, hn, idx, wts = _out_proj(merged, w_out[l], h, norm2_g[l], w_router, b_router)
        block_e, block_ord, used_experts, block_rows, row_tok, n_used, dest = _dispatch(idx)
        yrows = _experts(hn, w_gate[l], w_up[l], w_down[l],
                         block_e, block_ord, used_experts, block_rows, row_tok, n_used)
        h = _combine(yrows, h, wts, dest)
    return h.reshape(B, S, D)
```

```python
import functools

import jax
import jax.numpy as jnp
from jax import lax
from jax.experimental import pallas as pl
from jax.experimental.pallas import tpu as pltpu

F32 = jnp.float32
BF16 = jnp.bfloat16

D_MODEL = 2048
HEAD_DIM = 64
ATT_WIDTH = D_MODEL // 2
ATT_HEADS = ATT_WIDTH // HEAD_DIM
ATT_KV_HEADS = ATT_HEADS // 4
Q_PER_KV = ATT_HEADS // ATT_KV_HEADS
KV_WIDTH = ATT_KV_HEADS * HEAD_DIM
WINDOW = 128
ATT_BLOCK = 128
ROPE_DIM = HEAD_DIM // 4
ROPE_HALF = ROPE_DIM // 2
ROPE_THETA = 500000.0
SG_WIDTH = D_MODEL // 2
SG_GROUP_DIM = 128
SG_GROUPS = SG_WIDTH // SG_GROUP_DIM
SG_CHUNK = 128
OFF_Q = 0
OFF_K = OFF_Q + ATT_WIDTH
OFF_V = OFF_K + KV_WIDTH
OFF_U = OFF_V + KV_WIDTH
OFF_S = OFF_U + SG_WIDTH
OFF_G = OFF_S + SG_WIDTH
IN_COLS = OFF_G + 2 * D_MODEL
N_GROUPS = 8
EXPERTS_PER_GROUP = 8
N_EXPERTS = N_GROUPS * EXPERTS_PER_GROUP
TOP_K = 2
EXPERT_FF = D_MODEL // 4
MOE_BLOCK = 128
EPS = 1e-6
NEG_INF = -1e30

LANES = 128
HN_WORD_CHUNKS = D_MODEL // LANES // 2
ROWS_PER_ISSUE = 8
ROW_AHEAD = 4
ROW_SLOTS = ROW_AHEAD + 1
MERGE_ROW_PARTS = 4
OUT_ROW_PARTS = 4
WEIGHT_SLOTS = 3
COMBINE_AHEAD = 1
COMBINE_SLOTS = COMBINE_AHEAD + 1
VMEM_LIMIT = 56 * 1024 * 1024


def _params(sem, vmem=VMEM_LIMIT):
    return pltpu.CompilerParams(dimension_semantics=sem, vmem_limit_bytes=vmem)


def _rmsnorm_kernel(x_ref, g_ref, o_ref):
    x = x_ref[...]
    r = lax.rsqrt(jnp.mean(x * x, axis=-1, keepdims=True) + EPS)
    o_ref[...] = ((x * r) * g_ref[...]).astype(o_ref.dtype)


def _rmsnorm(x, g, tm=512):
    T, D = x.shape
    return pl.pallas_call(
        _rmsnorm_kernel,
        grid=(T // tm,),
        in_specs=[pl.BlockSpec((tm, D), lambda i: (i, 0)),
                  pl.BlockSpec((1, D), lambda i: (0, 0))],
        out_specs=pl.BlockSpec((tm, D), lambda i: (i, 0)),
        out_shape=jax.ShapeDtypeStruct((T, D), BF16),
        compiler_params=_params(("parallel",)),
        name="norm1",
    )(x, g.reshape(1, D))


def _proj_kernel(x_ref, w_ref, o_ref, wbf_ref):
    @pl.when(pl.program_id(1) == 0)
    def _():
        wbf_ref[...] = w_ref[...].astype(BF16)

    o_ref[...] = jnp.dot(x_ref[...], wbf_ref[...], preferred_element_type=F32)


def _in_proj(xn, w, tm=1024, tn=1280):
    T, D = xn.shape
    N = w.shape[1]
    return pl.pallas_call(
        _proj_kernel,
        grid=(N // tn, T // tm),
        in_specs=[pl.BlockSpec((tm, D), lambda j, i: (i, 0)),
                  pl.BlockSpec((D, tn), lambda j, i: (0, j))],
        out_specs=pl.BlockSpec((tm, tn), lambda j, i: (i, j)),
        out_shape=jax.ShapeDtypeStruct((T, N), F32),
        scratch_shapes=[pltpu.VMEM((D, tn), BF16)],
        compiler_params=_params(("arbitrary", "arbitrary")),
        name="in_proj",
    )(xn, w)


def _rope_table_kernel(pos_ref, invf_ref, cos_ref, sin_ref):
    ang = pos_ref[...].astype(F32) * invf_ref[...]
    cos_ref[...] = jnp.cos(ang)
    sin_ref[...] = jnp.sin(ang)


def _rope_tables(positions):
    T = positions.shape[0]
    rows = T * ROPE_HALF // LANES
    inv = ROPE_THETA ** (-jnp.arange(0, ROPE_DIM, 2, dtype=F32) / ROPE_DIM)
    invf = jnp.tile(inv, LANES // ROPE_HALF).reshape(1, LANES)
    pos = jnp.repeat(positions, ROPE_HALF).reshape(rows, LANES)
    whole = lambda: (0, 0)
    cos, sin = pl.pallas_call(
        _rope_table_kernel,
        in_specs=[pl.BlockSpec((rows, LANES), whole), pl.BlockSpec((1, LANES), whole)],
        out_specs=[pl.BlockSpec((rows, LANES), whole), pl.BlockSpec((rows, LANES), whole)],
        out_shape=[jax.ShapeDtypeStruct((rows, LANES), F32)] * 2,
        name="rope_tables",
    )(pos, invf)
    return cos.reshape(T, ROPE_HALF), sin.reshape(T, ROPE_HALF)


def _qkv_prep_kernel(p_ref, cos_ref, sin_ref, gq_ref, gk_ref, seg_ref, q_ref, k_ref, v_ref):
    cos = cos_ref[...]
    sin = sin_ref[...]
    lane = lax.broadcasted_iota(jnp.int32, cos.shape, 1)
    first_half = (lane % HEAD_DIM) < ROPE_HALF

    def norm_rope(x, g):
        x2 = x * x
        x2_hi = x2.astype(BF16)
        x2_lo = (x2 - x2_hi.astype(F32)).astype(BF16)
        both = jnp.dot(jnp.concatenate([x2_hi, x2_lo], axis=0), seg_ref[...],
                       preferred_element_type=F32)
        ssq = both[:x.shape[0]] + both[x.shape[0]:]
        xn = (x * lax.rsqrt(ssq * (1.0 / HEAD_DIM) + EPS)) * g
        partner = jnp.where(first_half,
                            pltpu.roll(xn, LANES - ROPE_HALF, 1),
                            pltpu.roll(xn, ROPE_HALF, 1))
        return xn * cos + partner * sin

    for c in range(ATT_WIDTH // LANES):
        x = p_ref[:, OFF_Q + c * LANES:OFF_Q + (c + 1) * LANES]
        q_ref[:, c * LANES:(c + 1) * LANES] = (
            norm_rope(x, gq_ref[...]) * (HEAD_DIM ** -0.5)).astype(q_ref.dtype)
    for c in range(KV_WIDTH // LANES):
        x = p_ref[:, OFF_K + c * LANES:OFF_K + (c + 1) * LANES]
        k_ref[:, c * LANES:(c + 1) * LANES] = norm_rope(x, gk_ref[...]).astype(k_ref.dtype)
    v_ref[...] = p_ref[:, OFF_V:OFF_V + KV_WIDTH].astype(v_ref.dtype)


def _qkv_prep(proj, positions, q_g, k_g, tq=256):
    T = proj.shape[0]
    width = OFF_U
    cos8, sin8 = _rope_tables(positions)
    rest = HEAD_DIM - ROPE_DIM
    cos_t = jnp.tile(jnp.concatenate([cos8, cos8, jnp.ones((T, rest), F32)], axis=1),
                     (1, LANES // HEAD_DIM))
    sin_t = jnp.tile(jnp.concatenate([-sin8, sin8, jnp.zeros((T, rest), F32)], axis=1),
                     (1, LANES // HEAD_DIM))
    head_of_lane = jnp.arange(LANES) // HEAD_DIM
    same_head = (head_of_lane[:, None] == head_of_lane[None, :]).astype(BF16)
    gq = jnp.tile(q_g, LANES // HEAD_DIM).reshape(1, LANES)
    gk = jnp.tile(k_g, LANES // HEAD_DIM).reshape(1, LANES)
    row = lambda i: (i, 0)
    const = lambda i: (0, 0)
    return pl.pallas_call(
        _qkv_prep_kernel,
        grid=(T // tq,),
        in_specs=[pl.BlockSpec((tq, width), row),
                  pl.BlockSpec((tq, LANES), row),
                  pl.BlockSpec((tq, LANES), row),
                  pl.BlockSpec((1, LANES), const),
                  pl.BlockSpec((1, LANES), const),
                  pl.BlockSpec((LANES, LANES), const)],
        out_specs=[pl.BlockSpec((tq, ATT_WIDTH), row),
                   pl.BlockSpec((tq, KV_WIDTH), row),
                   pl.BlockSpec((tq, KV_WIDTH), row)],
        out_shape=[jax.ShapeDtypeStruct((T, ATT_WIDTH), BF16),
                   jax.ShapeDtypeStruct((T, KV_WIDTH), BF16),
                   jax.ShapeDtypeStruct((T, KV_WIDTH), BF16)],
        compiler_params=_params(("parallel",)),
        name="qkv_prep",
    )(proj, cos_t, sin_t, gq, gk, same_head)


def _attn_kernel(sink_ref, q_ref, kp_ref, kc_ref, kn_ref, vp_ref, vc_ref, vn_ref, o_ref, *, nb):
    n = pl.program_id(1)
    rows = ATT_BLOCK
    qi = lax.broadcasted_iota(jnp.int32, (rows, ATT_BLOCK), 0) % ATT_BLOCK
    kj = lax.broadcasted_iota(jnp.int32, (rows, ATT_BLOCK), 1)
    lo_prev = jnp.where(n > 0, 0, ATT_BLOCK)
    hi_next = jnp.where(n < nb - 1, 0, -ATT_BLOCK)
    cap_prev = jnp.where(kj - qi >= lo_prev, jnp.inf, NEG_INF)
    cap_next = jnp.where(kj - qi <= hi_next, jnp.inf, NEG_INF)
    cap = jnp.concatenate([cap_prev, jnp.full((rows, ATT_BLOCK), jnp.inf, F32), cap_next], axis=1)

    kvhs = range(ATT_KV_HEADS)
    scores = []
    for kvh in kvhs:
        cols = slice(kvh * HEAD_DIM, (kvh + 1) * HEAD_DIM)
        k = jnp.concatenate([kp_ref[:, cols], kc_ref[:, cols], kn_ref[:, cols]], axis=0)
        q = jnp.concatenate(
            [q_ref[:, (kvh * Q_PER_KV + g) * HEAD_DIM:(kvh * Q_PER_KV + g + 1) * HEAD_DIM]
             for g in range(Q_PER_KV)], axis=0)
        scores.append(lax.dot_general(q, k, (((1,), (1,)), ((), ())),
                                      preferred_element_type=F32))
    probs = []
    for kvh in kvhs:
        strips = []
        for g in range(Q_PER_KV):
            sink = sink_ref[kvh * Q_PER_KV + g]
            sg = jnp.minimum(scores[kvh][g * ATT_BLOCK:(g + 1) * ATT_BLOCK], cap)
            m = jnp.maximum(jnp.max(sg, axis=-1, keepdims=True), sink)
            e = jnp.exp(sg - m)
            denom = jnp.sum(e, axis=-1, keepdims=True) + jnp.exp(sink - m)
            strips.append((e / denom).astype(BF16))
        probs.append(jnp.concatenate(strips, axis=0))
    for kvh in kvhs:
        cols = slice(kvh * HEAD_DIM, (kvh + 1) * HEAD_DIM)
        v = jnp.concatenate([vp_ref[:, cols], vc_ref[:, cols], vn_ref[:, cols]], axis=0)
        o = jnp.dot(probs[kvh], v, preferred_element_type=F32)
        for g in range(Q_PER_KV):
            h = kvh * Q_PER_KV + g
            o_ref[:, h * HEAD_DIM:(h + 1) * HEAD_DIM] = (
                o[g * ATT_BLOCK:(g + 1) * ATT_BLOCK].astype(o_ref.dtype))


def _attention(q, k, v, sink, batch):
    T = q.shape[0]
    nb = T // batch // ATT_BLOCK
    cur = lambda b, n: (b * nb + n, 0)
    prev = lambda b, n: (b * nb + jnp.maximum(n - 1, 0), 0)
    nxt = lambda b, n: (b * nb + jnp.minimum(n + 1, nb - 1), 0)
    kv = lambda im: pl.BlockSpec((ATT_BLOCK, KV_WIDTH), im)
    return pl.pallas_call(
        functools.partial(_attn_kernel, nb=nb),
        grid=(batch, nb),
        in_specs=[pl.BlockSpec(memory_space=pltpu.SMEM),
                  pl.BlockSpec((ATT_BLOCK, ATT_WIDTH), cur),
                  kv(prev), kv(cur), kv(nxt), kv(prev), kv(cur), kv(nxt)],
        out_specs=pl.BlockSpec((ATT_BLOCK, ATT_WIDTH), cur),
        out_shape=jax.ShapeDtypeStruct((T, ATT_WIDTH), BF16),
        compiler_params=_params(("parallel", "parallel")),
        name="window_attn",
    )(sink, q, k, k, k, v, v, v)


def _sg_kernel(u_ref, s_ref, lng_ref, lnb_ref, w_ref, b_ref, o_ref):
    groups = w_ref.shape[0]

    def one_chunk(ci, carry):
        rows = pl.ds(pl.multiple_of(ci * SG_CHUNK, SG_CHUNK), SG_CHUNK)
        gs = range(groups)
        cols = [slice(gi * SG_GROUP_DIM, (gi + 1) * SG_GROUP_DIM) for gi in gs]
        s = [jax.nn.gelu(s_ref[rows, cols[gi]]) for gi in gs]
        sc = [s[gi] - jnp.mean(s[gi], axis=-1, keepdims=True) for gi in gs]
        var = [jnp.mean(sc[gi] * sc[gi], axis=-1, keepdims=True) for gi in gs]
        sn = [(sc[gi] * lax.rsqrt(var[gi] + EPS)) * lng_ref[0, gi:gi + 1, :]
              + lnb_ref[0, gi:gi + 1, :] for gi in gs]
        mixed = [jnp.dot(w_ref[gi], sn[gi].astype(BF16), preferred_element_type=F32)
                 + b_ref[0, :, gi:gi + 1] for gi in gs]
        for gi in gs:
            o_ref[rows, cols[gi]] = (jax.nn.gelu(u_ref[rows, cols[gi]]) * mixed[gi]).astype(o_ref.dtype)
        return carry

    lax.fori_loop(0, u_ref.shape[0] // SG_CHUNK, one_chunk, 0)


def _spatial_gating(proj, ln_g, ln_b, w_s, b_s, halves=2, chunks=4):
    T = proj.shape[0]
    half_w = SG_WIDTH // halves
    gph = SG_GROUPS // halves
    u0, s0 = OFF_U // half_w, OFF_S // half_w
    tq = chunks * SG_CHUNK
    return pl.pallas_call(
        _sg_kernel,
        grid=(T // tq, halves),
        in_specs=[pl.BlockSpec((tq, half_w), lambda i, j: (i, u0 + j)),
                  pl.BlockSpec((tq, half_w), lambda i, j: (i, s0 + j)),
                  pl.BlockSpec((1, gph, SG_GROUP_DIM), lambda i, j: (j, 0, 0)),
                  pl.BlockSpec((1, gph, SG_GROUP_DIM), lambda i, j: (j, 0, 0)),
                  pl.BlockSpec((gph, SG_CHUNK, SG_CHUNK), lambda i, j: (j, 0, 0)),
                  pl.BlockSpec((1, SG_CHUNK, gph), lambda i, j: (j, 0, 0))],
        out_specs=pl.BlockSpec((tq, half_w), lambda i, j: (i, j)),
        out_shape=jax.ShapeDtypeStruct((T, SG_WIDTH), BF16),
        compiler_params=_params(("parallel", "parallel")),
        name="spatial_gating",
    )(proj, proj,
      ln_g.reshape(halves, gph, SG_GROUP_DIM), ln_b.reshape(halves, gph, SG_GROUP_DIM),
      w_s.astype(BF16),
      b_s.reshape(halves, gph, SG_CHUNK).transpose(0, 2, 1))


def _merge_kernel(att_ref, sgo_ref, ga_ref, gb_ref, wa_ref, wb_ref, o_ref, wa_bf, wb_bf):
    @pl.when(pl.program_id(1) == 0)
    def _():
        wa_bf[...] = wa_ref[...].astype(BF16)
        wb_bf[...] = wb_ref[...].astype(BF16)

    part_rows = att_ref.shape[0] // MERGE_ROW_PARTS
    parts = [slice(p * part_rows, (p + 1) * part_rows) for p in range(MERGE_ROW_PARTS)]
    a = [jnp.dot(att_ref[r, :], wa_bf[...], preferred_element_type=F32) for r in parts]
    b = [jnp.dot(sgo_ref[r, :], wb_bf[...], preferred_element_type=F32) for r in parts]
    for p, r in enumerate(parts):
        m = jax.nn.sigmoid(ga_ref[r, :]) * a[p] + jax.nn.sigmoid(gb_ref[r, :]) * b[p]
        o_ref[r, :] = m.astype(o_ref.dtype)


def _merge(att, sgo, proj, w_a, w_b, tm=1024, tn=512):
    T = att.shape[0]
    ga0 = OFF_G // tn
    gb0 = (OFF_G + D_MODEL) // tn
    return pl.pallas_call(
        _merge_kernel,
        grid=(D_MODEL // tn, T // tm),
        in_specs=[pl.BlockSpec((tm, ATT_WIDTH), lambda j, i: (i, 0)),
                  pl.BlockSpec((tm, SG_WIDTH), lambda j, i: (i, 0)),
                  pl.BlockSpec((tm, tn), lambda j, i: (i, ga0 + j)),
                  pl.BlockSpec((tm, tn), lambda j, i: (i, gb0 + j)),
                  pl.BlockSpec((ATT_WIDTH, tn), lambda j, i: (0, j)),
                  pl.BlockSpec((SG_WIDTH, tn), lambda j, i: (0, j))],
        out_specs=pl.BlockSpec((tm, tn), lambda j, i: (i, j)),
        out_shape=jax.ShapeDtypeStruct((T, D_MODEL), BF16),
        scratch_shapes=[pltpu.VMEM((ATT_WIDTH, tn), BF16),
                        pltpu.VMEM((SG_WIDTH, tn), BF16)],
        compiler_params=_params(("arbitrary", "arbitrary")),
        name="merge",
    )(att, sgo, proj, proj, w_a, w_b)


def _route(logits):
    lane = lax.broadcasted_iota(jnp.int32, logits.shape, 1)
    lane_f = lane.astype(F32)
    is_g = lane < N_GROUPS
    gl = jnp.where(is_g, logits, -jnp.inf)
    gmax = jnp.max(gl, axis=-1, keepdims=True)
    grp = jnp.min(jnp.where(gl == gmax, lane_f, float(LANES)), axis=-1, keepdims=True)
    gsum = jnp.sum(jnp.where(is_g, jnp.exp(logits - gmax), 0.0), axis=-1, keepdims=True)
    g_w = 1.0 / gsum
    e_lane = lane - N_GROUPS
    in_grp = jnp.logical_and(
        jnp.logical_and(e_lane >= 0, e_lane < N_EXPERTS),
        (e_lane // EXPERTS_PER_GROUP).astype(F32) == grp)
    el = jnp.where(in_grp, logits, -jnp.inf)
    v1 = jnp.max(el, axis=-1, keepdims=True)
    i1 = jnp.min(jnp.where(jnp.logical_and(in_grp, el == v1), lane_f, float(LANES)),
                 axis=-1, keepdims=True)
    rest = jnp.logical_and(in_grp, lane_f != i1)
    el2 = jnp.where(rest, logits, -jnp.inf)
    v2 = jnp.max(el2, axis=-1, keepdims=True)
    i2 = jnp.min(jnp.where(jnp.logical_and(rest, el2 == v2), lane_f, float(LANES)),
                 axis=-1, keepdims=True)
    e21 = jnp.exp(v2 - v1)
    w1 = g_w / (1.0 + e21)
    w2 = g_w * e21 / (1.0 + e21)
    idx = jnp.where(lane == 0, i1, i2) - float(N_GROUPS)
    wts = jnp.where(lane == 0, w1, jnp.where(lane == 1, w2, 0.0))
    return idx.astype(jnp.int32), wts


def _out_kernel(m_ref, w_ref, x_ref, g_ref, wr_ref, br_ref, h_ref, hn_ref, idx_ref, wt_ref):
    part_rows = m_ref.shape[0] // OUT_ROW_PARTS
    parts = [slice(p * part_rows, (p + 1) * part_rows) for p in range(OUT_ROW_PARTS)]
    hs = [x_ref[r, :] + jnp.dot(m_ref[r, :], w_ref[...], preferred_element_type=F32)
          for r in parts]
    for p, r in enumerate(parts):
        h = hs[p]
        h_ref[r, :] = h
        rs = lax.rsqrt(jnp.mean(h * h, axis=-1, keepdims=True) + EPS)
        hn = ((h * rs) * g_ref[...]).astype(BF16)
        bits = lax.bitcast_convert_type(hn.astype(F32), jnp.uint32)
        for c in range(HN_WORD_CHUNKS):
            hi = bits[:, c * LANES:(c + 1) * LANES]
            lo = bits[:, (c + HN_WORD_CHUNKS) * LANES:(c + HN_WORD_CHUNKS + 1) * LANES]
            hn_ref[pl.ds(p * part_rows * HN_WORD_CHUNKS + c, part_rows,
                         stride=HN_WORD_CHUNKS), :] = hi | (lo >> 16)
        logits = jnp.dot(hn, wr_ref[...], preferred_element_type=F32) + br_ref[...]
        idx, wts = _route(logits)
        idx_ref[r, :] = idx
        wt_ref[r, :] = wts


def _out_proj(merged, w_out, x, g2, w_router, b_router, tm=512):
    T, D = x.shape
    row = lambda i: (i, 0)
    const = lambda i: (0, 0)
    return pl.pallas_call(
        _out_kernel,
        grid=(T // tm,),
        in_specs=[pl.BlockSpec((tm, D), row),
                  pl.BlockSpec((D, D), const),
                  pl.BlockSpec((tm, D), row),
                  pl.BlockSpec((1, D), const),
                  pl.BlockSpec((D, LANES), const),
                  pl.BlockSpec((1, LANES), const)],
        out_specs=[pl.BlockSpec((tm, D), row),
                   pl.BlockSpec((tm * HN_WORD_CHUNKS, LANES), row),
                   pl.BlockSpec((tm, LANES), row),
                   pl.BlockSpec((tm, LANES), row)],
        out_shape=[jax.ShapeDtypeStruct((T, D), F32),
                   jax.ShapeDtypeStruct((T * HN_WORD_CHUNKS, LANES), jnp.uint32),
                   jax.ShapeDtypeStruct((T, LANES), jnp.int32),
                   jax.ShapeDtypeStruct((T, LANES), F32)],
        compiler_params=_params(("parallel",)),
        name="out_proj_router",
    )(merged, w_out.astype(BF16), x, g2.reshape(1, D), w_router, b_router)


def _expert_kernel(be_ref, eo_ref, ue_ref, nr_ref, nu_ref, *refs):
    rt_refs = refs[:ROW_SLOTS]
    (hn_hbm, wg_hbm, wu_hbm, wd_hbm, y_ref,
     xbuf, wgf, wuf, wdf, sem, wsem) = refs[ROW_SLOTS:]
    b = pl.program_id(0)
    n_used = nu_ref[0]
    n_exp = nu_ref[1]
    used = b < n_used
    slot = b % ROW_SLOTS

    def weight_copies(ordinal, slot_):
        e = ue_ref[ordinal]
        return (pltpu.make_async_copy(wg_hbm.at[e], wgf.at[slot_], wsem.at[slot_]),
                pltpu.make_async_copy(wu_hbm.at[e], wuf.at[slot_], wsem.at[slot_]),
                pltpu.make_async_copy(wd_hbm.at[e], wdf.at[slot_], wsem.at[slot_]))

    def start_weights(ordinal):
        for c in weight_copies(ordinal, ordinal % WEIGHT_SLOTS):
            c.start(priority=1)

    def issue_trips(blk):
        return (nr_ref[blk] + (ROWS_PER_ISSUE - 1)) // ROWS_PER_ISSUE

    def gather_rows(tok_ref, blk, slot_):
        def issue(g, c):
            for j in range(ROWS_PER_ISSUE):
                r = g * ROWS_PER_ISSUE + j
                src = pl.multiple_of(tok_ref[0, 0, r] * HN_WORD_CHUNKS, HN_WORD_CHUNKS)
                dst = pl.multiple_of(r * HN_WORD_CHUNKS, HN_WORD_CHUNKS)
                pltpu.make_async_copy(hn_hbm.at[pl.ds(src, HN_WORD_CHUNKS)],
                                      xbuf.at[slot_, pl.ds(dst, HN_WORD_CHUNKS)],
                                      sem.at[slot_]).start()
            return c

        lax.fori_loop(0, issue_trips(blk), issue, 0)

    @pl.when(b == 0)
    def _():
        start_weights(0)

        @pl.when(n_exp > 1)
        def _():
            start_weights(1)

        xbuf[...] = jnp.zeros(xbuf.shape, xbuf.dtype)
        gather_rows(rt_refs[0], 0, 0)
        for a in range(1, ROW_AHEAD):
            @pl.when(n_used > a)
            def _(a=a):
                gather_rows(rt_refs[a], a, a)

    @pl.when(b + ROW_AHEAD < n_used)
    def _():
        gather_rows(rt_refs[ROW_AHEAD], b + ROW_AHEAD, (b + ROW_AHEAD) % ROW_SLOTS)

    @pl.when(used)
    def _():
        first = jnp.logical_or(b == 0, be_ref[b] != be_ref[jnp.maximum(b - 1, 0)])

        ordinal = eo_ref[b]
        ws = ordinal % WEIGHT_SLOTS

        @pl.when(first)
        def _():
            for c in weight_copies(ordinal, ws):
                c.wait()

            @pl.when(ordinal + 2 < n_exp)
            def _():
                start_weights(ordinal + 2)

        words = issue_trips(b) * (ROWS_PER_ISSUE * HN_WORD_CHUNKS)
        pltpu.make_async_copy(hn_hbm.at[pl.ds(0, words)], xbuf.at[slot, pl.ds(0, words)],
                              sem.at[slot]).wait()

        halves = ([], [])
        for c in range(HN_WORD_CHUNKS):
            w = xbuf[slot, pl.ds(c, MOE_BLOCK, stride=HN_WORD_CHUNKS), :]
            hi = lax.bitcast_convert_type(w & jnp.uint32(0xFFFF0000), F32)
            lo = lax.bitcast_convert_type(w << 16, F32)
            halves[0].append(hi.astype(BF16))
            halves[1].append(lo.astype(BF16))
        x = jnp.concatenate(halves[0] + halves[1], axis=1)
        hg = jnp.dot(x, wgf[ws].astype(BF16), preferred_element_type=F32)
        hu = jnp.dot(x, wuf[ws].astype(BF16), preferred_element_type=F32)
        hdn = (jax.nn.silu(hg) * hu).astype(BF16)
        y = jnp.dot(hdn, wdf[ws].astype(BF16), preferred_element_type=F32)
        bits = lax.bitcast_convert_type(y.astype(BF16).astype(F32), jnp.uint32)
        for c in range(HN_WORD_CHUNKS):
            hi = bits[:, c * LANES:(c + 1) * LANES]
            lo = bits[:, (c + HN_WORD_CHUNKS) * LANES:(c + HN_WORD_CHUNKS + 1) * LANES]
            y_ref[pl.ds(c, MOE_BLOCK, stride=HN_WORD_CHUNKS), :] = hi | (lo >> 16)

    @pl.when(jnp.logical_not(used))
    def _():
        y_ref[...] = jnp.zeros(y_ref.shape, y_ref.dtype)


def _experts(hn, w_gate, w_up, w_down, block_e, block_ord, used_experts, block_rows, row_tok,
             n_used):
    D = w_gate.shape[1]
    n_rows = row_tok.shape[0]
    n_blocks = n_rows // MOE_BLOCK
    tok3 = row_tok.reshape(n_blocks, 1, MOE_BLOCK)
    hbm = pl.BlockSpec(memory_space=pl.ANY)
    tok_block = lambda im: pl.BlockSpec((1, 1, MOE_BLOCK), im, memory_space=pltpu.SMEM)
    grid_spec = pltpu.PrefetchScalarGridSpec(
        num_scalar_prefetch=5,
        grid=(n_blocks,),
        in_specs=[tok_block(lambda b, *_, a=a: (jnp.minimum(b + a, n_blocks - 1), 0, 0))
                  for a in range(ROW_SLOTS)] + [hbm, hbm, hbm, hbm],
        out_specs=pl.BlockSpec((MOE_BLOCK * HN_WORD_CHUNKS, LANES), lambda b, *_: (b, 0)),
        scratch_shapes=[pltpu.VMEM((ROW_SLOTS, MOE_BLOCK * HN_WORD_CHUNKS, LANES), jnp.uint32),
                        pltpu.VMEM((WEIGHT_SLOTS, D, EXPERT_FF), F32),
                        pltpu.VMEM((WEIGHT_SLOTS, D, EXPERT_FF), F32),
                        pltpu.VMEM((WEIGHT_SLOTS, EXPERT_FF, D), F32),
                        pltpu.SemaphoreType.DMA((ROW_SLOTS,)),
                        pltpu.SemaphoreType.DMA((WEIGHT_SLOTS,))],
    )
    return pl.pallas_call(
        _expert_kernel,
        grid_spec=grid_spec,
        out_shape=jax.ShapeDtypeStruct((n_rows * HN_WORD_CHUNKS, LANES), jnp.uint32),
        compiler_params=_params(("arbitrary",)),
        name="experts",
    )(block_e, block_ord, used_experts, block_rows, n_used, *([tok3] * ROW_SLOTS),
      hn, w_gate, w_up, w_down)


def _combine_kernel(*refs, tc):
    dest_refs = refs[:COMBINE_SLOTS]
    y_hbm, h_ref, wt_ref, o_ref, ybuf, sem = refs[COMBINE_SLOTS:]
    i = pl.program_id(0)
    slot = i % COMBINE_SLOTS

    def gather_rows(dest_ref, slot_):
        def issue(r, c):
            dst = pl.multiple_of(r * HN_WORD_CHUNKS, HN_WORD_CHUNKS)
            for k in range(TOP_K):
                src = pl.multiple_of(dest_ref[0, 0, r * TOP_K + k] * HN_WORD_CHUNKS, HN_WORD_CHUNKS)
                pltpu.make_async_copy(y_hbm.at[pl.ds(src, HN_WORD_CHUNKS)],
                                      ybuf.at[slot_, k, pl.ds(dst, HN_WORD_CHUNKS)],
                                      sem.at[slot_]).start(priority=k % 2)
            return c

        lax.fori_loop(0, tc, issue, 0, unroll=4)

    @pl.when(i == 0)
    def _():
        for a in range(COMBINE_AHEAD):
            gather_rows(dest_refs[a], a)

    @pl.when(i + COMBINE_AHEAD < pl.num_programs(0))
    def _():
        gather_rows(dest_refs[COMBINE_AHEAD], (i + COMBINE_AHEAD) % COMBINE_SLOTS)

    for k in range(TOP_K):
        pltpu.make_async_copy(y_hbm.at[pl.ds(0, tc * HN_WORD_CHUNKS)], ybuf.at[slot, k],
                              sem.at[slot]).wait()
    wt = wt_ref[...]
    w1, w2 = wt[:, 0:1], wt[:, 1:2]
    for c in range(HN_WORD_CHUNKS):
        rows = pl.ds(c, tc, stride=HN_WORD_CHUNKS)
        words = [ybuf[slot, k, rows, :] for k in range(TOP_K)]
        hi = [lax.bitcast_convert_type(w & jnp.uint32(0xFFFF0000), F32) for w in words]
        lo = [lax.bitcast_convert_type(w << 16, F32) for w in words]
        for col, y in ((c * LANES, hi), ((c + HN_WORD_CHUNKS) * LANES, lo)):
            o_ref[:, col:col + LANES] = h_ref[:, col:col + LANES] + (w1 * y[0] + w2 * y[1])


def _combine(yrows, h, wts, dest, tc=128):
    T, D = h.shape
    steps = T // tc
    row = lambda i: (i, 0)
    dest3 = dest.reshape(steps, 1, tc * TOP_K)
    dest_block = lambda im: pl.BlockSpec((1, 1, tc * TOP_K), im, memory_space=pltpu.SMEM)
    return pl.pallas_call(
        functools.partial(_combine_kernel, tc=tc),
        grid=(steps,),
        in_specs=[dest_block(lambda i, a=a: (jnp.minimum(i + a, steps - 1), 0, 0))
                  for a in range(COMBINE_SLOTS)]
                 + [pl.BlockSpec(memory_space=pl.ANY),
                    pl.BlockSpec((tc, D), row),
                    pl.BlockSpec((tc, LANES), row)],
        out_specs=pl.BlockSpec((tc, D), row),
        out_shape=jax.ShapeDtypeStruct((T, D), F32),
        scratch_shapes=[pltpu.VMEM((COMBINE_SLOTS, TOP_K, tc * HN_WORD_CHUNKS, LANES), jnp.uint32),
                        pltpu.SemaphoreType.DMA((COMBINE_SLOTS,))],
        compiler_params=_params(("arbitrary",)),
        name="combine",
    )(*([dest3] * COMBINE_SLOTS), yrows, h, wts)


SUBLANES = 8
META_ROWS = 256

def _lane_cumsum(x):
    lane = lax.broadcasted_iota(jnp.int32, x.shape, 1)
    s = 1
    while s < LANES:
        x = x + jnp.where(lane >= s, pltpu.roll(x, s, 1), 0)
        s *= 2
    return x


def _dispatch_kernel(idx_ref, dest_ref, meta_ref, run_ref, prefix_ref, start_ref, *, tb):
    p = pl.program_id(0)
    i = pl.program_id(1)
    idx = idx_ref[...]
    lane = lax.broadcasted_iota(jnp.int32, idx.shape, 1)
    e1 = idx[:, 0:1]
    e2 = idx[:, 1:2]
    onehot = jnp.where(jnp.logical_or(lane == e1, lane == e2), 1.0, 0.0)

    @pl.when(jnp.logical_and(p == 0, i == 0))
    def _():
        run_ref[...] = jnp.zeros(run_ref.shape, F32)

    @pl.when(p == 0)
    def _():
        prefix_ref[i] = run_ref[...]
        run_ref[...] = run_ref[...] + jnp.sum(onehot, axis=0, keepdims=True)

    @pl.when(jnp.logical_and(p == 1, i == 0))
    def _():
        counts = run_ref[...].astype(jnp.int32)
        nblk = (counts + (MOE_BLOCK - 1)) // MOE_BLOCK
        end_blk = _lane_cumsum(nblk)
        start_ref[...] = ((end_blk - nblk) * MOE_BLOCK).astype(F32)
        has = jnp.where(counts > 0, 1, 0)
        ordinal = _lane_cumsum(has) - 1
        start_blk, counts = (end_blk - nblk)[0:1], counts[0:1]
        end_blk, has, ordinal = end_blk[0:1], has[0:1], ordinal[0:1]
        rows = lax.broadcasted_iota(jnp.int32, (META_ROWS, LANES), 0)
        lanes = lax.broadcasted_iota(jnp.int32, (META_ROWS, LANES), 1)
        is_e = lanes < N_EXPERTS
        rsum = lambda v: jnp.sum(v, axis=-1, keepdims=True)
        be = rsum(jnp.where(jnp.logical_and(is_e, end_blk <= rows), 1, 0))
        be = jnp.minimum(be, N_EXPERTS - 1)
        eo = rsum(jnp.where(lanes == be, ordinal, 0))
        ue = rsum(jnp.where(jnp.logical_and(has > 0, ordinal == rows), lanes, 0))
        n_blk = rsum(jnp.where(lanes == N_EXPERTS - 1, end_blk, 0))
        n_exp = rsum(jnp.where(is_e, has, 0))
        own = lanes == be
        left = rsum(jnp.where(own, counts, 0)) - MOE_BLOCK * (rows - rsum(jnp.where(own, start_blk, 0)))
        n_valid = jnp.clip(left, 0, MOE_BLOCK)
        meta_ref[...] = jnp.where(
            lanes == 0, be, jnp.where(lanes == 1, eo, jnp.where(
                lanes == 2, ue, jnp.where(lanes == 3, n_blk, jnp.where(lanes == 4, n_exp, n_valid)))))

    @pl.when(p == 1)
    def _():
        r = lax.broadcasted_iota(jnp.int32, (tb, tb), 0)
        c = lax.broadcasted_iota(jnp.int32, (tb, tb), 1)
        earlier = jnp.where(c < r, 1.0, 0.0).astype(BF16)
        rank = jnp.dot(earlier, onehot.astype(BF16), preferred_element_type=F32)
        rank = rank + prefix_ref[i][0:1] + start_ref[0:1]
        d1 = jnp.sum(jnp.where(lane == e1, rank, 0.0), axis=-1, keepdims=True)
        d2 = jnp.sum(jnp.where(lane == e2, rank, 0.0), axis=-1, keepdims=True)
        dest_ref[...] = jnp.where(lane == 0, d1, d2).astype(jnp.int32)


def _dispatch(idx, tb=512):
    T = idx.shape[0]
    n_rows = T * TOP_K + N_EXPERTS * MOE_BLOCK
    n_blocks = n_rows // MOE_BLOCK
    assert n_blocks <= META_ROWS
    dest2, meta = pl.pallas_call(
        functools.partial(_dispatch_kernel, tb=tb),
        grid=(2, T // tb),
        in_specs=[pl.BlockSpec((tb, LANES), lambda p, i: (i, 0))],
        out_specs=[pl.BlockSpec((tb, LANES), lambda p, i: (i * p, 0)),
                   pl.BlockSpec((META_ROWS, LANES), lambda p, i: (0, 0))],
        out_shape=[jax.ShapeDtypeStruct((T, LANES), jnp.int32),
                   jax.ShapeDtypeStruct((META_ROWS, LANES), jnp.int32)],
        scratch_shapes=[pltpu.VMEM((SUBLANES, LANES), F32),
                        pltpu.VMEM((T // tb, SUBLANES, LANES), F32),
                        pltpu.VMEM((SUBLANES, LANES), F32)],
        compiler_params=_params(("arbitrary", "arbitrary")),
        name="dispatch",
    )(idx)
    dest = dest2[:, :TOP_K].reshape(T * TOP_K)
    tok = jnp.repeat(jnp.arange(T, dtype=jnp.int32), TOP_K)
    row_tok = jnp.zeros((n_rows,), jnp.int32).at[dest].set(tok)
    block_e = meta[:n_blocks, 0]
    block_ord = meta[:n_blocks, 1]
    used_experts = meta[:N_EXPERTS, 2]
    n_used = meta[0, 3:5]
    block_rows = meta[:n_blocks, 5]
    return block_e, block_ord, used_experts, block_rows, row_tok, n_used, dest


def kernel(x, positions, norm1_g, w_in, q_norm_g, k_norm_g, sink_logits, sg_ln_g, sg_ln_b, sg_w, sg_b, w_branch_att, w_branch_sg, w_out, norm2_g, w_group_router, b_group_router, w_expert_router, b_expert_router, w_gate, w_up, w_down):
    B, S, D = x.shape
    T = B * S
    h = x.reshape(T, D)
    pos = positions.reshape(T)
    for l in range(norm1_g.shape[0]):
        xn = _rmsnorm(h, norm1_g[l])
        proj = _in_proj(xn, w_in[l])
        q, k, v = _qkv_prep(proj, pos, q_norm_g[l], k_norm_g[l])
        att = _attention(q, k, v, sink_logits[l], B)
        sgo = _spatial_gating(proj, sg_ln_g[l], sg_ln_b[l], sg_w[l], sg_b[l])
        merged = _merge(att, sgo, proj, w_branch_att[l], w_branch_sg[l])
        pad = LANES - N_GROUPS - N_EXPERTS
        w_router = jnp.concatenate(
            [w_group_router[l], w_expert_router[l], jnp.zeros((D, pad), F32)], axis=1).astype(BF16)
        b_router = jnp.concatenate(
            [b_group_router[l], b_expert_router[l], jnp.zeros((pad,), F32)]).reshape(1, LANES)
        h, hn, idx, wts = _out_proj(merged, w_out[l], h, norm2_g[l], w_router, b_router)
        block_e, block_ord, used_experts, block_rows, row_tok, n_used, dest = _dispatch(idx)
        yrows = _experts(hn, w_gate[l], w_up[l], w_down[l],
                         block_e, block_ord, used_experts, block_rows, row_tok, n_used)
        h = _combine(yrows, h, wts, dest)
    return h.reshape(B, S, D)
```

```python
import functools

import jax
import jax.numpy as jnp
from jax import lax
from jax.experimental import pallas as pl
from jax.experimental.pallas import tpu as pltpu

F32 = jnp.float32
BF16 = jnp.bfloat16

D_MODEL = 2048
HEAD_DIM = 64
ATT_WIDTH = D_MODEL // 2
ATT_HEADS = ATT_WIDTH // HEAD_DIM
ATT_KV_HEADS = ATT_HEADS // 4
Q_PER_KV = ATT_HEADS // ATT_KV_HEADS
KV_WIDTH = ATT_KV_HEADS * HEAD_DIM
WINDOW = 128
ATT_BLOCK = 128
ROPE_DIM = HEAD_DIM // 4
ROPE_HALF = ROPE_DIM // 2
ROPE_THETA = 500000.0
SG_WIDTH = D_MODEL // 2
SG_GROUP_DIM = 128
SG_GROUPS = SG_WIDTH // SG_GROUP_DIM
SG_CHUNK = 128
OFF_Q = 0
OFF_K = OFF_Q + ATT_WIDTH
OFF_V = OFF_K + KV_WIDTH
OFF_U = OFF_V + KV_WIDTH
OFF_S = OFF_U + SG_WIDTH
OFF_G = OFF_S + SG_WIDTH
IN_COLS = OFF_G + 2 * D_MODEL
N_GROUPS = 8
EXPERTS_PER_GROUP = 8
N_EXPERTS = N_GROUPS * EXPERTS_PER_GROUP
TOP_K = 2
EXPERT_FF = D_MODEL // 4
MOE_BLOCK = 128
EPS = 1e-6
NEG_INF = -1e30

LANES = 128
HN_WORD_CHUNKS = D_MODEL // LANES // 2
ROWS_PER_ISSUE = 8
ROW_AHEAD = 4
ROW_SLOTS = ROW_AHEAD + 1
MERGE_ROW_PARTS = 4
OUT_ROW_PARTS = 4
WEIGHT_SLOTS = 3
COMBINE_AHEAD = 1
COMBINE_SLOTS = COMBINE_AHEAD + 1
VMEM_LIMIT = 56 * 1024 * 1024


def _params(sem, vmem=VMEM_LIMIT):
    return pltpu.CompilerParams(dimension_semantics=sem, vmem_limit_bytes=vmem)


def _rmsnorm_kernel(x_ref, g_ref, o_ref):
    x = x_ref[...]
    r = lax.rsqrt(jnp.mean(x * x, axis=-1, keepdims=True) + EPS)
    o_ref[...] = ((x * r) * g_ref[...]).astype(o_ref.dtype)


def _rmsnorm(x, g, tm=1024):
    T, D = x.shape
    return pl.pallas_call(
        _rmsnorm_kernel,
        grid=(T // tm,),
        in_specs=[pl.BlockSpec((tm, D), lambda i: (i, 0)),
                  pl.BlockSpec((1, D), lambda i: (0, 0))],
        out_specs=pl.BlockSpec((tm, D), lambda i: (i, 0)),
        out_shape=jax.ShapeDtypeStruct((T, D), BF16),
        compiler_params=_params(("parallel",)),
        name="norm1",
    )(x, g.reshape(1, D))


def _proj_kernel(x_ref, w_ref, o_ref, wbf_ref):
    @pl.when(pl.program_id(1) == 0)
    def _():
        wbf_ref[...] = w_ref[...].astype(BF16)

    o_ref[...] = jnp.dot(x_ref[...], wbf_ref[...], preferred_element_type=F32)


def _in_proj(xn, w, tm=1024, tn=1280):
    T, D = xn.shape
    N = w.shape[1]
    return pl.pallas_call(
        _proj_kernel,
        grid=(N // tn, T // tm),
        in_specs=[pl.BlockSpec((tm, D), lambda j, i: (i, 0)),
                  pl.BlockSpec((D, tn), lambda j, i: (0, j))],
        out_specs=pl.BlockSpec((tm, tn), lambda j, i: (i, j)),
        out_shape=jax.ShapeDtypeStruct((T, N), F32),
        scratch_shapes=[pltpu.VMEM((D, tn), BF16)],
        compiler_params=_params(("arbitrary", "arbitrary")),
        name="in_proj",
    )(xn, w)


def _rope_table_kernel(pos_ref, invf_ref, cos_ref, sin_ref):
    ang = pos_ref[...].astype(F32) * invf_ref[...]
    cos_ref[...] = jnp.cos(ang)
    sin_ref[...] = jnp.sin(ang)


def _rope_tables(positions):
    T = positions.shape[0]
    rows = T * ROPE_HALF // LANES
    inv = ROPE_THETA ** (-jnp.arange(0, ROPE_DIM, 2, dtype=F32) / ROPE_DIM)
    invf = jnp.tile(inv, LANES // ROPE_HALF).reshape(1, LANES)
    pos = jnp.repeat(positions, ROPE_HALF).reshape(rows, LANES)
    whole = lambda: (0, 0)
    cos, sin = pl.pallas_call(
        _rope_table_kernel,
        in_specs=[pl.BlockSpec((rows, LANES), whole), pl.BlockSpec((1, LANES), whole)],
        out_specs=[pl.BlockSpec((rows, LANES), whole), pl.BlockSpec((rows, LANES), whole)],
        out_shape=[jax.ShapeDtypeStruct((rows, LANES), F32)] * 2,
        name="rope_tables",
    )(pos, invf)
    return cos.reshape(T, ROPE_HALF), sin.reshape(T, ROPE_HALF)


def _qkv_prep_kernel(p_ref, cos_ref, sin_ref, gq_ref, gk_ref, seg_ref, q_ref, k_ref, v_ref):
    cos = cos_ref[...]
    sin = sin_ref[...]
    lane = lax.broadcasted_iota(jnp.int32, cos.shape, 1)
    first_half = (lane % HEAD_DIM) < ROPE_HALF

    def norm_rope(x, g):
        x2 = x * x
        x2_hi = x2.astype(BF16)
        x2_lo = (x2 - x2_hi.astype(F32)).astype(BF16)
        both = jnp.dot(jnp.concatenate([x2_hi, x2_lo], axis=0), seg_ref[...],
                       preferred_element_type=F32)
        ssq = both[:x.shape[0]] + both[x.shape[0]:]
        xn = (x * lax.rsqrt(ssq * (1.0 / HEAD_DIM) + EPS)) * g
        partner = jnp.where(first_half,
                            pltpu.roll(xn, LANES - ROPE_HALF, 1),
                            pltpu.roll(xn, ROPE_HALF, 1))
        return xn * cos + partner * sin

    for c in range(ATT_WIDTH // LANES):
        x = p_ref[:, OFF_Q + c * LANES:OFF_Q + (c + 1) * LANES]
        q_ref[:, c * LANES:(c + 1) * LANES] = (
            norm_rope(x, gq_ref[...]) * (HEAD_DIM ** -0.5)).astype(q_ref.dtype)
    for c in range(KV_WIDTH // LANES):
        x = p_ref[:, OFF_K + c * LANES:OFF_K + (c + 1) * LANES]
        k_ref[:, c * LANES:(c + 1) * LANES] = norm_rope(x, gk_ref[...]).astype(k_ref.dtype)
    v_ref[...] = p_ref[:, OFF_V:OFF_V + KV_WIDTH].astype(v_ref.dtype)


def _qkv_prep(proj, positions, q_g, k_g, tq=512):
    T = proj.shape[0]
    width = OFF_U
    cos8, sin8 = _rope_tables(positions)
    rest = HEAD_DIM - ROPE_DIM
    cos_t = jnp.tile(jnp.concatenate([cos8, cos8, jnp.ones((T, rest), F32)], axis=1),
                     (1, LANES // HEAD_DIM))
    sin_t = jnp.tile(jnp.concatenate([-sin8, sin8, jnp.zeros((T, rest), F32)], axis=1),
                     (1, LANES // HEAD_DIM))
    head_of_lane = jnp.arange(LANES) // HEAD_DIM
    same_head = (head_of_lane[:, None] == head_of_lane[None, :]).astype(BF16)
    gq = jnp.tile(q_g, LANES // HEAD_DIM).reshape(1, LANES)
    gk = jnp.tile(k_g, LANES // HEAD_DIM).reshape(1, LANES)
    row = lambda i: (i, 0)
    const = lambda i: (0, 0)
    return pl.pallas_call(
        _qkv_prep_kernel,
        grid=(T // tq,),
        in_specs=[pl.BlockSpec((tq, width), row),
                  pl.BlockSpec((tq, LANES), row),
                  pl.BlockSpec((tq, LANES), row),
                  pl.BlockSpec((1, LANES), const),
                  pl.BlockSpec((1, LANES), const),
                  pl.BlockSpec((LANES, LANES), const)],
        out_specs=[pl.BlockSpec((tq, ATT_WIDTH), row),
                   pl.BlockSpec((tq, KV_WIDTH), row),
                   pl.BlockSpec((tq, KV_WIDTH), row)],
        out_shape=[jax.ShapeDtypeStruct((T, ATT_WIDTH), BF16),
                   jax.ShapeDtypeStruct((T, KV_WIDTH), BF16),
                   jax.ShapeDtypeStruct((T, KV_WIDTH), BF16)],
        compiler_params=_params(("parallel",)),
        name="qkv_prep",
    )(proj, cos_t, sin_t, gq, gk, same_head)


def _attn_kernel(sink_ref, q_ref, kp_ref, kc_ref, kn_ref, vp_ref, vc_ref, vn_ref, o_ref, *, nb):
    n = pl.program_id(1)
    rows = ATT_BLOCK
    qi = lax.broadcasted_iota(jnp.int32, (rows, ATT_BLOCK), 0) % ATT_BLOCK
    kj = lax.broadcasted_iota(jnp.int32, (rows, ATT_BLOCK), 1)
    lo_prev = jnp.where(n > 0, 0, ATT_BLOCK)
    hi_next = jnp.where(n < nb - 1, 0, -ATT_BLOCK)
    cap_prev = jnp.where(kj - qi >= lo_prev, jnp.inf, NEG_INF)
    cap_next = jnp.where(kj - qi <= hi_next, jnp.inf, NEG_INF)
    cap = jnp.concatenate([cap_prev, jnp.full((rows, ATT_BLOCK), jnp.inf, F32), cap_next], axis=1)

    kvhs = range(ATT_KV_HEADS)
    scores = []
    for kvh in kvhs:
        cols = slice(kvh * HEAD_DIM, (kvh + 1) * HEAD_DIM)
        k = jnp.concatenate([kp_ref[:, cols], kc_ref[:, cols], kn_ref[:, cols]], axis=0)
        q = jnp.concatenate(
            [q_ref[:, (kvh * Q_PER_KV + g) * HEAD_DIM:(kvh * Q_PER_KV + g + 1) * HEAD_DIM]
             for g in range(Q_PER_KV)], axis=0)
        scores.append(lax.dot_general(q, k, (((1,), (1,)), ((), ())),
                                      preferred_element_type=F32))
    probs = []
    for kvh in kvhs:
        strips = []
        for g in range(Q_PER_KV):
            sink = sink_ref[kvh * Q_PER_KV + g]
            sg = jnp.minimum(scores[kvh][g * ATT_BLOCK:(g + 1) * ATT_BLOCK], cap)
            m = jnp.maximum(jnp.max(sg, axis=-1, keepdims=True), sink)
            e = jnp.exp(sg - m)
            denom = jnp.sum(e, axis=-1, keepdims=True) + jnp.exp(sink - m)
            strips.append((e / denom).astype(BF16))
        probs.append(jnp.concatenate(strips, axis=0))
    for kvh in kvhs:
        cols = slice(kvh * HEAD_DIM, (kvh + 1) * HEAD_DIM)
        v = jnp.concatenate([vp_ref[:, cols], vc_ref[:, cols], vn_ref[:, cols]], axis=0)
        o = jnp.dot(probs[kvh], v, preferred_element_type=F32)
        for g in range(Q_PER_KV):
            h = kvh * Q_PER_KV + g
            o_ref[:, h * HEAD_DIM:(h + 1) * HEAD_DIM] = (
                o[g * ATT_BLOCK:(g + 1) * ATT_BLOCK].astype(o_ref.dtype))


def _attention(q, k, v, sink, batch):
    T = q.shape[0]
    nb = T // batch // ATT_BLOCK
    cur = lambda b, n: (b * nb + n, 0)
    prev = lambda b, n: (b * nb + jnp.maximum(n - 1, 0), 0)
    nxt = lambda b, n: (b * nb + jnp.minimum(n + 1, nb - 1), 0)
    kv = lambda im: pl.BlockSpec((ATT_BLOCK, KV_WIDTH), im)
    return pl.pallas_call(
        functools.partial(_attn_kernel, nb=nb),
        grid=(batch, nb),
        in_specs=[pl.BlockSpec(memory_space=pltpu.SMEM),
                  pl.BlockSpec((ATT_BLOCK, ATT_WIDTH), cur),
                  kv(prev), kv(cur), kv(nxt), kv(prev), kv(cur), kv(nxt)],
        out_specs=pl.BlockSpec((ATT_BLOCK, ATT_WIDTH), cur),
        out_shape=jax.ShapeDtypeStruct((T, ATT_WIDTH), BF16),
        compiler_params=_params(("parallel", "parallel")),
        name="window_attn",
    )(sink, q, k, k, k, v, v, v)


def _sg_kernel(u_ref, s_ref, lng_ref, lnb_ref, w_ref, b_ref, o_ref):
    groups = w_ref.shape[0]

    def one_chunk(ci, carry):
        rows = pl.ds(pl.multiple_of(ci * SG_CHUNK, SG_CHUNK), SG_CHUNK)
        gs = range(groups)
        cols = [slice(gi * SG_GROUP_DIM, (gi + 1) * SG_GROUP_DIM) for gi in gs]
        s = [jax.nn.gelu(s_ref[rows, cols[gi]]) for gi in gs]
        sc = [s[gi] - jnp.mean(s[gi], axis=-1, keepdims=True) for gi in gs]
        var = [jnp.mean(sc[gi] * sc[gi], axis=-1, keepdims=True) for gi in gs]
        sn = [(sc[gi] * lax.rsqrt(var[gi] + EPS)) * lng_ref[0, gi:gi + 1, :]
              + lnb_ref[0, gi:gi + 1, :] for gi in gs]
        mixed = [jnp.dot(w_ref[gi], sn[gi].astype(BF16), preferred_element_type=F32)
                 + b_ref[0, :, gi:gi + 1] for gi in gs]
        for gi in gs:
            o_ref[rows, cols[gi]] = (jax.nn.gelu(u_ref[rows, cols[gi]]) * mixed[gi]).astype(o_ref.dtype)
        return carry

    lax.fori_loop(0, u_ref.shape[0] // SG_CHUNK, one_chunk, 0)


def _spatial_gating(proj, ln_g, ln_b, w_s, b_s, halves=2, chunks=8):
    T = proj.shape[0]
    half_w = SG_WIDTH // halves
    gph = SG_GROUPS // halves
    u0, s0 = OFF_U // half_w, OFF_S // half_w
    tq = chunks * SG_CHUNK
    return pl.pallas_call(
        _sg_kernel,
        grid=(T // tq, halves),
        in_specs=[pl.BlockSpec((tq, half_w), lambda i, j: (i, u0 + j)),
                  pl.BlockSpec((tq, half_w), lambda i, j: (i, s0 + j)),
                  pl.BlockSpec((1, gph, SG_GROUP_DIM), lambda i, j: (j, 0, 0)),
                  pl.BlockSpec((1, gph, SG_GROUP_DIM), lambda i, j: (j, 0, 0)),
                  pl.BlockSpec((gph, SG_CHUNK, SG_CHUNK), lambda i, j: (j, 0, 0)),
                  pl.BlockSpec((1, SG_CHUNK, gph), lambda i, j: (j, 0, 0))],
        out_specs=pl.BlockSpec((tq, half_w), lambda i, j: (i, j)),
        out_shape=jax.ShapeDtypeStruct((T, SG_WIDTH), BF16),
        compiler_params=_params(("parallel", "parallel")),
        name="spatial_gating",
    )(proj, proj,
      ln_g.reshape(halves, gph, SG_GROUP_DIM), ln_b.reshape(halves, gph, SG_GROUP_DIM),
      w_s.astype(BF16),
      b_s.reshape(halves, gph, SG_CHUNK).transpose(0, 2, 1))


def _merge_kernel(att_ref, sgo_ref, ga_ref, gb_ref, wa_ref, wb_ref, o_ref, wa_bf, wb_bf):
    @pl.when(pl.program_id(1) == 0)
    def _():
        wa_bf[...] = wa_ref[...].astype(BF16)
        wb_bf[...] = wb_ref[...].astype(BF16)

    part_rows = att_ref.shape[0] // MERGE_ROW_PARTS
    parts = [slice(p * part_rows, (p + 1) * part_rows) for p in range(MERGE_ROW_PARTS)]
    a = [jnp.dot(att_ref[r, :], wa_bf[...], preferred_element_type=F32) for r in parts]
    b = [jnp.dot(sgo_ref[r, :], wb_bf[...], preferred_element_type=F32) for r in parts]
    for p, r in enumerate(parts):
        m = jax.nn.sigmoid(ga_ref[r, :]) * a[p] + jax.nn.sigmoid(gb_ref[r, :]) * b[p]
        o_ref[r, :] = m.astype(o_ref.dtype)


def _merge(att, sgo, proj, w_a, w_b, tm=1024, tn=512):
    T = att.shape[0]
    ga0 = OFF_G // tn
    gb0 = (OFF_G + D_MODEL) // tn
    return pl.pallas_call(
        _merge_kernel,
        grid=(D_MODEL // tn, T // tm),
        in_specs=[pl.BlockSpec((tm, ATT_WIDTH), lambda j, i: (i, 0)),
                  pl.BlockSpec((tm, SG_WIDTH), lambda j, i: (i, 0)),
                  pl.BlockSpec((tm, tn), lambda j, i: (i, ga0 + j)),
                  pl.BlockSpec((tm, tn), lambda j, i: (i, gb0 + j)),
                  pl.BlockSpec((ATT_WIDTH, tn), lambda j, i: (0, j)),
                  pl.BlockSpec((SG_WIDTH, tn), lambda j, i: (0, j))],
        out_specs=pl.BlockSpec((tm, tn), lambda j, i: (i, j)),
        out_shape=jax.ShapeDtypeStruct((T, D_MODEL), BF16),
        scratch_shapes=[pltpu.VMEM((ATT_WIDTH, tn), BF16),
                        pltpu.VMEM((SG_WIDTH, tn), BF16)],
        compiler_params=_params(("arbitrary", "arbitrary")),
        name="merge",
    )(att, sgo, proj, proj, w_a, w_b)


def _route(logits):
    lane = lax.broadcasted_iota(jnp.int32, logits.shape, 1)
    lane_f = lane.astype(F32)
    is_g = lane < N_GROUPS
    gl = jnp.where(is_g, logits, -jnp.inf)
    gmax = jnp.max(gl, axis=-1, keepdims=True)
    grp = jnp.min(jnp.where(gl == gmax, lane_f, float(LANES)), axis=-1, keepdims=True)
    gsum = jnp.sum(jnp.where(is_g, jnp.exp(logits - gmax), 0.0), axis=-1, keepdims=True)
    g_w = 1.0 / gsum
    e_lane = lane - N_GROUPS
    in_grp = jnp.logical_and(
        jnp.logical_and(e_lane >= 0, e_lane < N_EXPERTS),
        (e_lane // EXPERTS_PER_GROUP).astype(F32) == grp)
    el = jnp.where(in_grp, logits, -jnp.inf)
    v1 = jnp.max(el, axis=-1, keepdims=True)
    i1 = jnp.min(jnp.where(jnp.logical_and(in_grp, el == v1), lane_f, float(LANES)),
                 axis=-1, keepdims=True)
    rest = jnp.logical_and(in_grp, lane_f != i1)
    el2 = jnp.where(rest, logits, -jnp.inf)
    v2 = jnp.max(el2, axis=-1, keepdims=True)
    i2 = jnp.min(jnp.where(jnp.logical_and(rest, el2 == v2), lane_f, float(LANES)),
                 axis=-1, keepdims=True)
    e21 = jnp.exp(v2 - v1)
    w1 = g_w / (1.0 + e21)
    w2 = g_w * e21 / (1.0 + e21)
    idx = jnp.where(lane == 0, i1, i2) - float(N_GROUPS)
    wts = jnp.where(lane == 0, w1, jnp.where(lane == 1, w2, 0.0))
    return idx.astype(jnp.int32), wts


def _out_kernel(m_ref, w_ref, x_ref, g_ref, wr_ref, br_ref, h_ref, hn_ref, idx_ref, wt_ref):
    part_rows = m_ref.shape[0] // OUT_ROW_PARTS
    parts = [slice(p * part_rows, (p + 1) * part_rows) for p in range(OUT_ROW_PARTS)]
    hs = [x_ref[r, :] + jnp.dot(m_ref[r, :], w_ref[...], preferred_element_type=F32)
          for r in parts]
    for p, r in enumerate(parts):
        h = hs[p]
        h_ref[r, :] = h
        rs = lax.rsqrt(jnp.mean(h * h, axis=-1, keepdims=True) + EPS)
        hn = ((h * rs) * g_ref[...]).astype(BF16)
        bits = lax.bitcast_convert_type(hn.astype(F32), jnp.uint32)
        for c in range(HN_WORD_CHUNKS):
            hi = bits[:, c * LANES:(c + 1) * LANES]
            lo = bits[:, (c + HN_WORD_CHUNKS) * LANES:(c + HN_WORD_CHUNKS + 1) * LANES]
            hn_ref[pl.ds(p * part_rows * HN_WORD_CHUNKS + c, part_rows,
                         stride=HN_WORD_CHUNKS), :] = hi | (lo >> 16)
        logits = jnp.dot(hn, wr_ref[...], preferred_element_type=F32) + br_ref[...]
        idx, wts = _route(logits)
        idx_ref[r, :] = idx
        wt_ref[r, :] = wts


def _out_proj(merged, w_out, x, g2, w_router, b_router, tm=512):
    T, D = x.shape
    row = lambda i: (i, 0)
    const = lambda i: (0, 0)
    return pl.pallas_call(
        _out_kernel,
        grid=(T // tm,),
        in_specs=[pl.BlockSpec((tm, D), row),
                  pl.BlockSpec((D, D), const),
                  pl.BlockSpec((tm, D), row),
                  pl.BlockSpec((1, D), const),
                  pl.BlockSpec((D, LANES), const),
                  pl.BlockSpec((1, LANES), const)],
        out_specs=[pl.BlockSpec((tm, D), row),
                   pl.BlockSpec((tm * HN_WORD_CHUNKS, LANES), row),
                   pl.BlockSpec((tm, LANES), row),
                   pl.BlockSpec((tm, LANES), row)],
        out_shape=[jax.ShapeDtypeStruct((T, D), F32),
                   jax.ShapeDtypeStruct((T * HN_WORD_CHUNKS, LANES), jnp.uint32),
                   jax.ShapeDtypeStruct((T, LANES), jnp.int32),
                   jax.ShapeDtypeStruct((T, LANES), F32)],
        compiler_params=_params(("parallel",)),
        name="out_proj_router",
    )(merged, w_out.astype(BF16), x, g2.reshape(1, D), w_router, b_router)


def _expert_kernel(be_ref, eo_ref, ue_ref, nr_ref, nu_ref, *refs):
    rt_refs = refs[:ROW_SLOTS]
    (hn_hbm, wg_hbm, wu_hbm, wd_hbm, y_ref,
     xbuf, wgf, wuf, wdf, sem, wsem) = refs[ROW_SLOTS:]
    b = pl.program_id(0)
    n_used = nu_ref[0]
    n_exp = nu_ref[1]
    used = b < n_used
    slot = b % ROW_SLOTS

    def weight_copies(ordinal, slot_):
        e = ue_ref[ordinal]
        return (pltpu.make_async_copy(wg_hbm.at[e], wgf.at[slot_], wsem.at[slot_]),
                pltpu.make_async_copy(wu_hbm.at[e], wuf.at[slot_], wsem.at[slot_]),
                pltpu.make_async_copy(wd_hbm.at[e], wdf.at[slot_], wsem.at[slot_]))

    def start_weights(ordinal):
        for c in weight_copies(ordinal, ordinal % WEIGHT_SLOTS):
            c.start(priority=1)

    def issue_trips(blk):
        return (nr_ref[blk] + (ROWS_PER_ISSUE - 1)) // ROWS_PER_ISSUE

    def gather_rows(tok_ref, blk, slot_):
        def issue(g, c):
            for j in range(ROWS_PER_ISSUE):
                r = g * ROWS_PER_ISSUE + j
                src = pl.multiple_of(tok_ref[0, 0, r] * HN_WORD_CHUNKS, HN_WORD_CHUNKS)
                dst = pl.multiple_of(r * HN_WORD_CHUNKS, HN_WORD_CHUNKS)
                pltpu.make_async_copy(hn_hbm.at[pl.ds(src, HN_WORD_CHUNKS)],
                                      xbuf.at[slot_, pl.ds(dst, HN_WORD_CHUNKS)],
                                      sem.at[slot_]).start()
            return c

        lax.fori_loop(0, issue_trips(blk), issue, 0)

    @pl.when(b == 0)
    def _():
        start_weights(0)

        @pl.when(n_exp > 1)
        def _():
            start_weights(1)

        xbuf[...] = jnp.zeros(xbuf.shape, xbuf.dtype)
        gather_rows(rt_refs[0], 0, 0)
        for a in range(1, ROW_AHEAD):
            @pl.when(n_used > a)
            def _(a=a):
                gather_rows(rt_refs[a], a, a)

    @pl.when(b + ROW_AHEAD < n_used)
    def _():
        gather_rows(rt_refs[ROW_AHEAD], b + ROW_AHEAD, (b + ROW_AHEAD) % ROW_SLOTS)

    @pl.when(used)
    def _():
        first = jnp.logical_or(b == 0, be_ref[b] != be_ref[jnp.maximum(b - 1, 0)])

        ordinal = eo_ref[b]
        ws = ordinal % WEIGHT_SLOTS

        @pl.when(first)
        def _():
            for c in weight_copies(ordinal, ws):
                c.wait()

            @pl.when(ordinal + 2 < n_exp)
            def _():
                start_weights(ordinal + 2)

        words = issue_trips(b) * (ROWS_PER_ISSUE * HN_WORD_CHUNKS)
        pltpu.make_async_copy(hn_hbm.at[pl.ds(0, words)], xbuf.at[slot, pl.ds(0, words)],
                              sem.at[slot]).wait()

        halves = ([], [])
        for c in range(HN_WORD_CHUNKS):
            w = xbuf[slot, pl.ds(c, MOE_BLOCK, stride=HN_WORD_CHUNKS), :]
            hi = lax.bitcast_convert_type(w & jnp.uint32(0xFFFF0000), F32)
            lo = lax.bitcast_convert_type(w << 16, F32)
            halves[0].append(hi.astype(BF16))
            halves[1].append(lo.astype(BF16))
        x = jnp.concatenate(halves[0] + halves[1], axis=1)
        hg = jnp.dot(x, wgf[ws].astype(BF16), preferred_element_type=F32)
        hu = jnp.dot(x, wuf[ws].astype(BF16), preferred_element_type=F32)
        hdn = (jax.nn.silu(hg) * hu).astype(BF16)
        y = jnp.dot(hdn, wdf[ws].astype(BF16), preferred_element_type=F32)
        bits = lax.bitcast_convert_type(y.astype(BF16).astype(F32), jnp.uint32)
        for c in range(HN_WORD_CHUNKS):
            hi = bits[:, c * LANES:(c + 1) * LANES]
            lo = bits[:, (c + HN_WORD_CHUNKS) * LANES:(c + HN_WORD_CHUNKS + 1) * LANES]
            y_ref[pl.ds(c, MOE_BLOCK, stride=HN_WORD_CHUNKS), :] = hi | (lo >> 16)

    @pl.when(jnp.logical_not(used))
    def _():
        y_ref[...] = jnp.zeros(y_ref.shape, y_ref.dtype)


def _experts(hn, w_gate, w_up, w_down, block_e, block_ord, used_experts, block_rows, row_tok,
             n_used):
    D = w_gate.shape[1]
    n_rows = row_tok.shape[0]
    n_blocks = n_rows // MOE_BLOCK
    tok3 = row_tok.reshape(n_blocks, 1, MOE_BLOCK)
    hbm = pl.BlockSpec(memory_space=pl.ANY)
    tok_block = lambda im: pl.BlockSpec((1, 1, MOE_BLOCK), im, memory_space=pltpu.SMEM)
    grid_spec = pltpu.PrefetchScalarGridSpec(
        num_scalar_prefetch=5,
        grid=(n_blocks,),
        in_specs=[tok_block(lambda b, *_, a=a: (jnp.minimum(b + a, n_blocks - 1), 0, 0))
                  for a in range(ROW_SLOTS)] + [hbm, hbm, hbm, hbm],
        out_specs=pl.BlockSpec((MOE_BLOCK * HN_WORD_CHUNKS, LANES), lambda b, *_: (b, 0)),
        scratch_shapes=[pltpu.VMEM((ROW_SLOTS, MOE_BLOCK * HN_WORD_CHUNKS, LANES), jnp.uint32),
                        pltpu.VMEM((WEIGHT_SLOTS, D, EXPERT_FF), F32),
                        pltpu.VMEM((WEIGHT_SLOTS, D, EXPERT_FF), F32),
                        pltpu.VMEM((WEIGHT_SLOTS, EXPERT_FF, D), F32),
                        pltpu.SemaphoreType.DMA((ROW_SLOTS,)),
                        pltpu.SemaphoreType.DMA((WEIGHT_SLOTS,))],
    )
    return pl.pallas_call(
        _expert_kernel,
        grid_spec=grid_spec,
        out_shape=jax.ShapeDtypeStruct((n_rows * HN_WORD_CHUNKS, LANES), jnp.uint32),
        compiler_params=_params(("arbitrary",)),
        name="experts",
    )(block_e, block_ord, used_experts, block_rows, n_used, *([tok3] * ROW_SLOTS),
      hn, w_gate, w_up, w_down)


def _combine_kernel(*refs, tc):
    dest_refs = refs[:COMBINE_SLOTS]
    y_hbm, h_ref, wt_ref, o_ref, ybuf, sem = refs[COMBINE_SLOTS:]
    i = pl.program_id(0)
    slot = i % COMBINE_SLOTS

    def gather_rows(dest_ref, slot_):
        def issue(r, c):
            dst = pl.multiple_of(r * HN_WORD_CHUNKS, HN_WORD_CHUNKS)
            for k in range(TOP_K):
                src = pl.multiple_of(dest_ref[0, 0, r * TOP_K + k] * HN_WORD_CHUNKS, HN_WORD_CHUNKS)
                pltpu.make_async_copy(y_hbm.at[pl.ds(src, HN_WORD_CHUNKS)],
                                      ybuf.at[slot_, k, pl.ds(dst, HN_WORD_CHUNKS)],
                                      sem.at[slot_]).start(priority=k % 2)
            return c

        lax.fori_loop(0, tc, issue, 0, unroll=4)

    @pl.when(i == 0)
    def _():
        for a in range(COMBINE_AHEAD):
            gather_rows(dest_refs[a], a)

    @pl.when(i + COMBINE_AHEAD < pl.num_programs(0))
    def _():
        gather_rows(dest_refs[COMBINE_AHEAD], (i + COMBINE_AHEAD) % COMBINE_SLOTS)

    for k in range(TOP_K):
        pltpu.make_async_copy(y_hbm.at[pl.ds(0, tc * HN_WORD_CHUNKS)], ybuf.at[slot, k],
                              sem.at[slot]).wait()
    wt = wt_ref[...]
    w1, w2 = wt[:, 0:1], wt[:, 1:2]
    for c in range(HN_WORD_CHUNKS):
        rows = pl.ds(c, tc, stride=HN_WORD_CHUNKS)
        words = [ybuf[slot, k, rows, :] for k in range(TOP_K)]
        hi = [lax.bitcast_convert_type(w & jnp.uint32(0xFFFF0000), F32) for w in words]
        lo = [lax.bitcast_convert_type(w << 16, F32) for w in words]
        for col, y in ((c * LANES, hi), ((c + HN_WORD_CHUNKS) * LANES, lo)):
            o_ref[:, col:col + LANES] = h_ref[:, col:col + LANES] + (w1 * y[0] + w2 * y[1])


def _combine(yrows, h, wts, dest, tc=256):
    T, D = h.shape
    steps = T // tc
    row = lambda i: (i, 0)
    dest3 = dest.reshape(steps, 1, tc * TOP_K)
    dest_block = lambda im: pl.BlockSpec((1, 1, tc * TOP_K), im, memory_space=pltpu.SMEM)
    return pl.pallas_call(
        functools.partial(_combine_kernel, tc=tc),
        grid=(steps,),
        in_specs=[dest_block(lambda i, a=a: (jnp.minimum(i + a, steps - 1), 0, 0))
                  for a in range(COMBINE_SLOTS)]
                 + [pl.BlockSpec(memory_space=pl.ANY),
                    pl.BlockSpec((tc, D), row),
                    pl.BlockSpec((tc, LANES), row)],
        out_specs=pl.BlockSpec((tc, D), row),
        out_shape=jax.ShapeDtypeStruct((T, D), F32),
        scratch_shapes=[pltpu.VMEM((COMBINE_SLOTS, TOP_K, tc * HN_WORD_CHUNKS, LANES), jnp.uint32),
                        pltpu.SemaphoreType.DMA((COMBINE_SLOTS,))],
        compiler_params=_params(("arbitrary",)),
        name="combine",
    )(*([dest3] * COMBINE_SLOTS), yrows, h, wts)


SUBLANES = 8
META_ROWS = 256

def _lane_cumsum(x):
    lane = lax.broadcasted_iota(jnp.int32, x.shape, 1)
    s = 1
    while s < LANES:
        x = x + jnp.where(lane >= s, pltpu.roll(x, s, 1), 0)
        s *= 2
    return x


def _dispatch_kernel(idx_ref, dest_ref, meta_ref, run_ref, prefix_ref, start_ref, *, tb):
    p = pl.program_id(0)
    i = pl.program_id(1)
    idx = idx_ref[...]
    lane = lax.broadcasted_iota(jnp.int32, idx.shape, 1)
    e1 = idx[:, 0:1]
    e2 = idx[:, 1:2]
    onehot = jnp.where(jnp.logical_or(lane == e1, lane == e2), 1.0, 0.0)

    @pl.when(jnp.logical_and(p == 0, i == 0))
    def _():
        run_ref[...] = jnp.zeros(run_ref.shape, F32)

    @pl.when(p == 0)
    def _():
        prefix_ref[i] = run_ref[...]
        run_ref[...] = run_ref[...] + jnp.sum(onehot, axis=0, keepdims=True)

    @pl.when(jnp.logical_and(p == 1, i == 0))
    def _():
        counts = run_ref[...].astype(jnp.int32)
        nblk = (counts + (MOE_BLOCK - 1)) // MOE_BLOCK
        end_blk = _lane_cumsum(nblk)
        start_ref[...] = ((end_blk - nblk) * MOE_BLOCK).astype(F32)
        has = jnp.where(counts > 0, 1, 0)
        ordinal = _lane_cumsum(has) - 1
        start_blk, counts = (end_blk - nblk)[0:1], counts[0:1]
        end_blk, has, ordinal = end_blk[0:1], has[0:1], ordinal[0:1]
        rows = lax.broadcasted_iota(jnp.int32, (META_ROWS, LANES), 0)
        lanes = lax.broadcasted_iota(jnp.int32, (META_ROWS, LANES), 1)
        is_e = lanes < N_EXPERTS
        rsum = lambda v: jnp.sum(v, axis=-1, keepdims=True)
        be = rsum(jnp.where(jnp.logical_and(is_e, end_blk <= rows), 1, 0))
        be = jnp.minimum(be, N_EXPERTS - 1)
        eo = rsum(jnp.where(lanes == be, ordinal, 0))
        ue = rsum(jnp.where(jnp.logical_and(has > 0, ordinal == rows), lanes, 0))
        n_blk = rsum(jnp.where(lanes == N_EXPERTS - 1, end_blk, 0))
        n_exp = rsum(jnp.where(is_e, has, 0))
        own = lanes == be
        left = rsum(jnp.where(own, counts, 0)) - MOE_BLOCK * (rows - rsum(jnp.where(own, start_blk, 0)))
        n_valid = jnp.clip(left, 0, MOE_BLOCK)
        meta_ref[...] = jnp.where(
            lanes == 0, be, jnp.where(lanes == 1, eo, jnp.where(
                lanes == 2, ue, jnp.where(lanes == 3, n_blk, jnp.where(lanes == 4, n_exp, n_valid)))))

    @pl.when(p == 1)
    def _():
        r = lax.broadcasted_iota(jnp.int32, (tb, tb), 0)
        c = lax.broadcasted_iota(jnp.int32, (tb, tb), 1)
        earlier = jnp.where(c < r, 1.0, 0.0).astype(BF16)
        rank = jnp.dot(earlier, onehot.astype(BF16), preferred_element_type=F32)
        rank = rank + prefix_ref[i][0:1] + start_ref[0:1]
        d1 = jnp.sum(jnp.where(lane == e1, rank, 0.0), axis=-1, keepdims=True)
        d2 = jnp.sum(jnp.where(lane == e2, rank, 0.0), axis=-1, keepdims=True)
        dest_ref[...] = jnp.where(lane == 0, d1, d2).astype(jnp.int32)


def _dispatch(idx, tb=512):
    T = idx.shape[0]
    n_rows = T * TOP_K + N_EXPERTS * MOE_BLOCK
    n_blocks = n_rows // MOE_BLOCK
    assert n_blocks <= META_ROWS
    dest2, meta = pl.pallas_call(
        functools.partial(_dispatch_kernel, tb=tb),
        grid=(2, T // tb),
        in_specs=[pl.BlockSpec((tb, LANES), lambda p, i: (i, 0))],
        out_specs=[pl.BlockSpec((tb, LANES), lambda p, i: (i * p, 0)),
                   pl.BlockSpec((META_ROWS, LANES), lambda p, i: (0, 0))],
        out_shape=[jax.ShapeDtypeStruct((T, LANES), jnp.int32),
                   jax.ShapeDtypeStruct((META_ROWS, LANES), jnp.int32)],
        scratch_shapes=[pltpu.VMEM((SUBLANES, LANES), F32),
                        pltpu.VMEM((T // tb, SUBLANES, LANES), F32),
                        pltpu.VMEM((SUBLANES, LANES), F32)],
        compiler_params=_params(("arbitrary", "arbitrary")),
        name="dispatch",
    )(idx)
    dest = dest2[:, :TOP_K].reshape(T * TOP_K)
    tok = jnp.repeat(jnp.arange(T, dtype=jnp.int32), TOP_K)
    row_tok = jnp.zeros((n_rows,), jnp.int32).at[dest].set(tok)
    block_e = meta[:n_blocks, 0]
    block_ord = meta[:n_blocks, 1]
    used_experts = meta[:N_EXPERTS, 2]
    n_used = meta[0, 3:5]
    block_rows = meta[:n_blocks, 5]
    return block_e, block_ord, used_experts, block_rows, row_tok, n_used, dest


def kernel(x, positions, norm1_g, w_in, q_norm_g, k_norm_g, sink_logits, sg_ln_g, sg_ln_b, sg_w, sg_b, w_branch_att, w_branch_sg, w_out, norm2_g, w_group_router, b_group_router, w_expert_router, b_expert_router, w_gate, w_up, w_down):
    B, S, D = x.shape
    T = B * S
    h = x.reshape(T, D)
    pos = positions.reshape(T)
    for l in range(norm1_g.shape[0]):
        xn = _rmsnorm(h, norm1_g[l])
        proj = _in_proj(xn, w_in[l])
        q, k, v = _qkv_prep(proj, pos, q_norm_g[l], k_norm_g[l])
        att = _attention(q, k, v, sink_logits[l], B)
        sgo = _spatial_gating(proj, sg_ln_g[l], sg_ln_b[l], sg_w[l], sg_b[l])
        merged = _merge(att, sgo, proj, w_branch_att[l], w_branch_sg[l])
        pad = LANES - N_GROUPS - N_EXPERTS
        w_router = jnp.concatenate(
            [w_group_router[l], w_expert_router[l], jnp.zeros((D, pad), F32)], axis=1).astype(BF16)
        b_router = jnp.concatenate(
            [b_group_router[l], b_expert_router[l], jnp.zeros((pad,), F32)]).reshape(1, LANES)
        h, hn, idx, wts = _out_proj(merged, w_out[l], h, norm2_g[l], w_router, b_router)
        block_e, block_ord, used_experts, block_rows, row_tok, n_used, dest = _dispatch(idx)
        yrows = _experts(hn, w_gate[l], w_up[l], w_down[l],
                         block_e, block_ord, used_experts, block_rows, row_tok, n_used)
        h = _combine(yrows, h, wts, dest)
    return h.reshape(B, S, D)
```

```python
import functools

import jax
import jax.numpy as jnp
from jax import lax
from jax.experimental import pallas as pl
from jax.experimental.pallas import tpu as pltpu

F32 = jnp.float32
BF16 = jnp.bfloat16

D_MODEL = 2048
HEAD_DIM = 64
ATT_WIDTH = D_MODEL // 2
ATT_HEADS = ATT_WIDTH // HEAD_DIM
ATT_KV_HEADS = ATT_HEADS // 4
Q_PER_KV = ATT_HEADS // ATT_KV_HEADS
KV_WIDTH = ATT_KV_HEADS * HEAD_DIM
WINDOW = 128
ATT_BLOCK = 128
ROPE_DIM = HEAD_DIM // 4
ROPE_HALF = ROPE_DIM // 2
ROPE_THETA = 500000.0
SG_WIDTH = D_MODEL // 2
SG_GROUP_DIM = 128
SG_GROUPS = SG_WIDTH // SG_GROUP_DIM
SG_CHUNK = 128
OFF_Q = 0
OFF_K = OFF_Q + ATT_WIDTH
OFF_V = OFF_K + KV_WIDTH
OFF_U = OFF_V + KV_WIDTH
OFF_S = OFF_U + SG_WIDTH
OFF_G = OFF_S + SG_WIDTH
IN_COLS = OFF_G + 2 * D_MODEL
N_GROUPS = 8
EXPERTS_PER_GROUP = 8
N_EXPERTS = N_GROUPS * EXPERTS_PER_GROUP
TOP_K = 2
EXPERT_FF = D_MODEL // 4
MOE_BLOCK = 128
EPS = 1e-6
NEG_INF = -1e30

LANES = 128
HN_WORD_CHUNKS = D_MODEL // LANES // 2
ROWS_PER_ISSUE = 8
ROW_AHEAD = 4
ROW_SLOTS = ROW_AHEAD + 1
SG_CHUNKS_PER_TRIP = 2
MERGE_ROW_PARTS = 4
OUT_ROW_PARTS = 4
WEIGHT_SLOTS = 3
COMBINE_AHEAD = 1
COMBINE_SLOTS = COMBINE_AHEAD + 1
VMEM_LIMIT = 56 * 1024 * 1024


def _params(sem, vmem=VMEM_LIMIT):
    return pltpu.CompilerParams(dimension_semantics=sem, vmem_limit_bytes=vmem)


def _rmsnorm_kernel(x_ref, g_ref, o_ref):
    x = x_ref[...]
    r = lax.rsqrt(jnp.mean(x * x, axis=-1, keepdims=True) + EPS)
    o_ref[...] = ((x * r) * g_ref[...]).astype(o_ref.dtype)


def _rmsnorm(x, g, tm=1024):
    T, D = x.shape
    return pl.pallas_call(
        _rmsnorm_kernel,
        grid=(T // tm,),
        in_specs=[pl.BlockSpec((tm, D), lambda i: (i, 0)),
                  pl.BlockSpec((1, D), lambda i: (0, 0))],
        out_specs=pl.BlockSpec((tm, D), lambda i: (i, 0)),
        out_shape=jax.ShapeDtypeStruct((T, D), BF16),
        compiler_params=_params(("parallel",)),
        name="norm1",
    )(x, g.reshape(1, D))


def _proj_kernel(x_ref, w_ref, o_ref, wbf_ref):
    @pl.when(pl.program_id(1) == 0)
    def _():
        wbf_ref[...] = w_ref[...].astype(BF16)

    o_ref[...] = jnp.dot(x_ref[...], wbf_ref[...], preferred_element_type=F32)


def _in_proj(xn, w, tm=1024, tn=1280):
    T, D = xn.shape
    N = w.shape[1]
    return pl.pallas_call(
        _proj_kernel,
        grid=(N // tn, T // tm),
        in_specs=[pl.BlockSpec((tm, D), lambda j, i: (i, 0)),
                  pl.BlockSpec((D, tn), lambda j, i: (0, j))],
        out_specs=pl.BlockSpec((tm, tn), lambda j, i: (i, j)),
        out_shape=jax.ShapeDtypeStruct((T, N), F32),
        scratch_shapes=[pltpu.VMEM((D, tn), BF16)],
        compiler_params=_params(("arbitrary", "arbitrary")),
        name="in_proj",
    )(xn, w)


def _rope_table_kernel(pos_ref, invf_ref, cos_ref, sin_ref):
    ang = pos_ref[...].astype(F32) * invf_ref[...]
    cos_ref[...] = jnp.cos(ang)
    sin_ref[...] = jnp.sin(ang)


def _rope_tables(positions):
    T = positions.shape[0]
    rows = T * ROPE_HALF // LANES
    inv = ROPE_THETA ** (-jnp.arange(0, ROPE_DIM, 2, dtype=F32) / ROPE_DIM)
    invf = jnp.tile(inv, LANES // ROPE_HALF).reshape(1, LANES)
    pos = jnp.repeat(positions, ROPE_HALF).reshape(rows, LANES)
    whole = lambda: (0, 0)
    cos, sin = pl.pallas_call(
        _rope_table_kernel,
        in_specs=[pl.BlockSpec((rows, LANES), whole), pl.BlockSpec((1, LANES), whole)],
        out_specs=[pl.BlockSpec((rows, LANES), whole), pl.BlockSpec((rows, LANES), whole)],
        out_shape=[jax.ShapeDtypeStruct((rows, LANES), F32)] * 2,
        name="rope_tables",
    )(pos, invf)
    return cos.reshape(T, ROPE_HALF), sin.reshape(T, ROPE_HALF)


def _qkv_prep_kernel(p_ref, cos_ref, sin_ref, gq_ref, gk_ref, seg_ref, q_ref, k_ref, v_ref):
    cos = cos_ref[...]
    sin = sin_ref[...]
    lane = lax.broadcasted_iota(jnp.int32, cos.shape, 1)
    first_half = (lane % HEAD_DIM) < ROPE_HALF

    def norm_rope(x, g):
        x2 = x * x
        x2_hi = x2.astype(BF16)
        x2_lo = (x2 - x2_hi.astype(F32)).astype(BF16)
        both = jnp.dot(jnp.concatenate([x2_hi, x2_lo], axis=0), seg_ref[...],
                       preferred_element_type=F32)
        ssq = both[:x.shape[0]] + both[x.shape[0]:]
        xn = (x * lax.rsqrt(ssq * (1.0 / HEAD_DIM) + EPS)) * g
        partner = jnp.where(first_half,
                            pltpu.roll(xn, LANES - ROPE_HALF, 1),
                            pltpu.roll(xn, ROPE_HALF, 1))
        return xn * cos + partner * sin

    for c in range(ATT_WIDTH // LANES):
        x = p_ref[:, OFF_Q + c * LANES:OFF_Q + (c + 1) * LANES]
        q_ref[:, c * LANES:(c + 1) * LANES] = (
            norm_rope(x, gq_ref[...]) * (HEAD_DIM ** -0.5)).astype(q_ref.dtype)
    for c in range(KV_WIDTH // LANES):
        x = p_ref[:, OFF_K + c * LANES:OFF_K + (c + 1) * LANES]
        k_ref[:, c * LANES:(c + 1) * LANES] = norm_rope(x, gk_ref[...]).astype(k_ref.dtype)
    v_ref[...] = p_ref[:, OFF_V:OFF_V + KV_WIDTH].astype(v_ref.dtype)


def _qkv_prep(proj, positions, q_g, k_g, tq=512):
    T = proj.shape[0]
    width = OFF_U
    cos8, sin8 = _rope_tables(positions)
    rest = HEAD_DIM - ROPE_DIM
    cos_t = jnp.tile(jnp.concatenate([cos8, cos8, jnp.ones((T, rest), F32)], axis=1),
                     (1, LANES // HEAD_DIM))
    sin_t = jnp.tile(jnp.concatenate([-sin8, sin8, jnp.zeros((T, rest), F32)], axis=1),
                     (1, LANES // HEAD_DIM))
    head_of_lane = jnp.arange(LANES) // HEAD_DIM
    same_head = (head_of_lane[:, None] == head_of_lane[None, :]).astype(BF16)
    gq = jnp.tile(q_g, LANES // HEAD_DIM).reshape(1, LANES)
    gk = jnp.tile(k_g, LANES // HEAD_DIM).reshape(1, LANES)
    row = lambda i: (i, 0)
    const = lambda i: (0, 0)
    return pl.pallas_call(
        _qkv_prep_kernel,
        grid=(T // tq,),
        in_specs=[pl.BlockSpec((tq, width), row),
                  pl.BlockSpec((tq, LANES), row),
                  pl.BlockSpec((tq, LANES), row),
                  pl.BlockSpec((1, LANES), const),
                  pl.BlockSpec((1, LANES), const),
                  pl.BlockSpec((LANES, LANES), const)],
        out_specs=[pl.BlockSpec((tq, ATT_WIDTH), row),
                   pl.BlockSpec((tq, KV_WIDTH), row),
                   pl.BlockSpec((tq, KV_WIDTH), row)],
        out_shape=[jax.ShapeDtypeStruct((T, ATT_WIDTH), BF16),
                   jax.ShapeDtypeStruct((T, KV_WIDTH), BF16),
                   jax.ShapeDtypeStruct((T, KV_WIDTH), BF16)],
        compiler_params=_params(("parallel",)),
        name="qkv_prep",
    )(proj, cos_t, sin_t, gq, gk, same_head)


def _attn_kernel(sink_ref, q_ref, kp_ref, kc_ref, kn_ref, vp_ref, vc_ref, vn_ref, o_ref, *, nb):
    n = pl.program_id(1)
    rows = ATT_BLOCK
    qi = lax.broadcasted_iota(jnp.int32, (rows, ATT_BLOCK), 0) % ATT_BLOCK
    kj = lax.broadcasted_iota(jnp.int32, (rows, ATT_BLOCK), 1)
    lo_prev = jnp.where(n > 0, 0, ATT_BLOCK)
    hi_next = jnp.where(n < nb - 1, 0, -ATT_BLOCK)
    cap_prev = jnp.where(kj - qi >= lo_prev, jnp.inf, NEG_INF)
    cap_next = jnp.where(kj - qi <= hi_next, jnp.inf, NEG_INF)
    cap = jnp.concatenate([cap_prev, jnp.full((rows, ATT_BLOCK), jnp.inf, F32), cap_next], axis=1)

    kvhs = range(ATT_KV_HEADS)
    scores = []
    for kvh in kvhs:
        cols = slice(kvh * HEAD_DIM, (kvh + 1) * HEAD_DIM)
        k = jnp.concatenate([kp_ref[:, cols], kc_ref[:, cols], kn_ref[:, cols]], axis=0)
        q = jnp.concatenate(
            [q_ref[:, (kvh * Q_PER_KV + g) * HEAD_DIM:(kvh * Q_PER_KV + g + 1) * HEAD_DIM]
             for g in range(Q_PER_KV)], axis=0)
        scores.append(lax.dot_general(q, k, (((1,), (1,)), ((), ())),
                                      preferred_element_type=F32))
    probs = []
    for kvh in kvhs:
        strips = []
        for g in range(Q_PER_KV):
            sink = sink_ref[kvh * Q_PER_KV + g]
            sg = jnp.minimum(scores[kvh][g * ATT_BLOCK:(g + 1) * ATT_BLOCK], cap)
            m = jnp.maximum(jnp.max(sg, axis=-1, keepdims=True), sink)
            e = jnp.exp(sg - m)
            denom = jnp.sum(e, axis=-1, keepdims=True) + jnp.exp(sink - m)
            strips.append((e / denom).astype(BF16))
        probs.append(jnp.concatenate(strips, axis=0))
    for kvh in kvhs:
        cols = slice(kvh * HEAD_DIM, (kvh + 1) * HEAD_DIM)
        v = jnp.concatenate([vp_ref[:, cols], vc_ref[:, cols], vn_ref[:, cols]], axis=0)
        o = jnp.dot(probs[kvh], v, preferred_element_type=F32)
        for g in range(Q_PER_KV):
            h = kvh * Q_PER_KV + g
            o_ref[:, h * HEAD_DIM:(h + 1) * HEAD_DIM] = (
                o[g * ATT_BLOCK:(g + 1) * ATT_BLOCK].astype(o_ref.dtype))


def _attention(q, k, v, sink, batch):
    T = q.shape[0]
    nb = T // batch // ATT_BLOCK
    cur = lambda b, n: (b * nb + n, 0)
    prev = lambda b, n: (b * nb + jnp.maximum(n - 1, 0), 0)
    nxt = lambda b, n: (b * nb + jnp.minimum(n + 1, nb - 1), 0)
    kv = lambda im: pl.BlockSpec((ATT_BLOCK, KV_WIDTH), im)
    return pl.pallas_call(
        functools.partial(_attn_kernel, nb=nb),
        grid=(batch, nb),
        in_specs=[pl.BlockSpec(memory_space=pltpu.SMEM),
                  pl.BlockSpec((ATT_BLOCK, ATT_WIDTH), cur),
                  kv(prev), kv(cur), kv(nxt), kv(prev), kv(cur), kv(nxt)],
        out_specs=pl.BlockSpec((ATT_BLOCK, ATT_WIDTH), cur),
        out_shape=jax.ShapeDtypeStruct((T, ATT_WIDTH), BF16),
        compiler_params=_params(("parallel", "parallel")),
        name="window_attn",
    )(sink, q, k, k, k, v, v, v)


def _sg_kernel(u_ref, s_ref, lng_ref, lnb_ref, w_ref, b_ref, o_ref):
    groups = w_ref.shape[0]

    def some_chunks(ci, carry):
        tiles = []
        for j in range(SG_CHUNKS_PER_TRIP):
            start = pl.multiple_of((ci * SG_CHUNKS_PER_TRIP + j) * SG_CHUNK, SG_CHUNK)
            for gi in range(groups):
                tiles.append((pl.ds(start, SG_CHUNK),
                              slice(gi * SG_GROUP_DIM, (gi + 1) * SG_GROUP_DIM), gi))
        ts = range(len(tiles))
        s = [jax.nn.gelu(s_ref[r, c]) for r, c, _ in tiles]
        sc = [s[t] - jnp.mean(s[t], axis=-1, keepdims=True) for t in ts]
        var = [jnp.mean(sc[t] * sc[t], axis=-1, keepdims=True) for t in ts]
        sn = [(sc[t] * lax.rsqrt(var[t] + EPS)) * lng_ref[0, gi:gi + 1, :]
              + lnb_ref[0, gi:gi + 1, :] for t, (_, _, gi) in enumerate(tiles)]
        mixed = [jnp.dot(w_ref[gi], sn[t].astype(BF16), preferred_element_type=F32)
                 + b_ref[0, :, gi:gi + 1] for t, (_, _, gi) in enumerate(tiles)]
        for t, (r, c, _) in enumerate(tiles):
            o_ref[r, c] = (jax.nn.gelu(u_ref[r, c]) * mixed[t]).astype(o_ref.dtype)
        return carry

    lax.fori_loop(0, u_ref.shape[0] // (SG_CHUNK * SG_CHUNKS_PER_TRIP), some_chunks, 0)


def _spatial_gating(proj, ln_g, ln_b, w_s, b_s, halves=2, chunks=8):
    T = proj.shape[0]
    half_w = SG_WIDTH // halves
    gph = SG_GROUPS // halves
    u0, s0 = OFF_U // half_w, OFF_S // half_w
    tq = chunks * SG_CHUNK
    return pl.pallas_call(
        _sg_kernel,
        grid=(T // tq, halves),
        in_specs=[pl.BlockSpec((tq, half_w), lambda i, j: (i, u0 + j)),
                  pl.BlockSpec((tq, half_w), lambda i, j: (i, s0 + j)),
                  pl.BlockSpec((1, gph, SG_GROUP_DIM), lambda i, j: (j, 0, 0)),
                  pl.BlockSpec((1, gph, SG_GROUP_DIM), lambda i, j: (j, 0, 0)),
                  pl.BlockSpec((gph, SG_CHUNK, SG_CHUNK), lambda i, j: (j, 0, 0)),
                  pl.BlockSpec((1, SG_CHUNK, gph), lambda i, j: (j, 0, 0))],
        out_specs=pl.BlockSpec((tq, half_w), lambda i, j: (i, j)),
        out_shape=jax.ShapeDtypeStruct((T, SG_WIDTH), BF16),
        compiler_params=_params(("parallel", "parallel")),
        name="spatial_gating",
    )(proj, proj,
      ln_g.reshape(halves, gph, SG_GROUP_DIM), ln_b.reshape(halves, gph, SG_GROUP_DIM),
      w_s.astype(BF16),
      b_s.reshape(halves, gph, SG_CHUNK).transpose(0, 2, 1))


def _merge_kernel(att_ref, sgo_ref, ga_ref, gb_ref, wa_ref, wb_ref, o_ref, wa_bf, wb_bf):
    @pl.when(pl.program_id(1) == 0)
    def _():
        wa_bf[...] = wa_ref[...].astype(BF16)
        wb_bf[...] = wb_ref[...].astype(BF16)

    part_rows = att_ref.shape[0] // MERGE_ROW_PARTS
    parts = [slice(p * part_rows, (p + 1) * part_rows) for p in range(MERGE_ROW_PARTS)]
    a = [jnp.dot(att_ref[r, :], wa_bf[...], preferred_element_type=F32) for r in parts]
    b = [jnp.dot(sgo_ref[r, :], wb_bf[...], preferred_element_type=F32) for r in parts]
    for p, r in enumerate(parts):
        m = jax.nn.sigmoid(ga_ref[r, :]) * a[p] + jax.nn.sigmoid(gb_ref[r, :]) * b[p]
        o_ref[r, :] = m.astype(o_ref.dtype)


def _merge(att, sgo, proj, w_a, w_b, tm=1024, tn=512):
    T = att.shape[0]
    ga0 = OFF_G // tn
    gb0 = (OFF_G + D_MODEL) // tn
    return pl.pallas_call(
        _merge_kernel,
        grid=(D_MODEL // tn, T // tm),
        in_specs=[pl.BlockSpec((tm, ATT_WIDTH), lambda j, i: (i, 0)),
                  pl.BlockSpec((tm, SG_WIDTH), lambda j, i: (i, 0)),
                  pl.BlockSpec((tm, tn), lambda j, i: (i, ga0 + j)),
                  pl.BlockSpec((tm, tn), lambda j, i: (i, gb0 + j)),
                  pl.BlockSpec((ATT_WIDTH, tn), lambda j, i: (0, j)),
                  pl.BlockSpec((SG_WIDTH, tn), lambda j, i: (0, j))],
        out_specs=pl.BlockSpec((tm, tn), lambda j, i: (i, j)),
        out_shape=jax.ShapeDtypeStruct((T, D_MODEL), BF16),
        scratch_shapes=[pltpu.VMEM((ATT_WIDTH, tn), BF16),
                        pltpu.VMEM((SG_WIDTH, tn), BF16)],
        compiler_params=_params(("arbitrary", "arbitrary")),
        name="merge",
    )(att, sgo, proj, proj, w_a, w_b)


def _route(logits):
    lane = lax.broadcasted_iota(jnp.int32, logits.shape, 1)
    lane_f = lane.astype(F32)
    is_g = lane < N_GROUPS
    gl = jnp.where(is_g, logits, -jnp.inf)
    gmax = jnp.max(gl, axis=-1, keepdims=True)
    grp = jnp.min(jnp.where(gl == gmax, lane_f, float(LANES)), axis=-1, keepdims=True)
    gsum = jnp.sum(jnp.where(is_g, jnp.exp(logits - gmax), 0.0), axis=-1, keepdims=True)
    g_w = 1.0 / gsum
    e_lane = lane - N_GROUPS
    in_grp = jnp.logical_and(
        jnp.logical_and(e_lane >= 0, e_lane < N_EXPERTS),
        (e_lane // EXPERTS_PER_GROUP).astype(F32) == grp)
    el = jnp.where(in_grp, logits, -jnp.inf)
    v1 = jnp.max(el, axis=-1, keepdims=True)
    i1 = jnp.min(jnp.where(jnp.logical_and(in_grp, el == v1), lane_f, float(LANES)),
                 axis=-1, keepdims=True)
    rest = jnp.logical_and(in_grp, lane_f != i1)
    el2 = jnp.where(rest, logits, -jnp.inf)
    v2 = jnp.max(el2, axis=-1, keepdims=True)
    i2 = jnp.min(jnp.where(jnp.logical_and(rest, el2 == v2), lane_f, float(LANES)),
                 axis=-1, keepdims=True)
    e21 = jnp.exp(v2 - v1)
    w1 = g_w / (1.0 + e21)
    w2 = g_w * e21 / (1.0 + e21)
    idx = jnp.where(lane == 0, i1, i2) - float(N_GROUPS)
    wts = jnp.where(lane == 0, w1, jnp.where(lane == 1, w2, 0.0))
    return idx.astype(jnp.int32), wts


def _out_kernel(m_ref, w_ref, x_ref, g_ref, wr_ref, br_ref, h_ref, hn_ref, idx_ref, wt_ref):
    part_rows = m_ref.shape[0] // OUT_ROW_PARTS
    parts = [slice(p * part_rows, (p + 1) * part_rows) for p in range(OUT_ROW_PARTS)]
    hs = [x_ref[r, :] + jnp.dot(m_ref[r, :], w_ref[...], preferred_element_type=F32)
          for r in parts]
    for p, r in enumerate(parts):
        h = hs[p]
        h_ref[r, :] = h
        rs = lax.rsqrt(jnp.mean(h * h, axis=-1, keepdims=True) + EPS)
        hn = ((h * rs) * g_ref[...]).astype(BF16)
        bits = lax.bitcast_convert_type(hn.astype(F32), jnp.uint32)
        for c in range(HN_WORD_CHUNKS):
            hi = bits[:, c * LANES:(c + 1) * LANES]
            lo = bits[:, (c + HN_WORD_CHUNKS) * LANES:(c + HN_WORD_CHUNKS + 1) * LANES]
            hn_ref[pl.ds(p * part_rows * HN_WORD_CHUNKS + c, part_rows,
                         stride=HN_WORD_CHUNKS), :] = hi | (lo >> 16)
        logits = jnp.dot(hn, wr_ref[...], preferred_element_type=F32) + br_ref[...]
        idx, wts = _route(logits)
        idx_ref[r, :] = idx
        wt_ref[r, :] = wts


def _out_proj(merged, w_out, x, g2, w_router, b_router, tm=512):
    T, D = x.shape
    row = lambda i: (i, 0)
    const = lambda i: (0, 0)
    return pl.pallas_call(
        _out_kernel,
        grid=(T // tm,),
        in_specs=[pl.BlockSpec((tm, D), row),
                  pl.BlockSpec((D, D), const),
                  pl.BlockSpec((tm, D), row),
                  pl.BlockSpec((1, D), const),
                  pl.BlockSpec((D, LANES), const),
                  pl.BlockSpec((1, LANES), const)],
        out_specs=[pl.BlockSpec((tm, D), row),
                   pl.BlockSpec((tm * HN_WORD_CHUNKS, LANES), row),
                   pl.BlockSpec((tm, LANES), row),
                   pl.BlockSpec((tm, LANES), row)],
        out_shape=[jax.ShapeDtypeStruct((T, D), F32),
                   jax.ShapeDtypeStruct((T * HN_WORD_CHUNKS, LANES), jnp.uint32),
                   jax.ShapeDtypeStruct((T, LANES), jnp.int32),
                   jax.ShapeDtypeStruct((T, LANES), F32)],
        compiler_params=_params(("parallel",)),
        name="out_proj_router",
    )(merged, w_out.astype(BF16), x, g2.reshape(1, D), w_router, b_router)


def _expert_kernel(be_ref, eo_ref, ue_ref, nr_ref, bo_ref, rt_ref, nu_ref,
                   hn_hbm, wg_hbm, wu_hbm, wd_hbm, y_ref, xbuf, wgf, wuf, wdf, sem, wsem):
    b = pl.program_id(0)
    n_used = nu_ref[0]
    n_exp = nu_ref[1]
    used = b < n_used
    slot = b % ROW_SLOTS

    def weight_copies(ordinal, slot_):
        e = ue_ref[ordinal]
        return (pltpu.make_async_copy(wg_hbm.at[e], wgf.at[slot_], wsem.at[slot_]),
                pltpu.make_async_copy(wu_hbm.at[e], wuf.at[slot_], wsem.at[slot_]),
                pltpu.make_async_copy(wd_hbm.at[e], wdf.at[slot_], wsem.at[slot_]))

    def start_weights(ordinal):
        for c in weight_copies(ordinal, ordinal % WEIGHT_SLOTS):
            c.start(priority=1)

    def issue_trips(blk):
        return (nr_ref[blk] + (ROWS_PER_ISSUE - 1)) // ROWS_PER_ISSUE

    def gather_rows(blk, slot_):
        first_row = bo_ref[blk]

        def issue(g, c):
            for j in range(ROWS_PER_ISSUE):
                r = g * ROWS_PER_ISSUE + j
                src = pl.multiple_of(rt_ref[first_row + r] * HN_WORD_CHUNKS, HN_WORD_CHUNKS)
                dst = pl.multiple_of(r * HN_WORD_CHUNKS, HN_WORD_CHUNKS)
                pltpu.make_async_copy(hn_hbm.at[pl.ds(src, HN_WORD_CHUNKS)],
                                      xbuf.at[slot_, pl.ds(dst, HN_WORD_CHUNKS)],
                                      sem.at[slot_]).start()
            return c

        lax.fori_loop(0, issue_trips(blk), issue, 0)

    @pl.when(b == 0)
    def _():
        start_weights(0)

        @pl.when(n_exp > 1)
        def _():
            start_weights(1)

        xbuf[...] = jnp.zeros(xbuf.shape, xbuf.dtype)
        gather_rows(0, 0)
        for a in range(1, ROW_AHEAD):
            @pl.when(n_used > a)
            def _(a=a):
                gather_rows(a, a)

    @pl.when(b + ROW_AHEAD < n_used)
    def _():
        gather_rows(b + ROW_AHEAD, (b + ROW_AHEAD) % ROW_SLOTS)

    @pl.when(used)
    def _():
        first = jnp.logical_or(b == 0, be_ref[b] != be_ref[jnp.maximum(b - 1, 0)])

        ordinal = eo_ref[b]
        ws = ordinal % WEIGHT_SLOTS

        @pl.when(first)
        def _():
            for c in weight_copies(ordinal, ws):
                c.wait()

            @pl.when(ordinal + 2 < n_exp)
            def _():
                start_weights(ordinal + 2)

        words = issue_trips(b) * (ROWS_PER_ISSUE * HN_WORD_CHUNKS)
        pltpu.make_async_copy(hn_hbm.at[pl.ds(0, words)], xbuf.at[slot, pl.ds(0, words)],
                              sem.at[slot]).wait()

        halves = ([], [])
        for c in range(HN_WORD_CHUNKS):
            w = xbuf[slot, pl.ds(c, MOE_BLOCK, stride=HN_WORD_CHUNKS), :]
            hi = lax.bitcast_convert_type(w & jnp.uint32(0xFFFF0000), F32)
            lo = lax.bitcast_convert_type(w << 16, F32)
            halves[0].append(hi.astype(BF16))
            halves[1].append(lo.astype(BF16))
        x = jnp.concatenate(halves[0] + halves[1], axis=1)
        hg = jnp.dot(x, wgf[ws].astype(BF16), preferred_element_type=F32)
        hu = jnp.dot(x, wuf[ws].astype(BF16), preferred_element_type=F32)
        hdn = (jax.nn.silu(hg) * hu).astype(BF16)
        y = jnp.dot(hdn, wdf[ws].astype(BF16), preferred_element_type=F32)
        bits = lax.bitcast_convert_type(y.astype(BF16).astype(F32), jnp.uint32)
        for c in range(HN_WORD_CHUNKS):
            hi = bits[:, c * LANES:(c + 1) * LANES]
            lo = bits[:, (c + HN_WORD_CHUNKS) * LANES:(c + HN_WORD_CHUNKS + 1) * LANES]
            y_ref[pl.ds(c, MOE_BLOCK, stride=HN_WORD_CHUNKS), :] = hi | (lo >> 16)

    @pl.when(jnp.logical_not(used))
    def _():
        y_ref[...] = jnp.zeros(y_ref.shape, y_ref.dtype)


def _experts(hn, w_gate, w_up, w_down, block_e, block_ord, used_experts, block_rows,
             block_offset, row_tok, n_used):
    D = w_gate.shape[1]
    n_blocks = block_e.shape[0]
    n_rows = n_blocks * MOE_BLOCK
    hbm = pl.BlockSpec(memory_space=pl.ANY)
    grid_spec = pltpu.PrefetchScalarGridSpec(
        num_scalar_prefetch=7,
        grid=(n_blocks,),
        in_specs=[hbm, hbm, hbm, hbm],
        out_specs=pl.BlockSpec((MOE_BLOCK * HN_WORD_CHUNKS, LANES), lambda b, *_: (b, 0)),
        scratch_shapes=[pltpu.VMEM((ROW_SLOTS, MOE_BLOCK * HN_WORD_CHUNKS, LANES), jnp.uint32),
                        pltpu.VMEM((WEIGHT_SLOTS, D, EXPERT_FF), F32),
                        pltpu.VMEM((WEIGHT_SLOTS, D, EXPERT_FF), F32),
                        pltpu.VMEM((WEIGHT_SLOTS, EXPERT_FF, D), F32),
                        pltpu.SemaphoreType.DMA((ROW_SLOTS,)),
                        pltpu.SemaphoreType.DMA((WEIGHT_SLOTS,))],
    )
    return pl.pallas_call(
        _expert_kernel,
        grid_spec=grid_spec,
        out_shape=jax.ShapeDtypeStruct((n_rows * HN_WORD_CHUNKS, LANES), jnp.uint32),
        compiler_params=_params(("arbitrary",)),
        name="experts",
    )(block_e, block_ord, used_experts, block_rows, block_offset, row_tok, n_used,
      hn, w_gate, w_up, w_down)


def _combine_kernel(*refs, tc):
    dest_refs = refs[:COMBINE_SLOTS]
    y_hbm, h_ref, wt_ref, o_ref, ybuf, sem = refs[COMBINE_SLOTS:]
    i = pl.program_id(0)
    slot = i % COMBINE_SLOTS

    def gather_rows(dest_ref, slot_):
        def issue(r, c):
            dst = pl.multiple_of(r * HN_WORD_CHUNKS, HN_WORD_CHUNKS)
            for k in range(TOP_K):
                src = pl.multiple_of(dest_ref[0, 0, r * TOP_K + k] * HN_WORD_CHUNKS, HN_WORD_CHUNKS)
                pltpu.make_async_copy(y_hbm.at[pl.ds(src, HN_WORD_CHUNKS)],
                                      ybuf.at[slot_, k, pl.ds(dst, HN_WORD_CHUNKS)],
                                      sem.at[slot_]).start(priority=k % 2)
            return c

        lax.fori_loop(0, tc, issue, 0, unroll=4)

    @pl.when(i == 0)
    def _():
        for a in range(COMBINE_AHEAD):
            gather_rows(dest_refs[a], a)

    @pl.when(i + COMBINE_AHEAD < pl.num_programs(0))
    def _():
        gather_rows(dest_refs[COMBINE_AHEAD], (i + COMBINE_AHEAD) % COMBINE_SLOTS)

    for k in range(TOP_K):
        pltpu.make_async_copy(y_hbm.at[pl.ds(0, tc * HN_WORD_CHUNKS)], ybuf.at[slot, k],
                              sem.at[slot]).wait()
    wt = wt_ref[...]
    w1, w2 = wt[:, 0:1], wt[:, 1:2]
    for c in range(HN_WORD_CHUNKS):
        rows = pl.ds(c, tc, stride=HN_WORD_CHUNKS)
        words = [ybuf[slot, k, rows, :] for k in range(TOP_K)]
        hi = [lax.bitcast_convert_type(w & jnp.uint32(0xFFFF0000), F32) for w in words]
        lo = [lax.bitcast_convert_type(w << 16, F32) for w in words]
        for col, y in ((c * LANES, hi), ((c + HN_WORD_CHUNKS) * LANES, lo)):
            o_ref[:, col:col + LANES] = h_ref[:, col:col + LANES] + (w1 * y[0] + w2 * y[1])


def _combine(yrows, h, wts, dest, tc=256):
    T, D = h.shape
    steps = T // tc
    row = lambda i: (i, 0)
    dest3 = dest.reshape(steps, 1, tc * TOP_K)
    dest_block = lambda im: pl.BlockSpec((1, 1, tc * TOP_K), im, memory_space=pltpu.SMEM)
    return pl.pallas_call(
        functools.partial(_combine_kernel, tc=tc),
        grid=(steps,),
        in_specs=[dest_block(lambda i, a=a: (jnp.minimum(i + a, steps - 1), 0, 0))
                  for a in range(COMBINE_SLOTS)]
                 + [pl.BlockSpec(memory_space=pl.ANY),
                    pl.BlockSpec((tc, D), row),
                    pl.BlockSpec((tc, LANES), row)],
        out_specs=pl.BlockSpec((tc, D), row),
        out_shape=jax.ShapeDtypeStruct((T, D), F32),
        scratch_shapes=[pltpu.VMEM((COMBINE_SLOTS, TOP_K, tc * HN_WORD_CHUNKS, LANES), jnp.uint32),
                        pltpu.SemaphoreType.DMA((COMBINE_SLOTS,))],
        compiler_params=_params(("arbitrary",)),
        name="combine",
    )(*([dest3] * COMBINE_SLOTS), yrows, h, wts)


SUBLANES = 8
META_ROWS = 256

def _lane_cumsum(x):
    lane = lax.broadcasted_iota(jnp.int32, x.shape, 1)
    s = 1
    while s < LANES:
        x = x + jnp.where(lane >= s, pltpu.roll(x, s, 1), 0)
        s *= 2
    return x


def _dispatch_kernel(idx_ref, dest_ref, meta_ref, run_ref, prefix_ref, start_ref, *, tb):
    p = pl.program_id(0)
    i = pl.program_id(1)
    idx = idx_ref[...]
    lane = lax.broadcasted_iota(jnp.int32, idx.shape, 1)
    e1 = idx[:, 0:1]
    e2 = idx[:, 1:2]
    onehot = jnp.where(jnp.logical_or(lane == e1, lane == e2), 1.0, 0.0)

    @pl.when(jnp.logical_and(p == 0, i == 0))
    def _():
        run_ref[...] = jnp.zeros(run_ref.shape, F32)

    @pl.when(p == 0)
    def _():
        prefix_ref[i] = run_ref[...]
        run_ref[...] = run_ref[...] + jnp.sum(onehot, axis=0, keepdims=True)

    @pl.when(jnp.logical_and(p == 1, i == 0))
    def _():
        counts = run_ref[...].astype(jnp.int32)
        nblk = (counts + (MOE_BLOCK - 1)) // MOE_BLOCK
        end_blk = _lane_cumsum(nblk)
        start_ref[...] = ((end_blk - nblk) * MOE_BLOCK).astype(F32)
        has = jnp.where(counts > 0, 1, 0)
        ordinal = _lane_cumsum(has) - 1
        first_row = (_lane_cumsum(counts) - counts)[0:1]
        start_blk, counts = (end_blk - nblk)[0:1], counts[0:1]
        end_blk, has, ordinal = end_blk[0:1], has[0:1], ordinal[0:1]
        rows = lax.broadcasted_iota(jnp.int32, (META_ROWS, LANES), 0)
        lanes = lax.broadcasted_iota(jnp.int32, (META_ROWS, LANES), 1)
        is_e = lanes < N_EXPERTS
        rsum = lambda v: jnp.sum(v, axis=-1, keepdims=True)
        be = rsum(jnp.where(jnp.logical_and(is_e, end_blk <= rows), 1, 0))
        be = jnp.minimum(be, N_EXPERTS - 1)
        eo = rsum(jnp.where(lanes == be, ordinal, 0))
        ue = rsum(jnp.where(jnp.logical_and(has > 0, ordinal == rows), lanes, 0))
        n_blk = rsum(jnp.where(lanes == N_EXPERTS - 1, end_blk, 0))
        n_exp = rsum(jnp.where(is_e, has, 0))
        own = lanes == be
        done = MOE_BLOCK * (rows - rsum(jnp.where(own, start_blk, 0)))
        n_valid = jnp.clip(rsum(jnp.where(own, counts, 0)) - done, 0, MOE_BLOCK)
        offset = rsum(jnp.where(own, first_row, 0)) + done
        columns = (be, eo, ue, n_blk, n_exp, n_valid, offset)
        meta = jnp.zeros((META_ROWS, LANES), jnp.int32)
        for col, val in enumerate(columns):
            meta = jnp.where(lanes == col, val, meta)
        meta_ref[...] = meta

    @pl.when(p == 1)
    def _():
        r = lax.broadcasted_iota(jnp.int32, (tb, tb), 0)
        c = lax.broadcasted_iota(jnp.int32, (tb, tb), 1)
        earlier = jnp.where(c < r, 1.0, 0.0).astype(BF16)
        rank = jnp.dot(earlier, onehot.astype(BF16), preferred_element_type=F32)
        rank = rank + prefix_ref[i][0:1] + start_ref[0:1]
        d1 = jnp.sum(jnp.where(lane == e1, rank, 0.0), axis=-1, keepdims=True)
        d2 = jnp.sum(jnp.where(lane == e2, rank, 0.0), axis=-1, keepdims=True)
        dest_ref[...] = jnp.where(lane == 0, d1, d2).astype(jnp.int32)


def _dispatch(idx, tb=512):
    T = idx.shape[0]
    n_rows = T * TOP_K + N_EXPERTS * MOE_BLOCK
    n_blocks = n_rows // MOE_BLOCK
    assert n_blocks <= META_ROWS
    dest2, meta = pl.pallas_call(
        functools.partial(_dispatch_kernel, tb=tb),
        grid=(2, T // tb),
        in_specs=[pl.BlockSpec((tb, LANES), lambda p, i: (i, 0))],
        out_specs=[pl.BlockSpec((tb, LANES), lambda p, i: (i * p, 0)),
                   pl.BlockSpec((META_ROWS, LANES), lambda p, i: (0, 0))],
        out_shape=[jax.ShapeDtypeStruct((T, LANES), jnp.int32),
                   jax.ShapeDtypeStruct((META_ROWS, LANES), jnp.int32)],
        scratch_shapes=[pltpu.VMEM((SUBLANES, LANES), F32),
                        pltpu.VMEM((T // tb, SUBLANES, LANES), F32),
                        pltpu.VMEM((SUBLANES, LANES), F32)],
        compiler_params=_params(("arbitrary", "arbitrary")),
        name="dispatch",
    )(idx)
    dest = dest2[:, :TOP_K].reshape(T * TOP_K)
    row_tok = (jnp.argsort(dest) // TOP_K).astype(jnp.int32)
    row_tok = jnp.concatenate([row_tok, jnp.zeros((ROWS_PER_ISSUE,), jnp.int32)])
    block_e = meta[:n_blocks, 0]
    block_ord = meta[:n_blocks, 1]
    used_experts = meta[:N_EXPERTS, 2]
    n_used = meta[0, 3:5]
    block_rows = meta[:n_blocks, 5]
    block_offset = meta[:n_blocks, 6]
    return block_e, block_ord, used_experts, block_rows, block_offset, row_tok, n_used, dest


def kernel(x, positions, norm1_g, w_in, q_norm_g, k_norm_g, sink_logits, sg_ln_g, sg_ln_b, sg_w, sg_b, w_branch_att, w_branch_sg, w_out, norm2_g, w_group_router, b_group_router, w_expert_router, b_expert_router, w_gate, w_up, w_down):
    B, S, D = x.shape
    T = B * S
    h = x.reshape(T, D)
    pos = positions.reshape(T)
    for l in range(norm1_g.shape[0]):
        xn = _rmsnorm(h, norm1_g[l])
        proj = _in_proj(xn, w_in[l])
        q, k, v = _qkv_prep(proj, pos, q_norm_g[l], k_norm_g[l])
        att = _attention(q, k, v, sink_logits[l], B)
        sgo = _spatial_gating(proj, sg_ln_g[l], sg_ln_b[l], sg_w[l], sg_b[l])
        merged = _merge(att, sgo, proj, w_branch_att[l], w_branch_sg[l])
        pad = LANES - N_GROUPS - N_EXPERTS
        w_router = jnp.concatenate(
            [w_group_router[l], w_expert_router[l], jnp.zeros((D, pad), F32)], axis=1).astype(BF16)
        b_router = jnp.concatenate(
            [b_group_router[l], b_expert_router[l], jnp.zeros((pad,), F32)]).reshape(1, LANES)
        h, hn, idx, wts = _out_proj(merged, w_out[l], h, norm2_g[l], w_router, b_router)
        (block_e, block_ord, used_experts, block_rows, block_offset, row_tok, n_used,
         dest) = _dispatch(idx)
        yrows = _experts(hn, w_gate[l], w_up[l], w_down[l], block_e, block_ord, used_experts,
                         block_rows, block_offset, row_tok, n_used)
        h = _combine(yrows, h, wts, dest)
    return h.reshape(B, S, D)
```

```python
import functools

import jax
import jax.numpy as jnp
from jax import lax
from jax.experimental import pallas as pl
from jax.experimental.pallas import tpu as pltpu

F32 = jnp.float32
BF16 = jnp.bfloat16

D_MODEL = 2048
HEAD_DIM = 64
ATT_WIDTH = D_MODEL // 2
ATT_HEADS = ATT_WIDTH // HEAD_DIM
ATT_KV_HEADS = ATT_HEADS // 4
Q_PER_KV = ATT_HEADS // ATT_KV_HEADS
KV_WIDTH = ATT_KV_HEADS * HEAD_DIM
WINDOW = 128
ATT_BLOCK = 128
ROPE_DIM = HEAD_DIM // 4
ROPE_HALF = ROPE_DIM // 2
ROPE_THETA = 500000.0
SG_WIDTH = D_MODEL // 2
SG_GROUP_DIM = 128
SG_GROUPS = SG_WIDTH // SG_GROUP_DIM
SG_CHUNK = 128
OFF_Q = 0
OFF_K = OFF_Q + ATT_WIDTH
OFF_V = OFF_K + KV_WIDTH
OFF_U = OFF_V + KV_WIDTH
OFF_S = OFF_U + SG_WIDTH
OFF_G = OFF_S + SG_WIDTH
IN_COLS = OFF_G + 2 * D_MODEL
N_GROUPS = 8
EXPERTS_PER_GROUP = 8
N_EXPERTS = N_GROUPS * EXPERTS_PER_GROUP
TOP_K = 2
EXPERT_FF = D_MODEL // 4
MOE_BLOCK = 128
EPS = 1e-6
NEG_INF = -1e30

LANES = 128
HN_WORD_CHUNKS = D_MODEL // LANES // 2
ROWS_PER_ISSUE = 8
ROW_AHEAD = 4
ROW_SLOTS = ROW_AHEAD + 1
SG_CHUNKS_PER_TRIP = 2
GATE_BLOCK = 512
MERGE_GATE_BLOCKS = 2
MERGE_ROW_PARTS = 2
OUT_ROW_PARTS = 4
WEIGHT_SLOTS = 3
COMBINE_AHEAD = 1
COMBINE_SLOTS = COMBINE_AHEAD + 1
VMEM_LIMIT = 56 * 1024 * 1024


def _params(sem, vmem=VMEM_LIMIT):
    return pltpu.CompilerParams(dimension_semantics=sem, vmem_limit_bytes=vmem)


def _rmsnorm_kernel(x_ref, g_ref, o_ref):
    x = x_ref[...]
    r = lax.rsqrt(jnp.mean(x * x, axis=-1, keepdims=True) + EPS)
    o_ref[...] = ((x * r) * g_ref[...]).astype(o_ref.dtype)


def _rmsnorm(x, g, tm=1024):
    T, D = x.shape
    return pl.pallas_call(
        _rmsnorm_kernel,
        grid=(T // tm,),
        in_specs=[pl.BlockSpec((tm, D), lambda i: (i, 0)),
                  pl.BlockSpec((1, D), lambda i: (0, 0))],
        out_specs=pl.BlockSpec((tm, D), lambda i: (i, 0)),
        out_shape=jax.ShapeDtypeStruct((T, D), BF16),
        compiler_params=_params(("parallel",)),
        name="norm1",
    )(x, g.reshape(1, D))


def _proj_kernel(x_ref, w_ref, o_ref, wbf_ref):
    @pl.when(pl.program_id(1) == 0)
    def _():
        wbf_ref[...] = w_ref[...].astype(BF16)

    o_ref[...] = jnp.dot(x_ref[...], wbf_ref[...], preferred_element_type=F32)


def _in_proj(xn, w, tm=1024, tn=1280):
    T, D = xn.shape
    N = w.shape[1]
    return pl.pallas_call(
        _proj_kernel,
        grid=(N // tn, T // tm),
        in_specs=[pl.BlockSpec((tm, D), lambda j, i: (i, 0)),
                  pl.BlockSpec((D, tn), lambda j, i: (0, j))],
        out_specs=pl.BlockSpec((tm, tn), lambda j, i: (i, j)),
        out_shape=jax.ShapeDtypeStruct((T, N), F32),
        scratch_shapes=[pltpu.VMEM((D, tn), BF16)],
        compiler_params=_params(("arbitrary", "arbitrary")),
        name="in_proj",
    )(xn, w)


def _rope_table_kernel(pos_ref, invf_ref, cos_ref, sin_ref):
    ang = pos_ref[...].astype(F32) * invf_ref[...]
    cos_ref[...] = jnp.cos(ang)
    sin_ref[...] = jnp.sin(ang)


def _rope_tables(positions):
    T = positions.shape[0]
    rows = T * ROPE_HALF // LANES
    inv = ROPE_THETA ** (-jnp.arange(0, ROPE_DIM, 2, dtype=F32) / ROPE_DIM)
    invf = jnp.tile(inv, LANES // ROPE_HALF).reshape(1, LANES)
    pos = jnp.repeat(positions, ROPE_HALF).reshape(rows, LANES)
    whole = lambda: (0, 0)
    cos, sin = pl.pallas_call(
        _rope_table_kernel,
        in_specs=[pl.BlockSpec((rows, LANES), whole), pl.BlockSpec((1, LANES), whole)],
        out_specs=[pl.BlockSpec((rows, LANES), whole), pl.BlockSpec((rows, LANES), whole)],
        out_shape=[jax.ShapeDtypeStruct((rows, LANES), F32)] * 2,
        name="rope_tables",
    )(pos, invf)
    return cos.reshape(T, ROPE_HALF), sin.reshape(T, ROPE_HALF)


def _qkv_prep_kernel(p_ref, cos_ref, sin_ref, gq_ref, gk_ref, seg_ref, q_ref, k_ref, v_ref):
    cos = cos_ref[...]
    sin = sin_ref[...]
    lane = lax.broadcasted_iota(jnp.int32, cos.shape, 1)
    first_half = (lane % HEAD_DIM) < ROPE_HALF

    def norm_rope(x, g):
        x2 = x * x
        x2_hi = x2.astype(BF16)
        x2_lo = (x2 - x2_hi.astype(F32)).astype(BF16)
        both = jnp.dot(jnp.concatenate([x2_hi, x2_lo], axis=0), seg_ref[...],
                       preferred_element_type=F32)
        ssq = both[:x.shape[0]] + both[x.shape[0]:]
        xn = (x * lax.rsqrt(ssq * (1.0 / HEAD_DIM) + EPS)) * g
        partner = jnp.where(first_half,
                            pltpu.roll(xn, LANES - ROPE_HALF, 1),
                            pltpu.roll(xn, ROPE_HALF, 1))
        return xn * cos + partner * sin

    for c in range(ATT_WIDTH // LANES):
        x = p_ref[:, OFF_Q + c * LANES:OFF_Q + (c + 1) * LANES]
        q_ref[:, c * LANES:(c + 1) * LANES] = (
            norm_rope(x, gq_ref[...]) * (HEAD_DIM ** -0.5)).astype(q_ref.dtype)
    for c in range(KV_WIDTH // LANES):
        x = p_ref[:, OFF_K + c * LANES:OFF_K + (c + 1) * LANES]
        k_ref[:, c * LANES:(c + 1) * LANES] = norm_rope(x, gk_ref[...]).astype(k_ref.dtype)
    v_ref[...] = p_ref[:, OFF_V:OFF_V + KV_WIDTH].astype(v_ref.dtype)


def _qkv_prep(proj, positions, q_g, k_g, tq=512):
    T = proj.shape[0]
    width = OFF_U
    cos8, sin8 = _rope_tables(positions)
    rest = HEAD_DIM - ROPE_DIM
    cos_t = jnp.tile(jnp.concatenate([cos8, cos8, jnp.ones((T, rest), F32)], axis=1),
                     (1, LANES // HEAD_DIM))
    sin_t = jnp.tile(jnp.concatenate([-sin8, sin8, jnp.zeros((T, rest), F32)], axis=1),
                     (1, LANES // HEAD_DIM))
    head_of_lane = jnp.arange(LANES) // HEAD_DIM
    same_head = (head_of_lane[:, None] == head_of_lane[None, :]).astype(BF16)
    gq = jnp.tile(q_g, LANES // HEAD_DIM).reshape(1, LANES)
    gk = jnp.tile(k_g, LANES // HEAD_DIM).reshape(1, LANES)
    row = lambda i: (i, 0)
    const = lambda i: (0, 0)
    return pl.pallas_call(
        _qkv_prep_kernel,
        grid=(T // tq,),
        in_specs=[pl.BlockSpec((tq, width), row),
                  pl.BlockSpec((tq, LANES), row),
                  pl.BlockSpec((tq, LANES), row),
                  pl.BlockSpec((1, LANES), const),
                  pl.BlockSpec((1, LANES), const),
                  pl.BlockSpec((LANES, LANES), const)],
        out_specs=[pl.BlockSpec((tq, ATT_WIDTH), row),
                   pl.BlockSpec((tq, KV_WIDTH), row),
                   pl.BlockSpec((tq, KV_WIDTH), row)],
        out_shape=[jax.ShapeDtypeStruct((T, ATT_WIDTH), BF16),
                   jax.ShapeDtypeStruct((T, KV_WIDTH), BF16),
                   jax.ShapeDtypeStruct((T, KV_WIDTH), BF16)],
        compiler_params=_params(("parallel",)),
        name="qkv_prep",
    )(proj, cos_t, sin_t, gq, gk, same_head)


def _attn_kernel(sink_ref, q_ref, kp_ref, kc_ref, kn_ref, vp_ref, vc_ref, vn_ref, o_ref, *, nb):
    n = pl.program_id(1)
    rows = ATT_BLOCK
    qi = lax.broadcasted_iota(jnp.int32, (rows, ATT_BLOCK), 0) % ATT_BLOCK
    kj = lax.broadcasted_iota(jnp.int32, (rows, ATT_BLOCK), 1)
    lo_prev = jnp.where(n > 0, 0, ATT_BLOCK)
    hi_next = jnp.where(n < nb - 1, 0, -ATT_BLOCK)
    cap_prev = jnp.where(kj - qi >= lo_prev, jnp.inf, NEG_INF)
    cap_next = jnp.where(kj - qi <= hi_next, jnp.inf, NEG_INF)
    cap = jnp.concatenate([cap_prev, jnp.full((rows, ATT_BLOCK), jnp.inf, F32), cap_next], axis=1)

    kvhs = range(ATT_KV_HEADS)
    scores = []
    for kvh in kvhs:
        cols = slice(kvh * HEAD_DIM, (kvh + 1) * HEAD_DIM)
        k = jnp.concatenate([kp_ref[:, cols], kc_ref[:, cols], kn_ref[:, cols]], axis=0)
        q = jnp.concatenate(
            [q_ref[:, (kvh * Q_PER_KV + g) * HEAD_DIM:(kvh * Q_PER_KV + g + 1) * HEAD_DIM]
             for g in range(Q_PER_KV)], axis=0)
        scores.append(lax.dot_general(q, k, (((1,), (1,)), ((), ())),
                                      preferred_element_type=F32))
    probs = []
    for kvh in kvhs:
        strips = []
        for g in range(Q_PER_KV):
            sink = sink_ref[kvh * Q_PER_KV + g]
            sg = jnp.minimum(scores[kvh][g * ATT_BLOCK:(g + 1) * ATT_BLOCK], cap)
            m = jnp.maximum(jnp.max(sg, axis=-1, keepdims=True), sink)
            e = jnp.exp(sg - m)
            denom = jnp.sum(e, axis=-1, keepdims=True) + jnp.exp(sink - m)
            strips.append((e / denom).astype(BF16))
        probs.append(jnp.concatenate(strips, axis=0))
    for kvh in kvhs:
        cols = slice(kvh * HEAD_DIM, (kvh + 1) * HEAD_DIM)
        v = jnp.concatenate([vp_ref[:, cols], vc_ref[:, cols], vn_ref[:, cols]], axis=0)
        o = jnp.dot(probs[kvh], v, preferred_element_type=F32)
        for g in range(Q_PER_KV):
            h = kvh * Q_PER_KV + g
            o_ref[:, h * HEAD_DIM:(h + 1) * HEAD_DIM] = (
                o[g * ATT_BLOCK:(g + 1) * ATT_BLOCK].astype(o_ref.dtype))


def _attention(q, k, v, sink, batch):
    T = q.shape[0]
    nb = T // batch // ATT_BLOCK
    cur = lambda b, n: (b * nb + n, 0)
    prev = lambda b, n: (b * nb + jnp.maximum(n - 1, 0), 0)
    nxt = lambda b, n: (b * nb + jnp.minimum(n + 1, nb - 1), 0)
    kv = lambda im: pl.BlockSpec((ATT_BLOCK, KV_WIDTH), im)
    return pl.pallas_call(
        functools.partial(_attn_kernel, nb=nb),
        grid=(batch, nb),
        in_specs=[pl.BlockSpec(memory_space=pltpu.SMEM),
                  pl.BlockSpec((ATT_BLOCK, ATT_WIDTH), cur),
                  kv(prev), kv(cur), kv(nxt), kv(prev), kv(cur), kv(nxt)],
        out_specs=pl.BlockSpec((ATT_BLOCK, ATT_WIDTH), cur),
        out_shape=jax.ShapeDtypeStruct((T, ATT_WIDTH), BF16),
        compiler_params=_params(("parallel", "parallel")),
        name="window_attn",
    )(sink, q, k, k, k, v, v, v)


def _sg_kernel(u_ref, s_ref, lng_ref, lnb_ref, w_ref, b_ref, o_ref):
    groups = w_ref.shape[0]

    def some_chunks(ci, carry):
        tiles = []
        for j in range(SG_CHUNKS_PER_TRIP):
            start = pl.multiple_of((ci * SG_CHUNKS_PER_TRIP + j) * SG_CHUNK, SG_CHUNK)
            for gi in range(groups):
                tiles.append((pl.ds(start, SG_CHUNK),
                              slice(gi * SG_GROUP_DIM, (gi + 1) * SG_GROUP_DIM), gi))
        ts = range(len(tiles))
        s = [jax.nn.gelu(s_ref[r, c]) for r, c, _ in tiles]
        sc = [s[t] - jnp.mean(s[t], axis=-1, keepdims=True) for t in ts]
        var = [jnp.mean(sc[t] * sc[t], axis=-1, keepdims=True) for t in ts]
        sn = [(sc[t] * lax.rsqrt(var[t] + EPS)) * lng_ref[0, gi:gi + 1, :]
              + lnb_ref[0, gi:gi + 1, :] for t, (_, _, gi) in enumerate(tiles)]
        mixed = [jnp.dot(w_ref[gi], sn[t].astype(BF16), preferred_element_type=F32)
                 + b_ref[0, :, gi:gi + 1] for t, (_, _, gi) in enumerate(tiles)]
        for t, (r, c, _) in enumerate(tiles):
            o_ref[r, c] = (jax.nn.gelu(u_ref[r, c]) * mixed[t]).astype(o_ref.dtype)
        return carry

    lax.fori_loop(0, u_ref.shape[0] // (SG_CHUNK * SG_CHUNKS_PER_TRIP), some_chunks, 0)


def _spatial_gating(proj, ln_g, ln_b, w_s, b_s, halves=2, chunks=8):
    T = proj.shape[0]
    half_w = SG_WIDTH // halves
    gph = SG_GROUPS // halves
    u0, s0 = OFF_U // half_w, OFF_S // half_w
    tq = chunks * SG_CHUNK
    return pl.pallas_call(
        _sg_kernel,
        grid=(T // tq, halves),
        in_specs=[pl.BlockSpec((tq, half_w), lambda i, j: (i, u0 + j)),
                  pl.BlockSpec((tq, half_w), lambda i, j: (i, s0 + j)),
                  pl.BlockSpec((1, gph, SG_GROUP_DIM), lambda i, j: (j, 0, 0)),
                  pl.BlockSpec((1, gph, SG_GROUP_DIM), lambda i, j: (j, 0, 0)),
                  pl.BlockSpec((gph, SG_CHUNK, SG_CHUNK), lambda i, j: (j, 0, 0)),
                  pl.BlockSpec((1, SG_CHUNK, gph), lambda i, j: (j, 0, 0))],
        out_specs=pl.BlockSpec((tq, half_w), lambda i, j: (i, j)),
        out_shape=jax.ShapeDtypeStruct((T, SG_WIDTH), BF16),
        compiler_params=_params(("parallel", "parallel")),
        name="spatial_gating",
    )(proj, proj,
      ln_g.reshape(halves, gph, SG_GROUP_DIM), ln_b.reshape(halves, gph, SG_GROUP_DIM),
      w_s.astype(BF16),
      b_s.reshape(halves, gph, SG_CHUNK).transpose(0, 2, 1))


def _merge_kernel(att_ref, sgo_ref, *refs):
    gate_refs = refs[:2 * MERGE_GATE_BLOCKS]
    wa_ref, wb_ref, o_ref, wa_bf, wb_bf = refs[2 * MERGE_GATE_BLOCKS:]

    @pl.when(pl.program_id(1) == 0)
    def _():
        wa_bf[...] = wa_ref[...].astype(BF16)
        wb_bf[...] = wb_ref[...].astype(BF16)

    part_rows = att_ref.shape[0] // MERGE_ROW_PARTS
    parts = [slice(p * part_rows, (p + 1) * part_rows) for p in range(MERGE_ROW_PARTS)]
    a = [jnp.dot(att_ref[r, :], wa_bf[...], preferred_element_type=F32) for r in parts]
    b = [jnp.dot(sgo_ref[r, :], wb_bf[...], preferred_element_type=F32) for r in parts]
    for p, r in enumerate(parts):
        for g in range(MERGE_GATE_BLOCKS):
            cols = slice(g * GATE_BLOCK, (g + 1) * GATE_BLOCK)
            ga_ref, gb_ref = gate_refs[g], gate_refs[MERGE_GATE_BLOCKS + g]
            m = (jax.nn.sigmoid(ga_ref[r, :]) * a[p][:, cols]
                 + jax.nn.sigmoid(gb_ref[r, :]) * b[p][:, cols])
            o_ref[r, cols] = m.astype(o_ref.dtype)


def _merge(att, sgo, proj, w_a, w_b, tm=512):
    T = att.shape[0]
    tn = MERGE_GATE_BLOCKS * GATE_BLOCK
    ga0 = OFF_G // GATE_BLOCK
    gb0 = (OFF_G + D_MODEL) // GATE_BLOCK
    gate = lambda first, g: pl.BlockSpec(
        (tm, GATE_BLOCK), lambda j, i: (i, first + j * MERGE_GATE_BLOCKS + g))
    gates = ([gate(ga0, g) for g in range(MERGE_GATE_BLOCKS)]
             + [gate(gb0, g) for g in range(MERGE_GATE_BLOCKS)])
    return pl.pallas_call(
        _merge_kernel,
        grid=(D_MODEL // tn, T // tm),
        in_specs=[pl.BlockSpec((tm, ATT_WIDTH), lambda j, i: (i, 0)),
                  pl.BlockSpec((tm, SG_WIDTH), lambda j, i: (i, 0))]
                 + gates
                 + [pl.BlockSpec((ATT_WIDTH, tn), lambda j, i: (0, j)),
                    pl.BlockSpec((SG_WIDTH, tn), lambda j, i: (0, j))],
        out_specs=pl.BlockSpec((tm, tn), lambda j, i: (i, j)),
        out_shape=jax.ShapeDtypeStruct((T, D_MODEL), BF16),
        scratch_shapes=[pltpu.VMEM((ATT_WIDTH, tn), BF16),
                        pltpu.VMEM((SG_WIDTH, tn), BF16)],
        compiler_params=_params(("arbitrary", "arbitrary")),
        name="merge",
    )(att, sgo, *([proj] * (2 * MERGE_GATE_BLOCKS)), w_a, w_b)


def _route(logits):
    lane = lax.broadcasted_iota(jnp.int32, logits.shape, 1)
    lane_f = lane.astype(F32)
    is_g = lane < N_GROUPS
    gl = jnp.where(is_g, logits, -jnp.inf)
    gmax = jnp.max(gl, axis=-1, keepdims=True)
    grp = jnp.min(jnp.where(gl == gmax, lane_f, float(LANES)), axis=-1, keepdims=True)
    gsum = jnp.sum(jnp.where(is_g, jnp.exp(logits - gmax), 0.0), axis=-1, keepdims=True)
    g_w = 1.0 / gsum
    e_lane = lane - N_GROUPS
    in_grp = jnp.logical_and(
        jnp.logical_and(e_lane >= 0, e_lane < N_EXPERTS),
        (e_lane // EXPERTS_PER_GROUP).astype(F32) == grp)
    el = jnp.where(in_grp, logits, -jnp.inf)
    v1 = jnp.max(el, axis=-1, keepdims=True)
    i1 = jnp.min(jnp.where(jnp.logical_and(in_grp, el == v1), lane_f, float(LANES)),
                 axis=-1, keepdims=True)
    rest = jnp.logical_and(in_grp, lane_f != i1)
    el2 = jnp.where(rest, logits, -jnp.inf)
    v2 = jnp.max(el2, axis=-1, keepdims=True)
    i2 = jnp.min(jnp.where(jnp.logical_and(rest, el2 == v2), lane_f, float(LANES)),
                 axis=-1, keepdims=True)
    e21 = jnp.exp(v2 - v1)
    w1 = g_w / (1.0 + e21)
    w2 = g_w * e21 / (1.0 + e21)
    idx = jnp.where(lane == 0, i1, i2) - float(N_GROUPS)
    wts = jnp.where(lane == 0, w1, jnp.where(lane == 1, w2, 0.0))
    return idx.astype(jnp.int32), wts


def _out_kernel(m_ref, w_ref, x_ref, g_ref, wr_ref, br_ref, h_ref, hn_ref, idx_ref, wt_ref):
    part_rows = m_ref.shape[0] // OUT_ROW_PARTS
    parts = [slice(p * part_rows, (p + 1) * part_rows) for p in range(OUT_ROW_PARTS)]
    hs = [x_ref[r, :] + jnp.dot(m_ref[r, :], w_ref[...], preferred_element_type=F32)
          for r in parts]
    for p, r in enumerate(parts):
        h = hs[p]
        h_ref[r, :] = h
        rs = lax.rsqrt(jnp.mean(h * h, axis=-1, keepdims=True) + EPS)
        hn = ((h * rs) * g_ref[...]).astype(BF16)
        bits = lax.bitcast_convert_type(hn.astype(F32), jnp.uint32)
        for c in range(HN_WORD_CHUNKS):
            hi = bits[:, c * LANES:(c + 1) * LANES]
            lo = bits[:, (c + HN_WORD_CHUNKS) * LANES:(c + HN_WORD_CHUNKS + 1) * LANES]
            hn_ref[pl.ds(p * part_rows * HN_WORD_CHUNKS + c, part_rows,
                         stride=HN_WORD_CHUNKS), :] = hi | (lo >> 16)
        logits = jnp.dot(hn, wr_ref[...], preferred_element_type=F32) + br_ref[...]
        idx, wts = _route(logits)
        idx_ref[r, :] = idx
        wt_ref[r, :] = wts


def _out_proj(merged, w_out, x, g2, w_router, b_router, tm=512):
    T, D = x.shape
    row = lambda i: (i, 0)
    const = lambda i: (0, 0)
    return pl.pallas_call(
        _out_kernel,
        grid=(T // tm,),
        in_specs=[pl.BlockSpec((tm, D), row),
                  pl.BlockSpec((D, D), const),
                  pl.BlockSpec((tm, D), row),
                  pl.BlockSpec((1, D), const),
                  pl.BlockSpec((D, LANES), const),
                  pl.BlockSpec((1, LANES), const)],
        out_specs=[pl.BlockSpec((tm, D), row),
                   pl.BlockSpec((tm * HN_WORD_CHUNKS, LANES), row),
                   pl.BlockSpec((tm, LANES), row),
                   pl.BlockSpec((tm, LANES), row)],
        out_shape=[jax.ShapeDtypeStruct((T, D), F32),
                   jax.ShapeDtypeStruct((T * HN_WORD_CHUNKS, LANES), jnp.uint32),
                   jax.ShapeDtypeStruct((T, LANES), jnp.int32),
                   jax.ShapeDtypeStruct((T, LANES), F32)],
        compiler_params=_params(("parallel",)),
        name="out_proj_router",
    )(merged, w_out.astype(BF16), x, g2.reshape(1, D), w_router, b_router)


def _expert_kernel(be_ref, eo_ref, ue_ref, nr_ref, bo_ref, rt_ref, nu_ref,
                   hn_hbm, wg_hbm, wu_hbm, wd_hbm, y_ref, xbuf, wgf, wuf, wdf, sem, wsem):
    b = pl.program_id(0)
    n_used = nu_ref[0]
    n_exp = nu_ref[1]
    used = b < n_used
    slot = b % ROW_SLOTS

    def weight_copies(ordinal, slot_):
        e = ue_ref[ordinal]
        return (pltpu.make_async_copy(wg_hbm.at[e], wgf.at[slot_], wsem.at[slot_]),
                pltpu.make_async_copy(wu_hbm.at[e], wuf.at[slot_], wsem.at[slot_]),
                pltpu.make_async_copy(wd_hbm.at[e], wdf.at[slot_], wsem.at[slot_]))

    def start_weights(ordinal):
        for c in weight_copies(ordinal, ordinal % WEIGHT_SLOTS):
            c.start(priority=1)

    def issue_trips(blk):
        return (nr_ref[blk] + (ROWS_PER_ISSUE - 1)) // ROWS_PER_ISSUE

    def gather_rows(blk, slot_):
        first_row = bo_ref[blk]

        def issue(g, c):
            for j in range(ROWS_PER_ISSUE):
                r = g * ROWS_PER_ISSUE + j
                src = pl.multiple_of(rt_ref[first_row + r] * HN_WORD_CHUNKS, HN_WORD_CHUNKS)
                dst = pl.multiple_of(r * HN_WORD_CHUNKS, HN_WORD_CHUNKS)
                pltpu.make_async_copy(hn_hbm.at[pl.ds(src, HN_WORD_CHUNKS)],
                                      xbuf.at[slot_, pl.ds(dst, HN_WORD_CHUNKS)],
                                      sem.at[slot_]).start()
            return c

        lax.fori_loop(0, issue_trips(blk), issue, 0)

    @pl.when(b == 0)
    def _():
        start_weights(0)

        @pl.when(n_exp > 1)
        def _():
            start_weights(1)

        xbuf[...] = jnp.zeros(xbuf.shape, xbuf.dtype)
        gather_rows(0, 0)
        for a in range(1, ROW_AHEAD):
            @pl.when(n_used > a)
            def _(a=a):
                gather_rows(a, a)

    @pl.when(b + ROW_AHEAD < n_used)
    def _():
        gather_rows(b + ROW_AHEAD, (b + ROW_AHEAD) % ROW_SLOTS)

    @pl.when(used)
    def _():
        first = jnp.logical_or(b == 0, be_ref[b] != be_ref[jnp.maximum(b - 1, 0)])

        ordinal = eo_ref[b]
        ws = ordinal % WEIGHT_SLOTS

        @pl.when(first)
        def _():
            for c in weight_copies(ordinal, ws):
                c.wait()

            @pl.when(ordinal + 2 < n_exp)
            def _():
                start_weights(ordinal + 2)

        words = issue_trips(b) * (ROWS_PER_ISSUE * HN_WORD_CHUNKS)
        pltpu.make_async_copy(hn_hbm.at[pl.ds(0, words)], xbuf.at[slot, pl.ds(0, words)],
                              sem.at[slot]).wait()

        halves = ([], [])
        for c in range(HN_WORD_CHUNKS):
            w = xbuf[slot, pl.ds(c, MOE_BLOCK, stride=HN_WORD_CHUNKS), :]
            hi = lax.bitcast_convert_type(w & jnp.uint32(0xFFFF0000), F32)
            lo = lax.bitcast_convert_type(w << 16, F32)
            halves[0].append(hi.astype(BF16))
            halves[1].append(lo.astype(BF16))
        x = jnp.concatenate(halves[0] + halves[1], axis=1)
        hg = jnp.dot(x, wgf[ws].astype(BF16), preferred_element_type=F32)
        hu = jnp.dot(x, wuf[ws].astype(BF16), preferred_element_type=F32)
        hdn = (jax.nn.silu(hg) * hu).astype(BF16)
        y = jnp.dot(hdn, wdf[ws].astype(BF16), preferred_element_type=F32)
        bits = lax.bitcast_convert_type(y.astype(BF16).astype(F32), jnp.uint32)
        for c in range(HN_WORD_CHUNKS):
            hi = bits[:, c * LANES:(c + 1) * LANES]
            lo = bits[:, (c + HN_WORD_CHUNKS) * LANES:(c + HN_WORD_CHUNKS + 1) * LANES]
            y_ref[pl.ds(c, MOE_BLOCK, stride=HN_WORD_CHUNKS), :] = hi | (lo >> 16)

    @pl.when(jnp.logical_not(used))
    def _():
        y_ref[...] = jnp.zeros(y_ref.shape, y_ref.dtype)


def _experts(hn, w_gate, w_up, w_down, block_e, block_ord, used_experts, block_rows,
             block_offset, row_tok, n_used):
    D = w_gate.shape[1]
    n_blocks = block_e.shape[0]
    n_rows = n_blocks * MOE_BLOCK
    hbm = pl.BlockSpec(memory_space=pl.ANY)
    grid_spec = pltpu.PrefetchScalarGridSpec(
        num_scalar_prefetch=7,
        grid=(n_blocks,),
        in_specs=[hbm, hbm, hbm, hbm],
        out_specs=pl.BlockSpec((MOE_BLOCK * HN_WORD_CHUNKS, LANES), lambda b, *_: (b, 0)),
        scratch_shapes=[pltpu.VMEM((ROW_SLOTS, MOE_BLOCK * HN_WORD_CHUNKS, LANES), jnp.uint32),
                        pltpu.VMEM((WEIGHT_SLOTS, D, EXPERT_FF), F32),
                        pltpu.VMEM((WEIGHT_SLOTS, D, EXPERT_FF), F32),
                        pltpu.VMEM((WEIGHT_SLOTS, EXPERT_FF, D), F32),
                        pltpu.SemaphoreType.DMA((ROW_SLOTS,)),
                        pltpu.SemaphoreType.DMA((WEIGHT_SLOTS,))],
    )
    return pl.pallas_call(
        _expert_kernel,
        grid_spec=grid_spec,
        out_shape=jax.ShapeDtypeStruct((n_rows * HN_WORD_CHUNKS, LANES), jnp.uint32),
        compiler_params=_params(("arbitrary",)),
        name="experts",
    )(block_e, block_ord, used_experts, block_rows, block_offset, row_tok, n_used,
      hn, w_gate, w_up, w_down)


def _combine_kernel(*refs, tc):
    dest_refs = refs[:COMBINE_SLOTS]
    y_hbm, h_ref, wt_ref, o_ref, ybuf, sem = refs[COMBINE_SLOTS:]
    i = pl.program_id(0)
    slot = i % COMBINE_SLOTS

    def gather_rows(dest_ref, slot_):
        def issue(r, c):
            dst = pl.multiple_of(r * HN_WORD_CHUNKS, HN_WORD_CHUNKS)
            for k in range(TOP_K):
                src = pl.multiple_of(dest_ref[0, 0, r * TOP_K + k] * HN_WORD_CHUNKS, HN_WORD_CHUNKS)
                pltpu.make_async_copy(y_hbm.at[pl.ds(src, HN_WORD_CHUNKS)],
                                      ybuf.at[slot_, k, pl.ds(dst, HN_WORD_CHUNKS)],
                                      sem.at[slot_]).start(priority=k % 2)
            return c

        lax.fori_loop(0, tc, issue, 0, unroll=4)

    @pl.when(i == 0)
    def _():
        for a in range(COMBINE_AHEAD):
            gather_rows(dest_refs[a], a)

    @pl.when(i + COMBINE_AHEAD < pl.num_programs(0))
    def _():
        gather_rows(dest_refs[COMBINE_AHEAD], (i + COMBINE_AHEAD) % COMBINE_SLOTS)

    for k in range(TOP_K):
        pltpu.make_async_copy(y_hbm.at[pl.ds(0, tc * HN_WORD_CHUNKS)], ybuf.at[slot, k],
                              sem.at[slot]).wait()
    wt = wt_ref[...]
    w1, w2 = wt[:, 0:1], wt[:, 1:2]
    for c in range(HN_WORD_CHUNKS):
        rows = pl.ds(c, tc, stride=HN_WORD_CHUNKS)
        words = [ybuf[slot, k, rows, :] for k in range(TOP_K)]
        hi = [lax.bitcast_convert_type(w & jnp.uint32(0xFFFF0000), F32) for w in words]
        lo = [lax.bitcast_convert_type(w << 16, F32) for w in words]
        for col, y in ((c * LANES, hi), ((c + HN_WORD_CHUNKS) * LANES, lo)):
            o_ref[:, col:col + LANES] = h_ref[:, col:col + LANES] + (w1 * y[0] + w2 * y[1])


def _combine(yrows, h, wts, dest, tc=256):
    T, D = h.shape
    steps = T // tc
    row = lambda i: (i, 0)
    dest3 = dest.reshape(steps, 1, tc * TOP_K)
    dest_block = lambda im: pl.BlockSpec((1, 1, tc * TOP_K), im, memory_space=pltpu.SMEM)
    return pl.pallas_call(
        functools.partial(_combine_kernel, tc=tc),
        grid=(steps,),
        in_specs=[dest_block(lambda i, a=a: (jnp.minimum(i + a, steps - 1), 0, 0))
                  for a in range(COMBINE_SLOTS)]
                 + [pl.BlockSpec(memory_space=pl.ANY),
                    pl.BlockSpec((tc, D), row),
                    pl.BlockSpec((tc, LANES), row)],
        out_specs=pl.BlockSpec((tc, D), row),
        out_shape=jax.ShapeDtypeStruct((T, D), F32),
        scratch_shapes=[pltpu.VMEM((COMBINE_SLOTS, TOP_K, tc * HN_WORD_CHUNKS, LANES), jnp.uint32),
                        pltpu.SemaphoreType.DMA((COMBINE_SLOTS,))],
        compiler_params=_params(("arbitrary",)),
        name="combine",
    )(*([dest3] * COMBINE_SLOTS), yrows, h, wts)


SUBLANES = 8
META_ROWS = 256

def _lane_cumsum(x):
    lane = lax.broadcasted_iota(jnp.int32, x.shape, 1)
    s = 1
    while s < LANES:
        x = x + jnp.where(lane >= s, pltpu.roll(x, s, 1), 0)
        s *= 2
    return x


def _dispatch_kernel(idx_ref, dest_ref, meta_ref, run_ref, prefix_ref, start_ref, *, tb):
    p = pl.program_id(0)
    i = pl.program_id(1)
    idx = idx_ref[...]
    lane = lax.broadcasted_iota(jnp.int32, idx.shape, 1)
    e1 = idx[:, 0:1]
    e2 = idx[:, 1:2]
    onehot = jnp.where(jnp.logical_or(lane == e1, lane == e2), 1.0, 0.0)

    @pl.when(jnp.logical_and(p == 0, i == 0))
    def _():
        run_ref[...] = jnp.zeros(run_ref.shape, F32)

    @pl.when(p == 0)
    def _():
        prefix_ref[i] = run_ref[...]
        run_ref[...] = run_ref[...] + jnp.sum(onehot, axis=0, keepdims=True)

    @pl.when(jnp.logical_and(p == 1, i == 0))
    def _():
        counts = run_ref[...].astype(jnp.int32)
        nblk = (counts + (MOE_BLOCK - 1)) // MOE_BLOCK
        end_blk = _lane_cumsum(nblk)
        start_ref[...] = ((end_blk - nblk) * MOE_BLOCK).astype(F32)
        has = jnp.where(counts > 0, 1, 0)
        ordinal = _lane_cumsum(has) - 1
        first_row = (_lane_cumsum(counts) - counts)[0:1]
        start_blk, counts = (end_blk - nblk)[0:1], counts[0:1]
        end_blk, has, ordinal = end_blk[0:1], has[0:1], ordinal[0:1]
        rows = lax.broadcasted_iota(jnp.int32, (META_ROWS, LANES), 0)
        lanes = lax.broadcasted_iota(jnp.int32, (META_ROWS, LANES), 1)
        is_e = lanes < N_EXPERTS
        rsum = lambda v: jnp.sum(v, axis=-1, keepdims=True)
        be = rsum(jnp.where(jnp.logical_and(is_e, end_blk <= rows), 1, 0))
        be = jnp.minimum(be, N_EXPERTS - 1)
        eo = rsum(jnp.where(lanes == be, ordinal, 0))
        ue = rsum(jnp.where(jnp.logical_and(has > 0, ordinal == rows), lanes, 0))
        n_blk = rsum(jnp.where(lanes == N_EXPERTS - 1, end_blk, 0))
        n_exp = rsum(jnp.where(is_e, has, 0))
        own = lanes == be
        done = MOE_BLOCK * (rows - rsum(jnp.where(own, start_blk, 0)))
        n_valid = jnp.clip(rsum(jnp.where(own, counts, 0)) - done, 0, MOE_BLOCK)
        offset = rsum(jnp.where(own, first_row, 0)) + done
        columns = (be, eo, ue, n_blk, n_exp, n_valid, offset)
        meta = jnp.zeros((META_ROWS, LANES), jnp.int32)
        for col, val in enumerate(columns):
            meta = jnp.where(lanes == col, val, meta)
        meta_ref[...] = meta

    @pl.when(p == 1)
    def _():
        r = lax.broadcasted_iota(jnp.int32, (tb, tb), 0)
        c = lax.broadcasted_iota(jnp.int32, (tb, tb), 1)
        earlier = jnp.where(c < r, 1.0, 0.0).astype(BF16)
        rank = jnp.dot(earlier, onehot.astype(BF16), preferred_element_type=F32)
        rank = rank + prefix_ref[i][0:1] + start_ref[0:1]
        d1 = jnp.sum(jnp.where(lane == e1, rank, 0.0), axis=-1, keepdims=True)
        d2 = jnp.sum(jnp.where(lane == e2, rank, 0.0), axis=-1, keepdims=True)
        dest_ref[...] = jnp.where(lane == 0, d1, d2).astype(jnp.int32)


def _dispatch(idx, tb=512):
    T = idx.shape[0]
    n_rows = T * TOP_K + N_EXPERTS * MOE_BLOCK
    n_blocks = n_rows // MOE_BLOCK
    assert n_blocks <= META_ROWS
    dest2, meta = pl.pallas_call(
        functools.partial(_dispatch_kernel, tb=tb),
        grid=(2, T // tb),
        in_specs=[pl.BlockSpec((tb, LANES), lambda p, i: (i, 0))],
        out_specs=[pl.BlockSpec((tb, LANES), lambda p, i: (i * p, 0)),
                   pl.BlockSpec((META_ROWS, LANES), lambda p, i: (0, 0))],
        out_shape=[jax.ShapeDtypeStruct((T, LANES), jnp.int32),
                   jax.ShapeDtypeStruct((META_ROWS, LANES), jnp.int32)],
        scratch_shapes=[pltpu.VMEM((SUBLANES, LANES), F32),
                        pltpu.VMEM((T // tb, SUBLANES, LANES), F32),
                        pltpu.VMEM((SUBLANES, LANES), F32)],
        compiler_params=_params(("arbitrary", "arbitrary")),
        name="dispatch",
    )(idx)
    dest = dest2[:, :TOP_K].reshape(T * TOP_K)
    row_tok = (jnp.argsort(dest) // TOP_K).astype(jnp.int32)
    row_tok = jnp.concatenate([row_tok, jnp.zeros((ROWS_PER_ISSUE,), jnp.int32)])
    block_e = meta[:n_blocks, 0]
    block_ord = meta[:n_blocks, 1]
    used_experts = meta[:N_EXPERTS, 2]
    n_used = meta[0, 3:5]
    block_rows = meta[:n_blocks, 5]
    block_offset = meta[:n_blocks, 6]
    return block_e, block_ord, used_experts, block_rows, block_offset, row_tok, n_used, dest


def kernel(x, positions, norm1_g, w_in, q_norm_g, k_norm_g, sink_logits, sg_ln_g, sg_ln_b, sg_w, sg_b, w_branch_att, w_branch_sg, w_out, norm2_g, w_group_router, b_group_router, w_expert_router, b_expert_router, w_gate, w_up, w_down):
    B, S, D = x.shape
    T = B * S
    h = x.reshape(T, D)
    pos = positions.reshape(T)
    for l in range(norm1_g.shape[0]):
        xn = _rmsnorm(h, norm1_g[l])
        proj = _in_proj(xn, w_in[l])
        q, k, v = _qkv_prep(proj, pos, q_norm_g[l], k_norm_g[l])
        att = _attention(q, k, v, sink_logits[l], B)
        sgo = _spatial_gating(proj, sg_ln_g[l], sg_ln_b[l], sg_w[l], sg_b[l])
        merged = _merge(att, sgo, proj, w_branch_att[l], w_branch_sg[l])
        pad = LANES - N_GROUPS - N_EXPERTS
        w_router = jnp.concatenate(
            [w_group_router[l], w_expert_router[l], jnp.zeros((D, pad), F32)], axis=1).astype(BF16)
        b_router = jnp.concatenate(
            [b_group_router[l], b_expert_router[l], jnp.zeros((pad,), F32)]).reshape(1, LANES)
        h, hn, idx, wts = _out_proj(merged, w_out[l], h, norm2_g[l], w_router, b_router)
        (block_e, block_ord, used_experts, block_rows, block_offset, row_tok, n_used,
         dest) = _dispatch(idx)
        yrows = _experts(hn, w_gate[l], w_up[l], w_down[l], block_e, block_ord, used_experts,
                         block_rows, block_offset, row_tok, n_used)
        h = _combine(yrows, h, wts, dest)
    return h.reshape(B, S, D)
```

```python
import functools

import jax
import jax.numpy as jnp
from jax import lax
from jax.experimental import pallas as pl
from jax.experimental.pallas import tpu as pltpu

F32 = jnp.float32
BF16 = jnp.bfloat16

D_MODEL = 2048
HEAD_DIM = 64
ATT_WIDTH = D_MODEL // 2
ATT_HEADS = ATT_WIDTH // HEAD_DIM
ATT_KV_HEADS = ATT_HEADS // 4
Q_PER_KV = ATT_HEADS // ATT_KV_HEADS
KV_WIDTH = ATT_KV_HEADS * HEAD_DIM
WINDOW = 128
ATT_BLOCK = 128
ROPE_DIM = HEAD_DIM // 4
ROPE_HALF = ROPE_DIM // 2
ROPE_THETA = 500000.0
SG_WIDTH = D_MODEL // 2
SG_GROUP_DIM = 128
SG_GROUPS = SG_WIDTH // SG_GROUP_DIM
SG_CHUNK = 128
OFF_Q = 0
OFF_K = OFF_Q + ATT_WIDTH
OFF_V = OFF_K + KV_WIDTH
OFF_U = OFF_V + KV_WIDTH
OFF_S = OFF_U + SG_WIDTH
OFF_G = OFF_S + SG_WIDTH
IN_COLS = OFF_G + 2 * D_MODEL
N_GROUPS = 8
EXPERTS_PER_GROUP = 8
N_EXPERTS = N_GROUPS * EXPERTS_PER_GROUP
TOP_K = 2
EXPERT_FF = D_MODEL // 4
MOE_BLOCK = 128
EPS = 1e-6
NEG_INF = -1e30

LANES = 128
HN_WORD_CHUNKS = D_MODEL // LANES // 2
ROWS_PER_ISSUE = 8
ROW_AHEAD = 4
ROW_SLOTS = ROW_AHEAD + 1
SG_CHUNKS_PER_TRIP = 2
GATE_BLOCK = 512
MERGE_GATE_BLOCKS = 2
MERGE_ROW_PARTS = 2
OUT_ROW_PARTS = 4
WEIGHT_SLOTS = 3
COMBINE_AHEAD = 1
COMBINE_SLOTS = COMBINE_AHEAD + 1
VMEM_LIMIT = 56 * 1024 * 1024


def _params(sem, vmem=VMEM_LIMIT):
    return pltpu.CompilerParams(dimension_semantics=sem, vmem_limit_bytes=vmem)


def _rmsnorm_kernel(x_ref, g_ref, o_ref):
    x = x_ref[...]
    r = lax.rsqrt(jnp.mean(x * x, axis=-1, keepdims=True) + EPS)
    o_ref[...] = ((x * r) * g_ref[...]).astype(o_ref.dtype)


def _rmsnorm(x, g, tm=1024):
    T, D = x.shape
    return pl.pallas_call(
        _rmsnorm_kernel,
        grid=(T // tm,),
        in_specs=[pl.BlockSpec((tm, D), lambda i: (i, 0)),
                  pl.BlockSpec((1, D), lambda i: (0, 0))],
        out_specs=pl.BlockSpec((tm, D), lambda i: (i, 0)),
        out_shape=jax.ShapeDtypeStruct((T, D), BF16),
        compiler_params=_params(("parallel",)),
        name="norm1",
    )(x, g.reshape(1, D))


def _proj_kernel(x_ref, w_ref, o_ref, wbf_ref):
    @pl.when(pl.program_id(1) == 0)
    def _():
        wbf_ref[...] = w_ref[...].astype(BF16)

    o_ref[...] = jnp.dot(x_ref[...], wbf_ref[...], preferred_element_type=F32)


def _in_proj(xn, w, tm=1024, tn=1280):
    T, D = xn.shape
    N = w.shape[1]
    return pl.pallas_call(
        _proj_kernel,
        grid=(N // tn, T // tm),
        in_specs=[pl.BlockSpec((tm, D), lambda j, i: (i, 0)),
                  pl.BlockSpec((D, tn), lambda j, i: (0, j))],
        out_specs=pl.BlockSpec((tm, tn), lambda j, i: (i, j)),
        out_shape=jax.ShapeDtypeStruct((T, N), F32),
        scratch_shapes=[pltpu.VMEM((D, tn), BF16)],
        compiler_params=_params(("arbitrary", "arbitrary")),
        name="in_proj",
    )(xn, w)


def _rope_table_kernel(pos_ref, invf_ref, cos_ref, sin_ref):
    ang = pos_ref[...].astype(F32) * invf_ref[...]
    cos_ref[...] = jnp.cos(ang)
    sin_ref[...] = jnp.sin(ang)


def _rope_tables(positions):
    T = positions.shape[0]
    rows = T * ROPE_HALF // LANES
    inv = ROPE_THETA ** (-jnp.arange(0, ROPE_DIM, 2, dtype=F32) / ROPE_DIM)
    invf = jnp.tile(inv, LANES // ROPE_HALF).reshape(1, LANES)
    pos = jnp.repeat(positions, ROPE_HALF).reshape(rows, LANES)
    whole = lambda: (0, 0)
    cos, sin = pl.pallas_call(
        _rope_table_kernel,
        in_specs=[pl.BlockSpec((rows, LANES), whole), pl.BlockSpec((1, LANES), whole)],
        out_specs=[pl.BlockSpec((rows, LANES), whole), pl.BlockSpec((rows, LANES), whole)],
        out_shape=[jax.ShapeDtypeStruct((rows, LANES), F32)] * 2,
        name="rope_tables",
    )(pos, invf)
    return cos.reshape(T, ROPE_HALF), sin.reshape(T, ROPE_HALF)


def _qkv_prep_kernel(p_ref, cos_ref, sin_ref, cpat_ref, spat_ref, gq_ref, gk_ref, seg_ref,
                     q_ref, k_ref, v_ref):
    lane = lax.broadcasted_iota(jnp.int32, (cos_ref.shape[0], LANES), 1)
    in_head = lane % HEAD_DIM

    def spread(t_ref, pattern_ref):
        t = t_ref[...]
        hi = t.astype(BF16)
        r1 = t - hi.astype(F32)
        mid = r1.astype(BF16)
        lo = (r1 - mid.astype(F32)).astype(BF16)
        return sum(jnp.dot(term, pattern_ref[...], preferred_element_type=F32)
                   for term in (hi, mid, lo))

    cos = jnp.where(in_head < ROPE_DIM, spread(cos_ref, cpat_ref), 1.0)
    sin = spread(sin_ref, spat_ref)
    first_half = in_head < ROPE_HALF

    def norm_rope(x, g):
        x2 = x * x
        x2_hi = x2.astype(BF16)
        x2_lo = (x2 - x2_hi.astype(F32)).astype(BF16)
        both = jnp.dot(jnp.concatenate([x2_hi, x2_lo], axis=0), seg_ref[...],
                       preferred_element_type=F32)
        ssq = both[:x.shape[0]] + both[x.shape[0]:]
        xn = (x * lax.rsqrt(ssq * (1.0 / HEAD_DIM) + EPS)) * g
        partner = jnp.where(first_half,
                            pltpu.roll(xn, LANES - ROPE_HALF, 1),
                            pltpu.roll(xn, ROPE_HALF, 1))
        return xn * cos + partner * sin

    for c in range(ATT_WIDTH // LANES):
        x = p_ref[:, OFF_Q + c * LANES:OFF_Q + (c + 1) * LANES]
        q_ref[:, c * LANES:(c + 1) * LANES] = (
            norm_rope(x, gq_ref[...]) * (HEAD_DIM ** -0.5)).astype(q_ref.dtype)
    for c in range(KV_WIDTH // LANES):
        x = p_ref[:, OFF_K + c * LANES:OFF_K + (c + 1) * LANES]
        k_ref[:, c * LANES:(c + 1) * LANES] = norm_rope(x, gk_ref[...]).astype(k_ref.dtype)
    v_ref[...] = p_ref[:, OFF_V:OFF_V + KV_WIDTH].astype(v_ref.dtype)


def _qkv_prep(proj, positions, q_g, k_g, tq=512):
    T = proj.shape[0]
    width = OFF_U
    cos8, sin8 = _rope_tables(positions)
    in_head = jnp.arange(LANES) % HEAD_DIM
    picks = (in_head[None, :] % ROPE_HALF == jnp.arange(ROPE_HALF)[:, None]) & (in_head < ROPE_DIM)
    cos_pattern = picks.astype(BF16)
    sin_pattern = (picks * jnp.where(in_head < ROPE_HALF, -1.0, 1.0)).astype(BF16)
    head_of_lane = jnp.arange(LANES) // HEAD_DIM
    same_head = (head_of_lane[:, None] == head_of_lane[None, :]).astype(BF16)
    gq = jnp.tile(q_g, LANES // HEAD_DIM).reshape(1, LANES)
    gk = jnp.tile(k_g, LANES // HEAD_DIM).reshape(1, LANES)
    row = lambda i: (i, 0)
    const = lambda i: (0, 0)
    return pl.pallas_call(
        _qkv_prep_kernel,
        grid=(T // tq,),
        in_specs=[pl.BlockSpec((tq, width), row),
                  pl.BlockSpec((tq, ROPE_HALF), row),
                  pl.BlockSpec((tq, ROPE_HALF), row),
                  pl.BlockSpec((ROPE_HALF, LANES), const),
                  pl.BlockSpec((ROPE_HALF, LANES), const),
                  pl.BlockSpec((1, LANES), const),
                  pl.BlockSpec((1, LANES), const),
                  pl.BlockSpec((LANES, LANES), const)],
        out_specs=[pl.BlockSpec((tq, ATT_WIDTH), row),
                   pl.BlockSpec((tq, KV_WIDTH), row),
                   pl.BlockSpec((tq, KV_WIDTH), row)],
        out_shape=[jax.ShapeDtypeStruct((T, ATT_WIDTH), BF16),
                   jax.ShapeDtypeStruct((T, KV_WIDTH), BF16),
                   jax.ShapeDtypeStruct((T, KV_WIDTH), BF16)],
        compiler_params=_params(("parallel",)),
        name="qkv_prep",
    )(proj, cos8, sin8, cos_pattern, sin_pattern, gq, gk, same_head)


def _attn_kernel(sink_ref, q_ref, kp_ref, kc_ref, kn_ref, vp_ref, vc_ref, vn_ref, o_ref, *, nb):
    n = pl.program_id(1)
    rows = ATT_BLOCK
    qi = lax.broadcasted_iota(jnp.int32, (rows, ATT_BLOCK), 0) % ATT_BLOCK
    kj = lax.broadcasted_iota(jnp.int32, (rows, ATT_BLOCK), 1)
    lo_prev = jnp.where(n > 0, 0, ATT_BLOCK)
    hi_next = jnp.where(n < nb - 1, 0, -ATT_BLOCK)
    cap_prev = jnp.where(kj - qi >= lo_prev, jnp.inf, NEG_INF)
    cap_next = jnp.where(kj - qi <= hi_next, jnp.inf, NEG_INF)
    cap = jnp.concatenate([cap_prev, jnp.full((rows, ATT_BLOCK), jnp.inf, F32), cap_next], axis=1)

    kvhs = range(ATT_KV_HEADS)
    scores = []
    for kvh in kvhs:
        cols = slice(kvh * HEAD_DIM, (kvh + 1) * HEAD_DIM)
        k = jnp.concatenate([kp_ref[:, cols], kc_ref[:, cols], kn_ref[:, cols]], axis=0)
        q = jnp.concatenate(
            [q_ref[:, (kvh * Q_PER_KV + g) * HEAD_DIM:(kvh * Q_PER_KV + g + 1) * HEAD_DIM]
             for g in range(Q_PER_KV)], axis=0)
        scores.append(lax.dot_general(q, k, (((1,), (1,)), ((), ())),
                                      preferred_element_type=F32))
    probs = []
    for kvh in kvhs:
        strips = []
        for g in range(Q_PER_KV):
            sink = sink_ref[kvh * Q_PER_KV + g]
            sg = jnp.minimum(scores[kvh][g * ATT_BLOCK:(g + 1) * ATT_BLOCK], cap)
            m = jnp.maximum(jnp.max(sg, axis=-1, keepdims=True), sink)
            e = jnp.exp(sg - m)
            denom = jnp.sum(e, axis=-1, keepdims=True) + jnp.exp(sink - m)
            strips.append((e / denom).astype(BF16))
        probs.append(jnp.concatenate(strips, axis=0))
    for kvh in kvhs:
        cols = slice(kvh * HEAD_DIM, (kvh + 1) * HEAD_DIM)
        v = jnp.concatenate([vp_ref[:, cols], vc_ref[:, cols], vn_ref[:, cols]], axis=0)
        o = jnp.dot(probs[kvh], v, preferred_element_type=F32)
        for g in range(Q_PER_KV):
            h = kvh * Q_PER_KV + g
            o_ref[:, h * HEAD_DIM:(h + 1) * HEAD_DIM] = (
                o[g * ATT_BLOCK:(g + 1) * ATT_BLOCK].astype(o_ref.dtype))


def _attention(q, k, v, sink, batch):
    T = q.shape[0]
    nb = T // batch // ATT_BLOCK
    cur = lambda b, n: (b * nb + n, 0)
    prev = lambda b, n: (b * nb + jnp.maximum(n - 1, 0), 0)
    nxt = lambda b, n: (b * nb + jnp.minimum(n + 1, nb - 1), 0)
    kv = lambda im: pl.BlockSpec((ATT_BLOCK, KV_WIDTH), im)
    return pl.pallas_call(
        functools.partial(_attn_kernel, nb=nb),
        grid=(batch, nb),
        in_specs=[pl.BlockSpec(memory_space=pltpu.SMEM),
                  pl.BlockSpec((ATT_BLOCK, ATT_WIDTH), cur),
                  kv(prev), kv(cur), kv(nxt), kv(prev), kv(cur), kv(nxt)],
        out_specs=pl.BlockSpec((ATT_BLOCK, ATT_WIDTH), cur),
        out_shape=jax.ShapeDtypeStruct((T, ATT_WIDTH), BF16),
        compiler_params=_params(("parallel", "parallel")),
        name="window_attn",
    )(sink, q, k, k, k, v, v, v)


def _sg_kernel(u_ref, s_ref, lng_ref, lnb_ref, w_ref, b_ref, o_ref):
    groups = w_ref.shape[0]

    def some_chunks(ci, carry):
        tiles = []
        for j in range(SG_CHUNKS_PER_TRIP):
            start = pl.multiple_of((ci * SG_CHUNKS_PER_TRIP + j) * SG_CHUNK, SG_CHUNK)
            for gi in range(groups):
                tiles.append((pl.ds(start, SG_CHUNK),
                              slice(gi * SG_GROUP_DIM, (gi + 1) * SG_GROUP_DIM), gi))
        ts = range(len(tiles))
        s = [jax.nn.gelu(s_ref[r, c]) for r, c, _ in tiles]
        sc = [s[t] - jnp.mean(s[t], axis=-1, keepdims=True) for t in ts]
        var = [jnp.mean(sc[t] * sc[t], axis=-1, keepdims=True) for t in ts]
        sn = [(sc[t] * lax.rsqrt(var[t] + EPS)) * lng_ref[0, gi:gi + 1, :]
              + lnb_ref[0, gi:gi + 1, :] for t, (_, _, gi) in enumerate(tiles)]
        mixed = [jnp.dot(w_ref[gi], sn[t].astype(BF16), preferred_element_type=F32)
                 + b_ref[0, :, gi:gi + 1] for t, (_, _, gi) in enumerate(tiles)]
        for t, (r, c, _) in enumerate(tiles):
            o_ref[r, c] = (jax.nn.gelu(u_ref[r, c]) * mixed[t]).astype(o_ref.dtype)
        return carry

    lax.fori_loop(0, u_ref.shape[0] // (SG_CHUNK * SG_CHUNKS_PER_TRIP), some_chunks, 0)


def _spatial_gating(proj, ln_g, ln_b, w_s, b_s, halves=2, chunks=8):
    T = proj.shape[0]
    half_w = SG_WIDTH // halves
    gph = SG_GROUPS // halves
    u0, s0 = OFF_U // half_w, OFF_S // half_w
    tq = chunks * SG_CHUNK
    return pl.pallas_call(
        _sg_kernel,
        grid=(T // tq, halves),
        in_specs=[pl.BlockSpec((tq, half_w), lambda i, j: (i, u0 + j)),
                  pl.BlockSpec((tq, half_w), lambda i, j: (i, s0 + j)),
                  pl.BlockSpec((1, gph, SG_GROUP_DIM), lambda i, j: (j, 0, 0)),
                  pl.BlockSpec((1, gph, SG_GROUP_DIM), lambda i, j: (j, 0, 0)),
                  pl.BlockSpec((gph, SG_CHUNK, SG_CHUNK), lambda i, j: (j, 0, 0)),
                  pl.BlockSpec((1, SG_CHUNK, gph), lambda i, j: (j, 0, 0))],
        out_specs=pl.BlockSpec((tq, half_w), lambda i, j: (i, j)),
        out_shape=jax.ShapeDtypeStruct((T, SG_WIDTH), BF16),
        compiler_params=_params(("parallel", "parallel")),
        name="spatial_gating",
    )(proj, proj,
      ln_g.reshape(halves, gph, SG_GROUP_DIM), ln_b.reshape(halves, gph, SG_GROUP_DIM),
      w_s.astype(BF16),
      b_s.reshape(halves, gph, SG_CHUNK).transpose(0, 2, 1))


def _merge_kernel(att_ref, sgo_ref, *refs):
    gate_refs = refs[:2 * MERGE_GATE_BLOCKS]
    wa_ref, wb_ref, o_ref, wa_bf, wb_bf = refs[2 * MERGE_GATE_BLOCKS:]

    @pl.when(pl.program_id(1) == 0)
    def _():
        wa_bf[...] = wa_ref[...].astype(BF16)
        wb_bf[...] = wb_ref[...].astype(BF16)

    part_rows = att_ref.shape[0] // MERGE_ROW_PARTS
    parts = [slice(p * part_rows, (p + 1) * part_rows) for p in range(MERGE_ROW_PARTS)]
    a = [jnp.dot(att_ref[r, :], wa_bf[...], preferred_element_type=F32) for r in parts]
    b = [jnp.dot(sgo_ref[r, :], wb_bf[...], preferred_element_type=F32) for r in parts]
    for p, r in enumerate(parts):
        for g in range(MERGE_GATE_BLOCKS):
            cols = slice(g * GATE_BLOCK, (g + 1) * GATE_BLOCK)
            ga_ref, gb_ref = gate_refs[g], gate_refs[MERGE_GATE_BLOCKS + g]
            m = (jax.nn.sigmoid(ga_ref[r, :]) * a[p][:, cols]
                 + jax.nn.sigmoid(gb_ref[r, :]) * b[p][:, cols])
            o_ref[r, cols] = m.astype(o_ref.dtype)


def _merge(att, sgo, proj, w_a, w_b, tm=512):
    T = att.shape[0]
    tn = MERGE_GATE_BLOCKS * GATE_BLOCK
    ga0 = OFF_G // GATE_BLOCK
    gb0 = (OFF_G + D_MODEL) // GATE_BLOCK
    gate = lambda first, g: pl.BlockSpec(
        (tm, GATE_BLOCK), lambda j, i: (i, first + j * MERGE_GATE_BLOCKS + g))
    gates = ([gate(ga0, g) for g in range(MERGE_GATE_BLOCKS)]
             + [gate(gb0, g) for g in range(MERGE_GATE_BLOCKS)])
    return pl.pallas_call(
        _merge_kernel,
        grid=(D_MODEL // tn, T // tm),
        in_specs=[pl.BlockSpec((tm, ATT_WIDTH), lambda j, i: (i, 0)),
                  pl.BlockSpec((tm, SG_WIDTH), lambda j, i: (i, 0))]
                 + gates
                 + [pl.BlockSpec((ATT_WIDTH, tn), lambda j, i: (0, j)),
                    pl.BlockSpec((SG_WIDTH, tn), lambda j, i: (0, j))],
        out_specs=pl.BlockSpec((tm, tn), lambda j, i: (i, j)),
        out_shape=jax.ShapeDtypeStruct((T, D_MODEL), BF16),
        scratch_shapes=[pltpu.VMEM((ATT_WIDTH, tn), BF16),
                        pltpu.VMEM((SG_WIDTH, tn), BF16)],
        compiler_params=_params(("arbitrary", "arbitrary")),
        name="merge",
    )(att, sgo, *([proj] * (2 * MERGE_GATE_BLOCKS)), w_a, w_b)


def _route(logits):
    lane = lax.broadcasted_iota(jnp.int32, logits.shape, 1)
    lane_f = lane.astype(F32)
    is_g = lane < N_GROUPS
    gl = jnp.where(is_g, logits, -jnp.inf)
    gmax = jnp.max(gl, axis=-1, keepdims=True)
    grp = jnp.min(jnp.where(gl == gmax, lane_f, float(LANES)), axis=-1, keepdims=True)
    gsum = jnp.sum(jnp.where(is_g, jnp.exp(logits - gmax), 0.0), axis=-1, keepdims=True)
    g_w = 1.0 / gsum
    e_lane = lane - N_GROUPS
    in_grp = jnp.logical_and(
        jnp.logical_and(e_lane >= 0, e_lane < N_EXPERTS),
        (e_lane // EXPERTS_PER_GROUP).astype(F32) == grp)
    el = jnp.where(in_grp, logits, -jnp.inf)
    v1 = jnp.max(el, axis=-1, keepdims=True)
    i1 = jnp.min(jnp.where(jnp.logical_and(in_grp, el == v1), lane_f, float(LANES)),
                 axis=-1, keepdims=True)
    rest = jnp.logical_and(in_grp, lane_f != i1)
    el2 = jnp.where(rest, logits, -jnp.inf)
    v2 = jnp.max(el2, axis=-1, keepdims=True)
    i2 = jnp.min(jnp.where(jnp.logical_and(rest, el2 == v2), lane_f, float(LANES)),
                 axis=-1, keepdims=True)
    e21 = jnp.exp(v2 - v1)
    w1 = g_w / (1.0 + e21)
    w2 = g_w * e21 / (1.0 + e21)
    idx = jnp.where(lane == 0, i1, i2) - float(N_GROUPS)
    wts = jnp.where(lane == 0, w1, jnp.where(lane == 1, w2, 0.0))
    return idx.astype(jnp.int32), wts


def _out_kernel(m_ref, w_ref, x_ref, g_ref, wr_ref, br_ref, h_ref, hn_ref, idx_ref, wt_ref):
    part_rows = m_ref.shape[0] // OUT_ROW_PARTS
    parts = [slice(p * part_rows, (p + 1) * part_rows) for p in range(OUT_ROW_PARTS)]
    hs = [x_ref[r, :] + jnp.dot(m_ref[r, :], w_ref[...], preferred_element_type=F32)
          for r in parts]
    for p, r in enumerate(parts):
        h = hs[p]
        h_ref[r, :] = h
        rs = lax.rsqrt(jnp.mean(h * h, axis=-1, keepdims=True) + EPS)
        hn = ((h * rs) * g_ref[...]).astype(BF16)
        bits = lax.bitcast_convert_type(hn.astype(F32), jnp.uint32)
        for c in range(HN_WORD_CHUNKS):
            hi = bits[:, c * LANES:(c + 1) * LANES]
            lo = bits[:, (c + HN_WORD_CHUNKS) * LANES:(c + HN_WORD_CHUNKS + 1) * LANES]
            hn_ref[pl.ds(p * part_rows * HN_WORD_CHUNKS + c, part_rows,
                         stride=HN_WORD_CHUNKS), :] = hi | (lo >> 16)
        logits = jnp.dot(hn, wr_ref[...], preferred_element_type=F32) + br_ref[...]
        idx, wts = _route(logits)
        idx_ref[r, :] = idx
        wt_ref[r, :] = wts


def _out_proj(merged, w_out, x, g2, w_router, b_router, tm=512):
    T, D = x.shape
    row = lambda i: (i, 0)
    const = lambda i: (0, 0)
    return pl.pallas_call(
        _out_kernel,
        grid=(T // tm,),
        in_specs=[pl.BlockSpec((tm, D), row),
                  pl.BlockSpec((D, D), const),
                  pl.BlockSpec((tm, D), row),
                  pl.BlockSpec((1, D), const),
                  pl.BlockSpec((D, LANES), const),
                  pl.BlockSpec((1, LANES), const)],
        out_specs=[pl.BlockSpec((tm, D), row),
                   pl.BlockSpec((tm * HN_WORD_CHUNKS, LANES), row),
                   pl.BlockSpec((tm, LANES), row),
                   pl.BlockSpec((tm, LANES), row)],
        out_shape=[jax.ShapeDtypeStruct((T, D), F32),
                   jax.ShapeDtypeStruct((T * HN_WORD_CHUNKS, LANES), jnp.uint32),
                   jax.ShapeDtypeStruct((T, LANES), jnp.int32),
                   jax.ShapeDtypeStruct((T, LANES), F32)],
        compiler_params=_params(("parallel",)),
        name="out_proj_router",
    )(merged, w_out.astype(BF16), x, g2.reshape(1, D), w_router, b_router)


def _expert_kernel(be_ref, eo_ref, ue_ref, nr_ref, bo_ref, rt_ref, nu_ref,
                   hn_hbm, wg_hbm, wu_hbm, wd_hbm, y_ref, xbuf, wgf, wuf, wdf, sem, wsem):
    b = pl.program_id(0)
    n_used = nu_ref[0]
    n_exp = nu_ref[1]
    used = b < n_used
    slot = b % ROW_SLOTS

    def weight_copies(ordinal, slot_):
        e = ue_ref[ordinal]
        return (pltpu.make_async_copy(wg_hbm.at[e], wgf.at[slot_], wsem.at[slot_]),
                pltpu.make_async_copy(wu_hbm.at[e], wuf.at[slot_], wsem.at[slot_]),
                pltpu.make_async_copy(wd_hbm.at[e], wdf.at[slot_], wsem.at[slot_]))

    def start_weights(ordinal):
        for c in weight_copies(ordinal, ordinal % WEIGHT_SLOTS):
            c.start(priority=1)

    def issue_trips(blk):
        return (nr_ref[blk] + (ROWS_PER_ISSUE - 1)) // ROWS_PER_ISSUE

    def gather_rows(blk, slot_):
        first_row = bo_ref[blk]

        def issue(g, c):
            for j in range(ROWS_PER_ISSUE):
                r = g * ROWS_PER_ISSUE + j
                src = pl.multiple_of(rt_ref[first_row + r] * HN_WORD_CHUNKS, HN_WORD_CHUNKS)
                dst = pl.multiple_of(r * HN_WORD_CHUNKS, HN_WORD_CHUNKS)
                pltpu.make_async_copy(hn_hbm.at[pl.ds(src, HN_WORD_CHUNKS)],
                                      xbuf.at[slot_, pl.ds(dst, HN_WORD_CHUNKS)],
                                      sem.at[slot_]).start()
            return c

        lax.fori_loop(0, issue_trips(blk), issue, 0)

    @pl.when(b == 0)
    def _():
        start_weights(0)

        @pl.when(n_exp > 1)
        def _():
            start_weights(1)

        xbuf[...] = jnp.zeros(xbuf.shape, xbuf.dtype)
        gather_rows(0, 0)
        for a in range(1, ROW_AHEAD):
            @pl.when(n_used > a)
            def _(a=a):
                gather_rows(a, a)

    @pl.when(b + ROW_AHEAD < n_used)
    def _():
        gather_rows(b + ROW_AHEAD, (b + ROW_AHEAD) % ROW_SLOTS)

    @pl.when(used)
    def _():
        first = jnp.logical_or(b == 0, be_ref[b] != be_ref[jnp.maximum(b - 1, 0)])

        ordinal = eo_ref[b]
        ws = ordinal % WEIGHT_SLOTS

        @pl.when(first)
        def _():
            for c in weight_copies(ordinal, ws):
                c.wait()

            @pl.when(ordinal + 2 < n_exp)
            def _():
                start_weights(ordinal + 2)

        words = issue_trips(b) * (ROWS_PER_ISSUE * HN_WORD_CHUNKS)
        pltpu.make_async_copy(hn_hbm.at[pl.ds(0, words)], xbuf.at[slot, pl.ds(0, words)],
                              sem.at[slot]).wait()

        halves = ([], [])
        for c in range(HN_WORD_CHUNKS):
            w = xbuf[slot, pl.ds(c, MOE_BLOCK, stride=HN_WORD_CHUNKS), :]
            hi = lax.bitcast_convert_type(w & jnp.uint32(0xFFFF0000), F32)
            lo = lax.bitcast_convert_type(w << 16, F32)
            halves[0].append(hi.astype(BF16))
            halves[1].append(lo.astype(BF16))
        x = jnp.concatenate(halves[0] + halves[1], axis=1)
        hg = jnp.dot(x, wgf[ws].astype(BF16), preferred_element_type=F32)
        hu = jnp.dot(x, wuf[ws].astype(BF16), preferred_element_type=F32)
        hdn = (jax.nn.silu(hg) * hu).astype(BF16)
        y = jnp.dot(hdn, wdf[ws].astype(BF16), preferred_element_type=F32)
        bits = lax.bitcast_convert_type(y.astype(BF16).astype(F32), jnp.uint32)
        for c in range(HN_WORD_CHUNKS):
            hi = bits[:, c * LANES:(c + 1) * LANES]
            lo = bits[:, (c + HN_WORD_CHUNKS) * LANES:(c + HN_WORD_CHUNKS + 1) * LANES]
            y_ref[pl.ds(c, MOE_BLOCK, stride=HN_WORD_CHUNKS), :] = hi | (lo >> 16)

    @pl.when(jnp.logical_not(used))
    def _():
        y_ref[...] = jnp.zeros(y_ref.shape, y_ref.dtype)


def _experts(hn, w_gate, w_up, w_down, block_e, block_ord, used_experts, block_rows,
             block_offset, row_tok, n_used):
    D = w_gate.shape[1]
    n_blocks = block_e.shape[0]
    n_rows = n_blocks * MOE_BLOCK
    hbm = pl.BlockSpec(memory_space=pl.ANY)
    grid_spec = pltpu.PrefetchScalarGridSpec(
        num_scalar_prefetch=7,
        grid=(n_blocks,),
        in_specs=[hbm, hbm, hbm, hbm],
        out_specs=pl.BlockSpec((MOE_BLOCK * HN_WORD_CHUNKS, LANES), lambda b, *_: (b, 0)),
        scratch_shapes=[pltpu.VMEM((ROW_SLOTS, MOE_BLOCK * HN_WORD_CHUNKS, LANES), jnp.uint32),
                        pltpu.VMEM((WEIGHT_SLOTS, D, EXPERT_FF), F32),
                        pltpu.VMEM((WEIGHT_SLOTS, D, EXPERT_FF), F32),
                        pltpu.VMEM((WEIGHT_SLOTS, EXPERT_FF, D), F32),
                        pltpu.SemaphoreType.DMA((ROW_SLOTS,)),
                        pltpu.SemaphoreType.DMA((WEIGHT_SLOTS,))],
    )
    return pl.pallas_call(
        _expert_kernel,
        grid_spec=grid_spec,
        out_shape=jax.ShapeDtypeStruct((n_rows * HN_WORD_CHUNKS, LANES), jnp.uint32),
        compiler_params=_params(("arbitrary",)),
        name="experts",
    )(block_e, block_ord, used_experts, block_rows, block_offset, row_tok, n_used,
      hn, w_gate, w_up, w_down)


def _combine_kernel(*refs, tc):
    dest_refs = refs[:COMBINE_SLOTS]
    y_hbm, h_ref, wt_ref, o_ref, ybuf, sem = refs[COMBINE_SLOTS:]
    i = pl.program_id(0)
    slot = i % COMBINE_SLOTS

    def gather_rows(dest_ref, slot_):
        def issue(r, c):
            dst = pl.multiple_of(r * HN_WORD_CHUNKS, HN_WORD_CHUNKS)
            for k in range(TOP_K):
                src = pl.multiple_of(dest_ref[0, 0, r * TOP_K + k] * HN_WORD_CHUNKS, HN_WORD_CHUNKS)
                pltpu.make_async_copy(y_hbm.at[pl.ds(src, HN_WORD_CHUNKS)],
                                      ybuf.at[slot_, k, pl.ds(dst, HN_WORD_CHUNKS)],
                                      sem.at[slot_]).start(priority=k % 2)
            return c

        lax.fori_loop(0, tc, issue, 0, unroll=4)

    @pl.when(i == 0)
    def _():
        for a in range(COMBINE_AHEAD):
            gather_rows(dest_refs[a], a)

    @pl.when(i + COMBINE_AHEAD < pl.num_programs(0))
    def _():
        gather_rows(dest_refs[COMBINE_AHEAD], (i + COMBINE_AHEAD) % COMBINE_SLOTS)

    for k in range(TOP_K):
        pltpu.make_async_copy(y_hbm.at[pl.ds(0, tc * HN_WORD_CHUNKS)], ybuf.at[slot, k],
                              sem.at[slot]).wait()
    wt = wt_ref[...]
    w1, w2 = wt[:, 0:1], wt[:, 1:2]
    for c in range(HN_WORD_CHUNKS):
        rows = pl.ds(c, tc, stride=HN_WORD_CHUNKS)
        words = [ybuf[slot, k, rows, :] for k in range(TOP_K)]
        hi = [lax.bitcast_convert_type(w & jnp.uint32(0xFFFF0000), F32) for w in words]
        lo = [lax.bitcast_convert_type(w << 16, F32) for w in words]
        for col, y in ((c * LANES, hi), ((c + HN_WORD_CHUNKS) * LANES, lo)):
            o_ref[:, col:col + LANES] = h_ref[:, col:col + LANES] + (w1 * y[0] + w2 * y[1])


def _combine(yrows, h, wts, dest, tc=256):
    T, D = h.shape
    steps = T // tc
    row = lambda i: (i, 0)
    dest3 = dest.reshape(steps, 1, tc * TOP_K)
    dest_block = lambda im: pl.BlockSpec((1, 1, tc * TOP_K), im, memory_space=pltpu.SMEM)
    return pl.pallas_call(
        functools.partial(_combine_kernel, tc=tc),
        grid=(steps,),
        in_specs=[dest_block(lambda i, a=a: (jnp.minimum(i + a, steps - 1), 0, 0))
                  for a in range(COMBINE_SLOTS)]
                 + [pl.BlockSpec(memory_space=pl.ANY),
                    pl.BlockSpec((tc, D), row),
                    pl.BlockSpec((tc, LANES), row)],
        out_specs=pl.BlockSpec((tc, D), row),
        out_shape=jax.ShapeDtypeStruct((T, D), F32),
        scratch_shapes=[pltpu.VMEM((COMBINE_SLOTS, TOP_K, tc * HN_WORD_CHUNKS, LANES), jnp.uint32),
                        pltpu.SemaphoreType.DMA((COMBINE_SLOTS,))],
        compiler_params=_params(("arbitrary",)),
        name="combine",
    )(*([dest3] * COMBINE_SLOTS), yrows, h, wts)


SUBLANES = 8
META_ROWS = 256

def _lane_cumsum(x):
    lane = lax.broadcasted_iota(jnp.int32, x.shape, 1)
    s = 1
    while s < LANES:
        x = x + jnp.where(lane >= s, pltpu.roll(x, s, 1), 0)
        s *= 2
    return x


def _dispatch_kernel(idx_ref, dest_ref, meta_ref, run_ref, prefix_ref, start_ref, *, tb):
    p = pl.program_id(0)
    i = pl.program_id(1)
    idx = idx_ref[...]
    lane = lax.broadcasted_iota(jnp.int32, idx.shape, 1)
    e1 = idx[:, 0:1]
    e2 = idx[:, 1:2]
    onehot = jnp.where(jnp.logical_or(lane == e1, lane == e2), 1.0, 0.0)

    @pl.when(jnp.logical_and(p == 0, i == 0))
    def _():
        run_ref[...] = jnp.zeros(run_ref.shape, F32)

    @pl.when(p == 0)
    def _():
        prefix_ref[i] = run_ref[...]
        run_ref[...] = run_ref[...] + jnp.sum(onehot, axis=0, keepdims=True)

    @pl.when(jnp.logical_and(p == 1, i == 0))
    def _():
        counts = run_ref[...].astype(jnp.int32)
        nblk = (counts + (MOE_BLOCK - 1)) // MOE_BLOCK
        end_blk = _lane_cumsum(nblk)
        start_ref[...] = ((end_blk - nblk) * MOE_BLOCK).astype(F32)
        has = jnp.where(counts > 0, 1, 0)
        ordinal = _lane_cumsum(has) - 1
        first_row = (_lane_cumsum(counts) - counts)[0:1]
        start_blk, counts = (end_blk - nblk)[0:1], counts[0:1]
        end_blk, has, ordinal = end_blk[0:1], has[0:1], ordinal[0:1]
        rows = lax.broadcasted_iota(jnp.int32, (META_ROWS, LANES), 0)
        lanes = lax.broadcasted_iota(jnp.int32, (META_ROWS, LANES), 1)
        is_e = lanes < N_EXPERTS
        rsum = lambda v: jnp.sum(v, axis=-1, keepdims=True)
        be = rsum(jnp.where(jnp.logical_and(is_e, end_blk <= rows), 1, 0))
        be = jnp.minimum(be, N_EXPERTS - 1)
        eo = rsum(jnp.where(lanes == be, ordinal, 0))
        ue = rsum(jnp.where(jnp.logical_and(has > 0, ordinal == rows), lanes, 0))
        n_blk = rsum(jnp.where(lanes == N_EXPERTS - 1, end_blk, 0))
        n_exp = rsum(jnp.where(is_e, has, 0))
        own = lanes == be
        done = MOE_BLOCK * (rows - rsum(jnp.where(own, start_blk, 0)))
        n_valid = jnp.clip(rsum(jnp.where(own, counts, 0)) - done, 0, MOE_BLOCK)
        offset = rsum(jnp.where(own, first_row, 0)) + done
        columns = (be, eo, ue, n_blk, n_exp, n_valid, offset)
        meta = jnp.zeros((META_ROWS, LANES), jnp.int32)
        for col, val in enumerate(columns):
            meta = jnp.where(lanes == col, val, meta)
        meta_ref[...] = meta

    @pl.when(p == 1)
    def _():
        r = lax.broadcasted_iota(jnp.int32, (tb, tb), 0)
        c = lax.broadcasted_iota(jnp.int32, (tb, tb), 1)
        earlier = jnp.where(c < r, 1.0, 0.0).astype(BF16)
        rank = jnp.dot(earlier, onehot.astype(BF16), preferred_element_type=F32)
        rank = rank + prefix_ref[i][0:1] + start_ref[0:1]
        d1 = jnp.sum(jnp.where(lane == e1, rank, 0.0), axis=-1, keepdims=True)
        d2 = jnp.sum(jnp.where(lane == e2, rank, 0.0), axis=-1, keepdims=True)
        dest_ref[...] = jnp.where(lane == 0, d1, d2).astype(jnp.int32)


def _dispatch(idx, tb=512):
    T = idx.shape[0]
    n_rows = T * TOP_K + N_EXPERTS * MOE_BLOCK
    n_blocks = n_rows // MOE_BLOCK
    assert n_blocks <= META_ROWS
    dest2, meta = pl.pallas_call(
        functools.partial(_dispatch_kernel, tb=tb),
        grid=(2, T // tb),
        in_specs=[pl.BlockSpec((tb, LANES), lambda p, i: (i, 0))],
        out_specs=[pl.BlockSpec((tb, LANES), lambda p, i: (i * p, 0)),
                   pl.BlockSpec((META_ROWS, LANES), lambda p, i: (0, 0))],
        out_shape=[jax.ShapeDtypeStruct((T, LANES), jnp.int32),
                   jax.ShapeDtypeStruct((META_ROWS, LANES), jnp.int32)],
        scratch_shapes=[pltpu.VMEM((SUBLANES, LANES), F32),
                        pltpu.VMEM((T // tb, SUBLANES, LANES), F32),
                        pltpu.VMEM((SUBLANES, LANES), F32)],
        compiler_params=_params(("arbitrary", "arbitrary")),
        name="dispatch",
    )(idx)
    dest = dest2[:, :TOP_K].reshape(T * TOP_K)
    row_tok = (jnp.argsort(dest) // TOP_K).astype(jnp.int32)
    row_tok = jnp.concatenate([row_tok, jnp.zeros((ROWS_PER_ISSUE,), jnp.int32)])
    block_e = meta[:n_blocks, 0]
    block_ord = meta[:n_blocks, 1]
    used_experts = meta[:N_EXPERTS, 2]
    n_used = meta[0, 3:5]
    block_rows = meta[:n_blocks, 5]
    block_offset = meta[:n_blocks, 6]
    return block_e, block_ord, used_experts, block_rows, block_offset, row_tok, n_used, dest


def kernel(x, positions, norm1_g, w_in, q_norm_g, k_norm_g, sink_logits, sg_ln_g, sg_ln_b, sg_w, sg_b, w_branch_att, w_branch_sg, w_out, norm2_g, w_group_router, b_group_router, w_expert_router, b_expert_router, w_gate, w_up, w_down):
    B, S, D = x.shape
    T = B * S
    h = x.reshape(T, D)
    pos = positions.reshape(T)
    for l in range(norm1_g.shape[0]):
        xn = _rmsnorm(h, norm1_g[l])
        proj = _in_proj(xn, w_in[l])
        q, k, v = _qkv_prep(proj, pos, q_norm_g[l], k_norm_g[l])
        att = _attention(q, k, v, sink_logits[l], B)
        sgo = _spatial_gating(proj, sg_ln_g[l], sg_ln_b[l], sg_w[l], sg_b[l])
        merged = _merge(att, sgo, proj, w_branch_att[l], w_branch_sg[l])
        pad = LANES - N_GROUPS - N_EXPERTS
        w_router = jnp.concatenate(
            [w_group_router[l], w_expert_router[l], jnp.zeros((D, pad), F32)], axis=1).astype(BF16)
        b_router = jnp.concatenate(
            [b_group_router[l], b_expert_router[l], jnp.zeros((pad,), F32)]).reshape(1, LANES)
        h, hn, idx, wts = _out_proj(merged, w_out[l], h, norm2_g[l], w_router, b_router)
        (block_e, block_ord, used_experts, block_rows, block_offset, row_tok, n_used,
         dest) = _dispatch(idx)
        yrows = _experts(hn, w_gate[l], w_up[l], w_down[l], block_e, block_ord, used_experts,
                         block_rows, block_offset, row_tok, n_used)
        h = _combine(yrows, h, wts, dest)
    return h.reshape(B, S, D)
```

```python
import functools

import jax
import jax.numpy as jnp
from jax import lax
from jax.experimental import pallas as pl
from jax.experimental.pallas import tpu as pltpu

F32 = jnp.float32
BF16 = jnp.bfloat16

D_MODEL = 2048
HEAD_DIM = 64
ATT_WIDTH = D_MODEL // 2
ATT_HEADS = ATT_WIDTH // HEAD_DIM
ATT_KV_HEADS = ATT_HEADS // 4
Q_PER_KV = ATT_HEADS // ATT_KV_HEADS
KV_WIDTH = ATT_KV_HEADS * HEAD_DIM
WINDOW = 128
ATT_BLOCK = 128
ROPE_DIM = HEAD_DIM // 4
ROPE_HALF = ROPE_DIM // 2
ROPE_THETA = 500000.0
SG_WIDTH = D_MODEL // 2
SG_GROUP_DIM = 128
SG_GROUPS = SG_WIDTH // SG_GROUP_DIM
SG_CHUNK = 128
OFF_Q = 0
OFF_K = OFF_Q + ATT_WIDTH
OFF_V = OFF_K + KV_WIDTH
OFF_U = OFF_V + KV_WIDTH
OFF_S = OFF_U + SG_WIDTH
OFF_G = OFF_S + SG_WIDTH
IN_COLS = OFF_G + 2 * D_MODEL
N_GROUPS = 8
EXPERTS_PER_GROUP = 8
N_EXPERTS = N_GROUPS * EXPERTS_PER_GROUP
TOP_K = 2
EXPERT_FF = D_MODEL // 4
MOE_BLOCK = 128
EPS = 1e-6
NEG_INF = -1e30

LANES = 128
SUBLANES = 8
BF16_BITS = 16
HN_WORD_CHUNKS = D_MODEL // LANES // 2
ROWS_PER_ISSUE = 8
ROW_AHEAD = 4
ROW_SLOTS = ROW_AHEAD + 1
SG_CHUNKS_PER_TRIP = 2
GATE_BLOCK = 512
MERGE_GATE_BLOCKS = 2
MERGE_ROW_PARTS = 2
OUT_ROW_PARTS = 4
WEIGHT_SLOTS = 3
COMBINE_AHEAD = 1
COMBINE_SLOTS = COMBINE_AHEAD + 1
META_ROWS = 256
VMEM_LIMIT = 56 * 1024 * 1024

assert WINDOW == ATT_BLOCK


def _pack_bf16_pairs(x):
    bits = lax.bitcast_convert_type(x.astype(BF16).astype(F32), jnp.uint32)
    words = []
    for c in range(HN_WORD_CHUNKS):
        hi = bits[:, c * LANES:(c + 1) * LANES]
        lo = bits[:, (c + HN_WORD_CHUNKS) * LANES:(c + HN_WORD_CHUNKS + 1) * LANES]
        words.append(hi | (lo >> BF16_BITS))
    return words


def _unpack_bf16_pair(word):
    high_mask = jnp.uint32(((1 << BF16_BITS) - 1) << BF16_BITS)
    return (lax.bitcast_convert_type(word & high_mask, F32),
            lax.bitcast_convert_type(word << BF16_BITS, F32))


def _params(sem, vmem=VMEM_LIMIT):
    return pltpu.CompilerParams(dimension_semantics=sem, vmem_limit_bytes=vmem)


def _rmsnorm_kernel(x_ref, g_ref, o_ref):
    x = x_ref[...]
    r = lax.rsqrt(jnp.mean(x * x, axis=-1, keepdims=True) + EPS)
    o_ref[...] = ((x * r) * g_ref[...]).astype(o_ref.dtype)


def _rmsnorm(x, g, tm=1024):
    T, D = x.shape
    return pl.pallas_call(
        _rmsnorm_kernel,
        grid=(T // tm,),
        in_specs=[pl.BlockSpec((tm, D), lambda i: (i, 0)),
                  pl.BlockSpec((1, D), lambda i: (0, 0))],
        out_specs=pl.BlockSpec((tm, D), lambda i: (i, 0)),
        out_shape=jax.ShapeDtypeStruct((T, D), BF16),
        compiler_params=_params(("parallel",)),
        name="norm1",
    )(x, g.reshape(1, D))


def _proj_kernel(x_ref, w_ref, o_ref, wbf_ref):
    @pl.when(pl.program_id(1) == 0)
    def _():
        wbf_ref[...] = w_ref[...].astype(BF16)

    o_ref[...] = jnp.dot(x_ref[...], wbf_ref[...], preferred_element_type=F32)


def _in_proj(xn, w, tm=1024, tn=1280):
    T, D = xn.shape
    N = w.shape[1]
    return pl.pallas_call(
        _proj_kernel,
        grid=(N // tn, T // tm),
        in_specs=[pl.BlockSpec((tm, D), lambda j, i: (i, 0)),
                  pl.BlockSpec((D, tn), lambda j, i: (0, j))],
        out_specs=pl.BlockSpec((tm, tn), lambda j, i: (i, j)),
        out_shape=jax.ShapeDtypeStruct((T, N), F32),
        scratch_shapes=[pltpu.VMEM((D, tn), BF16)],
        compiler_params=_params(("arbitrary", "arbitrary")),
        name="in_proj",
    )(xn, w)


def _rope_table_kernel(pos_ref, invf_ref, cos_ref, sin_ref):
    ang = pos_ref[...].astype(F32) * invf_ref[...]
    cos_ref[...] = jnp.cos(ang)
    sin_ref[...] = jnp.sin(ang)


def _rope_tables(positions):
    T = positions.shape[0]
    rows = T * ROPE_HALF // LANES
    inv = ROPE_THETA ** (-jnp.arange(0, ROPE_DIM, 2, dtype=F32) / ROPE_DIM)
    invf = jnp.tile(inv, LANES // ROPE_HALF).reshape(1, LANES)
    pos = jnp.repeat(positions, ROPE_HALF).reshape(rows, LANES)
    whole = lambda: (0, 0)
    cos, sin = pl.pallas_call(
        _rope_table_kernel,
        in_specs=[pl.BlockSpec((rows, LANES), whole), pl.BlockSpec((1, LANES), whole)],
        out_specs=[pl.BlockSpec((rows, LANES), whole), pl.BlockSpec((rows, LANES), whole)],
        out_shape=[jax.ShapeDtypeStruct((rows, LANES), F32)] * 2,
        name="rope_tables",
    )(pos, invf)
    return cos.reshape(T, ROPE_HALF), sin.reshape(T, ROPE_HALF)


def _qkv_prep_kernel(p_ref, cos_ref, sin_ref, cpat_ref, spat_ref, gq_ref, gk_ref, seg_ref,
                     q_ref, k_ref, v_ref):
    lane = lax.broadcasted_iota(jnp.int32, (cos_ref.shape[0], LANES), 1)
    in_head = lane % HEAD_DIM

    def spread(t_ref, pattern_ref):
        t = t_ref[...]
        hi = t.astype(BF16)
        r1 = t - hi.astype(F32)
        mid = r1.astype(BF16)
        lo = (r1 - mid.astype(F32)).astype(BF16)
        return sum(jnp.dot(term, pattern_ref[...], preferred_element_type=F32)
                   for term in (hi, mid, lo))

    cos = jnp.where(in_head < ROPE_DIM, spread(cos_ref, cpat_ref), 1.0)
    sin = spread(sin_ref, spat_ref)
    first_half = in_head < ROPE_HALF

    def norm_rope(x, g):
        x2 = x * x
        x2_hi = x2.astype(BF16)
        x2_lo = (x2 - x2_hi.astype(F32)).astype(BF16)
        both = jnp.dot(jnp.concatenate([x2_hi, x2_lo], axis=0), seg_ref[...],
                       preferred_element_type=F32)
        ssq = both[:x.shape[0]] + both[x.shape[0]:]
        xn = (x * lax.rsqrt(ssq * (1.0 / HEAD_DIM) + EPS)) * g
        partner = jnp.where(first_half,
                            pltpu.roll(xn, LANES - ROPE_HALF, 1),
                            pltpu.roll(xn, ROPE_HALF, 1))
        return xn * cos + partner * sin

    for c in range(ATT_WIDTH // LANES):
        x = p_ref[:, OFF_Q + c * LANES:OFF_Q + (c + 1) * LANES]
        q_ref[:, c * LANES:(c + 1) * LANES] = (
            norm_rope(x, gq_ref[...]) * (HEAD_DIM ** -0.5)).astype(q_ref.dtype)
    for c in range(KV_WIDTH // LANES):
        x = p_ref[:, OFF_K + c * LANES:OFF_K + (c + 1) * LANES]
        k_ref[:, c * LANES:(c + 1) * LANES] = norm_rope(x, gk_ref[...]).astype(k_ref.dtype)
    v_ref[...] = p_ref[:, OFF_V:OFF_V + KV_WIDTH].astype(v_ref.dtype)


def _qkv_prep(proj, positions, q_g, k_g, tq=512):
    T = proj.shape[0]
    width = OFF_U
    cos8, sin8 = _rope_tables(positions)
    in_head = jnp.arange(LANES) % HEAD_DIM
    picks = (in_head[None, :] % ROPE_HALF == jnp.arange(ROPE_HALF)[:, None]) & (in_head < ROPE_DIM)
    cos_pattern = picks.astype(BF16)
    sin_pattern = (picks * jnp.where(in_head < ROPE_HALF, -1.0, 1.0)).astype(BF16)
    head_of_lane = jnp.arange(LANES) // HEAD_DIM
    same_head = (head_of_lane[:, None] == head_of_lane[None, :]).astype(BF16)
    gq = jnp.tile(q_g, LANES // HEAD_DIM).reshape(1, LANES)
    gk = jnp.tile(k_g, LANES // HEAD_DIM).reshape(1, LANES)
    row = lambda i: (i, 0)
    const = lambda i: (0, 0)
    return pl.pallas_call(
        _qkv_prep_kernel,
        grid=(T // tq,),
        in_specs=[pl.BlockSpec((tq, width), row),
                  pl.BlockSpec((tq, ROPE_HALF), row),
                  pl.BlockSpec((tq, ROPE_HALF), row),
                  pl.BlockSpec((ROPE_HALF, LANES), const),
                  pl.BlockSpec((ROPE_HALF, LANES), const),
                  pl.BlockSpec((1, LANES), const),
                  pl.BlockSpec((1, LANES), const),
                  pl.BlockSpec((LANES, LANES), const)],
        out_specs=[pl.BlockSpec((tq, ATT_WIDTH), row),
                   pl.BlockSpec((tq, KV_WIDTH), row),
                   pl.BlockSpec((tq, KV_WIDTH), row)],
        out_shape=[jax.ShapeDtypeStruct((T, ATT_WIDTH), BF16),
                   jax.ShapeDtypeStruct((T, KV_WIDTH), BF16),
                   jax.ShapeDtypeStruct((T, KV_WIDTH), BF16)],
        compiler_params=_params(("parallel",)),
        name="qkv_prep",
    )(proj, cos8, sin8, cos_pattern, sin_pattern, gq, gk, same_head)


def _attn_kernel(sink_ref, q_ref, kp_ref, kc_ref, kn_ref, vp_ref, vc_ref, vn_ref, o_ref, *, nb):
    n = pl.program_id(1)
    rows = ATT_BLOCK
    qi = lax.broadcasted_iota(jnp.int32, (rows, ATT_BLOCK), 0) % ATT_BLOCK
    kj = lax.broadcasted_iota(jnp.int32, (rows, ATT_BLOCK), 1)
    lo_prev = jnp.where(n > 0, 0, ATT_BLOCK)
    hi_next = jnp.where(n < nb - 1, 0, -ATT_BLOCK)
    cap_prev = jnp.where(kj - qi >= lo_prev, jnp.inf, NEG_INF)
    cap_next = jnp.where(kj - qi <= hi_next, jnp.inf, NEG_INF)
    cap = jnp.concatenate([cap_prev, jnp.full((rows, ATT_BLOCK), jnp.inf, F32), cap_next], axis=1)

    kvhs = range(ATT_KV_HEADS)
    scores = []
    for kvh in kvhs:
        cols = slice(kvh * HEAD_DIM, (kvh + 1) * HEAD_DIM)
        k = jnp.concatenate([kp_ref[:, cols], kc_ref[:, cols], kn_ref[:, cols]], axis=0)
        q = jnp.concatenate(
            [q_ref[:, (kvh * Q_PER_KV + g) * HEAD_DIM:(kvh * Q_PER_KV + g + 1) * HEAD_DIM]
             for g in range(Q_PER_KV)], axis=0)
        scores.append(lax.dot_general(q, k, (((1,), (1,)), ((), ())),
                                      preferred_element_type=F32))
    probs = []
    for kvh in kvhs:
        strips = []
        for g in range(Q_PER_KV):
            sink = sink_ref[kvh * Q_PER_KV + g]
            sg = jnp.minimum(scores[kvh][g * ATT_BLOCK:(g + 1) * ATT_BLOCK], cap)
            m = jnp.maximum(jnp.max(sg, axis=-1, keepdims=True), sink)
            e = jnp.exp(sg - m)
            denom = jnp.sum(e, axis=-1, keepdims=True) + jnp.exp(sink - m)
            strips.append((e / denom).astype(BF16))
        probs.append(jnp.concatenate(strips, axis=0))
    for kvh in kvhs:
        cols = slice(kvh * HEAD_DIM, (kvh + 1) * HEAD_DIM)
        v = jnp.concatenate([vp_ref[:, cols], vc_ref[:, cols], vn_ref[:, cols]], axis=0)
        o = jnp.dot(probs[kvh], v, preferred_element_type=F32)
        for g in range(Q_PER_KV):
            h = kvh * Q_PER_KV + g
            o_ref[:, h * HEAD_DIM:(h + 1) * HEAD_DIM] = (
                o[g * ATT_BLOCK:(g + 1) * ATT_BLOCK].astype(o_ref.dtype))


def _attention(q, k, v, sink, batch):
    T = q.shape[0]
    nb = T // batch // ATT_BLOCK
    cur = lambda b, n: (b * nb + n, 0)
    prev = lambda b, n: (b * nb + jnp.maximum(n - 1, 0), 0)
    nxt = lambda b, n: (b * nb + jnp.minimum(n + 1, nb - 1), 0)
    kv = lambda im: pl.BlockSpec((ATT_BLOCK, KV_WIDTH), im)
    return pl.pallas_call(
        functools.partial(_attn_kernel, nb=nb),
        grid=(batch, nb),
        in_specs=[pl.BlockSpec(memory_space=pltpu.SMEM),
                  pl.BlockSpec((ATT_BLOCK, ATT_WIDTH), cur),
                  kv(prev), kv(cur), kv(nxt), kv(prev), kv(cur), kv(nxt)],
        out_specs=pl.BlockSpec((ATT_BLOCK, ATT_WIDTH), cur),
        out_shape=jax.ShapeDtypeStruct((T, ATT_WIDTH), BF16),
        compiler_params=_params(("parallel", "parallel")),
        name="window_attn",
    )(sink, q, k, k, k, v, v, v)


def _sg_kernel(u_ref, s_ref, lng_ref, lnb_ref, w_ref, b_ref, o_ref):
    groups = w_ref.shape[0]

    def some_chunks(ci, carry):
        tiles = []
        for j in range(SG_CHUNKS_PER_TRIP):
            start = pl.multiple_of((ci * SG_CHUNKS_PER_TRIP + j) * SG_CHUNK, SG_CHUNK)
            for gi in range(groups):
                tiles.append((pl.ds(start, SG_CHUNK),
                              slice(gi * SG_GROUP_DIM, (gi + 1) * SG_GROUP_DIM), gi))
        ts = range(len(tiles))
        s = [jax.nn.gelu(s_ref[r, c]) for r, c, _ in tiles]
        sc = [s[t] - jnp.mean(s[t], axis=-1, keepdims=True) for t in ts]
        var = [jnp.mean(sc[t] * sc[t], axis=-1, keepdims=True) for t in ts]
        sn = [(sc[t] * lax.rsqrt(var[t] + EPS)) * lng_ref[0, gi:gi + 1, :]
              + lnb_ref[0, gi:gi + 1, :] for t, (_, _, gi) in enumerate(tiles)]
        mixed = [jnp.dot(w_ref[gi], sn[t].astype(BF16), preferred_element_type=F32)
                 + b_ref[0, :, gi:gi + 1] for t, (_, _, gi) in enumerate(tiles)]
        for t, (r, c, _) in enumerate(tiles):
            o_ref[r, c] = (jax.nn.gelu(u_ref[r, c]) * mixed[t]).astype(o_ref.dtype)
        return carry

    lax.fori_loop(0, u_ref.shape[0] // (SG_CHUNK * SG_CHUNKS_PER_TRIP), some_chunks, 0)


def _spatial_gating(proj, ln_g, ln_b, w_s, b_s, halves=2, chunks=8):
    T = proj.shape[0]
    half_w = SG_WIDTH // halves
    gph = SG_GROUPS // halves
    u0, s0 = OFF_U // half_w, OFF_S // half_w
    tq = chunks * SG_CHUNK
    return pl.pallas_call(
        _sg_kernel,
        grid=(T // tq, halves),
        in_specs=[pl.BlockSpec((tq, half_w), lambda i, j: (i, u0 + j)),
                  pl.BlockSpec((tq, half_w), lambda i, j: (i, s0 + j)),
                  pl.BlockSpec((1, gph, SG_GROUP_DIM), lambda i, j: (j, 0, 0)),
                  pl.BlockSpec((1, gph, SG_GROUP_DIM), lambda i, j: (j, 0, 0)),
                  pl.BlockSpec((gph, SG_CHUNK, SG_CHUNK), lambda i, j: (j, 0, 0)),
                  pl.BlockSpec((1, SG_CHUNK, gph), lambda i, j: (j, 0, 0))],
        out_specs=pl.BlockSpec((tq, half_w), lambda i, j: (i, j)),
        out_shape=jax.ShapeDtypeStruct((T, SG_WIDTH), BF16),
        compiler_params=_params(("parallel", "parallel")),
        name="spatial_gating",
    )(proj, proj,
      ln_g.reshape(halves, gph, SG_GROUP_DIM), ln_b.reshape(halves, gph, SG_GROUP_DIM),
      w_s.astype(BF16),
      b_s.reshape(halves, gph, SG_CHUNK).transpose(0, 2, 1))


def _merge_kernel(att_ref, sgo_ref, *refs):
    gate_refs = refs[:2 * MERGE_GATE_BLOCKS]
    wa_ref, wb_ref, o_ref, wa_bf, wb_bf = refs[2 * MERGE_GATE_BLOCKS:]

    @pl.when(pl.program_id(1) == 0)
    def _():
        wa_bf[...] = wa_ref[...].astype(BF16)
        wb_bf[...] = wb_ref[...].astype(BF16)

    part_rows = att_ref.shape[0] // MERGE_ROW_PARTS
    parts = [slice(p * part_rows, (p + 1) * part_rows) for p in range(MERGE_ROW_PARTS)]
    a = [jnp.dot(att_ref[r, :], wa_bf[...], preferred_element_type=F32) for r in parts]
    b = [jnp.dot(sgo_ref[r, :], wb_bf[...], preferred_element_type=F32) for r in parts]
    for p, r in enumerate(parts):
        for g in range(MERGE_GATE_BLOCKS):
            cols = slice(g * GATE_BLOCK, (g + 1) * GATE_BLOCK)
            ga_ref, gb_ref = gate_refs[g], gate_refs[MERGE_GATE_BLOCKS + g]
            m = (jax.nn.sigmoid(ga_ref[r, :]) * a[p][:, cols]
                 + jax.nn.sigmoid(gb_ref[r, :]) * b[p][:, cols])
            o_ref[r, cols] = m.astype(o_ref.dtype)


def _merge(att, sgo, proj, w_a, w_b, tm=512):
    T = att.shape[0]
    tn = MERGE_GATE_BLOCKS * GATE_BLOCK
    ga0 = OFF_G // GATE_BLOCK
    gb0 = (OFF_G + D_MODEL) // GATE_BLOCK
    gate = lambda first, g: pl.BlockSpec(
        (tm, GATE_BLOCK), lambda j, i: (i, first + j * MERGE_GATE_BLOCKS + g))
    gates = ([gate(ga0, g) for g in range(MERGE_GATE_BLOCKS)]
             + [gate(gb0, g) for g in range(MERGE_GATE_BLOCKS)])
    return pl.pallas_call(
        _merge_kernel,
        grid=(D_MODEL // tn, T // tm),
        in_specs=[pl.BlockSpec((tm, ATT_WIDTH), lambda j, i: (i, 0)),
                  pl.BlockSpec((tm, SG_WIDTH), lambda j, i: (i, 0))]
                 + gates
                 + [pl.BlockSpec((ATT_WIDTH, tn), lambda j, i: (0, j)),
                    pl.BlockSpec((SG_WIDTH, tn), lambda j, i: (0, j))],
        out_specs=pl.BlockSpec((tm, tn), lambda j, i: (i, j)),
        out_shape=jax.ShapeDtypeStruct((T, D_MODEL), BF16),
        scratch_shapes=[pltpu.VMEM((ATT_WIDTH, tn), BF16),
                        pltpu.VMEM((SG_WIDTH, tn), BF16)],
        compiler_params=_params(("arbitrary", "arbitrary")),
        name="merge",
    )(att, sgo, *([proj] * (2 * MERGE_GATE_BLOCKS)), w_a, w_b)


def _route(logits):
    lane = lax.broadcasted_iota(jnp.int32, logits.shape, 1)
    lane_f = lane.astype(F32)
    is_g = lane < N_GROUPS
    gl = jnp.where(is_g, logits, -jnp.inf)
    gmax = jnp.max(gl, axis=-1, keepdims=True)
    grp = jnp.min(jnp.where(gl == gmax, lane_f, float(LANES)), axis=-1, keepdims=True)
    gsum = jnp.sum(jnp.where(is_g, jnp.exp(logits - gmax), 0.0), axis=-1, keepdims=True)
    g_w = 1.0 / gsum
    e_lane = lane - N_GROUPS
    in_grp = jnp.logical_and(
        jnp.logical_and(e_lane >= 0, e_lane < N_EXPERTS),
        (e_lane // EXPERTS_PER_GROUP).astype(F32) == grp)
    el = jnp.where(in_grp, logits, -jnp.inf)
    v1 = jnp.max(el, axis=-1, keepdims=True)
    i1 = jnp.min(jnp.where(jnp.logical_and(in_grp, el == v1), lane_f, float(LANES)),
                 axis=-1, keepdims=True)
    rest = jnp.logical_and(in_grp, lane_f != i1)
    el2 = jnp.where(rest, logits, -jnp.inf)
    v2 = jnp.max(el2, axis=-1, keepdims=True)
    i2 = jnp.min(jnp.where(jnp.logical_and(rest, el2 == v2), lane_f, float(LANES)),
                 axis=-1, keepdims=True)
    e21 = jnp.exp(v2 - v1)
    w1 = g_w / (1.0 + e21)
    w2 = g_w * e21 / (1.0 + e21)
    idx = jnp.where(lane == 0, i1, i2) - float(N_GROUPS)
    wts = jnp.where(lane == 0, w1, jnp.where(lane == 1, w2, 0.0))
    return idx.astype(jnp.int32), wts


def _out_kernel(m_ref, w_ref, x_ref, g_ref, wr_ref, br_ref, h_ref, hn_ref, idx_ref, wt_ref):
    part_rows = m_ref.shape[0] // OUT_ROW_PARTS
    parts = [slice(p * part_rows, (p + 1) * part_rows) for p in range(OUT_ROW_PARTS)]
    hs = [x_ref[r, :] + jnp.dot(m_ref[r, :], w_ref[...], preferred_element_type=F32)
          for r in parts]
    for p, r in enumerate(parts):
        h = hs[p]
        h_ref[r, :] = h
        rs = lax.rsqrt(jnp.mean(h * h, axis=-1, keepdims=True) + EPS)
        hn = ((h * rs) * g_ref[...]).astype(BF16)
        for c, word in enumerate(_pack_bf16_pairs(hn)):
            hn_ref[pl.ds(p * part_rows * HN_WORD_CHUNKS + c, part_rows,
                         stride=HN_WORD_CHUNKS), :] = word
        logits = jnp.dot(hn, wr_ref[...], preferred_element_type=F32) + br_ref[...]
        idx, wts = _route(logits)
        idx_ref[r, :] = idx
        wt_ref[r, :] = wts


def _out_proj(merged, w_out, x, g2, w_router, b_router, tm=512):
    T, D = x.shape
    row = lambda i: (i, 0)
    const = lambda i: (0, 0)
    return pl.pallas_call(
        _out_kernel,
        grid=(T // tm,),
        in_specs=[pl.BlockSpec((tm, D), row),
                  pl.BlockSpec((D, D), const),
                  pl.BlockSpec((tm, D), row),
                  pl.BlockSpec((1, D), const),
                  pl.BlockSpec((D, LANES), const),
                  pl.BlockSpec((1, LANES), const)],
        out_specs=[pl.BlockSpec((tm, D), row),
                   pl.BlockSpec((tm * HN_WORD_CHUNKS, LANES), row),
                   pl.BlockSpec((tm, LANES), row),
                   pl.BlockSpec((tm, LANES), row)],
        out_shape=[jax.ShapeDtypeStruct((T, D), F32),
                   jax.ShapeDtypeStruct((T * HN_WORD_CHUNKS, LANES), jnp.uint32),
                   jax.ShapeDtypeStruct((T, LANES), jnp.int32),
                   jax.ShapeDtypeStruct((T, LANES), F32)],
        compiler_params=_params(("parallel",)),
        name="out_proj_router",
    )(merged, w_out.astype(BF16), x, g2.reshape(1, D), w_router, b_router)


def _expert_kernel(be_ref, eo_ref, ue_ref, nr_ref, bo_ref, rt_ref, nu_ref,
                   hn_hbm, wg_hbm, wu_hbm, wd_hbm, y_ref, xbuf, wgf, wuf, wdf, sem, wsem):
    b = pl.program_id(0)
    n_used = nu_ref[0]
    n_exp = nu_ref[1]
    used = b < n_used
    slot = b % ROW_SLOTS

    def weight_copies(ordinal, slot_):
        e = ue_ref[ordinal]
        return (pltpu.make_async_copy(wg_hbm.at[e], wgf.at[slot_], wsem.at[slot_]),
                pltpu.make_async_copy(wu_hbm.at[e], wuf.at[slot_], wsem.at[slot_]),
                pltpu.make_async_copy(wd_hbm.at[e], wdf.at[slot_], wsem.at[slot_]))

    def start_weights(ordinal):
        for c in weight_copies(ordinal, ordinal % WEIGHT_SLOTS):
            c.start()

    def issue_trips(blk):
        return (nr_ref[blk] + (ROWS_PER_ISSUE - 1)) // ROWS_PER_ISSUE

    def gather_rows(blk, slot_):
        first_row = bo_ref[blk]

        def issue(g, c):
            for j in range(ROWS_PER_ISSUE):
                r = g * ROWS_PER_ISSUE + j
                src = pl.multiple_of(rt_ref[first_row + r] * HN_WORD_CHUNKS, HN_WORD_CHUNKS)
                dst = pl.multiple_of(r * HN_WORD_CHUNKS, HN_WORD_CHUNKS)
                pltpu.make_async_copy(hn_hbm.at[pl.ds(src, HN_WORD_CHUNKS)],
                                      xbuf.at[slot_, pl.ds(dst, HN_WORD_CHUNKS)],
                                      sem.at[slot_]).start()
            return c

        lax.fori_loop(0, issue_trips(blk), issue, 0)

    @pl.when(b == 0)
    def _():
        start_weights(0)

        @pl.when(n_exp > 1)
        def _():
            start_weights(1)

        xbuf[...] = jnp.zeros(xbuf.shape, xbuf.dtype)
        gather_rows(0, 0)
        for a in range(1, ROW_AHEAD):
            @pl.when(n_used > a)
            def _(a=a):
                gather_rows(a, a)

    @pl.when(b + ROW_AHEAD < n_used)
    def _():
        gather_rows(b + ROW_AHEAD, (b + ROW_AHEAD) % ROW_SLOTS)

    @pl.when(used)
    def _():
        first = jnp.logical_or(b == 0, be_ref[b] != be_ref[jnp.maximum(b - 1, 0)])

        ordinal = eo_ref[b]
        ws = ordinal % WEIGHT_SLOTS

        @pl.when(first)
        def _():
            for c in weight_copies(ordinal, ws):
                c.wait()

            @pl.when(ordinal + 2 < n_exp)
            def _():
                start_weights(ordinal + 2)

        words = issue_trips(b) * (ROWS_PER_ISSUE * HN_WORD_CHUNKS)
        pltpu.make_async_copy(hn_hbm.at[pl.ds(0, words)], xbuf.at[slot, pl.ds(0, words)],
                              sem.at[slot]).wait()

        halves = ([], [])
        for c in range(HN_WORD_CHUNKS):
            hi, lo = _unpack_bf16_pair(xbuf[slot, pl.ds(c, MOE_BLOCK, stride=HN_WORD_CHUNKS), :])
            halves[0].append(hi.astype(BF16))
            halves[1].append(lo.astype(BF16))
        x = jnp.concatenate(halves[0] + halves[1], axis=1)
        hg = jnp.dot(x, wgf[ws].astype(BF16), preferred_element_type=F32)
        hu = jnp.dot(x, wuf[ws].astype(BF16), preferred_element_type=F32)
        hdn = (jax.nn.silu(hg) * hu).astype(BF16)
        y = jnp.dot(hdn, wdf[ws].astype(BF16), preferred_element_type=F32)
        for c, word in enumerate(_pack_bf16_pairs(y)):
            y_ref[pl.ds(c, MOE_BLOCK, stride=HN_WORD_CHUNKS), :] = word

    @pl.when(jnp.logical_not(used))
    def _():
        y_ref[...] = jnp.zeros(y_ref.shape, y_ref.dtype)


def _experts(hn, w_gate, w_up, w_down, block_e, block_ord, used_experts, block_rows,
             block_offset, row_tok, n_used):
    D = w_gate.shape[1]
    n_blocks = block_e.shape[0]
    n_rows = n_blocks * MOE_BLOCK
    hbm = pl.BlockSpec(memory_space=pl.ANY)
    grid_spec = pltpu.PrefetchScalarGridSpec(
        num_scalar_prefetch=7,
        grid=(n_blocks,),
        in_specs=[hbm, hbm, hbm, hbm],
        out_specs=pl.BlockSpec((MOE_BLOCK * HN_WORD_CHUNKS, LANES), lambda b, *_: (b, 0)),
        scratch_shapes=[pltpu.VMEM((ROW_SLOTS, MOE_BLOCK * HN_WORD_CHUNKS, LANES), jnp.uint32),
                        pltpu.VMEM((WEIGHT_SLOTS, D, EXPERT_FF), F32),
                        pltpu.VMEM((WEIGHT_SLOTS, D, EXPERT_FF), F32),
                        pltpu.VMEM((WEIGHT_SLOTS, EXPERT_FF, D), F32),
                        pltpu.SemaphoreType.DMA((ROW_SLOTS,)),
                        pltpu.SemaphoreType.DMA((WEIGHT_SLOTS,))],
    )
    return pl.pallas_call(
        _expert_kernel,
        grid_spec=grid_spec,
        out_shape=jax.ShapeDtypeStruct((n_rows * HN_WORD_CHUNKS, LANES), jnp.uint32),
        compiler_params=_params(("arbitrary",)),
        name="experts",
    )(block_e, block_ord, used_experts, block_rows, block_offset, row_tok, n_used,
      hn, w_gate, w_up, w_down)


def _combine_kernel(*refs, tc):
    dest_refs = refs[:COMBINE_SLOTS]
    y_hbm, h_ref, wt_ref, o_ref, ybuf, sem = refs[COMBINE_SLOTS:]
    i = pl.program_id(0)
    slot = i % COMBINE_SLOTS

    def gather_rows(dest_ref, slot_):
        def issue(r, c):
            dst = pl.multiple_of(r * HN_WORD_CHUNKS, HN_WORD_CHUNKS)
            for k in range(TOP_K):
                src = pl.multiple_of(dest_ref[0, 0, r * TOP_K + k] * HN_WORD_CHUNKS, HN_WORD_CHUNKS)
                pltpu.make_async_copy(y_hbm.at[pl.ds(src, HN_WORD_CHUNKS)],
                                      ybuf.at[slot_, k, pl.ds(dst, HN_WORD_CHUNKS)],
                                      sem.at[slot_]).start(priority=k % 2)
            return c

        lax.fori_loop(0, tc, issue, 0, unroll=4)

    @pl.when(i == 0)
    def _():
        for a in range(COMBINE_AHEAD):
            gather_rows(dest_refs[a], a)

    @pl.when(i + COMBINE_AHEAD < pl.num_programs(0))
    def _():
        gather_rows(dest_refs[COMBINE_AHEAD], (i + COMBINE_AHEAD) % COMBINE_SLOTS)

    for k in range(TOP_K):
        pltpu.make_async_copy(y_hbm.at[pl.ds(0, tc * HN_WORD_CHUNKS)], ybuf.at[slot, k],
                              sem.at[slot]).wait()
    wt = wt_ref[...]
    w1, w2 = wt[:, 0:1], wt[:, 1:2]
    for c in range(HN_WORD_CHUNKS):
        rows = pl.ds(c, tc, stride=HN_WORD_CHUNKS)
        y1, y2 = (_unpack_bf16_pair(ybuf[slot, k, rows, :]) for k in range(TOP_K))
        for half, col in enumerate((c * LANES, (c + HN_WORD_CHUNKS) * LANES)):
            o_ref[:, col:col + LANES] = h_ref[:, col:col + LANES] + (w1 * y1[half] + w2 * y2[half])


def _combine(yrows, h, wts, dest, tc=512):
    T, D = h.shape
    steps = T // tc
    row = lambda i: (i, 0)
    dest3 = dest.reshape(steps, 1, tc * TOP_K)
    dest_block = lambda im: pl.BlockSpec((1, 1, tc * TOP_K), im, memory_space=pltpu.SMEM)
    return pl.pallas_call(
        functools.partial(_combine_kernel, tc=tc),
        grid=(steps,),
        in_specs=[dest_block(lambda i, a=a: (jnp.minimum(i + a, steps - 1), 0, 0))
                  for a in range(COMBINE_SLOTS)]
                 + [pl.BlockSpec(memory_space=pl.ANY),
                    pl.BlockSpec((tc, D), row),
                    pl.BlockSpec((tc, LANES), row)],
        out_specs=pl.BlockSpec((tc, D), row),
        out_shape=jax.ShapeDtypeStruct((T, D), F32),
        scratch_shapes=[pltpu.VMEM((COMBINE_SLOTS, TOP_K, tc * HN_WORD_CHUNKS, LANES), jnp.uint32),
                        pltpu.SemaphoreType.DMA((COMBINE_SLOTS,))],
        compiler_params=_params(("arbitrary",)),
        name="combine",
    )(*([dest3] * COMBINE_SLOTS), yrows, h, wts)


def _lane_cumsum(x):
    lane = lax.broadcasted_iota(jnp.int32, x.shape, 1)
    s = 1
    while s < LANES:
        x = x + jnp.where(lane >= s, pltpu.roll(x, s, 1), 0)
        s *= 2
    return x


def _dispatch_kernel(idx_ref, dest_ref, meta_ref, run_ref, prefix_ref, start_ref, *, tb):
    p = pl.program_id(0)
    i = pl.program_id(1)
    idx = idx_ref[...]
    lane = lax.broadcasted_iota(jnp.int32, idx.shape, 1)
    e1 = idx[:, 0:1]
    e2 = idx[:, 1:2]
    onehot = jnp.where(jnp.logical_or(lane == e1, lane == e2), 1.0, 0.0)

    @pl.when(jnp.logical_and(p == 0, i == 0))
    def _():
        run_ref[...] = jnp.zeros(run_ref.shape, F32)

    @pl.when(p == 0)
    def _():
        prefix_ref[i] = run_ref[...]
        run_ref[...] = run_ref[...] + jnp.sum(onehot, axis=0, keepdims=True)

    @pl.when(jnp.logical_and(p == 1, i == 0))
    def _():
        counts = run_ref[...].astype(jnp.int32)
        nblk = (counts + (MOE_BLOCK - 1)) // MOE_BLOCK
        end_blk = _lane_cumsum(nblk)
        start_ref[...] = ((end_blk - nblk) * MOE_BLOCK).astype(F32)
        has = jnp.where(counts > 0, 1, 0)
        ordinal = _lane_cumsum(has) - 1
        first_row = (_lane_cumsum(counts) - counts)[0:1]
        start_blk, counts = (end_blk - nblk)[0:1], counts[0:1]
        end_blk, has, ordinal = end_blk[0:1], has[0:1], ordinal[0:1]
        rows = lax.broadcasted_iota(jnp.int32, (META_ROWS, LANES), 0)
        lanes = lax.broadcasted_iota(jnp.int32, (META_ROWS, LANES), 1)
        is_e = lanes < N_EXPERTS
        rsum = lambda v: jnp.sum(v, axis=-1, keepdims=True)
        be = rsum(jnp.where(jnp.logical_and(is_e, end_blk <= rows), 1, 0))
        be = jnp.minimum(be, N_EXPERTS - 1)
        eo = rsum(jnp.where(lanes == be, ordinal, 0))
        ue = rsum(jnp.where(jnp.logical_and(has > 0, ordinal == rows), lanes, 0))
        n_blk = rsum(jnp.where(lanes == N_EXPERTS - 1, end_blk, 0))
        n_exp = rsum(jnp.where(is_e, has, 0))
        own = lanes == be
        done = MOE_BLOCK * (rows - rsum(jnp.where(own, start_blk, 0)))
        n_valid = jnp.clip(rsum(jnp.where(own, counts, 0)) - done, 0, MOE_BLOCK)
        offset = rsum(jnp.where(own, first_row, 0)) + done
        columns = (be, eo, ue, n_blk, n_exp, n_valid, offset)
        meta = jnp.zeros((META_ROWS, LANES), jnp.int32)
        for col, val in enumerate(columns):
            meta = jnp.where(lanes == col, val, meta)
        meta_ref[...] = meta

    @pl.when(p == 1)
    def _():
        r = lax.broadcasted_iota(jnp.int32, (tb, tb), 0)
        c = lax.broadcasted_iota(jnp.int32, (tb, tb), 1)
        earlier = jnp.where(c < r, 1.0, 0.0).astype(BF16)
        rank = jnp.dot(earlier, onehot.astype(BF16), preferred_element_type=F32)
        rank = rank + prefix_ref[i][0:1] + start_ref[0:1]
        d1 = jnp.sum(jnp.where(lane == e1, rank, 0.0), axis=-1, keepdims=True)
        d2 = jnp.sum(jnp.where(lane == e2, rank, 0.0), axis=-1, keepdims=True)
        dest_ref[...] = jnp.where(lane == 0, d1, d2).astype(jnp.int32)


def _dispatch(idx, tb=512):
    T = idx.shape[0]
    n_rows = T * TOP_K + N_EXPERTS * MOE_BLOCK
    n_blocks = n_rows // MOE_BLOCK
    assert n_blocks <= META_ROWS
    dest2, meta = pl.pallas_call(
        functools.partial(_dispatch_kernel, tb=tb),
        grid=(2, T // tb),
        in_specs=[pl.BlockSpec((tb, LANES), lambda p, i: (i, 0))],
        out_specs=[pl.BlockSpec((tb, LANES), lambda p, i: (i * p, 0)),
                   pl.BlockSpec((META_ROWS, LANES), lambda p, i: (0, 0))],
        out_shape=[jax.ShapeDtypeStruct((T, LANES), jnp.int32),
                   jax.ShapeDtypeStruct((META_ROWS, LANES), jnp.int32)],
        scratch_shapes=[pltpu.VMEM((SUBLANES, LANES), F32),
                        pltpu.VMEM((T // tb, SUBLANES, LANES), F32),
                        pltpu.VMEM((SUBLANES, LANES), F32)],
        compiler_params=_params(("arbitrary", "arbitrary")),
        name="dispatch",
    )(idx)
    dest = dest2[:, :TOP_K].reshape(T * TOP_K)
    row_tok = (jnp.argsort(dest) // TOP_K).astype(jnp.int32)
    row_tok = jnp.concatenate([row_tok, jnp.zeros((ROWS_PER_ISSUE,), jnp.int32)])
    block_e = meta[:n_blocks, 0]
    block_ord = meta[:n_blocks, 1]
    used_experts = meta[:N_EXPERTS, 2]
    n_used = meta[0, 3:5]
    block_rows = meta[:n_blocks, 5]
    block_offset = meta[:n_blocks, 6]
    return block_e, block_ord, used_experts, block_rows, block_offset, row_tok, n_used, dest


def kernel(x, positions, norm1_g, w_in, q_norm_g, k_norm_g, sink_logits, sg_ln_g, sg_ln_b, sg_w, sg_b, w_branch_att, w_branch_sg, w_out, norm2_g, w_group_router, b_group_router, w_expert_router, b_expert_router, w_gate, w_up, w_down):
    B, S, D = x.shape
    T = B * S
    assert D == D_MODEL and S % (2 * ATT_BLOCK) == 0 and S % SG_CHUNK == 0
    assert x.dtype == F32 and positions.shape == (B, S)
    assert w_in.shape[1:] == (D_MODEL, IN_COLS) and w_out.shape[1:] == (D_MODEL, D_MODEL)
    assert w_gate.shape[1:] == (N_EXPERTS, D_MODEL, EXPERT_FF) and w_down.shape[1:] == (
        N_EXPERTS, EXPERT_FF, D_MODEL)
    h = x.reshape(T, D)
    pos = positions.reshape(T)
    for l in range(norm1_g.shape[0]):
        xn = _rmsnorm(h, norm1_g[l])
        proj = _in_proj(xn, w_in[l])
        q, k, v = _qkv_prep(proj, pos, q_norm_g[l], k_norm_g[l])
        att = _attention(q, k, v, sink_logits[l], B)
        sgo = _spatial_gating(proj, sg_ln_g[l], sg_ln_b[l], sg_w[l], sg_b[l])
        merged = _merge(att, sgo, proj, w_branch_att[l], w_branch_sg[l])
        pad = LANES - N_GROUPS - N_EXPERTS
        w_router = jnp.concatenate(
            [w_group_router[l], w_expert_router[l], jnp.zeros((D, pad), F32)], axis=1).astype(BF16)
        b_router = jnp.concatenate(
            [b_group_router[l], b_expert_router[l], jnp.zeros((pad,), F32)]).reshape(1, LANES)
        h, hn, idx, wts = _out_proj(merged, w_out[l], h, norm2_g[l], w_router, b_router)
        (block_e, block_ord, used_experts, block_rows, block_offset, row_tok, n_used,
         dest) = _dispatch(idx)
        yrows = _experts(hn, w_gate[l], w_up[l], w_down[l], block_e, block_ord, used_experts,
                         block_rows, block_offset, row_tok, n_used)
        h = _combine(yrows, h, wts, dest)
    return h.reshape(B, S, D)
```

```python
import functools

import jax
import jax.numpy as jnp
from jax import lax
from jax.experimental import pallas as pl
from jax.experimental.pallas import tpu as pltpu

F32 = jnp.float32
BF16 = jnp.bfloat16

D_MODEL = 2048
HEAD_DIM = 64
ATT_WIDTH = D_MODEL // 2
ATT_HEADS = ATT_WIDTH // HEAD_DIM
ATT_KV_HEADS = ATT_HEADS // 4
Q_PER_KV = ATT_HEADS // ATT_KV_HEADS
KV_WIDTH = ATT_KV_HEADS * HEAD_DIM
WINDOW = 128
ATT_BLOCK = 128
ROPE_DIM = HEAD_DIM // 4
ROPE_HALF = ROPE_DIM // 2
ROPE_THETA = 500000.0
SG_WIDTH = D_MODEL // 2
SG_GROUP_DIM = 128
SG_GROUPS = SG_WIDTH // SG_GROUP_DIM
SG_CHUNK = 128
OFF_Q = 0
OFF_K = OFF_Q + ATT_WIDTH
OFF_V = OFF_K + KV_WIDTH
OFF_U = OFF_V + KV_WIDTH
OFF_S = OFF_U + SG_WIDTH
OFF_G = OFF_S + SG_WIDTH
IN_COLS = OFF_G + 2 * D_MODEL
N_GROUPS = 8
EXPERTS_PER_GROUP = 8
N_EXPERTS = N_GROUPS * EXPERTS_PER_GROUP
TOP_K = 2
EXPERT_FF = D_MODEL // 4
MOE_BLOCK = 128
EPS = 1e-6
NEG_INF = -1e30

LANES = 128
SUBLANES = 8
BF16_BITS = 16
HN_WORD_CHUNKS = D_MODEL // LANES // 2
ROWS_PER_ISSUE = 8
ROW_AHEAD = 4
ROW_SLOTS = ROW_AHEAD + 1
SG_CHUNKS_PER_TRIP = 2
GATE_BLOCK = 512
MERGE_GATE_BLOCKS = 2
MERGE_ROW_PARTS = 2
OUT_ROW_PARTS = 4
WEIGHT_SLOTS = 3
COMBINE_AHEAD = 1
COMBINE_SLOTS = COMBINE_AHEAD + 1
META_ROWS = 256
VMEM_LIMIT = 56 * 1024 * 1024

assert WINDOW == ATT_BLOCK


def _pack_bf16_pairs(x):
    bits = lax.bitcast_convert_type(x.astype(BF16).astype(F32), jnp.uint32)
    words = []
    for c in range(HN_WORD_CHUNKS):
        hi = bits[:, c * LANES:(c + 1) * LANES]
        lo = bits[:, (c + HN_WORD_CHUNKS) * LANES:(c + HN_WORD_CHUNKS + 1) * LANES]
        words.append(hi | (lo >> BF16_BITS))
    return words


def _unpack_bf16_pair(word):
    high_mask = jnp.uint32(((1 << BF16_BITS) - 1) << BF16_BITS)
    return (lax.bitcast_convert_type(word & high_mask, F32),
            lax.bitcast_convert_type(word << BF16_BITS, F32))


def _params(sem, vmem=VMEM_LIMIT):
    return pltpu.CompilerParams(dimension_semantics=sem, vmem_limit_bytes=vmem)


def _rmsnorm_kernel(x_ref, g_ref, o_ref):
    x = x_ref[...]
    r = lax.rsqrt(jnp.mean(x * x, axis=-1, keepdims=True) + EPS)
    o_ref[...] = ((x * r) * g_ref[...]).astype(o_ref.dtype)


def _rmsnorm(x, g, tm=1024):
    T, D = x.shape
    return pl.pallas_call(
        _rmsnorm_kernel,
        grid=(T // tm,),
        in_specs=[pl.BlockSpec((tm, D), lambda i: (i, 0)),
                  pl.BlockSpec((1, D), lambda i: (0, 0))],
        out_specs=pl.BlockSpec((tm, D), lambda i: (i, 0)),
        out_shape=jax.ShapeDtypeStruct((T, D), BF16),
        compiler_params=_params(("parallel",)),
        name="norm1",
    )(x, g.reshape(1, D))


def _proj_kernel(x_ref, w_ref, o_ref, wbf_ref):
    @pl.when(pl.program_id(1) == 0)
    def _():
        wbf_ref[...] = w_ref[...].astype(BF16)

    o_ref[...] = jnp.dot(x_ref[...], wbf_ref[...], preferred_element_type=F32)


def _in_proj(xn, w, tm=1024, tn=1280):
    T, D = xn.shape
    N = w.shape[1]
    return pl.pallas_call(
        _proj_kernel,
        grid=(N // tn, T // tm),
        in_specs=[pl.BlockSpec((tm, D), lambda j, i: (i, 0)),
                  pl.BlockSpec((D, tn), lambda j, i: (0, j))],
        out_specs=pl.BlockSpec((tm, tn), lambda j, i: (i, j)),
        out_shape=jax.ShapeDtypeStruct((T, N), F32),
        scratch_shapes=[pltpu.VMEM((D, tn), BF16)],
        compiler_params=_params(("arbitrary", "arbitrary")),
        name="in_proj",
    )(xn, w)


def _rope_table_kernel(pos_ref, invf_ref, cos_ref, sin_ref):
    ang = pos_ref[...].astype(F32) * invf_ref[...]
    cos_ref[...] = jnp.cos(ang)
    sin_ref[...] = jnp.sin(ang)


def _rope_tables(positions):
    T = positions.shape[0]
    rows = T * ROPE_HALF // LANES
    inv = ROPE_THETA ** (-jnp.arange(0, ROPE_DIM, 2, dtype=F32) / ROPE_DIM)
    invf = jnp.tile(inv, LANES // ROPE_HALF).reshape(1, LANES)
    pos = jnp.repeat(positions, ROPE_HALF).reshape(rows, LANES)
    whole = lambda: (0, 0)
    cos, sin = pl.pallas_call(
        _rope_table_kernel,
        in_specs=[pl.BlockSpec((rows, LANES), whole), pl.BlockSpec((1, LANES), whole)],
        out_specs=[pl.BlockSpec((rows, LANES), whole), pl.BlockSpec((rows, LANES), whole)],
        out_shape=[jax.ShapeDtypeStruct((rows, LANES), F32)] * 2,
        name="rope_tables",
    )(pos, invf)
    return cos.reshape(T, ROPE_HALF), sin.reshape(T, ROPE_HALF)


def _qkv_prep_kernel(p_ref, cos_ref, sin_ref, cpat_ref, spat_ref, gq_ref, gk_ref, seg_ref,
                     q_ref, k_ref, v_ref):
    lane = lax.broadcasted_iota(jnp.int32, (cos_ref.shape[0], LANES), 1)
    in_head = lane % HEAD_DIM

    def spread(t_ref, pattern_ref):
        t = t_ref[...]
        hi = t.astype(BF16)
        r1 = t - hi.astype(F32)
        mid = r1.astype(BF16)
        lo = (r1 - mid.astype(F32)).astype(BF16)
        return sum(jnp.dot(term, pattern_ref[...], preferred_element_type=F32)
                   for term in (hi, mid, lo))

    cos = jnp.where(in_head < ROPE_DIM, spread(cos_ref, cpat_ref), 1.0)
    sin = spread(sin_ref, spat_ref)
    first_half = in_head < ROPE_HALF

    def norm_rope(x, g):
        x2 = x * x
        x2_hi = x2.astype(BF16)
        x2_lo = (x2 - x2_hi.astype(F32)).astype(BF16)
        both = jnp.dot(jnp.concatenate([x2_hi, x2_lo], axis=0), seg_ref[...],
                       preferred_element_type=F32)
        ssq = both[:x.shape[0]] + both[x.shape[0]:]
        xn = (x * lax.rsqrt(ssq * (1.0 / HEAD_DIM) + EPS)) * g
        partner = jnp.where(first_half,
                            pltpu.roll(xn, LANES - ROPE_HALF, 1),
                            pltpu.roll(xn, ROPE_HALF, 1))
        return xn * cos + partner * sin

    for c in range(ATT_WIDTH // LANES):
        x = p_ref[:, OFF_Q + c * LANES:OFF_Q + (c + 1) * LANES]
        q_ref[:, c * LANES:(c + 1) * LANES] = (
            norm_rope(x, gq_ref[...]) * (HEAD_DIM ** -0.5)).astype(q_ref.dtype)
    for c in range(KV_WIDTH // LANES):
        x = p_ref[:, OFF_K + c * LANES:OFF_K + (c + 1) * LANES]
        k_ref[:, c * LANES:(c + 1) * LANES] = norm_rope(x, gk_ref[...]).astype(k_ref.dtype)
    v_ref[...] = p_ref[:, OFF_V:OFF_V + KV_WIDTH].astype(v_ref.dtype)


def _qkv_prep(proj, positions, q_g, k_g, tq=512):
    T = proj.shape[0]
    width = OFF_U
    cos8, sin8 = _rope_tables(positions)
    in_head = jnp.arange(LANES) % HEAD_DIM
    picks = (in_head[None, :] % ROPE_HALF == jnp.arange(ROPE_HALF)[:, None]) & (in_head < ROPE_DIM)
    cos_pattern = picks.astype(BF16)
    sin_pattern = (picks * jnp.where(in_head < ROPE_HALF, -1.0, 1.0)).astype(BF16)
    head_of_lane = jnp.arange(LANES) // HEAD_DIM
    same_head = (head_of_lane[:, None] == head_of_lane[None, :]).astype(BF16)
    gq = jnp.tile(q_g, LANES // HEAD_DIM).reshape(1, LANES)
    gk = jnp.tile(k_g, LANES // HEAD_DIM).reshape(1, LANES)
    row = lambda i: (i, 0)
    const = lambda i: (0, 0)
    return pl.pallas_call(
        _qkv_prep_kernel,
        grid=(T // tq,),
        in_specs=[pl.BlockSpec((tq, width), row),
                  pl.BlockSpec((tq, ROPE_HALF), row),
                  pl.BlockSpec((tq, ROPE_HALF), row),
                  pl.BlockSpec((ROPE_HALF, LANES), const),
                  pl.BlockSpec((ROPE_HALF, LANES), const),
                  pl.BlockSpec((1, LANES), const),
                  pl.BlockSpec((1, LANES), const),
                  pl.BlockSpec((LANES, LANES), const)],
        out_specs=[pl.BlockSpec((tq, ATT_WIDTH), row),
                   pl.BlockSpec((tq, KV_WIDTH), row),
                   pl.BlockSpec((tq, KV_WIDTH), row)],
        out_shape=[jax.ShapeDtypeStruct((T, ATT_WIDTH), BF16),
                   jax.ShapeDtypeStruct((T, KV_WIDTH), BF16),
                   jax.ShapeDtypeStruct((T, KV_WIDTH), BF16)],
        compiler_params=_params(("parallel",)),
        name="qkv_prep",
    )(proj, cos8, sin8, cos_pattern, sin_pattern, gq, gk, same_head)


def _attn_kernel(sink_ref, q_ref, kp_ref, kc_ref, kn_ref, vp_ref, vc_ref, vn_ref, o_ref, *, nb):
    n = pl.program_id(1)
    rows = ATT_BLOCK
    qi = lax.broadcasted_iota(jnp.int32, (rows, ATT_BLOCK), 0) % ATT_BLOCK
    kj = lax.broadcasted_iota(jnp.int32, (rows, ATT_BLOCK), 1)
    lo_prev = jnp.where(n > 0, 0, ATT_BLOCK)
    hi_next = jnp.where(n < nb - 1, 0, -ATT_BLOCK)
    cap_prev = jnp.where(kj - qi >= lo_prev, jnp.inf, NEG_INF)
    cap_next = jnp.where(kj - qi <= hi_next, jnp.inf, NEG_INF)
    cap = jnp.concatenate([cap_prev, jnp.full((rows, ATT_BLOCK), jnp.inf, F32), cap_next], axis=1)

    kvhs = range(ATT_KV_HEADS)
    scores = []
    for kvh in kvhs:
        cols = slice(kvh * HEAD_DIM, (kvh + 1) * HEAD_DIM)
        k = jnp.concatenate([kp_ref[:, cols], kc_ref[:, cols], kn_ref[:, cols]], axis=0)
        q = jnp.concatenate(
            [q_ref[:, (kvh * Q_PER_KV + g) * HEAD_DIM:(kvh * Q_PER_KV + g + 1) * HEAD_DIM]
             for g in range(Q_PER_KV)], axis=0)
        scores.append(lax.dot_general(q, k, (((1,), (1,)), ((), ())),
                                      preferred_element_type=F32))
    probs = []
    for kvh in kvhs:
        strips = []
        for g in range(Q_PER_KV):
            sink = sink_ref[kvh * Q_PER_KV + g]
            sg = jnp.minimum(scores[kvh][g * ATT_BLOCK:(g + 1) * ATT_BLOCK], cap)
            m = jnp.maximum(jnp.max(sg, axis=-1, keepdims=True), sink)
            e = jnp.exp(sg - m)
            denom = jnp.sum(e, axis=-1, keepdims=True) + jnp.exp(sink - m)
            strips.append((e / denom).astype(BF16))
        probs.append(jnp.concatenate(strips, axis=0))
    for kvh in kvhs:
        cols = slice(kvh * HEAD_DIM, (kvh + 1) * HEAD_DIM)
        v = jnp.concatenate([vp_ref[:, cols], vc_ref[:, cols], vn_ref[:, cols]], axis=0)
        o = jnp.dot(probs[kvh], v, preferred_element_type=F32)
        for g in range(Q_PER_KV):
            h = kvh * Q_PER_KV + g
            o_ref[:, h * HEAD_DIM:(h + 1) * HEAD_DIM] = (
                o[g * ATT_BLOCK:(g + 1) * ATT_BLOCK].astype(o_ref.dtype))


def _attention(q, k, v, sink, batch):
    T = q.shape[0]
    nb = T // batch // ATT_BLOCK
    cur = lambda b, n: (b * nb + n, 0)
    prev = lambda b, n: (b * nb + jnp.maximum(n - 1, 0), 0)
    nxt = lambda b, n: (b * nb + jnp.minimum(n + 1, nb - 1), 0)
    kv = lambda im: pl.BlockSpec((ATT_BLOCK, KV_WIDTH), im)
    return pl.pallas_call(
        functools.partial(_attn_kernel, nb=nb),
        grid=(batch, nb),
        in_specs=[pl.BlockSpec(memory_space=pltpu.SMEM),
                  pl.BlockSpec((ATT_BLOCK, ATT_WIDTH), cur),
                  kv(prev), kv(cur), kv(nxt), kv(prev), kv(cur), kv(nxt)],
        out_specs=pl.BlockSpec((ATT_BLOCK, ATT_WIDTH), cur),
        out_shape=jax.ShapeDtypeStruct((T, ATT_WIDTH), BF16),
        compiler_params=_params(("parallel", "parallel")),
        name="window_attn",
    )(sink, q, k, k, k, v, v, v)


def _sg_kernel(u_ref, s_ref, lng_ref, lnb_ref, w_ref, b_ref, o_ref):
    groups = w_ref.shape[0]

    def some_chunks(ci, carry):
        tiles = []
        for j in range(SG_CHUNKS_PER_TRIP):
            start = pl.multiple_of((ci * SG_CHUNKS_PER_TRIP + j) * SG_CHUNK, SG_CHUNK)
            for gi in range(groups):
                tiles.append((pl.ds(start, SG_CHUNK),
                              slice(gi * SG_GROUP_DIM, (gi + 1) * SG_GROUP_DIM), gi))
        ts = range(len(tiles))
        s = [jax.nn.gelu(s_ref[r, c]) for r, c, _ in tiles]
        sc = [s[t] - jnp.mean(s[t], axis=-1, keepdims=True) for t in ts]
        var = [jnp.mean(sc[t] * sc[t], axis=-1, keepdims=True) for t in ts]
        sn = [(sc[t] * lax.rsqrt(var[t] + EPS)) * lng_ref[0, gi:gi + 1, :]
              + lnb_ref[0, gi:gi + 1, :] for t, (_, _, gi) in enumerate(tiles)]
        mixed = [jnp.dot(w_ref[gi], sn[t].astype(BF16), preferred_element_type=F32)
                 + b_ref[0, :, gi:gi + 1] for t, (_, _, gi) in enumerate(tiles)]
        for t, (r, c, _) in enumerate(tiles):
            o_ref[r, c] = (jax.nn.gelu(u_ref[r, c]) * mixed[t]).astype(o_ref.dtype)
        return carry

    lax.fori_loop(0, u_ref.shape[0] // (SG_CHUNK * SG_CHUNKS_PER_TRIP), some_chunks, 0)


def _spatial_gating(proj, ln_g, ln_b, w_s, b_s, halves=2, chunks=8):
    T = proj.shape[0]
    half_w = SG_WIDTH // halves
    gph = SG_GROUPS // halves
    u0, s0 = OFF_U // half_w, OFF_S // half_w
    tq = chunks * SG_CHUNK
    return pl.pallas_call(
        _sg_kernel,
        grid=(T // tq, halves),
        in_specs=[pl.BlockSpec((tq, half_w), lambda i, j: (i, u0 + j)),
                  pl.BlockSpec((tq, half_w), lambda i, j: (i, s0 + j)),
                  pl.BlockSpec((1, gph, SG_GROUP_DIM), lambda i, j: (j, 0, 0)),
                  pl.BlockSpec((1, gph, SG_GROUP_DIM), lambda i, j: (j, 0, 0)),
                  pl.BlockSpec((gph, SG_CHUNK, SG_CHUNK), lambda i, j: (j, 0, 0)),
                  pl.BlockSpec((1, SG_CHUNK, gph), lambda i, j: (j, 0, 0))],
        out_specs=pl.BlockSpec((tq, half_w), lambda i, j: (i, j)),
        out_shape=jax.ShapeDtypeStruct((T, SG_WIDTH), BF16),
        compiler_params=_params(("parallel", "parallel")),
        name="spatial_gating",
    )(proj, proj,
      ln_g.reshape(halves, gph, SG_GROUP_DIM), ln_b.reshape(halves, gph, SG_GROUP_DIM),
      w_s.astype(BF16),
      b_s.reshape(halves, gph, SG_CHUNK).transpose(0, 2, 1))


def _merge_kernel(att_ref, sgo_ref, *refs):
    gate_refs = refs[:2 * MERGE_GATE_BLOCKS]
    wa_ref, wb_ref, o_ref, wa_bf, wb_bf = refs[2 * MERGE_GATE_BLOCKS:]

    @pl.when(pl.program_id(1) == 0)
    def _():
        wa_bf[...] = wa_ref[...].astype(BF16)
        wb_bf[...] = wb_ref[...].astype(BF16)

    part_rows = att_ref.shape[0] // MERGE_ROW_PARTS
    parts = [slice(p * part_rows, (p + 1) * part_rows) for p in range(MERGE_ROW_PARTS)]
    a = [jnp.dot(att_ref[r, :], wa_bf[...], preferred_element_type=F32) for r in parts]
    b = [jnp.dot(sgo_ref[r, :], wb_bf[...], preferred_element_type=F32) for r in parts]
    for p, r in enumerate(parts):
        for g in range(MERGE_GATE_BLOCKS):
            cols = slice(g * GATE_BLOCK, (g + 1) * GATE_BLOCK)
            ga_ref, gb_ref = gate_refs[g], gate_refs[MERGE_GATE_BLOCKS + g]
            m = (jax.nn.sigmoid(ga_ref[r, :]) * a[p][:, cols]
                 + jax.nn.sigmoid(gb_ref[r, :]) * b[p][:, cols])
            o_ref[r, cols] = m.astype(o_ref.dtype)


def _merge(att, sgo, proj, w_a, w_b, tm=512):
    T = att.shape[0]
    tn = MERGE_GATE_BLOCKS * GATE_BLOCK
    ga0 = OFF_G // GATE_BLOCK
    gb0 = (OFF_G + D_MODEL) // GATE_BLOCK
    gate = lambda first, g: pl.BlockSpec(
        (tm, GATE_BLOCK), lambda j, i: (i, first + j * MERGE_GATE_BLOCKS + g))
    gates = ([gate(ga0, g) for g in range(MERGE_GATE_BLOCKS)]
             + [gate(gb0, g) for g in range(MERGE_GATE_BLOCKS)])
    return pl.pallas_call(
        _merge_kernel,
        grid=(D_MODEL // tn, T // tm),
        in_specs=[pl.BlockSpec((tm, ATT_WIDTH), lambda j, i: (i, 0)),
                  pl.BlockSpec((tm, SG_WIDTH), lambda j, i: (i, 0))]
                 + gates
                 + [pl.BlockSpec((ATT_WIDTH, tn), lambda j, i: (0, j)),
                    pl.BlockSpec((SG_WIDTH, tn), lambda j, i: (0, j))],
        out_specs=pl.BlockSpec((tm, tn), lambda j, i: (i, j)),
        out_shape=jax.ShapeDtypeStruct((T, D_MODEL), BF16),
        scratch_shapes=[pltpu.VMEM((ATT_WIDTH, tn), BF16),
                        pltpu.VMEM((SG_WIDTH, tn), BF16)],
        compiler_params=_params(("arbitrary", "arbitrary")),
        name="merge",
    )(att, sgo, *([proj] * (2 * MERGE_GATE_BLOCKS)), w_a, w_b)


def _route(logits):
    lane = lax.broadcasted_iota(jnp.int32, logits.shape, 1)
    lane_f = lane.astype(F32)
    is_g = lane < N_GROUPS
    gl = jnp.where(is_g, logits, -jnp.inf)
    gmax = jnp.max(gl, axis=-1, keepdims=True)
    grp = jnp.min(jnp.where(gl == gmax, lane_f, float(LANES)), axis=-1, keepdims=True)
    gsum = jnp.sum(jnp.where(is_g, jnp.exp(logits - gmax), 0.0), axis=-1, keepdims=True)
    g_w = 1.0 / gsum
    e_lane = lane - N_GROUPS
    in_grp = jnp.logical_and(
        jnp.logical_and(e_lane >= 0, e_lane < N_EXPERTS),
        (e_lane // EXPERTS_PER_GROUP).astype(F32) == grp)
    el = jnp.where(in_grp, logits, -jnp.inf)
    v1 = jnp.max(el, axis=-1, keepdims=True)
    i1 = jnp.min(jnp.where(jnp.logical_and(in_grp, el == v1), lane_f, float(LANES)),
                 axis=-1, keepdims=True)
    rest = jnp.logical_and(in_grp, lane_f != i1)
    el2 = jnp.where(rest, logits, -jnp.inf)
    v2 = jnp.max(el2, axis=-1, keepdims=True)
    i2 = jnp.min(jnp.where(jnp.logical_and(rest, el2 == v2), lane_f, float(LANES)),
                 axis=-1, keepdims=True)
    e21 = jnp.exp(v2 - v1)
    w1 = g_w / (1.0 + e21)
    w2 = g_w * e21 / (1.0 + e21)
    idx = jnp.where(lane == 0, i1, i2) - float(N_GROUPS)
    wts = jnp.where(lane == 0, w1, jnp.where(lane == 1, w2, 0.0))
    return idx.astype(jnp.int32), wts


def _out_kernel(m_ref, w_ref, x_ref, g_ref, wr_ref, br_ref, h_ref, hn_ref, idx_ref, wt_ref):
    part_rows = m_ref.shape[0] // OUT_ROW_PARTS
    parts = [slice(p * part_rows, (p + 1) * part_rows) for p in range(OUT_ROW_PARTS)]
    hs = [x_ref[r, :] + jnp.dot(m_ref[r, :], w_ref[...], preferred_element_type=F32)
          for r in parts]
    for p, r in enumerate(parts):
        h = hs[p]
        h_ref[r, :] = h
        rs = lax.rsqrt(jnp.mean(h * h, axis=-1, keepdims=True) + EPS)
        hn = ((h * rs) * g_ref[...]).astype(BF16)
        for c, word in enumerate(_pack_bf16_pairs(hn)):
            hn_ref[pl.ds(p * part_rows * HN_WORD_CHUNKS + c, part_rows,
                         stride=HN_WORD_CHUNKS), :] = word
        logits = jnp.dot(hn, wr_ref[...], preferred_element_type=F32) + br_ref[...]
        idx, wts = _route(logits)
        idx_ref[r, :] = idx
        wt_ref[r, :] = wts


def _out_proj(merged, w_out, x, g2, w_router, b_router, tm=512):
    T, D = x.shape
    row = lambda i: (i, 0)
    const = lambda i: (0, 0)
    return pl.pallas_call(
        _out_kernel,
        grid=(T // tm,),
        in_specs=[pl.BlockSpec((tm, D), row),
                  pl.BlockSpec((D, D), const),
                  pl.BlockSpec((tm, D), row),
                  pl.BlockSpec((1, D), const),
                  pl.BlockSpec((D, LANES), const),
                  pl.BlockSpec((1, LANES), const)],
        out_specs=[pl.BlockSpec((tm, D), row),
                   pl.BlockSpec((tm * HN_WORD_CHUNKS, LANES), row),
                   pl.BlockSpec((tm, LANES), row),
                   pl.BlockSpec((tm, LANES), row)],
        out_shape=[jax.ShapeDtypeStruct((T, D), F32),
                   jax.ShapeDtypeStruct((T * HN_WORD_CHUNKS, LANES), jnp.uint32),
                   jax.ShapeDtypeStruct((T, LANES), jnp.int32),
                   jax.ShapeDtypeStruct((T, LANES), F32)],
        compiler_params=_params(("parallel",)),
        name="out_proj_router",
    )(merged, w_out.astype(BF16), x, g2.reshape(1, D), w_router, b_router)


def _expert_kernel(be_ref, eo_ref, ue_ref, nr_ref, bo_ref, rt_ref, nu_ref,
                   hn_hbm, wg_hbm, wu_hbm, wd_hbm, y_ref, xbuf, wgf, wuf, wdf, sem, wsem):
    b = pl.program_id(0)
    n_used = nu_ref[0]
    n_exp = nu_ref[1]
    used = b < n_used
    slot = b % ROW_SLOTS

    def weight_copies(ordinal, slot_):
        e = ue_ref[ordinal]
        return (pltpu.make_async_copy(wg_hbm.at[e], wgf.at[slot_], wsem.at[slot_]),
                pltpu.make_async_copy(wu_hbm.at[e], wuf.at[slot_], wsem.at[slot_]),
                pltpu.make_async_copy(wd_hbm.at[e], wdf.at[slot_], wsem.at[slot_]))

    def start_weights(ordinal):
        for c in weight_copies(ordinal, ordinal % WEIGHT_SLOTS):
            c.start()

    def issue_trips(blk):
        return (nr_ref[blk] + (ROWS_PER_ISSUE - 1)) // ROWS_PER_ISSUE

    def gather_rows(blk, slot_):
        first_row = bo_ref[blk]

        def issue(g, c):
            for j in range(ROWS_PER_ISSUE):
                r = g * ROWS_PER_ISSUE + j
                src = pl.multiple_of(rt_ref[first_row + r] * HN_WORD_CHUNKS, HN_WORD_CHUNKS)
                dst = pl.multiple_of(r * HN_WORD_CHUNKS, HN_WORD_CHUNKS)
                pltpu.make_async_copy(hn_hbm.at[pl.ds(src, HN_WORD_CHUNKS)],
                                      xbuf.at[slot_, pl.ds(dst, HN_WORD_CHUNKS)],
                                      sem.at[slot_]).start()
            return c

        lax.fori_loop(0, issue_trips(blk), issue, 0)

    @pl.when(b == 0)
    def _():
        start_weights(0)

        @pl.when(n_exp > 1)
        def _():
            start_weights(1)

        xbuf[...] = jnp.zeros(xbuf.shape, xbuf.dtype)
        gather_rows(0, 0)
        for a in range(1, ROW_AHEAD):
            @pl.when(n_used > a)
            def _(a=a):
                gather_rows(a, a)

    @pl.when(b + ROW_AHEAD < n_used)
    def _():
        gather_rows(b + ROW_AHEAD, (b + ROW_AHEAD) % ROW_SLOTS)

    @pl.when(used)
    def _():
        first = jnp.logical_or(b == 0, be_ref[b] != be_ref[jnp.maximum(b - 1, 0)])

        ordinal = eo_ref[b]
        ws = ordinal % WEIGHT_SLOTS

        @pl.when(first)
        def _():
            for c in weight_copies(ordinal, ws):
                c.wait()

            @pl.when(ordinal + 2 < n_exp)
            def _():
                start_weights(ordinal + 2)

        words = issue_trips(b) * (ROWS_PER_ISSUE * HN_WORD_CHUNKS)
        pltpu.make_async_copy(hn_hbm.at[pl.ds(0, words)], xbuf.at[slot, pl.ds(0, words)],
                              sem.at[slot]).wait()

        halves = ([], [])
        for c in range(HN_WORD_CHUNKS):
            hi, lo = _unpack_bf16_pair(xbuf[slot, pl.ds(c, MOE_BLOCK, stride=HN_WORD_CHUNKS), :])
            halves[0].append(hi.astype(BF16))
            halves[1].append(lo.astype(BF16))
        x = jnp.concatenate(halves[0] + halves[1], axis=1)
        hg = jnp.dot(x, wgf[ws].astype(BF16), preferred_element_type=F32)
        hu = jnp.dot(x, wuf[ws].astype(BF16), preferred_element_type=F32)
        hdn = (jax.nn.silu(hg) * hu).astype(BF16)
        y = jnp.dot(hdn, wdf[ws].astype(BF16), preferred_element_type=F32)
        for c, word in enumerate(_pack_bf16_pairs(y)):
            y_ref[pl.ds(c, MOE_BLOCK, stride=HN_WORD_CHUNKS), :] = word

    @pl.when(jnp.logical_not(used))
    def _():
        y_ref[...] = jnp.zeros(y_ref.shape, y_ref.dtype)


def _experts(hn, w_gate, w_up, w_down, block_e, block_ord, used_experts, block_rows,
             block_offset, row_tok, n_used):
    D = w_gate.shape[1]
    n_blocks = block_e.shape[0]
    n_rows = n_blocks * MOE_BLOCK
    hbm = pl.BlockSpec(memory_space=pl.ANY)
    grid_spec = pltpu.PrefetchScalarGridSpec(
        num_scalar_prefetch=7,
        grid=(n_blocks,),
        in_specs=[hbm, hbm, hbm, hbm],
        out_specs=pl.BlockSpec((MOE_BLOCK * HN_WORD_CHUNKS, LANES), lambda b, *_: (b, 0)),
        scratch_shapes=[pltpu.VMEM((ROW_SLOTS, MOE_BLOCK * HN_WORD_CHUNKS, LANES), jnp.uint32),
                        pltpu.VMEM((WEIGHT_SLOTS, D, EXPERT_FF), F32),
                        pltpu.VMEM((WEIGHT_SLOTS, D, EXPERT_FF), F32),
                        pltpu.VMEM((WEIGHT_SLOTS, EXPERT_FF, D), F32),
                        pltpu.SemaphoreType.DMA((ROW_SLOTS,)),
                        pltpu.SemaphoreType.DMA((WEIGHT_SLOTS,))],
    )
    return pl.pallas_call(
        _expert_kernel,
        grid_spec=grid_spec,
        out_shape=jax.ShapeDtypeStruct((n_rows * HN_WORD_CHUNKS, LANES), jnp.uint32),
        compiler_params=_params(("arbitrary",)),
        name="experts",
    )(block_e, block_ord, used_experts, block_rows, block_offset, row_tok, n_used,
      hn, w_gate, w_up, w_down)


def _combine_kernel(*refs, tc):
    dest_refs = refs[:COMBINE_SLOTS]
    y_hbm, h_ref, wt_ref, o_ref, ybuf, sem = refs[COMBINE_SLOTS:]
    i = pl.program_id(0)
    slot = i % COMBINE_SLOTS

    def gather_rows(dest_ref, slot_):
        def issue(r, c):
            dst = pl.multiple_of(r * HN_WORD_CHUNKS, HN_WORD_CHUNKS)
            for k in range(TOP_K):
                src = pl.multiple_of(dest_ref[0, 0, r * TOP_K + k] * HN_WORD_CHUNKS, HN_WORD_CHUNKS)
                pltpu.make_async_copy(y_hbm.at[pl.ds(src, HN_WORD_CHUNKS)],
                                      ybuf.at[slot_, k, pl.ds(dst, HN_WORD_CHUNKS)],
                                      sem.at[slot_]).start(priority=k % 2)
            return c

        lax.fori_loop(0, tc, issue, 0, unroll=4)

    @pl.when(i == 0)
    def _():
        for a in range(COMBINE_AHEAD):
            gather_rows(dest_refs[a], a)

    @pl.when(i + COMBINE_AHEAD < pl.num_programs(0))
    def _():
        gather_rows(dest_refs[COMBINE_AHEAD], (i + COMBINE_AHEAD) % COMBINE_SLOTS)

    for k in range(TOP_K):
        pltpu.make_async_copy(y_hbm.at[pl.ds(0, tc * HN_WORD_CHUNKS)], ybuf.at[slot, k],
                              sem.at[slot]).wait()
    wt = wt_ref[...]
    w1, w2 = wt[:, 0:1], wt[:, 1:2]
    for c in range(HN_WORD_CHUNKS):
        rows = pl.ds(c, tc, stride=HN_WORD_CHUNKS)
        y1, y2 = (_unpack_bf16_pair(ybuf[slot, k, rows, :]) for k in range(TOP_K))
        for half, col in enumerate((c * LANES, (c + HN_WORD_CHUNKS) * LANES)):
            o_ref[:, col:col + LANES] = h_ref[:, col:col + LANES] + (w1 * y1[half] + w2 * y2[half])


def _combine(yrows, h, wts, dest, tc=256):
    T, D = h.shape
    steps = T // tc
    row = lambda i: (i, 0)
    dest3 = dest.reshape(steps, 1, tc * TOP_K)
    dest_block = lambda im: pl.BlockSpec((1, 1, tc * TOP_K), im, memory_space=pltpu.SMEM)
    return pl.pallas_call(
        functools.partial(_combine_kernel, tc=tc),
        grid=(steps,),
        in_specs=[dest_block(lambda i, a=a: (jnp.minimum(i + a, steps - 1), 0, 0))
                  for a in range(COMBINE_SLOTS)]
                 + [pl.BlockSpec(memory_space=pl.ANY),
                    pl.BlockSpec((tc, D), row),
                    pl.BlockSpec((tc, LANES), row)],
        out_specs=pl.BlockSpec((tc, D), row),
        out_shape=jax.ShapeDtypeStruct((T, D), F32),
        scratch_shapes=[pltpu.VMEM((COMBINE_SLOTS, TOP_K, tc * HN_WORD_CHUNKS, LANES), jnp.uint32),
                        pltpu.SemaphoreType.DMA((COMBINE_SLOTS,))],
        compiler_params=_params(("arbitrary",)),
        name="combine",
    )(*([dest3] * COMBINE_SLOTS), yrows, h, wts)


def _lane_cumsum(x):
    lane = lax.broadcasted_iota(jnp.int32, x.shape, 1)
    s = 1
    while s < LANES:
        x = x + jnp.where(lane >= s, pltpu.roll(x, s, 1), 0)
        s *= 2
    return x


def _dispatch_kernel(idx_ref, dest_ref, meta_ref, run_ref, prefix_ref, start_ref, *, tb):
    p = pl.program_id(0)
    i = pl.program_id(1)
    idx = idx_ref[...]
    lane = lax.broadcasted_iota(jnp.int32, idx.shape, 1)
    e1 = idx[:, 0:1]
    e2 = idx[:, 1:2]
    onehot = jnp.where(jnp.logical_or(lane == e1, lane == e2), 1.0, 0.0)

    @pl.when(jnp.logical_and(p == 0, i == 0))
    def _():
        run_ref[...] = jnp.zeros(run_ref.shape, F32)

    @pl.when(p == 0)
    def _():
        prefix_ref[i] = run_ref[...]
        run_ref[...] = run_ref[...] + jnp.sum(onehot, axis=0, keepdims=True)

    @pl.when(jnp.logical_and(p == 1, i == 0))
    def _():
        counts = run_ref[...].astype(jnp.int32)
        nblk = (counts + (MOE_BLOCK - 1)) // MOE_BLOCK
        end_blk = _lane_cumsum(nblk)
        start_ref[...] = ((end_blk - nblk) * MOE_BLOCK).astype(F32)
        has = jnp.where(counts > 0, 1, 0)
        ordinal = _lane_cumsum(has) - 1
        first_row = (_lane_cumsum(counts) - counts)[0:1]
        start_blk, counts = (end_blk - nblk)[0:1], counts[0:1]
        end_blk, has, ordinal = end_blk[0:1], has[0:1], ordinal[0:1]
        rows = lax.broadcasted_iota(jnp.int32, (META_ROWS, LANES), 0)
        lanes = lax.broadcasted_iota(jnp.int32, (META_ROWS, LANES), 1)
        is_e = lanes < N_EXPERTS
        rsum = lambda v: jnp.sum(v, axis=-1, keepdims=True)
        be = rsum(jnp.where(jnp.logical_and(is_e, end_blk <= rows), 1, 0))
        be = jnp.minimum(be, N_EXPERTS - 1)
        eo = rsum(jnp.where(lanes == be, ordinal, 0))
        ue = rsum(jnp.where(jnp.logical_and(has > 0, ordinal == rows), lanes, 0))
        n_blk = rsum(jnp.where(lanes == N_EXPERTS - 1, end_blk, 0))
        n_exp = rsum(jnp.where(is_e, has, 0))
        own = lanes == be
        done = MOE_BLOCK * (rows - rsum(jnp.where(own, start_blk, 0)))
        n_valid = jnp.clip(rsum(jnp.where(own, counts, 0)) - done, 0, MOE_BLOCK)
        offset = rsum(jnp.where(own, first_row, 0)) + done
        columns = (be, eo, ue, n_blk, n_exp, n_valid, offset)
        meta = jnp.zeros((META_ROWS, LANES), jnp.int32)
        for col, val in enumerate(columns):
            meta = jnp.where(lanes == col, val, meta)
        meta_ref[...] = meta

    @pl.when(p == 1)
    def _():
        r = lax.broadcasted_iota(jnp.int32, (tb, tb), 0)
        c = lax.broadcasted_iota(jnp.int32, (tb, tb), 1)
        earlier = jnp.where(c < r, 1.0, 0.0).astype(BF16)
        rank = jnp.dot(earlier, onehot.astype(BF16), preferred_element_type=F32)
        rank = rank + prefix_ref[i][0:1] + start_ref[0:1]
        d1 = jnp.sum(jnp.where(lane == e1, rank, 0.0), axis=-1, keepdims=True)
        d2 = jnp.sum(jnp.where(lane == e2, rank, 0.0), axis=-1, keepdims=True)
        dest_ref[...] = jnp.where(lane == 0, d1, d2).astype(jnp.int32)


def _dispatch(idx, tb=512):
    T = idx.shape[0]
    n_rows = T * TOP_K + N_EXPERTS * MOE_BLOCK
    n_blocks = n_rows // MOE_BLOCK
    assert n_blocks <= META_ROWS
    dest2, meta = pl.pallas_call(
        functools.partial(_dispatch_kernel, tb=tb),
        grid=(2, T // tb),
        in_specs=[pl.BlockSpec((tb, LANES), lambda p, i: (i, 0))],
        out_specs=[pl.BlockSpec((tb, LANES), lambda p, i: (i * p, 0)),
                   pl.BlockSpec((META_ROWS, LANES), lambda p, i: (0, 0))],
        out_shape=[jax.ShapeDtypeStruct((T, LANES), jnp.int32),
                   jax.ShapeDtypeStruct((META_ROWS, LANES), jnp.int32)],
        scratch_shapes=[pltpu.VMEM((SUBLANES, LANES), F32),
                        pltpu.VMEM((T // tb, SUBLANES, LANES), F32),
                        pltpu.VMEM((SUBLANES, LANES), F32)],
        compiler_params=_params(("arbitrary", "arbitrary")),
        name="dispatch",
    )(idx)
    dest = dest2[:, :TOP_K].reshape(T * TOP_K)
    row_tok = (jnp.argsort(dest) // TOP_K).astype(jnp.int32)
    row_tok = jnp.concatenate([row_tok, jnp.zeros((ROWS_PER_ISSUE,), jnp.int32)])
    block_e = meta[:n_blocks, 0]
    block_ord = meta[:n_blocks, 1]
    used_experts = meta[:N_EXPERTS, 2]
    n_used = meta[0, 3:5]
    block_rows = meta[:n_blocks, 5]
    block_offset = meta[:n_blocks, 6]
    return block_e, block_ord, used_experts, block_rows, block_offset, row_tok, n_used, dest


def kernel(x, positions, norm1_g, w_in, q_norm_g, k_norm_g, sink_logits, sg_ln_g, sg_ln_b, sg_w, sg_b, w_branch_att, w_branch_sg, w_out, norm2_g, w_group_router, b_group_router, w_expert_router, b_expert_router, w_gate, w_up, w_down):
    B, S, D = x.shape
    T = B * S
    assert D == D_MODEL and S % (2 * ATT_BLOCK) == 0 and S % SG_CHUNK == 0
    assert x.dtype == F32 and positions.shape == (B, S)
    assert w_in.shape[1:] == (D_MODEL, IN_COLS) and w_out.shape[1:] == (D_MODEL, D_MODEL)
    assert w_gate.shape[1:] == (N_EXPERTS, D_MODEL, EXPERT_FF) and w_down.shape[1:] == (
        N_EXPERTS, EXPERT_FF, D_MODEL)
    h = x.reshape(T, D)
    pos = positions.reshape(T)
    for l in range(norm1_g.shape[0]):
        xn = _rmsnorm(h, norm1_g[l])
        proj = _in_proj(xn, w_in[l])
        q, k, v = _qkv_prep(proj, pos, q_norm_g[l], k_norm_g[l])
        att = _attention(q, k, v, sink_logits[l], B)
        sgo = _spatial_gating(proj, sg_ln_g[l], sg_ln_b[l], sg_w[l], sg_b[l])
        merged = _merge(att, sgo, proj, w_branch_att[l], w_branch_sg[l])
        pad = LANES - N_GROUPS - N_EXPERTS
        w_router = jnp.concatenate(
            [w_group_router[l], w_expert_router[l], jnp.zeros((D, pad), F32)], axis=1).astype(BF16)
        b_router = jnp.concatenate(
            [b_group_router[l], b_expert_router[l], jnp.zeros((pad,), F32)]).reshape(1, LANES)
        h, hn, idx, wts = _out_proj(merged, w_out[l], h, norm2_g[l], w_router, b_router)
        (block_e, block_ord, used_experts, block_rows, block_offset, row_tok, n_used,
         dest) = _dispatch(idx)
        yrows = _experts(hn, w_gate[l], w_up[l], w_down[l], block_e, block_ord, used_experts,
                         block_rows, block_offset, row_tok, n_used)
        h = _combine(yrows, h, wts, dest)
    return h.reshape(B, S, D)
```

```python
import functools

import jax
import jax.numpy as jnp
from jax import lax
from jax.experimental import pallas as pl
from jax.experimental.pallas import tpu as pltpu

F32 = jnp.float32
BF16 = jnp.bfloat16

D_MODEL = 2048
HEAD_DIM = 64
ATT_WIDTH = D_MODEL // 2
ATT_HEADS = ATT_WIDTH // HEAD_DIM
ATT_KV_HEADS = ATT_HEADS // 4
Q_PER_KV = ATT_HEADS // ATT_KV_HEADS
KV_WIDTH = ATT_KV_HEADS * HEAD_DIM
WINDOW = 128
ATT_BLOCK = 128
ROPE_DIM = HEAD_DIM // 4
ROPE_HALF = ROPE_DIM // 2
ROPE_THETA = 500000.0
SG_WIDTH = D_MODEL // 2
SG_GROUP_DIM = 128
SG_GROUPS = SG_WIDTH // SG_GROUP_DIM
SG_CHUNK = 128
OFF_Q = 0
OFF_K = OFF_Q + ATT_WIDTH
OFF_V = OFF_K + KV_WIDTH
OFF_U = OFF_V + KV_WIDTH
OFF_S = OFF_U + SG_WIDTH
OFF_G = OFF_S + SG_WIDTH
IN_COLS = OFF_G + 2 * D_MODEL
N_GROUPS = 8
EXPERTS_PER_GROUP = 8
N_EXPERTS = N_GROUPS * EXPERTS_PER_GROUP
TOP_K = 2
EXPERT_FF = D_MODEL // 4
MOE_BLOCK = 128
EPS = 1e-6
NEG_INF = -1e30

LANES = 128
SUBLANES = 8
BF16_BITS = 16
HN_WORD_CHUNKS = D_MODEL // LANES // 2
ROWS_PER_ISSUE = 8
ROW_AHEAD = 4
ROW_SLOTS = ROW_AHEAD + 1
SG_CHUNKS_PER_TRIP = 2
GATE_BLOCK = 512
MERGE_GATE_BLOCKS = 2
MERGE_ROW_PARTS = 2
OUT_ROW_PARTS = 4
WEIGHT_SLOTS = 3
COMBINE_AHEAD = 1
COMBINE_SLOTS = COMBINE_AHEAD + 1
META_ROWS = 256
VMEM_LIMIT = 56 * 1024 * 1024

assert WINDOW == ATT_BLOCK


def _pack_bf16_pairs(x):
    bits = lax.bitcast_convert_type(x.astype(BF16).astype(F32), jnp.uint32)
    words = []
    for c in range(HN_WORD_CHUNKS):
        hi = bits[:, c * LANES:(c + 1) * LANES]
        lo = bits[:, (c + HN_WORD_CHUNKS) * LANES:(c + HN_WORD_CHUNKS + 1) * LANES]
        words.append(hi | (lo >> BF16_BITS))
    return words


def _unpack_bf16_pair(word):
    high_mask = jnp.uint32(((1 << BF16_BITS) - 1) << BF16_BITS)
    return (lax.bitcast_convert_type(word & high_mask, F32),
            lax.bitcast_convert_type(word << BF16_BITS, F32))


def _params(sem, vmem=VMEM_LIMIT):
    return pltpu.CompilerParams(dimension_semantics=sem, vmem_limit_bytes=vmem)


def _rmsnorm_kernel(x_ref, g_ref, o_ref):
    x = x_ref[...]
    r = lax.rsqrt(jnp.mean(x * x, axis=-1, keepdims=True) + EPS)
    o_ref[...] = ((x * r) * g_ref[...]).astype(o_ref.dtype)


def _rmsnorm(x, g, tm=1024):
    T, D = x.shape
    return pl.pallas_call(
        _rmsnorm_kernel,
        grid=(T // tm,),
        in_specs=[pl.BlockSpec((tm, D), lambda i: (i, 0)),
                  pl.BlockSpec((1, D), lambda i: (0, 0))],
        out_specs=pl.BlockSpec((tm, D), lambda i: (i, 0)),
        out_shape=jax.ShapeDtypeStruct((T, D), BF16),
        compiler_params=_params(("parallel",)),
        name="norm1",
    )(x, g.reshape(1, D))


def _proj_kernel(x_ref, w_ref, o_ref, wbf_ref):
    @pl.when(pl.program_id(1) == 0)
    def _():
        wbf_ref[...] = w_ref[...].astype(BF16)

    o_ref[...] = jnp.dot(x_ref[...], wbf_ref[...], preferred_element_type=F32)


def _in_proj(xn, w, tm=1024, tn=1280):
    T, D = xn.shape
    N = w.shape[1]
    return pl.pallas_call(
        _proj_kernel,
        grid=(N // tn, T // tm),
        in_specs=[pl.BlockSpec((tm, D), lambda j, i: (i, 0)),
                  pl.BlockSpec((D, tn), lambda j, i: (0, j))],
        out_specs=pl.BlockSpec((tm, tn), lambda j, i: (i, j)),
        out_shape=jax.ShapeDtypeStruct((T, N), F32),
        scratch_shapes=[pltpu.VMEM((D, tn), BF16)],
        compiler_params=_params(("arbitrary", "arbitrary")),
        name="in_proj",
    )(xn, w)


def _rope_table_kernel(pos_ref, invf_ref, cos_ref, sin_ref):
    ang = pos_ref[...].astype(F32) * invf_ref[...]
    cos_ref[...] = jnp.cos(ang)
    sin_ref[...] = jnp.sin(ang)


def _rope_tables(positions):
    T = positions.shape[0]
    rows = T * ROPE_HALF // LANES
    inv = ROPE_THETA ** (-jnp.arange(0, ROPE_DIM, 2, dtype=F32) / ROPE_DIM)
    invf = jnp.tile(inv, LANES // ROPE_HALF).reshape(1, LANES)
    pos = jnp.repeat(positions, ROPE_HALF).reshape(rows, LANES)
    whole = lambda: (0, 0)
    cos, sin = pl.pallas_call(
        _rope_table_kernel,
        in_specs=[pl.BlockSpec((rows, LANES), whole), pl.BlockSpec((1, LANES), whole)],
        out_specs=[pl.BlockSpec((rows, LANES), whole), pl.BlockSpec((rows, LANES), whole)],
        out_shape=[jax.ShapeDtypeStruct((rows, LANES), F32)] * 2,
        name="rope_tables",
    )(pos, invf)
    return cos.reshape(T, ROPE_HALF), sin.reshape(T, ROPE_HALF)


def _qkv_prep_kernel(p_ref, cos_ref, sin_ref, cpat_ref, spat_ref, gq_ref, gk_ref, seg_ref,
                     q_ref, k_ref, v_ref):
    lane = lax.broadcasted_iota(jnp.int32, (cos_ref.shape[0], LANES), 1)
    in_head = lane % HEAD_DIM

    def spread(t_ref, pattern_ref):
        t = t_ref[...]
        hi = t.astype(BF16)
        r1 = t - hi.astype(F32)
        mid = r1.astype(BF16)
        lo = (r1 - mid.astype(F32)).astype(BF16)
        return sum(jnp.dot(term, pattern_ref[...], preferred_element_type=F32)
                   for term in (hi, mid, lo))

    cos = jnp.where(in_head < ROPE_DIM, spread(cos_ref, cpat_ref), 1.0)
    sin = spread(sin_ref, spat_ref)
    first_half = in_head < ROPE_HALF

    def norm_rope(x, g):
        x2 = x * x
        x2_hi = x2.astype(BF16)
        x2_lo = (x2 - x2_hi.astype(F32)).astype(BF16)
        both = jnp.dot(jnp.concatenate([x2_hi, x2_lo], axis=0), seg_ref[...],
                       preferred_element_type=F32)
        ssq = both[:x.shape[0]] + both[x.shape[0]:]
        xn = (x * lax.rsqrt(ssq * (1.0 / HEAD_DIM) + EPS)) * g
        partner = jnp.where(first_half,
                            pltpu.roll(xn, LANES - ROPE_HALF, 1),
                            pltpu.roll(xn, ROPE_HALF, 1))
        return xn * cos + partner * sin

    for c in range(ATT_WIDTH // LANES):
        x = p_ref[:, OFF_Q + c * LANES:OFF_Q + (c + 1) * LANES]
        q_ref[:, c * LANES:(c + 1) * LANES] = (
            norm_rope(x, gq_ref[...]) * (HEAD_DIM ** -0.5)).astype(q_ref.dtype)
    for c in range(KV_WIDTH // LANES):
        x = p_ref[:, OFF_K + c * LANES:OFF_K + (c + 1) * LANES]
        k_ref[:, c * LANES:(c + 1) * LANES] = norm_rope(x, gk_ref[...]).astype(k_ref.dtype)
    v_ref[...] = p_ref[:, OFF_V:OFF_V + KV_WIDTH].astype(v_ref.dtype)


def _qkv_prep(proj, positions, q_g, k_g, tq=512):
    T = proj.shape[0]
    width = OFF_U
    cos8, sin8 = _rope_tables(positions)
    in_head = jnp.arange(LANES) % HEAD_DIM
    picks = (in_head[None, :] % ROPE_HALF == jnp.arange(ROPE_HALF)[:, None]) & (in_head < ROPE_DIM)
    cos_pattern = picks.astype(BF16)
    sin_pattern = (picks * jnp.where(in_head < ROPE_HALF, -1.0, 1.0)).astype(BF16)
    head_of_lane = jnp.arange(LANES) // HEAD_DIM
    same_head = (head_of_lane[:, None] == head_of_lane[None, :]).astype(BF16)
    gq = jnp.tile(q_g, LANES // HEAD_DIM).reshape(1, LANES)
    gk = jnp.tile(k_g, LANES // HEAD_DIM).reshape(1, LANES)
    row = lambda i: (i, 0)
    const = lambda i: (0, 0)
    return pl.pallas_call(
        _qkv_prep_kernel,
        grid=(T // tq,),
        in_specs=[pl.BlockSpec((tq, width), row),
                  pl.BlockSpec((tq, ROPE_HALF), row),
                  pl.BlockSpec((tq, ROPE_HALF), row),
                  pl.BlockSpec((ROPE_HALF, LANES), const),
                  pl.BlockSpec((ROPE_HALF, LANES), const),
                  pl.BlockSpec((1, LANES), const),
                  pl.BlockSpec((1, LANES), const),
                  pl.BlockSpec((LANES, LANES), const)],
        out_specs=[pl.BlockSpec((tq, ATT_WIDTH), row),
                   pl.BlockSpec((tq, KV_WIDTH), row),
                   pl.BlockSpec((tq, KV_WIDTH), row)],
        out_shape=[jax.ShapeDtypeStruct((T, ATT_WIDTH), BF16),
                   jax.ShapeDtypeStruct((T, KV_WIDTH), BF16),
                   jax.ShapeDtypeStruct((T, KV_WIDTH), BF16)],
        compiler_params=_params(("parallel",)),
        name="qkv_prep",
    )(proj, cos8, sin8, cos_pattern, sin_pattern, gq, gk, same_head)


def _attn_kernel(sink_ref, q_ref, kp_ref, kc_ref, kn_ref, vp_ref, vc_ref, vn_ref, o_ref, *, nb):
    n = pl.program_id(1)
    rows = ATT_BLOCK
    qi = lax.broadcasted_iota(jnp.int32, (rows, ATT_BLOCK), 0) % ATT_BLOCK
    kj = lax.broadcasted_iota(jnp.int32, (rows, ATT_BLOCK), 1)
    lo_prev = jnp.where(n > 0, 0, ATT_BLOCK)
    hi_next = jnp.where(n < nb - 1, 0, -ATT_BLOCK)
    cap_prev = jnp.where(kj - qi >= lo_prev, jnp.inf, NEG_INF)
    cap_next = jnp.where(kj - qi <= hi_next, jnp.inf, NEG_INF)
    cap = jnp.concatenate([cap_prev, jnp.full((rows, ATT_BLOCK), jnp.inf, F32), cap_next], axis=1)

    kvhs = range(ATT_KV_HEADS)
    scores = []
    for kvh in kvhs:
        cols = slice(kvh * HEAD_DIM, (kvh + 1) * HEAD_DIM)
        k = jnp.concatenate([kp_ref[:, cols], kc_ref[:, cols], kn_ref[:, cols]], axis=0)
        q = jnp.concatenate(
            [q_ref[:, (kvh * Q_PER_KV + g) * HEAD_DIM:(kvh * Q_PER_KV + g + 1) * HEAD_DIM]
             for g in range(Q_PER_KV)], axis=0)
        scores.append(lax.dot_general(q, k, (((1,), (1,)), ((), ())),
                                      preferred_element_type=F32))
    probs = []
    for kvh in kvhs:
        strips = []
        for g in range(Q_PER_KV):
            sink = sink_ref[kvh * Q_PER_KV + g]
            sg = jnp.minimum(scores[kvh][g * ATT_BLOCK:(g + 1) * ATT_BLOCK], cap)
            m = jnp.maximum(jnp.max(sg, axis=-1, keepdims=True), sink)
            e = jnp.exp(sg - m)
            denom = jnp.sum(e, axis=-1, keepdims=True) + jnp.exp(sink - m)
            strips.append((e / denom).astype(BF16))
        probs.append(jnp.concatenate(strips, axis=0))
    for kvh in kvhs:
        cols = slice(kvh * HEAD_DIM, (kvh + 1) * HEAD_DIM)
        v = jnp.concatenate([vp_ref[:, cols], vc_ref[:, cols], vn_ref[:, cols]], axis=0)
        o = jnp.dot(probs[kvh], v, preferred_element_type=F32)
        for g in range(Q_PER_KV):
            h = kvh * Q_PER_KV + g
            o_ref[:, h * HEAD_DIM:(h + 1) * HEAD_DIM] = (
                o[g * ATT_BLOCK:(g + 1) * ATT_BLOCK].astype(o_ref.dtype))


def _attention(q, k, v, sink, batch):
    T = q.shape[0]
    nb = T // batch // ATT_BLOCK
    cur = lambda b, n: (b * nb + n, 0)
    prev = lambda b, n: (b * nb + jnp.maximum(n - 1, 0), 0)
    nxt = lambda b, n: (b * nb + jnp.minimum(n + 1, nb - 1), 0)
    kv = lambda im: pl.BlockSpec((ATT_BLOCK, KV_WIDTH), im)
    return pl.pallas_call(
        functools.partial(_attn_kernel, nb=nb),
        grid=(batch, nb),
        in_specs=[pl.BlockSpec(memory_space=pltpu.SMEM),
                  pl.BlockSpec((ATT_BLOCK, ATT_WIDTH), cur),
                  kv(prev), kv(cur), kv(nxt), kv(prev), kv(cur), kv(nxt)],
        out_specs=pl.BlockSpec((ATT_BLOCK, ATT_WIDTH), cur),
        out_shape=jax.ShapeDtypeStruct((T, ATT_WIDTH), BF16),
        compiler_params=_params(("parallel", "parallel")),
        name="window_attn",
    )(sink, q, k, k, k, v, v, v)


def _sg_kernel(u_ref, s_ref, lng_ref, lnb_ref, w_ref, b_ref, o_ref):
    groups = w_ref.shape[0]

    def some_chunks(ci, carry):
        tiles = []
        for j in range(SG_CHUNKS_PER_TRIP):
            start = pl.multiple_of((ci * SG_CHUNKS_PER_TRIP + j) * SG_CHUNK, SG_CHUNK)
            for gi in range(groups):
                tiles.append((pl.ds(start, SG_CHUNK),
                              slice(gi * SG_GROUP_DIM, (gi + 1) * SG_GROUP_DIM), gi))
        ts = range(len(tiles))
        s = [jax.nn.gelu(s_ref[r, c]) for r, c, _ in tiles]
        sc = [s[t] - jnp.mean(s[t], axis=-1, keepdims=True) for t in ts]
        var = [jnp.mean(sc[t] * sc[t], axis=-1, keepdims=True) for t in ts]
        sn = [(sc[t] * lax.rsqrt(var[t] + EPS)) * lng_ref[0, gi:gi + 1, :]
              + lnb_ref[0, gi:gi + 1, :] for t, (_, _, gi) in enumerate(tiles)]
        mixed = [jnp.dot(w_ref[gi], sn[t].astype(BF16), preferred_element_type=F32)
                 + b_ref[0, :, gi:gi + 1] for t, (_, _, gi) in enumerate(tiles)]
        for t, (r, c, _) in enumerate(tiles):
            o_ref[r, c] = (jax.nn.gelu(u_ref[r, c]) * mixed[t]).astype(o_ref.dtype)
        return carry

    lax.fori_loop(0, u_ref.shape[0] // (SG_CHUNK * SG_CHUNKS_PER_TRIP), some_chunks, 0)


def _spatial_gating(proj, ln_g, ln_b, w_s, b_s, halves=2, chunks=8):
    T = proj.shape[0]
    half_w = SG_WIDTH // halves
    gph = SG_GROUPS // halves
    u0, s0 = OFF_U // half_w, OFF_S // half_w
    tq = chunks * SG_CHUNK
    return pl.pallas_call(
        _sg_kernel,
        grid=(T // tq, halves),
        in_specs=[pl.BlockSpec((tq, half_w), lambda i, j: (i, u0 + j)),
                  pl.BlockSpec((tq, half_w), lambda i, j: (i, s0 + j)),
                  pl.BlockSpec((1, gph, SG_GROUP_DIM), lambda i, j: (j, 0, 0)),
                  pl.BlockSpec((1, gph, SG_GROUP_DIM), lambda i, j: (j, 0, 0)),
                  pl.BlockSpec((gph, SG_CHUNK, SG_CHUNK), lambda i, j: (j, 0, 0)),
                  pl.BlockSpec((1, SG_CHUNK, gph), lambda i, j: (j, 0, 0))],
        out_specs=pl.BlockSpec((tq, half_w), lambda i, j: (i, j)),
        out_shape=jax.ShapeDtypeStruct((T, SG_WIDTH), BF16),
        compiler_params=_params(("parallel", "parallel")),
        name="spatial_gating",
    )(proj, proj,
      ln_g.reshape(halves, gph, SG_GROUP_DIM), ln_b.reshape(halves, gph, SG_GROUP_DIM),
      w_s.astype(BF16),
      b_s.reshape(halves, gph, SG_CHUNK).transpose(0, 2, 1))


def _merge_kernel(att_ref, sgo_ref, *refs):
    gate_refs = refs[:2 * MERGE_GATE_BLOCKS]
    wa_ref, wb_ref, o_ref, wa_bf, wb_bf = refs[2 * MERGE_GATE_BLOCKS:]

    @pl.when(pl.program_id(1) == 0)
    def _():
        wa_bf[...] = wa_ref[...].astype(BF16)
        wb_bf[...] = wb_ref[...].astype(BF16)

    part_rows = att_ref.shape[0] // MERGE_ROW_PARTS
    parts = [slice(p * part_rows, (p + 1) * part_rows) for p in range(MERGE_ROW_PARTS)]
    a = [jnp.dot(att_ref[r, :], wa_bf[...], preferred_element_type=F32) for r in parts]
    b = [jnp.dot(sgo_ref[r, :], wb_bf[...], preferred_element_type=F32) for r in parts]
    for p, r in enumerate(parts):
        for g in range(MERGE_GATE_BLOCKS):
            cols = slice(g * GATE_BLOCK, (g + 1) * GATE_BLOCK)
            ga_ref, gb_ref = gate_refs[g], gate_refs[MERGE_GATE_BLOCKS + g]
            m = (jax.nn.sigmoid(ga_ref[r, :]) * a[p][:, cols]
                 + jax.nn.sigmoid(gb_ref[r, :]) * b[p][:, cols])
            o_ref[r, cols] = m.astype(o_ref.dtype)


def _merge(att, sgo, proj, w_a, w_b, tm=512):
    T = att.shape[0]
    tn = MERGE_GATE_BLOCKS * GATE_BLOCK
    ga0 = OFF_G // GATE_BLOCK
    gb0 = (OFF_G + D_MODEL) // GATE_BLOCK
    gate = lambda first, g: pl.BlockSpec(
        (tm, GATE_BLOCK), lambda j, i: (i, first + j * MERGE_GATE_BLOCKS + g))
    gates = ([gate(ga0, g) for g in range(MERGE_GATE_BLOCKS)]
             + [gate(gb0, g) for g in range(MERGE_GATE_BLOCKS)])
    return pl.pallas_call(
        _merge_kernel,
        grid=(D_MODEL // tn, T // tm),
        in_specs=[pl.BlockSpec((tm, ATT_WIDTH), lambda j, i: (i, 0)),
                  pl.BlockSpec((tm, SG_WIDTH), lambda j, i: (i, 0))]
                 + gates
                 + [pl.BlockSpec((ATT_WIDTH, tn), lambda j, i: (0, j)),
                    pl.BlockSpec((SG_WIDTH, tn), lambda j, i: (0, j))],
        out_specs=pl.BlockSpec((tm, tn), lambda j, i: (i, j)),
        out_shape=jax.ShapeDtypeStruct((T, D_MODEL), BF16),
        scratch_shapes=[pltpu.VMEM((ATT_WIDTH, tn), BF16),
                        pltpu.VMEM((SG_WIDTH, tn), BF16)],
        compiler_params=_params(("arbitrary", "arbitrary")),
        name="merge",
    )(att, sgo, *([proj] * (2 * MERGE_GATE_BLOCKS)), w_a, w_b)


def _route(logits):
    lane = lax.broadcasted_iota(jnp.int32, logits.shape, 1)
    lane_f = lane.astype(F32)
    is_g = lane < N_GROUPS
    gl = jnp.where(is_g, logits, -jnp.inf)
    gmax = jnp.max(gl, axis=-1, keepdims=True)
    grp = jnp.min(jnp.where(gl == gmax, lane_f, float(LANES)), axis=-1, keepdims=True)
    gsum = jnp.sum(jnp.where(is_g, jnp.exp(logits - gmax), 0.0), axis=-1, keepdims=True)
    g_w = 1.0 / gsum
    e_lane = lane - N_GROUPS
    in_grp = jnp.logical_and(
        jnp.logical_and(e_lane >= 0, e_lane < N_EXPERTS),
        (e_lane // EXPERTS_PER_GROUP).astype(F32) == grp)
    el = jnp.where(in_grp, logits, -jnp.inf)
    v1 = jnp.max(el, axis=-1, keepdims=True)
    i1 = jnp.min(jnp.where(jnp.logical_and(in_grp, el == v1), lane_f, float(LANES)),
                 axis=-1, keepdims=True)
    rest = jnp.logical_and(in_grp, lane_f != i1)
    el2 = jnp.where(rest, logits, -jnp.inf)
    v2 = jnp.max(el2, axis=-1, keepdims=True)
    i2 = jnp.min(jnp.where(jnp.logical_and(rest, el2 == v2), lane_f, float(LANES)),
                 axis=-1, keepdims=True)
    e21 = jnp.exp(v2 - v1)
    w1 = g_w / (1.0 + e21)
    w2 = g_w * e21 / (1.0 + e21)
    idx = jnp.where(lane == 0, i1, i2) - float(N_GROUPS)
    wts = jnp.where(lane == 0, w1, jnp.where(lane == 1, w2, 0.0))
    return idx.astype(jnp.int32), wts


def _out_kernel(m_ref, w_ref, x_ref, g_ref, wr_ref, br_ref, h_ref, hn_ref, idx_ref, wt_ref):
    part_rows = m_ref.shape[0] // OUT_ROW_PARTS
    parts = [slice(p * part_rows, (p + 1) * part_rows) for p in range(OUT_ROW_PARTS)]
    hs = [x_ref[r, :] + jnp.dot(m_ref[r, :], w_ref[...], preferred_element_type=F32)
          for r in parts]
    for p, r in enumerate(parts):
        h = hs[p]
        h_ref[r, :] = h
        rs = lax.rsqrt(jnp.mean(h * h, axis=-1, keepdims=True) + EPS)
        hn = ((h * rs) * g_ref[...]).astype(BF16)
        for c, word in enumerate(_pack_bf16_pairs(hn)):
            hn_ref[pl.ds(p * part_rows * HN_WORD_CHUNKS + c, part_rows,
                         stride=HN_WORD_CHUNKS), :] = word
        logits = jnp.dot(hn, wr_ref[...], preferred_element_type=F32) + br_ref[...]
        idx, wts = _route(logits)
        idx_ref[r, :] = idx
        wt_ref[r, :] = wts


def _out_proj(merged, w_out, x, g2, w_router, b_router, tm=512):
    T, D = x.shape
    row = lambda i: (i, 0)
    const = lambda i: (0, 0)
    return pl.pallas_call(
        _out_kernel,
        grid=(T // tm,),
        in_specs=[pl.BlockSpec((tm, D), row),
                  pl.BlockSpec((D, D), const),
                  pl.BlockSpec((tm, D), row),
                  pl.BlockSpec((1, D), const),
                  pl.BlockSpec((D, LANES), const),
                  pl.BlockSpec((1, LANES), const)],
        out_specs=[pl.BlockSpec((tm, D), row),
                   pl.BlockSpec((tm * HN_WORD_CHUNKS, LANES), row),
                   pl.BlockSpec((tm, LANES), row),
                   pl.BlockSpec((tm, LANES), row)],
        out_shape=[jax.ShapeDtypeStruct((T, D), F32),
                   jax.ShapeDtypeStruct((T * HN_WORD_CHUNKS, LANES), jnp.uint32),
                   jax.ShapeDtypeStruct((T, LANES), jnp.int32),
                   jax.ShapeDtypeStruct((T, LANES), F32)],
        compiler_params=_params(("parallel",)),
        name="out_proj_router",
    )(merged, w_out.astype(BF16), x, g2.reshape(1, D), w_router, b_router)


def _expert_kernel(be_ref, eo_ref, ue_ref, nr_ref, bo_ref, rt_ref, nu_ref,
                   hn_hbm, wg_hbm, wu_hbm, wd_hbm, y_ref, xbuf, wgf, wuf, wdf, sem, wsem):
    b = pl.program_id(0)
    n_used = nu_ref[0]
    n_exp = nu_ref[1]
    used = b < n_used
    slot = b % ROW_SLOTS

    def weight_copies(ordinal, slot_):
        e = ue_ref[ordinal]
        return (pltpu.make_async_copy(wg_hbm.at[e], wgf.at[slot_], wsem.at[slot_]),
                pltpu.make_async_copy(wu_hbm.at[e], wuf.at[slot_], wsem.at[slot_]),
                pltpu.make_async_copy(wd_hbm.at[e], wdf.at[slot_], wsem.at[slot_]))

    def start_weights(ordinal):
        for c in weight_copies(ordinal, ordinal % WEIGHT_SLOTS):
            c.start()

    def issue_trips(blk):
        return (nr_ref[blk] + (ROWS_PER_ISSUE - 1)) // ROWS_PER_ISSUE

    def gather_rows(blk, slot_):
        first_row = bo_ref[blk]

        def issue(g, c):
            for j in range(ROWS_PER_ISSUE):
                r = g * ROWS_PER_ISSUE + j
                src = pl.multiple_of(rt_ref[first_row + r] * HN_WORD_CHUNKS, HN_WORD_CHUNKS)
                dst = pl.multiple_of(r * HN_WORD_CHUNKS, HN_WORD_CHUNKS)
                pltpu.make_async_copy(hn_hbm.at[pl.ds(src, HN_WORD_CHUNKS)],
                                      xbuf.at[slot_, pl.ds(dst, HN_WORD_CHUNKS)],
                                      sem.at[slot_]).start()
            return c

        lax.fori_loop(0, issue_trips(blk), issue, 0)

    @pl.when(b == 0)
    def _():
        start_weights(0)

        @pl.when(n_exp > 1)
        def _():
            start_weights(1)

        xbuf[...] = jnp.zeros(xbuf.shape, xbuf.dtype)
        gather_rows(0, 0)
        for a in range(1, ROW_AHEAD):
            @pl.when(n_used > a)
            def _(a=a):
                gather_rows(a, a)

    @pl.when(b + ROW_AHEAD < n_used)
    def _():
        gather_rows(b + ROW_AHEAD, (b + ROW_AHEAD) % ROW_SLOTS)

    @pl.when(used)
    def _():
        first = jnp.logical_or(b == 0, be_ref[b] != be_ref[jnp.maximum(b - 1, 0)])

        ordinal = eo_ref[b]
        ws = ordinal % WEIGHT_SLOTS

        @pl.when(first)
        def _():
            for c in weight_copies(ordinal, ws):
                c.wait()

            @pl.when(ordinal + 2 < n_exp)
            def _():
                start_weights(ordinal + 2)

        words = issue_trips(b) * (ROWS_PER_ISSUE * HN_WORD_CHUNKS)
        pltpu.make_async_copy(hn_hbm.at[pl.ds(0, words)], xbuf.at[slot, pl.ds(0, words)],
                              sem.at[slot]).wait()

        halves = ([], [])
        for c in range(HN_WORD_CHUNKS):
            hi, lo = _unpack_bf16_pair(xbuf[slot, pl.ds(c, MOE_BLOCK, stride=HN_WORD_CHUNKS), :])
            halves[0].append(hi.astype(BF16))
            halves[1].append(lo.astype(BF16))
        x = jnp.concatenate(halves[0] + halves[1], axis=1)
        hg = jnp.dot(x, wgf[ws].astype(BF16), preferred_element_type=F32)
        hu = jnp.dot(x, wuf[ws].astype(BF16), preferred_element_type=F32)
        hdn = (jax.nn.silu(hg) * hu).astype(BF16)
        y = jnp.dot(hdn, wdf[ws].astype(BF16), preferred_element_type=F32)
        for c, word in enumerate(_pack_bf16_pairs(y)):
            y_ref[pl.ds(c, MOE_BLOCK, stride=HN_WORD_CHUNKS), :] = word

    @pl.when(jnp.logical_not(used))
    def _():
        y_ref[...] = jnp.zeros(y_ref.shape, y_ref.dtype)


def _experts(hn, w_gate, w_up, w_down, block_e, block_ord, used_experts, block_rows,
             block_offset, row_tok, n_used):
    D = w_gate.shape[1]
    n_blocks = block_e.shape[0]
    n_rows = n_blocks * MOE_BLOCK
    hbm = pl.BlockSpec(memory_space=pl.ANY)
    grid_spec = pltpu.PrefetchScalarGridSpec(
        num_scalar_prefetch=7,
        grid=(n_blocks,),
        in_specs=[hbm, hbm, hbm, hbm],
        out_specs=pl.BlockSpec((MOE_BLOCK * HN_WORD_CHUNKS, LANES), lambda b, *_: (b, 0)),
        scratch_shapes=[pltpu.VMEM((ROW_SLOTS, MOE_BLOCK * HN_WORD_CHUNKS, LANES), jnp.uint32),
                        pltpu.VMEM((WEIGHT_SLOTS, D, EXPERT_FF), F32),
                        pltpu.VMEM((WEIGHT_SLOTS, D, EXPERT_FF), F32),
                        pltpu.VMEM((WEIGHT_SLOTS, EXPERT_FF, D), F32),
                        pltpu.SemaphoreType.DMA((ROW_SLOTS,)),
                        pltpu.SemaphoreType.DMA((WEIGHT_SLOTS,))],
    )
    return pl.pallas_call(
        _expert_kernel,
        grid_spec=grid_spec,
        out_shape=jax.ShapeDtypeStruct((n_rows * HN_WORD_CHUNKS, LANES), jnp.uint32),
        compiler_params=_params(("arbitrary",)),
        name="experts",
    )(block_e, block_ord, used_experts, block_rows, block_offset, row_tok, n_used,
      hn, w_gate, w_up, w_down)


def _combine_kernel(*refs, tc):
    dest_refs = refs[:COMBINE_SLOTS]
    y_hbm, h_ref, wt_ref, o_ref, ybuf, sem = refs[COMBINE_SLOTS:]
    i = pl.program_id(0)
    slot = i % COMBINE_SLOTS

    def gather_rows(dest_ref, slot_):
        def issue(r, c):
            dst = pl.multiple_of(r * HN_WORD_CHUNKS, HN_WORD_CHUNKS)
            for k in range(TOP_K):
                src = pl.multiple_of(dest_ref[0, 0, r * TOP_K + k] * HN_WORD_CHUNKS, HN_WORD_CHUNKS)
                pltpu.make_async_copy(y_hbm.at[pl.ds(src, HN_WORD_CHUNKS)],
                                      ybuf.at[slot_, k, pl.ds(dst, HN_WORD_CHUNKS)],
                                      sem.at[slot_]).start(priority=k % 2)
            return c

        lax.fori_loop(0, tc, issue, 0, unroll=4)

    @pl.when(i == 0)
    def _():
        for a in range(COMBINE_AHEAD):
            gather_rows(dest_refs[a], a)

    @pl.when(i + COMBINE_AHEAD < pl.num_programs(0))
    def _():
        gather_rows(dest_refs[COMBINE_AHEAD], (i + COMBINE_AHEAD) % COMBINE_SLOTS)

    for k in range(TOP_K):
        pltpu.make_async_copy(y_hbm.at[pl.ds(0, tc * HN_WORD_CHUNKS)], ybuf.at[slot, k],
                              sem.at[slot]).wait()
    wt = wt_ref[...]
    w1, w2 = wt[:, 0:1], wt[:, 1:2]
    for c in range(HN_WORD_CHUNKS):
        rows = pl.ds(c, tc, stride=HN_WORD_CHUNKS)
        y1, y2 = (_unpack_bf16_pair(ybuf[slot, k, rows, :]) for k in range(TOP_K))
        for half, col in enumerate((c * LANES, (c + HN_WORD_CHUNKS) * LANES)):
            o_ref[:, col:col + LANES] = h_ref[:, col:col + LANES] + (w1 * y1[half] + w2 * y2[half])


def _combine(yrows, h, wts, dest, tc=256):
    T, D = h.shape
    steps = T // tc
    row = lambda i: (i, 0)
    dest3 = dest.reshape(steps, 1, tc * TOP_K)
    dest_block = lambda im: pl.BlockSpec((1, 1, tc * TOP_K), im, memory_space=pltpu.SMEM)
    return pl.pallas_call(
        functools.partial(_combine_kernel, tc=tc),
        grid=(steps,),
        in_specs=[dest_block(lambda i, a=a: (jnp.minimum(i + a, steps - 1), 0, 0))
                  for a in range(COMBINE_SLOTS)]
                 + [pl.BlockSpec(memory_space=pl.ANY),
                    pl.BlockSpec((tc, D), row),
                    pl.BlockSpec((tc, LANES), row)],
        out_specs=pl.BlockSpec((tc, D), row),
        out_shape=jax.ShapeDtypeStruct((T, D), F32),
        scratch_shapes=[pltpu.VMEM((COMBINE_SLOTS, TOP_K, tc * HN_WORD_CHUNKS, LANES), jnp.uint32),
                        pltpu.SemaphoreType.DMA((COMBINE_SLOTS,))],
        compiler_params=_params(("arbitrary",)),
        name="combine",
    )(*([dest3] * COMBINE_SLOTS), yrows, h, wts)


def _lane_cumsum(x):
    lane = lax.broadcasted_iota(jnp.int32, x.shape, 1)
    s = 1
    while s < LANES:
        x = x + jnp.where(lane >= s, pltpu.roll(x, s, 1), 0)
        s *= 2
    return x


def _dispatch_kernel(idx_ref, dest_ref, meta_ref, run_ref, prefix_ref, start_ref, *, tb):
    p = pl.program_id(0)
    i = pl.program_id(1)
    idx = idx_ref[...]
    lane = lax.broadcasted_iota(jnp.int32, idx.shape, 1)
    e1 = idx[:, 0:1]
    e2 = idx[:, 1:2]
    onehot = jnp.where(jnp.logical_or(lane == e1, lane == e2), 1.0, 0.0)

    @pl.when(jnp.logical_and(p == 0, i == 0))
    def _():
        run_ref[...] = jnp.zeros(run_ref.shape, F32)

    @pl.when(p == 0)
    def _():
        prefix_ref[i] = run_ref[...]
        run_ref[...] = run_ref[...] + jnp.sum(onehot, axis=0, keepdims=True)

    @pl.when(jnp.logical_and(p == 1, i == 0))
    def _():
        counts = run_ref[...].astype(jnp.int32)
        nblk = (counts + (MOE_BLOCK - 1)) // MOE_BLOCK
        end_blk = _lane_cumsum(nblk)
        start_ref[...] = ((end_blk - nblk) * MOE_BLOCK).astype(F32)
        has = jnp.where(counts > 0, 1, 0)
        ordinal = _lane_cumsum(has) - 1
        first_row = (_lane_cumsum(counts) - counts)[0:1]
        start_blk, counts = (end_blk - nblk)[0:1], counts[0:1]
        end_blk, has, ordinal = end_blk[0:1], has[0:1], ordinal[0:1]
        rows = lax.broadcasted_iota(jnp.int32, (META_ROWS, LANES), 0)
        lanes = lax.broadcasted_iota(jnp.int32, (META_ROWS, LANES), 1)
        is_e = lanes < N_EXPERTS
        rsum = lambda v: jnp.sum(v, axis=-1, keepdims=True)
        be = rsum(jnp.where(jnp.logical_and(is_e, end_blk <= rows), 1, 0))
        be = jnp.minimum(be, N_EXPERTS - 1)
        eo = rsum(jnp.where(lanes == be, ordinal, 0))
        ue = rsum(jnp.where(jnp.logical_and(has > 0, ordinal == rows), lanes, 0))
        n_blk = rsum(jnp.where(lanes == N_EXPERTS - 1, end_blk, 0))
        n_exp = rsum(jnp.where(is_e, has, 0))
        own = lanes == be
        done = MOE_BLOCK * (rows - rsum(jnp.where(own, start_blk, 0)))
        n_valid = jnp.clip(rsum(jnp.where(own, counts, 0)) - done, 0, MOE_BLOCK)
        offset = rsum(jnp.where(own, first_row, 0)) + done
        columns = (be, eo, ue, n_blk, n_exp, n_valid, offset)
        meta = jnp.zeros((META_ROWS, LANES), jnp.int32)
        for col, val in enumerate(columns):
            meta = jnp.where(lanes == col, val, meta)
        meta_ref[...] = meta

    @pl.when(p == 1)
    def _():
        r = lax.broadcasted_iota(jnp.int32, (tb, tb), 0)
        c = lax.broadcasted_iota(jnp.int32, (tb, tb), 1)
        earlier = jnp.where(c < r, 1.0, 0.0).astype(BF16)
        rank = jnp.dot(earlier, onehot.astype(BF16), preferred_element_type=F32)
        rank = rank + prefix_ref[i][0:1] + start_ref[0:1]
        d1 = jnp.sum(jnp.where(lane == e1, rank, 0.0), axis=-1, keepdims=True)
        d2 = jnp.sum(jnp.where(lane == e2, rank, 0.0), axis=-1, keepdims=True)
        dest_ref[...] = jnp.where(lane == 0, d1, d2).astype(jnp.int32)


def _dispatch(idx, tb=1024):
    T = idx.shape[0]
    n_rows = T * TOP_K + N_EXPERTS * MOE_BLOCK
    n_blocks = n_rows // MOE_BLOCK
    assert n_blocks <= META_ROWS
    dest2, meta = pl.pallas_call(
        functools.partial(_dispatch_kernel, tb=tb),
        grid=(2, T // tb),
        in_specs=[pl.BlockSpec((tb, LANES), lambda p, i: (i, 0))],
        out_specs=[pl.BlockSpec((tb, LANES), lambda p, i: (i * p, 0)),
                   pl.BlockSpec((META_ROWS, LANES), lambda p, i: (0, 0))],
        out_shape=[jax.ShapeDtypeStruct((T, LANES), jnp.int32),
                   jax.ShapeDtypeStruct((META_ROWS, LANES), jnp.int32)],
        scratch_shapes=[pltpu.VMEM((SUBLANES, LANES), F32),
                        pltpu.VMEM((T // tb, SUBLANES, LANES), F32),
                        pltpu.VMEM((SUBLANES, LANES), F32)],
        compiler_params=_params(("arbitrary", "arbitrary")),
        name="dispatch",
    )(idx)
    dest = dest2[:, :TOP_K].reshape(T * TOP_K)
    row_tok = (jnp.argsort(dest) // TOP_K).astype(jnp.int32)
    row_tok = jnp.concatenate([row_tok, jnp.zeros((ROWS_PER_ISSUE,), jnp.int32)])
    block_e = meta[:n_blocks, 0]
    block_ord = meta[:n_blocks, 1]
    used_experts = meta[:N_EXPERTS, 2]
    n_used = meta[0, 3:5]
    block_rows = meta[:n_blocks, 5]
    block_offset = meta[:n_blocks, 6]
    return block_e, block_ord, used_experts, block_rows, block_offset, row_tok, n_used, dest


def kernel(x, positions, norm1_g, w_in, q_norm_g, k_norm_g, sink_logits, sg_ln_g, sg_ln_b, sg_w, sg_b, w_branch_att, w_branch_sg, w_out, norm2_g, w_group_router, b_group_router, w_expert_router, b_expert_router, w_gate, w_up, w_down):
    B, S, D = x.shape
    T = B * S
    assert D == D_MODEL and S % (2 * ATT_BLOCK) == 0 and S % SG_CHUNK == 0
    assert x.dtype == F32 and positions.shape == (B, S)
    assert w_in.shape[1:] == (D_MODEL, IN_COLS) and w_out.shape[1:] == (D_MODEL, D_MODEL)
    assert w_gate.shape[1:] == (N_EXPERTS, D_MODEL, EXPERT_FF) and w_down.shape[1:] == (
        N_EXPERTS, EXPERT_FF, D_MODEL)
    h = x.reshape(T, D)
    pos = positions.reshape(T)
    for l in range(norm1_g.shape[0]):
        xn = _rmsnorm(h, norm1_g[l])
        proj = _in_proj(xn, w_in[l])
        q, k, v = _qkv_prep(proj, pos, q_norm_g[l], k_norm_g[l])
        att = _attention(q, k, v, sink_logits[l], B)
        sgo = _spatial_gating(proj, sg_ln_g[l], sg_ln_b[l], sg_w[l], sg_b[l])
        merged = _merge(att, sgo, proj, w_branch_att[l], w_branch_sg[l])
        pad = LANES - N_GROUPS - N_EXPERTS
        w_router = jnp.concatenate(
            [w_group_router[l], w_expert_router[l], jnp.zeros((D, pad), F32)], axis=1).astype(BF16)
        b_router = jnp.concatenate(
            [b_group_router[l], b_expert_router[l], jnp.zeros((pad,), F32)]).reshape(1, LANES)
        h, hn, idx, wts = _out_proj(merged, w_out[l], h, norm2_g[l], w_router, b_router)
        (block_e, block_ord, used_experts, block_rows, block_offset, row_tok, n_used,
         dest) = _dispatch(idx)
        yrows = _experts(hn, w_gate[l], w_up[l], w_down[l], block_e, block_ord, used_experts,
                         block_rows, block_offset, row_tok, n_used)
        h = _combine(yrows, h, wts, dest)
    return h.reshape(B, S, D)
```

```python
import functools

import jax
import jax.numpy as jnp
from jax import lax
from jax.experimental import pallas as pl
from jax.experimental.pallas import tpu as pltpu

F32 = jnp.float32
BF16 = jnp.bfloat16

D_MODEL = 2048
HEAD_DIM = 64
ATT_WIDTH = D_MODEL // 2
ATT_HEADS = ATT_WIDTH // HEAD_DIM
ATT_KV_HEADS = ATT_HEADS // 4
Q_PER_KV = ATT_HEADS // ATT_KV_HEADS
KV_WIDTH = ATT_KV_HEADS * HEAD_DIM
WINDOW = 128
ATT_BLOCK = 128
ROPE_DIM = HEAD_DIM // 4
ROPE_HALF = ROPE_DIM // 2
ROPE_THETA = 500000.0
SG_WIDTH = D_MODEL // 2
SG_GROUP_DIM = 128
SG_GROUPS = SG_WIDTH // SG_GROUP_DIM
SG_CHUNK = 128
OFF_Q = 0
OFF_K = OFF_Q + ATT_WIDTH
OFF_V = OFF_K + KV_WIDTH
OFF_U = OFF_V + KV_WIDTH
OFF_S = OFF_U + SG_WIDTH
OFF_G = OFF_S + SG_WIDTH
IN_COLS = OFF_G + 2 * D_MODEL
N_GROUPS = 8
EXPERTS_PER_GROUP = 8
N_EXPERTS = N_GROUPS * EXPERTS_PER_GROUP
TOP_K = 2
EXPERT_FF = D_MODEL // 4
MOE_BLOCK = 128
EPS = 1e-6
NEG_INF = -1e30

LANES = 128
SUBLANES = 8
BF16_BITS = 16
HN_WORD_CHUNKS = D_MODEL // LANES // 2
ROWS_PER_ISSUE = 8
ROW_AHEAD = 4
ROW_SLOTS = ROW_AHEAD + 1
SG_CHUNKS_PER_TRIP = 2
GATE_BLOCK = 512
MERGE_GATE_BLOCKS = 2
MERGE_ROW_PARTS = 2
OUT_ROW_PARTS = 4
WEIGHT_SLOTS = 3
COMBINE_AHEAD = 2
COMBINE_SLOTS = COMBINE_AHEAD + 1
META_ROWS = 256
VMEM_LIMIT = 56 * 1024 * 1024

assert WINDOW == ATT_BLOCK


def _pack_bf16_pairs(x):
    bits = lax.bitcast_convert_type(x.astype(BF16).astype(F32), jnp.uint32)
    words = []
    for c in range(HN_WORD_CHUNKS):
        hi = bits[:, c * LANES:(c + 1) * LANES]
        lo = bits[:, (c + HN_WORD_CHUNKS) * LANES:(c + HN_WORD_CHUNKS + 1) * LANES]
        words.append(hi | (lo >> BF16_BITS))
    return words


def _unpack_bf16_pair(word):
    high_mask = jnp.uint32(((1 << BF16_BITS) - 1) << BF16_BITS)
    return (lax.bitcast_convert_type(word & high_mask, F32),
            lax.bitcast_convert_type(word << BF16_BITS, F32))


def _params(sem, vmem=VMEM_LIMIT):
    return pltpu.CompilerParams(dimension_semantics=sem, vmem_limit_bytes=vmem)


def _rmsnorm_kernel(x_ref, g_ref, o_ref):
    x = x_ref[...]
    r = lax.rsqrt(jnp.mean(x * x, axis=-1, keepdims=True) + EPS)
    o_ref[...] = ((x * r) * g_ref[...]).astype(o_ref.dtype)


def _rmsnorm(x, g, tm=1024):
    T, D = x.shape
    return pl.pallas_call(
        _rmsnorm_kernel,
        grid=(T // tm,),
        in_specs=[pl.BlockSpec((tm, D), lambda i: (i, 0)),
                  pl.BlockSpec((1, D), lambda i: (0, 0))],
        out_specs=pl.BlockSpec((tm, D), lambda i: (i, 0)),
        out_shape=jax.ShapeDtypeStruct((T, D), BF16),
        compiler_params=_params(("parallel",)),
        name="norm1",
    )(x, g.reshape(1, D))


def _proj_kernel(x_ref, w_ref, o_ref, wbf_ref):
    @pl.when(pl.program_id(1) == 0)
    def _():
        wbf_ref[...] = w_ref[...].astype(BF16)

    o_ref[...] = jnp.dot(x_ref[...], wbf_ref[...], preferred_element_type=F32)


def _in_proj(xn, w, tm=1024, tn=1280):
    T, D = xn.shape
    N = w.shape[1]
    return pl.pallas_call(
        _proj_kernel,
        grid=(N // tn, T // tm),
        in_specs=[pl.BlockSpec((tm, D), lambda j, i: (i, 0)),
                  pl.BlockSpec((D, tn), lambda j, i: (0, j))],
        out_specs=pl.BlockSpec((tm, tn), lambda j, i: (i, j)),
        out_shape=jax.ShapeDtypeStruct((T, N), F32),
        scratch_shapes=[pltpu.VMEM((D, tn), BF16)],
        compiler_params=_params(("arbitrary", "arbitrary")),
        name="in_proj",
    )(xn, w)


def _rope_table_kernel(pos_ref, invf_ref, cos_ref, sin_ref):
    ang = pos_ref[...].astype(F32) * invf_ref[...]
    cos_ref[...] = jnp.cos(ang)
    sin_ref[...] = jnp.sin(ang)


def _rope_tables(positions):
    T = positions.shape[0]
    rows = T * ROPE_HALF // LANES
    inv = ROPE_THETA ** (-jnp.arange(0, ROPE_DIM, 2, dtype=F32) / ROPE_DIM)
    invf = jnp.tile(inv, LANES // ROPE_HALF).reshape(1, LANES)
    pos = jnp.repeat(positions, ROPE_HALF).reshape(rows, LANES)
    whole = lambda: (0, 0)
    cos, sin = pl.pallas_call(
        _rope_table_kernel,
        in_specs=[pl.BlockSpec((rows, LANES), whole), pl.BlockSpec((1, LANES), whole)],
        out_specs=[pl.BlockSpec((rows, LANES), whole), pl.BlockSpec((rows, LANES), whole)],
        out_shape=[jax.ShapeDtypeStruct((rows, LANES), F32)] * 2,
        name="rope_tables",
    )(pos, invf)
    return cos.reshape(T, ROPE_HALF), sin.reshape(T, ROPE_HALF)


def _qkv_prep_kernel(p_ref, cos_ref, sin_ref, cpat_ref, spat_ref, gq_ref, gk_ref, seg_ref,
                     q_ref, k_ref, v_ref):
    lane = lax.broadcasted_iota(jnp.int32, (cos_ref.shape[0], LANES), 1)
    in_head = lane % HEAD_DIM

    def spread(t_ref, pattern_ref):
        t = t_ref[...]
        hi = t.astype(BF16)
        r1 = t - hi.astype(F32)
        mid = r1.astype(BF16)
        lo = (r1 - mid.astype(F32)).astype(BF16)
        return sum(jnp.dot(term, pattern_ref[...], preferred_element_type=F32)
                   for term in (hi, mid, lo))

    cos = jnp.where(in_head < ROPE_DIM, spread(cos_ref, cpat_ref), 1.0)
    sin = spread(sin_ref, spat_ref)
    first_half = in_head < ROPE_HALF

    def norm_rope(x, g):
        x2 = x * x
        x2_hi = x2.astype(BF16)
        x2_lo = (x2 - x2_hi.astype(F32)).astype(BF16)
        both = jnp.dot(jnp.concatenate([x2_hi, x2_lo], axis=0), seg_ref[...],
                       preferred_element_type=F32)
        ssq = both[:x.shape[0]] + both[x.shape[0]:]
        xn = (x * lax.rsqrt(ssq * (1.0 / HEAD_DIM) + EPS)) * g
        partner = jnp.where(first_half,
                            pltpu.roll(xn, LANES - ROPE_HALF, 1),
                            pltpu.roll(xn, ROPE_HALF, 1))
        return xn * cos + partner * sin

    for c in range(ATT_WIDTH // LANES):
        x = p_ref[:, OFF_Q + c * LANES:OFF_Q + (c + 1) * LANES]
        q_ref[:, c * LANES:(c + 1) * LANES] = (
            norm_rope(x, gq_ref[...]) * (HEAD_DIM ** -0.5)).astype(q_ref.dtype)
    for c in range(KV_WIDTH // LANES):
        x = p_ref[:, OFF_K + c * LANES:OFF_K + (c + 1) * LANES]
        k_ref[:, c * LANES:(c + 1) * LANES] = norm_rope(x, gk_ref[...]).astype(k_ref.dtype)
    v_ref[...] = p_ref[:, OFF_V:OFF_V + KV_WIDTH].astype(v_ref.dtype)


def _qkv_prep(proj, positions, q_g, k_g, tq=1024):
    T = proj.shape[0]
    width = OFF_U
    cos8, sin8 = _rope_tables(positions)
    in_head = jnp.arange(LANES) % HEAD_DIM
    picks = (in_head[None, :] % ROPE_HALF == jnp.arange(ROPE_HALF)[:, None]) & (in_head < ROPE_DIM)
    cos_pattern = picks.astype(BF16)
    sin_pattern = (picks * jnp.where(in_head < ROPE_HALF, -1.0, 1.0)).astype(BF16)
    head_of_lane = jnp.arange(LANES) // HEAD_DIM
    same_head = (head_of_lane[:, None] == head_of_lane[None, :]).astype(BF16)
    gq = jnp.tile(q_g, LANES // HEAD_DIM).reshape(1, LANES)
    gk = jnp.tile(k_g, LANES // HEAD_DIM).reshape(1, LANES)
    row = lambda i: (i, 0)
    const = lambda i: (0, 0)
    return pl.pallas_call(
        _qkv_prep_kernel,
        grid=(T // tq,),
        in_specs=[pl.BlockSpec((tq, width), row),
                  pl.BlockSpec((tq, ROPE_HALF), row),
                  pl.BlockSpec((tq, ROPE_HALF), row),
                  pl.BlockSpec((ROPE_HALF, LANES), const),
                  pl.BlockSpec((ROPE_HALF, LANES), const),
                  pl.BlockSpec((1, LANES), const),
                  pl.BlockSpec((1, LANES), const),
                  pl.BlockSpec((LANES, LANES), const)],
        out_specs=[pl.BlockSpec((tq, ATT_WIDTH), row),
                   pl.BlockSpec((tq, KV_WIDTH), row),
                   pl.BlockSpec((tq, KV_WIDTH), row)],
        out_shape=[jax.ShapeDtypeStruct((T, ATT_WIDTH), BF16),
                   jax.ShapeDtypeStruct((T, KV_WIDTH), BF16),
                   jax.ShapeDtypeStruct((T, KV_WIDTH), BF16)],
        compiler_params=_params(("parallel",)),
        name="qkv_prep",
    )(proj, cos8, sin8, cos_pattern, sin_pattern, gq, gk, same_head)


def _attn_kernel(sink_ref, q_ref, kp_ref, kc_ref, kn_ref, vp_ref, vc_ref, vn_ref, o_ref, *, nb):
    n = pl.program_id(1)
    rows = ATT_BLOCK
    qi = lax.broadcasted_iota(jnp.int32, (rows, ATT_BLOCK), 0) % ATT_BLOCK
    kj = lax.broadcasted_iota(jnp.int32, (rows, ATT_BLOCK), 1)
    lo_prev = jnp.where(n > 0, 0, ATT_BLOCK)
    hi_next = jnp.where(n < nb - 1, 0, -ATT_BLOCK)
    cap_prev = jnp.where(kj - qi >= lo_prev, jnp.inf, NEG_INF)
    cap_next = jnp.where(kj - qi <= hi_next, jnp.inf, NEG_INF)
    cap = jnp.concatenate([cap_prev, jnp.full((rows, ATT_BLOCK), jnp.inf, F32), cap_next], axis=1)

    kvhs = range(ATT_KV_HEADS)
    scores = []
    for kvh in kvhs:
        cols = slice(kvh * HEAD_DIM, (kvh + 1) * HEAD_DIM)
        k = jnp.concatenate([kp_ref[:, cols], kc_ref[:, cols], kn_ref[:, cols]], axis=0)
        q = jnp.concatenate(
            [q_ref[:, (kvh * Q_PER_KV + g) * HEAD_DIM:(kvh * Q_PER_KV + g + 1) * HEAD_DIM]
             for g in range(Q_PER_KV)], axis=0)
        scores.append(lax.dot_general(q, k, (((1,), (1,)), ((), ())),
                                      preferred_element_type=F32))
    probs = []
    for kvh in kvhs:
        strips = []
        for g in range(Q_PER_KV):
            sink = sink_ref[kvh * Q_PER_KV + g]
            sg = jnp.minimum(scores[kvh][g * ATT_BLOCK:(g + 1) * ATT_BLOCK], cap)
            m = jnp.maximum(jnp.max(sg, axis=-1, keepdims=True), sink)
            e = jnp.exp(sg - m)
            denom = jnp.sum(e, axis=-1, keepdims=True) + jnp.exp(sink - m)
            strips.append((e / denom).astype(BF16))
        probs.append(jnp.concatenate(strips, axis=0))
    for kvh in kvhs:
        cols = slice(kvh * HEAD_DIM, (kvh + 1) * HEAD_DIM)
        v = jnp.concatenate([vp_ref[:, cols], vc_ref[:, cols], vn_ref[:, cols]], axis=0)
        o = jnp.dot(probs[kvh], v, preferred_element_type=F32)
        for g in range(Q_PER_KV):
            h = kvh * Q_PER_KV + g
            o_ref[:, h * HEAD_DIM:(h + 1) * HEAD_DIM] = (
                o[g * ATT_BLOCK:(g + 1) * ATT_BLOCK].astype(o_ref.dtype))


def _attention(q, k, v, sink, batch):
    T = q.shape[0]
    nb = T // batch // ATT_BLOCK
    cur = lambda b, n: (b * nb + n, 0)
    prev = lambda b, n: (b * nb + jnp.maximum(n - 1, 0), 0)
    nxt = lambda b, n: (b * nb + jnp.minimum(n + 1, nb - 1), 0)
    kv = lambda im: pl.BlockSpec((ATT_BLOCK, KV_WIDTH), im)
    return pl.pallas_call(
        functools.partial(_attn_kernel, nb=nb),
        grid=(batch, nb),
        in_specs=[pl.BlockSpec(memory_space=pltpu.SMEM),
                  pl.BlockSpec((ATT_BLOCK, ATT_WIDTH), cur),
                  kv(prev), kv(cur), kv(nxt), kv(prev), kv(cur), kv(nxt)],
        out_specs=pl.BlockSpec((ATT_BLOCK, ATT_WIDTH), cur),
        out_shape=jax.ShapeDtypeStruct((T, ATT_WIDTH), BF16),
        compiler_params=_params(("parallel", "parallel")),
        name="window_attn",
    )(sink, q, k, k, k, v, v, v)


def _sg_kernel(u_ref, s_ref, lng_ref, lnb_ref, w_ref, b_ref, o_ref):
    groups = w_ref.shape[0]

    def some_chunks(ci, carry):
        tiles = []
        for j in range(SG_CHUNKS_PER_TRIP):
            start = pl.multiple_of((ci * SG_CHUNKS_PER_TRIP + j) * SG_CHUNK, SG_CHUNK)
            for gi in range(groups):
                tiles.append((pl.ds(start, SG_CHUNK),
                              slice(gi * SG_GROUP_DIM, (gi + 1) * SG_GROUP_DIM), gi))
        ts = range(len(tiles))
        s = [jax.nn.gelu(s_ref[r, c]) for r, c, _ in tiles]
        sc = [s[t] - jnp.mean(s[t], axis=-1, keepdims=True) for t in ts]
        var = [jnp.mean(sc[t] * sc[t], axis=-1, keepdims=True) for t in ts]
        sn = [(sc[t] * lax.rsqrt(var[t] + EPS)) * lng_ref[0, gi:gi + 1, :]
              + lnb_ref[0, gi:gi + 1, :] for t, (_, _, gi) in enumerate(tiles)]
        mixed = [jnp.dot(w_ref[gi], sn[t].astype(BF16), preferred_element_type=F32)
                 + b_ref[0, :, gi:gi + 1] for t, (_, _, gi) in enumerate(tiles)]
        for t, (r, c, _) in enumerate(tiles):
            o_ref[r, c] = (jax.nn.gelu(u_ref[r, c]) * mixed[t]).astype(o_ref.dtype)
        return carry

    lax.fori_loop(0, u_ref.shape[0] // (SG_CHUNK * SG_CHUNKS_PER_TRIP), some_chunks, 0)


def _spatial_gating(proj, ln_g, ln_b, w_s, b_s, halves=2, chunks=16):
    T = proj.shape[0]
    half_w = SG_WIDTH // halves
    gph = SG_GROUPS // halves
    u0, s0 = OFF_U // half_w, OFF_S // half_w
    tq = chunks * SG_CHUNK
    return pl.pallas_call(
        _sg_kernel,
        grid=(T // tq, halves),
        in_specs=[pl.BlockSpec((tq, half_w), lambda i, j: (i, u0 + j)),
                  pl.BlockSpec((tq, half_w), lambda i, j: (i, s0 + j)),
                  pl.BlockSpec((1, gph, SG_GROUP_DIM), lambda i, j: (j, 0, 0)),
                  pl.BlockSpec((1, gph, SG_GROUP_DIM), lambda i, j: (j, 0, 0)),
                  pl.BlockSpec((gph, SG_CHUNK, SG_CHUNK), lambda i, j: (j, 0, 0)),
                  pl.BlockSpec((1, SG_CHUNK, gph), lambda i, j: (j, 0, 0))],
        out_specs=pl.BlockSpec((tq, half_w), lambda i, j: (i, j)),
        out_shape=jax.ShapeDtypeStruct((T, SG_WIDTH), BF16),
        compiler_params=_params(("parallel", "parallel")),
        name="spatial_gating",
    )(proj, proj,
      ln_g.reshape(halves, gph, SG_GROUP_DIM), ln_b.reshape(halves, gph, SG_GROUP_DIM),
      w_s.astype(BF16),
      b_s.reshape(halves, gph, SG_CHUNK).transpose(0, 2, 1))


def _merge_kernel(att_ref, sgo_ref, *refs):
    gate_refs = refs[:2 * MERGE_GATE_BLOCKS]
    wa_ref, wb_ref, o_ref, wa_bf, wb_bf = refs[2 * MERGE_GATE_BLOCKS:]

    @pl.when(pl.program_id(1) == 0)
    def _():
        wa_bf[...] = wa_ref[...].astype(BF16)
        wb_bf[...] = wb_ref[...].astype(BF16)

    part_rows = att_ref.shape[0] // MERGE_ROW_PARTS
    parts = [slice(p * part_rows, (p + 1) * part_rows) for p in range(MERGE_ROW_PARTS)]
    a = [jnp.dot(att_ref[r, :], wa_bf[...], preferred_element_type=F32) for r in parts]
    b = [jnp.dot(sgo_ref[r, :], wb_bf[...], preferred_element_type=F32) for r in parts]
    for p, r in enumerate(parts):
        for g in range(MERGE_GATE_BLOCKS):
            cols = slice(g * GATE_BLOCK, (g + 1) * GATE_BLOCK)
            ga_ref, gb_ref = gate_refs[g], gate_refs[MERGE_GATE_BLOCKS + g]
            m = (jax.nn.sigmoid(ga_ref[r, :]) * a[p][:, cols]
                 + jax.nn.sigmoid(gb_ref[r, :]) * b[p][:, cols])
            o_ref[r, cols] = m.astype(o_ref.dtype)


def _merge(att, sgo, proj, w_a, w_b, tm=512):
    T = att.shape[0]
    tn = MERGE_GATE_BLOCKS * GATE_BLOCK
    ga0 = OFF_G // GATE_BLOCK
    gb0 = (OFF_G + D_MODEL) // GATE_BLOCK
    gate = lambda first, g: pl.BlockSpec(
        (tm, GATE_BLOCK), lambda j, i: (i, first + j * MERGE_GATE_BLOCKS + g))
    gates = ([gate(ga0, g) for g in range(MERGE_GATE_BLOCKS)]
             + [gate(gb0, g) for g in range(MERGE_GATE_BLOCKS)])
    return pl.pallas_call(
        _merge_kernel,
        grid=(D_MODEL // tn, T // tm),
        in_specs=[pl.BlockSpec((tm, ATT_WIDTH), lambda j, i: (i, 0)),
                  pl.BlockSpec((tm, SG_WIDTH), lambda j, i: (i, 0))]
                 + gates
                 + [pl.BlockSpec((ATT_WIDTH, tn), lambda j, i: (0, j)),
                    pl.BlockSpec((SG_WIDTH, tn), lambda j, i: (0, j))],
        out_specs=pl.BlockSpec((tm, tn), lambda j, i: (i, j)),
        out_shape=jax.ShapeDtypeStruct((T, D_MODEL), BF16),
        scratch_shapes=[pltpu.VMEM((ATT_WIDTH, tn), BF16),
                        pltpu.VMEM((SG_WIDTH, tn), BF16)],
        compiler_params=_params(("arbitrary", "arbitrary")),
        name="merge",
    )(att, sgo, *([proj] * (2 * MERGE_GATE_BLOCKS)), w_a, w_b)


def _route(logits):
    lane = lax.broadcasted_iota(jnp.int32, logits.shape, 1)
    lane_f = lane.astype(F32)
    is_g = lane < N_GROUPS
    gl = jnp.where(is_g, logits, -jnp.inf)
    gmax = jnp.max(gl, axis=-1, keepdims=True)
    grp = jnp.min(jnp.where(gl == gmax, lane_f, float(LANES)), axis=-1, keepdims=True)
    gsum = jnp.sum(jnp.where(is_g, jnp.exp(logits - gmax), 0.0), axis=-1, keepdims=True)
    g_w = 1.0 / gsum
    e_lane = lane - N_GROUPS
    in_grp = jnp.logical_and(
        jnp.logical_and(e_lane >= 0, e_lane < N_EXPERTS),
        (e_lane // EXPERTS_PER_GROUP).astype(F32) == grp)
    el = jnp.where(in_grp, logits, -jnp.inf)
    v1 = jnp.max(el, axis=-1, keepdims=True)
    i1 = jnp.min(jnp.where(jnp.logical_and(in_grp, el == v1), lane_f, float(LANES)),
                 axis=-1, keepdims=True)
    rest = jnp.logical_and(in_grp, lane_f != i1)
    el2 = jnp.where(rest, logits, -jnp.inf)
    v2 = jnp.max(el2, axis=-1, keepdims=True)
    i2 = jnp.min(jnp.where(jnp.logical_and(rest, el2 == v2), lane_f, float(LANES)),
                 axis=-1, keepdims=True)
    e21 = jnp.exp(v2 - v1)
    w1 = g_w / (1.0 + e21)
    w2 = g_w * e21 / (1.0 + e21)
    idx = jnp.where(lane == 0, i1, i2) - float(N_GROUPS)
    wts = jnp.where(lane == 0, w1, jnp.where(lane == 1, w2, 0.0))
    return idx.astype(jnp.int32), wts


def _out_kernel(m_ref, w_ref, x_ref, g_ref, wr_ref, br_ref, h_ref, hn_ref, idx_ref, wt_ref):
    part_rows = m_ref.shape[0] // OUT_ROW_PARTS
    parts = [slice(p * part_rows, (p + 1) * part_rows) for p in range(OUT_ROW_PARTS)]
    hs = [x_ref[r, :] + jnp.dot(m_ref[r, :], w_ref[...], preferred_element_type=F32)
          for r in parts]
    for p, r in enumerate(parts):
        h = hs[p]
        h_ref[r, :] = h
        rs = lax.rsqrt(jnp.mean(h * h, axis=-1, keepdims=True) + EPS)
        hn = ((h * rs) * g_ref[...]).astype(BF16)
        for c, word in enumerate(_pack_bf16_pairs(hn)):
            hn_ref[pl.ds(p * part_rows * HN_WORD_CHUNKS + c, part_rows,
                         stride=HN_WORD_CHUNKS), :] = word
        logits = jnp.dot(hn, wr_ref[...], preferred_element_type=F32) + br_ref[...]
        idx, wts = _route(logits)
        idx_ref[r, :] = idx
        wt_ref[r, :] = wts


def _out_proj(merged, w_out, x, g2, w_router, b_router, tm=512):
    T, D = x.shape
    row = lambda i: (i, 0)
    const = lambda i: (0, 0)
    return pl.pallas_call(
        _out_kernel,
        grid=(T // tm,),
        in_specs=[pl.BlockSpec((tm, D), row),
                  pl.BlockSpec((D, D), const),
                  pl.BlockSpec((tm, D), row),
                  pl.BlockSpec((1, D), const),
                  pl.BlockSpec((D, LANES), const),
                  pl.BlockSpec((1, LANES), const)],
        out_specs=[pl.BlockSpec((tm, D), row),
                   pl.BlockSpec((tm * HN_WORD_CHUNKS, LANES), row),
                   pl.BlockSpec((tm, LANES), row),
                   pl.BlockSpec((tm, LANES), row)],
        out_shape=[jax.ShapeDtypeStruct((T, D), F32),
                   jax.ShapeDtypeStruct((T * HN_WORD_CHUNKS, LANES), jnp.uint32),
                   jax.ShapeDtypeStruct((T, LANES), jnp.int32),
                   jax.ShapeDtypeStruct((T, LANES), F32)],
        compiler_params=_params(("parallel",)),
        name="out_proj_router",
    )(merged, w_out.astype(BF16), x, g2.reshape(1, D), w_router, b_router)


def _expert_kernel(be_ref, eo_ref, ue_ref, nr_ref, bo_ref, rt_ref, nu_ref,
                   hn_hbm, wg_hbm, wu_hbm, wd_hbm, y_ref, xbuf, wgf, wuf, wdf, sem, wsem):
    b = pl.program_id(0)
    n_used = nu_ref[0]
    n_exp = nu_ref[1]
    used = b < n_used
    slot = b % ROW_SLOTS

    def weight_copies(ordinal, slot_):
        e = ue_ref[ordinal]
        return (pltpu.make_async_copy(wg_hbm.at[e], wgf.at[slot_], wsem.at[slot_]),
                pltpu.make_async_copy(wu_hbm.at[e], wuf.at[slot_], wsem.at[slot_]),
                pltpu.make_async_copy(wd_hbm.at[e], wdf.at[slot_], wsem.at[slot_]))

    def start_weights(ordinal):
        for c in weight_copies(ordinal, ordinal % WEIGHT_SLOTS):
            c.start()

    def issue_trips(blk):
        return (nr_ref[blk] + (ROWS_PER_ISSUE - 1)) // ROWS_PER_ISSUE

    def gather_rows(blk, slot_):
        first_row = bo_ref[blk]

        def issue(g, c):
            for j in range(ROWS_PER_ISSUE):
                r = g * ROWS_PER_ISSUE + j
                src = pl.multiple_of(rt_ref[first_row + r] * HN_WORD_CHUNKS, HN_WORD_CHUNKS)
                dst = pl.multiple_of(r * HN_WORD_CHUNKS, HN_WORD_CHUNKS)
                pltpu.make_async_copy(hn_hbm.at[pl.ds(src, HN_WORD_CHUNKS)],
                                      xbuf.at[slot_, pl.ds(dst, HN_WORD_CHUNKS)],
                                      sem.at[slot_]).start()
            return c

        lax.fori_loop(0, issue_trips(blk), issue, 0)

    @pl.when(b == 0)
    def _():
        start_weights(0)

        @pl.when(n_exp > 1)
        def _():
            start_weights(1)

        xbuf[...] = jnp.zeros(xbuf.shape, xbuf.dtype)
        gather_rows(0, 0)
        for a in range(1, ROW_AHEAD):
            @pl.when(n_used > a)
            def _(a=a):
                gather_rows(a, a)

    @pl.when(b + ROW_AHEAD < n_used)
    def _():
        gather_rows(b + ROW_AHEAD, (b + ROW_AHEAD) % ROW_SLOTS)

    @pl.when(used)
    def _():
        first = jnp.logical_or(b == 0, be_ref[b] != be_ref[jnp.maximum(b - 1, 0)])

        ordinal = eo_ref[b]
        ws = ordinal % WEIGHT_SLOTS

        @pl.when(first)
        def _():
            for c in weight_copies(ordinal, ws):
                c.wait()

            @pl.when(ordinal + 2 < n_exp)
            def _():
                start_weights(ordinal + 2)

        words = issue_trips(b) * (ROWS_PER_ISSUE * HN_WORD_CHUNKS)
        pltpu.make_async_copy(hn_hbm.at[pl.ds(0, words)], xbuf.at[slot, pl.ds(0, words)],
                              sem.at[slot]).wait()

        halves = ([], [])
        for c in range(HN_WORD_CHUNKS):
            hi, lo = _unpack_bf16_pair(xbuf[slot, pl.ds(c, MOE_BLOCK, stride=HN_WORD_CHUNKS), :])
            halves[0].append(hi.astype(BF16))
            halves[1].append(lo.astype(BF16))
        x = jnp.concatenate(halves[0] + halves[1], axis=1)
        hg = jnp.dot(x, wgf[ws].astype(BF16), preferred_element_type=F32)
        hu = jnp.dot(x, wuf[ws].astype(BF16), preferred_element_type=F32)
        hdn = (jax.nn.silu(hg) * hu).astype(BF16)
        y = jnp.dot(hdn, wdf[ws].astype(BF16), preferred_element_type=F32)
        for c, word in enumerate(_pack_bf16_pairs(y)):
            y_ref[pl.ds(c, MOE_BLOCK, stride=HN_WORD_CHUNKS), :] = word

    @pl.when(jnp.logical_not(used))
    def _():
        y_ref[...] = jnp.zeros(y_ref.shape, y_ref.dtype)


def _experts(hn, w_gate, w_up, w_down, block_e, block_ord, used_experts, block_rows,
             block_offset, row_tok, n_used):
    D = w_gate.shape[1]
    n_blocks = block_e.shape[0]
    n_rows = n_blocks * MOE_BLOCK
    hbm = pl.BlockSpec(memory_space=pl.ANY)
    grid_spec = pltpu.PrefetchScalarGridSpec(
        num_scalar_prefetch=7,
        grid=(n_blocks,),
        in_specs=[hbm, hbm, hbm, hbm],
        out_specs=pl.BlockSpec((MOE_BLOCK * HN_WORD_CHUNKS, LANES), lambda b, *_: (b, 0)),
        scratch_shapes=[pltpu.VMEM((ROW_SLOTS, MOE_BLOCK * HN_WORD_CHUNKS, LANES), jnp.uint32),
                        pltpu.VMEM((WEIGHT_SLOTS, D, EXPERT_FF), F32),
                        pltpu.VMEM((WEIGHT_SLOTS, D, EXPERT_FF), F32),
                        pltpu.VMEM((WEIGHT_SLOTS, EXPERT_FF, D), F32),
                        pltpu.SemaphoreType.DMA((ROW_SLOTS,)),
                        pltpu.SemaphoreType.DMA((WEIGHT_SLOTS,))],
    )
    return pl.pallas_call(
        _expert_kernel,
        grid_spec=grid_spec,
        out_shape=jax.ShapeDtypeStruct((n_rows * HN_WORD_CHUNKS, LANES), jnp.uint32),
        compiler_params=_params(("arbitrary",)),
        name="experts",
    )(block_e, block_ord, used_experts, block_rows, block_offset, row_tok, n_used,
      hn, w_gate, w_up, w_down)


def _combine_kernel(*refs, tc):
    dest_refs = refs[:COMBINE_SLOTS]
    y_hbm, h_ref, wt_ref, o_ref, ybuf, sem = refs[COMBINE_SLOTS:]
    i = pl.program_id(0)
    slot = i % COMBINE_SLOTS

    def gather_rows(dest_ref, slot_):
        def issue(r, c):
            dst = pl.multiple_of(r * HN_WORD_CHUNKS, HN_WORD_CHUNKS)
            for k in range(TOP_K):
                src = pl.multiple_of(dest_ref[0, 0, r * TOP_K + k] * HN_WORD_CHUNKS, HN_WORD_CHUNKS)
                pltpu.make_async_copy(y_hbm.at[pl.ds(src, HN_WORD_CHUNKS)],
                                      ybuf.at[slot_, k, pl.ds(dst, HN_WORD_CHUNKS)],
                                      sem.at[slot_]).start(priority=k % 2)
            return c

        lax.fori_loop(0, tc, issue, 0, unroll=4)

    @pl.when(i == 0)
    def _():
        for a in range(COMBINE_AHEAD):
            gather_rows(dest_refs[a], a)

    @pl.when(i + COMBINE_AHEAD < pl.num_programs(0))
    def _():
        gather_rows(dest_refs[COMBINE_AHEAD], (i + COMBINE_AHEAD) % COMBINE_SLOTS)

    for k in range(TOP_K):
        pltpu.make_async_copy(y_hbm.at[pl.ds(0, tc * HN_WORD_CHUNKS)], ybuf.at[slot, k],
                              sem.at[slot]).wait()
    wt = wt_ref[...]
    w1, w2 = wt[:, 0:1], wt[:, 1:2]
    for c in range(HN_WORD_CHUNKS):
        rows = pl.ds(c, tc, stride=HN_WORD_CHUNKS)
        y1, y2 = (_unpack_bf16_pair(ybuf[slot, k, rows, :]) for k in range(TOP_K))
        for half, col in enumerate((c * LANES, (c + HN_WORD_CHUNKS) * LANES)):
            o_ref[:, col:col + LANES] = h_ref[:, col:col + LANES] + (w1 * y1[half] + w2 * y2[half])


def _combine(yrows, h, wts, dest, tc=256):
    T, D = h.shape
    steps = T // tc
    row = lambda i: (i, 0)
    dest3 = dest.reshape(steps, 1, tc * TOP_K)
    dest_block = lambda im: pl.BlockSpec((1, 1, tc * TOP_K), im, memory_space=pltpu.SMEM)
    return pl.pallas_call(
        functools.partial(_combine_kernel, tc=tc),
        grid=(steps,),
        in_specs=[dest_block(lambda i, a=a: (jnp.minimum(i + a, steps - 1), 0, 0))
                  for a in range(COMBINE_SLOTS)]
                 + [pl.BlockSpec(memory_space=pl.ANY),
                    pl.BlockSpec((tc, D), row),
                    pl.BlockSpec((tc, LANES), row)],
        out_specs=pl.BlockSpec((tc, D), row),
        out_shape=jax.ShapeDtypeStruct((T, D), F32),
        scratch_shapes=[pltpu.VMEM((COMBINE_SLOTS, TOP_K, tc * HN_WORD_CHUNKS, LANES), jnp.uint32),
                        pltpu.SemaphoreType.DMA((COMBINE_SLOTS,))],
        compiler_params=_params(("arbitrary",)),
        name="combine",
    )(*([dest3] * COMBINE_SLOTS), yrows, h, wts)


def _lane_cumsum(x):
    lane = lax.broadcasted_iota(jnp.int32, x.shape, 1)
    s = 1
    while s < LANES:
        x = x + jnp.where(lane >= s, pltpu.roll(x, s, 1), 0)
        s *= 2
    return x


def _dispatch_kernel(idx_ref, dest_ref, meta_ref, run_ref, prefix_ref, start_ref, *, tb):
    p = pl.program_id(0)
    i = pl.program_id(1)
    idx = idx_ref[...]
    lane = lax.broadcasted_iota(jnp.int32, idx.shape, 1)
    e1 = idx[:, 0:1]
    e2 = idx[:, 1:2]
    onehot = jnp.where(jnp.logical_or(lane == e1, lane == e2), 1.0, 0.0)

    @pl.when(jnp.logical_and(p == 0, i == 0))
    def _():
        run_ref[...] = jnp.zeros(run_ref.shape, F32)

    @pl.when(p == 0)
    def _():
        prefix_ref[i] = run_ref[...]
        run_ref[...] = run_ref[...] + jnp.sum(onehot, axis=0, keepdims=True)

    @pl.when(jnp.logical_and(p == 1, i == 0))
    def _():
        counts = run_ref[...].astype(jnp.int32)
        nblk = (counts + (MOE_BLOCK - 1)) // MOE_BLOCK
        end_blk = _lane_cumsum(nblk)
        start_ref[...] = ((end_blk - nblk) * MOE_BLOCK).astype(F32)
        has = jnp.where(counts > 0, 1, 0)
        ordinal = _lane_cumsum(has) - 1
        first_row = (_lane_cumsum(counts) - counts)[0:1]
        start_blk, counts = (end_blk - nblk)[0:1], counts[0:1]
        end_blk, has, ordinal = end_blk[0:1], has[0:1], ordinal[0:1]
        rows = lax.broadcasted_iota(jnp.int32, (META_ROWS, LANES), 0)
        lanes = lax.broadcasted_iota(jnp.int32, (META_ROWS, LANES), 1)
        is_e = lanes < N_EXPERTS
        rsum = lambda v: jnp.sum(v, axis=-1, keepdims=True)
        be = rsum(jnp.where(jnp.logical_and(is_e, end_blk <= rows), 1, 0))
        be = jnp.minimum(be, N_EXPERTS - 1)
        eo = rsum(jnp.where(lanes == be, ordinal, 0))
        ue = rsum(jnp.where(jnp.logical_and(has > 0, ordinal == rows), lanes, 0))
        n_blk = rsum(jnp.where(lanes == N_EXPERTS - 1, end_blk, 0))
        n_exp = rsum(jnp.where(is_e, has, 0))
        own = lanes == be
        done = MOE_BLOCK * (rows - rsum(jnp.where(own, start_blk, 0)))
        n_valid = jnp.clip(rsum(jnp.where(own, counts, 0)) - done, 0, MOE_BLOCK)
        offset = rsum(jnp.where(own, first_row, 0)) + done
        columns = (be, eo, ue, n_blk, n_exp, n_valid, offset)
        meta = jnp.zeros((META_ROWS, LANES), jnp.int32)
        for col, val in enumerate(columns):
            meta = jnp.where(lanes == col, val, meta)
        meta_ref[...] = meta

    @pl.when(p == 1)
    def _():
        r = lax.broadcasted_iota(jnp.int32, (tb, tb), 0)
        c = lax.broadcasted_iota(jnp.int32, (tb, tb), 1)
        earlier = jnp.where(c < r, 1.0, 0.0).astype(BF16)
        rank = jnp.dot(earlier, onehot.astype(BF16), preferred_element_type=F32)
        rank = rank + prefix_ref[i][0:1] + start_ref[0:1]
        d1 = jnp.sum(jnp.where(lane == e1, rank, 0.0), axis=-1, keepdims=True)
        d2 = jnp.sum(jnp.where(lane == e2, rank, 0.0), axis=-1, keepdims=True)
        dest_ref[...] = jnp.where(lane == 0, d1, d2).astype(jnp.int32)


def _dispatch(idx, tb=1024):
    T = idx.shape[0]
    n_rows = T * TOP_K + N_EXPERTS * MOE_BLOCK
    n_blocks = n_rows // MOE_BLOCK
    assert n_blocks <= META_ROWS
    dest2, meta = pl.pallas_call(
        functools.partial(_dispatch_kernel, tb=tb),
        grid=(2, T // tb),
        in_specs=[pl.BlockSpec((tb, LANES), lambda p, i: (i, 0))],
        out_specs=[pl.BlockSpec((tb, LANES), lambda p, i: (i * p, 0)),
                   pl.BlockSpec((META_ROWS, LANES), lambda p, i: (0, 0))],
        out_shape=[jax.ShapeDtypeStruct((T, LANES), jnp.int32),
                   jax.ShapeDtypeStruct((META_ROWS, LANES), jnp.int32)],
        scratch_shapes=[pltpu.VMEM((SUBLANES, LANES), F32),
                        pltpu.VMEM((T // tb, SUBLANES, LANES), F32),
                        pltpu.VMEM((SUBLANES, LANES), F32)],
        compiler_params=_params(("arbitrary", "arbitrary")),
        name="dispatch",
    )(idx)
    dest = dest2[:, :TOP_K].reshape(T * TOP_K)
    row_tok = (jnp.argsort(dest) // TOP_K).astype(jnp.int32)
    row_tok = jnp.concatenate([row_tok, jnp.zeros((ROWS_PER_ISSUE,), jnp.int32)])
    block_e = meta[:n_blocks, 0]
    block_ord = meta[:n_blocks, 1]
    used_experts = meta[:N_EXPERTS, 2]
    n_used = meta[0, 3:5]
    block_rows = meta[:n_blocks, 5]
    block_offset = meta[:n_blocks, 6]
    return block_e, block_ord, used_experts, block_rows, block_offset, row_tok, n_used, dest


def kernel(x, positions, norm1_g, w_in, q_norm_g, k_norm_g, sink_logits, sg_ln_g, sg_ln_b, sg_w, sg_b, w_branch_att, w_branch_sg, w_out, norm2_g, w_group_router, b_group_router, w_expert_router, b_expert_router, w_gate, w_up, w_down):
    B, S, D = x.shape
    T = B * S
    assert D == D_MODEL and S % (2 * ATT_BLOCK) == 0 and S % SG_CHUNK == 0
    assert x.dtype == F32 and positions.shape == (B, S)
    assert w_in.shape[1:] == (D_MODEL, IN_COLS) and w_out.shape[1:] == (D_MODEL, D_MODEL)
    assert w_gate.shape[1:] == (N_EXPERTS, D_MODEL, EXPERT_FF) and w_down.shape[1:] == (
        N_EXPERTS, EXPERT_FF, D_MODEL)
    h = x.reshape(T, D)
    pos = positions.reshape(T)
    for l in range(norm1_g.shape[0]):
        xn = _rmsnorm(h, norm1_g[l])
        proj = _in_proj(xn, w_in[l])
        q, k, v = _qkv_prep(proj, pos, q_norm_g[l], k_norm_g[l])
        att = _attention(q, k, v, sink_logits[l], B)
        sgo = _spatial_gating(proj, sg_ln_g[l], sg_ln_b[l], sg_w[l], sg_b[l])
        merged = _merge(att, sgo, proj, w_branch_att[l], w_branch_sg[l])
        pad = LANES - N_GROUPS - N_EXPERTS
        w_router = jnp.concatenate(
            [w_group_router[l], w_expert_router[l], jnp.zeros((D, pad), F32)], axis=1).astype(BF16)
        b_router = jnp.concatenate(
            [b_group_router[l], b_expert_router[l], jnp.zeros((pad,), F32)]).reshape(1, LANES)
        h, hn, idx, wts = _out_proj(merged, w_out[l], h, norm2_g[l], w_router, b_router)
        (block_e, block_ord, used_experts, block_rows, block_offset, row_tok, n_used,
         dest) = _dispatch(idx)
        yrows = _experts(hn, w_gate[l], w_up[l], w_down[l], block_e, block_ord, used_experts,
                         block_rows, block_offset, row_tok, n_used)
        h = _combine(yrows, h, wts, dest)
    return h.reshape(B, S, D)
```

```python
import functools

import jax
import jax.numpy as jnp
from jax import lax
from jax.experimental import pallas as pl
from jax.experimental.pallas import tpu as pltpu

F32 = jnp.float32
BF16 = jnp.bfloat16

D_MODEL = 2048
HEAD_DIM = 64
ATT_WIDTH = D_MODEL // 2
ATT_HEADS = ATT_WIDTH // HEAD_DIM
ATT_KV_HEADS = ATT_HEADS // 4
Q_PER_KV = ATT_HEADS // ATT_KV_HEADS
KV_WIDTH = ATT_KV_HEADS * HEAD_DIM
WINDOW = 128
ATT_BLOCK = 128
ROPE_DIM = HEAD_DIM // 4
ROPE_HALF = ROPE_DIM // 2
ROPE_THETA = 500000.0
SG_WIDTH = D_MODEL // 2
SG_GROUP_DIM = 128
SG_GROUPS = SG_WIDTH // SG_GROUP_DIM
SG_CHUNK = 128
OFF_Q = 0
OFF_K = OFF_Q + ATT_WIDTH
OFF_V = OFF_K + KV_WIDTH
OFF_U = OFF_V + KV_WIDTH
OFF_S = OFF_U + SG_WIDTH
OFF_G = OFF_S + SG_WIDTH
IN_COLS = OFF_G + 2 * D_MODEL
N_GROUPS = 8
EXPERTS_PER_GROUP = 8
N_EXPERTS = N_GROUPS * EXPERTS_PER_GROUP
TOP_K = 2
EXPERT_FF = D_MODEL // 4
MOE_BLOCK = 128
EPS = 1e-6
NEG_INF = -1e30

LANES = 128
SUBLANES = 8
BF16_BITS = 16
HN_WORD_CHUNKS = D_MODEL // LANES // 2
ROWS_PER_ISSUE = 8
ROW_AHEAD = 4
ROW_SLOTS = ROW_AHEAD + 1
SG_CHUNKS_PER_TRIP = 2
GATE_BLOCK = 512
MERGE_GATE_BLOCKS = 2
MERGE_ROW_PARTS = 2
OUT_ROW_PARTS = 4
WEIGHT_SLOTS = 3
COMBINE_AHEAD = 1
COMBINE_SLOTS = COMBINE_AHEAD + 1
META_ROWS = 256
VMEM_LIMIT = 56 * 1024 * 1024

assert WINDOW == ATT_BLOCK


def _pack_bf16_pairs(x):
    bits = lax.bitcast_convert_type(x.astype(BF16).astype(F32), jnp.uint32)
    words = []
    for c in range(HN_WORD_CHUNKS):
        hi = bits[:, c * LANES:(c + 1) * LANES]
        lo = bits[:, (c + HN_WORD_CHUNKS) * LANES:(c + HN_WORD_CHUNKS + 1) * LANES]
        words.append(hi | (lo >> BF16_BITS))
    return words


def _unpack_bf16_pair(word):
    high_mask = jnp.uint32(((1 << BF16_BITS) - 1) << BF16_BITS)
    return (lax.bitcast_convert_type(word & high_mask, F32),
            lax.bitcast_convert_type(word << BF16_BITS, F32))


def _params(sem, vmem=VMEM_LIMIT):
    return pltpu.CompilerParams(dimension_semantics=sem, vmem_limit_bytes=vmem)


def _rmsnorm_kernel(x_ref, g_ref, o_ref):
    x = x_ref[...]
    r = lax.rsqrt(jnp.mean(x * x, axis=-1, keepdims=True) + EPS)
    o_ref[...] = ((x * r) * g_ref[...]).astype(o_ref.dtype)


def _rmsnorm(x, g, tm=1024):
    T, D = x.shape
    return pl.pallas_call(
        _rmsnorm_kernel,
        grid=(T // tm,),
        in_specs=[pl.BlockSpec((tm, D), lambda i: (i, 0)),
                  pl.BlockSpec((1, D), lambda i: (0, 0))],
        out_specs=pl.BlockSpec((tm, D), lambda i: (i, 0)),
        out_shape=jax.ShapeDtypeStruct((T, D), BF16),
        compiler_params=_params(("parallel",)),
        name="norm1",
    )(x, g.reshape(1, D))


def _proj_kernel(x_ref, w_ref, o_ref, wbf_ref):
    @pl.when(pl.program_id(1) == 0)
    def _():
        wbf_ref[...] = w_ref[...].astype(BF16)

    o_ref[...] = jnp.dot(x_ref[...], wbf_ref[...], preferred_element_type=F32)


def _in_proj(xn, w, tm=1024, tn=1280):
    T, D = xn.shape
    N = w.shape[1]
    return pl.pallas_call(
        _proj_kernel,
        grid=(N // tn, T // tm),
        in_specs=[pl.BlockSpec((tm, D), lambda j, i: (i, 0)),
                  pl.BlockSpec((D, tn), lambda j, i: (0, j))],
        out_specs=pl.BlockSpec((tm, tn), lambda j, i: (i, j)),
        out_shape=jax.ShapeDtypeStruct((T, N), F32),
        scratch_shapes=[pltpu.VMEM((D, tn), BF16)],
        compiler_params=_params(("arbitrary", "arbitrary")),
        name="in_proj",
    )(xn, w)


def _rope_table_kernel(pos_ref, invf_ref, cos_ref, sin_ref):
    ang = pos_ref[...].astype(F32) * invf_ref[...]
    cos_ref[...] = jnp.cos(ang)
    sin_ref[...] = jnp.sin(ang)


def _rope_tables(positions):
    T = positions.shape[0]
    rows = T * ROPE_HALF // LANES
    inv = ROPE_THETA ** (-jnp.arange(0, ROPE_DIM, 2, dtype=F32) / ROPE_DIM)
    invf = jnp.tile(inv, LANES // ROPE_HALF).reshape(1, LANES)
    pos = jnp.repeat(positions, ROPE_HALF).reshape(rows, LANES)
    whole = lambda: (0, 0)
    cos, sin = pl.pallas_call(
        _rope_table_kernel,
        in_specs=[pl.BlockSpec((rows, LANES), whole), pl.BlockSpec((1, LANES), whole)],
        out_specs=[pl.BlockSpec((rows, LANES), whole), pl.BlockSpec((rows, LANES), whole)],
        out_shape=[jax.ShapeDtypeStruct((rows, LANES), F32)] * 2,
        name="rope_tables",
    )(pos, invf)
    return cos.reshape(T, ROPE_HALF), sin.reshape(T, ROPE_HALF)


def _qkv_prep_kernel(p_ref, cos_ref, sin_ref, cpat_ref, spat_ref, gq_ref, gk_ref, seg_ref,
                     q_ref, k_ref, v_ref):
    lane = lax.broadcasted_iota(jnp.int32, (cos_ref.shape[0], LANES), 1)
    in_head = lane % HEAD_DIM

    def spread(t_ref, pattern_ref):
        t = t_ref[...]
        hi = t.astype(BF16)
        r1 = t - hi.astype(F32)
        mid = r1.astype(BF16)
        lo = (r1 - mid.astype(F32)).astype(BF16)
        return sum(jnp.dot(term, pattern_ref[...], preferred_element_type=F32)
                   for term in (hi, mid, lo))

    cos = jnp.where(in_head < ROPE_DIM, spread(cos_ref, cpat_ref), 1.0)
    sin = spread(sin_ref, spat_ref)
    first_half = in_head < ROPE_HALF

    def norm_rope(x, g):
        x2 = x * x
        x2_hi = x2.astype(BF16)
        x2_lo = (x2 - x2_hi.astype(F32)).astype(BF16)
        both = jnp.dot(jnp.concatenate([x2_hi, x2_lo], axis=0), seg_ref[...],
                       preferred_element_type=F32)
        ssq = both[:x.shape[0]] + both[x.shape[0]:]
        xn = (x * lax.rsqrt(ssq * (1.0 / HEAD_DIM) + EPS)) * g
        partner = jnp.where(first_half,
                            pltpu.roll(xn, LANES - ROPE_HALF, 1),
                            pltpu.roll(xn, ROPE_HALF, 1))
        return xn * cos + partner * sin

    for c in range(ATT_WIDTH // LANES):
        x = p_ref[:, OFF_Q + c * LANES:OFF_Q + (c + 1) * LANES]
        q_ref[:, c * LANES:(c + 1) * LANES] = (
            norm_rope(x, gq_ref[...]) * (HEAD_DIM ** -0.5)).astype(q_ref.dtype)
    for c in range(KV_WIDTH // LANES):
        x = p_ref[:, OFF_K + c * LANES:OFF_K + (c + 1) * LANES]
        k_ref[:, c * LANES:(c + 1) * LANES] = norm_rope(x, gk_ref[...]).astype(k_ref.dtype)
    v_ref[...] = p_ref[:, OFF_V:OFF_V + KV_WIDTH].astype(v_ref.dtype)


def _qkv_prep(proj, positions, q_g, k_g, tq=1024):
    T = proj.shape[0]
    width = OFF_U
    cos8, sin8 = _rope_tables(positions)
    in_head = jnp.arange(LANES) % HEAD_DIM
    picks = (in_head[None, :] % ROPE_HALF == jnp.arange(ROPE_HALF)[:, None]) & (in_head < ROPE_DIM)
    cos_pattern = picks.astype(BF16)
    sin_pattern = (picks * jnp.where(in_head < ROPE_HALF, -1.0, 1.0)).astype(BF16)
    head_of_lane = jnp.arange(LANES) // HEAD_DIM
    same_head = (head_of_lane[:, None] == head_of_lane[None, :]).astype(BF16)
    gq = jnp.tile(q_g, LANES // HEAD_DIM).reshape(1, LANES)
    gk = jnp.tile(k_g, LANES // HEAD_DIM).reshape(1, LANES)
    row = lambda i: (i, 0)
    const = lambda i: (0, 0)
    return pl.pallas_call(
        _qkv_prep_kernel,
        grid=(T // tq,),
        in_specs=[pl.BlockSpec((tq, width), row),
                  pl.BlockSpec((tq, ROPE_HALF), row),
                  pl.BlockSpec((tq, ROPE_HALF), row),
                  pl.BlockSpec((ROPE_HALF, LANES), const),
                  pl.BlockSpec((ROPE_HALF, LANES), const),
                  pl.BlockSpec((1, LANES), const),
                  pl.BlockSpec((1, LANES), const),
                  pl.BlockSpec((LANES, LANES), const)],
        out_specs=[pl.BlockSpec((tq, ATT_WIDTH), row),
                   pl.BlockSpec((tq, KV_WIDTH), row),
                   pl.BlockSpec((tq, KV_WIDTH), row)],
        out_shape=[jax.ShapeDtypeStruct((T, ATT_WIDTH), BF16),
                   jax.ShapeDtypeStruct((T, KV_WIDTH), BF16),
                   jax.ShapeDtypeStruct((T, KV_WIDTH), BF16)],
        compiler_params=_params(("parallel",)),
        name="qkv_prep",
    )(proj, cos8, sin8, cos_pattern, sin_pattern, gq, gk, same_head)


def _attn_kernel(sink_ref, q_ref, kp_ref, kc_ref, kn_ref, vp_ref, vc_ref, vn_ref, o_ref, *, nb):
    n = pl.program_id(1)
    rows = ATT_BLOCK
    qi = lax.broadcasted_iota(jnp.int32, (rows, ATT_BLOCK), 0) % ATT_BLOCK
    kj = lax.broadcasted_iota(jnp.int32, (rows, ATT_BLOCK), 1)
    lo_prev = jnp.where(n > 0, 0, ATT_BLOCK)
    hi_next = jnp.where(n < nb - 1, 0, -ATT_BLOCK)
    cap_prev = jnp.where(kj - qi >= lo_prev, jnp.inf, NEG_INF)
    cap_next = jnp.where(kj - qi <= hi_next, jnp.inf, NEG_INF)
    cap = jnp.concatenate([cap_prev, jnp.full((rows, ATT_BLOCK), jnp.inf, F32), cap_next], axis=1)

    kvhs = range(ATT_KV_HEADS)
    scores = []
    for kvh in kvhs:
        cols = slice(kvh * HEAD_DIM, (kvh + 1) * HEAD_DIM)
        k = jnp.concatenate([kp_ref[:, cols], kc_ref[:, cols], kn_ref[:, cols]], axis=0)
        q = jnp.concatenate(
            [q_ref[:, (kvh * Q_PER_KV + g) * HEAD_DIM:(kvh * Q_PER_KV + g + 1) * HEAD_DIM]
             for g in range(Q_PER_KV)], axis=0)
        scores.append(lax.dot_general(q, k, (((1,), (1,)), ((), ())),
                                      preferred_element_type=F32))
    probs = []
    for kvh in kvhs:
        strips = []
        for g in range(Q_PER_KV):
            sink = sink_ref[kvh * Q_PER_KV + g]
            sg = jnp.minimum(scores[kvh][g * ATT_BLOCK:(g + 1) * ATT_BLOCK], cap)
            m = jnp.maximum(jnp.max(sg, axis=-1, keepdims=True), sink)
            e = jnp.exp(sg - m)
            denom = jnp.sum(e, axis=-1, keepdims=True) + jnp.exp(sink - m)
            strips.append((e / denom).astype(BF16))
        probs.append(jnp.concatenate(strips, axis=0))
    for kvh in kvhs:
        cols = slice(kvh * HEAD_DIM, (kvh + 1) * HEAD_DIM)
        v = jnp.concatenate([vp_ref[:, cols], vc_ref[:, cols], vn_ref[:, cols]], axis=0)
        o = jnp.dot(probs[kvh], v, preferred_element_type=F32)
        for g in range(Q_PER_KV):
            h = kvh * Q_PER_KV + g
            o_ref[:, h * HEAD_DIM:(h + 1) * HEAD_DIM] = (
                o[g * ATT_BLOCK:(g + 1) * ATT_BLOCK].astype(o_ref.dtype))


def _attention(q, k, v, sink, batch):
    T = q.shape[0]
    nb = T // batch // ATT_BLOCK
    cur = lambda b, n: (b * nb + n, 0)
    prev = lambda b, n: (b * nb + jnp.maximum(n - 1, 0), 0)
    nxt = lambda b, n: (b * nb + jnp.minimum(n + 1, nb - 1), 0)
    kv = lambda im: pl.BlockSpec((ATT_BLOCK, KV_WIDTH), im)
    return pl.pallas_call(
        functools.partial(_attn_kernel, nb=nb),
        grid=(batch, nb),
        in_specs=[pl.BlockSpec(memory_space=pltpu.SMEM),
                  pl.BlockSpec((ATT_BLOCK, ATT_WIDTH), cur),
                  kv(prev), kv(cur), kv(nxt), kv(prev), kv(cur), kv(nxt)],
        out_specs=pl.BlockSpec((ATT_BLOCK, ATT_WIDTH), cur),
        out_shape=jax.ShapeDtypeStruct((T, ATT_WIDTH), BF16),
        compiler_params=_params(("parallel", "parallel")),
        name="window_attn",
    )(sink, q, k, k, k, v, v, v)


def _sg_kernel(u_ref, s_ref, lng_ref, lnb_ref, w_ref, b_ref, o_ref):
    groups = w_ref.shape[0]

    def some_chunks(ci, carry):
        tiles = []
        for j in range(SG_CHUNKS_PER_TRIP):
            start = pl.multiple_of((ci * SG_CHUNKS_PER_TRIP + j) * SG_CHUNK, SG_CHUNK)
            for gi in range(groups):
                tiles.append((pl.ds(start, SG_CHUNK),
                              slice(gi * SG_GROUP_DIM, (gi + 1) * SG_GROUP_DIM), gi))
        ts = range(len(tiles))
        s = [jax.nn.gelu(s_ref[r, c]) for r, c, _ in tiles]
        sc = [s[t] - jnp.mean(s[t], axis=-1, keepdims=True) for t in ts]
        var = [jnp.mean(sc[t] * sc[t], axis=-1, keepdims=True) for t in ts]
        sn = [(sc[t] * lax.rsqrt(var[t] + EPS)) * lng_ref[0, gi:gi + 1, :]
              + lnb_ref[0, gi:gi + 1, :] for t, (_, _, gi) in enumerate(tiles)]
        mixed = [jnp.dot(w_ref[gi], sn[t].astype(BF16), preferred_element_type=F32)
                 + b_ref[0, :, gi:gi + 1] for t, (_, _, gi) in enumerate(tiles)]
        for t, (r, c, _) in enumerate(tiles):
            o_ref[r, c] = (jax.nn.gelu(u_ref[r, c]) * mixed[t]).astype(o_ref.dtype)
        return carry

    lax.fori_loop(0, u_ref.shape[0] // (SG_CHUNK * SG_CHUNKS_PER_TRIP), some_chunks, 0)


def _spatial_gating(proj, ln_g, ln_b, w_s, b_s, halves=2, chunks=16):
    T = proj.shape[0]
    half_w = SG_WIDTH // halves
    gph = SG_GROUPS // halves
    u0, s0 = OFF_U // half_w, OFF_S // half_w
    tq = chunks * SG_CHUNK
    return pl.pallas_call(
        _sg_kernel,
        grid=(T // tq, halves),
        in_specs=[pl.BlockSpec((tq, half_w), lambda i, j: (i, u0 + j)),
                  pl.BlockSpec((tq, half_w), lambda i, j: (i, s0 + j)),
                  pl.BlockSpec((1, gph, SG_GROUP_DIM), lambda i, j: (j, 0, 0)),
                  pl.BlockSpec((1, gph, SG_GROUP_DIM), lambda i, j: (j, 0, 0)),
                  pl.BlockSpec((gph, SG_CHUNK, SG_CHUNK), lambda i, j: (j, 0, 0)),
                  pl.BlockSpec((1, SG_CHUNK, gph), lambda i, j: (j, 0, 0))],
        out_specs=pl.BlockSpec((tq, half_w), lambda i, j: (i, j)),
        out_shape=jax.ShapeDtypeStruct((T, SG_WIDTH), BF16),
        compiler_params=_params(("parallel", "parallel")),
        name="spatial_gating",
    )(proj, proj,
      ln_g.reshape(halves, gph, SG_GROUP_DIM), ln_b.reshape(halves, gph, SG_GROUP_DIM),
      w_s.astype(BF16),
      b_s.reshape(halves, gph, SG_CHUNK).transpose(0, 2, 1))


def _merge_kernel(att_ref, sgo_ref, *refs):
    gate_refs = refs[:2 * MERGE_GATE_BLOCKS]
    wa_ref, wb_ref, o_ref, wa_bf, wb_bf = refs[2 * MERGE_GATE_BLOCKS:]

    @pl.when(pl.program_id(1) == 0)
    def _():
        wa_bf[...] = wa_ref[...].astype(BF16)
        wb_bf[...] = wb_ref[...].astype(BF16)

    part_rows = att_ref.shape[0] // MERGE_ROW_PARTS
    parts = [slice(p * part_rows, (p + 1) * part_rows) for p in range(MERGE_ROW_PARTS)]
    a = [jnp.dot(att_ref[r, :], wa_bf[...], preferred_element_type=F32) for r in parts]
    b = [jnp.dot(sgo_ref[r, :], wb_bf[...], preferred_element_type=F32) for r in parts]
    for p, r in enumerate(parts):
        for g in range(MERGE_GATE_BLOCKS):
            cols = slice(g * GATE_BLOCK, (g + 1) * GATE_BLOCK)
            ga_ref, gb_ref = gate_refs[g], gate_refs[MERGE_GATE_BLOCKS + g]
            m = (jax.nn.sigmoid(ga_ref[r, :]) * a[p][:, cols]
                 + jax.nn.sigmoid(gb_ref[r, :]) * b[p][:, cols])
            o_ref[r, cols] = m.astype(o_ref.dtype)


def _merge(att, sgo, proj, w_a, w_b, tm=512):
    T = att.shape[0]
    tn = MERGE_GATE_BLOCKS * GATE_BLOCK
    ga0 = OFF_G // GATE_BLOCK
    gb0 = (OFF_G + D_MODEL) // GATE_BLOCK
    gate = lambda first, g: pl.BlockSpec(
        (tm, GATE_BLOCK), lambda j, i: (i, first + j * MERGE_GATE_BLOCKS + g))
    gates = ([gate(ga0, g) for g in range(MERGE_GATE_BLOCKS)]
             + [gate(gb0, g) for g in range(MERGE_GATE_BLOCKS)])
    return pl.pallas_call(
        _merge_kernel,
        grid=(D_MODEL // tn, T // tm),
        in_specs=[pl.BlockSpec((tm, ATT_WIDTH), lambda j, i: (i, 0)),
                  pl.BlockSpec((tm, SG_WIDTH), lambda j, i: (i, 0))]
                 + gates
                 + [pl.BlockSpec((ATT_WIDTH, tn), lambda j, i: (0, j)),
                    pl.BlockSpec((SG_WIDTH, tn), lambda j, i: (0, j))],
        out_specs=pl.BlockSpec((tm, tn), lambda j, i: (i, j)),
        out_shape=jax.ShapeDtypeStruct((T, D_MODEL), BF16),
        scratch_shapes=[pltpu.VMEM((ATT_WIDTH, tn), BF16),
                        pltpu.VMEM((SG_WIDTH, tn), BF16)],
        compiler_params=_params(("arbitrary", "arbitrary")),
        name="merge",
    )(att, sgo, *([proj] * (2 * MERGE_GATE_BLOCKS)), w_a, w_b)


def _route(logits):
    lane = lax.broadcasted_iota(jnp.int32, logits.shape, 1)
    lane_f = lane.astype(F32)
    is_g = lane < N_GROUPS
    gl = jnp.where(is_g, logits, -jnp.inf)
    gmax = jnp.max(gl, axis=-1, keepdims=True)
    grp = jnp.min(jnp.where(gl == gmax, lane_f, float(LANES)), axis=-1, keepdims=True)
    gsum = jnp.sum(jnp.where(is_g, jnp.exp(logits - gmax), 0.0), axis=-1, keepdims=True)
    g_w = 1.0 / gsum
    e_lane = lane - N_GROUPS
    in_grp = jnp.logical_and(
        jnp.logical_and(e_lane >= 0, e_lane < N_EXPERTS),
        (e_lane // EXPERTS_PER_GROUP).astype(F32) == grp)
    el = jnp.where(in_grp, logits, -jnp.inf)
    v1 = jnp.max(el, axis=-1, keepdims=True)
    i1 = jnp.min(jnp.where(jnp.logical_and(in_grp, el == v1), lane_f, float(LANES)),
                 axis=-1, keepdims=True)
    rest = jnp.logical_and(in_grp, lane_f != i1)
    el2 = jnp.where(rest, logits, -jnp.inf)
    v2 = jnp.max(el2, axis=-1, keepdims=True)
    i2 = jnp.min(jnp.where(jnp.logical_and(rest, el2 == v2), lane_f, float(LANES)),
                 axis=-1, keepdims=True)
    e21 = jnp.exp(v2 - v1)
    w1 = g_w / (1.0 + e21)
    w2 = g_w * e21 / (1.0 + e21)
    idx = jnp.where(lane == 0, i1, i2) - float(N_GROUPS)
    wts = jnp.where(lane == 0, w1, jnp.where(lane == 1, w2, 0.0))
    return idx.astype(jnp.int32), wts


def _out_kernel(m_ref, w_ref, x_ref, g_ref, wr_ref, br_ref, h_ref, hn_ref, idx_ref, wt_ref):
    part_rows = m_ref.shape[0] // OUT_ROW_PARTS
    parts = [slice(p * part_rows, (p + 1) * part_rows) for p in range(OUT_ROW_PARTS)]
    hs = [x_ref[r, :] + jnp.dot(m_ref[r, :], w_ref[...], preferred_element_type=F32)
          for r in parts]
    for p, r in enumerate(parts):
        h = hs[p]
        h_ref[r, :] = h
        rs = lax.rsqrt(jnp.mean(h * h, axis=-1, keepdims=True) + EPS)
        hn = ((h * rs) * g_ref[...]).astype(BF16)
        for c, word in enumerate(_pack_bf16_pairs(hn)):
            hn_ref[pl.ds(p * part_rows * HN_WORD_CHUNKS + c, part_rows,
                         stride=HN_WORD_CHUNKS), :] = word
        logits = jnp.dot(hn, wr_ref[...], preferred_element_type=F32) + br_ref[...]
        idx, wts = _route(logits)
        idx_ref[r, :] = idx
        wt_ref[r, :] = wts


def _out_proj(merged, w_out, x, g2, w_router, b_router, tm=512):
    T, D = x.shape
    row = lambda i: (i, 0)
    const = lambda i: (0, 0)
    return pl.pallas_call(
        _out_kernel,
        grid=(T // tm,),
        in_specs=[pl.BlockSpec((tm, D), row),
                  pl.BlockSpec((D, D), const),
                  pl.BlockSpec((tm, D), row),
                  pl.BlockSpec((1, D), const),
                  pl.BlockSpec((D, LANES), const),
                  pl.BlockSpec((1, LANES), const)],
        out_specs=[pl.BlockSpec((tm, D), row),
                   pl.BlockSpec((tm * HN_WORD_CHUNKS, LANES), row),
                   pl.BlockSpec((tm, LANES), row),
                   pl.BlockSpec((tm, LANES), row)],
        out_shape=[jax.ShapeDtypeStruct((T, D), F32),
                   jax.ShapeDtypeStruct((T * HN_WORD_CHUNKS, LANES), jnp.uint32),
                   jax.ShapeDtypeStruct((T, LANES), jnp.int32),
                   jax.ShapeDtypeStruct((T, LANES), F32)],
        compiler_params=_params(("parallel",)),
        name="out_proj_router",
    )(merged, w_out.astype(BF16), x, g2.reshape(1, D), w_router, b_router)


def _expert_kernel(be_ref, eo_ref, ue_ref, nr_ref, bo_ref, rt_ref, nu_ref,
                   hn_hbm, wg_hbm, wu_hbm, wd_hbm, y_ref, xbuf, wgf, wuf, wdf, sem, wsem):
    b = pl.program_id(0)
    n_used = nu_ref[0]
    n_exp = nu_ref[1]
    used = b < n_used
    slot = b % ROW_SLOTS

    def weight_copies(ordinal, slot_):
        e = ue_ref[ordinal]
        return (pltpu.make_async_copy(wg_hbm.at[e], wgf.at[slot_], wsem.at[slot_]),
                pltpu.make_async_copy(wu_hbm.at[e], wuf.at[slot_], wsem.at[slot_]),
                pltpu.make_async_copy(wd_hbm.at[e], wdf.at[slot_], wsem.at[slot_]))

    def start_weights(ordinal):
        for n, c in enumerate(weight_copies(ordinal, ordinal % WEIGHT_SLOTS)):
            c.start(priority=n % 2)

    def issue_trips(blk):
        return (nr_ref[blk] + (ROWS_PER_ISSUE - 1)) // ROWS_PER_ISSUE

    def gather_rows(blk, slot_):
        first_row = bo_ref[blk]

        def issue(g, c):
            for j in range(ROWS_PER_ISSUE):
                r = g * ROWS_PER_ISSUE + j
                src = pl.multiple_of(rt_ref[first_row + r] * HN_WORD_CHUNKS, HN_WORD_CHUNKS)
                dst = pl.multiple_of(r * HN_WORD_CHUNKS, HN_WORD_CHUNKS)
                pltpu.make_async_copy(hn_hbm.at[pl.ds(src, HN_WORD_CHUNKS)],
                                      xbuf.at[slot_, pl.ds(dst, HN_WORD_CHUNKS)],
                                      sem.at[slot_]).start()
            return c

        lax.fori_loop(0, issue_trips(blk), issue, 0)

    @pl.when(b == 0)
    def _():
        start_weights(0)

        @pl.when(n_exp > 1)
        def _():
            start_weights(1)

        xbuf[...] = jnp.zeros(xbuf.shape, xbuf.dtype)
        gather_rows(0, 0)
        for a in range(1, ROW_AHEAD):
            @pl.when(n_used > a)
            def _(a=a):
                gather_rows(a, a)

    @pl.when(b + ROW_AHEAD < n_used)
    def _():
        gather_rows(b + ROW_AHEAD, (b + ROW_AHEAD) % ROW_SLOTS)

    @pl.when(used)
    def _():
        first = jnp.logical_or(b == 0, be_ref[b] != be_ref[jnp.maximum(b - 1, 0)])

        ordinal = eo_ref[b]
        ws = ordinal % WEIGHT_SLOTS

        @pl.when(first)
        def _():
            for c in weight_copies(ordinal, ws):
                c.wait()

            @pl.when(ordinal + 2 < n_exp)
            def _():
                start_weights(ordinal + 2)

        words = issue_trips(b) * (ROWS_PER_ISSUE * HN_WORD_CHUNKS)
        pltpu.make_async_copy(hn_hbm.at[pl.ds(0, words)], xbuf.at[slot, pl.ds(0, words)],
                              sem.at[slot]).wait()

        halves = ([], [])
        for c in range(HN_WORD_CHUNKS):
            hi, lo = _unpack_bf16_pair(xbuf[slot, pl.ds(c, MOE_BLOCK, stride=HN_WORD_CHUNKS), :])
            halves[0].append(hi.astype(BF16))
            halves[1].append(lo.astype(BF16))
        x = jnp.concatenate(halves[0] + halves[1], axis=1)
        hg = jnp.dot(x, wgf[ws].astype(BF16), preferred_element_type=F32)
        hu = jnp.dot(x, wuf[ws].astype(BF16), preferred_element_type=F32)
        hdn = (jax.nn.silu(hg) * hu).astype(BF16)
        y = jnp.dot(hdn, wdf[ws].astype(BF16), preferred_element_type=F32)
        for c, word in enumerate(_pack_bf16_pairs(y)):
            y_ref[pl.ds(c, MOE_BLOCK, stride=HN_WORD_CHUNKS), :] = word

    @pl.when(jnp.logical_not(used))
    def _():
        y_ref[...] = jnp.zeros(y_ref.shape, y_ref.dtype)


def _experts(hn, w_gate, w_up, w_down, block_e, block_ord, used_experts, block_rows,
             block_offset, row_tok, n_used):
    D = w_gate.shape[1]
    n_blocks = block_e.shape[0]
    n_rows = n_blocks * MOE_BLOCK
    hbm = pl.BlockSpec(memory_space=pl.ANY)
    grid_spec = pltpu.PrefetchScalarGridSpec(
        num_scalar_prefetch=7,
        grid=(n_blocks,),
        in_specs=[hbm, hbm, hbm, hbm],
        out_specs=pl.BlockSpec((MOE_BLOCK * HN_WORD_CHUNKS, LANES), lambda b, *_: (b, 0)),
        scratch_shapes=[pltpu.VMEM((ROW_SLOTS, MOE_BLOCK * HN_WORD_CHUNKS, LANES), jnp.uint32),
                        pltpu.VMEM((WEIGHT_SLOTS, D, EXPERT_FF), F32),
                        pltpu.VMEM((WEIGHT_SLOTS, D, EXPERT_FF), F32),
                        pltpu.VMEM((WEIGHT_SLOTS, EXPERT_FF, D), F32),
                        pltpu.SemaphoreType.DMA((ROW_SLOTS,)),
                        pltpu.SemaphoreType.DMA((WEIGHT_SLOTS,))],
    )
    return pl.pallas_call(
        _expert_kernel,
        grid_spec=grid_spec,
        out_shape=jax.ShapeDtypeStruct((n_rows * HN_WORD_CHUNKS, LANES), jnp.uint32),
        compiler_params=_params(("arbitrary",)),
        name="experts",
    )(block_e, block_ord, used_experts, block_rows, block_offset, row_tok, n_used,
      hn, w_gate, w_up, w_down)


def _combine_kernel(*refs, tc):
    dest_refs = refs[:COMBINE_SLOTS]
    y_hbm, h_ref, wt_ref, o_ref, ybuf, sem = refs[COMBINE_SLOTS:]
    i = pl.program_id(0)
    slot = i % COMBINE_SLOTS

    def gather_rows(dest_ref, slot_):
        def issue(r, c):
            dst = pl.multiple_of(r * HN_WORD_CHUNKS, HN_WORD_CHUNKS)
            for k in range(TOP_K):
                src = pl.multiple_of(dest_ref[0, 0, r * TOP_K + k] * HN_WORD_CHUNKS, HN_WORD_CHUNKS)
                pltpu.make_async_copy(y_hbm.at[pl.ds(src, HN_WORD_CHUNKS)],
                                      ybuf.at[slot_, k, pl.ds(dst, HN_WORD_CHUNKS)],
                                      sem.at[slot_]).start(priority=k % 2)
            return c

        lax.fori_loop(0, tc, issue, 0, unroll=4)

    @pl.when(i == 0)
    def _():
        for a in range(COMBINE_AHEAD):
            gather_rows(dest_refs[a], a)

    @pl.when(i + COMBINE_AHEAD < pl.num_programs(0))
    def _():
        gather_rows(dest_refs[COMBINE_AHEAD], (i + COMBINE_AHEAD) % COMBINE_SLOTS)

    for k in range(TOP_K):
        pltpu.make_async_copy(y_hbm.at[pl.ds(0, tc * HN_WORD_CHUNKS)], ybuf.at[slot, k],
                              sem.at[slot]).wait()
    wt = wt_ref[...]
    w1, w2 = wt[:, 0:1], wt[:, 1:2]
    for c in range(HN_WORD_CHUNKS):
        rows = pl.ds(c, tc, stride=HN_WORD_CHUNKS)
        y1, y2 = (_unpack_bf16_pair(ybuf[slot, k, rows, :]) for k in range(TOP_K))
        for half, col in enumerate((c * LANES, (c + HN_WORD_CHUNKS) * LANES)):
            o_ref[:, col:col + LANES] = h_ref[:, col:col + LANES] + (w1 * y1[half] + w2 * y2[half])


def _combine(yrows, h, wts, dest, tc=256):
    T, D = h.shape
    steps = T // tc
    row = lambda i: (i, 0)
    dest3 = dest.reshape(steps, 1, tc * TOP_K)
    dest_block = lambda im: pl.BlockSpec((1, 1, tc * TOP_K), im, memory_space=pltpu.SMEM)
    return pl.pallas_call(
        functools.partial(_combine_kernel, tc=tc),
        grid=(steps,),
        in_specs=[dest_block(lambda i, a=a: (jnp.minimum(i + a, steps - 1), 0, 0))
                  for a in range(COMBINE_SLOTS)]
                 + [pl.BlockSpec(memory_space=pl.ANY),
                    pl.BlockSpec((tc, D), row),
                    pl.BlockSpec((tc, LANES), row)],
        out_specs=pl.BlockSpec((tc, D), row),
        out_shape=jax.ShapeDtypeStruct((T, D), F32),
        scratch_shapes=[pltpu.VMEM((COMBINE_SLOTS, TOP_K, tc * HN_WORD_CHUNKS, LANES), jnp.uint32),
                        pltpu.SemaphoreType.DMA((COMBINE_SLOTS,))],
        compiler_params=_params(("arbitrary",)),
        name="combine",
    )(*([dest3] * COMBINE_SLOTS), yrows, h, wts)


def _lane_cumsum(x):
    lane = lax.broadcasted_iota(jnp.int32, x.shape, 1)
    s = 1
    while s < LANES:
        x = x + jnp.where(lane >= s, pltpu.roll(x, s, 1), 0)
        s *= 2
    return x


def _dispatch_kernel(idx_ref, dest_ref, meta_ref, run_ref, prefix_ref, start_ref, *, tb):
    p = pl.program_id(0)
    i = pl.program_id(1)
    idx = idx_ref[...]
    lane = lax.broadcasted_iota(jnp.int32, idx.shape, 1)
    e1 = idx[:, 0:1]
    e2 = idx[:, 1:2]
    onehot = jnp.where(jnp.logical_or(lane == e1, lane == e2), 1.0, 0.0)

    @pl.when(jnp.logical_and(p == 0, i == 0))
    def _():
        run_ref[...] = jnp.zeros(run_ref.shape, F32)

    @pl.when(p == 0)
    def _():
        prefix_ref[i] = run_ref[...]
        run_ref[...] = run_ref[...] + jnp.sum(onehot, axis=0, keepdims=True)

    @pl.when(jnp.logical_and(p == 1, i == 0))
    def _():
        counts = run_ref[...].astype(jnp.int32)
        nblk = (counts + (MOE_BLOCK - 1)) // MOE_BLOCK
        end_blk = _lane_cumsum(nblk)
        start_ref[...] = ((end_blk - nblk) * MOE_BLOCK).astype(F32)
        has = jnp.where(counts > 0, 1, 0)
        ordinal = _lane_cumsum(has) - 1
        first_row = (_lane_cumsum(counts) - counts)[0:1]
        start_blk, counts = (end_blk - nblk)[0:1], counts[0:1]
        end_blk, has, ordinal = end_blk[0:1], has[0:1], ordinal[0:1]
        rows = lax.broadcasted_iota(jnp.int32, (META_ROWS, LANES), 0)
        lanes = lax.broadcasted_iota(jnp.int32, (META_ROWS, LANES), 1)
        is_e = lanes < N_EXPERTS
        rsum = lambda v: jnp.sum(v, axis=-1, keepdims=True)
        be = rsum(jnp.where(jnp.logical_and(is_e, end_blk <= rows), 1, 0))
        be = jnp.minimum(be, N_EXPERTS - 1)
        eo = rsum(jnp.where(lanes == be, ordinal, 0))
        ue = rsum(jnp.where(jnp.logical_and(has > 0, ordinal == rows), lanes, 0))
        n_blk = rsum(jnp.where(lanes == N_EXPERTS - 1, end_blk, 0))
        n_exp = rsum(jnp.where(is_e, has, 0))
        own = lanes == be
        done = MOE_BLOCK * (rows - rsum(jnp.where(own, start_blk, 0)))
        n_valid = jnp.clip(rsum(jnp.where(own, counts, 0)) - done, 0, MOE_BLOCK)
        offset = rsum(jnp.where(own, first_row, 0)) + done
        columns = (be, eo, ue, n_blk, n_exp, n_valid, offset)
        meta = jnp.zeros((META_ROWS, LANES), jnp.int32)
        for col, val in enumerate(columns):
            meta = jnp.where(lanes == col, val, meta)
        meta_ref[...] = meta

    @pl.when(p == 1)
    def _():
        r = lax.broadcasted_iota(jnp.int32, (tb, tb), 0)
        c = lax.broadcasted_iota(jnp.int32, (tb, tb), 1)
        earlier = jnp.where(c < r, 1.0, 0.0).astype(BF16)
        rank = jnp.dot(earlier, onehot.astype(BF16), preferred_element_type=F32)
        rank = rank + prefix_ref[i][0:1] + start_ref[0:1]
        d1 = jnp.sum(jnp.where(lane == e1, rank, 0.0), axis=-1, keepdims=True)
        d2 = jnp.sum(jnp.where(lane == e2, rank, 0.0), axis=-1, keepdims=True)
        dest_ref[...] = jnp.where(lane == 0, d1, d2).astype(jnp.int32)


def _dispatch(idx, tb=1024):
    T = idx.shape[0]
    n_rows = T * TOP_K + N_EXPERTS * MOE_BLOCK
    n_blocks = n_rows // MOE_BLOCK
    assert n_blocks <= META_ROWS
    dest2, meta = pl.pallas_call(
        functools.partial(_dispatch_kernel, tb=tb),
        grid=(2, T // tb),
        in_specs=[pl.BlockSpec((tb, LANES), lambda p, i: (i, 0))],
        out_specs=[pl.BlockSpec((tb, LANES), lambda p, i: (i * p, 0)),
                   pl.BlockSpec((META_ROWS, LANES), lambda p, i: (0, 0))],
        out_shape=[jax.ShapeDtypeStruct((T, LANES), jnp.int32),
                   jax.ShapeDtypeStruct((META_ROWS, LANES), jnp.int32)],
        scratch_shapes=[pltpu.VMEM((SUBLANES, LANES), F32),
                        pltpu.VMEM((T // tb, SUBLANES, LANES), F32),
                        pltpu.VMEM((SUBLANES, LANES), F32)],
        compiler_params=_params(("arbitrary", "arbitrary")),
        name="dispatch",
    )(idx)
    dest = dest2[:, :TOP_K].reshape(T * TOP_K)
    row_tok = (jnp.argsort(dest) // TOP_K).astype(jnp.int32)
    row_tok = jnp.concatenate([row_tok, jnp.zeros((ROWS_PER_ISSUE,), jnp.int32)])
    block_e = meta[:n_blocks, 0]
    block_ord = meta[:n_blocks, 1]
    used_experts = meta[:N_EXPERTS, 2]
    n_used = meta[0, 3:5]
    block_rows = meta[:n_blocks, 5]
    block_offset = meta[:n_blocks, 6]
    return block_e, block_ord, used_experts, block_rows, block_offset, row_tok, n_used, dest


def kernel(x, positions, norm1_g, w_in, q_norm_g, k_norm_g, sink_logits, sg_ln_g, sg_ln_b, sg_w, sg_b, w_branch_att, w_branch_sg, w_out, norm2_g, w_group_router, b_group_router, w_expert_router, b_expert_router, w_gate, w_up, w_down):
    B, S, D = x.shape
    T = B * S
    assert D == D_MODEL and S % (2 * ATT_BLOCK) == 0 and S % SG_CHUNK == 0
    assert x.dtype == F32 and positions.shape == (B, S)
    assert w_in.shape[1:] == (D_MODEL, IN_COLS) and w_out.shape[1:] == (D_MODEL, D_MODEL)
    assert w_gate.shape[1:] == (N_EXPERTS, D_MODEL, EXPERT_FF) and w_down.shape[1:] == (
        N_EXPERTS, EXPERT_FF, D_MODEL)
    h = x.reshape(T, D)
    pos = positions.reshape(T)
    for l in range(norm1_g.shape[0]):
        xn = _rmsnorm(h, norm1_g[l])
        proj = _in_proj(xn, w_in[l])
        q, k, v = _qkv_prep(proj, pos, q_norm_g[l], k_norm_g[l])
        att = _attention(q, k, v, sink_logits[l], B)
        sgo = _spatial_gating(proj, sg_ln_g[l], sg_ln_b[l], sg_w[l], sg_b[l])
        merged = _merge(att, sgo, proj, w_branch_att[l], w_branch_sg[l])
        pad = LANES - N_GROUPS - N_EXPERTS
        w_router = jnp.concatenate(
            [w_group_router[l], w_expert_router[l], jnp.zeros((D, pad), F32)], axis=1).astype(BF16)
        b_router = jnp.concatenate(
            [b_group_router[l], b_expert_router[l], jnp.zeros((pad,), F32)]).reshape(1, LANES)
        h, hn, idx, wts = _out_proj(merged, w_out[l], h, norm2_g[l], w_router, b_router)
        (block_e, block_ord, used_experts, block_rows, block_offset, row_tok, n_used,
         dest) = _dispatch(idx)
        yrows = _experts(hn, w_gate[l], w_up[l], w_down[l], block_e, block_ord, used_experts,
                         block_rows, block_offset, row_tok, n_used)
        h = _combine(yrows, h, wts, dest)
    return h.reshape(B, S, D)
```
